```python
import math
import jax, jax.numpy as jnp
from jax import lax
import numpy as np

D_MODEL = 1024
BATCH = 8
SEQ = 8192
DEPTH = 1

HEAD_DIM = 64
DIL_PATTERNS = ((128, 1), (512, 4), (2048, 16))
N_DIL_GROUPS = len(DIL_PATTERNS)
HEADS_PER_GROUP = 4
N_ATTN_HEADS = N_DIL_GROUPS * HEADS_PER_GROUP
ATTN_WIDTH = N_ATTN_HEADS * HEAD_DIM
ATTN_OUT_WIDTH = HEADS_PER_GROUP * HEAD_DIM
QBLK = 128
ROPE_THETA = 10000.0

CHUNK = 128
GMLP_GROUPS = 4
GMLP_GROUP_CH = 128
GMLP_WIDTH = GMLP_GROUPS * GMLP_GROUP_CH

D_FF = 4 * D_MODEL
EPS = 1e-6

Q0 = 0
K0 = Q0 + ATTN_WIDTH
V0 = K0 + ATTN_WIDTH
U0 = V0 + ATTN_WIDTH
Z0 = U0 + GMLP_WIDTH
GA0 = Z0 + GMLP_WIDTH
GB0 = GA0 + D_MODEL
IN_WIDTH = GB0 + D_MODEL

kernel_name = "hybrid_dilated_attn_gmlp_block"


def _rmsnorm(x, gain):
    xf = x.astype(jnp.float32)
    y = xf * lax.rsqrt(jnp.mean(xf * xf, axis=-1, keepdims=True) + EPS)
    return (y * gain.astype(jnp.float32)).astype(x.dtype)


def _layernorm(x, gain, bias):
    xf = x.astype(jnp.float32)
    mu = jnp.mean(xf, axis=-1, keepdims=True)
    var = jnp.mean(jnp.square(xf - mu), axis=-1, keepdims=True)
    y = (xf - mu) * lax.rsqrt(var + EPS)
    return (y * gain.astype(jnp.float32) + bias.astype(jnp.float32)).astype(x.dtype)


def _rope(x):
    S, Dh = x.shape[1], x.shape[-1]
    half = Dh // 2
    inv_freq = ROPE_THETA ** (-jnp.arange(half, dtype=jnp.float32) / half)
    ang = jnp.arange(S, dtype=jnp.float32)[:, None] * inv_freq[None, :]
    cos = jnp.cos(ang)[None, :, None, :]
    sin = jnp.sin(ang)[None, :, None, :]
    xf = x.astype(jnp.float32)
    x1, x2 = xf[..., :half], xf[..., half:]
    return jnp.concatenate([x1 * cos - x2 * sin, x2 * cos + x1 * sin], axis=-1).astype(x.dtype)


def _dilated_window_attention(q, k, v, dilation, n_back):
    B, S, H, Dh = q.shape
    L = S // dilation
    nb = -(-L // QBLK)
    Lp = nb * QBLK

    def to_sub(t):
        t = t.reshape(B, L, dilation, H, Dh).transpose(0, 2, 3, 1, 4)
        t = jnp.pad(t, ((0, 0), (0, 0), (0, 0), (0, Lp - L), (0, 0)))
        return t.reshape(B, dilation, H, nb, QBLK, Dh)

    def with_prev(t):
        prev = jnp.pad(t[:, :, :, :-1], ((0, 0), (0, 0), (0, 0), (1, 0), (0, 0), (0, 0)))
        return jnp.concatenate([prev, t], axis=4)

    qb, kb, vb = to_sub(q), to_sub(k), to_sub(v)
    kc, vc = with_prev(kb), with_prev(vb)
    s = jnp.einsum('brhnqe,brhnke->brhnqk', qb.astype(jnp.float32),
                   kc.astype(jnp.float32)) * (Dh ** -0.5)
    qi = jnp.arange(QBLK)[:, None]
    kj = jnp.arange(2 * QBLK)[None, :]
    dist = qi + QBLK - kj
    band = (dist >= 0) & (dist <= n_back)
    key_sub = jnp.arange(nb)[:, None, None] * QBLK + kj[None] - QBLK
    mask = band[None] & (key_sub >= 0)
    s = jnp.where(mask, s, jnp.float32(-1e30))
    m = jnp.max(s, axis=-1, keepdims=True)
    p = jnp.exp(s - m)
    den = jnp.sum(p, axis=-1, keepdims=True)
    o = jnp.einsum('brhnqk,brhnke->brhnqe', p, vc.astype(jnp.float32)) / den
    lse = (m + jnp.log(den))[..., 0]
    o = o.reshape(B, dilation, H, Lp, Dh)[:, :, :, :L].transpose(0, 3, 1, 2, 4).reshape(B, S, H, Dh)
    lse = lse.reshape(B, dilation, H, Lp)[:, :, :, :L].transpose(0, 3, 1, 2).reshape(B, S, H)
    return o, lse


def _mixer_dilated_attention(q, k, v):
    B, S = q.shape[0], q.shape[1]
    outs, lses = [], []
    for g, (window, dilation) in enumerate(DIL_PATTERNS):
        sl = slice(g * HEADS_PER_GROUP, (g + 1) * HEADS_PER_GROUP)
        o, lse = _dilated_window_attention(q[:, :, sl], k[:, :, sl], v[:, :, sl],
                                           dilation, window // dilation)
        outs.append(o)
        lses.append(lse)
    alpha = jax.nn.softmax(jnp.stack(lses, axis=0), axis=0)
    o = jnp.sum(alpha[..., None] * jnp.stack(outs, axis=0), axis=0)
    return o.reshape(B, S, ATTN_OUT_WIDTH).astype(q.dtype)


def _mixer_chunked_gmlp(u, z, ln_gain, ln_bias, w_spatial, b_spatial):
    B, S, _ = u.shape
    nc = S // CHUNK
    z = _layernorm(z, ln_gain, ln_bias)
    zc = z.reshape(B, nc, CHUNK, GMLP_GROUPS, GMLP_GROUP_CH)
    tril = jnp.tril(jnp.ones((CHUNK, CHUNK), dtype=bool))
    w = jnp.where(tril[None], w_spatial, jnp.zeros_like(w_spatial))
    sz = jnp.einsum('gij,bcjgd->bcigd', w.astype(jnp.float32), zc.astype(jnp.float32))
    sz = sz + b_spatial.T.astype(jnp.float32)[None, None, :, :, None]
    out = u.reshape(B, nc, CHUNK, GMLP_GROUPS, GMLP_GROUP_CH).astype(jnp.float32) * sz
    return out.reshape(B, S, GMLP_WIDTH).astype(u.dtype)


def _fwd_setup_inputs(seed: int = 0) -> dict:
    key = jax.random.key(seed)
    ks = jax.random.split(key, 16)
    f32 = jnp.float32

    def dense(k, fan_in, fan_out):
        return jax.random.normal(k, (DEPTH, fan_in, fan_out), f32) * (fan_in ** -0.5)

    def gain(k, n):
        return 1.0 + 0.02 * jax.random.normal(k, (DEPTH, n), f32)

    return {
        "x": jax.random.normal(ks[0], (BATCH, SEQ, D_MODEL), f32),
        "norm_pre_mix": gain(ks[1], D_MODEL),
        "w_in": dense(ks[2], D_MODEL, IN_WIDTH),
        "w_spatial": jax.random.normal(ks[3], (DEPTH, GMLP_GROUPS, CHUNK, CHUNK), f32) * (CHUNK ** -0.5),
        "b_spatial": 1.0 + 0.1 * jax.random.normal(ks[4], (DEPTH, GMLP_GROUPS, CHUNK), f32),
        "ln_v_gain": gain(ks[5], GMLP_WIDTH),
        "ln_v_bias": 0.02 * jax.random.normal(ks[6], (DEPTH, GMLP_WIDTH), f32),
        "w_branch_attn": dense(ks[7], ATTN_OUT_WIDTH, D_MODEL),
        "w_branch_gmlp": dense(ks[8], GMLP_WIDTH, D_MODEL),
        "w_out": dense(ks[9], D_MODEL, D_MODEL),
        "norm_post_mix": gain(ks[10], D_MODEL),
        "norm_pre_mlp": gain(ks[11], D_MODEL),
        "w_mlp_in": dense(ks[12], D_MODEL, D_FF),
        "w_mlp_out": dense(ks[13], D_FF, D_MODEL),
        "norm_post_mlp": gain(ks[14], D_MODEL),
    }


def _fwd_reference(x, norm_pre_mix, w_in, w_spatial, b_spatial, ln_v_gain, ln_v_bias,
              w_branch_attn, w_branch_gmlp, w_out, norm_post_mix, norm_pre_mlp,
              w_mlp_in, w_mlp_out, norm_post_mlp):
    B, S, D = x.shape
    for layer in range(DEPTH):
        h = _rmsnorm(x, norm_pre_mix[layer])
        proj = jnp.einsum('bsd,de->bse', h, w_in[layer])
        q = _rope(proj[..., Q0:K0].reshape(B, S, N_ATTN_HEADS, HEAD_DIM))
        k = _rope(proj[..., K0:V0].reshape(B, S, N_ATTN_HEADS, HEAD_DIM))
        v = proj[..., V0:U0].reshape(B, S, N_ATTN_HEADS, HEAD_DIM)
        u = jax.nn.gelu(proj[..., U0:Z0])
        z = jax.nn.gelu(proj[..., Z0:GA0])
        gate_a = jax.nn.sigmoid(proj[..., GA0:GB0])
        gate_b = jax.nn.sigmoid(proj[..., GB0:IN_WIDTH])

        y_attn = _mixer_dilated_attention(q, k, v)
        y_gmlp = _mixer_chunked_gmlp(u, z, ln_v_gain[layer], ln_v_bias[layer],
                                     w_spatial[layer], b_spatial[layer])
        merged = (gate_a * jnp.einsum('bse,ed->bsd', y_attn, w_branch_attn[layer])
                  + gate_b * jnp.einsum('bse,ed->bsd', y_gmlp, w_branch_gmlp[layer]))
        y = jnp.einsum('bsd,de->bse', merged, w_out[layer])
        x = x + _rmsnorm(y, norm_post_mix[layer])

        h = _rmsnorm(x, norm_pre_mlp[layer])
        a = jax.nn.relu(jnp.einsum('bsd,df->bsf', h, w_mlp_in[layer]))
        y = jnp.einsum('bsf,fd->bsd', a * a, w_mlp_out[layer])
        x = x + _rmsnorm(y, norm_post_mlp[layer])
    return x


import jax as _jax
import jax.numpy as _jnp

TWIN_FORMAT = 'train_step'
FWD_PARAMS = ['x', 'norm_pre_mix', 'w_in', 'w_spatial', 'b_spatial', 'ln_v_gain', 'ln_v_bias', 'w_branch_attn', 'w_branch_gmlp', 'w_out', 'norm_post_mix', 'norm_pre_mlp', 'w_mlp_in', 'w_mlp_out', 'norm_post_mlp']
TWIN_WEIGHTS = ['norm_pre_mix', 'w_in', 'w_spatial', 'b_spatial', 'ln_v_gain', 'ln_v_bias', 'w_branch_attn', 'w_branch_gmlp', 'w_out', 'norm_post_mix', 'norm_pre_mlp', 'w_mlp_in', 'w_mlp_out', 'norm_post_mlp']
TWIN_DIFF_INPUT = 'x'
TWIN_INPUTS = ['x', 'norm_pre_mix', 'w_in', 'w_spatial', 'b_spatial', 'ln_v_gain', 'ln_v_bias', 'w_branch_attn', 'w_branch_gmlp', 'w_out', 'norm_post_mix', 'norm_pre_mlp', 'w_mlp_in', 'w_mlp_out', 'norm_post_mlp', 'loss_target', 'm_norm_pre_mix', 'm_w_in', 'm_w_spatial', 'm_b_spatial', 'm_ln_v_gain', 'm_ln_v_bias', 'm_w_branch_attn', 'm_w_branch_gmlp', 'm_w_out', 'm_norm_post_mix', 'm_norm_pre_mlp', 'm_w_mlp_in', 'm_w_mlp_out', 'm_norm_post_mlp', 'v_norm_pre_mix', 'v_w_in', 'v_w_spatial', 'v_b_spatial', 'v_ln_v_gain', 'v_ln_v_bias', 'v_w_branch_attn', 'v_w_branch_gmlp', 'v_w_out', 'v_norm_post_mix', 'v_norm_pre_mlp', 'v_w_mlp_in', 'v_w_mlp_out', 'v_norm_post_mlp']
TWIN_OUTPUTS = ['loss', 'grad_x', 'grad_norm_pre_mix', 'grad_w_in', 'grad_w_spatial', 'grad_b_spatial', 'grad_ln_v_gain', 'grad_ln_v_bias', 'grad_w_branch_attn', 'grad_w_branch_gmlp', 'grad_w_out', 'grad_norm_post_mix', 'grad_norm_pre_mlp', 'grad_w_mlp_in', 'grad_w_mlp_out', 'grad_norm_post_mlp', 'delta_norm_pre_mix', 'delta_w_in', 'delta_w_spatial', 'delta_b_spatial', 'delta_ln_v_gain', 'delta_ln_v_bias', 'delta_w_branch_attn', 'delta_w_branch_gmlp', 'delta_w_out', 'delta_norm_post_mix', 'delta_norm_pre_mlp', 'delta_w_mlp_in', 'delta_w_mlp_out', 'delta_norm_post_mlp', 'new_m_norm_pre_mix', 'new_m_w_in', 'new_m_w_spatial', 'new_m_b_spatial', 'new_m_ln_v_gain', 'new_m_ln_v_bias', 'new_m_w_branch_attn', 'new_m_w_branch_gmlp', 'new_m_w_out', 'new_m_norm_post_mix', 'new_m_norm_pre_mlp', 'new_m_w_mlp_in', 'new_m_w_mlp_out', 'new_m_norm_post_mlp', 'new_v_norm_pre_mix', 'new_v_w_in', 'new_v_w_spatial', 'new_v_b_spatial', 'new_v_ln_v_gain', 'new_v_ln_v_bias', 'new_v_w_branch_attn', 'new_v_w_branch_gmlp', 'new_v_w_out', 'new_v_norm_post_mix', 'new_v_norm_pre_mlp', 'new_v_w_mlp_in', 'new_v_w_mlp_out', 'new_v_norm_post_mlp']
TWIN_LEAF_KINDS = {'loss': 'loss', 'grad_x': 'grad_x', 'grad_norm_pre_mix': 'grad_w', 'grad_w_in': 'grad_w', 'grad_w_spatial': 'grad_w', 'grad_b_spatial': 'grad_w', 'grad_ln_v_gain': 'grad_w', 'grad_ln_v_bias': 'grad_w', 'grad_w_branch_attn': 'grad_w', 'grad_w_branch_gmlp': 'grad_w', 'grad_w_out': 'grad_w', 'grad_norm_post_mix': 'grad_w', 'grad_norm_pre_mlp': 'grad_w', 'grad_w_mlp_in': 'grad_w', 'grad_w_mlp_out': 'grad_w', 'grad_norm_post_mlp': 'grad_w', 'delta_norm_pre_mix': 'delta_w', 'delta_w_in': 'delta_w', 'delta_w_spatial': 'delta_w', 'delta_b_spatial': 'delta_w', 'delta_ln_v_gain': 'delta_w', 'delta_ln_v_bias': 'delta_w', 'delta_w_branch_attn': 'delta_w', 'delta_w_branch_gmlp': 'delta_w', 'delta_w_out': 'delta_w', 'delta_norm_post_mix': 'delta_w', 'delta_norm_pre_mlp': 'delta_w', 'delta_w_mlp_in': 'delta_w', 'delta_w_mlp_out': 'delta_w', 'delta_norm_post_mlp': 'delta_w', 'new_m_norm_pre_mix': 'new_m', 'new_m_w_in': 'new_m', 'new_m_w_spatial': 'new_m', 'new_m_b_spatial': 'new_m', 'new_m_ln_v_gain': 'new_m', 'new_m_ln_v_bias': 'new_m', 'new_m_w_branch_attn': 'new_m', 'new_m_w_branch_gmlp': 'new_m', 'new_m_w_out': 'new_m', 'new_m_norm_post_mix': 'new_m', 'new_m_norm_pre_mlp': 'new_m', 'new_m_w_mlp_in': 'new_m', 'new_m_w_mlp_out': 'new_m', 'new_m_norm_post_mlp': 'new_m', 'new_v_norm_pre_mix': 'new_v', 'new_v_w_in': 'new_v', 'new_v_w_spatial': 'new_v', 'new_v_b_spatial': 'new_v', 'new_v_ln_v_gain': 'new_v', 'new_v_ln_v_bias': 'new_v', 'new_v_w_branch_attn': 'new_v', 'new_v_w_branch_gmlp': 'new_v', 'new_v_w_out': 'new_v', 'new_v_norm_post_mix': 'new_v', 'new_v_norm_pre_mlp': 'new_v', 'new_v_w_mlp_in': 'new_v', 'new_v_w_mlp_out': 'new_v', 'new_v_norm_post_mlp': 'new_v'}


def _forward(args):
    return _fwd_reference(*[args[k] for k in FWD_PARAMS])


def _output_shape():
    out = _jax.eval_shape(lambda: _forward(_fwd_setup_inputs(0)))
    return out.shape, out.dtype

N_MICROBATCH = 1
ADAM_LR = 0.001
ADAM_B1 = 0.9
ADAM_B2 = 0.999
ADAM_EPS = 1e-08
ADAM_WD = 0.01
ADAM_STEP = 10
PER_EXAMPLE_BATCH_AXIS = {'x': 0, 'loss_target': 0}
SHARED_INPUTS = []
_WEIGHT_DTYPES = {'norm_pre_mix': _jnp.float32, 'w_in': _jnp.float32, 'w_spatial': _jnp.float32, 'b_spatial': _jnp.float32, 'ln_v_gain': _jnp.float32, 'ln_v_bias': _jnp.float32, 'w_branch_attn': _jnp.float32, 'w_branch_gmlp': _jnp.float32, 'w_out': _jnp.float32, 'norm_post_mix': _jnp.float32, 'norm_pre_mlp': _jnp.float32, 'w_mlp_in': _jnp.float32, 'w_mlp_out': _jnp.float32, 'norm_post_mlp': _jnp.float32}
MOMENT_SCALE = {'norm_pre_mix': 1.175516e+00, 'w_in': 4.289113e-01, 'w_spatial': 4.384141e-01, 'b_spatial': 6.572774e-01, 'ln_v_gain': 5.949975e-01, 'ln_v_bias': 6.076546e-01, 'w_branch_attn': 1.746276e-01, 'w_branch_gmlp': 9.165052e+00, 'w_out': 8.962969e+00, 'norm_post_mix': 6.473038e+01, 'norm_pre_mlp': 3.382850e+00, 'w_mlp_in': 1.661894e+00, 'w_mlp_out': 9.291157e+00, 'norm_post_mlp': 6.632358e+01}


def _to_microbatches(a, axis):
    t = _jnp.moveaxis(a, axis, 0)
    t = t.reshape((N_MICROBATCH, t.shape[0] // N_MICROBATCH) + t.shape[1:])
    return _jnp.moveaxis(t, 1, axis + 1)


def setup_inputs(seed: int = 0) -> dict:
    inp = _fwd_setup_inputs(seed)
    key = _jax.random.fold_in(_jax.random.key(seed), 7919)
    shape, _ = _output_shape()
    out = dict(inp)
    out["loss_target"] = _jax.random.normal(_jax.random.fold_in(key, 0), shape, _jnp.float32)
    for i, name in enumerate(TWIN_WEIGHTS):
        w = inp[name].astype(_jnp.float32)
        if MOMENT_SCALE is None:
            s = _jnp.sqrt(_jnp.mean(_jnp.square(w)) + 1e-30)
        else:
            s = MOMENT_SCALE[name]
        km, kv = _jax.random.split(_jax.random.fold_in(key, i + 1))
        out[name] = w
        out["m_" + name] = s * _jax.random.normal(km, w.shape, _jnp.float32)
        out["v_" + name] = (s * s) * _jax.random.uniform(kv, w.shape, _jnp.float32, 0.5, 1.5)
    if N_MICROBATCH > 1:
        for name, axis in PER_EXAMPLE_BATCH_AXIS.items():
            out[name] = _to_microbatches(out[name], axis)
    return {'x': out['x'], 'norm_pre_mix': out['norm_pre_mix'], 'w_in': out['w_in'], 'w_spatial': out['w_spatial'], 'b_spatial': out['b_spatial'], 'ln_v_gain': out['ln_v_gain'], 'ln_v_bias': out['ln_v_bias'], 'w_branch_attn': out['w_branch_attn'], 'w_branch_gmlp': out['w_branch_gmlp'], 'w_out': out['w_out'], 'norm_post_mix': out['norm_post_mix'], 'norm_pre_mlp': out['norm_pre_mlp'], 'w_mlp_in': out['w_mlp_in'], 'w_mlp_out': out['w_mlp_out'], 'norm_post_mlp': out['norm_post_mlp'], 'loss_target': out['loss_target'], 'm_norm_pre_mix': out['m_norm_pre_mix'], 'm_w_in': out['m_w_in'], 'm_w_spatial': out['m_w_spatial'], 'm_b_spatial': out['m_b_spatial'], 'm_ln_v_gain': out['m_ln_v_gain'], 'm_ln_v_bias': out['m_ln_v_bias'], 'm_w_branch_attn': out['m_w_branch_attn'], 'm_w_branch_gmlp': out['m_w_branch_gmlp'], 'm_w_out': out['m_w_out'], 'm_norm_post_mix': out['m_norm_post_mix'], 'm_norm_pre_mlp': out['m_norm_pre_mlp'], 'm_w_mlp_in': out['m_w_mlp_in'], 'm_w_mlp_out': out['m_w_mlp_out'], 'm_norm_post_mlp': out['m_norm_post_mlp'], 'v_norm_pre_mix': out['v_norm_pre_mix'], 'v_w_in': out['v_w_in'], 'v_w_spatial': out['v_w_spatial'], 'v_b_spatial': out['v_b_spatial'], 'v_ln_v_gain': out['v_ln_v_gain'], 'v_ln_v_bias': out['v_ln_v_bias'], 'v_w_branch_attn': out['v_w_branch_attn'], 'v_w_branch_gmlp': out['v_w_branch_gmlp'], 'v_w_out': out['v_w_out'], 'v_norm_post_mix': out['v_norm_post_mix'], 'v_norm_pre_mlp': out['v_norm_pre_mlp'], 'v_w_mlp_in': out['v_w_mlp_in'], 'v_w_mlp_out': out['v_w_mlp_out'], 'v_norm_post_mlp': out['v_norm_post_mlp']}


def _loss(weights, diff, rest, loss_target):
    with _jax.named_scope("forward"):
        args = {**rest, TWIN_DIFF_INPUT: diff, **{k: w.astype(_WEIGHT_DTYPES[k]) for k, w in weights.items()}}
        y = _forward(args)
    with _jax.named_scope("loss_head"):
        err = _jnp.square(y.astype(_jnp.float32) - loss_target)
        return 0.5 * _jnp.sum(_jnp.mean(err, axis=-1)) if err.ndim else 0.5 * err


def _adamw(w, g, m, v):
    m = ADAM_B1 * m + (1.0 - ADAM_B1) * g
    v = ADAM_B2 * v + (1.0 - ADAM_B2) * _jnp.square(g)
    m_hat = m / (1.0 - ADAM_B1 ** ADAM_STEP)
    v_hat = v / (1.0 - ADAM_B2 ** ADAM_STEP)
    delta = -ADAM_LR * (m_hat / (_jnp.sqrt(v_hat) + ADAM_EPS) + ADAM_WD * w)
    return delta, m, v


def reference(x, norm_pre_mix, w_in, w_spatial, b_spatial, ln_v_gain, ln_v_bias, w_branch_attn, w_branch_gmlp, w_out, norm_post_mix, norm_pre_mlp, w_mlp_in, w_mlp_out, norm_post_mlp, loss_target, m_norm_pre_mix, m_w_in, m_w_spatial, m_b_spatial, m_ln_v_gain, m_ln_v_bias, m_w_branch_attn, m_w_branch_gmlp, m_w_out, m_norm_post_mix, m_norm_pre_mlp, m_w_mlp_in, m_w_mlp_out, m_norm_post_mlp, v_norm_pre_mix, v_w_in, v_w_spatial, v_b_spatial, v_ln_v_gain, v_ln_v_bias, v_w_branch_attn, v_w_branch_gmlp, v_w_out, v_norm_post_mix, v_norm_pre_mlp, v_w_mlp_in, v_w_mlp_out, v_norm_post_mlp):
    given = dict(x=x, norm_pre_mix=norm_pre_mix, w_in=w_in, w_spatial=w_spatial, b_spatial=b_spatial, ln_v_gain=ln_v_gain, ln_v_bias=ln_v_bias, w_branch_attn=w_branch_attn, w_branch_gmlp=w_branch_gmlp, w_out=w_out, norm_post_mix=norm_post_mix, norm_pre_mlp=norm_pre_mlp, w_mlp_in=w_mlp_in, w_mlp_out=w_mlp_out, norm_post_mlp=norm_post_mlp, loss_target=loss_target, m_norm_pre_mix=m_norm_pre_mix, m_w_in=m_w_in, m_w_spatial=m_w_spatial, m_b_spatial=m_b_spatial, m_ln_v_gain=m_ln_v_gain, m_ln_v_bias=m_ln_v_bias, m_w_branch_attn=m_w_branch_attn, m_w_branch_gmlp=m_w_branch_gmlp, m_w_out=m_w_out, m_norm_post_mix=m_norm_post_mix, m_norm_pre_mlp=m_norm_pre_mlp, m_w_mlp_in=m_w_mlp_in, m_w_mlp_out=m_w_mlp_out, m_norm_post_mlp=m_norm_post_mlp, v_norm_pre_mix=v_norm_pre_mix, v_w_in=v_w_in, v_w_spatial=v_w_spatial, v_b_spatial=v_b_spatial, v_ln_v_gain=v_ln_v_gain, v_ln_v_bias=v_ln_v_bias, v_w_branch_attn=v_w_branch_attn, v_w_branch_gmlp=v_w_branch_gmlp, v_w_out=v_w_out, v_norm_post_mix=v_norm_post_mix, v_norm_pre_mlp=v_norm_pre_mlp, v_w_mlp_in=v_w_mlp_in, v_w_mlp_out=v_w_mlp_out, v_norm_post_mlp=v_norm_post_mlp)
    weights = {n: given[n] for n in TWIN_WEIGHTS}
    shared = {n: given[n] for n in SHARED_INPUTS}
    per_example = {n: given[n] for n in ['x']}
    grad_fn = _jax.value_and_grad(_loss, argnums=(0, 1))

    def one_microbatch(ex, loss_target):
        ex = dict(ex)
        diff = ex.pop(TWIN_DIFF_INPUT)
        return grad_fn(weights, diff, {**shared, **ex}, loss_target)

    if N_MICROBATCH == 1:
        loss, (grad_w, grad_x) = one_microbatch(per_example, given["loss_target"])
    else:
        def body(carry, xs):
            loss_sum, grad_sum = carry
            l_k, (gw_k, gx_k) = one_microbatch(xs[0], xs[1])
            with _jax.named_scope("update"):
                return (loss_sum + l_k, _jax.tree.map(_jnp.add, grad_sum, gw_k)), gx_k

        init = (_jnp.zeros((), _jnp.float32), _jax.tree.map(_jnp.zeros_like, weights))
        (loss, grad_w), grad_x = _jax.lax.scan(body, init, (per_example, given["loss_target"]))
    with _jax.named_scope("update"):
        delta_w, new_m, new_v = {}, {}, {}
        for n in TWIN_WEIGHTS:
            delta_w[n], new_m[n], new_v[n] = _adamw(weights[n], grad_w[n], given["m_" + n], given["v_" + n])
    return (loss, grad_x, *[grad_w[n] for n in TWIN_WEIGHTS], *[delta_w[n] for n in TWIN_WEIGHTS],
            *[new_m[n] for n in TWIN_WEIGHTS], *[new_v[n] for n in TWIN_WEIGHTS])
```

```python
import math

import jax
import jax.numpy as jnp
from jax import lax
from jax.experimental import pallas as pl
from jax.experimental.pallas import tpu as pltpu

F32 = jnp.float32
BF16 = jnp.bfloat16
MESH = pl.DeviceIdType.MESH

D_MODEL = 1024
HEAD_DIM = 64
HEADS_PER_GROUP = 4
GROUP_W = HEADS_PER_GROUP * HEAD_DIM
DILATIONS = (1, 4, 16)
N_GROUPS = len(DILATIONS)
ATTN_W = N_GROUPS * GROUP_W
QKV_W = 3 * ATTN_W
GMLP_W = 512
GMLP_GROUPS = 4
CHUNK = 128
REST_W = 2 * GMLP_W + 2 * D_MODEL
IN_W = QKV_W + REST_W
D_FF = 4096
QBLK = 128
ROPE_THETA = 10000.0
EPS = 1e-6
NEG = -1e30
SCALE = HEAD_DIM ** -0.5
N_CHIPS = 4

ADAM_LR = 0.001
ADAM_B1 = 0.9
ADAM_B2 = 0.999
ADAM_EPS = 1e-08
ADAM_WD = 0.01
ADAM_STEP = 10

MIB = 1024 * 1024
HBM_SPEC = pl.BlockSpec(memory_space=pltpu.HBM)
VMEM_SPEC = pl.BlockSpec(memory_space=pltpu.VMEM)


def _params(semantics, vmem_mib):
    return pltpu.CompilerParams(dimension_semantics=semantics, vmem_limit_bytes=vmem_mib * MIB)


def _dot(a, b):
    return jnp.dot(a, b, preferred_element_type=F32)


def _dot_nt(a, b):
    return lax.dot_general(a, b, (((1,), (1,)), ((), ())), preferred_element_type=F32)


def _dot_tn(a, b):
    return lax.dot_general(a, b, (((0,), (0,)), ((), ())), preferred_element_type=F32)


_GELU_C = math.sqrt(2.0 / math.pi)


def _gelu(x):
    return x * (0.5 * (1.0 + jnp.tanh(_GELU_C * (x + 0.044715 * (x * x * x)))))


def _gelu_grad(x):
    t = jnp.tanh(_GELU_C * (x + 0.044715 * (x * x * x)))
    return 0.5 * (1.0 + t) + 0.5 * x * (1.0 - t * t) * (_GELU_C * (1.0 + 3.0 * 0.044715 * (x * x)))


def _rsqrt_ms(v):
    return lax.rsqrt(jnp.mean(v * v, axis=-1, keepdims=True) + EPS)


def _rmsnorm_bwd(dn, src, gain):
    r = _rsqrt_ms(src)
    t = gain * dn
    dgain = jnp.sum(dn * (src * r), axis=0, keepdims=True)
    dsrc = r * t - src * ((r * r * r) * jnp.mean(t * src, axis=-1, keepdims=True))
    return dsrc, dgain


def _rot_half(v):
    w = v.shape[-1]
    lane = lax.broadcasted_iota(jnp.int32, v.shape, v.ndim - 1)
    return jnp.where((lane % HEAD_DIM) < HEAD_DIM // 2, pltpu.roll(v, w - HEAD_DIM // 2, v.ndim - 1),
                     pltpu.roll(v, HEAD_DIM // 2, v.ndim - 1))


def _head_masks(shape):
    lane = lax.broadcasted_iota(jnp.int32, shape, 1)
    return [(lane >= h * HEAD_DIM) & (lane < (h + 1) * HEAD_DIM) for h in range(HEADS_PER_GROUP)]


def _rope_tables(seq):
    half = HEAD_DIM // 2
    inv_freq = ROPE_THETA ** (-jnp.arange(half, dtype=F32) / half)
    ang = jnp.arange(seq, dtype=F32)[:, None] * inv_freq[None, :]
    cos, sin = jnp.cos(ang), jnp.sin(ang)
    cos_t = jnp.tile(jnp.concatenate([cos, cos], axis=-1), (1, HEADS_PER_GROUP))
    sin_t = jnp.tile(jnp.concatenate([-sin, sin], axis=-1), (1, HEADS_PER_GROUP))
    return cos_t, sin_t


def _in_proj(x, g0, w_in, cos_t, sin_t):
    seq = x.shape[0]
    tm, tn = 256, 256
    n_qk = 2 * ATTN_W // tn
    n_qkv = QKV_W // tn

    def body(x_ref, g_ref, w_ref, cos_ref, sin_ref, h_ref, qkv_ref, rest_ref):
        xv = x_ref[...]
        hb = ((xv * _rsqrt_ms(xv)) * g_ref[...]).astype(BF16)
        h_ref[...] = hb
        cos, sin = cos_ref[...], sin_ref[...]
        for j in range(IN_W // tn):
            p = _dot(hb, w_ref[:, j * tn:(j + 1) * tn])
            if j < n_qk:
                qkv_ref[:, j * tn:(j + 1) * tn] = (p * cos + _rot_half(p) * sin).astype(BF16)
            elif j < n_qkv:
                qkv_ref[:, j * tn:(j + 1) * tn] = p.astype(BF16)
            else:
                rest_ref[:, (j - n_qkv) * tn:(j - n_qkv + 1) * tn] = p

    return pl.pallas_call(
        body, name="in_proj", grid=(seq // tm,),
        in_specs=[pl.BlockSpec((tm, D_MODEL), lambda i: (i, 0)),
                  pl.BlockSpec((1, D_MODEL), lambda i: (0, 0)),
                  pl.BlockSpec((D_MODEL, IN_W), lambda i: (0, 0)),
                  pl.BlockSpec((tm, GROUP_W), lambda i: (i, 0)),
                  pl.BlockSpec((tm, GROUP_W), lambda i: (i, 0))],
        out_specs=[pl.BlockSpec((tm, D_MODEL), lambda i: (i, 0)),
                   pl.BlockSpec((tm, QKV_W), lambda i: (i, 0)),
                   pl.BlockSpec((tm, REST_W), lambda i: (i, 0))],
        out_shape=[jax.ShapeDtypeStruct((seq, D_MODEL), BF16),
                   jax.ShapeDtypeStruct((seq, QKV_W), BF16),
                   jax.ShapeDtypeStruct((seq, REST_W), F32)],
        compiler_params=_params(("arbitrary",), 48),
    )(x, g0, w_in, cos_t, sin_t)


def _band_masks():
    qi = lax.broadcasted_iota(jnp.int32, (QBLK, QBLK), 0)
    kj = lax.broadcasted_iota(jnp.int32, (QBLK, QBLK), 1)
    return kj <= qi, kj >= qi


def _attn_tile(length):
    return min(512, length)


def _attn_fwd(qkv, group, dil):
    seq = qkv.shape[0]
    length = seq // dil
    tq = _attn_tile(length)
    nsub = tq // QBLK
    nblk = length // tq
    per_res = QKV_W // GROUP_W

    def body(q_ref, k_ref, v_ref, kp_ref, vp_ref, o_ref, l_ref):
        n = pl.program_id(1)
        mask_c, mask_p0 = _band_masks()
        hmask = _head_masks((QBLK, GROUP_W))
        zero = jnp.zeros((), BF16)
        for b in range(nsub):
            rows = slice(b * QBLK, (b + 1) * QBLK)
            q = q_ref[rows, :]
            kc, vc = k_ref[rows, :], v_ref[rows, :]
            if b == 0:
                kp, vp = kp_ref[...], vp_ref[...]
                mask_p = mask_p0 & (n > 0)
            else:
                prow = slice((b - 1) * QBLK, b * QBLK)
                kp, vp = k_ref[prow, :], v_ref[prow, :]
                mask_p = mask_p0
            o_acc = jnp.zeros((QBLK, GROUP_W), F32)
            l_acc = jnp.zeros((QBLK, GROUP_W), F32)
            for h in range(HEADS_PER_GROUP):
                hm = hmask[h]
                sc = jnp.where(mask_c, _dot_nt(q, jnp.where(hm, kc, zero)) * SCALE, NEG)
                sp = jnp.where(mask_p, _dot_nt(q, jnp.where(hm, kp, zero)) * SCALE, NEG)
                m = jnp.maximum(jnp.max(sc, axis=-1, keepdims=True), jnp.max(sp, axis=-1, keepdims=True))
                pc, pp = jnp.exp(sc - m), jnp.exp(sp - m)
                den = jnp.sum(pc, axis=-1, keepdims=True) + jnp.sum(pp, axis=-1, keepdims=True)
                pv = _dot(pc.astype(BF16), jnp.where(hm, vc, zero)) + _dot(pp.astype(BF16), jnp.where(hm, vp, zero))
                o_acc = o_acc + pv / den
                l_acc = l_acc + jnp.where(hm, m + jnp.log(den), 0.0)
            o_ref[rows, :] = o_acc
            l_ref[rows, :] = l_acc

    view = qkv.reshape(length, dil * QKV_W)
    cur = lambda off: pl.BlockSpec((tq, GROUP_W), lambda r, n: (n, r * per_res + off + group))
    prev = lambda off: pl.BlockSpec((QBLK, GROUP_W), lambda r, n: (jnp.maximum(n * nsub - 1, 0), r * per_res + off + group))
    o, l = pl.pallas_call(
        body, name=f"attn_fwd_d{dil}", grid=(dil, nblk),
        in_specs=[cur(0), cur(N_GROUPS), cur(2 * N_GROUPS), prev(N_GROUPS), prev(2 * N_GROUPS)],
        out_specs=[pl.BlockSpec((tq, GROUP_W), lambda r, n: (n, r))] * 2,
        out_shape=[jax.ShapeDtypeStruct((length, dil * GROUP_W), F32)] * 2,
        compiler_params=_params(("arbitrary", "arbitrary"), 32),
    )(view, view, view, view, view)
    return o.reshape(seq, GROUP_W), l.reshape(seq, GROUP_W)


def _attn_bwd(qkv, dy, y, lse, cos_t, sin_t, group, dil):
    seq = qkv.shape[0]
    length = seq // dil
    tq = _attn_tile(length)
    nsub = tq // QBLK
    nblk = length // tq
    per_res = QKV_W // GROUP_W

    def body(q_ref, k_ref, v_ref, kp_ref, vp_ref, qn_ref, dy_ref, y_ref, l_ref, dyn_ref, yn_ref, ln_ref,
             cos_ref, sin_ref, out_ref, dq_s, dk_s, dv_s):
        n = pl.program_id(1)
        mask_c, mask_p0 = _band_masks()
        hmask = _head_masks((QBLK, GROUP_W))
        zero = jnp.zeros((), BF16)

        def head_terms(h, q, dyv, yv, lv, kk, vv, mask):
            hm = hmask[h]
            delta = jnp.sum(jnp.where(hm, dyv * yv, 0.0), axis=-1, keepdims=True)
            lh = jnp.max(jnp.where(hm, lv, NEG), axis=-1, keepdims=True)
            s = _dot_nt(q, jnp.where(hm, kk, zero)) * SCALE
            p = jnp.exp(jnp.where(mask, s - lh, NEG))
            dyb = jnp.where(hm, dyv, 0.0).astype(BF16)
            dp = _dot_nt(dyb, vv)
            ds = p * (dp - delta)
            return p.astype(BF16), ds.astype(BF16), dyb, jnp.where(hm, q, zero)

        dk_s[...] = jnp.zeros(dk_s.shape, F32)
        dv_s[...] = jnp.zeros(dv_s.shape, F32)
        for b in range(nsub):
            rows = slice(b * QBLK, (b + 1) * QBLK)
            q, dyv, yv, lv = q_ref[rows, :], dy_ref[rows, :], y_ref[rows, :], l_ref[rows, :]
            kc, vc = k_ref[rows, :], v_ref[rows, :]
            if b == 0:
                kp, vp = kp_ref[...], vp_ref[...]
                mask_p = mask_p0 & (n > 0)
            else:
                prow = slice((b - 1) * QBLK, b * QBLK)
                kp, vp = k_ref[prow, :], v_ref[prow, :]
                mask_p = mask_p0
            dq = jnp.zeros((QBLK, GROUP_W), F32)
            for h in range(HEADS_PER_GROUP):
                hm = hmask[h]
                pc, dsc, dyb, qh = head_terms(h, q, dyv, yv, lv, kc, vc, mask_c)
                pp, dsp, _, _ = head_terms(h, q, dyv, yv, lv, kp, vp, mask_p)
                dq = dq + (_dot(dsc, jnp.where(hm, kc, zero)) + _dot(dsp, jnp.where(hm, kp, zero))) * SCALE
                dv_s[rows, :] += _dot_tn(pc, dyb)
                dk_s[rows, :] += _dot_tn(dsc, qh) * SCALE
                if b > 0:
                    dv_s[prow, :] += _dot_tn(pp, dyb)
                    dk_s[prow, :] += _dot_tn(dsp, qh) * SCALE
            dq_s[rows, :] = dq
        rows = slice((nsub - 1) * QBLK, nsub * QBLK)
        qn, dyn, yn, ln = qn_ref[...], dyn_ref[...], yn_ref[...], ln_ref[...]
        mask_n = mask_p0 & (n < nblk - 1)
        for h in range(HEADS_PER_GROUP):
            pn, dsn, dyb, qh = head_terms(h, qn, dyn, yn, ln, k_ref[rows, :], v_ref[rows, :], mask_n)
            dv_s[rows, :] += _dot_tn(pn, dyb)
            dk_s[rows, :] += _dot_tn(dsn, qh) * SCALE
        cos, sin = cos_ref[...], sin_ref[...]
        dq, dk = dq_s[...], dk_s[...]
        out_ref[:, 0:GROUP_W] = (dq * cos - _rot_half(dq) * sin).astype(BF16)
        out_ref[:, GROUP_W:2 * GROUP_W] = (dk * cos - _rot_half(dk) * sin).astype(BF16)
        out_ref[:, 2 * GROUP_W:3 * GROUP_W] = dv_s[...].astype(BF16)

    qview = qkv.reshape(length, dil * QKV_W)
    rs = lambda a: a.reshape(length, dil * GROUP_W)
    cur = lambda off: pl.BlockSpec((tq, GROUP_W), lambda r, n: (n, r * per_res + off + group))
    prev = lambda off: pl.BlockSpec((QBLK, GROUP_W), lambda r, n: (jnp.maximum(n * nsub - 1, 0), r * per_res + off + group))
    nxt_q = pl.BlockSpec((QBLK, GROUP_W), lambda r, n: (jnp.minimum((n + 1) * nsub, nblk * nsub - 1), r * per_res + group))
    tok = pl.BlockSpec((tq, GROUP_W), lambda r, n: (n, r))
    tok_next = pl.BlockSpec((QBLK, GROUP_W), lambda r, n: (jnp.minimum((n + 1) * nsub, nblk * nsub - 1), r))
    out = pl.pallas_call(
        body, name=f"attn_bwd_d{dil}", grid=(dil, nblk),
        in_specs=[cur(0), cur(N_GROUPS), cur(2 * N_GROUPS), prev(N_GROUPS), prev(2 * N_GROUPS), nxt_q,
                  tok, tok, tok, tok_next, tok_next, tok_next, tok, tok],
        out_specs=pl.BlockSpec((tq, 3 * GROUP_W), lambda r, n: (n, r)),
        out_shape=jax.ShapeDtypeStruct((length, dil * 3 * GROUP_W), BF16),
        scratch_shapes=[pltpu.VMEM((tq, GROUP_W), F32)] * 3,
        compiler_params=_params(("arbitrary", "arbitrary"), 32),
    )(qview, qview, qview, qview, qview, qview, rs(dy), rs(y), rs(lse), rs(dy), rs(y), rs(lse), rs(cos_t), rs(sin_t))
    return out.reshape(seq, 3 * GROUP_W)


def _layernorm_stats(z):
    mu = jnp.mean(z, axis=-1, keepdims=True)
    zc = z - mu
    rstd = lax.rsqrt(jnp.mean(zc * zc, axis=-1, keepdims=True) + EPS)
    return zc * rstd, rstd


def _tril_mask():
    row = lax.broadcasted_iota(jnp.int32, (CHUNK, CHUNK), 0)
    col = lax.broadcasted_iota(jnp.int32, (CHUNK, CHUNK), 1)
    return col <= row


def _mix_fwd(o_l, rest, x, w_sp, b_col, ln_g, ln_b, w_ba, w_bg, w_out, g1):
    seq = x.shape[0]
    tm = 256

    def body(o0, l0, o1, l1, o2, l2, up_ref, zp_ref, gap_ref, gbp_ref, x_ref, wsp_ref, bcol_ref, lg_ref, lb_ref,
             wba_ref, wbg_ref, wout_ref, g1_ref, ya_ref, lj_ref, yg_ref, mg_ref, y_ref, x1_ref):
        lses = [l0[...], l1[...], l2[...]]
        m = jnp.maximum(jnp.maximum(lses[0], lses[1]), lses[2])
        es = [jnp.exp(l - m) for l in lses]
        tot = es[0] + es[1] + es[2]
        ya = (es[0] * o0[...] + es[1] * o1[...] + es[2] * o2[...]) / tot
        ya_ref[...] = ya
        lj_ref[...] = m + jnp.log(tot)
        zhat, _ = _layernorm_stats(_gelu(zp_ref[...]))
        zln = (zhat * lg_ref[...] + lb_ref[...]).astype(BF16)
        u = _gelu(up_ref[...])
        tril = _tril_mask()
        for g in range(GMLP_GROUPS):
            wm = jnp.where(tril, wsp_ref[g], 0.0).astype(BF16)
            cols = slice(g * CHUNK, (g + 1) * CHUNK)
            for c in range(tm // CHUNK):
                rows = slice(c * CHUNK, (c + 1) * CHUNK)
                sz = _dot(wm, zln[rows, cols]) + bcol_ref[g]
                yg_ref[rows, cols] = (u[rows, cols] * sz).astype(BF16)
        a = _dot(ya.astype(BF16), wba_ref[...])
        bm = _dot(yg_ref[...], wbg_ref[...])
        merged = (jax.nn.sigmoid(gap_ref[...]) * a + jax.nn.sigmoid(gbp_ref[...]) * bm).astype(BF16)
        mg_ref[...] = merged
        yv = _dot(merged, wout_ref[...])
        y_ref[...] = yv
        x1_ref[...] = x_ref[...] + (yv * _rsqrt_ms(yv)) * g1_ref[...]

    tok = lambda w: pl.BlockSpec((tm, w), lambda i: (i, 0))
    full = lambda *s: pl.BlockSpec(s, lambda i: (0,) * len(s))
    return pl.pallas_call(
        body, name="mix_fwd", grid=(seq // tm,),
        in_specs=[tok(GROUP_W)] * 6 + [
            pl.BlockSpec((tm, GMLP_W), lambda i: (i, 0)), pl.BlockSpec((tm, GMLP_W), lambda i: (i, 1)),
            pl.BlockSpec((tm, D_MODEL), lambda i: (i, 1)), pl.BlockSpec((tm, D_MODEL), lambda i: (i, 2)),
            tok(D_MODEL), full(GMLP_GROUPS, CHUNK, CHUNK), full(GMLP_GROUPS, CHUNK, 1), full(1, GMLP_W), full(1, GMLP_W),
            full(GROUP_W, D_MODEL), full(GMLP_W, D_MODEL), full(D_MODEL, D_MODEL), full(1, D_MODEL)],
        out_specs=[tok(GROUP_W), tok(GROUP_W), tok(GMLP_W), tok(D_MODEL), tok(D_MODEL), tok(D_MODEL)],
        out_shape=[jax.ShapeDtypeStruct((seq, GROUP_W), F32), jax.ShapeDtypeStruct((seq, GROUP_W), F32),
                   jax.ShapeDtypeStruct((seq, GMLP_W), BF16), jax.ShapeDtypeStruct((seq, D_MODEL), BF16),
                   jax.ShapeDtypeStruct((seq, D_MODEL), F32), jax.ShapeDtypeStruct((seq, D_MODEL), F32)],
        compiler_params=_params(("arbitrary",), 48),
    )(*o_l, rest, rest, rest, rest, x, w_sp, b_col, ln_g, ln_b, w_ba, w_bg, w_out, g1)


def _mlp_fwd(x1, g2, g3, w_mi, w_mo, target):
    seq = x1.shape[0]
    tm, tf = 512, 512
    nf = D_FF // tf

    def body(x1_ref, g2_ref, g3_ref, wmi_ref, wmo_ref, t_ref, h2_ref, a_ref, dy2_ref, dout_ref, loss_ref, dg3_ref,
             h2_s, acc_s):
        i, j = pl.program_id(0), pl.program_id(1)

        @pl.when(j == 0)
        def _():
            xv = x1_ref[...]
            hb = ((xv * _rsqrt_ms(xv)) * g2_ref[...]).astype(BF16)
            h2_s[...] = hb
            h2_ref[...] = hb
            acc_s[...] = jnp.zeros(acc_s.shape, F32)

        @pl.when((i == 0) & (j == 0))
        def _():
            loss_ref[...] = jnp.zeros(loss_ref.shape, F32)
            dg3_ref[...] = jnp.zeros(dg3_ref.shape, F32)

        a = jnp.maximum(_dot(h2_s[...], wmi_ref[...]), 0.0)
        a_ref[...] = a.astype(BF16)
        acc_s[...] += _dot((a * a).astype(BF16), wmo_ref[...])

        @pl.when(j == nf - 1)
        def _():
            y2 = acc_s[...]
            r3 = _rsqrt_ms(y2)
            out = x1_ref[...] + (y2 * r3) * g3_ref[...]
            diff = out - t_ref[...]
            tile_loss = 0.5 * jnp.sum(jnp.mean(diff * diff, axis=-1, keepdims=True), axis=0, keepdims=True)
            loss_ref[...] += jnp.broadcast_to(tile_loss, loss_ref.shape)
            dout = diff * (1.0 / D_MODEL)
            dout_ref[...] = dout
            dy2, dg3 = _rmsnorm_bwd(dout, y2, g3_ref[...])
            dy2_ref[...] = dy2.astype(BF16)
            dg3_ref[...] += dg3

    tok = lambda w: pl.BlockSpec((tm, w), lambda i, j: (i, 0))
    vec = pl.BlockSpec((1, D_MODEL), lambda i, j: (0, 0))
    return pl.pallas_call(
        body, name="mlp_fwd", grid=(seq // tm, nf),
        in_specs=[tok(D_MODEL), vec, vec, pl.BlockSpec((D_MODEL, tf), lambda i, j: (0, j)),
                  pl.BlockSpec((tf, D_MODEL), lambda i, j: (j, 0)), tok(D_MODEL)],
        out_specs=[tok(D_MODEL), pl.BlockSpec((tm, tf), lambda i, j: (i, j)), tok(D_MODEL), tok(D_MODEL),
                   pl.BlockSpec((8, 128), lambda i, j: (0, 0)), vec],
        out_shape=[jax.ShapeDtypeStruct((seq, D_MODEL), BF16), jax.ShapeDtypeStruct((seq, D_FF), BF16),
                   jax.ShapeDtypeStruct((seq, D_MODEL), BF16), jax.ShapeDtypeStruct((seq, D_MODEL), F32),
                   jax.ShapeDtypeStruct((8, 128), F32), jax.ShapeDtypeStruct((1, D_MODEL), F32)],
        scratch_shapes=[pltpu.VMEM((tm, D_MODEL), BF16), pltpu.VMEM((tm, D_MODEL), F32)],
        compiler_params=_params(("arbitrary", "arbitrary"), 48),
    )(x1, g2, g3, w_mi, w_mo, target)


def _mlp_bwd(dy2, a, w_mo, w_mi, dout, x1, y, g2, g1):
    seq = x1.shape[0]
    tm, tf = 512, 512
    nf = D_FF // tf

    def body(dy2_ref, a_ref, wmo_ref, wmi_ref, dout_ref, x1_ref, y_ref, g2_ref, g1_ref,
             dap_ref, dx1_ref, dy_ref, dg2_ref, dg1_ref, acc_s):
        i, j = pl.program_id(0), pl.program_id(1)

        @pl.when(j == 0)
        def _():
            acc_s[...] = jnp.zeros(acc_s.shape, F32)

        @pl.when((i == 0) & (j == 0))
        def _():
            dg2_ref[...] = jnp.zeros(dg2_ref.shape, F32)
            dg1_ref[...] = jnp.zeros(dg1_ref.shape, F32)

        da2 = _dot_nt(dy2_ref[...], wmo_ref[...])
        dap = (da2 * (2.0 * a_ref[...].astype(F32))).astype(BF16)
        dap_ref[...] = dap
        acc_s[...] += _dot_nt(dap, wmi_ref[...])

        @pl.when(j == nf - 1)
        def _():
            dres, dg2 = _rmsnorm_bwd(acc_s[...], x1_ref[...], g2_ref[...])
            dx1 = dout_ref[...] + dres
            dx1_ref[...] = dx1
            dg2_ref[...] += dg2
            dyv, dg1 = _rmsnorm_bwd(dx1, y_ref[...], g1_ref[...])
            dy_ref[...] = dyv.astype(BF16)
            dg1_ref[...] += dg1

    tok = lambda w: pl.BlockSpec((tm, w), lambda i, j: (i, 0))
    vec = pl.BlockSpec((1, D_MODEL), lambda i, j: (0, 0))
    return pl.pallas_call(
        body, name="mlp_bwd", grid=(seq // tm, nf),
        in_specs=[tok(D_MODEL), pl.BlockSpec((tm, tf), lambda i, j: (i, j)),
                  pl.BlockSpec((tf, D_MODEL), lambda i, j: (j, 0)), pl.BlockSpec((D_MODEL, tf), lambda i, j: (0, j)),
                  tok(D_MODEL), tok(D_MODEL), tok(D_MODEL), vec, vec],
        out_specs=[pl.BlockSpec((tm, tf), lambda i, j: (i, j)), tok(D_MODEL), tok(D_MODEL), vec, vec],
        out_shape=[jax.ShapeDtypeStruct((seq, D_FF), BF16), jax.ShapeDtypeStruct((seq, D_MODEL), F32),
                   jax.ShapeDtypeStruct((seq, D_MODEL), BF16), jax.ShapeDtypeStruct((1, D_MODEL), F32),
                   jax.ShapeDtypeStruct((1, D_MODEL), F32)],
        scratch_shapes=[pltpu.VMEM((tm, D_MODEL), F32)],
        compiler_params=_params(("arbitrary", "arbitrary"), 48),
    )(dy2, a, w_mo, w_mi, dout, x1, y, g2, g1)


def _tn_matmul(a, b, name, bm, bn, square_a=False):
    seq, m = a.shape
    n = b.shape[1]
    ts = 512

    def body(a_ref, b_ref, o_ref):
        @pl.when(pl.program_id(2) == 0)
        def _():
            o_ref[...] = jnp.zeros(o_ref.shape, F32)

        av = a_ref[...]
        if square_a:
            af = av.astype(F32)
            av = (af * af).astype(BF16)
        o_ref[...] += _dot_tn(av, b_ref[...])

    return pl.pallas_call(
        body, name=name, grid=(m // bm, n // bn, seq // ts),
        in_specs=[pl.BlockSpec((ts, bm), lambda mi, ni, s: (s, mi)), pl.BlockSpec((ts, bn), lambda mi, ni, s: (s, ni))],
        out_specs=pl.BlockSpec((bm, bn), lambda mi, ni, s: (mi, ni)),
        out_shape=jax.ShapeDtypeStruct((m, n), F32),
        compiler_params=_params(("arbitrary", "arbitrary", "arbitrary"), 40),
    )(a, b)


def _mix_bwd(dy, ya, yg, mg, rest, w_out, w_ba, w_bg, w_sp, b_col, ln_g, ln_b):
    seq = dy.shape[0]
    tm = 256

    def body(dy_ref, ya_ref, yg_ref, mg_ref, up_ref, zp_ref, gap_ref, gbp_ref, wout_ref, wba_ref, wbg_ref,
             wsp_ref, bcol_ref, lg_ref, lb_ref,
             dya_ref, dpr_ref, dwout_ref, dwba_ref, dwbg_ref, dwsp_ref, dbb_ref, dlg_ref, dlb_ref, dzln_s, du_s):
        @pl.when(pl.program_id(0) == 0)
        def _():
            for ref in (dwout_ref, dwba_ref, dwbg_ref, dwsp_ref, dbb_ref, dlg_ref, dlb_ref):
                ref[...] = jnp.zeros(ref.shape, F32)

        dyv = dy_ref[...]
        dm = _dot_nt(dyv, wout_ref[...])
        dwout_ref[...] += _dot_tn(mg_ref[...], dyv)
        yab = ya_ref[...].astype(BF16)
        ygb = yg_ref[...]
        a = _dot(yab, wba_ref[...])
        bm = _dot(ygb, wbg_ref[...])
        ga = jax.nn.sigmoid(gap_ref[...])
        gb = jax.nn.sigmoid(gbp_ref[...])
        dpr_ref[:, 2 * GMLP_W:2 * GMLP_W + D_MODEL] = (dm * a * (ga * (1.0 - ga))).astype(BF16)
        dpr_ref[:, 2 * GMLP_W + D_MODEL:REST_W] = (dm * bm * (gb * (1.0 - gb))).astype(BF16)
        da = (dm * ga).astype(BF16)
        db = (dm * gb).astype(BF16)
        dwba_ref[...] += _dot_tn(yab, da)
        dwbg_ref[...] += _dot_tn(ygb, db)
        dya_ref[...] = _dot_nt(da, wba_ref[...])
        dyg = _dot_nt(db, wbg_ref[...])

        zp = zp_ref[...]
        zhat, rstd = _layernorm_stats(_gelu(zp))
        lg = lg_ref[...]
        zln = (zhat * lg + lb_ref[...]).astype(BF16)
        up = up_ref[...]
        u = _gelu(up)
        tril = _tril_mask()
        for g in range(GMLP_GROUPS):
            wm = jnp.where(tril, wsp_ref[g], 0.0).astype(BF16)
            cols = slice(g * CHUNK, (g + 1) * CHUNK)
            for c in range(tm // CHUNK):
                rows = slice(c * CHUNK, (c + 1) * CHUNK)
                zb = zln[rows, cols]
                sz = _dot(wm, zb) + bcol_ref[g]
                dyg_cg = dyg[rows, cols]
                du_s[rows, cols] = dyg_cg * sz
                dsz = dyg_cg * u[rows, cols]
                dszb = dsz.astype(BF16)
                dbb_ref[g] += jnp.broadcast_to(jnp.sum(dsz, axis=-1, keepdims=True), (CHUNK, CHUNK))
                dwsp_ref[g] += jnp.where(tril, _dot_nt(dszb, zb), 0.0)
                dzln_s[rows, cols] = _dot_tn(wm, dszb)
        dzln = dzln_s[...]
        dlg_ref[...] += jnp.sum(dzln * zhat, axis=0, keepdims=True)
        dlb_ref[...] += jnp.sum(dzln, axis=0, keepdims=True)
        dzh = dzln * lg
        dz = rstd * (dzh - jnp.mean(dzh, axis=-1, keepdims=True) - zhat * jnp.mean(dzh * zhat, axis=-1, keepdims=True))
        dpr_ref[:, GMLP_W:2 * GMLP_W] = (dz * _gelu_grad(zp)).astype(BF16)
        dpr_ref[:, 0:GMLP_W] = (du_s[...] * _gelu_grad(up)).astype(BF16)

    tok = lambda w: pl.BlockSpec((tm, w), lambda i: (i, 0))
    full = lambda *s: pl.BlockSpec(s, lambda i: (0,) * len(s))
    return pl.pallas_call(
        body, name="mix_bwd", grid=(seq // tm,),
        in_specs=[tok(D_MODEL), tok(GROUP_W), tok(GMLP_W), tok(D_MODEL),
                  pl.BlockSpec((tm, GMLP_W), lambda i: (i, 0)), pl.BlockSpec((tm, GMLP_W), lambda i: (i, 1)),
                  pl.BlockSpec((tm, D_MODEL), lambda i: (i, 1)), pl.BlockSpec((tm, D_MODEL), lambda i: (i, 2)),
                  full(D_MODEL, D_MODEL), full(GROUP_W, D_MODEL), full(GMLP_W, D_MODEL),
                  full(GMLP_GROUPS, CHUNK, CHUNK), full(GMLP_GROUPS, CHUNK, 1), full(1, GMLP_W), full(1, GMLP_W)],
        out_specs=[tok(GROUP_W), tok(REST_W), full(D_MODEL, D_MODEL), full(GROUP_W, D_MODEL), full(GMLP_W, D_MODEL),
                   full(GMLP_GROUPS, CHUNK, CHUNK), full(GMLP_GROUPS, CHUNK, CHUNK), full(1, GMLP_W), full(1, GMLP_W)],
        out_shape=[jax.ShapeDtypeStruct((seq, GROUP_W), F32), jax.ShapeDtypeStruct((seq, REST_W), BF16),
                   jax.ShapeDtypeStruct((D_MODEL, D_MODEL), F32), jax.ShapeDtypeStruct((GROUP_W, D_MODEL), F32),
                   jax.ShapeDtypeStruct((GMLP_W, D_MODEL), F32), jax.ShapeDtypeStruct((GMLP_GROUPS, CHUNK, CHUNK), F32),
                   jax.ShapeDtypeStruct((GMLP_GROUPS, CHUNK, CHUNK), F32), jax.ShapeDtypeStruct((1, GMLP_W), F32),
                   jax.ShapeDtypeStruct((1, GMLP_W), F32)],
        scratch_shapes=[pltpu.VMEM((tm, GMLP_W), F32), pltpu.VMEM((tm, GMLP_W), F32)],
        compiler_params=_params(("arbitrary",), 56),
    )(dy, ya, yg, mg, rest, rest, rest, rest, w_out, w_ba, w_bg, w_sp, b_col, ln_g, ln_b)


def _in_proj_bwd(dqkv, drest, w_qkv, w_rest, x, dx1, g0):
    seq = x.shape[0]
    tm = 256

    def body(d0, d1, d2, dr_ref, w0, w1, w2, wr_ref, x_ref, dx1_ref, g_ref, gx_ref, dg_ref):
        @pl.when(pl.program_id(0) == 0)
        def _():
            dg_ref[...] = jnp.zeros(dg_ref.shape, F32)

        dh = _dot_nt(dr_ref[...], wr_ref[...])
        for d_ref, w_ref in ((d0, w0), (d1, w1), (d2, w2)):
            dh = dh + _dot_nt(d_ref[...], w_ref[...])
        dres, dg = _rmsnorm_bwd(dh, x_ref[...], g_ref[...])
        gx_ref[...] = dx1_ref[...] + dres
        dg_ref[...] += dg

    tok = lambda w: pl.BlockSpec((tm, w), lambda i: (i, 0))
    full = lambda *s: pl.BlockSpec(s, lambda i: (0,) * len(s))
    return pl.pallas_call(
        body, name="in_proj_bwd", grid=(seq // tm,),
        in_specs=[tok(3 * GROUP_W)] * 3 + [tok(REST_W)] + [full(D_MODEL, 3 * GROUP_W)] * 3 + [full(D_MODEL, REST_W)]
        + [tok(D_MODEL), tok(D_MODEL), full(1, D_MODEL)],
        out_specs=[tok(D_MODEL), full(1, D_MODEL)],
        out_shape=[jax.ShapeDtypeStruct((seq, D_MODEL), F32), jax.ShapeDtypeStruct((1, D_MODEL), F32)],
        compiler_params=_params(("arbitrary",), 48),
    )(*dqkv, drest, *w_qkv, w_rest, x, dx1, g0)


def _adamw(w, g, m, v, name):
    rows, cols = w.shape
    tr = 256 if rows % 256 == 0 else rows
    c1 = 1.0 - ADAM_B1 ** ADAM_STEP
    c2 = 1.0 - ADAM_B2 ** ADAM_STEP

    def body(w_ref, g_ref, m_ref, v_ref, d_ref, nm_ref, nv_ref):
        gv = g_ref[...]
        nm = ADAM_B1 * m_ref[...] + (1.0 - ADAM_B1) * gv
        nv = ADAM_B2 * v_ref[...] + (1.0 - ADAM_B2) * (gv * gv)
        d_ref[...] = -ADAM_LR * ((nm / c1) / (jnp.sqrt(nv / c2) + ADAM_EPS) + ADAM_WD * w_ref[...])
        nm_ref[...] = nm
        nv_ref[...] = nv

    spec = pl.BlockSpec((tr, cols), lambda i: (i, 0))
    return pl.pallas_call(
        body, name=name, grid=(rows // tr,),
        in_specs=[spec] * 4, out_specs=[spec] * 3,
        out_shape=[jax.ShapeDtypeStruct((rows, cols), F32)] * 3,
        compiler_params=_params(("arbitrary",), 40),
    )(w, g, m, v)


def _place():
    x, y, c = lax.axis_index("x"), lax.axis_index("y"), lax.axis_index("c")
    chips = [(1 - x, y), (x, 1 - y), (1 - x, 1 - y)]
    return x, y, c, chips


def _gather_weights(shards):
    n = len(shards)
    halves = [s.shape[0] // 2 for s in shards]

    def body(*refs):
        ins, outs, stages = refs[:n], refs[n:2 * n], refs[2 * n:3 * n]
        send_sems, recv_sems, local_sems = refs[3 * n:]
        x, y, c, chips = _place()
        mine = 2 * x + y
        sibling = (x, y, 1 - c)
        for t in range(n):
            stages[t][...] = ins[t][...].astype(BF16)
        local = [pltpu.make_async_copy(stages[t], outs[t].at[mine], local_sems.at[t]) for t in range(n)]
        for cp in local:
            cp.start()

        def half_of(t, chip, which):
            return outs[t].at[chip, pl.ds(which * halves[t], halves[t]), :]

        def from_stage(t, j, to):
            return pltpu.make_async_remote_copy(
                src_ref=stages[t].at[pl.ds(c * halves[t], halves[t]), :], dst_ref=half_of(t, mine, c),
                send_sem=send_sems.at[t, j], recv_sem=recv_sems.at[t, j], device_id=to, device_id_type=MESH)

        def passed_on(t, j, chip, which, to):
            return pltpu.make_async_remote_copy(
                src_ref=half_of(t, chip, which), dst_ref=half_of(t, chip, which),
                send_sem=send_sems.at[t, 3 + j], recv_sem=recv_sems.at[t, 3 + j], device_id=to, device_id_type=MESH)

        sends = []
        for t in range(n):
            for j, (px, py) in enumerate(chips):
                cp = from_stage(t, j, (px, py, c))
                cp.start()
                sends.append(cp)
        for j, (px, py) in enumerate(chips):
            chip = 2 * px + py
            for t in range(n):
                pltpu.make_async_remote_copy(
                    src_ref=stages[t].at[pl.ds(c * halves[t], halves[t]), :], dst_ref=half_of(t, chip, c),
                    send_sem=send_sems.at[t, j], recv_sem=recv_sems.at[t, j], device_id=(px, py, c),
                    device_id_type=MESH).wait_recv()
                cp = passed_on(t, j, chip, c, sibling)
                cp.start()
                sends.append(cp)
        for j, (px, py) in enumerate(chips):
            for t in range(n):
                passed_on(t, j, 2 * px + py, 1 - c, sibling).wait_recv()
        for cp in sends:
            cp.wait_send()
        for cp in local:
            cp.wait()

    stage_bytes = sum(s.size * 6 for s in shards)
    return pl.pallas_call(
        body, name="gather_weights",
        in_specs=[VMEM_SPEC] * n, out_specs=[HBM_SPEC] * n,
        out_shape=[jax.ShapeDtypeStruct((N_CHIPS,) + s.shape, BF16) for s in shards],
        scratch_shapes=[pltpu.VMEM(s.shape, BF16) for s in shards]
        + [pltpu.SemaphoreType.DMA((n, 6)), pltpu.SemaphoreType.DMA((n, 6)), pltpu.SemaphoreType.DMA((n,))],
        compiler_params=pltpu.CompilerParams(vmem_limit_bytes=stage_bytes + 8 * MIB),
    )(*shards)


def _pair_exchange(g4):
    _, rows, lanes = g4.shape
    rh = rows // 2

    def body(g_ref, a_ref, send_sem, recv_sem):
        x, y, c, _ = _place()
        cp = pltpu.make_async_remote_copy(
            src_ref=g_ref.at[:, pl.ds((1 - c) * rh, rh), :], dst_ref=a_ref, send_sem=send_sem, recv_sem=recv_sem,
            device_id=(x, y, 1 - c), device_id_type=MESH)
        cp.start()
        cp.wait()

    return pl.pallas_call(
        body, name="grad_pair_exchange", in_specs=[HBM_SPEC], out_specs=HBM_SPEC,
        out_shape=jax.ShapeDtypeStruct((N_CHIPS, rh, lanes), F32),
        scratch_shapes=[pltpu.SemaphoreType.DMA, pltpu.SemaphoreType.DMA],
    )(g4)


def _pair_add(g4, a4, core):
    _, rows, lanes = g4.shape
    rh = rows // 2
    tr = rh // 4
    nb = rh // tr

    def body(c_ref, g_ref, a_ref, o_ref):
        o_ref[...] = g_ref[...] + a_ref[...]

    blk = (1, tr, lanes)
    return pl.pallas_call(
        body, name="grad_pair_add",
        grid_spec=pltpu.PrefetchScalarGridSpec(
            num_scalar_prefetch=1, grid=(N_CHIPS, nb),
            in_specs=[pl.BlockSpec(blk, lambda j, i, cr: (j, cr[0] * nb + i, 0)), pl.BlockSpec(blk, lambda j, i, cr: (j, i, 0))],
            out_specs=pl.BlockSpec(blk, lambda j, i, cr: (j, i, 0))),
        out_shape=jax.ShapeDtypeStruct((N_CHIPS, rh, lanes), F32),
        compiler_params=_params(("arbitrary", "arbitrary"), 32),
    )(core, g4, a4)


def _chip_exchange(s4):
    _, rh, lanes = s4.shape

    def body(s_ref, b_ref, send_sems, recv_sems):
        x, y, c, chips = _place()
        cps = []
        for j, (px, py) in enumerate(chips):
            cp = pltpu.make_async_remote_copy(
                src_ref=s_ref.at[2 * px + py], dst_ref=b_ref.at[j], send_sem=send_sems.at[j], recv_sem=recv_sems.at[j],
                device_id=(px, py, c), device_id_type=MESH)
            cp.start()
            cps.append(cp)
        for cp in cps:
            cp.wait()

    return pl.pallas_call(
        body, name="grad_chip_exchange", in_specs=[HBM_SPEC], out_specs=HBM_SPEC,
        out_shape=jax.ShapeDtypeStruct((3, rh, lanes), F32),
        scratch_shapes=[pltpu.SemaphoreType.DMA((3,)), pltpu.SemaphoreType.DMA((3,))],
    )(s4)


def _chip_add(s4, b3, chip):
    _, rh, lanes = s4.shape
    tr = rh // 4

    def body(m_ref, s_ref, b0, b1, b2, o_ref):
        o_ref[...] = ((s_ref[0] + b0[0]) + b1[0]) + b2[0]

    blk = (1, tr, lanes)
    return pl.pallas_call(
        body, name="grad_chip_add",
        grid_spec=pltpu.PrefetchScalarGridSpec(
            num_scalar_prefetch=1, grid=(rh // tr,),
            in_specs=[pl.BlockSpec(blk, lambda i, mr: (mr[0], i, 0)), pl.BlockSpec(blk, lambda i, mr: (0, i, 0)),
                      pl.BlockSpec(blk, lambda i, mr: (1, i, 0)), pl.BlockSpec(blk, lambda i, mr: (2, i, 0))],
            out_specs=pl.BlockSpec((tr, lanes), lambda i, mr: (i, 0))),
        out_shape=jax.ShapeDtypeStruct((rh, lanes), F32),
        compiler_params=_params(("arbitrary",), 32),
    )(chip, s4, b3, b3, b3)


def _pair_share(f):
    rh, lanes = f.shape

    def body(f_ref, o_ref, send_sem, recv_sem, local_sem):
        x, y, c, _ = _place()
        local = pltpu.make_async_copy(f_ref, o_ref.at[pl.ds(c * rh, rh), :], local_sem)
        local.start()
        cp = pltpu.make_async_remote_copy(
            src_ref=f_ref, dst_ref=o_ref.at[pl.ds(c * rh, rh), :], send_sem=send_sem, recv_sem=recv_sem,
            device_id=(x, y, 1 - c), device_id_type=MESH)
        cp.start()
        cp.wait_send()
        pltpu.make_async_remote_copy(
            src_ref=f_ref, dst_ref=o_ref.at[pl.ds((1 - c) * rh, rh), :], send_sem=send_sem, recv_sem=recv_sem,
            device_id=(x, y, 1 - c), device_id_type=MESH).wait_recv()
        local.wait()

    return pl.pallas_call(
        body, name="grad_pair_share", in_specs=[HBM_SPEC], out_specs=HBM_SPEC,
        out_shape=jax.ShapeDtypeStruct((2 * rh, lanes), F32),
        scratch_shapes=[pltpu.SemaphoreType.DMA, pltpu.SemaphoreType.DMA, pltpu.SemaphoreType.DMA],
    )(f)


def _all_reduce_small(p):
    rows, lanes = p.shape
    flips = [(fx, fy, fc) for fx in (0, 1) for fy in (0, 1) for fc in (0, 1)][1:]

    def body(p_ref, o_ref, buf, send_sems, recv_sems):
        x, y, c, _ = _place()
        me = 4 * x + 2 * y + c
        buf[me] = p_ref[...]
        peers = [((1 - x) if fx else x, (1 - y) if fy else y, (1 - c) if fc else c) for fx, fy, fc in flips]
        cps = []
        for k, peer in enumerate(peers):
            cp = pltpu.make_async_remote_copy(
                src_ref=p_ref, dst_ref=buf.at[me], send_sem=send_sems.at[k], recv_sem=recv_sems.at[k],
                device_id=peer, device_id_type=MESH)
            cp.start()
            cps.append(cp)
        for k, (px, py, pc) in enumerate(peers):
            pltpu.make_async_remote_copy(
                src_ref=p_ref, dst_ref=buf.at[4 * px + 2 * py + pc], send_sem=send_sems.at[k], recv_sem=recv_sems.at[k],
                device_id=(px, py, pc), device_id_type=MESH).wait_recv()
        for cp in cps:
            cp.wait_send()
        acc = buf[0]
        for s in range(1, 8):
            acc = acc + buf[s]
        o_ref[...] = acc

    return pl.pallas_call(
        body, name="small_all_reduce", in_specs=[VMEM_SPEC], out_specs=VMEM_SPEC,
        out_shape=jax.ShapeDtypeStruct((rows, lanes), F32),
        scratch_shapes=[pltpu.VMEM((8, rows, lanes), F32), pltpu.SemaphoreType.DMA((7,)), pltpu.SemaphoreType.DMA((7,))],
        compiler_params=pltpu.CompilerParams(vmem_limit_bytes=32 * MIB),
    )(p)


BIG = ("w_in", "w_branch_attn", "w_branch_gmlp", "w_out", "w_mlp_in", "w_mlp_out")
COLUMN_SHARDED = ("w_in", "w_branch_attn", "w_branch_gmlp", "w_mlp_in")
SMALL = ("norm_pre_mix", "w_spatial", "b_spatial", "ln_v_gain", "ln_v_bias", "norm_post_mix", "norm_pre_mlp", "norm_post_mlp")
ORDER = ("norm_pre_mix", "w_in", "w_spatial", "b_spatial", "ln_v_gain", "ln_v_bias", "w_branch_attn", "w_branch_gmlp",
         "w_out", "norm_post_mix", "norm_pre_mlp", "w_mlp_in", "w_mlp_out", "norm_post_mlp")


def _full_weight(name, gathered):
    if name in COLUMN_SHARDED:
        return jnp.transpose(gathered, (1, 0, 2)).reshape(gathered.shape[1], -1)
    return gathered.reshape(-1, gathered.shape[2])


def _shard_of(name, full, j):
    if name in COLUMN_SHARDED:
        w = full.shape[1] // N_CHIPS
        return full[:, j * w:(j + 1) * w]
    r = full.shape[0] // N_CHIPS
    return full[j * r:(j + 1) * r]


def _rows8(a):
    a = a.reshape(-1, 128)
    pad = (-a.shape[0]) % 8
    return jnp.pad(a, ((0, pad), (0, 0))) if pad else a


def _qkv_columns(group):
    return [(sec * ATTN_W + group * GROUP_W, sec * ATTN_W + (group + 1) * GROUP_W) for sec in range(3)]


def _local_step(x, target, small, full):
    seq = x.shape[0]
    cos_t, sin_t = _rope_tables(seq)
    g0, g1, g2, g3 = small["norm_pre_mix"], small["norm_post_mix"], small["norm_pre_mlp"], small["norm_post_mlp"]
    w_sp = small["w_spatial"]
    b_col = small["b_spatial"].reshape(GMLP_GROUPS, CHUNK, 1)
    ln_g, ln_b = small["ln_v_gain"], small["ln_v_bias"]
    w_in = full["w_in"]

    h, qkv, rest = _in_proj(x, g0, w_in, cos_t, sin_t)
    o_l = []
    for g, dil in enumerate(DILATIONS):
        o_l.extend(_attn_fwd(qkv, g, dil))
    ya, lse, yg, mg, y, x1 = _mix_fwd(o_l, rest, x, w_sp, b_col, ln_g, ln_b, full["w_branch_attn"], full["w_branch_gmlp"],
                                       full["w_out"], g1)
    h2, a, dy2, dout, loss8, dg3 = _mlp_fwd(x1, g2, g3, full["w_mlp_in"], full["w_mlp_out"], target)
    dap, dx1, dy, dg2, dg1 = _mlp_bwd(dy2, a, full["w_mlp_out"], full["w_mlp_in"], dout, x1, y, g2, g1)
    d_wmo = _tn_matmul(a, dy2, "grad_w_mlp_out", 1024, 1024, square_a=True)
    d_wmi = _tn_matmul(h2, dap, "grad_w_mlp_in", 1024, 1024)
    dya, drest, d_wout, d_wba, d_wbg, d_wsp, d_bb, d_lg, d_lb = _mix_bwd(
        dy, ya, yg, mg, rest, full["w_out"], full["w_branch_attn"], full["w_branch_gmlp"], w_sp, b_col, ln_g, ln_b)
    dqkv = [_attn_bwd(qkv, dya, ya, lse, cos_t, sin_t, g, dil) for g, dil in enumerate(DILATIONS)]
    w_qkv = [jnp.concatenate([w_in[:, lo:hi] for lo, hi in _qkv_columns(g)], axis=1) for g in range(N_GROUPS)]
    w_rest = w_in[:, QKV_W:]
    grad_x, dg0 = _in_proj_bwd(dqkv, drest, w_qkv, w_rest, x, dx1, g0)
    d_qkv = [_tn_matmul(h, dqkv[g], f"grad_w_in_qkv{g}", 1024, 3 * GROUP_W) for g in range(N_GROUPS)]
    d_rest = _tn_matmul(h, drest, "grad_w_in_rest", 1024, 1024)
    d_win = jnp.concatenate([d_qkv[g][:, s * GROUP_W:(s + 1) * GROUP_W] for s in range(3) for g in range(N_GROUPS)]
                            + [d_rest], axis=1)
    grads = {"norm_pre_mix": dg0, "w_in": d_win, "w_spatial": d_wsp, "b_spatial": d_bb[:, :, 0], "ln_v_gain": d_lg,
             "ln_v_bias": d_lb, "w_branch_attn": d_wba, "w_branch_gmlp": d_wbg, "w_out": d_wout, "norm_post_mix": dg1,
             "norm_pre_mlp": dg2, "w_mlp_in": d_wmi, "w_mlp_out": d_wmo, "norm_post_mlp": dg3}
    return loss8[0, 0], grad_x, grads


def kernel(x, norm_pre_mix, w_in, w_spatial, b_spatial, ln_v_gain, ln_v_bias, w_branch_attn, w_branch_gmlp, w_out, norm_post_mix, norm_pre_mlp, w_mlp_in, w_mlp_out, norm_post_mlp, loss_target, m_norm_pre_mix, m_w_in, m_w_spatial, m_b_spatial, m_ln_v_gain, m_ln_v_bias, m_w_branch_attn, m_w_branch_gmlp, m_w_out, m_norm_post_mix, m_norm_pre_mlp, m_w_mlp_in, m_w_mlp_out, m_norm_post_mlp, v_norm_pre_mix, v_w_in, v_w_spatial, v_b_spatial, v_ln_v_gain, v_ln_v_bias, v_w_branch_attn, v_w_branch_gmlp, v_w_out, v_norm_post_mix, v_norm_pre_mlp, v_w_mlp_in, v_w_mlp_out, v_norm_post_mlp):
    given = dict(norm_pre_mix=norm_pre_mix, w_in=w_in, w_spatial=w_spatial, b_spatial=b_spatial, ln_v_gain=ln_v_gain,
                 ln_v_bias=ln_v_bias, w_branch_attn=w_branch_attn, w_branch_gmlp=w_branch_gmlp, w_out=w_out,
                 norm_post_mix=norm_post_mix, norm_pre_mlp=norm_pre_mlp, w_mlp_in=w_mlp_in, w_mlp_out=w_mlp_out,
                 norm_post_mlp=norm_post_mlp)
    moments_m = dict(norm_pre_mix=m_norm_pre_mix, w_in=m_w_in, w_spatial=m_w_spatial, b_spatial=m_b_spatial,
                     ln_v_gain=m_ln_v_gain, ln_v_bias=m_ln_v_bias, w_branch_attn=m_w_branch_attn,
                     w_branch_gmlp=m_w_branch_gmlp, w_out=m_w_out, norm_post_mix=m_norm_post_mix,
                     norm_pre_mlp=m_norm_pre_mlp, w_mlp_in=m_w_mlp_in, w_mlp_out=m_w_mlp_out, norm_post_mlp=m_norm_post_mlp)
    moments_v = dict(norm_pre_mix=v_norm_pre_mix, w_in=v_w_in, w_spatial=v_w_spatial, b_spatial=v_b_spatial,
                     ln_v_gain=v_ln_v_gain, ln_v_bias=v_ln_v_bias, w_branch_attn=v_w_branch_attn,
                     w_branch_gmlp=v_w_branch_gmlp, w_out=v_w_out, norm_post_mix=v_norm_post_mix,
                     norm_pre_mlp=v_norm_pre_mlp, w_mlp_in=v_w_mlp_in, w_mlp_out=v_w_mlp_out, norm_post_mlp=v_norm_post_mlp)
    cx, cy, cc = lax.axis_index("x"), lax.axis_index("y"), lax.axis_index("c")

    shards = [given[n][0] for n in BIG]
    gathered = _gather_weights(shards)
    full = {n: _full_weight(n, gw) for n, gw in zip(BIG, gathered)}
    small = {n: given[n][0] if given[n].ndim > 2 else given[n] for n in SMALL}

    loss, grad_x, grads = _local_step(x[0], loss_target[0], small, full)
    loss = lax.psum(loss, ("x", "y", "c"))

    g4 = jnp.stack([jnp.concatenate([_shard_of(n, grads[n], j).reshape(-1, 128) for n in BIG], axis=0)
                    for j in range(N_CHIPS)], axis=0)
    core = cc.astype(jnp.int32).reshape(1)
    chip = (2 * cx + cy).astype(jnp.int32).reshape(1)
    s4 = _pair_add(g4, _pair_exchange(g4), core)
    reduced = _pair_share(_chip_add(s4, _chip_exchange(s4), chip))
    grad_shard, row = {}, 0
    for n, s in zip(BIG, shards):
        cnt = s.size // 128
        grad_shard[n] = reduced[row:row + cnt].reshape(s.shape)
        row += cnt

    packed = jnp.concatenate([_rows8(grads[n]) for n in SMALL], axis=0)
    summed = _all_reduce_small(packed)
    row = 0
    for n in SMALL:
        shape = given[n][0].shape
        cnt = -(-(given[n][0].size // 128) // 8) * 8
        grad_shard[n] = summed[row:row + given[n][0].size // 128].reshape(shape)
        row += cnt

    deltas, new_m, new_v = {}, {}, {}
    for n in ORDER:
        shape = given[n].shape
        two_d = (-1, shape[-1])
        d, nm, nv = _adamw(given[n].reshape(two_d), grad_shard[n].reshape(two_d), moments_m[n].reshape(two_d),
                           moments_v[n].reshape(two_d), "adamw_" + n)
        deltas[n], new_m[n], new_v[n] = d.reshape(shape), nm.reshape(shape), nv.reshape(shape)
    grad_out = [grad_shard[n].reshape(given[n].shape) for n in ORDER]
    return (loss, grad_x[None], *grad_out, *[deltas[n] for n in ORDER], *[new_m[n] for n in ORDER],
            *[new_v[n] for n in ORDER])
```

```python
import math

import jax
import jax.numpy as jnp
from jax import lax
from jax.experimental import pallas as pl
from jax.experimental.pallas import tpu as pltpu

F32 = jnp.float32
BF16 = jnp.bfloat16
MESH = pl.DeviceIdType.MESH

D_MODEL = 1024
HEAD_DIM = 64
HEADS_PER_GROUP = 4
GROUP_W = HEADS_PER_GROUP * HEAD_DIM
DILATIONS = (1, 4, 16)
N_GROUPS = len(DILATIONS)
ATTN_W = N_GROUPS * GROUP_W
QKV_W = 3 * ATTN_W
GMLP_W = 512
GMLP_GROUPS = 4
CHUNK = 128
REST_W = 2 * GMLP_W + 2 * D_MODEL
IN_W = QKV_W + REST_W
D_FF = 4096
QBLK = 128
ROPE_THETA = 10000.0
EPS = 1e-6
NEG = -1e30
SCALE = HEAD_DIM ** -0.5
N_CHIPS = 4

ADAM_LR = 0.001
ADAM_B1 = 0.9
ADAM_B2 = 0.999
ADAM_EPS = 1e-08
ADAM_WD = 0.01
ADAM_STEP = 10

MIB = 1024 * 1024
HBM_SPEC = pl.BlockSpec(memory_space=pltpu.HBM)
VMEM_SPEC = pl.BlockSpec(memory_space=pltpu.VMEM)


def _params(semantics, vmem_mib):
    return pltpu.CompilerParams(dimension_semantics=semantics, vmem_limit_bytes=vmem_mib * MIB)


def _dot(a, b):
    return jnp.dot(a, b, preferred_element_type=F32)


def _dot_nt(a, b):
    return lax.dot_general(a, b, (((1,), (1,)), ((), ())), preferred_element_type=F32)


def _dot_tn(a, b):
    return lax.dot_general(a, b, (((0,), (0,)), ((), ())), preferred_element_type=F32)


_GELU_C = math.sqrt(2.0 / math.pi)


def _gelu(x):
    return x * (0.5 * (1.0 + jnp.tanh(_GELU_C * (x + 0.044715 * (x * x * x)))))


def _gelu_grad(x):
    t = jnp.tanh(_GELU_C * (x + 0.044715 * (x * x * x)))
    return 0.5 * (1.0 + t) + 0.5 * x * (1.0 - t * t) * (_GELU_C * (1.0 + 3.0 * 0.044715 * (x * x)))


def _rsqrt_ms(v):
    return lax.rsqrt(jnp.mean(v * v, axis=-1, keepdims=True) + EPS)


def _rmsnorm_bwd(dn, src, gain):
    r = _rsqrt_ms(src)
    t = gain * dn
    dgain = jnp.sum(dn * (src * r), axis=0, keepdims=True)
    dsrc = r * t - src * ((r * r * r) * jnp.mean(t * src, axis=-1, keepdims=True))
    return dsrc, dgain


def _rot_half(v):
    w = v.shape[-1]
    lane = lax.broadcasted_iota(jnp.int32, v.shape, v.ndim - 1)
    return jnp.where((lane % HEAD_DIM) < HEAD_DIM // 2, pltpu.roll(v, w - HEAD_DIM // 2, v.ndim - 1),
                     pltpu.roll(v, HEAD_DIM // 2, v.ndim - 1))


def _head_masks(shape):
    lane = lax.broadcasted_iota(jnp.int32, shape, 1)
    return [(lane >= h * HEAD_DIM) & (lane < (h + 1) * HEAD_DIM) for h in range(HEADS_PER_GROUP)]


LANES = 128


def _put_residue(slab, val, out_ref, dil, width, col0):
    tm, w = val.shape
    if dil == 1:
        out_ref[:, col0:col0 + w] = val.astype(out_ref.dtype)
        return
    for k in range(w // LANES):
        slab[k] = val[:, k * LANES:(k + 1) * LANES]
    for r in range(dil):
        for k in range(w // LANES):
            c = r * width + col0 + k * LANES
            out_ref[:, c:c + LANES] = slab[k, pl.ds(r, tm // dil, stride=dil), :].astype(out_ref.dtype)


def _get_tokens(slab, in_ref, dil, width, col0, w):
    if dil == 1:
        return in_ref[:, col0:col0 + w].astype(F32)
    rows = in_ref.shape[0]
    for r in range(dil):
        for k in range(w // LANES):
            c = r * width + col0 + k * LANES
            slab[k, pl.ds(r, rows, stride=dil), :] = in_ref[:, c:c + LANES].astype(F32)
    return jnp.concatenate([slab[k] for k in range(w // LANES)], axis=1)


def _rope_tables(seq):
    half = HEAD_DIM // 2
    inv_freq = ROPE_THETA ** (-jnp.arange(half, dtype=F32) / half)
    freq = jnp.tile(inv_freq, LANES // half).reshape(1, LANES)
    tm = 512

    def body(f_ref, *refs):
        outs, slab_c, slab_s = refs[:-2], refs[-2], refs[-1]
        row = lax.broadcasted_iota(jnp.int32, (tm, LANES), 0) + pl.program_id(0) * tm
        lane = lax.broadcasted_iota(jnp.int32, (tm, LANES), 1)
        ang = row.astype(F32) * f_ref[...]
        cos = jnp.cos(ang)
        sin = jnp.where((lane % HEAD_DIM) < half, -jnp.sin(ang), jnp.sin(ang))
        slab_c[0] = cos
        slab_s[0] = sin
        for i, dil in enumerate(DILATIONS):
            for tab, slab in ((outs[2 * i], slab_c), (outs[2 * i + 1], slab_s)):
                for r in range(dil):
                    piece = slab[0, pl.ds(r, tm // dil, stride=dil), :] if dil > 1 else slab[0]
                    for k in range(GROUP_W // LANES):
                        tab[:, r * GROUP_W + k * LANES:r * GROUP_W + (k + 1) * LANES] = piece

    outs = pl.pallas_call(
        body, name="rope_tables", grid=(seq // tm,),
        in_specs=[pl.BlockSpec((1, LANES), lambda i: (0, 0))],
        out_specs=[pl.BlockSpec((tm // d, d * GROUP_W), lambda i: (i, 0)) for d in DILATIONS for _ in range(2)],
        out_shape=[jax.ShapeDtypeStruct((seq // d, d * GROUP_W), F32) for d in DILATIONS for _ in range(2)],
        scratch_shapes=[pltpu.VMEM((1, tm, LANES), F32)] * 2,
        compiler_params=_params(("arbitrary",), 32),
    )(freq)
    return {d: (outs[2 * i], outs[2 * i + 1]) for i, d in enumerate(DILATIONS)}


def _in_proj(x, g0, w_in, cos_t, sin_t):
    seq = x.shape[0]
    tm, tn = 256, GROUP_W
    n_qk = 2 * ATTN_W // tn
    n_qkv = QKV_W // tn

    def body(x_ref, g_ref, w_ref, cos_ref, sin_ref, *refs):
        h_refs, qkv_refs, rest_ref, slab = refs[:N_GROUPS], refs[N_GROUPS:2 * N_GROUPS], refs[2 * N_GROUPS], refs[-1]
        xv = x_ref[...]
        hf = (xv * _rsqrt_ms(xv)) * g_ref[...]
        hb = hf.astype(BF16)
        for g, dil in enumerate(DILATIONS):
            _put_residue(slab, hf, h_refs[g], dil, D_MODEL, 0)
        cos, sin = cos_ref[...], sin_ref[...]
        for j in range(IN_W // tn):
            p = _dot(hb, w_ref[:, j * tn:(j + 1) * tn])
            if j < n_qkv:
                if j < n_qk:
                    p = p * cos + _rot_half(p) * sin
                section, g = divmod(j, N_GROUPS)
                _put_residue(slab, p, qkv_refs[g], DILATIONS[g], 3 * GROUP_W, section * GROUP_W)
            else:
                rest_ref[:, (j - n_qkv) * tn:(j - n_qkv + 1) * tn] = p

    return pl.pallas_call(
        body, name="in_proj", grid=(seq // tm,),
        in_specs=[pl.BlockSpec((tm, D_MODEL), lambda i: (i, 0)),
                  pl.BlockSpec((1, D_MODEL), lambda i: (0, 0)),
                  pl.BlockSpec((D_MODEL, IN_W), lambda i: (0, 0)),
                  pl.BlockSpec((tm, GROUP_W), lambda i: (i, 0)),
                  pl.BlockSpec((tm, GROUP_W), lambda i: (i, 0))],
        out_specs=[pl.BlockSpec((tm // d, d * D_MODEL), lambda i: (i, 0)) for d in DILATIONS]
        + [pl.BlockSpec((tm // d, d * 3 * GROUP_W), lambda i: (i, 0)) for d in DILATIONS]
        + [pl.BlockSpec((tm, REST_W), lambda i: (i, 0))],
        out_shape=[jax.ShapeDtypeStruct((seq // d, d * D_MODEL), BF16) for d in DILATIONS]
        + [jax.ShapeDtypeStruct((seq // d, d * 3 * GROUP_W), BF16) for d in DILATIONS]
        + [jax.ShapeDtypeStruct((seq, REST_W), F32)],
        scratch_shapes=[pltpu.VMEM((D_MODEL // LANES, tm, LANES), F32)],
        compiler_params=_params(("arbitrary",), 56),
    )(x, g0, w_in, cos_t, sin_t)


def _band_masks():
    qi = lax.broadcasted_iota(jnp.int32, (QBLK, QBLK), 0)
    kj = lax.broadcasted_iota(jnp.int32, (QBLK, QBLK), 1)
    return kj <= qi, kj >= qi


def _attn_tile(length):
    return min(512, length)


def _attn_fwd(qkv, dil):
    length = qkv.shape[0]
    tq = _attn_tile(length)
    nsub = tq // QBLK
    nblk = length // tq

    def body(q_ref, k_ref, v_ref, kp_ref, vp_ref, o_ref, l_ref):
        n = pl.program_id(1)
        mask_c, mask_p0 = _band_masks()
        hmask = _head_masks((QBLK, GROUP_W))
        zero = jnp.zeros((), BF16)
        for b in range(nsub):
            rows = slice(b * QBLK, (b + 1) * QBLK)
            q = q_ref[rows, :]
            kc, vc = k_ref[rows, :], v_ref[rows, :]
            if b == 0:
                kp, vp = kp_ref[...], vp_ref[...]
                mask_p = mask_p0 & (n > 0)
            else:
                prow = slice((b - 1) * QBLK, b * QBLK)
                kp, vp = k_ref[prow, :], v_ref[prow, :]
                mask_p = mask_p0
            o_acc = jnp.zeros((QBLK, GROUP_W), F32)
            l_acc = jnp.zeros((QBLK, GROUP_W), F32)
            for h in range(HEADS_PER_GROUP):
                hm = hmask[h]
                sc = jnp.where(mask_c, _dot_nt(q, jnp.where(hm, kc, zero)) * SCALE, NEG)
                sp = jnp.where(mask_p, _dot_nt(q, jnp.where(hm, kp, zero)) * SCALE, NEG)
                m = jnp.maximum(jnp.max(sc, axis=-1, keepdims=True), jnp.max(sp, axis=-1, keepdims=True))
                pc, pp = jnp.exp(sc - m), jnp.exp(sp - m)
                den = jnp.sum(pc, axis=-1, keepdims=True) + jnp.sum(pp, axis=-1, keepdims=True)
                pv = _dot(pc.astype(BF16), jnp.where(hm, vc, zero)) + _dot(pp.astype(BF16), jnp.where(hm, vp, zero))
                o_acc = o_acc + pv / den
                l_acc = l_acc + jnp.where(hm, m + jnp.log(den), 0.0)
            o_ref[rows, :] = o_acc
            l_ref[rows, :] = l_acc

    cur = lambda sec: pl.BlockSpec((tq, GROUP_W), lambda r, n: (n, r * 3 + sec))
    prev = lambda sec: pl.BlockSpec((QBLK, GROUP_W), lambda r, n: (jnp.maximum(n * nsub - 1, 0), r * 3 + sec))
    return pl.pallas_call(
        body, name=f"attn_fwd_d{dil}", grid=(dil, nblk),
        in_specs=[cur(0), cur(1), cur(2), prev(1), prev(2)],
        out_specs=[pl.BlockSpec((tq, GROUP_W), lambda r, n: (n, r))] * 2,
        out_shape=[jax.ShapeDtypeStruct((length, dil * GROUP_W), F32)] * 2,
        compiler_params=_params(("arbitrary", "arbitrary"), 32),
    )(qkv, qkv, qkv, qkv, qkv)


def _attn_bwd(qkv, dy, y, lse, cos_t, sin_t, dil):
    length = qkv.shape[0]
    tq = _attn_tile(length)
    nsub = tq // QBLK
    nblk = length // tq

    def body(q_ref, k_ref, v_ref, kp_ref, vp_ref, qn_ref, dy_ref, y_ref, l_ref, dyn_ref, yn_ref, ln_ref,
             cos_ref, sin_ref, out_ref, dq_s, dk_s, dv_s):
        n = pl.program_id(1)
        mask_c, mask_p0 = _band_masks()
        hmask = _head_masks((QBLK, GROUP_W))
        zero = jnp.zeros((), BF16)

        def head_terms(h, q, dyv, yv, lv, kk, vv, mask):
            hm = hmask[h]
            delta = jnp.sum(jnp.where(hm, dyv * yv, 0.0), axis=-1, keepdims=True)
            lh = jnp.max(jnp.where(hm, lv, NEG), axis=-1, keepdims=True)
            s = _dot_nt(q, jnp.where(hm, kk, zero)) * SCALE
            p = jnp.exp(jnp.where(mask, s - lh, NEG))
            dyb = jnp.where(hm, dyv, 0.0).astype(BF16)
            dp = _dot_nt(dyb, vv)
            ds = p * (dp - delta)
            return p.astype(BF16), ds.astype(BF16), dyb, jnp.where(hm, q, zero)

        dk_s[...] = jnp.zeros(dk_s.shape, F32)
        dv_s[...] = jnp.zeros(dv_s.shape, F32)
        for b in range(nsub):
            rows = slice(b * QBLK, (b + 1) * QBLK)
            q, dyv, yv, lv = q_ref[rows, :], dy_ref[rows, :], y_ref[rows, :], l_ref[rows, :]
            kc, vc = k_ref[rows, :], v_ref[rows, :]
            if b == 0:
                kp, vp = kp_ref[...], vp_ref[...]
                mask_p = mask_p0 & (n > 0)
            else:
                prow = slice((b - 1) * QBLK, b * QBLK)
                kp, vp = k_ref[prow, :], v_ref[prow, :]
                mask_p = mask_p0
            dq = jnp.zeros((QBLK, GROUP_W), F32)
            for h in range(HEADS_PER_GROUP):
                hm = hmask[h]
                pc, dsc, dyb, qh = head_terms(h, q, dyv, yv, lv, kc, vc, mask_c)
                pp, dsp, _, _ = head_terms(h, q, dyv, yv, lv, kp, vp, mask_p)
                dq = dq + (_dot(dsc, jnp.where(hm, kc, zero)) + _dot(dsp, jnp.where(hm, kp, zero))) * SCALE
                dv_s[rows, :] += _dot_tn(pc, dyb)
                dk_s[rows, :] += _dot_tn(dsc, qh) * SCALE
                if b > 0:
                    dv_s[prow, :] += _dot_tn(pp, dyb)
                    dk_s[prow, :] += _dot_tn(dsp, qh) * SCALE
            dq_s[rows, :] = dq
        rows = slice((nsub - 1) * QBLK, nsub * QBLK)
        qn, dyn, yn, ln = qn_ref[...], dyn_ref[...], yn_ref[...], ln_ref[...]
        mask_n = mask_p0 & (n < nblk - 1)
        for h in range(HEADS_PER_GROUP):
            pn, dsn, dyb, qh = head_terms(h, qn, dyn, yn, ln, k_ref[rows, :], v_ref[rows, :], mask_n)
            dv_s[rows, :] += _dot_tn(pn, dyb)
            dk_s[rows, :] += _dot_tn(dsn, qh) * SCALE
        cos, sin = cos_ref[...], sin_ref[...]
        dq, dk = dq_s[...], dk_s[...]
        out_ref[:, 0:GROUP_W] = (dq * cos - _rot_half(dq) * sin).astype(BF16)
        out_ref[:, GROUP_W:2 * GROUP_W] = (dk * cos - _rot_half(dk) * sin).astype(BF16)
        out_ref[:, 2 * GROUP_W:3 * GROUP_W] = dv_s[...].astype(BF16)

    cur = lambda sec: pl.BlockSpec((tq, GROUP_W), lambda r, n: (n, r * 3 + sec))
    prev = lambda sec: pl.BlockSpec((QBLK, GROUP_W), lambda r, n: (jnp.maximum(n * nsub - 1, 0), r * 3 + sec))
    nxt_q = pl.BlockSpec((QBLK, GROUP_W), lambda r, n: (jnp.minimum((n + 1) * nsub, nblk * nsub - 1), r * 3))
    tok = pl.BlockSpec((tq, GROUP_W), lambda r, n: (n, r))
    tok_next = pl.BlockSpec((QBLK, GROUP_W), lambda r, n: (jnp.minimum((n + 1) * nsub, nblk * nsub - 1), r))
    return pl.pallas_call(
        body, name=f"attn_bwd_d{dil}", grid=(dil, nblk),
        in_specs=[cur(0), cur(1), cur(2), prev(1), prev(2), nxt_q,
                  tok, tok, tok, tok_next, tok_next, tok_next, tok, tok],
        out_specs=pl.BlockSpec((tq, 3 * GROUP_W), lambda r, n: (n, r)),
        out_shape=jax.ShapeDtypeStruct((length, dil * 3 * GROUP_W), BF16),
        scratch_shapes=[pltpu.VMEM((tq, GROUP_W), F32)] * 3,
        compiler_params=_params(("arbitrary", "arbitrary"), 32),
    )(qkv, qkv, qkv, qkv, qkv, qkv, dy, y, lse, dy, y, lse, cos_t, sin_t)


def _layernorm_stats(z):
    mu = jnp.mean(z, axis=-1, keepdims=True)
    zc = z - mu
    rstd = lax.rsqrt(jnp.mean(zc * zc, axis=-1, keepdims=True) + EPS)
    return zc * rstd, rstd


def _tril_mask():
    row = lax.broadcasted_iota(jnp.int32, (CHUNK, CHUNK), 0)
    col = lax.broadcasted_iota(jnp.int32, (CHUNK, CHUNK), 1)
    return col <= row


def _mix_fwd(o_l, rest, x, w_sp, b_col, ln_g, ln_b, w_ba, w_bg, w_out, g1):
    seq = x.shape[0]
    tm = 256

    def body(o0, l0, o1, l1, o2, l2, up_ref, zp_ref, gap_ref, gbp_ref, x_ref, wsp_ref, bcol_ref, lg_ref, lb_ref,
             wba_ref, wbg_ref, wout_ref, g1_ref, ya0, lj0, ya1, lj1, ya2, lj2, yg_ref, mg_ref, y_ref, x1_ref, slab):
        outs = [_get_tokens(slab, o, d, GROUP_W, 0, GROUP_W) for o, d in zip((o0, o1, o2), DILATIONS)]
        lses = [_get_tokens(slab, l, d, GROUP_W, 0, GROUP_W) for l, d in zip((l0, l1, l2), DILATIONS)]
        m = jnp.maximum(jnp.maximum(lses[0], lses[1]), lses[2])
        es = [jnp.exp(l - m) for l in lses]
        tot = es[0] + es[1] + es[2]
        ya = (es[0] * outs[0] + es[1] * outs[1] + es[2] * outs[2]) / tot
        lj = m + jnp.log(tot)
        for ya_ref, lj_ref, d in zip((ya0, ya1, ya2), (lj0, lj1, lj2), DILATIONS):
            _put_residue(slab, ya, ya_ref, d, GROUP_W, 0)
            _put_residue(slab, lj, lj_ref, d, GROUP_W, 0)
        zhat, _ = _layernorm_stats(_gelu(zp_ref[...]))
        zln = (zhat * lg_ref[...] + lb_ref[...]).astype(BF16)
        u = _gelu(up_ref[...])
        tril = _tril_mask()
        for g in range(GMLP_GROUPS):
            wm = jnp.where(tril, wsp_ref[g], 0.0).astype(BF16)
            cols = slice(g * CHUNK, (g + 1) * CHUNK)
            for c in range(tm // CHUNK):
                rows = slice(c * CHUNK, (c + 1) * CHUNK)
                sz = _dot(wm, zln[rows, cols]) + bcol_ref[g]
                yg_ref[rows, cols] = (u[rows, cols] * sz).astype(BF16)
        a = _dot(ya.astype(BF16), wba_ref[...])
        bm = _dot(yg_ref[...], wbg_ref[...])
        merged = (jax.nn.sigmoid(gap_ref[...]) * a + jax.nn.sigmoid(gbp_ref[...]) * bm).astype(BF16)
        mg_ref[...] = merged
        yv = _dot(merged, wout_ref[...])
        y_ref[...] = yv
        x1_ref[...] = x_ref[...] + (yv * _rsqrt_ms(yv)) * g1_ref[...]

    tok = lambda w: pl.BlockSpec((tm, w), lambda i: (i, 0))
    res = lambda d: pl.BlockSpec((tm // d, d * GROUP_W), lambda i: (i, 0))
    full = lambda *s: pl.BlockSpec(s, lambda i: (0,) * len(s))
    res_specs = [res(d) for d in DILATIONS for _ in range(2)]
    return pl.pallas_call(
        body, name="mix_fwd", grid=(seq // tm,),
        in_specs=res_specs + [
            pl.BlockSpec((tm, GMLP_W), lambda i: (i, 0)), pl.BlockSpec((tm, GMLP_W), lambda i: (i, 1)),
            pl.BlockSpec((tm, D_MODEL), lambda i: (i, 1)), pl.BlockSpec((tm, D_MODEL), lambda i: (i, 2)),
            tok(D_MODEL), full(GMLP_GROUPS, CHUNK, CHUNK), full(GMLP_GROUPS, CHUNK, 1), full(1, GMLP_W), full(1, GMLP_W),
            full(GROUP_W, D_MODEL), full(GMLP_W, D_MODEL), full(D_MODEL, D_MODEL), full(1, D_MODEL)],
        out_specs=res_specs + [tok(GMLP_W), tok(D_MODEL), tok(D_MODEL), tok(D_MODEL)],
        out_shape=[jax.ShapeDtypeStruct((seq // d, d * GROUP_W), F32) for d in DILATIONS for _ in range(2)]
        + [jax.ShapeDtypeStruct((seq, GMLP_W), BF16), jax.ShapeDtypeStruct((seq, D_MODEL), BF16),
           jax.ShapeDtypeStruct((seq, D_MODEL), F32), jax.ShapeDtypeStruct((seq, D_MODEL), F32)],
        scratch_shapes=[pltpu.VMEM((GROUP_W // LANES, tm, LANES), F32)],
        compiler_params=_params(("arbitrary",), 48),
    )(*o_l, rest, rest, rest, rest, x, w_sp, b_col, ln_g, ln_b, w_ba, w_bg, w_out, g1)


def _mlp_fwd(x1, g2, g3, w_mi, w_mo, target):
    seq = x1.shape[0]
    tm, tf = 512, 512
    nf = D_FF // tf

    def body(x1_ref, g2_ref, g3_ref, wmi_ref, wmo_ref, t_ref, h2_ref, a_ref, dy2_ref, dout_ref, loss_ref, dg3_ref,
             h2_s, acc_s):
        i, j = pl.program_id(0), pl.program_id(1)

        @pl.when(j == 0)
        def _():
            xv = x1_ref[...]
            hb = ((xv * _rsqrt_ms(xv)) * g2_ref[...]).astype(BF16)
            h2_s[...] = hb
            h2_ref[...] = hb
            acc_s[...] = jnp.zeros(acc_s.shape, F32)

        @pl.when((i == 0) & (j == 0))
        def _():
            loss_ref[...] = jnp.zeros(loss_ref.shape, F32)
            dg3_ref[...] = jnp.zeros(dg3_ref.shape, F32)

        a = jnp.maximum(_dot(h2_s[...], wmi_ref[...]), 0.0)
        a_ref[...] = a.astype(BF16)
        acc_s[...] += _dot((a * a).astype(BF16), wmo_ref[...])

        @pl.when(j == nf - 1)
        def _():
            y2 = acc_s[...]
            r3 = _rsqrt_ms(y2)
            out = x1_ref[...] + (y2 * r3) * g3_ref[...]
            diff = out - t_ref[...]
            tile_loss = 0.5 * jnp.sum(jnp.mean(diff * diff, axis=-1, keepdims=True), axis=0, keepdims=True)
            loss_ref[...] += jnp.broadcast_to(tile_loss, loss_ref.shape)
            dout = diff * (1.0 / D_MODEL)
            dout_ref[...] = dout
            dy2, dg3 = _rmsnorm_bwd(dout, y2, g3_ref[...])
            dy2_ref[...] = dy2.astype(BF16)
            dg3_ref[...] += dg3

    tok = lambda w: pl.BlockSpec((tm, w), lambda i, j: (i, 0))
    vec = pl.BlockSpec((1, D_MODEL), lambda i, j: (0, 0))
    return pl.pallas_call(
        body, name="mlp_fwd", grid=(seq // tm, nf),
        in_specs=[tok(D_MODEL), vec, vec, pl.BlockSpec((D_MODEL, tf), lambda i, j: (0, j)),
                  pl.BlockSpec((tf, D_MODEL), lambda i, j: (j, 0)), tok(D_MODEL)],
        out_specs=[tok(D_MODEL), pl.BlockSpec((tm, tf), lambda i, j: (i, j)), tok(D_MODEL), tok(D_MODEL),
                   pl.BlockSpec((8, 128), lambda i, j: (0, 0)), vec],
        out_shape=[jax.ShapeDtypeStruct((seq, D_MODEL), BF16), jax.ShapeDtypeStruct((seq, D_FF), BF16),
                   jax.ShapeDtypeStruct((seq, D_MODEL), BF16), jax.ShapeDtypeStruct((seq, D_MODEL), F32),
                   jax.ShapeDtypeStruct((8, 128), F32), jax.ShapeDtypeStruct((1, D_MODEL), F32)],
        scratch_shapes=[pltpu.VMEM((tm, D_MODEL), BF16), pltpu.VMEM((tm, D_MODEL), F32)],
        compiler_params=_params(("arbitrary", "arbitrary"), 48),
    )(x1, g2, g3, w_mi, w_mo, target)


def _mlp_bwd(dy2, a, w_mo, w_mi, dout, x1, y, g2, g1):
    seq = x1.shape[0]
    tm, tf = 512, 512
    nf = D_FF // tf

    def body(dy2_ref, a_ref, wmo_ref, wmi_ref, dout_ref, x1_ref, y_ref, g2_ref, g1_ref,
             dap_ref, dx1_ref, dy_ref, dg2_ref, dg1_ref, acc_s):
        i, j = pl.program_id(0), pl.program_id(1)

        @pl.when(j == 0)
        def _():
            acc_s[...] = jnp.zeros(acc_s.shape, F32)

        @pl.when((i == 0) & (j == 0))
        def _():
            dg2_ref[...] = jnp.zeros(dg2_ref.shape, F32)
            dg1_ref[...] = jnp.zeros(dg1_ref.shape, F32)

        da2 = _dot_nt(dy2_ref[...], wmo_ref[...])
        dap = (da2 * (2.0 * a_ref[...].astype(F32))).astype(BF16)
        dap_ref[...] = dap
        acc_s[...] += _dot_nt(dap, wmi_ref[...])

        @pl.when(j == nf - 1)
        def _():
            dres, dg2 = _rmsnorm_bwd(acc_s[...], x1_ref[...], g2_ref[...])
            dx1 = dout_ref[...] + dres
            dx1_ref[...] = dx1
            dg2_ref[...] += dg2
            dyv, dg1 = _rmsnorm_bwd(dx1, y_ref[...], g1_ref[...])
            dy_ref[...] = dyv.astype(BF16)
            dg1_ref[...] += dg1

    tok = lambda w: pl.BlockSpec((tm, w), lambda i, j: (i, 0))
    vec = pl.BlockSpec((1, D_MODEL), lambda i, j: (0, 0))
    return pl.pallas_call(
        body, name="mlp_bwd", grid=(seq // tm, nf),
        in_specs=[tok(D_MODEL), pl.BlockSpec((tm, tf), lambda i, j: (i, j)),
                  pl.BlockSpec((tf, D_MODEL), lambda i, j: (j, 0)), pl.BlockSpec((D_MODEL, tf), lambda i, j: (0, j)),
                  tok(D_MODEL), tok(D_MODEL), tok(D_MODEL), vec, vec],
        out_specs=[pl.BlockSpec((tm, tf), lambda i, j: (i, j)), tok(D_MODEL), tok(D_MODEL), vec, vec],
        out_shape=[jax.ShapeDtypeStruct((seq, D_FF), BF16), jax.ShapeDtypeStruct((seq, D_MODEL), F32),
                   jax.ShapeDtypeStruct((seq, D_MODEL), BF16), jax.ShapeDtypeStruct((1, D_MODEL), F32),
                   jax.ShapeDtypeStruct((1, D_MODEL), F32)],
        scratch_shapes=[pltpu.VMEM((tm, D_MODEL), F32)],
        compiler_params=_params(("arbitrary", "arbitrary"), 48),
    )(dy2, a, w_mo, w_mi, dout, x1, y, g2, g1)


def _tn_matmul(a, b, name, bm, bn, square_a=False, column_shards=False):
    seq, m = a.shape
    n = b.shape[1]
    ts = 512

    def body(a_ref, b_ref, o_ref):
        @pl.when(pl.program_id(2) == 0)
        def _():
            o_ref[...] = jnp.zeros(o_ref.shape, F32)

        av = a_ref[...]
        if square_a:
            af = av.astype(F32)
            av = (af * af).astype(BF16)
        o_ref[...] += _dot_tn(av, b_ref[...])

    if column_shards:
        out_spec = pl.BlockSpec((None, bm, bn), lambda mi, ni, s: (ni, mi, 0))
        out_shape = jax.ShapeDtypeStruct((n // bn, m, bn), F32)
    else:
        out_spec = pl.BlockSpec((bm, bn), lambda mi, ni, s: (mi, ni))
        out_shape = jax.ShapeDtypeStruct((m, n), F32)
    return pl.pallas_call(
        body, name=name, grid=(m // bm, n // bn, seq // ts),
        in_specs=[pl.BlockSpec((ts, bm), lambda mi, ni, s: (s, mi)), pl.BlockSpec((ts, bn), lambda mi, ni, s: (s, ni))],
        out_specs=out_spec, out_shape=out_shape,
        compiler_params=_params(("arbitrary", "arbitrary", "arbitrary"), 40),
    )(a, b)


def _tn_matmul_residue(a, b, dil, name):
    length = a.shape[0]
    m, n = a.shape[1] // dil, b.shape[1] // dil
    ts = min(512, length)

    def body(a_ref, b_ref, o_ref):
        @pl.when((pl.program_id(0) == 0) & (pl.program_id(1) == 0))
        def _():
            o_ref[...] = jnp.zeros(o_ref.shape, F32)

        o_ref[...] += _dot_tn(a_ref[...], b_ref[...])

    return pl.pallas_call(
        body, name=name, grid=(dil, length // ts),
        in_specs=[pl.BlockSpec((ts, m), lambda r, s: (s, r)), pl.BlockSpec((ts, n), lambda r, s: (s, r))],
        out_specs=pl.BlockSpec((m, n), lambda r, s: (0, 0)),
        out_shape=jax.ShapeDtypeStruct((m, n), F32),
        compiler_params=_params(("arbitrary", "arbitrary"), 40),
    )(a, b)


def _mix_bwd(dy, ya, yg, mg, rest, w_out, w_ba, w_bg, w_sp, b_col, ln_g, ln_b):
    seq = dy.shape[0]
    tm = 256

    def body(dy_ref, ya_ref, yg_ref, mg_ref, up_ref, zp_ref, gap_ref, gbp_ref, wout_ref, wba_ref, wbg_ref,
             wsp_ref, bcol_ref, lg_ref, lb_ref,
             dya0, dya1, dya2, dpr_ref, dwout_ref, dwba_ref, dwbg_ref, dwsp_ref, dbb_ref, dlg_ref, dlb_ref,
             dzln_s, du_s, slab):
        @pl.when(pl.program_id(0) == 0)
        def _():
            for ref in (dwout_ref, dwba_ref, dwbg_ref, dwsp_ref, dbb_ref, dlg_ref, dlb_ref):
                ref[...] = jnp.zeros(ref.shape, F32)

        dyv = dy_ref[...]
        dm = _dot_nt(dyv, wout_ref[...])
        dwout_ref[...] += _dot_tn(mg_ref[...], dyv)
        yab = ya_ref[...].astype(BF16)
        ygb = yg_ref[...]
        a = _dot(yab, wba_ref[...])
        bm = _dot(ygb, wbg_ref[...])
        ga = jax.nn.sigmoid(gap_ref[...])
        gb = jax.nn.sigmoid(gbp_ref[...])
        dpr_ref[:, 2 * GMLP_W:2 * GMLP_W + D_MODEL] = (dm * a * (ga * (1.0 - ga))).astype(BF16)
        dpr_ref[:, 2 * GMLP_W + D_MODEL:REST_W] = (dm * bm * (gb * (1.0 - gb))).astype(BF16)
        da = (dm * ga).astype(BF16)
        db = (dm * gb).astype(BF16)
        dwba = _dot_tn(yab, da)
        dwbg = _dot_tn(ygb, db)
        shard_w = D_MODEL // N_CHIPS
        for j in range(N_CHIPS):
            dwba_ref[j] += dwba[:, j * shard_w:(j + 1) * shard_w]
            dwbg_ref[j] += dwbg[:, j * shard_w:(j + 1) * shard_w]
        dya = _dot_nt(da, wba_ref[...])
        for dya_ref, d in zip((dya0, dya1, dya2), DILATIONS):
            _put_residue(slab, dya, dya_ref, d, GROUP_W, 0)
        dyg = _dot_nt(db, wbg_ref[...])

        zp = zp_ref[...]
        zhat, rstd = _layernorm_stats(_gelu(zp))
        lg = lg_ref[...]
        zln = (zhat * lg + lb_ref[...]).astype(BF16)
        up = up_ref[...]
        u = _gelu(up)
        tril = _tril_mask()
        for g in range(GMLP_GROUPS):
            wm = jnp.where(tril, wsp_ref[g], 0.0).astype(BF16)
            cols = slice(g * CHUNK, (g + 1) * CHUNK)
            for c in range(tm // CHUNK):
                rows = slice(c * CHUNK, (c + 1) * CHUNK)
                zb = zln[rows, cols]
                sz = _dot(wm, zb) + bcol_ref[g]
                dyg_cg = dyg[rows, cols]
                du_s[rows, cols] = dyg_cg * sz
                dsz = dyg_cg * u[rows, cols]
                dszb = dsz.astype(BF16)
                dbb_ref[g] += jnp.broadcast_to(jnp.sum(dsz, axis=-1, keepdims=True), (CHUNK, CHUNK))
                dwsp_ref[g] += jnp.where(tril, _dot_nt(dszb, zb), 0.0)
                dzln_s[rows, cols] = _dot_tn(wm, dszb)
        dzln = dzln_s[...]
        dlg_ref[...] += jnp.sum(dzln * zhat, axis=0, keepdims=True)
        dlb_ref[...] += jnp.sum(dzln, axis=0, keepdims=True)
        dzh = dzln * lg
        dz = rstd * (dzh - jnp.mean(dzh, axis=-1, keepdims=True) - zhat * jnp.mean(dzh * zhat, axis=-1, keepdims=True))
        dpr_ref[:, GMLP_W:2 * GMLP_W] = (dz * _gelu_grad(zp)).astype(BF16)
        dpr_ref[:, 0:GMLP_W] = (du_s[...] * _gelu_grad(up)).astype(BF16)

    tok = lambda w: pl.BlockSpec((tm, w), lambda i: (i, 0))
    full = lambda *s: pl.BlockSpec(s, lambda i: (0,) * len(s))
    return pl.pallas_call(
        body, name="mix_bwd", grid=(seq // tm,),
        in_specs=[tok(D_MODEL), tok(GROUP_W), tok(GMLP_W), tok(D_MODEL),
                  pl.BlockSpec((tm, GMLP_W), lambda i: (i, 0)), pl.BlockSpec((tm, GMLP_W), lambda i: (i, 1)),
                  pl.BlockSpec((tm, D_MODEL), lambda i: (i, 1)), pl.BlockSpec((tm, D_MODEL), lambda i: (i, 2)),
                  full(D_MODEL, D_MODEL), full(GROUP_W, D_MODEL), full(GMLP_W, D_MODEL),
                  full(GMLP_GROUPS, CHUNK, CHUNK), full(GMLP_GROUPS, CHUNK, 1), full(1, GMLP_W), full(1, GMLP_W)],
        out_specs=[pl.BlockSpec((tm // d, d * GROUP_W), lambda i: (i, 0)) for d in DILATIONS]
        + [tok(REST_W), full(D_MODEL, D_MODEL), full(N_CHIPS, GROUP_W, D_MODEL // N_CHIPS),
           full(N_CHIPS, GMLP_W, D_MODEL // N_CHIPS),
           full(GMLP_GROUPS, CHUNK, CHUNK), full(GMLP_GROUPS, CHUNK, CHUNK), full(1, GMLP_W), full(1, GMLP_W)],
        out_shape=[jax.ShapeDtypeStruct((seq // d, d * GROUP_W), F32) for d in DILATIONS]
        + [jax.ShapeDtypeStruct((seq, REST_W), BF16),
           jax.ShapeDtypeStruct((D_MODEL, D_MODEL), F32), jax.ShapeDtypeStruct((N_CHIPS, GROUP_W, D_MODEL // N_CHIPS), F32),
           jax.ShapeDtypeStruct((N_CHIPS, GMLP_W, D_MODEL // N_CHIPS), F32),
           jax.ShapeDtypeStruct((GMLP_GROUPS, CHUNK, CHUNK), F32),
           jax.ShapeDtypeStruct((GMLP_GROUPS, CHUNK, CHUNK), F32), jax.ShapeDtypeStruct((1, GMLP_W), F32),
           jax.ShapeDtypeStruct((1, GMLP_W), F32)],
        scratch_shapes=[pltpu.VMEM((tm, GMLP_W), F32), pltpu.VMEM((tm, GMLP_W), F32),
                        pltpu.VMEM((GROUP_W // LANES, tm, LANES), F32)],
        compiler_params=_params(("arbitrary",), 56),
    )(dy, ya, yg, mg, rest, rest, rest, rest, w_out, w_ba, w_bg, w_sp, b_col, ln_g, ln_b)


def _in_proj_bwd(dqkv, drest, w_qkv, w_rest, x, dx1, g0):
    seq = x.shape[0]
    tm = 256

    def body(d0, d1, d2, dr_ref, w0, w1, w2, wr_ref, x_ref, dx1_ref, g_ref, gx_ref, dg_ref, slab):
        @pl.when(pl.program_id(0) == 0)
        def _():
            dg_ref[...] = jnp.zeros(dg_ref.shape, F32)

        dh = _dot_nt(dr_ref[...], wr_ref[...])
        for d_ref, w_ref, dil in zip((d0, d1, d2), (w0, w1, w2), DILATIONS):
            piece = d_ref[...] if dil == 1 else _get_tokens(slab, d_ref, dil, 3 * GROUP_W, 0, 3 * GROUP_W).astype(BF16)
            dh = dh + _dot_nt(piece, w_ref[...])
        dres, dg = _rmsnorm_bwd(dh, x_ref[...], g_ref[...])
        gx_ref[...] = dx1_ref[...] + dres
        dg_ref[...] += dg

    tok = lambda w: pl.BlockSpec((tm, w), lambda i: (i, 0))
    full = lambda *s: pl.BlockSpec(s, lambda i: (0,) * len(s))
    return pl.pallas_call(
        body, name="in_proj_bwd", grid=(seq // tm,),
        in_specs=[pl.BlockSpec((tm // d, d * 3 * GROUP_W), lambda i: (i, 0)) for d in DILATIONS] + [tok(REST_W)]
        + [full(D_MODEL, 3 * GROUP_W)] * 3 + [full(D_MODEL, REST_W)] + [tok(D_MODEL), tok(D_MODEL), full(1, D_MODEL)],
        out_specs=[tok(D_MODEL), full(1, D_MODEL)],
        out_shape=[jax.ShapeDtypeStruct((seq, D_MODEL), F32), jax.ShapeDtypeStruct((1, D_MODEL), F32)],
        scratch_shapes=[pltpu.VMEM((3 * GROUP_W // LANES, tm, LANES), F32)],
        compiler_params=_params(("arbitrary",), 48),
    )(*dqkv, drest, *w_qkv, w_rest, x, dx1, g0)


def _adamw(w, g, m, v, name):
    rows, cols = w.shape
    tr = 256 if rows % 256 == 0 else rows
    c1 = 1.0 - ADAM_B1 ** ADAM_STEP
    c2 = 1.0 - ADAM_B2 ** ADAM_STEP

    def body(w_ref, g_ref, m_ref, v_ref, d_ref, nm_ref, nv_ref):
        gv = g_ref[...]
        nm = ADAM_B1 * m_ref[...] + (1.0 - ADAM_B1) * gv
        nv = ADAM_B2 * v_ref[...] + (1.0 - ADAM_B2) * (gv * gv)
        d_ref[...] = -ADAM_LR * ((nm / c1) / (jnp.sqrt(nv / c2) + ADAM_EPS) + ADAM_WD * w_ref[...])
        nm_ref[...] = nm
        nv_ref[...] = nv

    spec = pl.BlockSpec((tr, cols), lambda i: (i, 0))
    return pl.pallas_call(
        body, name=name, grid=(rows // tr,),
        in_specs=[spec] * 4, out_specs=[spec] * 3,
        out_shape=[jax.ShapeDtypeStruct((rows, cols), F32)] * 3,
        compiler_params=_params(("arbitrary",), 40),
    )(w, g, m, v)


def _place():
    x, y, c = lax.axis_index("x"), lax.axis_index("y"), lax.axis_index("c")
    chips = [(1 - x, y), (x, 1 - y), (1 - x, 1 - y)]
    return x, y, c, chips


def _gather_weights(shards):
    n = len(shards)
    halves = [s.shape[0] // 2 for s in shards]

    def body(*refs):
        ins, outs, stages = refs[:n], refs[n:2 * n], refs[2 * n:3 * n]
        send_sems, recv_sems, local_sems = refs[3 * n:]
        x, y, c, chips = _place()
        mine = 2 * x + y
        sibling = (x, y, 1 - c)
        for t in range(n):
            stages[t][...] = ins[t][...].astype(BF16)
        local = [pltpu.make_async_copy(stages[t], outs[t].at[mine], local_sems.at[t]) for t in range(n)]
        for cp in local:
            cp.start()

        def half_of(t, chip, which):
            return outs[t].at[chip, pl.ds(which * halves[t], halves[t]), :]

        def from_stage(t, j, to):
            return pltpu.make_async_remote_copy(
                src_ref=stages[t].at[pl.ds(c * halves[t], halves[t]), :], dst_ref=half_of(t, mine, c),
                send_sem=send_sems.at[t, j], recv_sem=recv_sems.at[t, j], device_id=to, device_id_type=MESH)

        def passed_on(t, j, chip, which, to):
            return pltpu.make_async_remote_copy(
                src_ref=half_of(t, chip, which), dst_ref=half_of(t, chip, which),
                send_sem=send_sems.at[t, 3 + j], recv_sem=recv_sems.at[t, 3 + j], device_id=to, device_id_type=MESH)

        sends = []
        for t in range(n):
            for j, (px, py) in enumerate(chips):
                cp = from_stage(t, j, (px, py, c))
                cp.start()
                sends.append(cp)
        for j, (px, py) in enumerate(chips):
            chip = 2 * px + py
            for t in range(n):
                pltpu.make_async_remote_copy(
                    src_ref=stages[t].at[pl.ds(c * halves[t], halves[t]), :], dst_ref=half_of(t, chip, c),
                    send_sem=send_sems.at[t, j], recv_sem=recv_sems.at[t, j], device_id=(px, py, c),
                    device_id_type=MESH).wait_recv()
                cp = passed_on(t, j, chip, c, sibling)
                cp.start()
                sends.append(cp)
        for j, (px, py) in enumerate(chips):
            for t in range(n):
                passed_on(t, j, 2 * px + py, 1 - c, sibling).wait_recv()
        for cp in sends:
            cp.wait_send()
        for cp in local:
            cp.wait()

    stage_bytes = sum(s.size * 6 for s in shards)
    return pl.pallas_call(
        body, name="gather_weights",
        in_specs=[VMEM_SPEC] * n, out_specs=[HBM_SPEC] * n,
        out_shape=[jax.ShapeDtypeStruct((N_CHIPS,) + s.shape, BF16) for s in shards],
        scratch_shapes=[pltpu.VMEM(s.shape, BF16) for s in shards]
        + [pltpu.SemaphoreType.DMA((n, 6)), pltpu.SemaphoreType.DMA((n, 6)), pltpu.SemaphoreType.DMA((n,))],
        compiler_params=pltpu.CompilerParams(vmem_limit_bytes=stage_bytes + 8 * MIB),
    )(*shards)


def _pair_exchange(grads, name):
    n = len(grads)
    halves = [g.shape[1] // 2 for g in grads]

    def body(*refs):
        ins, outs, send_sems, recv_sems = refs[:n], refs[n:2 * n], refs[2 * n], refs[2 * n + 1]
        x, y, c, _ = _place()
        cps = []
        for t in range(n):
            cp = pltpu.make_async_remote_copy(
                src_ref=ins[t].at[:, pl.ds((1 - c) * halves[t], halves[t]), :], dst_ref=outs[t],
                send_sem=send_sems.at[t], recv_sem=recv_sems.at[t], device_id=(x, y, 1 - c), device_id_type=MESH)
            cp.start()
            cps.append(cp)
        for cp in cps:
            cp.wait()

    return pl.pallas_call(
        body, name=name, in_specs=[HBM_SPEC] * n, out_specs=[HBM_SPEC] * n,
        out_shape=[jax.ShapeDtypeStruct((N_CHIPS, h, g.shape[2]), F32) for g, h in zip(grads, halves)],
        scratch_shapes=[pltpu.SemaphoreType.DMA((n,)), pltpu.SemaphoreType.DMA((n,))],
    )(*grads)


def _row_tile(rows):
    return min(rows, 256)


def _pair_add(grad, other, place, name):
    _, rows, cols = grad.shape
    rh = rows // 2
    tr = _row_tile(rh)
    nb = rh // tr

    def body(p_ref, g_ref, a_ref, wire_ref, own_ref):
        s = g_ref[...] + a_ref[...]
        wire_ref[...] = s.astype(BF16)

        @pl.when(pl.program_id(1) == p_ref[1])
        def _():
            own_ref[...] = s

    blk = (None, tr, cols)
    return pl.pallas_call(
        body, name=name,
        grid_spec=pltpu.PrefetchScalarGridSpec(
            num_scalar_prefetch=1, grid=(nb, N_CHIPS),
            in_specs=[pl.BlockSpec(blk, lambda i, j, p: (j, p[0] * nb + i, 0)), pl.BlockSpec(blk, lambda i, j, p: (j, i, 0))],
            out_specs=[pl.BlockSpec(blk, lambda i, j, p: (j, i, 0)), pl.BlockSpec((tr, cols), lambda i, j, p: (i, 0))]),
        out_shape=[jax.ShapeDtypeStruct((N_CHIPS, rh, cols), BF16), jax.ShapeDtypeStruct((rh, cols), F32)],
        compiler_params=_params(("arbitrary", "arbitrary"), 32),
    )(place, grad, other)


def _chip_exchange(wires, name):
    n = len(wires)

    def body(*refs):
        ins, outs, send_sems, recv_sems = refs[:n], refs[n:2 * n], refs[2 * n], refs[2 * n + 1]
        x, y, c, chips = _place()
        cps = []
        for t in range(n):
            for j, (px, py) in enumerate(chips):
                cp = pltpu.make_async_remote_copy(
                    src_ref=ins[t].at[2 * px + py], dst_ref=outs[t].at[j], send_sem=send_sems.at[t, j],
                    recv_sem=recv_sems.at[t, j], device_id=(px, py, c), device_id_type=MESH)
                cp.start()
                cps.append(cp)
        for cp in cps:
            cp.wait()

    return pl.pallas_call(
        body, name=name, in_specs=[HBM_SPEC] * n, out_specs=[HBM_SPEC] * n,
        out_shape=[jax.ShapeDtypeStruct((3,) + w.shape[1:], BF16) for w in wires],
        scratch_shapes=[pltpu.SemaphoreType.DMA((n, 3)), pltpu.SemaphoreType.DMA((n, 3))],
    )(*wires)


def _chip_add(own, arrived, place, name):
    rh, cols = own.shape
    tr = _row_tile(rh)
    nb = rh // tr

    def body(p_ref, s_ref, b0, b1, b2, o_ref):
        o_ref[...] = ((s_ref[...] + b0[...].astype(F32)) + b1[...].astype(F32)) + b2[...].astype(F32)

    blk = (None, tr, cols)
    return pl.pallas_call(
        body, name=name,
        grid_spec=pltpu.PrefetchScalarGridSpec(
            num_scalar_prefetch=1, grid=(nb,),
            in_specs=[pl.BlockSpec((tr, cols), lambda i, p: (i, 0)), pl.BlockSpec(blk, lambda i, p: (0, i, 0)),
                      pl.BlockSpec(blk, lambda i, p: (1, i, 0)), pl.BlockSpec(blk, lambda i, p: (2, i, 0))],
            out_specs=pl.BlockSpec((tr, cols), lambda i, p: (p[0] * nb + i, 0))),
        out_shape=jax.ShapeDtypeStruct((2 * rh, cols), F32),
        compiler_params=_params(("arbitrary",), 32),
    )(place, own, arrived, arrived, arrived)


def _pair_share(halves, name):
    n = len(halves)
    rhs = [h.shape[0] // 2 for h in halves]

    def body(*refs):
        outs, send_sems, recv_sems = refs[n:2 * n], refs[2 * n], refs[2 * n + 1]
        x, y, c, _ = _place()

        def copy(t, which):
            rows = outs[t].at[pl.ds(which * rhs[t], rhs[t]), :]
            return pltpu.make_async_remote_copy(src_ref=rows, dst_ref=rows, send_sem=send_sems.at[t], recv_sem=recv_sems.at[t],
                                                device_id=(x, y, 1 - c), device_id_type=MESH)

        for t in range(n):
            copy(t, c).start()
        for t in range(n):
            copy(t, c).wait_send()
            copy(t, 1 - c).wait_recv()

    return pl.pallas_call(
        body, name=name, in_specs=[HBM_SPEC] * n, out_specs=[HBM_SPEC] * n,
        out_shape=[jax.ShapeDtypeStruct(h.shape, F32) for h in halves],
        input_output_aliases={t: t for t in range(n)},
        scratch_shapes=[pltpu.SemaphoreType.DMA((n,)), pltpu.SemaphoreType.DMA((n,))],
    )(*halves)


def _reduce_scatter(grads, place, tag):
    names = list(grads)
    others = _pair_exchange([grads[n] for n in names], f"{tag}_pair_exchange")
    wires, owns = zip(*[_pair_add(grads[n], o, place, f"{tag}_pair_add_{n}") for n, o in zip(names, others)])
    arrived = _chip_exchange(list(wires), f"{tag}_chip_exchange")
    halves = [_chip_add(own, arr, place, f"{tag}_chip_add_{n}") for n, own, arr in zip(names, owns, arrived)]
    return dict(zip(names, _pair_share(halves, f"{tag}_pair_share")))


def _all_reduce_small(p):
    rows, lanes = p.shape
    flips = [(fx, fy, fc) for fx in (0, 1) for fy in (0, 1) for fc in (0, 1)][1:]

    def body(p_ref, o_ref, buf, send_sems, recv_sems):
        x, y, c, _ = _place()
        me = 4 * x + 2 * y + c
        buf[me] = p_ref[...]
        peers = [((1 - x) if fx else x, (1 - y) if fy else y, (1 - c) if fc else c) for fx, fy, fc in flips]
        cps = []
        for k, peer in enumerate(peers):
            cp = pltpu.make_async_remote_copy(
                src_ref=p_ref, dst_ref=buf.at[me], send_sem=send_sems.at[k], recv_sem=recv_sems.at[k],
                device_id=peer, device_id_type=MESH)
            cp.start()
            cps.append(cp)
        for k, (px, py, pc) in enumerate(peers):
            pltpu.make_async_remote_copy(
                src_ref=p_ref, dst_ref=buf.at[4 * px + 2 * py + pc], send_sem=send_sems.at[k], recv_sem=recv_sems.at[k],
                device_id=(px, py, pc), device_id_type=MESH).wait_recv()
        for cp in cps:
            cp.wait_send()
        acc = buf[0]
        for s in range(1, 8):
            acc = acc + buf[s]
        o_ref[...] = acc

    return pl.pallas_call(
        body, name="small_all_reduce", in_specs=[VMEM_SPEC], out_specs=VMEM_SPEC,
        out_shape=jax.ShapeDtypeStruct((rows, lanes), F32),
        scratch_shapes=[pltpu.VMEM((8, rows, lanes), F32), pltpu.SemaphoreType.DMA((7,)), pltpu.SemaphoreType.DMA((7,))],
        compiler_params=pltpu.CompilerParams(vmem_limit_bytes=32 * MIB),
    )(p)


BIG = ("w_in", "w_branch_attn", "w_branch_gmlp", "w_out", "w_mlp_in", "w_mlp_out")
COLUMN_SHARDED = ("w_in", "w_branch_attn", "w_branch_gmlp", "w_mlp_in")
SMALL = ("norm_pre_mix", "w_spatial", "b_spatial", "ln_v_gain", "ln_v_bias", "norm_post_mix", "norm_pre_mlp", "norm_post_mlp")
ORDER = ("norm_pre_mix", "w_in", "w_spatial", "b_spatial", "ln_v_gain", "ln_v_bias", "w_branch_attn", "w_branch_gmlp",
         "w_out", "norm_post_mix", "norm_pre_mlp", "w_mlp_in", "w_mlp_out", "norm_post_mlp")


def _full_weight(name, gathered):
    if name in COLUMN_SHARDED:
        return jnp.transpose(gathered, (1, 0, 2)).reshape(gathered.shape[1], -1)
    return gathered.reshape(-1, gathered.shape[2])


def _rows8(a):
    a = a.reshape(-1, 128)
    pad = (-a.shape[0]) % 8
    return jnp.pad(a, ((0, pad), (0, 0))) if pad else a


def _qkv_columns(group):
    return [(sec * ATTN_W + group * GROUP_W, sec * ATTN_W + (group + 1) * GROUP_W) for sec in range(3)]


def _local_step(x, target, small, full):
    seq = x.shape[0]
    tables = _rope_tables(seq)
    g0, g1, g2, g3 = small["norm_pre_mix"], small["norm_post_mix"], small["norm_pre_mlp"], small["norm_post_mlp"]
    w_sp = small["w_spatial"]
    b_col = small["b_spatial"].reshape(GMLP_GROUPS, CHUNK, 1)
    ln_g, ln_b = small["ln_v_gain"], small["ln_v_bias"]
    w_in = full["w_in"]

    *hq, rest = _in_proj(x, g0, w_in, *tables[1])
    h, qkv = hq[:N_GROUPS], hq[N_GROUPS:]
    o_l = []
    for g, dil in enumerate(DILATIONS):
        o_l.extend(_attn_fwd(qkv[g], dil))
    *ya_l, yg, mg, y, x1 = _mix_fwd(o_l, rest, x, w_sp, b_col, ln_g, ln_b, full["w_branch_attn"], full["w_branch_gmlp"],
                                    full["w_out"], g1)
    ya, lse = ya_l[0::2], ya_l[1::2]
    h2, a, dy2, dout, loss8, dg3 = _mlp_fwd(x1, g2, g3, full["w_mlp_in"], full["w_mlp_out"], target)
    dap, dx1, dy, dg2, dg1 = _mlp_bwd(dy2, a, full["w_mlp_out"], full["w_mlp_in"], dout, x1, y, g2, g1)
    d_wmo = _tn_matmul(a, dy2, "grad_w_mlp_out", 1024, 1024, square_a=True)
    d_wmi = _tn_matmul(h2, dap, "grad_w_mlp_in", 1024, 1024, column_shards=True)
    *dya, drest, d_wout, d_wba, d_wbg, d_wsp, d_bb, d_lg, d_lb = _mix_bwd(
        dy, ya[0], yg, mg, rest, full["w_out"], full["w_branch_attn"], full["w_branch_gmlp"], w_sp, b_col, ln_g, ln_b)
    dqkv = [_attn_bwd(qkv[g], dya[g], ya[g], lse[g], *tables[dil], dil) for g, dil in enumerate(DILATIONS)]
    w_qkv = [jnp.concatenate([w_in[:, lo:hi] for lo, hi in _qkv_columns(g)], axis=1) for g in range(N_GROUPS)]
    w_rest = w_in[:, QKV_W:]
    d_qkv = [_tn_matmul_residue(h[g], dqkv[g], dil, f"grad_w_in_qkv{g}") for g, dil in enumerate(DILATIONS)]
    d_rest = _tn_matmul(h[0], drest, "grad_w_in_rest", 1024, 1024)
    grad_x, dg0 = _in_proj_bwd(dqkv, drest, w_qkv, w_rest, x, dx1, g0)
    d_win = jnp.concatenate([d_qkv[g][:, s * GROUP_W:(s + 1) * GROUP_W] for s in range(3) for g in range(N_GROUPS)]
                            + [d_rest], axis=1)
    shard_w = IN_W // N_CHIPS
    d_win = jnp.stack([d_win[:, j * shard_w:(j + 1) * shard_w] for j in range(N_CHIPS)], axis=0)
    big = {"w_in": d_win, "w_branch_attn": d_wba, "w_branch_gmlp": d_wbg,
           "w_out": d_wout.reshape(N_CHIPS, D_MODEL // N_CHIPS, D_MODEL), "w_mlp_in": d_wmi,
           "w_mlp_out": d_wmo.reshape(N_CHIPS, D_FF // N_CHIPS, D_MODEL)}
    little = {"norm_pre_mix": dg0, "w_spatial": d_wsp, "b_spatial": d_bb[:, :, 0], "ln_v_gain": d_lg, "ln_v_bias": d_lb,
              "norm_post_mix": dg1, "norm_pre_mlp": dg2, "norm_post_mlp": dg3}
    return loss8[0, 0], grad_x, big, little


def kernel(x, norm_pre_mix, w_in, w_spatial, b_spatial, ln_v_gain, ln_v_bias, w_branch_attn, w_branch_gmlp, w_out, norm_post_mix, norm_pre_mlp, w_mlp_in, w_mlp_out, norm_post_mlp, loss_target, m_norm_pre_mix, m_w_in, m_w_spatial, m_b_spatial, m_ln_v_gain, m_ln_v_bias, m_w_branch_attn, m_w_branch_gmlp, m_w_out, m_norm_post_mix, m_norm_pre_mlp, m_w_mlp_in, m_w_mlp_out, m_norm_post_mlp, v_norm_pre_mix, v_w_in, v_w_spatial, v_b_spatial, v_ln_v_gain, v_ln_v_bias, v_w_branch_attn, v_w_branch_gmlp, v_w_out, v_norm_post_mix, v_norm_pre_mlp, v_w_mlp_in, v_w_mlp_out, v_norm_post_mlp):
    given = dict(norm_pre_mix=norm_pre_mix, w_in=w_in, w_spatial=w_spatial, b_spatial=b_spatial, ln_v_gain=ln_v_gain,
                 ln_v_bias=ln_v_bias, w_branch_attn=w_branch_attn, w_branch_gmlp=w_branch_gmlp, w_out=w_out,
                 norm_post_mix=norm_post_mix, norm_pre_mlp=norm_pre_mlp, w_mlp_in=w_mlp_in, w_mlp_out=w_mlp_out,
                 norm_post_mlp=norm_post_mlp)
    moments_m = dict(norm_pre_mix=m_norm_pre_mix, w_in=m_w_in, w_spatial=m_w_spatial, b_spatial=m_b_spatial,
                     ln_v_gain=m_ln_v_gain, ln_v_bias=m_ln_v_bias, w_branch_attn=m_w_branch_attn,
                     w_branch_gmlp=m_w_branch_gmlp, w_out=m_w_out, norm_post_mix=m_norm_post_mix,
                     norm_pre_mlp=m_norm_pre_mlp, w_mlp_in=m_w_mlp_in, w_mlp_out=m_w_mlp_out, norm_post_mlp=m_norm_post_mlp)
    moments_v = dict(norm_pre_mix=v_norm_pre_mix, w_in=v_w_in, w_spatial=v_w_spatial, b_spatial=v_b_spatial,
                     ln_v_gain=v_ln_v_gain, ln_v_bias=v_ln_v_bias, w_branch_attn=v_w_branch_attn,
                     w_branch_gmlp=v_w_branch_gmlp, w_out=v_w_out, norm_post_mix=v_norm_post_mix,
                     norm_pre_mlp=v_norm_pre_mlp, w_mlp_in=v_w_mlp_in, w_mlp_out=v_w_mlp_out, norm_post_mlp=v_norm_post_mlp)
    cx, cy, cc = lax.axis_index("x"), lax.axis_index("y"), lax.axis_index("c")

    shards = [given[n][0] for n in BIG]
    gathered = _gather_weights(shards)
    full = {n: _full_weight(n, gw) for n, gw in zip(BIG, gathered)}
    small = {n: given[n][0] if given[n].ndim > 2 else given[n] for n in SMALL}

    loss, grad_x, big, grads = _local_step(x[0], loss_target[0], small, full)
    loss = lax.psum(loss, ("x", "y", "c"))

    place = jnp.stack([cc, 2 * cx + cy]).astype(jnp.int32)
    grad_shard = _reduce_scatter(big, place, "grad")

    packed = jnp.concatenate([_rows8(grads[n]) for n in SMALL], axis=0)
    summed = _all_reduce_small(packed)
    row = 0
    for n in SMALL:
        shape = given[n][0].shape
        cnt = -(-(given[n][0].size // 128) // 8) * 8
        grad_shard[n] = summed[row:row + given[n][0].size // 128].reshape(shape)
        row += cnt

    deltas, new_m, new_v = {}, {}, {}
    for n in ORDER:
        shape = given[n].shape
        two_d = (-1, shape[-1])
        d, nm, nv = _adamw(given[n].reshape(two_d), grad_shard[n].reshape(two_d), moments_m[n].reshape(two_d),
                           moments_v[n].reshape(two_d), "adamw_" + n)
        deltas[n], new_m[n], new_v[n] = d.reshape(shape), nm.reshape(shape), nv.reshape(shape)
    grad_out = [grad_shard[n].reshape(given[n].shape) for n in ORDER]
    return (loss, grad_x[None], *grad_out, *[deltas[n] for n in ORDER], *[new_m[n] for n in ORDER],
            *[new_v[n] for n in ORDER])
```

```python
import math

import jax
import jax.numpy as jnp
from jax import lax
from jax.experimental import pallas as pl
from jax.experimental.pallas import tpu as pltpu

F32 = jnp.float32
BF16 = jnp.bfloat16
MESH = pl.DeviceIdType.MESH

D_MODEL = 1024
HEAD_DIM = 64
HEADS_PER_GROUP = 4
GROUP_W = HEADS_PER_GROUP * HEAD_DIM
DILATIONS = (1, 4, 16)
N_GROUPS = len(DILATIONS)
ATTN_W = N_GROUPS * GROUP_W
QKV_W = 3 * ATTN_W
GMLP_W = 512
GMLP_GROUPS = 4
CHUNK = 128
REST_W = 2 * GMLP_W + 2 * D_MODEL
IN_W = QKV_W + REST_W
D_FF = 4096
QBLK = 128
ROPE_THETA = 10000.0
EPS = 1e-6
NEG = -1e30
SCALE = HEAD_DIM ** -0.5
N_CHIPS = 4

ADAM_LR = 0.001
ADAM_B1 = 0.9
ADAM_B2 = 0.999
ADAM_EPS = 1e-08
ADAM_WD = 0.01
ADAM_STEP = 10

MIB = 1024 * 1024
HBM_SPEC = pl.BlockSpec(memory_space=pltpu.HBM)
VMEM_SPEC = pl.BlockSpec(memory_space=pltpu.VMEM)


MLP_TM = 256


def _params(semantics, vmem_mib):
    return pltpu.CompilerParams(dimension_semantics=semantics, vmem_limit_bytes=vmem_mib * MIB)


def _resident(shape):
    return pl.BlockSpec(shape, lambda *_: (0,) * len(shape), pipeline_mode=pl.Buffered(1))


def _dot(a, b):
    return jnp.dot(a, b, preferred_element_type=F32)


def _dot_nt(a, b):
    return lax.dot_general(a, b, (((1,), (1,)), ((), ())), preferred_element_type=F32)


def _dot_tn(a, b):
    return lax.dot_general(a, b, (((0,), (0,)), ((), ())), preferred_element_type=F32)


_GELU_C = math.sqrt(2.0 / math.pi)


def _gelu(x):
    return x * (0.5 * (1.0 + jnp.tanh(_GELU_C * (x + 0.044715 * (x * x * x)))))


def _gelu_grad(x):
    t = jnp.tanh(_GELU_C * (x + 0.044715 * (x * x * x)))
    return 0.5 * (1.0 + t) + 0.5 * x * (1.0 - t * t) * (_GELU_C * (1.0 + 3.0 * 0.044715 * (x * x)))


def _rsqrt_ms(v):
    return lax.rsqrt(jnp.mean(v * v, axis=-1, keepdims=True) + EPS)


def _rmsnorm_bwd(dn, src, gain):
    r = _rsqrt_ms(src)
    t = gain * dn
    dgain = jnp.sum(dn * (src * r), axis=0, keepdims=True)
    dsrc = r * t - src * ((r * r * r) * jnp.mean(t * src, axis=-1, keepdims=True))
    return dsrc, dgain


def _rot_half(v):
    w = v.shape[-1]
    lane = lax.broadcasted_iota(jnp.int32, v.shape, v.ndim - 1)
    return jnp.where((lane % HEAD_DIM) < HEAD_DIM // 2, pltpu.roll(v, w - HEAD_DIM // 2, v.ndim - 1),
                     pltpu.roll(v, HEAD_DIM // 2, v.ndim - 1))


def _head_masks(shape):
    lane = lax.broadcasted_iota(jnp.int32, shape, 1)
    return [(lane >= h * HEAD_DIM) & (lane < (h + 1) * HEAD_DIM) for h in range(HEADS_PER_GROUP)]


LANES = 128


def _put_residue(slab, val, out_ref, dil, width, col0):
    tm, w = val.shape
    if dil == 1:
        out_ref[:, col0:col0 + w] = val.astype(out_ref.dtype)
        return
    for k in range(w // LANES):
        slab[k] = val[:, k * LANES:(k + 1) * LANES]
    for r in range(dil):
        for k in range(w // LANES):
            c = r * width + col0 + k * LANES
            out_ref[:, c:c + LANES] = slab[k, pl.ds(r, tm // dil, stride=dil), :].astype(out_ref.dtype)


def _get_tokens(slab, in_ref, dil, width, col0, w):
    if dil == 1:
        return in_ref[:, col0:col0 + w].astype(F32)
    rows = in_ref.shape[0]
    for r in range(dil):
        for k in range(w // LANES):
            c = r * width + col0 + k * LANES
            slab[k, pl.ds(r, rows, stride=dil), :] = in_ref[:, c:c + LANES].astype(F32)
    return jnp.concatenate([slab[k] for k in range(w // LANES)], axis=1)


def _rope_tables(seq):
    half = HEAD_DIM // 2
    inv_freq = ROPE_THETA ** (-jnp.arange(half, dtype=F32) / half)
    freq = jnp.tile(inv_freq, LANES // half).reshape(1, LANES)
    tm = 512

    def body(f_ref, *refs):
        outs, slab_c, slab_s = refs[:-2], refs[-2], refs[-1]
        row = lax.broadcasted_iota(jnp.int32, (tm, LANES), 0) + pl.program_id(0) * tm
        lane = lax.broadcasted_iota(jnp.int32, (tm, LANES), 1)
        ang = row.astype(F32) * f_ref[...]
        cos = jnp.cos(ang)
        sin = jnp.where((lane % HEAD_DIM) < half, -jnp.sin(ang), jnp.sin(ang))
        slab_c[0] = cos
        slab_s[0] = sin
        for i, dil in enumerate(DILATIONS):
            for tab, slab in ((outs[2 * i], slab_c), (outs[2 * i + 1], slab_s)):
                for r in range(dil):
                    piece = slab[0, pl.ds(r, tm // dil, stride=dil), :] if dil > 1 else slab[0]
                    for k in range(GROUP_W // LANES):
                        tab[:, r * GROUP_W + k * LANES:r * GROUP_W + (k + 1) * LANES] = piece

    outs = pl.pallas_call(
        body, name="rope_tables", grid=(seq // tm,),
        in_specs=[pl.BlockSpec((1, LANES), lambda i: (0, 0))],
        out_specs=[pl.BlockSpec((tm // d, d * GROUP_W), lambda i: (i, 0)) for d in DILATIONS for _ in range(2)],
        out_shape=[jax.ShapeDtypeStruct((seq // d, d * GROUP_W), F32) for d in DILATIONS for _ in range(2)],
        scratch_shapes=[pltpu.VMEM((1, tm, LANES), F32)] * 2,
        compiler_params=_params(("arbitrary",), 32),
    )(freq)
    return {d: (outs[2 * i], outs[2 * i + 1]) for i, d in enumerate(DILATIONS)}


def _in_proj(x, g0, w_in, cos_t, sin_t):
    seq = x.shape[0]
    tm, tn = 256, GROUP_W
    n_qk = 2 * ATTN_W // tn
    n_qkv = QKV_W // tn

    def body(x_ref, g_ref, w_ref, cos_ref, sin_ref, *refs):
        h_refs, qkv_refs, rest_ref, slab = refs[:N_GROUPS], refs[N_GROUPS:2 * N_GROUPS], refs[2 * N_GROUPS], refs[-1]
        xv = x_ref[...]
        hf = (xv * _rsqrt_ms(xv)) * g_ref[...]
        hb = hf.astype(BF16)
        for g, dil in enumerate(DILATIONS):
            _put_residue(slab, hf, h_refs[g], dil, D_MODEL, 0)
        cos, sin = cos_ref[...], sin_ref[...]
        for j in range(IN_W // tn):
            p = _dot(hb, w_ref[:, j * tn:(j + 1) * tn])
            if j < n_qkv:
                if j < n_qk:
                    p = p * cos + _rot_half(p) * sin
                section, g = divmod(j, N_GROUPS)
                _put_residue(slab, p, qkv_refs[g], DILATIONS[g], 3 * GROUP_W, section * GROUP_W)
            else:
                rest_ref[:, (j - n_qkv) * tn:(j - n_qkv + 1) * tn] = p.astype(BF16)

    return pl.pallas_call(
        body, name="in_proj", grid=(seq // tm,),
        in_specs=[pl.BlockSpec((tm, D_MODEL), lambda i: (i, 0)),
                  pl.BlockSpec((1, D_MODEL), lambda i: (0, 0)),
                  _resident((D_MODEL, IN_W)),
                  pl.BlockSpec((tm, GROUP_W), lambda i: (i, 0)),
                  pl.BlockSpec((tm, GROUP_W), lambda i: (i, 0))],
        out_specs=[pl.BlockSpec((tm // d, d * D_MODEL), lambda i: (i, 0)) for d in DILATIONS]
        + [pl.BlockSpec((tm // d, d * 3 * GROUP_W), lambda i: (i, 0)) for d in DILATIONS]
        + [pl.BlockSpec((tm, REST_W), lambda i: (i, 0))],
        out_shape=[jax.ShapeDtypeStruct((seq // d, d * D_MODEL), BF16) for d in DILATIONS]
        + [jax.ShapeDtypeStruct((seq // d, d * 3 * GROUP_W), BF16) for d in DILATIONS]
        + [jax.ShapeDtypeStruct((seq, REST_W), BF16)],
        scratch_shapes=[pltpu.VMEM((D_MODEL // LANES, tm, LANES), F32)],
        compiler_params=_params(("arbitrary",), 56),
    )(x, g0, w_in, cos_t, sin_t)


def _band_masks():
    qi = lax.broadcasted_iota(jnp.int32, (QBLK, QBLK), 0)
    kj = lax.broadcasted_iota(jnp.int32, (QBLK, QBLK), 1)
    return kj <= qi, kj >= qi


def _attn_tile(length):
    return min(512, length)


def _attn_fwd(qkv, dil):
    length = qkv.shape[0]
    tq = _attn_tile(length)
    nsub = tq // QBLK
    nblk = length // tq

    def body(q_ref, k_ref, v_ref, kp_ref, vp_ref, o_ref, l_ref):
        n = pl.program_id(1)
        mask_c, mask_p0 = _band_masks()
        hmask = _head_masks((QBLK, GROUP_W))
        zero = jnp.zeros((), BF16)
        for b in range(nsub):
            rows = slice(b * QBLK, (b + 1) * QBLK)
            q = q_ref[rows, :]
            kc, vc = k_ref[rows, :], v_ref[rows, :]
            if b == 0:
                kp, vp = kp_ref[...], vp_ref[...]
                mask_p = mask_p0 & (n > 0)
            else:
                prow = slice((b - 1) * QBLK, b * QBLK)
                kp, vp = k_ref[prow, :], v_ref[prow, :]
                mask_p = mask_p0
            o_acc = jnp.zeros((QBLK, GROUP_W), F32)
            l_acc = jnp.zeros((QBLK, GROUP_W), F32)
            for h in range(HEADS_PER_GROUP):
                hm = hmask[h]
                sc = jnp.where(mask_c, _dot_nt(q, jnp.where(hm, kc, zero)) * SCALE, NEG)
                sp = jnp.where(mask_p, _dot_nt(q, jnp.where(hm, kp, zero)) * SCALE, NEG)
                m = jnp.maximum(jnp.max(sc, axis=-1, keepdims=True), jnp.max(sp, axis=-1, keepdims=True))
                pc, pp = jnp.exp(sc - m), jnp.exp(sp - m)
                den = jnp.sum(pc, axis=-1, keepdims=True) + jnp.sum(pp, axis=-1, keepdims=True)
                pv = _dot(pc.astype(BF16), jnp.where(hm, vc, zero)) + _dot(pp.astype(BF16), jnp.where(hm, vp, zero))
                o_acc = o_acc + pv / den
                l_acc = l_acc + jnp.where(hm, m + jnp.log(den), 0.0)
            o_ref[rows, :] = o_acc
            l_ref[rows, :] = l_acc

    cur = lambda sec: pl.BlockSpec((tq, GROUP_W), lambda r, n: (n, r * 3 + sec))
    prev = lambda sec: pl.BlockSpec((QBLK, GROUP_W), lambda r, n: (jnp.maximum(n * nsub - 1, 0), r * 3 + sec))
    return pl.pallas_call(
        body, name=f"attn_fwd_d{dil}", grid=(dil, nblk),
        in_specs=[cur(0), cur(1), cur(2), prev(1), prev(2)],
        out_specs=[pl.BlockSpec((tq, GROUP_W), lambda r, n: (n, r))] * 2,
        out_shape=[jax.ShapeDtypeStruct((length, dil * GROUP_W), F32)] * 2,
        compiler_params=_params(("arbitrary", "arbitrary"), 32),
    )(qkv, qkv, qkv, qkv, qkv)


def _attn_bwd(qkv, dy, y, lse, cos_t, sin_t, dil):
    length = qkv.shape[0]
    tq = _attn_tile(length)
    nsub = tq // QBLK
    nblk = length // tq

    def body(q_ref, k_ref, v_ref, kp_ref, vp_ref, qn_ref, dy_ref, y_ref, l_ref, dyn_ref, yn_ref, ln_ref,
             cos_ref, sin_ref, out_ref, dq_s, dk_s, dv_s):
        n = pl.program_id(1)
        mask_c, mask_p0 = _band_masks()
        hmask = _head_masks((QBLK, GROUP_W))
        zero = jnp.zeros((), BF16)

        def head_terms(h, q, dyv, yv, lv, kk, vv, mask):
            hm = hmask[h]
            delta = jnp.sum(jnp.where(hm, dyv * yv, 0.0), axis=-1, keepdims=True)
            lh = jnp.max(jnp.where(hm, lv, NEG), axis=-1, keepdims=True)
            s = _dot_nt(q, jnp.where(hm, kk, zero)) * SCALE
            p = jnp.exp(jnp.where(mask, s - lh, NEG))
            dyb = jnp.where(hm, dyv, 0.0).astype(BF16)
            dp = _dot_nt(dyb, vv)
            ds = p * (dp - delta)
            return p.astype(BF16), ds.astype(BF16), dyb, jnp.where(hm, q, zero)

        dk_s[...] = jnp.zeros(dk_s.shape, F32)
        dv_s[...] = jnp.zeros(dv_s.shape, F32)
        for b in range(nsub):
            rows = slice(b * QBLK, (b + 1) * QBLK)
            q, dyv, yv, lv = q_ref[rows, :], dy_ref[rows, :], y_ref[rows, :], l_ref[rows, :]
            kc, vc = k_ref[rows, :], v_ref[rows, :]
            if b == 0:
                kp, vp = kp_ref[...], vp_ref[...]
                mask_p = mask_p0 & (n > 0)
            else:
                prow = slice((b - 1) * QBLK, b * QBLK)
                kp, vp = k_ref[prow, :], v_ref[prow, :]
                mask_p = mask_p0
            dq = jnp.zeros((QBLK, GROUP_W), F32)
            for h in range(HEADS_PER_GROUP):
                hm = hmask[h]
                pc, dsc, dyb, qh = head_terms(h, q, dyv, yv, lv, kc, vc, mask_c)
                pp, dsp, _, _ = head_terms(h, q, dyv, yv, lv, kp, vp, mask_p)
                dq = dq + (_dot(dsc, jnp.where(hm, kc, zero)) + _dot(dsp, jnp.where(hm, kp, zero))) * SCALE
                dv_s[rows, :] += _dot_tn(pc, dyb)
                dk_s[rows, :] += _dot_tn(dsc, qh) * SCALE
                if b > 0:
                    dv_s[prow, :] += _dot_tn(pp, dyb)
                    dk_s[prow, :] += _dot_tn(dsp, qh) * SCALE
            dq_s[rows, :] = dq
        rows = slice((nsub - 1) * QBLK, nsub * QBLK)
        qn, dyn, yn, ln = qn_ref[...], dyn_ref[...], yn_ref[...], ln_ref[...]
        mask_n = mask_p0 & (n < nblk - 1)
        for h in range(HEADS_PER_GROUP):
            pn, dsn, dyb, qh = head_terms(h, qn, dyn, yn, ln, k_ref[rows, :], v_ref[rows, :], mask_n)
            dv_s[rows, :] += _dot_tn(pn, dyb)
            dk_s[rows, :] += _dot_tn(dsn, qh) * SCALE
        cos, sin = cos_ref[...], sin_ref[...]
        dq, dk = dq_s[...], dk_s[...]
        out_ref[:, 0:GROUP_W] = (dq * cos - _rot_half(dq) * sin).astype(BF16)
        out_ref[:, GROUP_W:2 * GROUP_W] = (dk * cos - _rot_half(dk) * sin).astype(BF16)
        out_ref[:, 2 * GROUP_W:3 * GROUP_W] = dv_s[...].astype(BF16)

    cur = lambda sec: pl.BlockSpec((tq, GROUP_W), lambda r, n: (n, r * 3 + sec))
    prev = lambda sec: pl.BlockSpec((QBLK, GROUP_W), lambda r, n: (jnp.maximum(n * nsub - 1, 0), r * 3 + sec))
    nxt_q = pl.BlockSpec((QBLK, GROUP_W), lambda r, n: (jnp.minimum((n + 1) * nsub, nblk * nsub - 1), r * 3))
    tok = pl.BlockSpec((tq, GROUP_W), lambda r, n: (n, r))
    tok_next = pl.BlockSpec((QBLK, GROUP_W), lambda r, n: (jnp.minimum((n + 1) * nsub, nblk * nsub - 1), r))
    return pl.pallas_call(
        body, name=f"attn_bwd_d{dil}", grid=(dil, nblk),
        in_specs=[cur(0), cur(1), cur(2), prev(1), prev(2), nxt_q,
                  tok, tok, tok, tok_next, tok_next, tok_next, tok, tok],
        out_specs=pl.BlockSpec((tq, 3 * GROUP_W), lambda r, n: (n, r)),
        out_shape=jax.ShapeDtypeStruct((length, dil * 3 * GROUP_W), BF16),
        scratch_shapes=[pltpu.VMEM((tq, GROUP_W), F32)] * 3,
        compiler_params=_params(("arbitrary", "arbitrary"), 32),
    )(qkv, qkv, qkv, qkv, qkv, qkv, dy, y, lse, dy, y, lse, cos_t, sin_t)


def _layernorm_stats(z):
    mu = jnp.mean(z, axis=-1, keepdims=True)
    zc = z - mu
    rstd = lax.rsqrt(jnp.mean(zc * zc, axis=-1, keepdims=True) + EPS)
    return zc * rstd, rstd


def _tril_mask():
    row = lax.broadcasted_iota(jnp.int32, (CHUNK, CHUNK), 0)
    col = lax.broadcasted_iota(jnp.int32, (CHUNK, CHUNK), 1)
    return col <= row


def _mix_fwd(o_l, rest, x, w_sp, b_col, ln_g, ln_b, w_ba, w_bg, w_out, g1):
    seq = x.shape[0]
    tm = 256

    def body(o0, l0, o1, l1, o2, l2, up_ref, zp_ref, gap_ref, gbp_ref, x_ref, wsp_ref, bcol_ref, lg_ref, lb_ref,
             wba_ref, wbg_ref, wout_ref, g1_ref, ya0, lj0, ya1, lj1, ya2, lj2, yg_ref, mg_ref, y_ref, x1_ref, slab):
        outs = [_get_tokens(slab, o, d, GROUP_W, 0, GROUP_W) for o, d in zip((o0, o1, o2), DILATIONS)]
        lses = [_get_tokens(slab, l, d, GROUP_W, 0, GROUP_W) for l, d in zip((l0, l1, l2), DILATIONS)]
        m = jnp.maximum(jnp.maximum(lses[0], lses[1]), lses[2])
        es = [jnp.exp(l - m) for l in lses]
        tot = es[0] + es[1] + es[2]
        ya = (es[0] * outs[0] + es[1] * outs[1] + es[2] * outs[2]) / tot
        lj = m + jnp.log(tot)
        for ya_ref, lj_ref, d in zip((ya0, ya1, ya2), (lj0, lj1, lj2), DILATIONS):
            _put_residue(slab, ya, ya_ref, d, GROUP_W, 0)
            _put_residue(slab, lj, lj_ref, d, GROUP_W, 0)
        zhat, _ = _layernorm_stats(_gelu(zp_ref[...].astype(F32)))
        zln = (zhat * lg_ref[...] + lb_ref[...]).astype(BF16)
        u = _gelu(up_ref[...].astype(F32))
        tril = _tril_mask()
        for g in range(GMLP_GROUPS):
            wm = jnp.where(tril, wsp_ref[g], 0.0).astype(BF16)
            cols = slice(g * CHUNK, (g + 1) * CHUNK)
            for c in range(tm // CHUNK):
                rows = slice(c * CHUNK, (c + 1) * CHUNK)
                sz = _dot(wm, zln[rows, cols]) + bcol_ref[g]
                yg_ref[rows, cols] = (u[rows, cols] * sz).astype(BF16)
        a = _dot(ya.astype(BF16), wba_ref[...])
        bm = _dot(yg_ref[...], wbg_ref[...])
        merged = (jax.nn.sigmoid(gap_ref[...].astype(F32)) * a + jax.nn.sigmoid(gbp_ref[...].astype(F32)) * bm).astype(BF16)
        mg_ref[...] = merged
        yv = _dot(merged, wout_ref[...])
        y_ref[...] = yv
        x1_ref[...] = x_ref[...] + (yv * _rsqrt_ms(yv)) * g1_ref[...]

    tok = lambda w: pl.BlockSpec((tm, w), lambda i: (i, 0))
    res = lambda d: pl.BlockSpec((tm // d, d * GROUP_W), lambda i: (i, 0))
    full = lambda *s: pl.BlockSpec(s, lambda i: (0,) * len(s))
    res_specs = [res(d) for d in DILATIONS for _ in range(2)]
    return pl.pallas_call(
        body, name="mix_fwd", grid=(seq // tm,),
        in_specs=res_specs + [
            pl.BlockSpec((tm, GMLP_W), lambda i: (i, 0)), pl.BlockSpec((tm, GMLP_W), lambda i: (i, 1)),
            pl.BlockSpec((tm, D_MODEL), lambda i: (i, 1)), pl.BlockSpec((tm, D_MODEL), lambda i: (i, 2)),
            tok(D_MODEL), full(GMLP_GROUPS, CHUNK, CHUNK), full(GMLP_GROUPS, CHUNK, 1), full(1, GMLP_W), full(1, GMLP_W),
            full(GROUP_W, D_MODEL), full(GMLP_W, D_MODEL), full(D_MODEL, D_MODEL), full(1, D_MODEL)],
        out_specs=res_specs + [tok(GMLP_W), tok(D_MODEL), tok(D_MODEL), tok(D_MODEL)],
        out_shape=[jax.ShapeDtypeStruct((seq // d, d * GROUP_W), F32) for d in DILATIONS for _ in range(2)]
        + [jax.ShapeDtypeStruct((seq, GMLP_W), BF16), jax.ShapeDtypeStruct((seq, D_MODEL), BF16),
           jax.ShapeDtypeStruct((seq, D_MODEL), F32), jax.ShapeDtypeStruct((seq, D_MODEL), F32)],
        scratch_shapes=[pltpu.VMEM((GROUP_W // LANES, tm, LANES), F32)],
        compiler_params=_params(("arbitrary",), 48),
    )(*o_l, rest, rest, rest, rest, x, w_sp, b_col, ln_g, ln_b, w_ba, w_bg, w_out, g1)


def _mlp_fwd(x1, g2, g3, w_mi, w_mo, target):
    seq = x1.shape[0]
    tm, tf = MLP_TM, 512

    def body(x1_ref, g2_ref, g3_ref, wmi_ref, wmo_ref, t_ref, h2_ref, a_ref, dy2_ref, dout_ref, loss_ref, dg3_ref, sq_s):
        @pl.when(pl.program_id(0) == 0)
        def _():
            loss_ref[...] = jnp.zeros(loss_ref.shape, F32)
            dg3_ref[...] = jnp.zeros(dg3_ref.shape, F32)

        xv = x1_ref[...]
        hb = ((xv * _rsqrt_ms(xv)) * g2_ref[...]).astype(BF16)
        h2_ref[...] = hb
        for j in range(D_FF // tf):
            cols = slice(j * tf, (j + 1) * tf)
            a = jnp.maximum(_dot(hb, wmi_ref[:, cols]), 0.0)
            a_ref[:, cols] = a.astype(BF16)
            sq_s[:, cols] = (a * a).astype(BF16)
        y2 = _dot(sq_s[...], wmo_ref[...])
        r3 = _rsqrt_ms(y2)
        out = xv + (y2 * r3) * g3_ref[...]
        diff = out - t_ref[...]
        tile_loss = 0.5 * jnp.sum(jnp.mean(diff * diff, axis=-1, keepdims=True), axis=0, keepdims=True)
        loss_ref[...] += jnp.broadcast_to(tile_loss, loss_ref.shape)
        dout = diff * (1.0 / D_MODEL)
        dout_ref[...] = dout
        dy2, dg3 = _rmsnorm_bwd(dout, y2, g3_ref[...])
        dy2_ref[...] = dy2.astype(BF16)
        dg3_ref[...] += dg3

    tok = lambda w: pl.BlockSpec((tm, w), lambda i: (i, 0))
    vec = pl.BlockSpec((1, D_MODEL), lambda i: (0, 0))
    return pl.pallas_call(
        body, name="mlp_fwd", grid=(seq // tm,),
        in_specs=[tok(D_MODEL), vec, vec, _resident((D_MODEL, D_FF)), _resident((D_FF, D_MODEL)), tok(D_MODEL)],
        out_specs=[tok(D_MODEL), tok(D_FF), tok(D_MODEL), tok(D_MODEL), pl.BlockSpec((8, 128), lambda i: (0, 0)), vec],
        out_shape=[jax.ShapeDtypeStruct((seq, D_MODEL), BF16), jax.ShapeDtypeStruct((seq, D_FF), BF16),
                   jax.ShapeDtypeStruct((seq, D_MODEL), BF16), jax.ShapeDtypeStruct((seq, D_MODEL), F32),
                   jax.ShapeDtypeStruct((8, 128), F32), jax.ShapeDtypeStruct((1, D_MODEL), F32)],
        scratch_shapes=[pltpu.VMEM((tm, D_FF), BF16)],
        compiler_params=_params(("arbitrary",), 56),
    )(x1, g2, g3, w_mi, w_mo, target)


def _mlp_bwd(dy2, a, w_mo, w_mi, dout, x1, y, g2, g1):
    seq = x1.shape[0]
    tm, tf = MLP_TM, 512

    def body(dy2_ref, a_ref, wmo_ref, wmi_ref, dout_ref, x1_ref, y_ref, g2_ref, g1_ref,
             dap_ref, dx1_ref, dy_ref, dg2_ref, dg1_ref):
        @pl.when(pl.program_id(0) == 0)
        def _():
            dg2_ref[...] = jnp.zeros(dg2_ref.shape, F32)
            dg1_ref[...] = jnp.zeros(dg1_ref.shape, F32)

        dy2v = dy2_ref[...]
        for j in range(D_FF // tf):
            cols = slice(j * tf, (j + 1) * tf)
            da2 = _dot_nt(dy2v, wmo_ref[cols, :])
            dap_ref[:, cols] = (da2 * (2.0 * a_ref[:, cols].astype(F32))).astype(BF16)
        dh2 = _dot_nt(dap_ref[...], wmi_ref[...])
        dres, dg2 = _rmsnorm_bwd(dh2, x1_ref[...], g2_ref[...])
        dx1 = dout_ref[...] + dres
        dx1_ref[...] = dx1
        dg2_ref[...] += dg2
        dyv, dg1 = _rmsnorm_bwd(dx1, y_ref[...], g1_ref[...])
        dy_ref[...] = dyv.astype(BF16)
        dg1_ref[...] += dg1

    tok = lambda w: pl.BlockSpec((tm, w), lambda i: (i, 0))
    vec = pl.BlockSpec((1, D_MODEL), lambda i: (0, 0))
    return pl.pallas_call(
        body, name="mlp_bwd", grid=(seq // tm,),
        in_specs=[tok(D_MODEL), tok(D_FF), _resident((D_FF, D_MODEL)), _resident((D_MODEL, D_FF)),
                  tok(D_MODEL), tok(D_MODEL), tok(D_MODEL), vec, vec],
        out_specs=[tok(D_FF), tok(D_MODEL), tok(D_MODEL), vec, vec],
        out_shape=[jax.ShapeDtypeStruct((seq, D_FF), BF16), jax.ShapeDtypeStruct((seq, D_MODEL), F32),
                   jax.ShapeDtypeStruct((seq, D_MODEL), BF16), jax.ShapeDtypeStruct((1, D_MODEL), F32),
                   jax.ShapeDtypeStruct((1, D_MODEL), F32)],
        compiler_params=_params(("arbitrary",), 56),
    )(dy2, a, w_mo, w_mi, dout, x1, y, g2, g1)


def _tn_matmul(a, b, name, bm, bn, square_a=False, column_shards=False):
    seq, m = a.shape
    n = b.shape[1]
    ts = 512

    def body(a_ref, b_ref, o_ref):
        @pl.when(pl.program_id(2) == 0)
        def _():
            o_ref[...] = jnp.zeros(o_ref.shape, F32)

        av = a_ref[...]
        if square_a:
            af = av.astype(F32)
            av = (af * af).astype(BF16)
        o_ref[...] += _dot_tn(av, b_ref[...])

    if column_shards:
        out_spec = pl.BlockSpec((None, bm, bn), lambda mi, ni, s: (ni, mi, 0))
        out_shape = jax.ShapeDtypeStruct((n // bn, m, bn), F32)
    else:
        out_spec = pl.BlockSpec((bm, bn), lambda mi, ni, s: (mi, ni))
        out_shape = jax.ShapeDtypeStruct((m, n), F32)
    return pl.pallas_call(
        body, name=name, grid=(m // bm, n // bn, seq // ts),
        in_specs=[pl.BlockSpec((ts, bm), lambda mi, ni, s: (s, mi)), pl.BlockSpec((ts, bn), lambda mi, ni, s: (s, ni))],
        out_specs=out_spec, out_shape=out_shape,
        compiler_params=_params(("arbitrary", "arbitrary", "arbitrary"), 40),
    )(a, b)


def _tn_matmul_residue(a, b, dil, name):
    length = a.shape[0]
    m, n = a.shape[1] // dil, b.shape[1] // dil
    ts = min(512, length)

    def body(a_ref, b_ref, o_ref):
        @pl.when((pl.program_id(0) == 0) & (pl.program_id(1) == 0))
        def _():
            o_ref[...] = jnp.zeros(o_ref.shape, F32)

        o_ref[...] += _dot_tn(a_ref[...], b_ref[...])

    return pl.pallas_call(
        body, name=name, grid=(dil, length // ts),
        in_specs=[pl.BlockSpec((ts, m), lambda r, s: (s, r)), pl.BlockSpec((ts, n), lambda r, s: (s, r))],
        out_specs=pl.BlockSpec((m, n), lambda r, s: (0, 0)),
        out_shape=jax.ShapeDtypeStruct((m, n), F32),
        compiler_params=_params(("arbitrary", "arbitrary"), 40),
    )(a, b)


def _mix_bwd(dy, ya, yg, mg, rest, w_out, w_ba, w_bg, w_sp, b_col, ln_g, ln_b):
    seq = dy.shape[0]
    tm = 256

    def body(dy_ref, ya_ref, yg_ref, mg_ref, up_ref, zp_ref, gap_ref, gbp_ref, wout_ref, wba_ref, wbg_ref,
             wsp_ref, bcol_ref, lg_ref, lb_ref,
             dya0, dya1, dya2, dpr_ref, dwout_ref, dwba_ref, dwbg_ref, dwsp_ref, dbb_ref, dlg_ref, dlb_ref,
             dzln_s, du_s, slab):
        @pl.when(pl.program_id(0) == 0)
        def _():
            for ref in (dwout_ref, dwba_ref, dwbg_ref, dwsp_ref, dbb_ref, dlg_ref, dlb_ref):
                ref[...] = jnp.zeros(ref.shape, F32)

        dyv = dy_ref[...]
        dm = _dot_nt(dyv, wout_ref[...])
        dwout_ref[...] += _dot_tn(mg_ref[...], dyv)
        yab = ya_ref[...].astype(BF16)
        ygb = yg_ref[...]
        a = _dot(yab, wba_ref[...])
        bm = _dot(ygb, wbg_ref[...])
        ga = jax.nn.sigmoid(gap_ref[...].astype(F32))
        gb = jax.nn.sigmoid(gbp_ref[...].astype(F32))
        dpr_ref[:, 2 * GMLP_W:2 * GMLP_W + D_MODEL] = (dm * a * (ga * (1.0 - ga))).astype(BF16)
        dpr_ref[:, 2 * GMLP_W + D_MODEL:REST_W] = (dm * bm * (gb * (1.0 - gb))).astype(BF16)
        da = (dm * ga).astype(BF16)
        db = (dm * gb).astype(BF16)
        dwba = _dot_tn(yab, da)
        dwbg = _dot_tn(ygb, db)
        shard_w = D_MODEL // N_CHIPS
        for j in range(N_CHIPS):
            dwba_ref[j] += dwba[:, j * shard_w:(j + 1) * shard_w]
            dwbg_ref[j] += dwbg[:, j * shard_w:(j + 1) * shard_w]
        dya = _dot_nt(da, wba_ref[...])
        for dya_ref, d in zip((dya0, dya1, dya2), DILATIONS):
            _put_residue(slab, dya, dya_ref, d, GROUP_W, 0)
        dyg = _dot_nt(db, wbg_ref[...])

        zp = zp_ref[...].astype(F32)
        zhat, rstd = _layernorm_stats(_gelu(zp))
        lg = lg_ref[...]
        zln = (zhat * lg + lb_ref[...]).astype(BF16)
        up = up_ref[...].astype(F32)
        u = _gelu(up)
        tril = _tril_mask()
        for g in range(GMLP_GROUPS):
            wm = jnp.where(tril, wsp_ref[g], 0.0).astype(BF16)
            cols = slice(g * CHUNK, (g + 1) * CHUNK)
            for c in range(tm // CHUNK):
                rows = slice(c * CHUNK, (c + 1) * CHUNK)
                zb = zln[rows, cols]
                sz = _dot(wm, zb) + bcol_ref[g]
                dyg_cg = dyg[rows, cols]
                du_s[rows, cols] = dyg_cg * sz
                dsz = dyg_cg * u[rows, cols]
                dszb = dsz.astype(BF16)
                dbb_ref[g] += jnp.broadcast_to(jnp.sum(dsz, axis=-1, keepdims=True), (CHUNK, CHUNK))
                dwsp_ref[g] += jnp.where(tril, _dot_nt(dszb, zb), 0.0)
                dzln_s[rows, cols] = _dot_tn(wm, dszb)
        dzln = dzln_s[...]
        dlg_ref[...] += jnp.sum(dzln * zhat, axis=0, keepdims=True)
        dlb_ref[...] += jnp.sum(dzln, axis=0, keepdims=True)
        dzh = dzln * lg
        dz = rstd * (dzh - jnp.mean(dzh, axis=-1, keepdims=True) - zhat * jnp.mean(dzh * zhat, axis=-1, keepdims=True))
        dpr_ref[:, GMLP_W:2 * GMLP_W] = (dz * _gelu_grad(zp)).astype(BF16)
        dpr_ref[:, 0:GMLP_W] = (du_s[...] * _gelu_grad(up)).astype(BF16)

    tok = lambda w: pl.BlockSpec((tm, w), lambda i: (i, 0))
    full = lambda *s: pl.BlockSpec(s, lambda i: (0,) * len(s))
    return pl.pallas_call(
        body, name="mix_bwd", grid=(seq // tm,),
        in_specs=[tok(D_MODEL), tok(GROUP_W), tok(GMLP_W), tok(D_MODEL),
                  pl.BlockSpec((tm, GMLP_W), lambda i: (i, 0)), pl.BlockSpec((tm, GMLP_W), lambda i: (i, 1)),
                  pl.BlockSpec((tm, D_MODEL), lambda i: (i, 1)), pl.BlockSpec((tm, D_MODEL), lambda i: (i, 2)),
                  full(D_MODEL, D_MODEL), full(GROUP_W, D_MODEL), full(GMLP_W, D_MODEL),
                  full(GMLP_GROUPS, CHUNK, CHUNK), full(GMLP_GROUPS, CHUNK, 1), full(1, GMLP_W), full(1, GMLP_W)],
        out_specs=[pl.BlockSpec((tm // d, d * GROUP_W), lambda i: (i, 0)) for d in DILATIONS]
        + [tok(REST_W), full(D_MODEL, D_MODEL), full(N_CHIPS, GROUP_W, D_MODEL // N_CHIPS),
           full(N_CHIPS, GMLP_W, D_MODEL // N_CHIPS),
           full(GMLP_GROUPS, CHUNK, CHUNK), full(GMLP_GROUPS, CHUNK, CHUNK), full(1, GMLP_W), full(1, GMLP_W)],
        out_shape=[jax.ShapeDtypeStruct((seq // d, d * GROUP_W), F32) for d in DILATIONS]
        + [jax.ShapeDtypeStruct((seq, REST_W), BF16),
           jax.ShapeDtypeStruct((D_MODEL, D_MODEL), F32), jax.ShapeDtypeStruct((N_CHIPS, GROUP_W, D_MODEL // N_CHIPS), F32),
           jax.ShapeDtypeStruct((N_CHIPS, GMLP_W, D_MODEL // N_CHIPS), F32),
           jax.ShapeDtypeStruct((GMLP_GROUPS, CHUNK, CHUNK), F32),
           jax.ShapeDtypeStruct((GMLP_GROUPS, CHUNK, CHUNK), F32), jax.ShapeDtypeStruct((1, GMLP_W), F32),
           jax.ShapeDtypeStruct((1, GMLP_W), F32)],
        scratch_shapes=[pltpu.VMEM((tm, GMLP_W), F32), pltpu.VMEM((tm, GMLP_W), F32),
                        pltpu.VMEM((GROUP_W // LANES, tm, LANES), F32)],
        compiler_params=_params(("arbitrary",), 56),
    )(dy, ya, yg, mg, rest, rest, rest, rest, w_out, w_ba, w_bg, w_sp, b_col, ln_g, ln_b)


def _in_proj_bwd(dqkv, drest, w_qkv, w_rest, x, dx1, g0):
    seq = x.shape[0]
    tm = 256

    def body(d0, d1, d2, dr_ref, w0, w1, w2, wr_ref, x_ref, dx1_ref, g_ref, gx_ref, dg_ref, slab):
        @pl.when(pl.program_id(0) == 0)
        def _():
            dg_ref[...] = jnp.zeros(dg_ref.shape, F32)

        dh = _dot_nt(dr_ref[...], wr_ref[...])
        for d_ref, w_ref, dil in zip((d0, d1, d2), (w0, w1, w2), DILATIONS):
            piece = d_ref[...] if dil == 1 else _get_tokens(slab, d_ref, dil, 3 * GROUP_W, 0, 3 * GROUP_W).astype(BF16)
            dh = dh + _dot_nt(piece, w_ref[...])
        dres, dg = _rmsnorm_bwd(dh, x_ref[...], g_ref[...])
        gx_ref[...] = dx1_ref[...] + dres
        dg_ref[...] += dg

    tok = lambda w: pl.BlockSpec((tm, w), lambda i: (i, 0))
    full = lambda *s: pl.BlockSpec(s, lambda i: (0,) * len(s))
    return pl.pallas_call(
        body, name="in_proj_bwd", grid=(seq // tm,),
        in_specs=[pl.BlockSpec((tm // d, d * 3 * GROUP_W), lambda i: (i, 0)) for d in DILATIONS] + [tok(REST_W)]
        + [_resident((D_MODEL, 3 * GROUP_W))] * 3 + [_resident((D_MODEL, REST_W))]
        + [tok(D_MODEL), tok(D_MODEL), full(1, D_MODEL)],
        out_specs=[tok(D_MODEL), full(1, D_MODEL)],
        out_shape=[jax.ShapeDtypeStruct((seq, D_MODEL), F32), jax.ShapeDtypeStruct((1, D_MODEL), F32)],
        scratch_shapes=[pltpu.VMEM((3 * GROUP_W // LANES, tm, LANES), F32)],
        compiler_params=_params(("arbitrary",), 48),
    )(*dqkv, drest, *w_qkv, w_rest, x, dx1, g0)


def _adamw(w, g, m, v, name):
    rows, cols = w.shape
    tr = 256 if rows % 256 == 0 else rows
    c1 = 1.0 - ADAM_B1 ** ADAM_STEP
    c2 = 1.0 - ADAM_B2 ** ADAM_STEP

    def body(w_ref, g_ref, m_ref, v_ref, d_ref, nm_ref, nv_ref):
        gv = g_ref[...]
        nm = ADAM_B1 * m_ref[...] + (1.0 - ADAM_B1) * gv
        nv = ADAM_B2 * v_ref[...] + (1.0 - ADAM_B2) * (gv * gv)
        d_ref[...] = -ADAM_LR * ((nm / c1) / (jnp.sqrt(nv / c2) + ADAM_EPS) + ADAM_WD * w_ref[...])
        nm_ref[...] = nm
        nv_ref[...] = nv

    spec = pl.BlockSpec((tr, cols), lambda i: (i, 0))
    return pl.pallas_call(
        body, name=name, grid=(rows // tr,),
        in_specs=[spec] * 4, out_specs=[spec] * 3,
        out_shape=[jax.ShapeDtypeStruct((rows, cols), F32)] * 3,
        compiler_params=_params(("arbitrary",), 40),
    )(w, g, m, v)


def _place():
    x, y, c = lax.axis_index("x"), lax.axis_index("y"), lax.axis_index("c")
    chips = [(1 - x, y), (x, 1 - y), (1 - x, 1 - y)]
    return x, y, c, chips


def _gather_weights(shards):
    n = len(shards)
    halves = [s.shape[0] // 2 for s in shards]

    def body(*refs):
        ins, outs, stages = refs[:n], refs[n:2 * n], refs[2 * n:3 * n]
        send_sems, recv_sems, local_sems = refs[3 * n:]
        x, y, c, chips = _place()
        mine = 2 * x + y
        sibling = (x, y, 1 - c)
        for t in range(n):
            stages[t][...] = ins[t][...].astype(BF16)
        local = [pltpu.make_async_copy(stages[t], outs[t].at[mine], local_sems.at[t]) for t in range(n)]
        for cp in local:
            cp.start()

        def half_of(t, chip, which):
            return outs[t].at[chip, pl.ds(which * halves[t], halves[t]), :]

        def from_stage(t, j, to):
            return pltpu.make_async_remote_copy(
                src_ref=stages[t].at[pl.ds(c * halves[t], halves[t]), :], dst_ref=half_of(t, mine, c),
                send_sem=send_sems.at[t, j], recv_sem=recv_sems.at[t, j], device_id=to, device_id_type=MESH)

        def passed_on(t, j, chip, which, to):
            return pltpu.make_async_remote_copy(
                src_ref=half_of(t, chip, which), dst_ref=half_of(t, chip, which),
                send_sem=send_sems.at[t, 3 + j], recv_sem=recv_sems.at[t, 3 + j], device_id=to, device_id_type=MESH)

        sends = []
        for t in range(n):
            for j, (px, py) in enumerate(chips):
                cp = from_stage(t, j, (px, py, c))
                cp.start()
                sends.append(cp)
        for j, (px, py) in enumerate(chips):
            chip = 2 * px + py
            for t in range(n):
                pltpu.make_async_remote_copy(
                    src_ref=stages[t].at[pl.ds(c * halves[t], halves[t]), :], dst_ref=half_of(t, chip, c),
                    send_sem=send_sems.at[t, j], recv_sem=recv_sems.at[t, j], device_id=(px, py, c),
                    device_id_type=MESH).wait_recv()
                cp = passed_on(t, j, chip, c, sibling)
                cp.start()
                sends.append(cp)
        for j, (px, py) in enumerate(chips):
            for t in range(n):
                passed_on(t, j, 2 * px + py, 1 - c, sibling).wait_recv()
        for cp in sends:
            cp.wait_send()
        for cp in local:
            cp.wait()

    stage_bytes = sum(s.size * 6 for s in shards)
    return pl.pallas_call(
        body, name="gather_weights",
        in_specs=[VMEM_SPEC] * n, out_specs=[HBM_SPEC] * n,
        out_shape=[jax.ShapeDtypeStruct((N_CHIPS,) + s.shape, BF16) for s in shards],
        scratch_shapes=[pltpu.VMEM(s.shape, BF16) for s in shards]
        + [pltpu.SemaphoreType.DMA((n, 6)), pltpu.SemaphoreType.DMA((n, 6)), pltpu.SemaphoreType.DMA((n,))],
        compiler_params=pltpu.CompilerParams(vmem_limit_bytes=stage_bytes + 8 * MIB),
    )(*shards)


def _pair_exchange(grads, name):
    n = len(grads)
    halves = [g.shape[1] // 2 for g in grads]

    def body(*refs):
        ins, outs, send_sems, recv_sems = refs[:n], refs[n:2 * n], refs[2 * n], refs[2 * n + 1]
        x, y, c, _ = _place()
        cps = []
        for t in range(n):
            cp = pltpu.make_async_remote_copy(
                src_ref=ins[t].at[:, pl.ds((1 - c) * halves[t], halves[t]), :], dst_ref=outs[t],
                send_sem=send_sems.at[t], recv_sem=recv_sems.at[t], device_id=(x, y, 1 - c), device_id_type=MESH)
            cp.start()
            cps.append(cp)
        for cp in cps:
            cp.wait()

    return pl.pallas_call(
        body, name=name, in_specs=[HBM_SPEC] * n, out_specs=[HBM_SPEC] * n,
        out_shape=[jax.ShapeDtypeStruct((N_CHIPS, h, g.shape[2]), F32) for g, h in zip(grads, halves)],
        scratch_shapes=[pltpu.SemaphoreType.DMA((n,)), pltpu.SemaphoreType.DMA((n,))],
    )(*grads)


def _row_tile(rows):
    return min(rows, 256)


def _pair_add(grad, other, place, name):
    _, rows, cols = grad.shape
    rh = rows // 2
    tr = _row_tile(rh)
    nb = rh // tr

    def body(p_ref, g_ref, a_ref, wire_ref, own_ref):
        s = g_ref[...] + a_ref[...]
        wire_ref[...] = s.astype(BF16)

        @pl.when(pl.program_id(1) == p_ref[1])
        def _():
            own_ref[...] = s

    blk = (None, tr, cols)
    return pl.pallas_call(
        body, name=name,
        grid_spec=pltpu.PrefetchScalarGridSpec(
            num_scalar_prefetch=1, grid=(nb, N_CHIPS),
            in_specs=[pl.BlockSpec(blk, lambda i, j, p: (j, p[0] * nb + i, 0)), pl.BlockSpec(blk, lambda i, j, p: (j, i, 0))],
            out_specs=[pl.BlockSpec(blk, lambda i, j, p: (j, i, 0)), pl.BlockSpec((tr, cols), lambda i, j, p: (i, 0))]),
        out_shape=[jax.ShapeDtypeStruct((N_CHIPS, rh, cols), BF16), jax.ShapeDtypeStruct((rh, cols), F32)],
        compiler_params=_params(("arbitrary", "arbitrary"), 32),
    )(place, grad, other)


def _chip_exchange(wires, name):
    n = len(wires)

    def body(*refs):
        ins, outs, send_sems, recv_sems = refs[:n], refs[n:2 * n], refs[2 * n], refs[2 * n + 1]
        x, y, c, chips = _place()
        cps = []
        for t in range(n):
            for j, (px, py) in enumerate(chips):
                cp = pltpu.make_async_remote_copy(
                    src_ref=ins[t].at[2 * px + py], dst_ref=outs[t].at[j], send_sem=send_sems.at[t, j],
                    recv_sem=recv_sems.at[t, j], device_id=(px, py, c), device_id_type=MESH)
                cp.start()
                cps.append(cp)
        for cp in cps:
            cp.wait()

    return pl.pallas_call(
        body, name=name, in_specs=[HBM_SPEC] * n, out_specs=[HBM_SPEC] * n,
        out_shape=[jax.ShapeDtypeStruct((3,) + w.shape[1:], BF16) for w in wires],
        scratch_shapes=[pltpu.SemaphoreType.DMA((n, 3)), pltpu.SemaphoreType.DMA((n, 3))],
    )(*wires)


def _chip_add(own, arrived, place, name):
    rh, cols = own.shape
    tr = _row_tile(rh)
    nb = rh // tr

    def body(p_ref, s_ref, b0, b1, b2, o_ref):
        o_ref[...] = ((s_ref[...] + b0[...].astype(F32)) + b1[...].astype(F32)) + b2[...].astype(F32)

    blk = (None, tr, cols)
    return pl.pallas_call(
        body, name=name,
        grid_spec=pltpu.PrefetchScalarGridSpec(
            num_scalar_prefetch=1, grid=(nb,),
            in_specs=[pl.BlockSpec((tr, cols), lambda i, p: (i, 0)), pl.BlockSpec(blk, lambda i, p: (0, i, 0)),
                      pl.BlockSpec(blk, lambda i, p: (1, i, 0)), pl.BlockSpec(blk, lambda i, p: (2, i, 0))],
            out_specs=pl.BlockSpec((tr, cols), lambda i, p: (p[0] * nb + i, 0))),
        out_shape=jax.ShapeDtypeStruct((2 * rh, cols), F32),
        compiler_params=_params(("arbitrary",), 32),
    )(place, own, arrived, arrived, arrived)


def _pair_share(halves, name):
    n = len(halves)
    rhs = [h.shape[0] // 2 for h in halves]

    def body(*refs):
        outs, send_sems, recv_sems = refs[n:2 * n], refs[2 * n], refs[2 * n + 1]
        x, y, c, _ = _place()

        def copy(t, which):
            rows = outs[t].at[pl.ds(which * rhs[t], rhs[t]), :]
            return pltpu.make_async_remote_copy(src_ref=rows, dst_ref=rows, send_sem=send_sems.at[t], recv_sem=recv_sems.at[t],
                                                device_id=(x, y, 1 - c), device_id_type=MESH)

        for t in range(n):
            copy(t, c).start()
        for t in range(n):
            copy(t, c).wait_send()
            copy(t, 1 - c).wait_recv()

    return pl.pallas_call(
        body, name=name, in_specs=[HBM_SPEC] * n, out_specs=[HBM_SPEC] * n,
        out_shape=[jax.ShapeDtypeStruct(h.shape, F32) for h in halves],
        input_output_aliases={t: t for t in range(n)},
        scratch_shapes=[pltpu.SemaphoreType.DMA((n,)), pltpu.SemaphoreType.DMA((n,))],
    )(*halves)


def _reduce_scatter(grads, place, tag):
    names = list(grads)
    others = _pair_exchange([grads[n] for n in names], f"{tag}_pair_exchange")
    wires, owns = zip(*[_pair_add(grads[n], o, place, f"{tag}_pair_add_{n}") for n, o in zip(names, others)])
    arrived = _chip_exchange(list(wires), f"{tag}_chip_exchange")
    halves = [_chip_add(own, arr, place, f"{tag}_chip_add_{n}") for n, own, arr in zip(names, owns, arrived)]
    return dict(zip(names, _pair_share(halves, f"{tag}_pair_share")))


def _all_reduce_small(p):
    rows, lanes = p.shape
    flips = [(fx, fy, fc) for fx in (0, 1) for fy in (0, 1) for fc in (0, 1)][1:]

    def body(p_ref, o_ref, buf, send_sems, recv_sems):
        x, y, c, _ = _place()
        me = 4 * x + 2 * y + c
        buf[me] = p_ref[...]
        peers = [((1 - x) if fx else x, (1 - y) if fy else y, (1 - c) if fc else c) for fx, fy, fc in flips]
        cps = []
        for k, peer in enumerate(peers):
            cp = pltpu.make_async_remote_copy(
                src_ref=p_ref, dst_ref=buf.at[me], send_sem=send_sems.at[k], recv_sem=recv_sems.at[k],
                device_id=peer, device_id_type=MESH)
            cp.start()
            cps.append(cp)
        for k, (px, py, pc) in enumerate(peers):
            pltpu.make_async_remote_copy(
                src_ref=p_ref, dst_ref=buf.at[4 * px + 2 * py + pc], send_sem=send_sems.at[k], recv_sem=recv_sems.at[k],
                device_id=(px, py, pc), device_id_type=MESH).wait_recv()
        for cp in cps:
            cp.wait_send()
        acc = buf[0]
        for s in range(1, 8):
            acc = acc + buf[s]
        o_ref[...] = acc

    return pl.pallas_call(
        body, name="small_all_reduce", in_specs=[VMEM_SPEC], out_specs=VMEM_SPEC,
        out_shape=jax.ShapeDtypeStruct((rows, lanes), F32),
        scratch_shapes=[pltpu.VMEM((8, rows, lanes), F32), pltpu.SemaphoreType.DMA((7,)), pltpu.SemaphoreType.DMA((7,))],
        compiler_params=pltpu.CompilerParams(vmem_limit_bytes=32 * MIB),
    )(p)


BIG = ("w_in", "w_branch_attn", "w_branch_gmlp", "w_out", "w_mlp_in", "w_mlp_out")
COLUMN_SHARDED = ("w_in", "w_branch_attn", "w_branch_gmlp", "w_mlp_in")
SMALL = ("norm_pre_mix", "w_spatial", "b_spatial", "ln_v_gain", "ln_v_bias", "norm_post_mix", "norm_pre_mlp", "norm_post_mlp")
ORDER = ("norm_pre_mix", "w_in", "w_spatial", "b_spatial", "ln_v_gain", "ln_v_bias", "w_branch_attn", "w_branch_gmlp",
         "w_out", "norm_post_mix", "norm_pre_mlp", "w_mlp_in", "w_mlp_out", "norm_post_mlp")


def _full_weight(name, gathered):
    if name in COLUMN_SHARDED:
        return jnp.transpose(gathered, (1, 0, 2)).reshape(gathered.shape[1], -1)
    return gathered.reshape(-1, gathered.shape[2])


def _rows8(a):
    a = a.reshape(-1, 128)
    pad = (-a.shape[0]) % 8
    return jnp.pad(a, ((0, pad), (0, 0))) if pad else a


def _qkv_columns(group):
    return [(sec * ATTN_W + group * GROUP_W, sec * ATTN_W + (group + 1) * GROUP_W) for sec in range(3)]


def _local_step(x, target, small, full):
    seq = x.shape[0]
    tables = _rope_tables(seq)
    g0, g1, g2, g3 = small["norm_pre_mix"], small["norm_post_mix"], small["norm_pre_mlp"], small["norm_post_mlp"]
    w_sp = small["w_spatial"]
    b_col = small["b_spatial"].reshape(GMLP_GROUPS, CHUNK, 1)
    ln_g, ln_b = small["ln_v_gain"], small["ln_v_bias"]
    w_in = full["w_in"]

    *hq, rest = _in_proj(x, g0, w_in, *tables[1])
    h, qkv = hq[:N_GROUPS], hq[N_GROUPS:]
    o_l = []
    for g, dil in enumerate(DILATIONS):
        o_l.extend(_attn_fwd(qkv[g], dil))
    *ya_l, yg, mg, y, x1 = _mix_fwd(o_l, rest, x, w_sp, b_col, ln_g, ln_b, full["w_branch_attn"], full["w_branch_gmlp"],
                                    full["w_out"], g1)
    ya, lse = ya_l[0::2], ya_l[1::2]
    h2, a, dy2, dout, loss8, dg3 = _mlp_fwd(x1, g2, g3, full["w_mlp_in"], full["w_mlp_out"], target)
    dap, dx1, dy, dg2, dg1 = _mlp_bwd(dy2, a, full["w_mlp_out"], full["w_mlp_in"], dout, x1, y, g2, g1)
    d_wmo = _tn_matmul(a, dy2, "grad_w_mlp_out", 1024, 1024, square_a=True)
    d_wmi = _tn_matmul(h2, dap, "grad_w_mlp_in", 1024, 1024, column_shards=True)
    *dya, drest, d_wout, d_wba, d_wbg, d_wsp, d_bb, d_lg, d_lb = _mix_bwd(
        dy, ya[0], yg, mg, rest, full["w_out"], full["w_branch_attn"], full["w_branch_gmlp"], w_sp, b_col, ln_g, ln_b)
    dqkv = [_attn_bwd(qkv[g], dya[g], ya[g], lse[g], *tables[dil], dil) for g, dil in enumerate(DILATIONS)]
    w_qkv = [jnp.concatenate([w_in[:, lo:hi] for lo, hi in _qkv_columns(g)], axis=1) for g in range(N_GROUPS)]
    w_rest = w_in[:, QKV_W:]
    d_qkv = [_tn_matmul_residue(h[g], dqkv[g], dil, f"grad_w_in_qkv{g}") for g, dil in enumerate(DILATIONS)]
    d_rest = _tn_matmul(h[0], drest, "grad_w_in_rest", 1024, 1024)
    grad_x, dg0 = _in_proj_bwd(dqkv, drest, w_qkv, w_rest, x, dx1, g0)
    d_win = jnp.concatenate([d_qkv[g][:, s * GROUP_W:(s + 1) * GROUP_W] for s in range(3) for g in range(N_GROUPS)]
                            + [d_rest], axis=1)
    shard_w = IN_W // N_CHIPS
    d_win = jnp.stack([d_win[:, j * shard_w:(j + 1) * shard_w] for j in range(N_CHIPS)], axis=0)
    big = {"w_in": d_win, "w_branch_attn": d_wba, "w_branch_gmlp": d_wbg,
           "w_out": d_wout.reshape(N_CHIPS, D_MODEL // N_CHIPS, D_MODEL), "w_mlp_in": d_wmi,
           "w_mlp_out": d_wmo.reshape(N_CHIPS, D_FF // N_CHIPS, D_MODEL)}
    little = {"norm_pre_mix": dg0, "w_spatial": d_wsp, "b_spatial": d_bb[:, :, 0], "ln_v_gain": d_lg, "ln_v_bias": d_lb,
              "norm_post_mix": dg1, "norm_pre_mlp": dg2, "norm_post_mlp": dg3}
    return loss8[0, 0], grad_x, big, little


def kernel(x, norm_pre_mix, w_in, w_spatial, b_spatial, ln_v_gain, ln_v_bias, w_branch_attn, w_branch_gmlp, w_out, norm_post_mix, norm_pre_mlp, w_mlp_in, w_mlp_out, norm_post_mlp, loss_target, m_norm_pre_mix, m_w_in, m_w_spatial, m_b_spatial, m_ln_v_gain, m_ln_v_bias, m_w_branch_attn, m_w_branch_gmlp, m_w_out, m_norm_post_mix, m_norm_pre_mlp, m_w_mlp_in, m_w_mlp_out, m_norm_post_mlp, v_norm_pre_mix, v_w_in, v_w_spatial, v_b_spatial, v_ln_v_gain, v_ln_v_bias, v_w_branch_attn, v_w_branch_gmlp, v_w_out, v_norm_post_mix, v_norm_pre_mlp, v_w_mlp_in, v_w_mlp_out, v_norm_post_mlp):
    given = dict(norm_pre_mix=norm_pre_mix, w_in=w_in, w_spatial=w_spatial, b_spatial=b_spatial, ln_v_gain=ln_v_gain,
                 ln_v_bias=ln_v_bias, w_branch_attn=w_branch_attn, w_branch_gmlp=w_branch_gmlp, w_out=w_out,
                 norm_post_mix=norm_post_mix, norm_pre_mlp=norm_pre_mlp, w_mlp_in=w_mlp_in, w_mlp_out=w_mlp_out,
                 norm_post_mlp=norm_post_mlp)
    moments_m = dict(norm_pre_mix=m_norm_pre_mix, w_in=m_w_in, w_spatial=m_w_spatial, b_spatial=m_b_spatial,
                     ln_v_gain=m_ln_v_gain, ln_v_bias=m_ln_v_bias, w_branch_attn=m_w_branch_attn,
                     w_branch_gmlp=m_w_branch_gmlp, w_out=m_w_out, norm_post_mix=m_norm_post_mix,
                     norm_pre_mlp=m_norm_pre_mlp, w_mlp_in=m_w_mlp_in, w_mlp_out=m_w_mlp_out, norm_post_mlp=m_norm_post_mlp)
    moments_v = dict(norm_pre_mix=v_norm_pre_mix, w_in=v_w_in, w_spatial=v_w_spatial, b_spatial=v_b_spatial,
                     ln_v_gain=v_ln_v_gain, ln_v_bias=v_ln_v_bias, w_branch_attn=v_w_branch_attn,
                     w_branch_gmlp=v_w_branch_gmlp, w_out=v_w_out, norm_post_mix=v_norm_post_mix,
                     norm_pre_mlp=v_norm_pre_mlp, w_mlp_in=v_w_mlp_in, w_mlp_out=v_w_mlp_out, norm_post_mlp=v_norm_post_mlp)
    cx, cy, cc = lax.axis_index("x"), lax.axis_index("y"), lax.axis_index("c")

    shards = [given[n][0] for n in BIG]
    gathered = _gather_weights(shards)
    full = {n: _full_weight(n, gw) for n, gw in zip(BIG, gathered)}
    small = {n: given[n][0] if given[n].ndim > 2 else given[n] for n in SMALL}

    loss, grad_x, big, grads = _local_step(x[0], loss_target[0], small, full)
    loss = lax.psum(loss, ("x", "y", "c"))

    place = jnp.stack([cc, 2 * cx + cy]).astype(jnp.int32)
    grad_shard = _reduce_scatter(big, place, "grad")

    packed = jnp.concatenate([_rows8(grads[n]) for n in SMALL], axis=0)
    summed = _all_reduce_small(packed)
    row = 0
    for n in SMALL:
        shape = given[n][0].shape
        cnt = -(-(given[n][0].size // 128) // 8) * 8
        grad_shard[n] = summed[row:row + given[n][0].size // 128].reshape(shape)
        row += cnt

    deltas, new_m, new_v = {}, {}, {}
    for n in ORDER:
        shape = given[n].shape
        two_d = (-1, shape[-1])
        d, nm, nv = _adamw(given[n].reshape(two_d), grad_shard[n].reshape(two_d), moments_m[n].reshape(two_d),
                           moments_v[n].reshape(two_d), "adamw_" + n)
        deltas[n], new_m[n], new_v[n] = d.reshape(shape), nm.reshape(shape), nv.reshape(shape)
    grad_out = [grad_shard[n].reshape(given[n].shape) for n in ORDER]
    return (loss, grad_x[None], *grad_out, *[deltas[n] for n in ORDER], *[new_m[n] for n in ORDER],
            *[new_v[n] for n in ORDER])
```

```python
import math

import jax
import jax.numpy as jnp
from jax import lax
from jax.experimental import pallas as pl
from jax.experimental.pallas import tpu as pltpu

F32 = jnp.float32
BF16 = jnp.bfloat16
MESH = pl.DeviceIdType.MESH

D_MODEL = 1024
HEAD_DIM = 64
HEADS_PER_GROUP = 4
GROUP_W = HEADS_PER_GROUP * HEAD_DIM
DILATIONS = (1, 4, 16)
N_GROUPS = len(DILATIONS)
ATTN_W = N_GROUPS * GROUP_W
QKV_W = 3 * ATTN_W
GMLP_W = 512
GMLP_GROUPS = 4
CHUNK = 128
REST_W = 2 * GMLP_W + 2 * D_MODEL
IN_W = QKV_W + REST_W
D_FF = 4096
QBLK = 128
ROPE_THETA = 10000.0
EPS = 1e-6
NEG = -1e30
SCALE = HEAD_DIM ** -0.5
N_CHIPS = 4

ADAM_LR = 0.001
ADAM_B1 = 0.9
ADAM_B2 = 0.999
ADAM_EPS = 1e-08
ADAM_WD = 0.01
ADAM_STEP = 10

MIB = 1024 * 1024
HBM_SPEC = pl.BlockSpec(memory_space=pltpu.HBM)
VMEM_SPEC = pl.BlockSpec(memory_space=pltpu.VMEM)


MLP_TM = 256


def _params(semantics, vmem_mib):
    return pltpu.CompilerParams(dimension_semantics=semantics, vmem_limit_bytes=vmem_mib * MIB)


def _resident(shape):
    return pl.BlockSpec(shape, lambda *_: (0,) * len(shape), pipeline_mode=pl.Buffered(1))


def _dot(a, b):
    return jnp.dot(a, b, preferred_element_type=F32)


def _dot_nt(a, b):
    return lax.dot_general(a, b, (((1,), (1,)), ((), ())), preferred_element_type=F32)


def _dot_tn(a, b):
    return lax.dot_general(a, b, (((0,), (0,)), ((), ())), preferred_element_type=F32)


_GELU_C = math.sqrt(2.0 / math.pi)


def _gelu(x):
    return x * (0.5 * (1.0 + jnp.tanh(_GELU_C * (x + 0.044715 * (x * x * x)))))


def _gelu_grad(x):
    t = jnp.tanh(_GELU_C * (x + 0.044715 * (x * x * x)))
    return 0.5 * (1.0 + t) + 0.5 * x * (1.0 - t * t) * (_GELU_C * (1.0 + 3.0 * 0.044715 * (x * x)))


def _rsqrt_ms(v):
    return lax.rsqrt(jnp.mean(v * v, axis=-1, keepdims=True) + EPS)


def _rmsnorm_bwd(dn, src, gain):
    r = _rsqrt_ms(src)
    t = gain * dn
    dgain = jnp.sum(dn * (src * r), axis=0, keepdims=True)
    dsrc = r * t - src * ((r * r * r) * jnp.mean(t * src, axis=-1, keepdims=True))
    return dsrc, dgain


def _rot_half(v):
    w = v.shape[-1]
    lane = lax.broadcasted_iota(jnp.int32, v.shape, v.ndim - 1)
    return jnp.where((lane % HEAD_DIM) < HEAD_DIM // 2, pltpu.roll(v, w - HEAD_DIM // 2, v.ndim - 1),
                     pltpu.roll(v, HEAD_DIM // 2, v.ndim - 1))


def _head_masks(shape):
    lane = lax.broadcasted_iota(jnp.int32, shape, 1)
    return [(lane >= h * HEAD_DIM) & (lane < (h + 1) * HEAD_DIM) for h in range(HEADS_PER_GROUP)]


LANES = 128


def _put_residue(slab, val, out_ref, dil, width, col0):
    tm, w = val.shape
    if dil == 1:
        out_ref[:, col0:col0 + w] = val.astype(out_ref.dtype)
        return
    for k in range(w // LANES):
        slab[k] = val[:, k * LANES:(k + 1) * LANES]
    for r in range(dil):
        for k in range(w // LANES):
            c = r * width + col0 + k * LANES
            out_ref[:, c:c + LANES] = slab[k, pl.ds(r, tm // dil, stride=dil), :].astype(out_ref.dtype)


def _get_tokens(slab, in_ref, dil, width, col0, w):
    if dil == 1:
        return in_ref[:, col0:col0 + w].astype(F32)
    rows = in_ref.shape[0]
    for r in range(dil):
        for k in range(w // LANES):
            c = r * width + col0 + k * LANES
            slab[k, pl.ds(r, rows, stride=dil), :] = in_ref[:, c:c + LANES].astype(F32)
    return jnp.concatenate([slab[k] for k in range(w // LANES)], axis=1)


def _rope_tables(seq):
    half = HEAD_DIM // 2
    inv_freq = ROPE_THETA ** (-jnp.arange(half, dtype=F32) / half)
    freq = jnp.tile(inv_freq, LANES // half).reshape(1, LANES)
    tm = 512

    def body(f_ref, *refs):
        outs, slab_c, slab_s = refs[:-2], refs[-2], refs[-1]
        row = lax.broadcasted_iota(jnp.int32, (tm, LANES), 0) + pl.program_id(0) * tm
        lane = lax.broadcasted_iota(jnp.int32, (tm, LANES), 1)
        ang = row.astype(F32) * f_ref[...]
        cos = jnp.cos(ang)
        sin = jnp.where((lane % HEAD_DIM) < half, -jnp.sin(ang), jnp.sin(ang))
        slab_c[0] = cos
        slab_s[0] = sin
        for i, dil in enumerate(DILATIONS):
            for tab, slab in ((outs[2 * i], slab_c), (outs[2 * i + 1], slab_s)):
                for r in range(dil):
                    piece = slab[0, pl.ds(r, tm // dil, stride=dil), :] if dil > 1 else slab[0]
                    for k in range(GROUP_W // LANES):
                        tab[:, r * GROUP_W + k * LANES:r * GROUP_W + (k + 1) * LANES] = piece

    outs = pl.pallas_call(
        body, name="rope_tables", grid=(seq // tm,),
        in_specs=[pl.BlockSpec((1, LANES), lambda i: (0, 0))],
        out_specs=[pl.BlockSpec((tm // d, d * GROUP_W), lambda i: (i, 0)) for d in DILATIONS for _ in range(2)],
        out_shape=[jax.ShapeDtypeStruct((seq // d, d * GROUP_W), F32) for d in DILATIONS for _ in range(2)],
        scratch_shapes=[pltpu.VMEM((1, tm, LANES), F32)] * 2,
        compiler_params=_params(("arbitrary",), 32),
    )(freq)
    return {d: (outs[2 * i], outs[2 * i + 1]) for i, d in enumerate(DILATIONS)}


def _in_proj(x, g0, w_in, cos_t, sin_t, rider=None):
    seq = x.shape[0]
    tm, tn = 256, GROUP_W
    n_qk = 2 * ATTN_W // tn
    n_qkv = QKV_W // tn

    def body(x_ref, g_ref, w_ref, cos_ref, sin_ref, *refs):
        h_refs, qkv_refs, rest_ref, slab = refs[:N_GROUPS], refs[N_GROUPS:2 * N_GROUPS], refs[2 * N_GROUPS], refs[-1]
        xv = x_ref[...]
        hf = (xv * _rsqrt_ms(xv)) * g_ref[...]
        hb = hf.astype(BF16)
        for g, dil in enumerate(DILATIONS):
            _put_residue(slab, hf, h_refs[g], dil, D_MODEL, 0)
        cos, sin = cos_ref[...], sin_ref[...]
        for j in range(IN_W // tn):
            p = _dot(hb, w_ref[:, j * tn:(j + 1) * tn])
            if j < n_qkv:
                if j < n_qk:
                    p = p * cos + _rot_half(p) * sin
                section, g = divmod(j, N_GROUPS)
                _put_residue(slab, p, qkv_refs[g], DILATIONS[g], 3 * GROUP_W, section * GROUP_W)
            else:
                rest_ref[:, (j - n_qkv) * tn:(j - n_qkv + 1) * tn] = p.astype(BF16)

    return _call(
        body, name="in_proj", grid=(seq // tm,),
        in_specs=[pl.BlockSpec((tm, D_MODEL), lambda i: (i, 0)),
                  pl.BlockSpec((1, D_MODEL), lambda i: (0, 0)),
                  _resident((D_MODEL, IN_W)),
                  pl.BlockSpec((tm, GROUP_W), lambda i: (i, 0)),
                  pl.BlockSpec((tm, GROUP_W), lambda i: (i, 0))],
        out_specs=[pl.BlockSpec((tm // d, d * D_MODEL), lambda i: (i, 0)) for d in DILATIONS]
        + [pl.BlockSpec((tm // d, d * 3 * GROUP_W), lambda i: (i, 0)) for d in DILATIONS]
        + [pl.BlockSpec((tm, REST_W), lambda i: (i, 0))],
        out_shape=[jax.ShapeDtypeStruct((seq // d, d * D_MODEL), BF16) for d in DILATIONS]
        + [jax.ShapeDtypeStruct((seq // d, d * 3 * GROUP_W), BF16) for d in DILATIONS]
        + [jax.ShapeDtypeStruct((seq, REST_W), BF16)],
        scratch_shapes=[pltpu.VMEM((D_MODEL // LANES, tm, LANES), F32)],
        params=_params(("arbitrary",), 56), args=(x, g0, w_in, cos_t, sin_t), rider=rider)


def _band_masks():
    qi = lax.broadcasted_iota(jnp.int32, (QBLK, QBLK), 0)
    kj = lax.broadcasted_iota(jnp.int32, (QBLK, QBLK), 1)
    return kj <= qi, kj >= qi


def _attn_tile(length):
    return min(512, length)


def _attn_fwd(qkv, dil):
    length = qkv.shape[0]
    tq = _attn_tile(length)
    nsub = tq // QBLK
    nblk = length // tq

    def body(q_ref, k_ref, v_ref, kp_ref, vp_ref, o_ref, l_ref):
        n = pl.program_id(1)
        mask_c, mask_p0 = _band_masks()
        hmask = _head_masks((QBLK, GROUP_W))
        zero = jnp.zeros((), BF16)
        for b in range(nsub):
            rows = slice(b * QBLK, (b + 1) * QBLK)
            q = q_ref[rows, :]
            kc, vc = k_ref[rows, :], v_ref[rows, :]
            if b == 0:
                kp, vp = kp_ref[...], vp_ref[...]
                mask_p = mask_p0 & (n > 0)
            else:
                prow = slice((b - 1) * QBLK, b * QBLK)
                kp, vp = k_ref[prow, :], v_ref[prow, :]
                mask_p = mask_p0
            o_acc = jnp.zeros((QBLK, GROUP_W), F32)
            l_acc = jnp.zeros((QBLK, GROUP_W), F32)
            for h in range(HEADS_PER_GROUP):
                hm = hmask[h]
                sc = jnp.where(mask_c, _dot_nt(q, jnp.where(hm, kc, zero)) * SCALE, NEG)
                sp = jnp.where(mask_p, _dot_nt(q, jnp.where(hm, kp, zero)) * SCALE, NEG)
                m = jnp.maximum(jnp.max(sc, axis=-1, keepdims=True), jnp.max(sp, axis=-1, keepdims=True))
                pc, pp = jnp.exp(sc - m), jnp.exp(sp - m)
                den = jnp.sum(pc, axis=-1, keepdims=True) + jnp.sum(pp, axis=-1, keepdims=True)
                pv = _dot(pc.astype(BF16), jnp.where(hm, vc, zero)) + _dot(pp.astype(BF16), jnp.where(hm, vp, zero))
                o_acc = o_acc + pv / den
                l_acc = l_acc + jnp.where(hm, m + jnp.log(den), 0.0)
            o_ref[rows, :] = o_acc
            l_ref[rows, :] = l_acc

    cur = lambda sec: pl.BlockSpec((tq, GROUP_W), lambda r, n: (n, r * 3 + sec))
    prev = lambda sec: pl.BlockSpec((QBLK, GROUP_W), lambda r, n: (jnp.maximum(n * nsub - 1, 0), r * 3 + sec))
    return pl.pallas_call(
        body, name=f"attn_fwd_d{dil}", grid=(dil, nblk),
        in_specs=[cur(0), cur(1), cur(2), prev(1), prev(2)],
        out_specs=[pl.BlockSpec((tq, GROUP_W), lambda r, n: (n, r))] * 2,
        out_shape=[jax.ShapeDtypeStruct((length, dil * GROUP_W), F32)] * 2,
        compiler_params=_params(("arbitrary", "arbitrary"), 32),
    )(qkv, qkv, qkv, qkv, qkv)


def _attn_bwd(qkv, dy, y, lse, cos_t, sin_t, dil, rider=None):
    length = qkv.shape[0]
    tq = _attn_tile(length)
    nsub = tq // QBLK
    nblk = length // tq

    def body(q_ref, k_ref, v_ref, kp_ref, vp_ref, qn_ref, dy_ref, y_ref, l_ref, dyn_ref, yn_ref, ln_ref,
             cos_ref, sin_ref, out_ref, dq_s, dk_s, dv_s):
        n = pl.program_id(1)
        mask_c, mask_p0 = _band_masks()
        hmask = _head_masks((QBLK, GROUP_W))
        zero = jnp.zeros((), BF16)

        def head_terms(h, q, dyv, yv, lv, kk, vv, mask):
            hm = hmask[h]
            delta = jnp.sum(jnp.where(hm, dyv * yv, 0.0), axis=-1, keepdims=True)
            lh = jnp.max(jnp.where(hm, lv, NEG), axis=-1, keepdims=True)
            s = _dot_nt(q, jnp.where(hm, kk, zero)) * SCALE
            p = jnp.exp(jnp.where(mask, s - lh, NEG))
            dyb = jnp.where(hm, dyv, 0.0).astype(BF16)
            dp = _dot_nt(dyb, vv)
            ds = p * (dp - delta)
            return p.astype(BF16), ds.astype(BF16), dyb, jnp.where(hm, q, zero)

        dk_s[...] = jnp.zeros(dk_s.shape, F32)
        dv_s[...] = jnp.zeros(dv_s.shape, F32)
        for b in range(nsub):
            rows = slice(b * QBLK, (b + 1) * QBLK)
            q, dyv, yv, lv = q_ref[rows, :], dy_ref[rows, :], y_ref[rows, :], l_ref[rows, :]
            kc, vc = k_ref[rows, :], v_ref[rows, :]
            if b == 0:
                kp, vp = kp_ref[...], vp_ref[...]
                mask_p = mask_p0 & (n > 0)
            else:
                prow = slice((b - 1) * QBLK, b * QBLK)
                kp, vp = k_ref[prow, :], v_ref[prow, :]
                mask_p = mask_p0
            dq = jnp.zeros((QBLK, GROUP_W), F32)
            for h in range(HEADS_PER_GROUP):
                hm = hmask[h]
                pc, dsc, dyb, qh = head_terms(h, q, dyv, yv, lv, kc, vc, mask_c)
                pp, dsp, _, _ = head_terms(h, q, dyv, yv, lv, kp, vp, mask_p)
                dq = dq + (_dot(dsc, jnp.where(hm, kc, zero)) + _dot(dsp, jnp.where(hm, kp, zero))) * SCALE
                dv_s[rows, :] += _dot_tn(pc, dyb)
                dk_s[rows, :] += _dot_tn(dsc, qh) * SCALE
                if b > 0:
                    dv_s[prow, :] += _dot_tn(pp, dyb)
                    dk_s[prow, :] += _dot_tn(dsp, qh) * SCALE
            dq_s[rows, :] = dq
        rows = slice((nsub - 1) * QBLK, nsub * QBLK)
        qn, dyn, yn, ln = qn_ref[...], dyn_ref[...], yn_ref[...], ln_ref[...]
        mask_n = mask_p0 & (n < nblk - 1)
        for h in range(HEADS_PER_GROUP):
            pn, dsn, dyb, qh = head_terms(h, qn, dyn, yn, ln, k_ref[rows, :], v_ref[rows, :], mask_n)
            dv_s[rows, :] += _dot_tn(pn, dyb)
            dk_s[rows, :] += _dot_tn(dsn, qh) * SCALE
        cos, sin = cos_ref[...], sin_ref[...]
        dq, dk = dq_s[...], dk_s[...]
        out_ref[:, 0:GROUP_W] = (dq * cos - _rot_half(dq) * sin).astype(BF16)
        out_ref[:, GROUP_W:2 * GROUP_W] = (dk * cos - _rot_half(dk) * sin).astype(BF16)
        out_ref[:, 2 * GROUP_W:3 * GROUP_W] = dv_s[...].astype(BF16)

    cur = lambda sec: pl.BlockSpec((tq, GROUP_W), lambda r, n: (n, r * 3 + sec))
    prev = lambda sec: pl.BlockSpec((QBLK, GROUP_W), lambda r, n: (jnp.maximum(n * nsub - 1, 0), r * 3 + sec))
    nxt_q = pl.BlockSpec((QBLK, GROUP_W), lambda r, n: (jnp.minimum((n + 1) * nsub, nblk * nsub - 1), r * 3))
    tok = pl.BlockSpec((tq, GROUP_W), lambda r, n: (n, r))
    tok_next = pl.BlockSpec((QBLK, GROUP_W), lambda r, n: (jnp.minimum((n + 1) * nsub, nblk * nsub - 1), r))
    (out,), riding = _call(
        body, name=f"attn_bwd_d{dil}", grid=(dil, nblk),
        in_specs=[cur(0), cur(1), cur(2), prev(1), prev(2), nxt_q,
                  tok, tok, tok, tok_next, tok_next, tok_next, tok, tok],
        out_specs=[pl.BlockSpec((tq, 3 * GROUP_W), lambda r, n: (n, r))],
        out_shape=[jax.ShapeDtypeStruct((length, dil * 3 * GROUP_W), BF16)],
        scratch_shapes=[pltpu.VMEM((tq, GROUP_W), F32)] * 3,
        params=_params(("arbitrary", "arbitrary"), 32),
        args=(qkv, qkv, qkv, qkv, qkv, qkv, dy, y, lse, dy, y, lse, cos_t, sin_t), rider=rider)
    return out, riding


def _layernorm_stats(z):
    mu = jnp.mean(z, axis=-1, keepdims=True)
    zc = z - mu
    rstd = lax.rsqrt(jnp.mean(zc * zc, axis=-1, keepdims=True) + EPS)
    return zc * rstd, rstd


def _tril_mask():
    row = lax.broadcasted_iota(jnp.int32, (CHUNK, CHUNK), 0)
    col = lax.broadcasted_iota(jnp.int32, (CHUNK, CHUNK), 1)
    return col <= row


def _mix_fwd(o_l, rest, x, w_sp, b_col, ln_g, ln_b, w_ba, w_bg, w_out, g1):
    seq = x.shape[0]
    tm = 256

    def body(o0, l0, o1, l1, o2, l2, up_ref, zp_ref, gap_ref, gbp_ref, x_ref, wsp_ref, bcol_ref, lg_ref, lb_ref,
             wba_ref, wbg_ref, wout_ref, g1_ref, ya0, lj0, ya1, lj1, ya2, lj2, yg_ref, mg_ref, y_ref, x1_ref, slab):
        outs = [_get_tokens(slab, o, d, GROUP_W, 0, GROUP_W) for o, d in zip((o0, o1, o2), DILATIONS)]
        lses = [_get_tokens(slab, l, d, GROUP_W, 0, GROUP_W) for l, d in zip((l0, l1, l2), DILATIONS)]
        m = jnp.maximum(jnp.maximum(lses[0], lses[1]), lses[2])
        es = [jnp.exp(l - m) for l in lses]
        tot = es[0] + es[1] + es[2]
        ya = (es[0] * outs[0] + es[1] * outs[1] + es[2] * outs[2]) / tot
        lj = m + jnp.log(tot)
        for ya_ref, lj_ref, d in zip((ya0, ya1, ya2), (lj0, lj1, lj2), DILATIONS):
            _put_residue(slab, ya, ya_ref, d, GROUP_W, 0)
            _put_residue(slab, lj, lj_ref, d, GROUP_W, 0)
        zhat, _ = _layernorm_stats(_gelu(zp_ref[...].astype(F32)))
        zln = (zhat * lg_ref[...] + lb_ref[...]).astype(BF16)
        u = _gelu(up_ref[...].astype(F32))
        tril = _tril_mask()
        for g in range(GMLP_GROUPS):
            wm = jnp.where(tril, wsp_ref[g], 0.0).astype(BF16)
            cols = slice(g * CHUNK, (g + 1) * CHUNK)
            for c in range(tm // CHUNK):
                rows = slice(c * CHUNK, (c + 1) * CHUNK)
                sz = _dot(wm, zln[rows, cols]) + bcol_ref[g]
                yg_ref[rows, cols] = (u[rows, cols] * sz).astype(BF16)
        a = _dot(ya.astype(BF16), wba_ref[...])
        bm = _dot(yg_ref[...], wbg_ref[...])
        merged = (jax.nn.sigmoid(gap_ref[...].astype(F32)) * a + jax.nn.sigmoid(gbp_ref[...].astype(F32)) * bm).astype(BF16)
        mg_ref[...] = merged
        yv = _dot(merged, wout_ref[...])
        y_ref[...] = yv
        x1_ref[...] = x_ref[...] + (yv * _rsqrt_ms(yv)) * g1_ref[...]

    tok = lambda w: pl.BlockSpec((tm, w), lambda i: (i, 0))
    res = lambda d: pl.BlockSpec((tm // d, d * GROUP_W), lambda i: (i, 0))
    full = lambda *s: pl.BlockSpec(s, lambda i: (0,) * len(s))
    res_specs = [res(d) for d in DILATIONS for _ in range(2)]
    return pl.pallas_call(
        body, name="mix_fwd", grid=(seq // tm,),
        in_specs=res_specs + [
            pl.BlockSpec((tm, GMLP_W), lambda i: (i, 0)), pl.BlockSpec((tm, GMLP_W), lambda i: (i, 1)),
            pl.BlockSpec((tm, D_MODEL), lambda i: (i, 1)), pl.BlockSpec((tm, D_MODEL), lambda i: (i, 2)),
            tok(D_MODEL), full(GMLP_GROUPS, CHUNK, CHUNK), full(GMLP_GROUPS, CHUNK, 1), full(1, GMLP_W), full(1, GMLP_W),
            full(GROUP_W, D_MODEL), full(GMLP_W, D_MODEL), full(D_MODEL, D_MODEL), full(1, D_MODEL)],
        out_specs=res_specs + [tok(GMLP_W), tok(D_MODEL), tok(D_MODEL), tok(D_MODEL)],
        out_shape=[jax.ShapeDtypeStruct((seq // d, d * GROUP_W), F32) for d in DILATIONS for _ in range(2)]
        + [jax.ShapeDtypeStruct((seq, GMLP_W), BF16), jax.ShapeDtypeStruct((seq, D_MODEL), BF16),
           jax.ShapeDtypeStruct((seq, D_MODEL), F32), jax.ShapeDtypeStruct((seq, D_MODEL), F32)],
        scratch_shapes=[pltpu.VMEM((GROUP_W // LANES, tm, LANES), F32)],
        compiler_params=_params(("arbitrary",), 48),
    )(*o_l, rest, rest, rest, rest, x, w_sp, b_col, ln_g, ln_b, w_ba, w_bg, w_out, g1)


def _mlp_fwd(x1, g2, g3, w_mi, w_mo, target):
    seq = x1.shape[0]
    tm, tf = MLP_TM, 512

    def body(x1_ref, g2_ref, g3_ref, wmi_ref, wmo_ref, t_ref, h2_ref, a_ref, dy2_ref, dout_ref, loss_ref, dg3_ref, sq_s):
        @pl.when(pl.program_id(0) == 0)
        def _():
            loss_ref[...] = jnp.zeros(loss_ref.shape, F32)
            dg3_ref[...] = jnp.zeros(dg3_ref.shape, F32)

        xv = x1_ref[...]
        hb = ((xv * _rsqrt_ms(xv)) * g2_ref[...]).astype(BF16)
        h2_ref[...] = hb
        for j in range(D_FF // tf):
            cols = slice(j * tf, (j + 1) * tf)
            a = jnp.maximum(_dot(hb, wmi_ref[:, cols]), 0.0)
            a_ref[:, cols] = a.astype(BF16)
            sq_s[:, cols] = (a * a).astype(BF16)
        y2 = _dot(sq_s[...], wmo_ref[...])
        r3 = _rsqrt_ms(y2)
        out = xv + (y2 * r3) * g3_ref[...]
        diff = out - t_ref[...]
        tile_loss = 0.5 * jnp.sum(jnp.mean(diff * diff, axis=-1, keepdims=True), axis=0, keepdims=True)
        loss_ref[...] += jnp.broadcast_to(tile_loss, loss_ref.shape)
        dout = diff * (1.0 / D_MODEL)
        dout_ref[...] = dout
        dy2, dg3 = _rmsnorm_bwd(dout, y2, g3_ref[...])
        dy2_ref[...] = dy2.astype(BF16)
        dg3_ref[...] += dg3

    tok = lambda w: pl.BlockSpec((tm, w), lambda i: (i, 0))
    vec = pl.BlockSpec((1, D_MODEL), lambda i: (0, 0))
    return pl.pallas_call(
        body, name="mlp_fwd", grid=(seq // tm,),
        in_specs=[tok(D_MODEL), vec, vec, _resident((D_MODEL, D_FF)), _resident((D_FF, D_MODEL)), tok(D_MODEL)],
        out_specs=[tok(D_MODEL), tok(D_FF), tok(D_MODEL), tok(D_MODEL), pl.BlockSpec((8, 128), lambda i: (0, 0)), vec],
        out_shape=[jax.ShapeDtypeStruct((seq, D_MODEL), BF16), jax.ShapeDtypeStruct((seq, D_FF), BF16),
                   jax.ShapeDtypeStruct((seq, D_MODEL), BF16), jax.ShapeDtypeStruct((seq, D_MODEL), F32),
                   jax.ShapeDtypeStruct((8, 128), F32), jax.ShapeDtypeStruct((1, D_MODEL), F32)],
        scratch_shapes=[pltpu.VMEM((tm, D_FF), BF16)],
        compiler_params=_params(("arbitrary",), 56),
    )(x1, g2, g3, w_mi, w_mo, target)


def _mlp_bwd(dy2, a, w_mo, w_mi, dout, x1, y, g2, g1):
    seq = x1.shape[0]
    tm, tf = MLP_TM, 512

    def body(dy2_ref, a_ref, wmo_ref, wmi_ref, dout_ref, x1_ref, y_ref, g2_ref, g1_ref,
             dap_ref, dx1_ref, dy_ref, dg2_ref, dg1_ref):
        @pl.when(pl.program_id(0) == 0)
        def _():
            dg2_ref[...] = jnp.zeros(dg2_ref.shape, F32)
            dg1_ref[...] = jnp.zeros(dg1_ref.shape, F32)

        dy2v = dy2_ref[...]
        for j in range(D_FF // tf):
            cols = slice(j * tf, (j + 1) * tf)
            da2 = _dot_nt(dy2v, wmo_ref[cols, :])
            dap_ref[:, cols] = (da2 * (2.0 * a_ref[:, cols].astype(F32))).astype(BF16)
        dh2 = _dot_nt(dap_ref[...], wmi_ref[...])
        dres, dg2 = _rmsnorm_bwd(dh2, x1_ref[...], g2_ref[...])
        dx1 = dout_ref[...] + dres
        dx1_ref[...] = dx1
        dg2_ref[...] += dg2
        dyv, dg1 = _rmsnorm_bwd(dx1, y_ref[...], g1_ref[...])
        dy_ref[...] = dyv.astype(BF16)
        dg1_ref[...] += dg1

    tok = lambda w: pl.BlockSpec((tm, w), lambda i: (i, 0))
    vec = pl.BlockSpec((1, D_MODEL), lambda i: (0, 0))
    return pl.pallas_call(
        body, name="mlp_bwd", grid=(seq // tm,),
        in_specs=[tok(D_MODEL), tok(D_FF), _resident((D_FF, D_MODEL)), _resident((D_MODEL, D_FF)),
                  tok(D_MODEL), tok(D_MODEL), tok(D_MODEL), vec, vec],
        out_specs=[tok(D_FF), tok(D_MODEL), tok(D_MODEL), vec, vec],
        out_shape=[jax.ShapeDtypeStruct((seq, D_FF), BF16), jax.ShapeDtypeStruct((seq, D_MODEL), F32),
                   jax.ShapeDtypeStruct((seq, D_MODEL), BF16), jax.ShapeDtypeStruct((1, D_MODEL), F32),
                   jax.ShapeDtypeStruct((1, D_MODEL), F32)],
        compiler_params=_params(("arbitrary",), 56),
    )(dy2, a, w_mo, w_mi, dout, x1, y, g2, g1)


def _tn_matmul(a, b, name, bm, bn, square_a=False, column_shards=False):
    seq, m = a.shape
    n = b.shape[1]
    ts = 512

    def body(a_ref, b_ref, o_ref):
        @pl.when(pl.program_id(2) == 0)
        def _():
            o_ref[...] = jnp.zeros(o_ref.shape, F32)

        av = a_ref[...]
        if square_a:
            af = av.astype(F32)
            av = (af * af).astype(BF16)
        o_ref[...] += _dot_tn(av, b_ref[...])

    if column_shards:
        out_spec = pl.BlockSpec((None, bm, bn), lambda mi, ni, s: (ni, mi, 0))
        out_shape = jax.ShapeDtypeStruct((n // bn, m, bn), F32)
    else:
        out_spec = pl.BlockSpec((bm, bn), lambda mi, ni, s: (mi, ni))
        out_shape = jax.ShapeDtypeStruct((m, n), F32)
    return pl.pallas_call(
        body, name=name, grid=(m // bm, n // bn, seq // ts),
        in_specs=[pl.BlockSpec((ts, bm), lambda mi, ni, s: (s, mi)), pl.BlockSpec((ts, bn), lambda mi, ni, s: (s, ni))],
        out_specs=out_spec, out_shape=out_shape,
        compiler_params=_params(("arbitrary", "arbitrary", "arbitrary"), 40),
    )(a, b)


def _tn_matmul_residue(a, b, dil, name):
    length = a.shape[0]
    m, n = a.shape[1] // dil, b.shape[1] // dil
    ts = min(512, length)

    def body(a_ref, b_ref, o_ref):
        @pl.when((pl.program_id(0) == 0) & (pl.program_id(1) == 0))
        def _():
            o_ref[...] = jnp.zeros(o_ref.shape, F32)

        o_ref[...] += _dot_tn(a_ref[...], b_ref[...])

    return pl.pallas_call(
        body, name=name, grid=(dil, length // ts),
        in_specs=[pl.BlockSpec((ts, m), lambda r, s: (s, r)), pl.BlockSpec((ts, n), lambda r, s: (s, r))],
        out_specs=pl.BlockSpec((m, n), lambda r, s: (0, 0)),
        out_shape=jax.ShapeDtypeStruct((m, n), F32),
        compiler_params=_params(("arbitrary", "arbitrary"), 40),
    )(a, b)


def _mix_bwd(dy, ya, yg, mg, rest, w_out, w_ba, w_bg, w_sp, b_col, ln_g, ln_b, rider=None):
    seq = dy.shape[0]
    tm = 256

    def body(dy_ref, ya_ref, yg_ref, mg_ref, up_ref, zp_ref, gap_ref, gbp_ref, wout_ref, wba_ref, wbg_ref,
             wsp_ref, bcol_ref, lg_ref, lb_ref,
             dya0, dya1, dya2, dpr_ref, dwout_ref, dwba_ref, dwbg_ref, dwsp_ref, dbb_ref, dlg_ref, dlb_ref,
             dzln_s, du_s, slab):
        @pl.when(pl.program_id(0) == 0)
        def _():
            for ref in (dwout_ref, dwba_ref, dwbg_ref, dwsp_ref, dbb_ref, dlg_ref, dlb_ref):
                ref[...] = jnp.zeros(ref.shape, F32)

        dyv = dy_ref[...]
        dm = _dot_nt(dyv, wout_ref[...])
        dwout_ref[...] += _dot_tn(mg_ref[...], dyv)
        yab = ya_ref[...].astype(BF16)
        ygb = yg_ref[...]
        a = _dot(yab, wba_ref[...])
        bm = _dot(ygb, wbg_ref[...])
        ga = jax.nn.sigmoid(gap_ref[...].astype(F32))
        gb = jax.nn.sigmoid(gbp_ref[...].astype(F32))
        dpr_ref[:, 2 * GMLP_W:2 * GMLP_W + D_MODEL] = (dm * a * (ga * (1.0 - ga))).astype(BF16)
        dpr_ref[:, 2 * GMLP_W + D_MODEL:REST_W] = (dm * bm * (gb * (1.0 - gb))).astype(BF16)
        da = (dm * ga).astype(BF16)
        db = (dm * gb).astype(BF16)
        dwba = _dot_tn(yab, da)
        dwbg = _dot_tn(ygb, db)
        shard_w = D_MODEL // N_CHIPS
        for j in range(N_CHIPS):
            dwba_ref[j] += dwba[:, j * shard_w:(j + 1) * shard_w]
            dwbg_ref[j] += dwbg[:, j * shard_w:(j + 1) * shard_w]
        dya = _dot_nt(da, wba_ref[...])
        for dya_ref, d in zip((dya0, dya1, dya2), DILATIONS):
            _put_residue(slab, dya, dya_ref, d, GROUP_W, 0)
        dyg = _dot_nt(db, wbg_ref[...])

        zp = zp_ref[...].astype(F32)
        zhat, rstd = _layernorm_stats(_gelu(zp))
        lg = lg_ref[...]
        zln = (zhat * lg + lb_ref[...]).astype(BF16)
        up = up_ref[...].astype(F32)
        u = _gelu(up)
        tril = _tril_mask()
        for g in range(GMLP_GROUPS):
            wm = jnp.where(tril, wsp_ref[g], 0.0).astype(BF16)
            cols = slice(g * CHUNK, (g + 1) * CHUNK)
            for c in range(tm // CHUNK):
                rows = slice(c * CHUNK, (c + 1) * CHUNK)
                zb = zln[rows, cols]
                sz = _dot(wm, zb) + bcol_ref[g]
                dyg_cg = dyg[rows, cols]
                du_s[rows, cols] = dyg_cg * sz
                dsz = dyg_cg * u[rows, cols]
                dszb = dsz.astype(BF16)
                dbb_ref[g] += jnp.broadcast_to(jnp.sum(dsz, axis=-1, keepdims=True), (CHUNK, CHUNK))
                dwsp_ref[g] += jnp.where(tril, _dot_nt(dszb, zb), 0.0)
                dzln_s[rows, cols] = _dot_tn(wm, dszb)
        dzln = dzln_s[...]
        dlg_ref[...] += jnp.sum(dzln * zhat, axis=0, keepdims=True)
        dlb_ref[...] += jnp.sum(dzln, axis=0, keepdims=True)
        dzh = dzln * lg
        dz = rstd * (dzh - jnp.mean(dzh, axis=-1, keepdims=True) - zhat * jnp.mean(dzh * zhat, axis=-1, keepdims=True))
        dpr_ref[:, GMLP_W:2 * GMLP_W] = (dz * _gelu_grad(zp)).astype(BF16)
        dpr_ref[:, 0:GMLP_W] = (du_s[...] * _gelu_grad(up)).astype(BF16)

    tok = lambda w: pl.BlockSpec((tm, w), lambda i: (i, 0))
    full = lambda *s: pl.BlockSpec(s, lambda i: (0,) * len(s))
    return _call(
        body, name="mix_bwd", grid=(seq // tm,),
        in_specs=[tok(D_MODEL), tok(GROUP_W), tok(GMLP_W), tok(D_MODEL),
                  pl.BlockSpec((tm, GMLP_W), lambda i: (i, 0)), pl.BlockSpec((tm, GMLP_W), lambda i: (i, 1)),
                  pl.BlockSpec((tm, D_MODEL), lambda i: (i, 1)), pl.BlockSpec((tm, D_MODEL), lambda i: (i, 2)),
                  full(D_MODEL, D_MODEL), full(GROUP_W, D_MODEL), full(GMLP_W, D_MODEL),
                  full(GMLP_GROUPS, CHUNK, CHUNK), full(GMLP_GROUPS, CHUNK, 1), full(1, GMLP_W), full(1, GMLP_W)],
        out_specs=[pl.BlockSpec((tm // d, d * GROUP_W), lambda i: (i, 0)) for d in DILATIONS]
        + [tok(REST_W), full(D_MODEL, D_MODEL), full(N_CHIPS, GROUP_W, D_MODEL // N_CHIPS),
           full(N_CHIPS, GMLP_W, D_MODEL // N_CHIPS),
           full(GMLP_GROUPS, CHUNK, CHUNK), full(GMLP_GROUPS, CHUNK, CHUNK), full(1, GMLP_W), full(1, GMLP_W)],
        out_shape=[jax.ShapeDtypeStruct((seq // d, d * GROUP_W), F32) for d in DILATIONS]
        + [jax.ShapeDtypeStruct((seq, REST_W), BF16),
           jax.ShapeDtypeStruct((D_MODEL, D_MODEL), F32), jax.ShapeDtypeStruct((N_CHIPS, GROUP_W, D_MODEL // N_CHIPS), F32),
           jax.ShapeDtypeStruct((N_CHIPS, GMLP_W, D_MODEL // N_CHIPS), F32),
           jax.ShapeDtypeStruct((GMLP_GROUPS, CHUNK, CHUNK), F32),
           jax.ShapeDtypeStruct((GMLP_GROUPS, CHUNK, CHUNK), F32), jax.ShapeDtypeStruct((1, GMLP_W), F32),
           jax.ShapeDtypeStruct((1, GMLP_W), F32)],
        scratch_shapes=[pltpu.VMEM((tm, GMLP_W), F32), pltpu.VMEM((tm, GMLP_W), F32),
                        pltpu.VMEM((GROUP_W // LANES, tm, LANES), F32)],
        params=_params(("arbitrary",), 56),
        args=(dy, ya, yg, mg, rest, rest, rest, rest, w_out, w_ba, w_bg, w_sp, b_col, ln_g, ln_b), rider=rider)


def _in_proj_bwd(dqkv, drest, w_qkv, w_rest, x, dx1, g0, so_far, part, parts, rider=None):
    seq = x.shape[0]
    tm = 256
    steps = seq // tm // parts
    off = part * steps
    gx_so_far, dg_so_far = so_far

    def body(d0, d1, d2, dr_ref, w0, w1, w2, wr_ref, x_ref, dx1_ref, g_ref, dg_in_ref, gx_in_ref, gx_ref, dg_ref, slab):
        @pl.when(pl.program_id(0) == 0)
        def _():
            dg_ref[...] = dg_in_ref[...]

        dh = _dot_nt(dr_ref[...], wr_ref[...])
        for d_ref, w_ref, dil in zip((d0, d1, d2), (w0, w1, w2), DILATIONS):
            piece = d_ref[...] if dil == 1 else _get_tokens(slab, d_ref, dil, 3 * GROUP_W, 0, 3 * GROUP_W).astype(BF16)
            dh = dh + _dot_nt(piece, w_ref[...])
        dres, dg = _rmsnorm_bwd(dh, x_ref[...], g_ref[...])
        gx_ref[...] = dx1_ref[...] + dres
        dg_ref[...] += dg

    tok = lambda w: pl.BlockSpec((tm, w), lambda i: (i + off, 0))
    full = lambda *s: pl.BlockSpec(s, lambda i: (0,) * len(s))
    in_specs = ([pl.BlockSpec((tm // d, d * 3 * GROUP_W), lambda i: (i + off, 0)) for d in DILATIONS] + [tok(REST_W)]
                + [_resident((D_MODEL, 3 * GROUP_W))] * 3 + [_resident((D_MODEL, REST_W))]
                + [tok(D_MODEL), tok(D_MODEL), full(1, D_MODEL), full(1, D_MODEL), HBM_SPEC])
    return _call(
        body, name=f"in_proj_bwd_{part}", grid=(steps,), in_specs=in_specs,
        out_specs=[tok(D_MODEL), full(1, D_MODEL)],
        out_shape=[jax.ShapeDtypeStruct((seq, D_MODEL), F32), jax.ShapeDtypeStruct((1, D_MODEL), F32)],
        scratch_shapes=[pltpu.VMEM((3 * GROUP_W // LANES, tm, LANES), F32)],
        params=_params(("arbitrary",), 48), args=(*dqkv, drest, *w_qkv, w_rest, x, dx1, g0, dg_so_far, gx_so_far),
        rider=rider, aliases={len(in_specs) - 1: 0})


def _adamw(w, g, m, v, name):
    rows, cols = w.shape
    tr = 256 if rows % 256 == 0 else rows
    c1 = 1.0 - ADAM_B1 ** ADAM_STEP
    c2 = 1.0 - ADAM_B2 ** ADAM_STEP

    def body(w_ref, g_ref, m_ref, v_ref, d_ref, nm_ref, nv_ref):
        gv = g_ref[...]
        nm = ADAM_B1 * m_ref[...] + (1.0 - ADAM_B1) * gv
        nv = ADAM_B2 * v_ref[...] + (1.0 - ADAM_B2) * (gv * gv)
        d_ref[...] = -ADAM_LR * ((nm / c1) / (jnp.sqrt(nv / c2) + ADAM_EPS) + ADAM_WD * w_ref[...])
        nm_ref[...] = nm
        nv_ref[...] = nv

    spec = pl.BlockSpec((tr, cols), lambda i: (i, 0))
    return pl.pallas_call(
        body, name=name, grid=(rows // tr,),
        in_specs=[spec] * 4, out_specs=[spec] * 3,
        out_shape=[jax.ShapeDtypeStruct((rows, cols), F32)] * 3,
        compiler_params=_params(("arbitrary",), 40),
    )(w, g, m, v)


def _place():
    x, y, c = lax.axis_index("x"), lax.axis_index("y"), lax.axis_index("c")
    chips = [(1 - x, y), (x, 1 - y), (1 - x, 1 - y)]
    return x, y, c, chips


class _Exchange:
    def __init__(self, inputs, out_shapes, n_sems, start, finish, aliases=None):
        self.inputs, self.out_shapes, self.n_sems = list(inputs), list(out_shapes), n_sems
        self.start, self.finish, self.aliases = start, finish, dict(aliases or {})

    def scratch(self):
        return [pltpu.SemaphoreType.DMA((self.n_sems,)), pltpu.SemaphoreType.DMA((self.n_sems,))]


def _together(*parts):
    ins = [len(p.inputs) for p in parts]
    outs = [len(p.out_shapes) for p in parts]

    def split(refs, counts):
        pos, pieces = 0, []
        for cnt in counts:
            pieces.append(refs[pos:pos + cnt])
            pos += cnt
        return pieces

    def run(which):
        def go(in_refs, out_refs, *sems):
            for k, (p, i, o) in enumerate(zip(parts, split(in_refs, ins), split(out_refs, outs))):
                getattr(p, which)(i, o, sems[2 * k], sems[2 * k + 1])
        return go

    both = _Exchange([a for p in parts for a in p.inputs], [s for p in parts for s in p.out_shapes], 0, run("start"),
                     run("finish"))
    both.aliases = {sum(ins[:k]) + i: sum(outs[:k]) + o for k, p in enumerate(parts) for i, o in p.aliases.items()}
    both.scratch = lambda: [s for p in parts for s in p.scratch()]
    return both


def _run_exchange(ex, name):
    n_in, n_out = len(ex.inputs), len(ex.out_shapes)

    def body(*refs):
        ins, outs, sems = refs[:n_in], refs[n_in:n_in + n_out], refs[n_in + n_out:]
        ex.start(ins, outs, *sems)
        ex.finish(ins, outs, *sems)

    return pl.pallas_call(
        body, name=name, in_specs=[HBM_SPEC] * n_in, out_specs=[HBM_SPEC] * n_out, out_shape=ex.out_shapes,
        scratch_shapes=ex.scratch(), input_output_aliases=ex.aliases,
    )(*ex.inputs)


def _call(body, *, name, grid, in_specs, out_specs, out_shape, scratch_shapes, params, args, rider=None, aliases=None):
    in_specs, out_specs, out_shape, scratch_shapes = list(in_specs), list(out_specs), list(out_shape), list(scratch_shapes)
    aliases = dict(aliases or {})
    if rider is None:
        outs = pl.pallas_call(body, name=name, grid=grid, in_specs=in_specs, out_specs=out_specs, out_shape=out_shape,
                              scratch_shapes=scratch_shapes, input_output_aliases=aliases, compiler_params=params)(*args)
        return list(outs), []
    n_in, n_out, n_scr = len(in_specs), len(out_specs), len(scratch_shapes)
    r_in, r_out = len(rider.inputs), len(rider.out_shapes)

    def wrapped(*refs):
        ins, r_ins = refs[:n_in], refs[n_in:n_in + r_in]
        pos = n_in + r_in
        outs, r_outs = refs[pos:pos + n_out], refs[pos + n_out:pos + n_out + r_out]
        pos += n_out + r_out
        scr, sems = refs[pos:pos + n_scr], refs[pos + n_scr:]
        ids = [pl.program_id(k) for k in range(len(grid))]
        first, last = ids[0] == 0, ids[0] == grid[0] - 1
        for k in range(1, len(grid)):
            first, last = first & (ids[k] == 0), last & (ids[k] == grid[k] - 1)

        @pl.when(first)
        def _():
            rider.start(r_ins, r_outs, *sems)

        body(*ins, *outs, *scr)

        @pl.when(last)
        def _():
            rider.finish(r_ins, r_outs, *sems)

    outs = pl.pallas_call(
        wrapped, name=name, grid=grid, in_specs=in_specs + [HBM_SPEC] * r_in, out_specs=out_specs + [HBM_SPEC] * r_out,
        out_shape=out_shape + rider.out_shapes, scratch_shapes=scratch_shapes + rider.scratch(),
        input_output_aliases={**aliases, **{n_in + i: n_out + o for i, o in rider.aliases.items()}}, compiler_params=params,
    )(*args, *rider.inputs)
    return list(outs[:n_out]), list(outs[n_out:])


def _stage_weights(shards):
    n = len(shards)

    def body(*refs):
        ins, outs, stages, sems = refs[:n], refs[n:2 * n], refs[2 * n:3 * n], refs[3 * n]
        x, y, _, _ = _place()
        copies = []
        for t in range(n):
            stages[t][...] = ins[t][...].astype(BF16)
            copies.append(pltpu.make_async_copy(stages[t], outs[t].at[2 * x + y], sems.at[t]))
            copies[-1].start()
        for cp in copies:
            cp.wait()

    stage_bytes = sum(s.size * 6 for s in shards)
    return pl.pallas_call(
        body, name="stage_weights", in_specs=[VMEM_SPEC] * n, out_specs=[HBM_SPEC] * n,
        out_shape=[jax.ShapeDtypeStruct((N_CHIPS,) + s.shape, BF16) for s in shards],
        scratch_shapes=[pltpu.VMEM(s.shape, BF16) for s in shards] + [pltpu.SemaphoreType.DMA((n,))],
        compiler_params=pltpu.CompilerParams(vmem_limit_bytes=stage_bytes + 8 * MIB),
    )(*shards)


def _gather(buffers):
    n = len(buffers)
    halves = [b.shape[1] // 2 for b in buffers]

    def half_of(outs, t, chip, which):
        return outs[t].at[chip, pl.ds(which * halves[t], halves[t]), :]

    def copy(outs, sems, t, k, chip, which, to):
        rows = half_of(outs, t, chip, which)
        return pltpu.make_async_remote_copy(src_ref=rows, dst_ref=rows, send_sem=sems[0].at[6 * t + k],
                                            recv_sem=sems[1].at[6 * t + k], device_id=to, device_id_type=MESH)

    def start(ins, outs, *sems):
        x, y, c, chips = _place()
        for t in range(n):
            for j, (px, py) in enumerate(chips):
                copy(outs, sems, t, j, 2 * x + y, c, (px, py, c)).start()

    def finish(ins, outs, *sems):
        x, y, c, chips = _place()
        sibling = (x, y, 1 - c)
        for j, (px, py) in enumerate(chips):
            for t in range(n):
                copy(outs, sems, t, j, 2 * px + py, c, (px, py, c)).wait_recv()
                copy(outs, sems, t, 3 + j, 2 * px + py, c, sibling).start()
        for j, (px, py) in enumerate(chips):
            for t in range(n):
                copy(outs, sems, t, 3 + j, 2 * px + py, 1 - c, sibling).wait_recv()
        for j, (px, py) in enumerate(chips):
            for t in range(n):
                copy(outs, sems, t, j, 2 * x + y, c, (px, py, c)).wait_send()
                copy(outs, sems, t, 3 + j, 2 * px + py, c, sibling).wait_send()

    return _Exchange(buffers, [jax.ShapeDtypeStruct(b.shape, b.dtype) for b in buffers], 6 * n, start, finish,
                     aliases={t: t for t in range(n)})


def _pair_exchange(grads):
    n = len(grads)
    halves = [g.shape[1] // 2 for g in grads]

    def copies(ins, outs, send_sems, recv_sems):
        x, y, c, _ = _place()
        return [pltpu.make_async_remote_copy(
            src_ref=ins[t].at[:, pl.ds((1 - c) * halves[t], halves[t]), :], dst_ref=outs[t],
            send_sem=send_sems.at[t], recv_sem=recv_sems.at[t], device_id=(x, y, 1 - c), device_id_type=MESH)
            for t in range(n)]

    def start(*refs):
        for cp in copies(*refs):
            cp.start()

    def finish(*refs):
        for cp in copies(*refs):
            cp.wait()

    return _Exchange(grads, [jax.ShapeDtypeStruct((N_CHIPS, h, g.shape[2]), F32) for g, h in zip(grads, halves)], n,
                     start, finish)


def _row_tile(rows):
    return min(rows, 256)


def _pair_add(grad, other, place, name):
    _, rows, cols = grad.shape
    rh = rows // 2
    tr = _row_tile(rh)
    nb = rh // tr

    def body(p_ref, g_ref, a_ref, wire_ref, own_ref):
        s = g_ref[...] + a_ref[...]
        wire_ref[...] = s.astype(BF16)

        @pl.when(pl.program_id(1) == p_ref[1])
        def _():
            own_ref[...] = s

    blk = (None, tr, cols)
    return pl.pallas_call(
        body, name=name,
        grid_spec=pltpu.PrefetchScalarGridSpec(
            num_scalar_prefetch=1, grid=(nb, N_CHIPS),
            in_specs=[pl.BlockSpec(blk, lambda i, j, p: (j, p[0] * nb + i, 0)), pl.BlockSpec(blk, lambda i, j, p: (j, i, 0))],
            out_specs=[pl.BlockSpec(blk, lambda i, j, p: (j, i, 0)), pl.BlockSpec((tr, cols), lambda i, j, p: (i, 0))]),
        out_shape=[jax.ShapeDtypeStruct((N_CHIPS, rh, cols), BF16), jax.ShapeDtypeStruct((rh, cols), F32)],
        compiler_params=_params(("arbitrary", "arbitrary"), 32),
    )(place, grad, other)


def _chip_exchange(wires):
    n = len(wires)

    def copies(ins, outs, send_sems, recv_sems):
        x, y, c, chips = _place()
        return [pltpu.make_async_remote_copy(
            src_ref=ins[t].at[2 * px + py], dst_ref=outs[t].at[j], send_sem=send_sems.at[3 * t + j],
            recv_sem=recv_sems.at[3 * t + j], device_id=(px, py, c), device_id_type=MESH)
            for t in range(n) for j, (px, py) in enumerate(chips)]

    def start(*refs):
        for cp in copies(*refs):
            cp.start()

    def finish(*refs):
        for cp in copies(*refs):
            cp.wait()

    return _Exchange(wires, [jax.ShapeDtypeStruct((3,) + w.shape[1:], BF16) for w in wires], 3 * n, start, finish)


def _chip_add(own, arrived, place, name):
    rh, cols = own.shape
    tr = _row_tile(rh)
    nb = rh // tr

    def body(p_ref, s_ref, b0, b1, b2, o_ref):
        o_ref[...] = ((s_ref[...] + b0[...].astype(F32)) + b1[...].astype(F32)) + b2[...].astype(F32)

    blk = (None, tr, cols)
    return pl.pallas_call(
        body, name=name,
        grid_spec=pltpu.PrefetchScalarGridSpec(
            num_scalar_prefetch=1, grid=(nb,),
            in_specs=[pl.BlockSpec((tr, cols), lambda i, p: (i, 0)), pl.BlockSpec(blk, lambda i, p: (0, i, 0)),
                      pl.BlockSpec(blk, lambda i, p: (1, i, 0)), pl.BlockSpec(blk, lambda i, p: (2, i, 0))],
            out_specs=pl.BlockSpec((tr, cols), lambda i, p: (p[0] * nb + i, 0))),
        out_shape=jax.ShapeDtypeStruct((2 * rh, cols), F32),
        compiler_params=_params(("arbitrary",), 32),
    )(place, own, arrived, arrived, arrived)


def _pair_share(halves):
    n = len(halves)
    rhs = [h.shape[0] // 2 for h in halves]

    def copy(outs, send_sems, recv_sems, t, which):
        x, y, c, _ = _place()
        rows = outs[t].at[pl.ds(which * rhs[t], rhs[t]), :]
        return pltpu.make_async_remote_copy(src_ref=rows, dst_ref=rows, send_sem=send_sems.at[t], recv_sem=recv_sems.at[t],
                                            device_id=(x, y, 1 - c), device_id_type=MESH)

    def start(ins, outs, send_sems, recv_sems):
        c = lax.axis_index("c")
        for t in range(n):
            copy(outs, send_sems, recv_sems, t, c).start()

    def finish(ins, outs, send_sems, recv_sems):
        c = lax.axis_index("c")
        for t in range(n):
            copy(outs, send_sems, recv_sems, t, c).wait_send()
            copy(outs, send_sems, recv_sems, t, 1 - c).wait_recv()

    return _Exchange(halves, [jax.ShapeDtypeStruct(h.shape, F32) for h in halves], n, start, finish,
                     aliases={t: t for t in range(n)})


class _GradReduction:
    def __init__(self, grads, place, tag):
        self.names, self.grads, self.place, self.tag = list(grads), grads, place, tag

    def pair_exchange(self):
        return _pair_exchange([self.grads[n] for n in self.names])

    def chip_exchange(self, others):
        sums = [_pair_add(self.grads[n], o, self.place, f"{self.tag}_pair_add_{n}") for n, o in zip(self.names, others)]
        self.owns = [own for _, own in sums]
        return _chip_exchange([wire for wire, _ in sums])

    def pair_share(self, arrived):
        return _pair_share([_chip_add(own, arr, self.place, f"{self.tag}_chip_add_{n}")
                            for n, own, arr in zip(self.names, self.owns, arrived)])

    def result(self, shared):
        return dict(zip(self.names, shared))


def _all_reduce_small(p):
    rows, lanes = p.shape
    flips = [(fx, fy, fc) for fx in (0, 1) for fy in (0, 1) for fc in (0, 1)][1:]

    def body(p_ref, o_ref, buf, send_sems, recv_sems):
        x, y, c, _ = _place()
        me = 4 * x + 2 * y + c
        buf[me] = p_ref[...]
        peers = [((1 - x) if fx else x, (1 - y) if fy else y, (1 - c) if fc else c) for fx, fy, fc in flips]
        cps = []
        for k, peer in enumerate(peers):
            cp = pltpu.make_async_remote_copy(
                src_ref=p_ref, dst_ref=buf.at[me], send_sem=send_sems.at[k], recv_sem=recv_sems.at[k],
                device_id=peer, device_id_type=MESH)
            cp.start()
            cps.append(cp)
        for k, (px, py, pc) in enumerate(peers):
            pltpu.make_async_remote_copy(
                src_ref=p_ref, dst_ref=buf.at[4 * px + 2 * py + pc], send_sem=send_sems.at[k], recv_sem=recv_sems.at[k],
                device_id=(px, py, pc), device_id_type=MESH).wait_recv()
        for cp in cps:
            cp.wait_send()
        acc = buf[0]
        for s in range(1, 8):
            acc = acc + buf[s]
        o_ref[...] = acc

    return pl.pallas_call(
        body, name="small_all_reduce", in_specs=[VMEM_SPEC], out_specs=VMEM_SPEC,
        out_shape=jax.ShapeDtypeStruct((rows, lanes), F32),
        scratch_shapes=[pltpu.VMEM((8, rows, lanes), F32), pltpu.SemaphoreType.DMA((7,)), pltpu.SemaphoreType.DMA((7,))],
        compiler_params=pltpu.CompilerParams(vmem_limit_bytes=32 * MIB),
    )(p)


BIG = ("w_in", "w_branch_attn", "w_branch_gmlp", "w_out", "w_mlp_in", "w_mlp_out")
COLUMN_SHARDED = ("w_in", "w_branch_attn", "w_branch_gmlp", "w_mlp_in")
SMALL = ("norm_pre_mix", "w_spatial", "b_spatial", "ln_v_gain", "ln_v_bias", "norm_post_mix", "norm_pre_mlp", "norm_post_mlp")
ORDER = ("norm_pre_mix", "w_in", "w_spatial", "b_spatial", "ln_v_gain", "ln_v_bias", "w_branch_attn", "w_branch_gmlp",
         "w_out", "norm_post_mix", "norm_pre_mlp", "w_mlp_in", "w_mlp_out", "norm_post_mlp")


def _full_weight(name, gathered):
    if name in COLUMN_SHARDED:
        return jnp.transpose(gathered, (1, 0, 2)).reshape(gathered.shape[1], -1)
    return gathered.reshape(-1, gathered.shape[2])


def _rows8(a):
    a = a.reshape(-1, 128)
    pad = (-a.shape[0]) % 8
    return jnp.pad(a, ((0, pad), (0, 0))) if pad else a


def _qkv_columns(group):
    return [(sec * ATTN_W + group * GROUP_W, sec * ATTN_W + (group + 1) * GROUP_W) for sec in range(3)]


def _device_step(x, target, small, shards, place):
    seq = x.shape[0]
    g0, g1, g2, g3 = small["norm_pre_mix"], small["norm_post_mix"], small["norm_pre_mlp"], small["norm_post_mlp"]
    w_sp = small["w_spatial"]
    b_col = small["b_spatial"].reshape(GMLP_GROUPS, CHUNK, 1)
    ln_g, ln_b = small["ln_v_gain"], small["ln_v_bias"]

    staged = _stage_weights(shards)
    w_in = _full_weight("w_in", _run_exchange(_gather(staged[:1]), "gather_w_in")[0])
    tables = _rope_tables(seq)
    (*hq, rest), gathered = _in_proj(x, g0, w_in, *tables[1], rider=_gather(staged[1:]))
    full = {n: _full_weight(n, gw) for n, gw in zip(BIG[1:], gathered)}
    h, qkv = hq[:N_GROUPS], hq[N_GROUPS:]

    o_l = []
    for g, dil in enumerate(DILATIONS):
        o_l.extend(_attn_fwd(qkv[g], dil))
    *ya_l, yg, mg, y, x1 = _mix_fwd(o_l, rest, x, w_sp, b_col, ln_g, ln_b, full["w_branch_attn"], full["w_branch_gmlp"],
                                    full["w_out"], g1)
    ya, lse = ya_l[0::2], ya_l[1::2]
    h2, a, dy2, dout, loss8, dg3 = _mlp_fwd(x1, g2, g3, full["w_mlp_in"], full["w_mlp_out"], target)
    dap, dx1, dy, dg2, dg1 = _mlp_bwd(dy2, a, full["w_mlp_out"], full["w_mlp_in"], dout, x1, y, g2, g1)
    d_wmo = _tn_matmul(a, dy2, "grad_w_mlp_out", 1024, 1024, square_a=True)
    d_wmi = _tn_matmul(h2, dap, "grad_w_mlp_in", 1024, 1024, column_shards=True)

    mlp = _GradReduction({"w_mlp_in": d_wmi, "w_mlp_out": d_wmo.reshape(N_CHIPS, D_FF // N_CHIPS, D_MODEL)}, place, "mlp")
    (*dya, drest, d_wout, d_wba, d_wbg, d_wsp, d_bb, d_lg, d_lb), mlp_others = _mix_bwd(
        dy, ya[0], yg, mg, rest, full["w_out"], full["w_branch_attn"], full["w_branch_gmlp"], w_sp, b_col, ln_g, ln_b,
        rider=mlp.pair_exchange())
    mix = _GradReduction({"w_branch_attn": d_wba, "w_branch_gmlp": d_wbg,
                          "w_out": d_wout.reshape(N_CHIPS, D_MODEL // N_CHIPS, D_MODEL)}, place, "mix")
    n_mlp = len(mlp.names)
    attn = lambda g, rider: _attn_bwd(qkv[g], dya[g], ya[g], lse[g], *tables[DILATIONS[g]], DILATIONS[g], rider=rider)
    dqkv0, riding = attn(0, _together(mlp.chip_exchange(mlp_others), mix.pair_exchange()))
    dqkv1, riding = attn(1, _together(mlp.pair_share(riding[:n_mlp]), mix.chip_exchange(riding[n_mlp:])))
    reduced = mlp.result(riding[:n_mlp])
    dqkv2, riding = attn(2, mix.pair_share(riding[n_mlp:]))
    reduced.update(mix.result(riding))
    dqkv = [dqkv0, dqkv1, dqkv2]

    d_qkv = [_tn_matmul_residue(h[g], dqkv[g], dil, f"grad_w_in_qkv{g}") for g, dil in enumerate(DILATIONS)]
    d_rest = _tn_matmul(h[0], drest, "grad_w_in_rest", 1024, 1024)
    d_win = jnp.concatenate([d_qkv[g][:, s * GROUP_W:(s + 1) * GROUP_W] for s in range(3) for g in range(N_GROUPS)]
                            + [d_rest], axis=1)
    shard_w = IN_W // N_CHIPS
    d_win = jnp.stack([d_win[:, j * shard_w:(j + 1) * shard_w] for j in range(N_CHIPS)], axis=0)
    first = _GradReduction({"w_in": d_win}, place, "w_in")
    w_qkv = [jnp.concatenate([w_in[:, lo:hi] for lo, hi in _qkv_columns(g)], axis=1) for g in range(N_GROUPS)]
    w_rest = w_in[:, QKV_W:]
    so_far = (lax.empty((seq, D_MODEL), F32), jnp.zeros((1, D_MODEL), F32))
    so_far, riding = _in_proj_bwd(dqkv, drest, w_qkv, w_rest, x, dx1, g0, so_far, 0, 2, rider=first.pair_exchange())
    (grad_x, dg0), riding = _in_proj_bwd(dqkv, drest, w_qkv, w_rest, x, dx1, g0, so_far, 1, 2,
                                         rider=first.chip_exchange(riding))
    reduced.update(first.result(_run_exchange(first.pair_share(riding), "w_in_pair_share")))
    little = {"norm_pre_mix": dg0, "w_spatial": d_wsp, "b_spatial": d_bb[:, :, 0], "ln_v_gain": d_lg, "ln_v_bias": d_lb,
              "norm_post_mix": dg1, "norm_pre_mlp": dg2, "norm_post_mlp": dg3}
    return loss8[0, 0], grad_x, reduced, little


def kernel(x, norm_pre_mix, w_in, w_spatial, b_spatial, ln_v_gain, ln_v_bias, w_branch_attn, w_branch_gmlp, w_out, norm_post_mix, norm_pre_mlp, w_mlp_in, w_mlp_out, norm_post_mlp, loss_target, m_norm_pre_mix, m_w_in, m_w_spatial, m_b_spatial, m_ln_v_gain, m_ln_v_bias, m_w_branch_attn, m_w_branch_gmlp, m_w_out, m_norm_post_mix, m_norm_pre_mlp, m_w_mlp_in, m_w_mlp_out, m_norm_post_mlp, v_norm_pre_mix, v_w_in, v_w_spatial, v_b_spatial, v_ln_v_gain, v_ln_v_bias, v_w_branch_attn, v_w_branch_gmlp, v_w_out, v_norm_post_mix, v_norm_pre_mlp, v_w_mlp_in, v_w_mlp_out, v_norm_post_mlp):
    given = dict(norm_pre_mix=norm_pre_mix, w_in=w_in, w_spatial=w_spatial, b_spatial=b_spatial, ln_v_gain=ln_v_gain,
                 ln_v_bias=ln_v_bias, w_branch_attn=w_branch_attn, w_branch_gmlp=w_branch_gmlp, w_out=w_out,
                 norm_post_mix=norm_post_mix, norm_pre_mlp=norm_pre_mlp, w_mlp_in=w_mlp_in, w_mlp_out=w_mlp_out,
                 norm_post_mlp=norm_post_mlp)
    moments_m = dict(norm_pre_mix=m_norm_pre_mix, w_in=m_w_in, w_spatial=m_w_spatial, b_spatial=m_b_spatial,
                     ln_v_gain=m_ln_v_gain, ln_v_bias=m_ln_v_bias, w_branch_attn=m_w_branch_attn,
                     w_branch_gmlp=m_w_branch_gmlp, w_out=m_w_out, norm_post_mix=m_norm_post_mix,
                     norm_pre_mlp=m_norm_pre_mlp, w_mlp_in=m_w_mlp_in, w_mlp_out=m_w_mlp_out, norm_post_mlp=m_norm_post_mlp)
    moments_v = dict(norm_pre_mix=v_norm_pre_mix, w_in=v_w_in, w_spatial=v_w_spatial, b_spatial=v_b_spatial,
                     ln_v_gain=v_ln_v_gain, ln_v_bias=v_ln_v_bias, w_branch_attn=v_w_branch_attn,
                     w_branch_gmlp=v_w_branch_gmlp, w_out=v_w_out, norm_post_mix=v_norm_post_mix,
                     norm_pre_mlp=v_norm_pre_mlp, w_mlp_in=v_w_mlp_in, w_mlp_out=v_w_mlp_out, norm_post_mlp=v_norm_post_mlp)
    cx, cy, cc = lax.axis_index("x"), lax.axis_index("y"), lax.axis_index("c")

    shards = [given[n][0] for n in BIG]
    small = {n: given[n][0] if given[n].ndim > 2 else given[n] for n in SMALL}
    place = jnp.stack([cc, 2 * cx + cy]).astype(jnp.int32)
    loss, grad_x, grad_shard, grads = _device_step(x[0], loss_target[0], small, shards, place)
    loss = lax.psum(loss, ("x", "y", "c"))

    packed = jnp.concatenate([_rows8(grads[n]) for n in SMALL], axis=0)
    summed = _all_reduce_small(packed)
    row = 0
    for n in SMALL:
        shape = given[n][0].shape
        cnt = -(-(given[n][0].size // 128) // 8) * 8
        grad_shard[n] = summed[row:row + given[n][0].size // 128].reshape(shape)
        row += cnt

    deltas, new_m, new_v = {}, {}, {}
    for n in ORDER:
        shape = given[n].shape
        two_d = (-1, shape[-1])
        d, nm, nv = _adamw(given[n].reshape(two_d), grad_shard[n].reshape(two_d), moments_m[n].reshape(two_d),
                           moments_v[n].reshape(two_d), "adamw_" + n)
        deltas[n], new_m[n], new_v[n] = d.reshape(shape), nm.reshape(shape), nv.reshape(shape)
    grad_out = [grad_shard[n].reshape(given[n].shape) for n in ORDER]
    return (loss, grad_x[None], *grad_out, *[deltas[n] for n in ORDER], *[new_m[n] for n in ORDER],
            *[new_v[n] for n in ORDER])
```

```python
import math

import jax
import jax.numpy as jnp
from jax import lax
from jax.experimental import pallas as pl
from jax.experimental.pallas import tpu as pltpu

F32 = jnp.float32
BF16 = jnp.bfloat16
MESH = pl.DeviceIdType.MESH

D_MODEL = 1024
HEAD_DIM = 64
HEADS_PER_GROUP = 4
GROUP_W = HEADS_PER_GROUP * HEAD_DIM
DILATIONS = (1, 4, 16)
N_GROUPS = len(DILATIONS)
ATTN_W = N_GROUPS * GROUP_W
QKV_W = 3 * ATTN_W
GMLP_W = 512
GMLP_GROUPS = 4
CHUNK = 128
REST_W = 2 * GMLP_W + 2 * D_MODEL
IN_W = QKV_W + REST_W
D_FF = 4096
QBLK = 128
ROPE_THETA = 10000.0
EPS = 1e-6
NEG = -1e30
SCALE = HEAD_DIM ** -0.5
N_CHIPS = 4

ADAM_LR = 0.001
ADAM_B1 = 0.9
ADAM_B2 = 0.999
ADAM_EPS = 1e-08
ADAM_WD = 0.01
ADAM_STEP = 10

MIB = 1024 * 1024
HBM_SPEC = pl.BlockSpec(memory_space=pltpu.HBM)
VMEM_SPEC = pl.BlockSpec(memory_space=pltpu.VMEM)


MLP_TM = 256


def _params(semantics, vmem_mib):
    return pltpu.CompilerParams(dimension_semantics=semantics, vmem_limit_bytes=vmem_mib * MIB)


def _resident(shape):
    return pl.BlockSpec(shape, lambda *_: (0,) * len(shape), pipeline_mode=pl.Buffered(1))


def _dot(a, b):
    return jnp.dot(a, b, preferred_element_type=F32)


def _dot_nt(a, b):
    return lax.dot_general(a, b, (((1,), (1,)), ((), ())), preferred_element_type=F32)


def _dot_tn(a, b):
    return lax.dot_general(a, b, (((0,), (0,)), ((), ())), preferred_element_type=F32)


_GELU_C = math.sqrt(2.0 / math.pi)


def _gelu(x):
    return x * (0.5 * (1.0 + jnp.tanh(_GELU_C * (x + 0.044715 * (x * x * x)))))


def _gelu_grad(x):
    t = jnp.tanh(_GELU_C * (x + 0.044715 * (x * x * x)))
    return 0.5 * (1.0 + t) + 0.5 * x * (1.0 - t * t) * (_GELU_C * (1.0 + 3.0 * 0.044715 * (x * x)))


def _rsqrt_ms(v):
    return lax.rsqrt(jnp.mean(v * v, axis=-1, keepdims=True) + EPS)


def _rmsnorm_bwd(dn, src, gain):
    r = _rsqrt_ms(src)
    t = gain * dn
    dgain = jnp.sum(dn * (src * r), axis=0, keepdims=True)
    dsrc = r * t - src * ((r * r * r) * jnp.mean(t * src, axis=-1, keepdims=True))
    return dsrc, dgain


def _rot_half(v):
    w = v.shape[-1]
    lane = lax.broadcasted_iota(jnp.int32, v.shape, v.ndim - 1)
    return jnp.where((lane % HEAD_DIM) < HEAD_DIM // 2, pltpu.roll(v, w - HEAD_DIM // 2, v.ndim - 1),
                     pltpu.roll(v, HEAD_DIM // 2, v.ndim - 1))


def _head_masks(shape):
    lane = lax.broadcasted_iota(jnp.int32, shape, 1)
    return [(lane >= h * HEAD_DIM) & (lane < (h + 1) * HEAD_DIM) for h in range(HEADS_PER_GROUP)]


def _head_stack(block, hmask):
    zero = jnp.zeros((), block.dtype)
    return jnp.concatenate([jnp.where(hm, block, zero) for hm in hmask], axis=0)


LANES = 128


def _put_residue(slab, val, out_ref, dil, width, col0):
    tm, w = val.shape
    if dil == 1:
        out_ref[:, col0:col0 + w] = val.astype(out_ref.dtype)
        return
    for k in range(w // LANES):
        slab[k] = val[:, k * LANES:(k + 1) * LANES]
    for r in range(dil):
        for k in range(w // LANES):
            c = r * width + col0 + k * LANES
            out_ref[:, c:c + LANES] = slab[k, pl.ds(r, tm // dil, stride=dil), :].astype(out_ref.dtype)


def _get_tokens(slab, in_ref, dil, width, col0, w):
    if dil == 1:
        return in_ref[:, col0:col0 + w].astype(F32)
    rows = in_ref.shape[0]
    for r in range(dil):
        for k in range(w // LANES):
            c = r * width + col0 + k * LANES
            slab[k, pl.ds(r, rows, stride=dil), :] = in_ref[:, c:c + LANES].astype(F32)
    return jnp.concatenate([slab[k] for k in range(w // LANES)], axis=1)


def _rope_tables(seq):
    half = HEAD_DIM // 2
    inv_freq = ROPE_THETA ** (-jnp.arange(half, dtype=F32) / half)
    freq = jnp.tile(inv_freq, LANES // half).reshape(1, LANES)
    tm = 512

    def body(f_ref, *refs):
        outs, slab_c, slab_s = refs[:-2], refs[-2], refs[-1]
        row = lax.broadcasted_iota(jnp.int32, (tm, LANES), 0) + pl.program_id(0) * tm
        lane = lax.broadcasted_iota(jnp.int32, (tm, LANES), 1)
        ang = row.astype(F32) * f_ref[...]
        cos = jnp.cos(ang)
        sin = jnp.where((lane % HEAD_DIM) < half, -jnp.sin(ang), jnp.sin(ang))
        slab_c[0] = cos
        slab_s[0] = sin
        for i, dil in enumerate(DILATIONS):
            for tab, slab in ((outs[2 * i], slab_c), (outs[2 * i + 1], slab_s)):
                for r in range(dil):
                    piece = slab[0, pl.ds(r, tm // dil, stride=dil), :] if dil > 1 else slab[0]
                    for k in range(GROUP_W // LANES):
                        tab[:, r * GROUP_W + k * LANES:r * GROUP_W + (k + 1) * LANES] = piece

    outs = pl.pallas_call(
        body, name="rope_tables", grid=(seq // tm,),
        in_specs=[pl.BlockSpec((1, LANES), lambda i: (0, 0))],
        out_specs=[pl.BlockSpec((tm // d, d * GROUP_W), lambda i: (i, 0)) for d in DILATIONS for _ in range(2)],
        out_shape=[jax.ShapeDtypeStruct((seq // d, d * GROUP_W), F32) for d in DILATIONS for _ in range(2)],
        scratch_shapes=[pltpu.VMEM((1, tm, LANES), F32)] * 2,
        compiler_params=_params(("arbitrary",), 32),
    )(freq)
    return {d: (outs[2 * i], outs[2 * i + 1]) for i, d in enumerate(DILATIONS)}


def _in_proj(x, g0, w_in, cos_t, sin_t, rider=None):
    seq = x.shape[0]
    tm, tn = 256, GROUP_W
    n_qk = 2 * ATTN_W // tn
    n_qkv = QKV_W // tn

    def body(x_ref, g_ref, w_ref, cos_ref, sin_ref, *refs):
        h_refs, qkv_refs, rest_ref, slab = refs[:N_GROUPS], refs[N_GROUPS:2 * N_GROUPS], refs[2 * N_GROUPS], refs[-1]
        xv = x_ref[...]
        hf = (xv * _rsqrt_ms(xv)) * g_ref[...]
        hb = hf.astype(BF16)
        for g, dil in enumerate(DILATIONS):
            _put_residue(slab, hf, h_refs[g], dil, D_MODEL, 0)
        cos, sin = cos_ref[...], sin_ref[...]
        for j in range(IN_W // tn):
            p = _dot(hb, w_ref[:, j * tn:(j + 1) * tn])
            if j < n_qkv:
                if j < n_qk:
                    p = p * cos + _rot_half(p) * sin
                section, g = divmod(j, N_GROUPS)
                _put_residue(slab, p, qkv_refs[g], DILATIONS[g], 3 * GROUP_W, section * GROUP_W)
            else:
                rest_ref[:, (j - n_qkv) * tn:(j - n_qkv + 1) * tn] = p.astype(BF16)

    return _call(
        body, name="in_proj", grid=(seq // tm,),
        in_specs=[pl.BlockSpec((tm, D_MODEL), lambda i: (i, 0)),
                  pl.BlockSpec((1, D_MODEL), lambda i: (0, 0)),
                  _resident((D_MODEL, IN_W)),
                  pl.BlockSpec((tm, GROUP_W), lambda i: (i, 0)),
                  pl.BlockSpec((tm, GROUP_W), lambda i: (i, 0))],
        out_specs=[pl.BlockSpec((tm // d, d * D_MODEL), lambda i: (i, 0)) for d in DILATIONS]
        + [pl.BlockSpec((tm // d, d * 3 * GROUP_W), lambda i: (i, 0)) for d in DILATIONS]
        + [pl.BlockSpec((tm, REST_W), lambda i: (i, 0))],
        out_shape=[jax.ShapeDtypeStruct((seq // d, d * D_MODEL), BF16) for d in DILATIONS]
        + [jax.ShapeDtypeStruct((seq // d, d * 3 * GROUP_W), BF16) for d in DILATIONS]
        + [jax.ShapeDtypeStruct((seq, REST_W), BF16)],
        scratch_shapes=[pltpu.VMEM((D_MODEL // LANES, tm, LANES), F32)],
        params=_params(("arbitrary",), 56), args=(x, g0, w_in, cos_t, sin_t), rider=rider)


def _band_masks():
    qi = lax.broadcasted_iota(jnp.int32, (QBLK, QBLK), 0)
    kj = lax.broadcasted_iota(jnp.int32, (QBLK, QBLK), 1)
    return kj <= qi, kj >= qi


def _attn_tile(length):
    return min(512, length)


def _attn_fwd(qkv, dil):
    length = qkv.shape[0]
    tq = _attn_tile(length)
    nsub = tq // QBLK
    nblk = length // tq

    def body(q_ref, k_ref, v_ref, kp_ref, vp_ref, o_ref, l_ref):
        n = pl.program_id(1)
        mask_c, mask_p0 = _band_masks()
        hmask = _head_masks((QBLK, GROUP_W))
        sub = lambda ref, b: ref[b * QBLK:(b + 1) * QBLK, :]
        kbd = [_head_stack(kp_ref[...], hmask)] + [_head_stack(sub(k_ref, b), hmask) for b in range(nsub)]
        vbd = [_head_stack(vp_ref[...], hmask)] + [_head_stack(sub(v_ref, b), hmask) for b in range(nsub)]
        for b in range(nsub):
            q = sub(q_ref, b)
            sp = _dot_nt(q, kbd[b]) * SCALE
            sc = _dot_nt(q, kbd[b + 1]) * SCALE
            mask_p = mask_p0 & (n > 0) if b == 0 else mask_p0
            pps, pcs = [], []
            den_all = jnp.zeros((QBLK, GROUP_W), F32)
            l_all = jnp.zeros((QBLK, GROUP_W), F32)
            for h in range(HEADS_PER_GROUP):
                cols = slice(h * QBLK, (h + 1) * QBLK)
                sph = jnp.where(mask_p, sp[:, cols], NEG)
                sch = jnp.where(mask_c, sc[:, cols], NEG)
                m = jnp.maximum(jnp.max(sch, axis=-1, keepdims=True), jnp.max(sph, axis=-1, keepdims=True))
                pc, pp = jnp.exp(sch - m), jnp.exp(sph - m)
                den = jnp.sum(pc, axis=-1, keepdims=True) + jnp.sum(pp, axis=-1, keepdims=True)
                pps.append(pp.astype(BF16))
                pcs.append(pc.astype(BF16))
                den_all = jnp.where(hmask[h], den, den_all)
                l_all = jnp.where(hmask[h], m + jnp.log(den), l_all)
            pv = _dot(jnp.concatenate(pps, axis=1), vbd[b]) + _dot(jnp.concatenate(pcs, axis=1), vbd[b + 1])
            o_ref[b * QBLK:(b + 1) * QBLK, :] = pv / den_all
            l_ref[b * QBLK:(b + 1) * QBLK, :] = l_all

    cur = lambda sec: pl.BlockSpec((tq, GROUP_W), lambda r, n: (n, r * 3 + sec))
    prev = lambda sec: pl.BlockSpec((QBLK, GROUP_W), lambda r, n: (jnp.maximum(n * nsub - 1, 0), r * 3 + sec))
    return pl.pallas_call(
        body, name=f"attn_fwd_d{dil}", grid=(dil, nblk),
        in_specs=[cur(0), cur(1), cur(2), prev(1), prev(2)],
        out_specs=[pl.BlockSpec((tq, GROUP_W), lambda r, n: (n, r))] * 2,
        out_shape=[jax.ShapeDtypeStruct((length, dil * GROUP_W), F32)] * 2,
        compiler_params=_params(("arbitrary", "arbitrary"), 32),
    )(qkv, qkv, qkv, qkv, qkv)


def _attn_bwd(qkv, dy, y, lse, cos_t, sin_t, dil, rider=None):
    length = qkv.shape[0]
    tq = _attn_tile(length)
    nsub = tq // QBLK
    nblk = length // tq

    def body(q_ref, k_ref, v_ref, kp_ref, vp_ref, qn_ref, dy_ref, y_ref, l_ref, dyn_ref, yn_ref, ln_ref,
             cos_ref, sin_ref, out_ref, dq_s, dk_s, dv_s):
        n = pl.program_id(1)
        mask_c, mask_p0 = _band_masks()
        hmask = _head_masks((QBLK, GROUP_W))
        sub = lambda ref, b: ref[b * QBLK:(b + 1) * QBLK, :]
        kbd = [_head_stack(kp_ref[...], hmask)] + [_head_stack(sub(k_ref, b), hmask) for b in range(nsub)]
        vbd = [_head_stack(vp_ref[...], hmask)] + [_head_stack(sub(v_ref, b), hmask) for b in range(nsub)]
        dk_s[...] = jnp.zeros(dk_s.shape, F32)
        dv_s[...] = jnp.zeros(dv_s.shape, F32)

        def query_block(q, dyv, yv, lv, key_blocks):
            dyb = dyv.astype(BF16)
            qbd = _head_stack(q, hmask)
            dybd = jnp.concatenate([jnp.where(hm, dyv, 0.0).astype(BF16) for hm in hmask], axis=0)
            prod = dyv * yv
            deltas = [jnp.sum(jnp.where(hm, prod, 0.0), axis=-1, keepdims=True) for hm in hmask]
            lses = [jnp.max(jnp.where(hm, lv, NEG), axis=-1, keepdims=True) for hm in hmask]
            dq = jnp.zeros((QBLK, GROUP_W), F32)
            for kb, mask in key_blocks:
                s = _dot_nt(q, kbd[kb]) * SCALE
                dp = _dot_nt(dyb, vbd[kb])
                ps, dss = [], []
                for h in range(HEADS_PER_GROUP):
                    cols = slice(h * QBLK, (h + 1) * QBLK)
                    p = jnp.exp(jnp.where(mask, s[:, cols] - lses[h], NEG))
                    ps.append(p.astype(BF16))
                    dss.append((p * (dp[:, cols] - deltas[h])).astype(BF16))
                dq = dq + _dot(jnp.concatenate(dss, axis=1), kbd[kb])
                if kb >= 1:
                    krows = slice((kb - 1) * QBLK, kb * QBLK)
                    dv_s[krows, :] += _dot_tn(jnp.concatenate(ps, axis=0), dybd)
                    dk_s[krows, :] += _dot_tn(jnp.concatenate(dss, axis=0), qbd) * SCALE
            return dq * SCALE

        for b in range(nsub):
            mask_p = mask_p0 & (n > 0) if b == 0 else mask_p0
            dq_s[b * QBLK:(b + 1) * QBLK, :] = query_block(sub(q_ref, b), sub(dy_ref, b), sub(y_ref, b), sub(l_ref, b),
                                                            [(b, mask_p), (b + 1, mask_c)])
        query_block(qn_ref[...], dyn_ref[...], yn_ref[...], ln_ref[...], [(nsub, mask_p0 & (n < nblk - 1))])
        cos, sin = cos_ref[...], sin_ref[...]
        dq, dk = dq_s[...], dk_s[...]
        out_ref[:, 0:GROUP_W] = (dq * cos - _rot_half(dq) * sin).astype(BF16)
        out_ref[:, GROUP_W:2 * GROUP_W] = (dk * cos - _rot_half(dk) * sin).astype(BF16)
        out_ref[:, 2 * GROUP_W:3 * GROUP_W] = dv_s[...].astype(BF16)

    cur = lambda sec: pl.BlockSpec((tq, GROUP_W), lambda r, n: (n, r * 3 + sec))
    prev = lambda sec: pl.BlockSpec((QBLK, GROUP_W), lambda r, n: (jnp.maximum(n * nsub - 1, 0), r * 3 + sec))
    nxt_q = pl.BlockSpec((QBLK, GROUP_W), lambda r, n: (jnp.minimum((n + 1) * nsub, nblk * nsub - 1), r * 3))
    tok = pl.BlockSpec((tq, GROUP_W), lambda r, n: (n, r))
    tok_next = pl.BlockSpec((QBLK, GROUP_W), lambda r, n: (jnp.minimum((n + 1) * nsub, nblk * nsub - 1), r))
    (out,), riding = _call(
        body, name=f"attn_bwd_d{dil}", grid=(dil, nblk),
        in_specs=[cur(0), cur(1), cur(2), prev(1), prev(2), nxt_q,
                  tok, tok, tok, tok_next, tok_next, tok_next, tok, tok],
        out_specs=[pl.BlockSpec((tq, 3 * GROUP_W), lambda r, n: (n, r))],
        out_shape=[jax.ShapeDtypeStruct((length, dil * 3 * GROUP_W), BF16)],
        scratch_shapes=[pltpu.VMEM((tq, GROUP_W), F32)] * 3,
        params=_params(("arbitrary", "arbitrary"), 32),
        args=(qkv, qkv, qkv, qkv, qkv, qkv, dy, y, lse, dy, y, lse, cos_t, sin_t), rider=rider)
    return out, riding


def _layernorm_stats(z):
    mu = jnp.mean(z, axis=-1, keepdims=True)
    zc = z - mu
    rstd = lax.rsqrt(jnp.mean(zc * zc, axis=-1, keepdims=True) + EPS)
    return zc * rstd, rstd


def _tril_mask():
    row = lax.broadcasted_iota(jnp.int32, (CHUNK, CHUNK), 0)
    col = lax.broadcasted_iota(jnp.int32, (CHUNK, CHUNK), 1)
    return col <= row


def _mix_fwd(o_l, rest, x, w_sp, b_col, ln_g, ln_b, w_ba, w_bg, w_out, g1):
    seq = x.shape[0]
    tm = 256

    def body(o0, l0, o1, l1, o2, l2, up_ref, zp_ref, gap_ref, gbp_ref, x_ref, wsp_ref, bcol_ref, lg_ref, lb_ref,
             wba_ref, wbg_ref, wout_ref, g1_ref, ya0, lj0, ya1, lj1, ya2, lj2, yg_ref, mg_ref, y_ref, x1_ref, slab):
        outs = [_get_tokens(slab, o, d, GROUP_W, 0, GROUP_W) for o, d in zip((o0, o1, o2), DILATIONS)]
        lses = [_get_tokens(slab, l, d, GROUP_W, 0, GROUP_W) for l, d in zip((l0, l1, l2), DILATIONS)]
        m = jnp.maximum(jnp.maximum(lses[0], lses[1]), lses[2])
        es = [jnp.exp(l - m) for l in lses]
        tot = es[0] + es[1] + es[2]
        ya = (es[0] * outs[0] + es[1] * outs[1] + es[2] * outs[2]) / tot
        lj = m + jnp.log(tot)
        for ya_ref, lj_ref, d in zip((ya0, ya1, ya2), (lj0, lj1, lj2), DILATIONS):
            _put_residue(slab, ya, ya_ref, d, GROUP_W, 0)
            _put_residue(slab, lj, lj_ref, d, GROUP_W, 0)
        zhat, _ = _layernorm_stats(_gelu(zp_ref[...].astype(F32)))
        zln = (zhat * lg_ref[...] + lb_ref[...]).astype(BF16)
        u = _gelu(up_ref[...].astype(F32))
        tril = _tril_mask()
        for g in range(GMLP_GROUPS):
            wm = jnp.where(tril, wsp_ref[g], 0.0).astype(BF16)
            cols = slice(g * CHUNK, (g + 1) * CHUNK)
            for c in range(tm // CHUNK):
                rows = slice(c * CHUNK, (c + 1) * CHUNK)
                sz = _dot(wm, zln[rows, cols]) + bcol_ref[g]
                yg_ref[rows, cols] = (u[rows, cols] * sz).astype(BF16)
        a = _dot(ya.astype(BF16), wba_ref[...])
        bm = _dot(yg_ref[...], wbg_ref[...])
        merged = (jax.nn.sigmoid(gap_ref[...].astype(F32)) * a + jax.nn.sigmoid(gbp_ref[...].astype(F32)) * bm).astype(BF16)
        mg_ref[...] = merged
        yv = _dot(merged, wout_ref[...])
        y_ref[...] = yv
        x1_ref[...] = x_ref[...] + (yv * _rsqrt_ms(yv)) * g1_ref[...]

    tok = lambda w: pl.BlockSpec((tm, w), lambda i: (i, 0))
    res = lambda d: pl.BlockSpec((tm // d, d * GROUP_W), lambda i: (i, 0))
    full = lambda *s: pl.BlockSpec(s, lambda i: (0,) * len(s))
    res_specs = [res(d) for d in DILATIONS for _ in range(2)]
    return pl.pallas_call(
        body, name="mix_fwd", grid=(seq // tm,),
        in_specs=res_specs + [
            pl.BlockSpec((tm, GMLP_W), lambda i: (i, 0)), pl.BlockSpec((tm, GMLP_W), lambda i: (i, 1)),
            pl.BlockSpec((tm, D_MODEL), lambda i: (i, 1)), pl.BlockSpec((tm, D_MODEL), lambda i: (i, 2)),
            tok(D_MODEL), full(GMLP_GROUPS, CHUNK, CHUNK), full(GMLP_GROUPS, CHUNK, 1), full(1, GMLP_W), full(1, GMLP_W),
            full(GROUP_W, D_MODEL), full(GMLP_W, D_MODEL), full(D_MODEL, D_MODEL), full(1, D_MODEL)],
        out_specs=res_specs + [tok(GMLP_W), tok(D_MODEL), tok(D_MODEL), tok(D_MODEL)],
        out_shape=[jax.ShapeDtypeStruct((seq // d, d * GROUP_W), F32) for d in DILATIONS for _ in range(2)]
        + [jax.ShapeDtypeStruct((seq, GMLP_W), BF16), jax.ShapeDtypeStruct((seq, D_MODEL), BF16),
           jax.ShapeDtypeStruct((seq, D_MODEL), F32), jax.ShapeDtypeStruct((seq, D_MODEL), F32)],
        scratch_shapes=[pltpu.VMEM((GROUP_W // LANES, tm, LANES), F32)],
        compiler_params=_params(("arbitrary",), 48),
    )(*o_l, rest, rest, rest, rest, x, w_sp, b_col, ln_g, ln_b, w_ba, w_bg, w_out, g1)


def _mlp_fwd(x1, g2, g3, w_mi, w_mo, target):
    seq = x1.shape[0]
    tm, tf = MLP_TM, 512

    def body(x1_ref, g2_ref, g3_ref, wmi_ref, wmo_ref, t_ref, h2_ref, a_ref, dy2_ref, dout_ref, loss_ref, dg3_ref, sq_s):
        @pl.when(pl.program_id(0) == 0)
        def _():
            loss_ref[...] = jnp.zeros(loss_ref.shape, F32)
            dg3_ref[...] = jnp.zeros(dg3_ref.shape, F32)

        xv = x1_ref[...]
        hb = ((xv * _rsqrt_ms(xv)) * g2_ref[...]).astype(BF16)
        h2_ref[...] = hb
        for j in range(D_FF // tf):
            cols = slice(j * tf, (j + 1) * tf)
            a = jnp.maximum(_dot(hb, wmi_ref[:, cols]), 0.0)
            a_ref[:, cols] = a.astype(BF16)
            sq_s[:, cols] = (a * a).astype(BF16)
        y2 = _dot(sq_s[...], wmo_ref[...])
        r3 = _rsqrt_ms(y2)
        out = xv + (y2 * r3) * g3_ref[...]
        diff = out - t_ref[...]
        tile_loss = 0.5 * jnp.sum(jnp.mean(diff * diff, axis=-1, keepdims=True), axis=0, keepdims=True)
        loss_ref[...] += jnp.broadcast_to(tile_loss, loss_ref.shape)
        dout = diff * (1.0 / D_MODEL)
        dout_ref[...] = dout
        dy2, dg3 = _rmsnorm_bwd(dout, y2, g3_ref[...])
        dy2_ref[...] = dy2.astype(BF16)
        dg3_ref[...] += dg3

    tok = lambda w: pl.BlockSpec((tm, w), lambda i: (i, 0))
    vec = pl.BlockSpec((1, D_MODEL), lambda i: (0, 0))
    return pl.pallas_call(
        body, name="mlp_fwd", grid=(seq // tm,),
        in_specs=[tok(D_MODEL), vec, vec, _resident((D_MODEL, D_FF)), _resident((D_FF, D_MODEL)), tok(D_MODEL)],
        out_specs=[tok(D_MODEL), tok(D_FF), tok(D_MODEL), tok(D_MODEL), pl.BlockSpec((8, 128), lambda i: (0, 0)), vec],
        out_shape=[jax.ShapeDtypeStruct((seq, D_MODEL), BF16), jax.ShapeDtypeStruct((seq, D_FF), BF16),
                   jax.ShapeDtypeStruct((seq, D_MODEL), BF16), jax.ShapeDtypeStruct((seq, D_MODEL), F32),
                   jax.ShapeDtypeStruct((8, 128), F32), jax.ShapeDtypeStruct((1, D_MODEL), F32)],
        scratch_shapes=[pltpu.VMEM((tm, D_FF), BF16)],
        compiler_params=_params(("arbitrary",), 56),
    )(x1, g2, g3, w_mi, w_mo, target)


def _mlp_bwd(dy2, a, w_mo, w_mi, dout, x1, y, g2, g1):
    seq = x1.shape[0]
    tm, tf = MLP_TM, 512

    def body(dy2_ref, a_ref, wmo_ref, wmi_ref, dout_ref, x1_ref, y_ref, g2_ref, g1_ref,
             dap_ref, dx1_ref, dy_ref, dg2_ref, dg1_ref):
        @pl.when(pl.program_id(0) == 0)
        def _():
            dg2_ref[...] = jnp.zeros(dg2_ref.shape, F32)
            dg1_ref[...] = jnp.zeros(dg1_ref.shape, F32)

        dy2v = dy2_ref[...]
        for j in range(D_FF // tf):
            cols = slice(j * tf, (j + 1) * tf)
            da2 = _dot_nt(dy2v, wmo_ref[cols, :])
            dap_ref[:, cols] = (da2 * (2.0 * a_ref[:, cols].astype(F32))).astype(BF16)
        dh2 = _dot_nt(dap_ref[...], wmi_ref[...])
        dres, dg2 = _rmsnorm_bwd(dh2, x1_ref[...], g2_ref[...])
        dx1 = dout_ref[...] + dres
        dx1_ref[...] = dx1
        dg2_ref[...] += dg2
        dyv, dg1 = _rmsnorm_bwd(dx1, y_ref[...], g1_ref[...])
        dy_ref[...] = dyv.astype(BF16)
        dg1_ref[...] += dg1

    tok = lambda w: pl.BlockSpec((tm, w), lambda i: (i, 0))
    vec = pl.BlockSpec((1, D_MODEL), lambda i: (0, 0))
    return pl.pallas_call(
        body, name="mlp_bwd", grid=(seq // tm,),
        in_specs=[tok(D_MODEL), tok(D_FF), _resident((D_FF, D_MODEL)), _resident((D_MODEL, D_FF)),
                  tok(D_MODEL), tok(D_MODEL), tok(D_MODEL), vec, vec],
        out_specs=[tok(D_FF), tok(D_MODEL), tok(D_MODEL), vec, vec],
        out_shape=[jax.ShapeDtypeStruct((seq, D_FF), BF16), jax.ShapeDtypeStruct((seq, D_MODEL), F32),
                   jax.ShapeDtypeStruct((seq, D_MODEL), BF16), jax.ShapeDtypeStruct((1, D_MODEL), F32),
                   jax.ShapeDtypeStruct((1, D_MODEL), F32)],
        compiler_params=_params(("arbitrary",), 56),
    )(dy2, a, w_mo, w_mi, dout, x1, y, g2, g1)


def _tn_matmul(a, b, name, bm, bn, square_a=False, column_shards=False):
    seq, m = a.shape
    n = b.shape[1]
    ts = 512

    def body(a_ref, b_ref, o_ref):
        @pl.when(pl.program_id(2) == 0)
        def _():
            o_ref[...] = jnp.zeros(o_ref.shape, F32)

        av = a_ref[...]
        if square_a:
            af = av.astype(F32)
            av = (af * af).astype(BF16)
        o_ref[...] += _dot_tn(av, b_ref[...])

    if column_shards:
        out_spec = pl.BlockSpec((None, bm, bn), lambda mi, ni, s: (ni, mi, 0))
        out_shape = jax.ShapeDtypeStruct((n // bn, m, bn), F32)
    else:
        out_spec = pl.BlockSpec((bm, bn), lambda mi, ni, s: (mi, ni))
        out_shape = jax.ShapeDtypeStruct((m, n), F32)
    return pl.pallas_call(
        body, name=name, grid=(m // bm, n // bn, seq // ts),
        in_specs=[pl.BlockSpec((ts, bm), lambda mi, ni, s: (s, mi)), pl.BlockSpec((ts, bn), lambda mi, ni, s: (s, ni))],
        out_specs=out_spec, out_shape=out_shape,
        compiler_params=_params(("arbitrary", "arbitrary", "arbitrary"), 40),
    )(a, b)


def _tn_matmul_residue(a, b, dil, name):
    length = a.shape[0]
    m, n = a.shape[1] // dil, b.shape[1] // dil
    ts = min(512, length)

    def body(a_ref, b_ref, o_ref):
        @pl.when((pl.program_id(0) == 0) & (pl.program_id(1) == 0))
        def _():
            o_ref[...] = jnp.zeros(o_ref.shape, F32)

        o_ref[...] += _dot_tn(a_ref[...], b_ref[...])

    return pl.pallas_call(
        body, name=name, grid=(dil, length // ts),
        in_specs=[pl.BlockSpec((ts, m), lambda r, s: (s, r)), pl.BlockSpec((ts, n), lambda r, s: (s, r))],
        out_specs=pl.BlockSpec((m, n), lambda r, s: (0, 0)),
        out_shape=jax.ShapeDtypeStruct((m, n), F32),
        compiler_params=_params(("arbitrary", "arbitrary"), 40),
    )(a, b)


def _mix_bwd(dy, ya, yg, mg, rest, w_out, w_ba, w_bg, w_sp, b_col, ln_g, ln_b, rider=None):
    seq = dy.shape[0]
    tm = 256

    def body(dy_ref, ya_ref, yg_ref, mg_ref, up_ref, zp_ref, gap_ref, gbp_ref, wout_ref, wba_ref, wbg_ref,
             wsp_ref, bcol_ref, lg_ref, lb_ref,
             dya0, dya1, dya2, dpr_ref, dwout_ref, dwba_ref, dwbg_ref, dwsp_ref, dbb_ref, dlg_ref, dlb_ref,
             dzln_s, du_s, slab):
        @pl.when(pl.program_id(0) == 0)
        def _():
            for ref in (dwout_ref, dwba_ref, dwbg_ref, dwsp_ref, dbb_ref, dlg_ref, dlb_ref):
                ref[...] = jnp.zeros(ref.shape, F32)

        dyv = dy_ref[...]
        dm = _dot_nt(dyv, wout_ref[...])
        dwout_ref[...] += _dot_tn(mg_ref[...], dyv)
        yab = ya_ref[...].astype(BF16)
        ygb = yg_ref[...]
        a = _dot(yab, wba_ref[...])
        bm = _dot(ygb, wbg_ref[...])
        ga = jax.nn.sigmoid(gap_ref[...].astype(F32))
        gb = jax.nn.sigmoid(gbp_ref[...].astype(F32))
        dpr_ref[:, 2 * GMLP_W:2 * GMLP_W + D_MODEL] = (dm * a * (ga * (1.0 - ga))).astype(BF16)
        dpr_ref[:, 2 * GMLP_W + D_MODEL:REST_W] = (dm * bm * (gb * (1.0 - gb))).astype(BF16)
        da = (dm * ga).astype(BF16)
        db = (dm * gb).astype(BF16)
        dwba = _dot_tn(yab, da)
        dwbg = _dot_tn(ygb, db)
        shard_w = D_MODEL // N_CHIPS
        for j in range(N_CHIPS):
            dwba_ref[j] += dwba[:, j * shard_w:(j + 1) * shard_w]
            dwbg_ref[j] += dwbg[:, j * shard_w:(j + 1) * shard_w]
        dya = _dot_nt(da, wba_ref[...])
        for dya_ref, d in zip((dya0, dya1, dya2), DILATIONS):
            _put_residue(slab, dya, dya_ref, d, GROUP_W, 0)
        dyg = _dot_nt(db, wbg_ref[...])

        zp = zp_ref[...].astype(F32)
        zhat, rstd = _layernorm_stats(_gelu(zp))
        lg = lg_ref[...]
        zln = (zhat * lg + lb_ref[...]).astype(BF16)
        up = up_ref[...].astype(F32)
        u = _gelu(up)
        tril = _tril_mask()
        for g in range(GMLP_GROUPS):
            wm = jnp.where(tril, wsp_ref[g], 0.0).astype(BF16)
            cols = slice(g * CHUNK, (g + 1) * CHUNK)
            for c in range(tm // CHUNK):
                rows = slice(c * CHUNK, (c + 1) * CHUNK)
                zb = zln[rows, cols]
                sz = _dot(wm, zb) + bcol_ref[g]
                dyg_cg = dyg[rows, cols]
                du_s[rows, cols] = dyg_cg * sz
                dsz = dyg_cg * u[rows, cols]
                dszb = dsz.astype(BF16)
                dbb_ref[g] += jnp.broadcast_to(jnp.sum(dsz, axis=-1, keepdims=True), (CHUNK, CHUNK))
                dwsp_ref[g] += jnp.where(tril, _dot_nt(dszb, zb), 0.0)
                dzln_s[rows, cols] = _dot_tn(wm, dszb)
        dzln = dzln_s[...]
        dlg_ref[...] += jnp.sum(dzln * zhat, axis=0, keepdims=True)
        dlb_ref[...] += jnp.sum(dzln, axis=0, keepdims=True)
        dzh = dzln * lg
        dz = rstd * (dzh - jnp.mean(dzh, axis=-1, keepdims=True) - zhat * jnp.mean(dzh * zhat, axis=-1, keepdims=True))
        dpr_ref[:, GMLP_W:2 * GMLP_W] = (dz * _gelu_grad(zp)).astype(BF16)
        dpr_ref[:, 0:GMLP_W] = (du_s[...] * _gelu_grad(up)).astype(BF16)

    tok = lambda w: pl.BlockSpec((tm, w), lambda i: (i, 0))
    full = lambda *s: pl.BlockSpec(s, lambda i: (0,) * len(s))
    return _call(
        body, name="mix_bwd", grid=(seq // tm,),
        in_specs=[tok(D_MODEL), tok(GROUP_W), tok(GMLP_W), tok(D_MODEL),
                  pl.BlockSpec((tm, GMLP_W), lambda i: (i, 0)), pl.BlockSpec((tm, GMLP_W), lambda i: (i, 1)),
                  pl.BlockSpec((tm, D_MODEL), lambda i: (i, 1)), pl.BlockSpec((tm, D_MODEL), lambda i: (i, 2)),
                  full(D_MODEL, D_MODEL), full(GROUP_W, D_MODEL), full(GMLP_W, D_MODEL),
                  full(GMLP_GROUPS, CHUNK, CHUNK), full(GMLP_GROUPS, CHUNK, 1), full(1, GMLP_W), full(1, GMLP_W)],
        out_specs=[pl.BlockSpec((tm // d, d * GROUP_W), lambda i: (i, 0)) for d in DILATIONS]
        + [tok(REST_W), full(D_MODEL, D_MODEL), full(N_CHIPS, GROUP_W, D_MODEL // N_CHIPS),
           full(N_CHIPS, GMLP_W, D_MODEL // N_CHIPS),
           full(GMLP_GROUPS, CHUNK, CHUNK), full(GMLP_GROUPS, CHUNK, CHUNK), full(1, GMLP_W), full(1, GMLP_W)],
        out_shape=[jax.ShapeDtypeStruct((seq // d, d * GROUP_W), F32) for d in DILATIONS]
        + [jax.ShapeDtypeStruct((seq, REST_W), BF16),
           jax.ShapeDtypeStruct((D_MODEL, D_MODEL), F32), jax.ShapeDtypeStruct((N_CHIPS, GROUP_W, D_MODEL // N_CHIPS), F32),
           jax.ShapeDtypeStruct((N_CHIPS, GMLP_W, D_MODEL // N_CHIPS), F32),
           jax.ShapeDtypeStruct((GMLP_GROUPS, CHUNK, CHUNK), F32),
           jax.ShapeDtypeStruct((GMLP_GROUPS, CHUNK, CHUNK), F32), jax.ShapeDtypeStruct((1, GMLP_W), F32),
           jax.ShapeDtypeStruct((1, GMLP_W), F32)],
        scratch_shapes=[pltpu.VMEM((tm, GMLP_W), F32), pltpu.VMEM((tm, GMLP_W), F32),
                        pltpu.VMEM((GROUP_W // LANES, tm, LANES), F32)],
        params=_params(("arbitrary",), 56),
        args=(dy, ya, yg, mg, rest, rest, rest, rest, w_out, w_ba, w_bg, w_sp, b_col, ln_g, ln_b), rider=rider)


IN_PROJ_BWD_TM = 256


def _in_proj_bwd(dqkv, drest, w_qkv, w_rest, x, dx1, g0, so_far, span, rider=None):
    seq = x.shape[0]
    tm = IN_PROJ_BWD_TM
    off, steps = span
    gx_so_far, dg_so_far = so_far

    def body(d0, d1, d2, dr_ref, w0, w1, w2, wr_ref, x_ref, dx1_ref, g_ref, dg_in_ref, gx_in_ref, gx_ref, dg_ref, slab):
        @pl.when(pl.program_id(0) == 0)
        def _():
            dg_ref[...] = dg_in_ref[...]

        dh = _dot_nt(dr_ref[...], wr_ref[...])
        for d_ref, w_ref, dil in zip((d0, d1, d2), (w0, w1, w2), DILATIONS):
            piece = d_ref[...] if dil == 1 else _get_tokens(slab, d_ref, dil, 3 * GROUP_W, 0, 3 * GROUP_W).astype(BF16)
            dh = dh + _dot_nt(piece, w_ref[...])
        dres, dg = _rmsnorm_bwd(dh, x_ref[...], g_ref[...])
        gx_ref[...] = dx1_ref[...] + dres
        dg_ref[...] += dg

    tok = lambda w: pl.BlockSpec((tm, w), lambda i: (i + off, 0))
    full = lambda *s: pl.BlockSpec(s, lambda i: (0,) * len(s))
    in_specs = ([pl.BlockSpec((tm // d, d * 3 * GROUP_W), lambda i: (i + off, 0)) for d in DILATIONS] + [tok(REST_W)]
                + [_resident((D_MODEL, 3 * GROUP_W))] * 3 + [_resident((D_MODEL, REST_W))]
                + [tok(D_MODEL), tok(D_MODEL), full(1, D_MODEL), full(1, D_MODEL), HBM_SPEC])
    return _call(
        body, name=f"in_proj_bwd_{off}", grid=(steps,), in_specs=in_specs,
        out_specs=[tok(D_MODEL), full(1, D_MODEL)],
        out_shape=[jax.ShapeDtypeStruct((seq, D_MODEL), F32), jax.ShapeDtypeStruct((1, D_MODEL), F32)],
        scratch_shapes=[pltpu.VMEM((3 * GROUP_W // LANES, tm, LANES), F32)],
        params=_params(("arbitrary",), 48), args=(*dqkv, drest, *w_qkv, w_rest, x, dx1, g0, dg_so_far, gx_so_far),
        rider=rider, aliases={len(in_specs) - 1: 0})


def _adamw(w, g, m, v, name):
    rows, cols = w.shape
    tr = 256 if rows % 256 == 0 else rows
    c1 = 1.0 - ADAM_B1 ** ADAM_STEP
    c2 = 1.0 - ADAM_B2 ** ADAM_STEP

    def body(w_ref, g_ref, m_ref, v_ref, go_ref, d_ref, nm_ref, nv_ref):
        gv = g_ref[...]
        go_ref[...] = gv
        nm = ADAM_B1 * m_ref[...] + (1.0 - ADAM_B1) * gv
        nv = ADAM_B2 * v_ref[...] + (1.0 - ADAM_B2) * (gv * gv)
        d_ref[...] = -ADAM_LR * ((nm / c1) / (jnp.sqrt(nv / c2) + ADAM_EPS) + ADAM_WD * w_ref[...])
        nm_ref[...] = nm
        nv_ref[...] = nv

    spec = pl.BlockSpec((tr, cols), lambda i: (i, 0))
    return pl.pallas_call(
        body, name=name, grid=(rows // tr,),
        in_specs=[spec] * 4, out_specs=[spec] * 4,
        out_shape=[jax.ShapeDtypeStruct((rows, cols), F32)] * 4,
        compiler_params=_params(("arbitrary",), 40),
    )(w, g, m, v)


def _place():
    x, y, c = lax.axis_index("x"), lax.axis_index("y"), lax.axis_index("c")
    chips = [(1 - x, y), (x, 1 - y), (1 - x, 1 - y)]
    return x, y, c, chips


class _Exchange:
    def __init__(self, inputs, out_shapes, n_sems, start, finish, aliases=None):
        self.inputs, self.out_shapes, self.n_sems = list(inputs), list(out_shapes), n_sems
        self.start, self.finish, self.aliases = start, finish, dict(aliases or {})

    def scratch(self):
        return [pltpu.SemaphoreType.DMA((self.n_sems,)), pltpu.SemaphoreType.DMA((self.n_sems,))]


def _together(*parts):
    ins = [len(p.inputs) for p in parts]
    outs = [len(p.out_shapes) for p in parts]

    def split(refs, counts):
        pos, pieces = 0, []
        for cnt in counts:
            pieces.append(refs[pos:pos + cnt])
            pos += cnt
        return pieces

    def run(which):
        def go(in_refs, out_refs, *sems):
            for k, (p, i, o) in enumerate(zip(parts, split(in_refs, ins), split(out_refs, outs))):
                getattr(p, which)(i, o, sems[2 * k], sems[2 * k + 1])
        return go

    both = _Exchange([a for p in parts for a in p.inputs], [s for p in parts for s in p.out_shapes], 0, run("start"),
                     run("finish"))
    both.aliases = {sum(ins[:k]) + i: sum(outs[:k]) + o for k, p in enumerate(parts) for i, o in p.aliases.items()}
    both.scratch = lambda: [s for p in parts for s in p.scratch()]
    return both


def _run_exchange(ex, name):
    n_in, n_out = len(ex.inputs), len(ex.out_shapes)

    def body(*refs):
        ins, outs, sems = refs[:n_in], refs[n_in:n_in + n_out], refs[n_in + n_out:]
        ex.start(ins, outs, *sems)
        ex.finish(ins, outs, *sems)

    return pl.pallas_call(
        body, name=name, in_specs=[HBM_SPEC] * n_in, out_specs=[HBM_SPEC] * n_out, out_shape=ex.out_shapes,
        scratch_shapes=ex.scratch(), input_output_aliases=ex.aliases,
    )(*ex.inputs)


def _call(body, *, name, grid, in_specs, out_specs, out_shape, scratch_shapes, params, args, rider=None, aliases=None):
    in_specs, out_specs, out_shape, scratch_shapes = list(in_specs), list(out_specs), list(out_shape), list(scratch_shapes)
    aliases = dict(aliases or {})
    if rider is None:
        outs = pl.pallas_call(body, name=name, grid=grid, in_specs=in_specs, out_specs=out_specs, out_shape=out_shape,
                              scratch_shapes=scratch_shapes, input_output_aliases=aliases, compiler_params=params)(*args)
        return list(outs), []
    n_in, n_out, n_scr = len(in_specs), len(out_specs), len(scratch_shapes)
    r_in, r_out = len(rider.inputs), len(rider.out_shapes)

    def wrapped(*refs):
        ins, r_ins = refs[:n_in], refs[n_in:n_in + r_in]
        pos = n_in + r_in
        outs, r_outs = refs[pos:pos + n_out], refs[pos + n_out:pos + n_out + r_out]
        pos += n_out + r_out
        scr, sems = refs[pos:pos + n_scr], refs[pos + n_scr:]
        ids = [pl.program_id(k) for k in range(len(grid))]
        first, last = ids[0] == 0, ids[0] == grid[0] - 1
        for k in range(1, len(grid)):
            first, last = first & (ids[k] == 0), last & (ids[k] == grid[k] - 1)

        @pl.when(first)
        def _():
            rider.start(r_ins, r_outs, *sems)

        body(*ins, *outs, *scr)

        @pl.when(last)
        def _():
            rider.finish(r_ins, r_outs, *sems)

    outs = pl.pallas_call(
        wrapped, name=name, grid=grid, in_specs=in_specs + [HBM_SPEC] * r_in, out_specs=out_specs + [HBM_SPEC] * r_out,
        out_shape=out_shape + rider.out_shapes, scratch_shapes=scratch_shapes + rider.scratch(),
        input_output_aliases={**aliases, **{n_in + i: n_out + o for i, o in rider.aliases.items()}}, compiler_params=params,
    )(*args, *rider.inputs)
    return list(outs[:n_out]), list(outs[n_out:])


def _stage_weights(shards):
    n = len(shards)

    def body(*refs):
        ins, outs, stages, sems = refs[:n], refs[n:2 * n], refs[2 * n:3 * n], refs[3 * n]
        x, y, _, _ = _place()
        copies = []
        for t in range(n):
            stages[t][...] = ins[t][...].astype(BF16)
            copies.append(pltpu.make_async_copy(stages[t], outs[t].at[2 * x + y], sems.at[t]))
            copies[-1].start()
        for cp in copies:
            cp.wait()

    stage_bytes = sum(s.size * 6 for s in shards)
    return pl.pallas_call(
        body, name="stage_weights", in_specs=[VMEM_SPEC] * n, out_specs=[HBM_SPEC] * n,
        out_shape=[jax.ShapeDtypeStruct((N_CHIPS,) + s.shape, BF16) for s in shards],
        scratch_shapes=[pltpu.VMEM(s.shape, BF16) for s in shards] + [pltpu.SemaphoreType.DMA((n,))],
        compiler_params=pltpu.CompilerParams(vmem_limit_bytes=stage_bytes + 8 * MIB),
    )(*shards)


def _gather(buffers):
    n = len(buffers)
    halves = [b.shape[1] // 2 for b in buffers]

    def half_of(outs, t, chip, which):
        return outs[t].at[chip, pl.ds(which * halves[t], halves[t]), :]

    def copy(outs, sems, t, k, chip, which, to):
        rows = half_of(outs, t, chip, which)
        return pltpu.make_async_remote_copy(src_ref=rows, dst_ref=rows, send_sem=sems[0].at[6 * t + k],
                                            recv_sem=sems[1].at[6 * t + k], device_id=to, device_id_type=MESH)

    def start(ins, outs, *sems):
        x, y, c, chips = _place()
        for t in range(n):
            for j, (px, py) in enumerate(chips):
                copy(outs, sems, t, j, 2 * x + y, c, (px, py, c)).start()

    def finish(ins, outs, *sems):
        x, y, c, chips = _place()
        sibling = (x, y, 1 - c)
        for j, (px, py) in enumerate(chips):
            for t in range(n):
                copy(outs, sems, t, j, 2 * px + py, c, (px, py, c)).wait_recv()
                copy(outs, sems, t, 3 + j, 2 * px + py, c, sibling).start()
        for j, (px, py) in enumerate(chips):
            for t in range(n):
                copy(outs, sems, t, 3 + j, 2 * px + py, 1 - c, sibling).wait_recv()
        for j, (px, py) in enumerate(chips):
            for t in range(n):
                copy(outs, sems, t, j, 2 * x + y, c, (px, py, c)).wait_send()
                copy(outs, sems, t, 3 + j, 2 * px + py, c, sibling).wait_send()

    return _Exchange(buffers, [jax.ShapeDtypeStruct(b.shape, b.dtype) for b in buffers], 6 * n, start, finish,
                     aliases={t: t for t in range(n)})


def _pair_exchange(grads):
    n = len(grads)
    halves = [g.shape[1] // 2 for g in grads]

    def copies(ins, outs, send_sems, recv_sems):
        x, y, c, _ = _place()
        return [pltpu.make_async_remote_copy(
            src_ref=ins[t].at[:, pl.ds((1 - c) * halves[t], halves[t]), :], dst_ref=outs[t],
            send_sem=send_sems.at[t], recv_sem=recv_sems.at[t], device_id=(x, y, 1 - c), device_id_type=MESH)
            for t in range(n)]

    def start(*refs):
        for cp in copies(*refs):
            cp.start()

    def finish(*refs):
        for cp in copies(*refs):
            cp.wait()

    return _Exchange(grads, [jax.ShapeDtypeStruct((N_CHIPS, h, g.shape[2]), F32) for g, h in zip(grads, halves)], n,
                     start, finish)


def _row_tile(rows):
    return min(rows, 256)


def _pair_add(grad, other, place, name):
    _, rows, cols = grad.shape
    rh = rows // 2
    tr = _row_tile(rh)
    nb = rh // tr

    def body(p_ref, g_ref, a_ref, wire_ref, own_ref):
        s = g_ref[...] + a_ref[...]
        wire_ref[...] = s.astype(BF16)

        @pl.when(pl.program_id(1) == p_ref[1])
        def _():
            own_ref[...] = s

    blk = (None, tr, cols)
    return pl.pallas_call(
        body, name=name,
        grid_spec=pltpu.PrefetchScalarGridSpec(
            num_scalar_prefetch=1, grid=(nb, N_CHIPS),
            in_specs=[pl.BlockSpec(blk, lambda i, j, p: (j, p[0] * nb + i, 0)), pl.BlockSpec(blk, lambda i, j, p: (j, i, 0))],
            out_specs=[pl.BlockSpec(blk, lambda i, j, p: (j, i, 0)), pl.BlockSpec((tr, cols), lambda i, j, p: (i, 0))]),
        out_shape=[jax.ShapeDtypeStruct((N_CHIPS, rh, cols), BF16), jax.ShapeDtypeStruct((rh, cols), F32)],
        compiler_params=_params(("arbitrary", "arbitrary"), 32),
    )(place, grad, other)


def _chip_exchange(wires):
    n = len(wires)

    def copies(ins, outs, send_sems, recv_sems):
        x, y, c, chips = _place()
        return [pltpu.make_async_remote_copy(
            src_ref=ins[t].at[2 * px + py], dst_ref=outs[t].at[j], send_sem=send_sems.at[3 * t + j],
            recv_sem=recv_sems.at[3 * t + j], device_id=(px, py, c), device_id_type=MESH)
            for t in range(n) for j, (px, py) in enumerate(chips)]

    def start(*refs):
        for cp in copies(*refs):
            cp.start()

    def finish(*refs):
        for cp in copies(*refs):
            cp.wait()

    return _Exchange(wires, [jax.ShapeDtypeStruct((3,) + w.shape[1:], BF16) for w in wires], 3 * n, start, finish)


def _chip_add(own, arrived, place, name):
    rh, cols = own.shape
    tr = _row_tile(rh)
    nb = rh // tr

    def body(p_ref, s_ref, b0, b1, b2, o_ref):
        o_ref[...] = ((s_ref[...] + b0[...].astype(F32)) + b1[...].astype(F32)) + b2[...].astype(F32)

    blk = (None, tr, cols)
    return pl.pallas_call(
        body, name=name,
        grid_spec=pltpu.PrefetchScalarGridSpec(
            num_scalar_prefetch=1, grid=(nb,),
            in_specs=[pl.BlockSpec((tr, cols), lambda i, p: (i, 0)), pl.BlockSpec(blk, lambda i, p: (0, i, 0)),
                      pl.BlockSpec(blk, lambda i, p: (1, i, 0)), pl.BlockSpec(blk, lambda i, p: (2, i, 0))],
            out_specs=pl.BlockSpec((tr, cols), lambda i, p: (p[0] * nb + i, 0))),
        out_shape=jax.ShapeDtypeStruct((2 * rh, cols), F32),
        compiler_params=_params(("arbitrary",), 32),
    )(place, own, arrived, arrived, arrived)


def _pair_share(halves):
    n = len(halves)
    rhs = [h.shape[0] // 2 for h in halves]

    def copy(outs, send_sems, recv_sems, t, which):
        x, y, c, _ = _place()
        rows = outs[t].at[pl.ds(which * rhs[t], rhs[t]), :]
        return pltpu.make_async_remote_copy(src_ref=rows, dst_ref=rows, send_sem=send_sems.at[t], recv_sem=recv_sems.at[t],
                                            device_id=(x, y, 1 - c), device_id_type=MESH)

    def start(ins, outs, send_sems, recv_sems):
        c = lax.axis_index("c")
        for t in range(n):
            copy(outs, send_sems, recv_sems, t, c).start()

    def finish(ins, outs, send_sems, recv_sems):
        c = lax.axis_index("c")
        for t in range(n):
            copy(outs, send_sems, recv_sems, t, c).wait_send()
            copy(outs, send_sems, recv_sems, t, 1 - c).wait_recv()

    return _Exchange(halves, [jax.ShapeDtypeStruct(h.shape, F32) for h in halves], n, start, finish,
                     aliases={t: t for t in range(n)})


class _GradReduction:
    def __init__(self, grads, place, tag):
        self.names, self.grads, self.place, self.tag = list(grads), grads, place, tag

    def pair_exchange(self):
        return _pair_exchange([self.grads[n] for n in self.names])

    def chip_exchange(self, others):
        sums = [_pair_add(self.grads[n], o, self.place, f"{self.tag}_pair_add_{n}") for n, o in zip(self.names, others)]
        self.owns = [own for _, own in sums]
        return _chip_exchange([wire for wire, _ in sums])

    def pair_share(self, arrived):
        return _pair_share([_chip_add(own, arr, self.place, f"{self.tag}_chip_add_{n}")
                            for n, own, arr in zip(self.names, self.owns, arrived)])

    def result(self, shared):
        return dict(zip(self.names, shared))


def _all_reduce_small(p):
    rows, lanes = p.shape
    flips = [(fx, fy, fc) for fx in (0, 1) for fy in (0, 1) for fc in (0, 1)][1:]

    def body(p_ref, o_ref, buf, send_sems, recv_sems):
        x, y, c, _ = _place()
        me = 4 * x + 2 * y + c
        buf[me] = p_ref[...]
        peers = [((1 - x) if fx else x, (1 - y) if fy else y, (1 - c) if fc else c) for fx, fy, fc in flips]
        cps = []
        for k, peer in enumerate(peers):
            cp = pltpu.make_async_remote_copy(
                src_ref=p_ref, dst_ref=buf.at[me], send_sem=send_sems.at[k], recv_sem=recv_sems.at[k],
                device_id=peer, device_id_type=MESH)
            cp.start()
            cps.append(cp)
        for k, (px, py, pc) in enumerate(peers):
            pltpu.make_async_remote_copy(
                src_ref=p_ref, dst_ref=buf.at[4 * px + 2 * py + pc], send_sem=send_sems.at[k], recv_sem=recv_sems.at[k],
                device_id=(px, py, pc), device_id_type=MESH).wait_recv()
        for cp in cps:
            cp.wait_send()
        acc = buf[0]
        for s in range(1, 8):
            acc = acc + buf[s]
        o_ref[...] = acc

    return pl.pallas_call(
        body, name="small_all_reduce", in_specs=[VMEM_SPEC], out_specs=VMEM_SPEC,
        out_shape=jax.ShapeDtypeStruct((rows, lanes), F32),
        scratch_shapes=[pltpu.VMEM((8, rows, lanes), F32), pltpu.SemaphoreType.DMA((7,)), pltpu.SemaphoreType.DMA((7,))],
        compiler_params=pltpu.CompilerParams(vmem_limit_bytes=32 * MIB),
    )(p)


BIG = ("w_in", "w_branch_attn", "w_branch_gmlp", "w_out", "w_mlp_in", "w_mlp_out")
COLUMN_SHARDED = ("w_in", "w_branch_attn", "w_branch_gmlp", "w_mlp_in")
SMALL = ("norm_pre_mix", "w_spatial", "b_spatial", "ln_v_gain", "ln_v_bias", "norm_post_mix", "norm_pre_mlp", "norm_post_mlp")
ORDER = ("norm_pre_mix", "w_in", "w_spatial", "b_spatial", "ln_v_gain", "ln_v_bias", "w_branch_attn", "w_branch_gmlp",
         "w_out", "norm_post_mix", "norm_pre_mlp", "w_mlp_in", "w_mlp_out", "norm_post_mlp")


def _full_weight(name, gathered):
    if name in COLUMN_SHARDED:
        return jnp.transpose(gathered, (1, 0, 2)).reshape(gathered.shape[1], -1)
    return gathered.reshape(-1, gathered.shape[2])


def _rows8(a):
    a = a.reshape(-1, 128)
    pad = (-a.shape[0]) % 8
    return jnp.pad(a, ((0, pad), (0, 0))) if pad else a


def _qkv_columns(group):
    return [(sec * ATTN_W + group * GROUP_W, sec * ATTN_W + (group + 1) * GROUP_W) for sec in range(3)]


def _device_step(x, target, small, shards, place):
    seq = x.shape[0]
    g0, g1, g2, g3 = small["norm_pre_mix"], small["norm_post_mix"], small["norm_pre_mlp"], small["norm_post_mlp"]
    w_sp = small["w_spatial"]
    b_col = small["b_spatial"].reshape(GMLP_GROUPS, CHUNK, 1)
    ln_g, ln_b = small["ln_v_gain"], small["ln_v_bias"]

    staged = _stage_weights(shards)
    w_in = _full_weight("w_in", _run_exchange(_gather(staged[:1]), "gather_w_in")[0])
    tables = _rope_tables(seq)
    (*hq, rest), gathered = _in_proj(x, g0, w_in, *tables[1], rider=_gather(staged[1:]))
    full = {n: _full_weight(n, gw) for n, gw in zip(BIG[1:], gathered)}
    h, qkv = hq[:N_GROUPS], hq[N_GROUPS:]

    o_l = []
    for g, dil in enumerate(DILATIONS):
        o_l.extend(_attn_fwd(qkv[g], dil))
    *ya_l, yg, mg, y, x1 = _mix_fwd(o_l, rest, x, w_sp, b_col, ln_g, ln_b, full["w_branch_attn"], full["w_branch_gmlp"],
                                    full["w_out"], g1)
    ya, lse = ya_l[0::2], ya_l[1::2]
    h2, a, dy2, dout, loss8, dg3 = _mlp_fwd(x1, g2, g3, full["w_mlp_in"], full["w_mlp_out"], target)
    dap, dx1, dy, dg2, dg1 = _mlp_bwd(dy2, a, full["w_mlp_out"], full["w_mlp_in"], dout, x1, y, g2, g1)
    d_wmo = _tn_matmul(a, dy2, "grad_w_mlp_out", 1024, 1024, square_a=True)
    d_wmi = _tn_matmul(h2, dap, "grad_w_mlp_in", 1024, 1024, column_shards=True)

    mlp = _GradReduction({"w_mlp_in": d_wmi, "w_mlp_out": d_wmo.reshape(N_CHIPS, D_FF // N_CHIPS, D_MODEL)}, place, "mlp")
    (*dya, drest, d_wout, d_wba, d_wbg, d_wsp, d_bb, d_lg, d_lb), mlp_others = _mix_bwd(
        dy, ya[0], yg, mg, rest, full["w_out"], full["w_branch_attn"], full["w_branch_gmlp"], w_sp, b_col, ln_g, ln_b,
        rider=mlp.pair_exchange())
    mix = _GradReduction({"w_branch_attn": d_wba, "w_branch_gmlp": d_wbg,
                          "w_out": d_wout.reshape(N_CHIPS, D_MODEL // N_CHIPS, D_MODEL)}, place, "mix")
    n_mlp = len(mlp.names)
    attn = lambda g, rider: _attn_bwd(qkv[g], dya[g], ya[g], lse[g], *tables[DILATIONS[g]], DILATIONS[g], rider=rider)
    dqkv0, riding = attn(0, _together(mlp.chip_exchange(mlp_others), mix.pair_exchange()))
    dqkv1, riding = attn(1, _together(mlp.pair_share(riding[:n_mlp]), mix.chip_exchange(riding[n_mlp:])))
    reduced = mlp.result(riding[:n_mlp])
    dqkv2, riding = attn(2, mix.pair_share(riding[n_mlp:]))
    reduced.update(mix.result(riding))
    dqkv = [dqkv0, dqkv1, dqkv2]

    d_qkv = [_tn_matmul_residue(h[g], dqkv[g], dil, f"grad_w_in_qkv{g}") for g, dil in enumerate(DILATIONS)]
    d_rest = _tn_matmul(h[0], drest, "grad_w_in_rest", 1024, 1024)
    d_win = jnp.concatenate([d_qkv[g][:, s * GROUP_W:(s + 1) * GROUP_W] for s in range(3) for g in range(N_GROUPS)]
                            + [d_rest], axis=1)
    shard_w = IN_W // N_CHIPS
    d_win = jnp.stack([d_win[:, j * shard_w:(j + 1) * shard_w] for j in range(N_CHIPS)], axis=0)
    first = _GradReduction({"w_in": d_win}, place, "w_in")
    w_qkv = [jnp.concatenate([w_in[:, lo:hi] for lo, hi in _qkv_columns(g)], axis=1) for g in range(N_GROUPS)]
    w_rest = w_in[:, QKV_W:]
    tiles = seq // IN_PROJ_BWD_TM
    so_far = (lax.empty((seq, D_MODEL), F32), jnp.zeros((1, D_MODEL), F32))
    in_bwd = lambda so_far, span, rider: _in_proj_bwd(dqkv, drest, w_qkv, w_rest, x, dx1, g0, so_far, span, rider=rider)
    so_far, riding = in_bwd(so_far, (0, tiles // 4), first.pair_exchange())
    so_far, riding = in_bwd(so_far, (tiles // 4, tiles // 2), first.chip_exchange(riding))
    shared = _run_exchange(first.pair_share(riding), "w_in_pair_share")
    (grad_x, dg0), _ = in_bwd(so_far, (3 * tiles // 4, tiles // 4), None)
    reduced.update(first.result(shared))
    little = {"norm_pre_mix": dg0, "w_spatial": d_wsp, "b_spatial": d_bb[:, :, 0], "ln_v_gain": d_lg, "ln_v_bias": d_lb,
              "norm_post_mix": dg1, "norm_pre_mlp": dg2, "norm_post_mlp": dg3}
    return loss8[0, 0], grad_x, reduced, little


def kernel(x, norm_pre_mix, w_in, w_spatial, b_spatial, ln_v_gain, ln_v_bias, w_branch_attn, w_branch_gmlp, w_out, norm_post_mix, norm_pre_mlp, w_mlp_in, w_mlp_out, norm_post_mlp, loss_target, m_norm_pre_mix, m_w_in, m_w_spatial, m_b_spatial, m_ln_v_gain, m_ln_v_bias, m_w_branch_attn, m_w_branch_gmlp, m_w_out, m_norm_post_mix, m_norm_pre_mlp, m_w_mlp_in, m_w_mlp_out, m_norm_post_mlp, v_norm_pre_mix, v_w_in, v_w_spatial, v_b_spatial, v_ln_v_gain, v_ln_v_bias, v_w_branch_attn, v_w_branch_gmlp, v_w_out, v_norm_post_mix, v_norm_pre_mlp, v_w_mlp_in, v_w_mlp_out, v_norm_post_mlp):
    given = dict(norm_pre_mix=norm_pre_mix, w_in=w_in, w_spatial=w_spatial, b_spatial=b_spatial, ln_v_gain=ln_v_gain,
                 ln_v_bias=ln_v_bias, w_branch_attn=w_branch_attn, w_branch_gmlp=w_branch_gmlp, w_out=w_out,
                 norm_post_mix=norm_post_mix, norm_pre_mlp=norm_pre_mlp, w_mlp_in=w_mlp_in, w_mlp_out=w_mlp_out,
                 norm_post_mlp=norm_post_mlp)
    moments_m = dict(norm_pre_mix=m_norm_pre_mix, w_in=m_w_in, w_spatial=m_w_spatial, b_spatial=m_b_spatial,
                     ln_v_gain=m_ln_v_gain, ln_v_bias=m_ln_v_bias, w_branch_attn=m_w_branch_attn,
                     w_branch_gmlp=m_w_branch_gmlp, w_out=m_w_out, norm_post_mix=m_norm_post_mix,
                     norm_pre_mlp=m_norm_pre_mlp, w_mlp_in=m_w_mlp_in, w_mlp_out=m_w_mlp_out, norm_post_mlp=m_norm_post_mlp)
    moments_v = dict(norm_pre_mix=v_norm_pre_mix, w_in=v_w_in, w_spatial=v_w_spatial, b_spatial=v_b_spatial,
                     ln_v_gain=v_ln_v_gain, ln_v_bias=v_ln_v_bias, w_branch_attn=v_w_branch_attn,
                     w_branch_gmlp=v_w_branch_gmlp, w_out=v_w_out, norm_post_mix=v_norm_post_mix,
                     norm_pre_mlp=v_norm_pre_mlp, w_mlp_in=v_w_mlp_in, w_mlp_out=v_w_mlp_out, norm_post_mlp=v_norm_post_mlp)
    cx, cy, cc = lax.axis_index("x"), lax.axis_index("y"), lax.axis_index("c")

    shards = [given[n][0] for n in BIG]
    small = {n: given[n][0] if given[n].ndim > 2 else given[n] for n in SMALL}
    place = jnp.stack([cc, 2 * cx + cy]).astype(jnp.int32)
    loss, grad_x, grad_shard, grads = _device_step(x[0], loss_target[0], small, shards, place)
    loss = lax.psum(loss, ("x", "y", "c"))

    packed = jnp.concatenate([_rows8(grads[n]) for n in SMALL], axis=0)
    summed = _all_reduce_small(packed)
    row = 0
    for n in SMALL:
        shape = given[n][0].shape
        cnt = -(-(given[n][0].size // 128) // 8) * 8
        grad_shard[n] = summed[row:row + given[n][0].size // 128].reshape(shape)
        row += cnt

    grad_out, deltas, new_m, new_v = {}, {}, {}, {}
    for n in ORDER:
        shape = given[n].shape
        two_d = (-1, shape[-1])
        outs = _adamw(given[n].reshape(two_d), grad_shard[n].reshape(two_d), moments_m[n].reshape(two_d),
                      moments_v[n].reshape(two_d), "adamw_" + n)
        grad_out[n], deltas[n], new_m[n], new_v[n] = [o.reshape(shape) for o in outs]
    return (loss, grad_x[None], *[grad_out[n] for n in ORDER], *[deltas[n] for n in ORDER], *[new_m[n] for n in ORDER],
            *[new_v[n] for n in ORDER])
```

```python
import math

import jax
import jax.numpy as jnp
from jax import lax
from jax.experimental import pallas as pl
from jax.experimental.pallas import tpu as pltpu

F32 = jnp.float32
BF16 = jnp.bfloat16
MESH = pl.DeviceIdType.MESH

D_MODEL = 1024
HEAD_DIM = 64
HEADS_PER_GROUP = 4
GROUP_W = HEADS_PER_GROUP * HEAD_DIM
DILATIONS = (1, 4, 16)
N_GROUPS = len(DILATIONS)
ATTN_W = N_GROUPS * GROUP_W
QKV_W = 3 * ATTN_W
GMLP_W = 512
GMLP_GROUPS = 4
CHUNK = 128
REST_W = 2 * GMLP_W + 2 * D_MODEL
IN_W = QKV_W + REST_W
D_FF = 4096
QBLK = 128
ROPE_THETA = 10000.0
EPS = 1e-6
NEG = -1e30
SCALE = HEAD_DIM ** -0.5
N_CHIPS = 4

ADAM_LR = 0.001
ADAM_B1 = 0.9
ADAM_B2 = 0.999
ADAM_EPS = 1e-08
ADAM_WD = 0.01
ADAM_STEP = 10

MIB = 1024 * 1024
HBM_SPEC = pl.BlockSpec(memory_space=pltpu.HBM)
VMEM_SPEC = pl.BlockSpec(memory_space=pltpu.VMEM)


MLP_TM = 256


def _params(semantics, vmem_mib):
    return pltpu.CompilerParams(dimension_semantics=semantics, vmem_limit_bytes=vmem_mib * MIB)


def _in_hbm(a):
    return pltpu.with_memory_space_constraint(a, pltpu.HBM) if a.size * a.dtype.itemsize >= MIB else a


def _resident(shape):
    return pl.BlockSpec(shape, lambda *_: (0,) * len(shape), pipeline_mode=pl.Buffered(1))


def _dot(a, b):
    return jnp.dot(a, b, preferred_element_type=F32)


def _dot_nt(a, b):
    return lax.dot_general(a, b, (((1,), (1,)), ((), ())), preferred_element_type=F32)


def _dot_tn(a, b):
    return lax.dot_general(a, b, (((0,), (0,)), ((), ())), preferred_element_type=F32)


_GELU_C = math.sqrt(2.0 / math.pi)


def _gelu(x):
    return x * (0.5 * (1.0 + jnp.tanh(_GELU_C * (x + 0.044715 * (x * x * x)))))


def _gelu_grad(x):
    t = jnp.tanh(_GELU_C * (x + 0.044715 * (x * x * x)))
    return 0.5 * (1.0 + t) + 0.5 * x * (1.0 - t * t) * (_GELU_C * (1.0 + 3.0 * 0.044715 * (x * x)))


def _rsqrt_ms(v):
    return lax.rsqrt(jnp.mean(v * v, axis=-1, keepdims=True) + EPS)


def _rmsnorm_bwd(dn, src, gain):
    r = _rsqrt_ms(src)
    t = gain * dn
    dgain = jnp.sum(dn * (src * r), axis=0, keepdims=True)
    dsrc = r * t - src * ((r * r * r) * jnp.mean(t * src, axis=-1, keepdims=True))
    return dsrc, dgain


def _rot_half(v):
    w = v.shape[-1]
    lane = lax.broadcasted_iota(jnp.int32, v.shape, v.ndim - 1)
    return jnp.where((lane % HEAD_DIM) < HEAD_DIM // 2, pltpu.roll(v, w - HEAD_DIM // 2, v.ndim - 1),
                     pltpu.roll(v, HEAD_DIM // 2, v.ndim - 1))


def _head_masks(shape):
    lane = lax.broadcasted_iota(jnp.int32, shape, 1)
    return [(lane >= h * HEAD_DIM) & (lane < (h + 1) * HEAD_DIM) for h in range(HEADS_PER_GROUP)]


def _head_stack(block, hmask):
    zero = jnp.zeros((), block.dtype)
    return jnp.concatenate([jnp.where(hm, block, zero) for hm in hmask], axis=0)


LANES = 128


def _put_residue(slab, val, out_ref, dil, width, col0):
    tm, w = val.shape
    if dil == 1:
        out_ref[:, col0:col0 + w] = val.astype(out_ref.dtype)
        return
    for k in range(w // LANES):
        slab[k] = val[:, k * LANES:(k + 1) * LANES]
    for r in range(dil):
        for k in range(w // LANES):
            c = r * width + col0 + k * LANES
            out_ref[:, c:c + LANES] = slab[k, pl.ds(r, tm // dil, stride=dil), :].astype(out_ref.dtype)


def _get_tokens(slab, in_ref, dil, width, col0, w):
    if dil == 1:
        return in_ref[:, col0:col0 + w].astype(F32)
    rows = in_ref.shape[0]
    for r in range(dil):
        for k in range(w // LANES):
            c = r * width + col0 + k * LANES
            slab[k, pl.ds(r, rows, stride=dil), :] = in_ref[:, c:c + LANES].astype(F32)
    return jnp.concatenate([slab[k] for k in range(w // LANES)], axis=1)


def _rope_tables(seq):
    half = HEAD_DIM // 2
    inv_freq = ROPE_THETA ** (-jnp.arange(half, dtype=F32) / half)
    freq = jnp.tile(inv_freq, LANES // half).reshape(1, LANES)
    tm = 512

    def body(f_ref, *refs):
        outs, slab_c, slab_s = refs[:-2], refs[-2], refs[-1]
        row = lax.broadcasted_iota(jnp.int32, (tm, LANES), 0) + pl.program_id(0) * tm
        lane = lax.broadcasted_iota(jnp.int32, (tm, LANES), 1)
        ang = row.astype(F32) * f_ref[...]
        cos = jnp.cos(ang)
        sin = jnp.where((lane % HEAD_DIM) < half, -jnp.sin(ang), jnp.sin(ang))
        slab_c[0] = cos
        slab_s[0] = sin
        for i, dil in enumerate(DILATIONS):
            for tab, slab in ((outs[2 * i], slab_c), (outs[2 * i + 1], slab_s)):
                for r in range(dil):
                    piece = slab[0, pl.ds(r, tm // dil, stride=dil), :] if dil > 1 else slab[0]
                    for k in range(GROUP_W // LANES):
                        tab[:, r * GROUP_W + k * LANES:r * GROUP_W + (k + 1) * LANES] = piece

    outs = pl.pallas_call(
        body, name="rope_tables", grid=(seq // tm,),
        in_specs=[pl.BlockSpec((1, LANES), lambda i: (0, 0))],
        out_specs=[pl.BlockSpec((tm // d, d * GROUP_W), lambda i: (i, 0)) for d in DILATIONS for _ in range(2)],
        out_shape=[jax.ShapeDtypeStruct((seq // d, d * GROUP_W), F32) for d in DILATIONS for _ in range(2)],
        scratch_shapes=[pltpu.VMEM((1, tm, LANES), F32)] * 2,
        compiler_params=_params(("arbitrary",), 32),
    )(freq)
    return {d: (outs[2 * i], outs[2 * i + 1]) for i, d in enumerate(DILATIONS)}


def _in_proj(x, g0, w_in, cos_t, sin_t, rider=None):
    seq = x.shape[0]
    tm, tn = 256, GROUP_W
    n_qk = 2 * ATTN_W // tn
    n_qkv = QKV_W // tn

    def body(x_ref, g_ref, w_ref, cos_ref, sin_ref, *refs):
        h_refs, qkv_refs, rest_ref, slab = refs[:N_GROUPS], refs[N_GROUPS:2 * N_GROUPS], refs[2 * N_GROUPS], refs[-1]
        xv = x_ref[...]
        hf = (xv * _rsqrt_ms(xv)) * g_ref[...]
        hb = hf.astype(BF16)
        for g, dil in enumerate(DILATIONS):
            _put_residue(slab, hf, h_refs[g], dil, D_MODEL, 0)
        cos, sin = cos_ref[...], sin_ref[...]
        for j in range(IN_W // tn):
            p = _dot(hb, w_ref[:, j * tn:(j + 1) * tn])
            if j < n_qkv:
                if j < n_qk:
                    p = p * cos + _rot_half(p) * sin
                section, g = divmod(j, N_GROUPS)
                _put_residue(slab, p, qkv_refs[g], DILATIONS[g], 3 * GROUP_W, section * GROUP_W)
            else:
                rest_ref[:, (j - n_qkv) * tn:(j - n_qkv + 1) * tn] = p.astype(BF16)

    return _call(
        body, name="in_proj", grid=(seq // tm,),
        in_specs=[pl.BlockSpec((tm, D_MODEL), lambda i: (i, 0)),
                  pl.BlockSpec((1, D_MODEL), lambda i: (0, 0)),
                  _resident((D_MODEL, IN_W)),
                  pl.BlockSpec((tm, GROUP_W), lambda i: (i, 0)),
                  pl.BlockSpec((tm, GROUP_W), lambda i: (i, 0))],
        out_specs=[pl.BlockSpec((tm // d, d * D_MODEL), lambda i: (i, 0)) for d in DILATIONS]
        + [pl.BlockSpec((tm // d, d * 3 * GROUP_W), lambda i: (i, 0)) for d in DILATIONS]
        + [pl.BlockSpec((tm, REST_W), lambda i: (i, 0))],
        out_shape=[jax.ShapeDtypeStruct((seq // d, d * D_MODEL), BF16) for d in DILATIONS]
        + [jax.ShapeDtypeStruct((seq // d, d * 3 * GROUP_W), BF16) for d in DILATIONS]
        + [jax.ShapeDtypeStruct((seq, REST_W), BF16)],
        scratch_shapes=[pltpu.VMEM((D_MODEL // LANES, tm, LANES), F32)],
        params=_params(("arbitrary",), 56), args=(x, g0, w_in, cos_t, sin_t), rider=rider)


def _band_masks():
    qi = lax.broadcasted_iota(jnp.int32, (QBLK, QBLK), 0)
    kj = lax.broadcasted_iota(jnp.int32, (QBLK, QBLK), 1)
    return kj <= qi, kj >= qi


def _attn_tile(length):
    return min(512, length)


def _attn_fwd(qkv, dil, rider=None):
    length = qkv.shape[0]
    tq = _attn_tile(length)
    nsub = tq // QBLK
    nblk = length // tq

    def body(q_ref, k_ref, v_ref, kp_ref, vp_ref, o_ref, l_ref):
        n = pl.program_id(1)
        mask_c, mask_p0 = _band_masks()
        hmask = _head_masks((QBLK, GROUP_W))
        zero = jnp.zeros((), BF16)
        for b in range(nsub):
            rows = slice(b * QBLK, (b + 1) * QBLK)
            q = q_ref[rows, :]
            kc, vc = k_ref[rows, :], v_ref[rows, :]
            if b == 0:
                kp, vp = kp_ref[...], vp_ref[...]
                mask_p = mask_p0 & (n > 0)
            else:
                prow = slice((b - 1) * QBLK, b * QBLK)
                kp, vp = k_ref[prow, :], v_ref[prow, :]
                mask_p = mask_p0
            o_acc = jnp.zeros((QBLK, GROUP_W), F32)
            l_acc = jnp.zeros((QBLK, GROUP_W), F32)
            for h in range(HEADS_PER_GROUP):
                hm = hmask[h]
                sc = jnp.where(mask_c, _dot_nt(q, jnp.where(hm, kc, zero)) * SCALE, NEG)
                sp = jnp.where(mask_p, _dot_nt(q, jnp.where(hm, kp, zero)) * SCALE, NEG)
                m = jnp.maximum(jnp.max(sc, axis=-1, keepdims=True), jnp.max(sp, axis=-1, keepdims=True))
                pc, pp = jnp.exp(sc - m), jnp.exp(sp - m)
                den = jnp.sum(pc, axis=-1, keepdims=True) + jnp.sum(pp, axis=-1, keepdims=True)
                pv = _dot(pc.astype(BF16), jnp.where(hm, vc, zero)) + _dot(pp.astype(BF16), jnp.where(hm, vp, zero))
                o_acc = o_acc + pv / den
                l_acc = l_acc + jnp.where(hm, m + jnp.log(den), 0.0)
            o_ref[rows, :] = o_acc
            l_ref[rows, :] = l_acc

    cur = lambda sec: pl.BlockSpec((tq, GROUP_W), lambda r, n: (n, r * 3 + sec))
    prev = lambda sec: pl.BlockSpec((QBLK, GROUP_W), lambda r, n: (jnp.maximum(n * nsub - 1, 0), r * 3 + sec))
    return _call(
        body, name=f"attn_fwd_d{dil}", grid=(dil, nblk),
        in_specs=[cur(0), cur(1), cur(2), prev(1), prev(2)],
        out_specs=[pl.BlockSpec((tq, GROUP_W), lambda r, n: (n, r))] * 2,
        out_shape=[jax.ShapeDtypeStruct((length, dil * GROUP_W), F32)] * 2, scratch_shapes=[],
        params=_params(("arbitrary", "arbitrary"), 32), args=(qkv, qkv, qkv, qkv, qkv), rider=rider)


def _attn_bwd(qkv, dy, y, lse, cos_t, sin_t, dil, rider=None):
    length = qkv.shape[0]
    tq = _attn_tile(length)
    nsub = tq // QBLK
    nblk = length // tq

    def body(q_ref, k_ref, v_ref, kp_ref, vp_ref, qn_ref, dy_ref, y_ref, l_ref, dyn_ref, yn_ref, ln_ref,
             cos_ref, sin_ref, out_ref, dq_s, dk_s, dv_s):
        n = pl.program_id(1)
        mask_c, mask_p0 = _band_masks()
        hmask = _head_masks((QBLK, GROUP_W))
        sub = lambda ref, b: ref[b * QBLK:(b + 1) * QBLK, :]
        kbd = [_head_stack(kp_ref[...], hmask)] + [_head_stack(sub(k_ref, b), hmask) for b in range(nsub)]
        vbd = [_head_stack(vp_ref[...], hmask)] + [_head_stack(sub(v_ref, b), hmask) for b in range(nsub)]
        dk_s[...] = jnp.zeros(dk_s.shape, F32)
        dv_s[...] = jnp.zeros(dv_s.shape, F32)

        def query_block(q, dyv, yv, lv, key_blocks):
            dyb = dyv.astype(BF16)
            qbd = _head_stack(q, hmask)
            dybd = jnp.concatenate([jnp.where(hm, dyv, 0.0).astype(BF16) for hm in hmask], axis=0)
            prod = dyv * yv
            deltas = [jnp.sum(jnp.where(hm, prod, 0.0), axis=-1, keepdims=True) for hm in hmask]
            lses = [jnp.max(jnp.where(hm, lv, NEG), axis=-1, keepdims=True) for hm in hmask]
            dq = jnp.zeros((QBLK, GROUP_W), F32)
            for kb, mask in key_blocks:
                s = _dot_nt(q, kbd[kb]) * SCALE
                dp = _dot_nt(dyb, vbd[kb])
                ps, dss = [], []
                for h in range(HEADS_PER_GROUP):
                    cols = slice(h * QBLK, (h + 1) * QBLK)
                    p = jnp.exp(jnp.where(mask, s[:, cols] - lses[h], NEG))
                    ps.append(p.astype(BF16))
                    dss.append((p * (dp[:, cols] - deltas[h])).astype(BF16))
                dq = dq + _dot(jnp.concatenate(dss, axis=1), kbd[kb])
                if kb >= 1:
                    krows = slice((kb - 1) * QBLK, kb * QBLK)
                    dv_s[krows, :] += _dot_tn(jnp.concatenate(ps, axis=0), dybd)
                    dk_s[krows, :] += _dot_tn(jnp.concatenate(dss, axis=0), qbd) * SCALE
            return dq * SCALE

        for b in range(nsub):
            mask_p = mask_p0 & (n > 0) if b == 0 else mask_p0
            dq_s[b * QBLK:(b + 1) * QBLK, :] = query_block(sub(q_ref, b), sub(dy_ref, b), sub(y_ref, b), sub(l_ref, b),
                                                            [(b, mask_p), (b + 1, mask_c)])
        query_block(qn_ref[...], dyn_ref[...], yn_ref[...], ln_ref[...], [(nsub, mask_p0 & (n < nblk - 1))])
        cos, sin = cos_ref[...], sin_ref[...]
        dq, dk = dq_s[...], dk_s[...]
        out_ref[:, 0:GROUP_W] = (dq * cos - _rot_half(dq) * sin).astype(BF16)
        out_ref[:, GROUP_W:2 * GROUP_W] = (dk * cos - _rot_half(dk) * sin).astype(BF16)
        out_ref[:, 2 * GROUP_W:3 * GROUP_W] = dv_s[...].astype(BF16)

    cur = lambda sec: pl.BlockSpec((tq, GROUP_W), lambda r, n: (n, r * 3 + sec))
    prev = lambda sec: pl.BlockSpec((QBLK, GROUP_W), lambda r, n: (jnp.maximum(n * nsub - 1, 0), r * 3 + sec))
    nxt_q = pl.BlockSpec((QBLK, GROUP_W), lambda r, n: (jnp.minimum((n + 1) * nsub, nblk * nsub - 1), r * 3))
    tok = pl.BlockSpec((tq, GROUP_W), lambda r, n: (n, r))
    tok_next = pl.BlockSpec((QBLK, GROUP_W), lambda r, n: (jnp.minimum((n + 1) * nsub, nblk * nsub - 1), r))
    (out,), riding = _call(
        body, name=f"attn_bwd_d{dil}", grid=(dil, nblk),
        in_specs=[cur(0), cur(1), cur(2), prev(1), prev(2), nxt_q,
                  tok, tok, tok, tok_next, tok_next, tok_next, tok, tok],
        out_specs=[pl.BlockSpec((tq, 3 * GROUP_W), lambda r, n: (n, r))],
        out_shape=[jax.ShapeDtypeStruct((length, dil * 3 * GROUP_W), BF16)],
        scratch_shapes=[pltpu.VMEM((tq, GROUP_W), F32)] * 3,
        params=_params(("arbitrary", "arbitrary"), 32),
        args=(qkv, qkv, qkv, qkv, qkv, qkv, dy, y, lse, dy, y, lse, cos_t, sin_t), rider=rider)
    return out, riding


def _layernorm_stats(z):
    mu = jnp.mean(z, axis=-1, keepdims=True)
    zc = z - mu
    rstd = lax.rsqrt(jnp.mean(zc * zc, axis=-1, keepdims=True) + EPS)
    return zc * rstd, rstd


def _tril_mask():
    row = lax.broadcasted_iota(jnp.int32, (CHUNK, CHUNK), 0)
    col = lax.broadcasted_iota(jnp.int32, (CHUNK, CHUNK), 1)
    return col <= row


def _mix_fwd(o_l, rest, x, w_sp, b_col, ln_g, ln_b, w_ba, w_bg, w_out, g1):
    seq = x.shape[0]
    tm = 256

    def body(o0, l0, o1, l1, o2, l2, up_ref, zp_ref, gap_ref, gbp_ref, x_ref, wsp_ref, bcol_ref, lg_ref, lb_ref,
             wba_ref, wbg_ref, wout_ref, g1_ref, ya0, lj0, ya1, lj1, ya2, lj2, yg_ref, mg_ref, y_ref, x1_ref, slab):
        outs = [_get_tokens(slab, o, d, GROUP_W, 0, GROUP_W) for o, d in zip((o0, o1, o2), DILATIONS)]
        lses = [_get_tokens(slab, l, d, GROUP_W, 0, GROUP_W) for l, d in zip((l0, l1, l2), DILATIONS)]
        m = jnp.maximum(jnp.maximum(lses[0], lses[1]), lses[2])
        es = [jnp.exp(l - m) for l in lses]
        tot = es[0] + es[1] + es[2]
        ya = (es[0] * outs[0] + es[1] * outs[1] + es[2] * outs[2]) / tot
        lj = m + jnp.log(tot)
        for ya_ref, lj_ref, d in zip((ya0, ya1, ya2), (lj0, lj1, lj2), DILATIONS):
            _put_residue(slab, ya, ya_ref, d, GROUP_W, 0)
            _put_residue(slab, lj, lj_ref, d, GROUP_W, 0)
        zhat, _ = _layernorm_stats(_gelu(zp_ref[...].astype(F32)))
        zln = (zhat * lg_ref[...] + lb_ref[...]).astype(BF16)
        u = _gelu(up_ref[...].astype(F32))
        tril = _tril_mask()
        for g in range(GMLP_GROUPS):
            wm = jnp.where(tril, wsp_ref[g], 0.0).astype(BF16)
            cols = slice(g * CHUNK, (g + 1) * CHUNK)
            for c in range(tm // CHUNK):
                rows = slice(c * CHUNK, (c + 1) * CHUNK)
                sz = _dot(wm, zln[rows, cols]) + bcol_ref[g]
                yg_ref[rows, cols] = (u[rows, cols] * sz).astype(BF16)
        a = _dot(ya.astype(BF16), wba_ref[...])
        bm = _dot(yg_ref[...], wbg_ref[...])
        merged = (jax.nn.sigmoid(gap_ref[...].astype(F32)) * a + jax.nn.sigmoid(gbp_ref[...].astype(F32)) * bm).astype(BF16)
        mg_ref[...] = merged
        yv = _dot(merged, wout_ref[...])
        y_ref[...] = yv
        x1_ref[...] = x_ref[...] + (yv * _rsqrt_ms(yv)) * g1_ref[...]

    tok = lambda w: pl.BlockSpec((tm, w), lambda i: (i, 0))
    res = lambda d: pl.BlockSpec((tm // d, d * GROUP_W), lambda i: (i, 0))
    full = lambda *s: pl.BlockSpec(s, lambda i: (0,) * len(s))
    res_specs = [res(d) for d in DILATIONS for _ in range(2)]
    return pl.pallas_call(
        body, name="mix_fwd", grid=(seq // tm,),
        in_specs=res_specs + [
            pl.BlockSpec((tm, GMLP_W), lambda i: (i, 0)), pl.BlockSpec((tm, GMLP_W), lambda i: (i, 1)),
            pl.BlockSpec((tm, D_MODEL), lambda i: (i, 1)), pl.BlockSpec((tm, D_MODEL), lambda i: (i, 2)),
            tok(D_MODEL), full(GMLP_GROUPS, CHUNK, CHUNK), full(GMLP_GROUPS, CHUNK, 1), full(1, GMLP_W), full(1, GMLP_W),
            full(GROUP_W, D_MODEL), full(GMLP_W, D_MODEL), full(D_MODEL, D_MODEL), full(1, D_MODEL)],
        out_specs=res_specs + [tok(GMLP_W), tok(D_MODEL), tok(D_MODEL), tok(D_MODEL)],
        out_shape=[jax.ShapeDtypeStruct((seq // d, d * GROUP_W), F32) for d in DILATIONS for _ in range(2)]
        + [jax.ShapeDtypeStruct((seq, GMLP_W), BF16), jax.ShapeDtypeStruct((seq, D_MODEL), BF16),
           jax.ShapeDtypeStruct((seq, D_MODEL), F32), jax.ShapeDtypeStruct((seq, D_MODEL), F32)],
        scratch_shapes=[pltpu.VMEM((GROUP_W // LANES, tm, LANES), F32)],
        compiler_params=_params(("arbitrary",), 48),
    )(*map(_in_hbm, (*o_l, rest, rest, rest, rest, x, w_sp, b_col, ln_g, ln_b, w_ba, w_bg, w_out, g1)))


def _mlp_fwd(x1, g2, g3, w_mi, w_mo, target):
    seq = x1.shape[0]
    tm, tf = MLP_TM, 512

    def body(x1_ref, g2_ref, g3_ref, wmi_ref, wmo_ref, t_ref, h2_ref, a_ref, dy2_ref, dout_ref, loss_ref, dg3_ref, sq_s):
        @pl.when(pl.program_id(0) == 0)
        def _():
            loss_ref[...] = jnp.zeros(loss_ref.shape, F32)
            dg3_ref[...] = jnp.zeros(dg3_ref.shape, F32)

        xv = x1_ref[...]
        hb = ((xv * _rsqrt_ms(xv)) * g2_ref[...]).astype(BF16)
        h2_ref[...] = hb
        for j in range(D_FF // tf):
            cols = slice(j * tf, (j + 1) * tf)
            a = jnp.maximum(_dot(hb, wmi_ref[:, cols]), 0.0)
            a_ref[:, cols] = a.astype(BF16)
            sq_s[:, cols] = (a * a).astype(BF16)
        y2 = _dot(sq_s[...], wmo_ref[...])
        r3 = _rsqrt_ms(y2)
        out = xv + (y2 * r3) * g3_ref[...]
        diff = out - t_ref[...]
        tile_loss = 0.5 * jnp.sum(jnp.mean(diff * diff, axis=-1, keepdims=True), axis=0, keepdims=True)
        loss_ref[...] += jnp.broadcast_to(tile_loss, loss_ref.shape)
        dout = diff * (1.0 / D_MODEL)
        dout_ref[...] = dout
        dy2, dg3 = _rmsnorm_bwd(dout, y2, g3_ref[...])
        dy2_ref[...] = dy2.astype(BF16)
        dg3_ref[...] += dg3

    tok = lambda w: pl.BlockSpec((tm, w), lambda i: (i, 0))
    vec = pl.BlockSpec((1, D_MODEL), lambda i: (0, 0))
    return pl.pallas_call(
        body, name="mlp_fwd", grid=(seq // tm,),
        in_specs=[tok(D_MODEL), vec, vec, _resident((D_MODEL, D_FF)), _resident((D_FF, D_MODEL)), tok(D_MODEL)],
        out_specs=[tok(D_MODEL), tok(D_FF), tok(D_MODEL), tok(D_MODEL), pl.BlockSpec((8, 128), lambda i: (0, 0)), vec],
        out_shape=[jax.ShapeDtypeStruct((seq, D_MODEL), BF16), jax.ShapeDtypeStruct((seq, D_FF), BF16),
                   jax.ShapeDtypeStruct((seq, D_MODEL), BF16), jax.ShapeDtypeStruct((seq, D_MODEL), F32),
                   jax.ShapeDtypeStruct((8, 128), F32), jax.ShapeDtypeStruct((1, D_MODEL), F32)],
        scratch_shapes=[pltpu.VMEM((tm, D_FF), BF16)],
        compiler_params=_params(("arbitrary",), 56),
    )(*map(_in_hbm, (x1, g2, g3, w_mi, w_mo, target)))


def _mlp_bwd(dy2, a, w_mo, w_mi, dout, x1, y, g2, g1):
    seq = x1.shape[0]
    tm, tf = MLP_TM, 512

    def body(dy2_ref, a_ref, wmo_ref, wmi_ref, dout_ref, x1_ref, y_ref, g2_ref, g1_ref,
             dap_ref, dx1_ref, dy_ref, dg2_ref, dg1_ref):
        @pl.when(pl.program_id(0) == 0)
        def _():
            dg2_ref[...] = jnp.zeros(dg2_ref.shape, F32)
            dg1_ref[...] = jnp.zeros(dg1_ref.shape, F32)

        dy2v = dy2_ref[...]
        for j in range(D_FF // tf):
            cols = slice(j * tf, (j + 1) * tf)
            da2 = _dot_nt(dy2v, wmo_ref[cols, :])
            dap_ref[:, cols] = (da2 * (2.0 * a_ref[:, cols].astype(F32))).astype(BF16)
        dh2 = _dot_nt(dap_ref[...], wmi_ref[...])
        dres, dg2 = _rmsnorm_bwd(dh2, x1_ref[...], g2_ref[...])
        dx1 = dout_ref[...] + dres
        dx1_ref[...] = dx1
        dg2_ref[...] += dg2
        dyv, dg1 = _rmsnorm_bwd(dx1, y_ref[...], g1_ref[...])
        dy_ref[...] = dyv.astype(BF16)
        dg1_ref[...] += dg1

    tok = lambda w: pl.BlockSpec((tm, w), lambda i: (i, 0))
    vec = pl.BlockSpec((1, D_MODEL), lambda i: (0, 0))
    return pl.pallas_call(
        body, name="mlp_bwd", grid=(seq // tm,),
        in_specs=[tok(D_MODEL), tok(D_FF), _resident((D_FF, D_MODEL)), _resident((D_MODEL, D_FF)),
                  tok(D_MODEL), tok(D_MODEL), tok(D_MODEL), vec, vec],
        out_specs=[tok(D_FF), tok(D_MODEL), tok(D_MODEL), vec, vec],
        out_shape=[jax.ShapeDtypeStruct((seq, D_FF), BF16), jax.ShapeDtypeStruct((seq, D_MODEL), F32),
                   jax.ShapeDtypeStruct((seq, D_MODEL), BF16), jax.ShapeDtypeStruct((1, D_MODEL), F32),
                   jax.ShapeDtypeStruct((1, D_MODEL), F32)],
        compiler_params=_params(("arbitrary",), 56),
    )(*map(_in_hbm, (dy2, a, w_mo, w_mi, dout, x1, y, g2, g1)))


def _tn_matmul(a, b, name, bm, bn, square_a=False, column_shards=False):
    seq, m = a.shape
    n = b.shape[1]
    ts = 512

    def body(a_ref, b_ref, o_ref):
        @pl.when(pl.program_id(2) == 0)
        def _():
            o_ref[...] = jnp.zeros(o_ref.shape, F32)

        av = a_ref[...]
        if square_a:
            af = av.astype(F32)
            av = (af * af).astype(BF16)
        o_ref[...] += _dot_tn(av, b_ref[...])

    if column_shards:
        out_spec = pl.BlockSpec((None, bm, bn), lambda mi, ni, s: (ni, mi, 0))
        out_shape = jax.ShapeDtypeStruct((n // bn, m, bn), F32)
    else:
        out_spec = pl.BlockSpec((bm, bn), lambda mi, ni, s: (mi, ni))
        out_shape = jax.ShapeDtypeStruct((m, n), F32)
    return pl.pallas_call(
        body, name=name, grid=(m // bm, n // bn, seq // ts),
        in_specs=[pl.BlockSpec((ts, bm), lambda mi, ni, s: (s, mi)), pl.BlockSpec((ts, bn), lambda mi, ni, s: (s, ni))],
        out_specs=out_spec, out_shape=out_shape,
        compiler_params=_params(("arbitrary", "arbitrary", "arbitrary"), 40),
    )(_in_hbm(a), _in_hbm(b))


def _tn_matmul_residue(a, b, dil, name):
    length = a.shape[0]
    m, n = a.shape[1] // dil, b.shape[1] // dil
    ts = min(512, length)

    def body(a_ref, b_ref, o_ref):
        @pl.when((pl.program_id(0) == 0) & (pl.program_id(1) == 0))
        def _():
            o_ref[...] = jnp.zeros(o_ref.shape, F32)

        o_ref[...] += _dot_tn(a_ref[...], b_ref[...])

    return pl.pallas_call(
        body, name=name, grid=(dil, length // ts),
        in_specs=[pl.BlockSpec((ts, m), lambda r, s: (s, r)), pl.BlockSpec((ts, n), lambda r, s: (s, r))],
        out_specs=pl.BlockSpec((m, n), lambda r, s: (0, 0)),
        out_shape=jax.ShapeDtypeStruct((m, n), F32),
        compiler_params=_params(("arbitrary", "arbitrary"), 40),
    )(_in_hbm(a), _in_hbm(b))


def _mix_bwd(dy, ya, yg, mg, rest, w_out, w_ba, w_bg, w_sp, b_col, ln_g, ln_b, rider=None):
    seq = dy.shape[0]
    tm = 256

    def body(dy_ref, ya_ref, yg_ref, mg_ref, up_ref, zp_ref, gap_ref, gbp_ref, wout_ref, wba_ref, wbg_ref,
             wsp_ref, bcol_ref, lg_ref, lb_ref,
             dya0, dya1, dya2, dpr_ref, dwout_ref, dwba_ref, dwbg_ref, dwsp_ref, dbb_ref, dlg_ref, dlb_ref,
             dzln_s, du_s, slab):
        @pl.when(pl.program_id(0) == 0)
        def _():
            for ref in (dwout_ref, dwba_ref, dwbg_ref, dwsp_ref, dbb_ref, dlg_ref, dlb_ref):
                ref[...] = jnp.zeros(ref.shape, F32)

        dyv = dy_ref[...]
        dm = _dot_nt(dyv, wout_ref[...])
        dwout_ref[...] += _dot_tn(mg_ref[...], dyv)
        yab = ya_ref[...].astype(BF16)
        ygb = yg_ref[...]
        a = _dot(yab, wba_ref[...])
        bm = _dot(ygb, wbg_ref[...])
        ga = jax.nn.sigmoid(gap_ref[...].astype(F32))
        gb = jax.nn.sigmoid(gbp_ref[...].astype(F32))
        dpr_ref[:, 2 * GMLP_W:2 * GMLP_W + D_MODEL] = (dm * a * (ga * (1.0 - ga))).astype(BF16)
        dpr_ref[:, 2 * GMLP_W + D_MODEL:REST_W] = (dm * bm * (gb * (1.0 - gb))).astype(BF16)
        da = (dm * ga).astype(BF16)
        db = (dm * gb).astype(BF16)
        dwba = _dot_tn(yab, da)
        dwbg = _dot_tn(ygb, db)
        shard_w = D_MODEL // N_CHIPS
        for j in range(N_CHIPS):
            dwba_ref[j] += dwba[:, j * shard_w:(j + 1) * shard_w]
            dwbg_ref[j] += dwbg[:, j * shard_w:(j + 1) * shard_w]
        dya = _dot_nt(da, wba_ref[...])
        for dya_ref, d in zip((dya0, dya1, dya2), DILATIONS):
            _put_residue(slab, dya, dya_ref, d, GROUP_W, 0)
        dyg = _dot_nt(db, wbg_ref[...])

        zp = zp_ref[...].astype(F32)
        zhat, rstd = _layernorm_stats(_gelu(zp))
        lg = lg_ref[...]
        zln = (zhat * lg + lb_ref[...]).astype(BF16)
        up = up_ref[...].astype(F32)
        u = _gelu(up)
        tril = _tril_mask()
        for g in range(GMLP_GROUPS):
            wm = jnp.where(tril, wsp_ref[g], 0.0).astype(BF16)
            cols = slice(g * CHUNK, (g + 1) * CHUNK)
            for c in range(tm // CHUNK):
                rows = slice(c * CHUNK, (c + 1) * CHUNK)
                zb = zln[rows, cols]
                sz = _dot(wm, zb) + bcol_ref[g]
                dyg_cg = dyg[rows, cols]
                du_s[rows, cols] = dyg_cg * sz
                dsz = dyg_cg * u[rows, cols]
                dszb = dsz.astype(BF16)
                dbb_ref[g] += jnp.broadcast_to(jnp.sum(dsz, axis=-1, keepdims=True), (CHUNK, CHUNK))
                dwsp_ref[g] += jnp.where(tril, _dot_nt(dszb, zb), 0.0)
                dzln_s[rows, cols] = _dot_tn(wm, dszb)
        dzln = dzln_s[...]
        dlg_ref[...] += jnp.sum(dzln * zhat, axis=0, keepdims=True)
        dlb_ref[...] += jnp.sum(dzln, axis=0, keepdims=True)
        dzh = dzln * lg
        dz = rstd * (dzh - jnp.mean(dzh, axis=-1, keepdims=True) - zhat * jnp.mean(dzh * zhat, axis=-1, keepdims=True))
        dpr_ref[:, GMLP_W:2 * GMLP_W] = (dz * _gelu_grad(zp)).astype(BF16)
        dpr_ref[:, 0:GMLP_W] = (du_s[...] * _gelu_grad(up)).astype(BF16)

    tok = lambda w: pl.BlockSpec((tm, w), lambda i: (i, 0))
    full = lambda *s: pl.BlockSpec(s, lambda i: (0,) * len(s))
    return _call(
        body, name="mix_bwd", grid=(seq // tm,),
        in_specs=[tok(D_MODEL), tok(GROUP_W), tok(GMLP_W), tok(D_MODEL),
                  pl.BlockSpec((tm, GMLP_W), lambda i: (i, 0)), pl.BlockSpec((tm, GMLP_W), lambda i: (i, 1)),
                  pl.BlockSpec((tm, D_MODEL), lambda i: (i, 1)), pl.BlockSpec((tm, D_MODEL), lambda i: (i, 2)),
                  full(D_MODEL, D_MODEL), full(GROUP_W, D_MODEL), full(GMLP_W, D_MODEL),
                  full(GMLP_GROUPS, CHUNK, CHUNK), full(GMLP_GROUPS, CHUNK, 1), full(1, GMLP_W), full(1, GMLP_W)],
        out_specs=[pl.BlockSpec((tm // d, d * GROUP_W), lambda i: (i, 0)) for d in DILATIONS]
        + [tok(REST_W), full(D_MODEL, D_MODEL), full(N_CHIPS, GROUP_W, D_MODEL // N_CHIPS),
           full(N_CHIPS, GMLP_W, D_MODEL // N_CHIPS),
           full(GMLP_GROUPS, CHUNK, CHUNK), full(GMLP_GROUPS, CHUNK, CHUNK), full(1, GMLP_W), full(1, GMLP_W)],
        out_shape=[jax.ShapeDtypeStruct((seq // d, d * GROUP_W), F32) for d in DILATIONS]
        + [jax.ShapeDtypeStruct((seq, REST_W), BF16),
           jax.ShapeDtypeStruct((D_MODEL, D_MODEL), F32), jax.ShapeDtypeStruct((N_CHIPS, GROUP_W, D_MODEL // N_CHIPS), F32),
           jax.ShapeDtypeStruct((N_CHIPS, GMLP_W, D_MODEL // N_CHIPS), F32),
           jax.ShapeDtypeStruct((GMLP_GROUPS, CHUNK, CHUNK), F32),
           jax.ShapeDtypeStruct((GMLP_GROUPS, CHUNK, CHUNK), F32), jax.ShapeDtypeStruct((1, GMLP_W), F32),
           jax.ShapeDtypeStruct((1, GMLP_W), F32)],
        scratch_shapes=[pltpu.VMEM((tm, GMLP_W), F32), pltpu.VMEM((tm, GMLP_W), F32),
                        pltpu.VMEM((GROUP_W // LANES, tm, LANES), F32)],
        params=_params(("arbitrary",), 56),
        args=(dy, ya, yg, mg, rest, rest, rest, rest, w_out, w_ba, w_bg, w_sp, b_col, ln_g, ln_b), rider=rider)


IN_PROJ_BWD_TM = 256


def _in_proj_bwd(dqkv, drest, w_qkv, w_rest, x, dx1, g0, so_far, span, rider=None):
    seq = x.shape[0]
    tm = IN_PROJ_BWD_TM
    off, steps = span
    gx_so_far, dg_so_far = so_far

    def body(d0, d1, d2, dr_ref, w0, w1, w2, wr_ref, x_ref, dx1_ref, g_ref, dg_in_ref, gx_in_ref, gx_ref, dg_ref, slab):
        @pl.when(pl.program_id(0) == 0)
        def _():
            dg_ref[...] = dg_in_ref[...]

        dh = _dot_nt(dr_ref[...], wr_ref[...])
        for d_ref, w_ref, dil in zip((d0, d1, d2), (w0, w1, w2), DILATIONS):
            piece = d_ref[...] if dil == 1 else _get_tokens(slab, d_ref, dil, 3 * GROUP_W, 0, 3 * GROUP_W).astype(BF16)
            dh = dh + _dot_nt(piece, w_ref[...])
        dres, dg = _rmsnorm_bwd(dh, x_ref[...], g_ref[...])
        gx_ref[...] = dx1_ref[...] + dres
        dg_ref[...] += dg

    tok = lambda w: pl.BlockSpec((tm, w), lambda i: (i + off, 0))
    full = lambda *s: pl.BlockSpec(s, lambda i: (0,) * len(s))
    in_specs = ([pl.BlockSpec((tm // d, d * 3 * GROUP_W), lambda i: (i + off, 0)) for d in DILATIONS] + [tok(REST_W)]
                + [_resident((D_MODEL, 3 * GROUP_W))] * 3 + [_resident((D_MODEL, REST_W))]
                + [tok(D_MODEL), tok(D_MODEL), full(1, D_MODEL), full(1, D_MODEL), HBM_SPEC])
    return _call(
        body, name=f"in_proj_bwd_{off}", grid=(steps,), in_specs=in_specs,
        out_specs=[tok(D_MODEL), full(1, D_MODEL)],
        out_shape=[jax.ShapeDtypeStruct((seq, D_MODEL), F32), jax.ShapeDtypeStruct((1, D_MODEL), F32)],
        scratch_shapes=[pltpu.VMEM((3 * GROUP_W // LANES, tm, LANES), F32)],
        params=_params(("arbitrary",), 48), args=(*dqkv, drest, *w_qkv, w_rest, x, dx1, g0, dg_so_far, gx_so_far),
        rider=rider, aliases={len(in_specs) - 1: 0})


def _adamw(w, g, m, v, name):
    rows, cols = w.shape
    tr = 256 if rows % 256 == 0 else rows
    c1 = 1.0 - ADAM_B1 ** ADAM_STEP
    c2 = 1.0 - ADAM_B2 ** ADAM_STEP

    def body(w_ref, g_ref, m_ref, v_ref, go_ref, d_ref, nm_ref, nv_ref):
        gv = g_ref[...]
        go_ref[...] = gv
        nm = ADAM_B1 * m_ref[...] + (1.0 - ADAM_B1) * gv
        nv = ADAM_B2 * v_ref[...] + (1.0 - ADAM_B2) * (gv * gv)
        d_ref[...] = -ADAM_LR * ((nm / c1) / (jnp.sqrt(nv / c2) + ADAM_EPS) + ADAM_WD * w_ref[...])
        nm_ref[...] = nm
        nv_ref[...] = nv

    spec = pl.BlockSpec((tr, cols), lambda i: (i, 0))
    return pl.pallas_call(
        body, name=name, grid=(rows // tr,),
        in_specs=[spec] * 4, out_specs=[spec] * 4,
        out_shape=[jax.ShapeDtypeStruct((rows, cols), F32)] * 4,
        compiler_params=_params(("arbitrary",), 40),
    )(*map(_in_hbm, (w, g, m, v)))


def _place():
    x, y, c = lax.axis_index("x"), lax.axis_index("y"), lax.axis_index("c")
    chips = [(1 - x, y), (x, 1 - y), (1 - x, 1 - y)]
    return x, y, c, chips


class _Exchange:
    def __init__(self, inputs, out_shapes, n_sems, start, finish, aliases=None):
        self.inputs, self.out_shapes, self.n_sems = list(inputs), list(out_shapes), n_sems
        self.start, self.finish, self.aliases = start, finish, dict(aliases or {})

    def scratch(self):
        return [pltpu.SemaphoreType.DMA((self.n_sems,)), pltpu.SemaphoreType.DMA((self.n_sems,))]


def _together(*parts):
    ins = [len(p.inputs) for p in parts]
    outs = [len(p.out_shapes) for p in parts]

    def split(refs, counts):
        pos, pieces = 0, []
        for cnt in counts:
            pieces.append(refs[pos:pos + cnt])
            pos += cnt
        return pieces

    def run(which):
        def go(in_refs, out_refs, *sems):
            for k, (p, i, o) in enumerate(zip(parts, split(in_refs, ins), split(out_refs, outs))):
                getattr(p, which)(i, o, sems[2 * k], sems[2 * k + 1])
        return go

    both = _Exchange([a for p in parts for a in p.inputs], [s for p in parts for s in p.out_shapes], 0, run("start"),
                     run("finish"))
    both.aliases = {sum(ins[:k]) + i: sum(outs[:k]) + o for k, p in enumerate(parts) for i, o in p.aliases.items()}
    both.scratch = lambda: [s for p in parts for s in p.scratch()]
    return both


def _run_exchange(ex, name):
    n_in, n_out = len(ex.inputs), len(ex.out_shapes)

    def body(*refs):
        ins, outs, sems = refs[:n_in], refs[n_in:n_in + n_out], refs[n_in + n_out:]
        ex.start(ins, outs, *sems)
        ex.finish(ins, outs, *sems)

    return pl.pallas_call(
        body, name=name, in_specs=[HBM_SPEC] * n_in, out_specs=[HBM_SPEC] * n_out, out_shape=ex.out_shapes,
        scratch_shapes=ex.scratch(), input_output_aliases=ex.aliases,
    )(*ex.inputs)


def _call(body, *, name, grid, in_specs, out_specs, out_shape, scratch_shapes, params, args, rider=None, aliases=None):
    in_specs, out_specs, out_shape, scratch_shapes = list(in_specs), list(out_specs), list(out_shape), list(scratch_shapes)
    aliases = dict(aliases or {})
    args = [_in_hbm(a) for a in args]
    if rider is None:
        outs = pl.pallas_call(body, name=name, grid=grid, in_specs=in_specs, out_specs=out_specs, out_shape=out_shape,
                              scratch_shapes=scratch_shapes, input_output_aliases=aliases, compiler_params=params)(*args)
        return list(outs), []
    n_in, n_out, n_scr = len(in_specs), len(out_specs), len(scratch_shapes)
    r_in, r_out = len(rider.inputs), len(rider.out_shapes)

    def wrapped(*refs):
        ins, r_ins = refs[:n_in], refs[n_in:n_in + r_in]
        pos = n_in + r_in
        outs, r_outs = refs[pos:pos + n_out], refs[pos + n_out:pos + n_out + r_out]
        pos += n_out + r_out
        scr, sems = refs[pos:pos + n_scr], refs[pos + n_scr:]
        ids = [pl.program_id(k) for k in range(len(grid))]
        first, last = ids[0] == 0, ids[0] == grid[0] - 1
        for k in range(1, len(grid)):
            first, last = first & (ids[k] == 0), last & (ids[k] == grid[k] - 1)

        @pl.when(first)
        def _():
            rider.start(r_ins, r_outs, *sems)

        body(*ins, *outs, *scr)

        @pl.when(last)
        def _():
            rider.finish(r_ins, r_outs, *sems)

    outs = pl.pallas_call(
        wrapped, name=name, grid=grid, in_specs=in_specs + [HBM_SPEC] * r_in, out_specs=out_specs + [HBM_SPEC] * r_out,
        out_shape=out_shape + rider.out_shapes, scratch_shapes=scratch_shapes + rider.scratch(),
        input_output_aliases={**aliases, **{n_in + i: n_out + o for i, o in rider.aliases.items()}}, compiler_params=params,
    )(*args, *rider.inputs)
    return list(outs[:n_out]), list(outs[n_out:])


def _stage_weights(shards):
    n = len(shards)

    def body(*refs):
        ins, outs, stages, sems = refs[:n], refs[n:2 * n], refs[2 * n:3 * n], refs[3 * n]
        x, y, _, _ = _place()
        copies = []
        for t in range(n):
            stages[t][...] = ins[t][...].astype(BF16)
            copies.append(pltpu.make_async_copy(stages[t], outs[t].at[2 * x + y], sems.at[t]))
            copies[-1].start()
        for cp in copies:
            cp.wait()

    stage_bytes = sum(s.size * 6 for s in shards)
    return pl.pallas_call(
        body, name="stage_weights", in_specs=[VMEM_SPEC] * n, out_specs=[HBM_SPEC] * n,
        out_shape=[jax.ShapeDtypeStruct((N_CHIPS,) + s.shape, BF16) for s in shards],
        scratch_shapes=[pltpu.VMEM(s.shape, BF16) for s in shards] + [pltpu.SemaphoreType.DMA((n,))],
        compiler_params=pltpu.CompilerParams(vmem_limit_bytes=stage_bytes + 8 * MIB),
    )(*shards)


def _gather(buffers, stage="both"):
    n = len(buffers)
    halves = [b.shape[1] // 2 for b in buffers]

    def half_of(outs, t, chip, which):
        return outs[t].at[chip, pl.ds(which * halves[t], halves[t]), :]

    def copy(outs, sems, t, k, chip, which, to):
        rows = half_of(outs, t, chip, which)
        return pltpu.make_async_remote_copy(src_ref=rows, dst_ref=rows, send_sem=sems[0].at[6 * t + k],
                                            recv_sem=sems[1].at[6 * t + k], device_id=to, device_id_type=MESH)

    def to_chips(outs, sems, what):
        x, y, c, chips = _place()
        for t in range(n):
            for j, (px, py) in enumerate(chips):
                if what == "start":
                    copy(outs, sems, t, j, 2 * x + y, c, (px, py, c)).start()
                else:
                    copy(outs, sems, t, j, 2 * px + py, c, (px, py, c)).wait_recv()
                    copy(outs, sems, t, j, 2 * x + y, c, (px, py, c)).wait_send()

    def to_sibling(outs, sems, what):
        x, y, c, chips = _place()
        for t in range(n):
            for j, (px, py) in enumerate(chips):
                if what == "start":
                    copy(outs, sems, t, 3 + j, 2 * px + py, c, (x, y, 1 - c)).start()
                else:
                    copy(outs, sems, t, 3 + j, 2 * px + py, 1 - c, (x, y, 1 - c)).wait_recv()
                    copy(outs, sems, t, 3 + j, 2 * px + py, c, (x, y, 1 - c)).wait_send()

    def start(ins, outs, *sems):
        (to_sibling if stage == "pair" else to_chips)(outs, sems, "start")

    def finish(ins, outs, *sems):
        if stage != "pair":
            to_chips(outs, sems, "finish")
        if stage == "both":
            to_sibling(outs, sems, "start")
        if stage != "chips":
            to_sibling(outs, sems, "finish")

    return _Exchange(buffers, [jax.ShapeDtypeStruct(b.shape, b.dtype) for b in buffers], 6 * n, start, finish,
                     aliases={t: t for t in range(n)})


def _pair_exchange(grads):
    n = len(grads)
    halves = [g.shape[1] // 2 for g in grads]

    def copies(ins, outs, send_sems, recv_sems):
        x, y, c, _ = _place()
        return [pltpu.make_async_remote_copy(
            src_ref=ins[t].at[:, pl.ds((1 - c) * halves[t], halves[t]), :], dst_ref=outs[t],
            send_sem=send_sems.at[t], recv_sem=recv_sems.at[t], device_id=(x, y, 1 - c), device_id_type=MESH)
            for t in range(n)]

    def start(*refs):
        for cp in copies(*refs):
            cp.start()

    def finish(*refs):
        for cp in copies(*refs):
            cp.wait()

    return _Exchange(grads, [jax.ShapeDtypeStruct((N_CHIPS, h, g.shape[2]), F32) for g, h in zip(grads, halves)], n,
                     start, finish)


def _row_tile(rows):
    return min(rows, 256)


def _pair_add(grad, other, place, name):
    _, rows, cols = grad.shape
    rh = rows // 2
    tr = _row_tile(rh)
    nb = rh // tr

    def body(p_ref, g_ref, a_ref, wire_ref, own_ref):
        s = g_ref[...] + a_ref[...]
        wire_ref[...] = s.astype(BF16)

        @pl.when(pl.program_id(1) == p_ref[1])
        def _():
            own_ref[...] = s

    blk = (None, tr, cols)
    return pl.pallas_call(
        body, name=name,
        grid_spec=pltpu.PrefetchScalarGridSpec(
            num_scalar_prefetch=1, grid=(nb, N_CHIPS),
            in_specs=[pl.BlockSpec(blk, lambda i, j, p: (j, p[0] * nb + i, 0)), pl.BlockSpec(blk, lambda i, j, p: (j, i, 0))],
            out_specs=[pl.BlockSpec(blk, lambda i, j, p: (j, i, 0)), pl.BlockSpec((tr, cols), lambda i, j, p: (i, 0))]),
        out_shape=[jax.ShapeDtypeStruct((N_CHIPS, rh, cols), BF16), jax.ShapeDtypeStruct((rh, cols), F32)],
        compiler_params=_params(("arbitrary", "arbitrary"), 32),
    )(place, grad, other)


def _chip_exchange(wires):
    n = len(wires)

    def copies(ins, outs, send_sems, recv_sems):
        x, y, c, chips = _place()
        return [pltpu.make_async_remote_copy(
            src_ref=ins[t].at[2 * px + py], dst_ref=outs[t].at[j], send_sem=send_sems.at[3 * t + j],
            recv_sem=recv_sems.at[3 * t + j], device_id=(px, py, c), device_id_type=MESH)
            for t in range(n) for j, (px, py) in enumerate(chips)]

    def start(*refs):
        for cp in copies(*refs):
            cp.start()

    def finish(*refs):
        for cp in copies(*refs):
            cp.wait()

    return _Exchange(wires, [jax.ShapeDtypeStruct((3,) + w.shape[1:], BF16) for w in wires], 3 * n, start, finish)


def _chip_add(own, arrived, place, name):
    rh, cols = own.shape
    tr = _row_tile(rh)
    nb = rh // tr

    def body(p_ref, s_ref, b0, b1, b2, o_ref):
        o_ref[...] = ((s_ref[...] + b0[...].astype(F32)) + b1[...].astype(F32)) + b2[...].astype(F32)

    blk = (None, tr, cols)
    return pl.pallas_call(
        body, name=name,
        grid_spec=pltpu.PrefetchScalarGridSpec(
            num_scalar_prefetch=1, grid=(nb,),
            in_specs=[pl.BlockSpec((tr, cols), lambda i, p: (i, 0)), pl.BlockSpec(blk, lambda i, p: (0, i, 0)),
                      pl.BlockSpec(blk, lambda i, p: (1, i, 0)), pl.BlockSpec(blk, lambda i, p: (2, i, 0))],
            out_specs=pl.BlockSpec((tr, cols), lambda i, p: (p[0] * nb + i, 0))),
        out_shape=jax.ShapeDtypeStruct((2 * rh, cols), F32),
        compiler_params=_params(("arbitrary",), 32),
    )(place, own, arrived, arrived, arrived)


def _pair_share(halves):
    n = len(halves)
    rhs = [h.shape[0] // 2 for h in halves]

    def copy(outs, send_sems, recv_sems, t, which):
        x, y, c, _ = _place()
        rows = outs[t].at[pl.ds(which * rhs[t], rhs[t]), :]
        return pltpu.make_async_remote_copy(src_ref=rows, dst_ref=rows, send_sem=send_sems.at[t], recv_sem=recv_sems.at[t],
                                            device_id=(x, y, 1 - c), device_id_type=MESH)

    def start(ins, outs, send_sems, recv_sems):
        c = lax.axis_index("c")
        for t in range(n):
            copy(outs, send_sems, recv_sems, t, c).start()

    def finish(ins, outs, send_sems, recv_sems):
        c = lax.axis_index("c")
        for t in range(n):
            copy(outs, send_sems, recv_sems, t, c).wait_send()
            copy(outs, send_sems, recv_sems, t, 1 - c).wait_recv()

    return _Exchange(halves, [jax.ShapeDtypeStruct(h.shape, F32) for h in halves], n, start, finish,
                     aliases={t: t for t in range(n)})


class _GradReduction:
    def __init__(self, grads, place, tag):
        self.names, self.grads, self.place, self.tag = list(grads), grads, place, tag

    def pair_exchange(self):
        return _pair_exchange([self.grads[n] for n in self.names])

    def chip_exchange(self, others):
        sums = [_pair_add(self.grads[n], o, self.place, f"{self.tag}_pair_add_{n}") for n, o in zip(self.names, others)]
        self.owns = [own for _, own in sums]
        return _chip_exchange([wire for wire, _ in sums])

    def pair_share(self, arrived):
        return _pair_share([_chip_add(own, arr, self.place, f"{self.tag}_chip_add_{n}")
                            for n, own, arr in zip(self.names, self.owns, arrived)])

    def result(self, shared):
        return dict(zip(self.names, shared))


def _all_reduce_small(p):
    rows, lanes = p.shape
    flips = [(fx, fy, fc) for fx in (0, 1) for fy in (0, 1) for fc in (0, 1)][1:]

    def body(p_ref, o_ref, buf, send_sems, recv_sems):
        x, y, c, _ = _place()
        me = 4 * x + 2 * y + c
        buf[me] = p_ref[...]
        peers = [((1 - x) if fx else x, (1 - y) if fy else y, (1 - c) if fc else c) for fx, fy, fc in flips]
        cps = []
        for k, peer in enumerate(peers):
            cp = pltpu.make_async_remote_copy(
                src_ref=p_ref, dst_ref=buf.at[me], send_sem=send_sems.at[k], recv_sem=recv_sems.at[k],
                device_id=peer, device_id_type=MESH)
            cp.start()
            cps.append(cp)
        for k, (px, py, pc) in enumerate(peers):
            pltpu.make_async_remote_copy(
                src_ref=p_ref, dst_ref=buf.at[4 * px + 2 * py + pc], send_sem=send_sems.at[k], recv_sem=recv_sems.at[k],
                device_id=(px, py, pc), device_id_type=MESH).wait_recv()
        for cp in cps:
            cp.wait_send()
        acc = buf[0]
        for s in range(1, 8):
            acc = acc + buf[s]
        o_ref[...] = acc

    return pl.pallas_call(
        body, name="small_all_reduce", in_specs=[VMEM_SPEC], out_specs=VMEM_SPEC,
        out_shape=jax.ShapeDtypeStruct((rows, lanes), F32),
        scratch_shapes=[pltpu.VMEM((8, rows, lanes), F32), pltpu.SemaphoreType.DMA((7,)), pltpu.SemaphoreType.DMA((7,))],
        compiler_params=pltpu.CompilerParams(vmem_limit_bytes=32 * MIB),
    )(p)


BIG = ("w_in", "w_branch_attn", "w_branch_gmlp", "w_out", "w_mlp_in", "w_mlp_out")
COLUMN_SHARDED = ("w_in", "w_branch_attn", "w_branch_gmlp", "w_mlp_in")
SMALL = ("norm_pre_mix", "w_spatial", "b_spatial", "ln_v_gain", "ln_v_bias", "norm_post_mix", "norm_pre_mlp", "norm_post_mlp")
ORDER = ("norm_pre_mix", "w_in", "w_spatial", "b_spatial", "ln_v_gain", "ln_v_bias", "w_branch_attn", "w_branch_gmlp",
         "w_out", "norm_post_mix", "norm_pre_mlp", "w_mlp_in", "w_mlp_out", "norm_post_mlp")


def _full_weight(name, gathered):
    if name in COLUMN_SHARDED:
        return jnp.transpose(gathered, (1, 0, 2)).reshape(gathered.shape[1], -1)
    return gathered.reshape(-1, gathered.shape[2])


def _rows8(a):
    a = a.reshape(-1, 128)
    pad = (-a.shape[0]) % 8
    return jnp.pad(a, ((0, pad), (0, 0))) if pad else a


def _qkv_columns(group):
    return [(sec * ATTN_W + group * GROUP_W, sec * ATTN_W + (group + 1) * GROUP_W) for sec in range(3)]


def _device_step(x, target, small, shards, place):
    seq = x.shape[0]
    g0, g1, g2, g3 = small["norm_pre_mix"], small["norm_post_mix"], small["norm_pre_mlp"], small["norm_post_mlp"]
    w_sp = small["w_spatial"]
    b_col = small["b_spatial"].reshape(GMLP_GROUPS, CHUNK, 1)
    ln_g, ln_b = small["ln_v_gain"], small["ln_v_bias"]

    staged = _stage_weights(shards)
    w_in = _full_weight("w_in", _run_exchange(_gather(staged[:1]), "gather_w_in")[0])
    tables = _rope_tables(seq)
    (*hq, rest), landed = _in_proj(x, g0, w_in, *tables[1], rider=_gather(staged[1:], "chips"))
    h, qkv = hq[:N_GROUPS], hq[N_GROUPS:]

    o_l, gathered = _attn_fwd(qkv[0], DILATIONS[0], rider=_gather(landed, "pair"))
    full = {n: _full_weight(n, gw) for n, gw in zip(BIG[1:], gathered)}
    for g in range(1, N_GROUPS):
        o_l.extend(_attn_fwd(qkv[g], DILATIONS[g])[0])
    *ya_l, yg, mg, y, x1 = _mix_fwd(o_l, rest, x, w_sp, b_col, ln_g, ln_b, full["w_branch_attn"], full["w_branch_gmlp"],
                                    full["w_out"], g1)
    ya, lse = ya_l[0::2], ya_l[1::2]
    h2, a, dy2, dout, loss8, dg3 = _mlp_fwd(x1, g2, g3, full["w_mlp_in"], full["w_mlp_out"], target)
    dap, dx1, dy, dg2, dg1 = _mlp_bwd(dy2, a, full["w_mlp_out"], full["w_mlp_in"], dout, x1, y, g2, g1)
    d_wmo = _tn_matmul(a, dy2, "grad_w_mlp_out", 1024, 1024, square_a=True)
    d_wmi = _tn_matmul(h2, dap, "grad_w_mlp_in", 1024, 1024, column_shards=True)

    mlp = _GradReduction({"w_mlp_in": d_wmi, "w_mlp_out": d_wmo.reshape(N_CHIPS, D_FF // N_CHIPS, D_MODEL)}, place, "mlp")
    (*dya, drest, d_wout, d_wba, d_wbg, d_wsp, d_bb, d_lg, d_lb), mlp_others = _mix_bwd(
        dy, ya[0], yg, mg, rest, full["w_out"], full["w_branch_attn"], full["w_branch_gmlp"], w_sp, b_col, ln_g, ln_b,
        rider=mlp.pair_exchange())
    mix = _GradReduction({"w_branch_attn": d_wba, "w_branch_gmlp": d_wbg,
                          "w_out": d_wout.reshape(N_CHIPS, D_MODEL // N_CHIPS, D_MODEL)}, place, "mix")
    n_mlp = len(mlp.names)
    attn = lambda g, rider: _attn_bwd(qkv[g], dya[g], ya[g], lse[g], *tables[DILATIONS[g]], DILATIONS[g], rider=rider)
    dqkv0, riding = attn(0, _together(mlp.chip_exchange(mlp_others), mix.pair_exchange()))
    dqkv1, riding = attn(1, _together(mlp.pair_share(riding[:n_mlp]), mix.chip_exchange(riding[n_mlp:])))
    reduced = mlp.result(riding[:n_mlp])
    dqkv2, riding = attn(2, mix.pair_share(riding[n_mlp:]))
    reduced.update(mix.result(riding))
    dqkv = [dqkv0, dqkv1, dqkv2]

    d_qkv = [_tn_matmul_residue(h[g], dqkv[g], dil, f"grad_w_in_qkv{g}") for g, dil in enumerate(DILATIONS)]
    d_rest = _tn_matmul(h[0], drest, "grad_w_in_rest", 1024, 1024)
    d_win = jnp.concatenate([d_qkv[g][:, s * GROUP_W:(s + 1) * GROUP_W] for s in range(3) for g in range(N_GROUPS)]
                            + [d_rest], axis=1)
    shard_w = IN_W // N_CHIPS
    d_win = jnp.stack([d_win[:, j * shard_w:(j + 1) * shard_w] for j in range(N_CHIPS)], axis=0)
    first = _GradReduction({"w_in": d_win}, place, "w_in")
    w_qkv = [jnp.concatenate([w_in[:, lo:hi] for lo, hi in _qkv_columns(g)], axis=1) for g in range(N_GROUPS)]
    w_rest = w_in[:, QKV_W:]
    tiles = seq // IN_PROJ_BWD_TM
    so_far = (lax.empty((seq, D_MODEL), F32), jnp.zeros((1, D_MODEL), F32))
    in_bwd = lambda so_far, span, rider: _in_proj_bwd(dqkv, drest, w_qkv, w_rest, x, dx1, g0, so_far, span, rider=rider)
    so_far, riding = in_bwd(so_far, (0, tiles // 4), first.pair_exchange())
    so_far, riding = in_bwd(so_far, (tiles // 4, tiles // 2), first.chip_exchange(riding))
    shared = _run_exchange(first.pair_share(riding), "w_in_pair_share")
    (grad_x, dg0), _ = in_bwd(so_far, (3 * tiles // 4, tiles // 4), None)
    reduced.update(first.result(shared))
    little = {"norm_pre_mix": dg0, "w_spatial": d_wsp, "b_spatial": d_bb[:, :, 0], "ln_v_gain": d_lg, "ln_v_bias": d_lb,
              "norm_post_mix": dg1, "norm_pre_mlp": dg2, "norm_post_mlp": dg3}
    return loss8[0, 0], grad_x, reduced, little


def kernel(x, norm_pre_mix, w_in, w_spatial, b_spatial, ln_v_gain, ln_v_bias, w_branch_attn, w_branch_gmlp, w_out, norm_post_mix, norm_pre_mlp, w_mlp_in, w_mlp_out, norm_post_mlp, loss_target, m_norm_pre_mix, m_w_in, m_w_spatial, m_b_spatial, m_ln_v_gain, m_ln_v_bias, m_w_branch_attn, m_w_branch_gmlp, m_w_out, m_norm_post_mix, m_norm_pre_mlp, m_w_mlp_in, m_w_mlp_out, m_norm_post_mlp, v_norm_pre_mix, v_w_in, v_w_spatial, v_b_spatial, v_ln_v_gain, v_ln_v_bias, v_w_branch_attn, v_w_branch_gmlp, v_w_out, v_norm_post_mix, v_norm_pre_mlp, v_w_mlp_in, v_w_mlp_out, v_norm_post_mlp):
    given = dict(norm_pre_mix=norm_pre_mix, w_in=w_in, w_spatial=w_spatial, b_spatial=b_spatial, ln_v_gain=ln_v_gain,
                 ln_v_bias=ln_v_bias, w_branch_attn=w_branch_attn, w_branch_gmlp=w_branch_gmlp, w_out=w_out,
                 norm_post_mix=norm_post_mix, norm_pre_mlp=norm_pre_mlp, w_mlp_in=w_mlp_in, w_mlp_out=w_mlp_out,
                 norm_post_mlp=norm_post_mlp)
    moments_m = dict(norm_pre_mix=m_norm_pre_mix, w_in=m_w_in, w_spatial=m_w_spatial, b_spatial=m_b_spatial,
                     ln_v_gain=m_ln_v_gain, ln_v_bias=m_ln_v_bias, w_branch_attn=m_w_branch_attn,
                     w_branch_gmlp=m_w_branch_gmlp, w_out=m_w_out, norm_post_mix=m_norm_post_mix,
                     norm_pre_mlp=m_norm_pre_mlp, w_mlp_in=m_w_mlp_in, w_mlp_out=m_w_mlp_out, norm_post_mlp=m_norm_post_mlp)
    moments_v = dict(norm_pre_mix=v_norm_pre_mix, w_in=v_w_in, w_spatial=v_w_spatial, b_spatial=v_b_spatial,
                     ln_v_gain=v_ln_v_gain, ln_v_bias=v_ln_v_bias, w_branch_attn=v_w_branch_attn,
                     w_branch_gmlp=v_w_branch_gmlp, w_out=v_w_out, norm_post_mix=v_norm_post_mix,
                     norm_pre_mlp=v_norm_pre_mlp, w_mlp_in=v_w_mlp_in, w_mlp_out=v_w_mlp_out, norm_post_mlp=v_norm_post_mlp)
    cx, cy, cc = lax.axis_index("x"), lax.axis_index("y"), lax.axis_index("c")

    shards = [given[n][0] for n in BIG]
    small = {n: given[n][0] if given[n].ndim > 2 else given[n] for n in SMALL}
    place = jnp.stack([cc, 2 * cx + cy]).astype(jnp.int32)
    loss, grad_x, grad_shard, grads = _device_step(x[0], loss_target[0], small, shards, place)
    loss = lax.psum(loss, ("x", "y", "c"))

    packed = jnp.concatenate([_rows8(grads[n]) for n in SMALL], axis=0)
    summed = _all_reduce_small(packed)
    row = 0
    for n in SMALL:
        shape = given[n][0].shape
        cnt = -(-(given[n][0].size // 128) // 8) * 8
        grad_shard[n] = summed[row:row + given[n][0].size // 128].reshape(shape)
        row += cnt

    grad_out, deltas, new_m, new_v = {}, {}, {}, {}
    for n in ORDER:
        shape = given[n].shape
        two_d = (-1, shape[-1])
        outs = _adamw(given[n].reshape(two_d), grad_shard[n].reshape(two_d), moments_m[n].reshape(two_d),
                      moments_v[n].reshape(two_d), "adamw_" + n)
        grad_out[n], deltas[n], new_m[n], new_v[n] = [o.reshape(shape) for o in outs]
    return (loss, grad_x[None], *[grad_out[n] for n in ORDER], *[deltas[n] for n in ORDER], *[new_m[n] for n in ORDER],
            *[new_v[n] for n in ORDER])
```

```python
import math

import jax
import jax.numpy as jnp
from jax import lax
from jax.experimental import pallas as pl
from jax.experimental.pallas import tpu as pltpu

F32 = jnp.float32
BF16 = jnp.bfloat16
MESH = pl.DeviceIdType.MESH

D_MODEL = 1024
HEAD_DIM = 64
HEADS_PER_GROUP = 4
GROUP_W = HEADS_PER_GROUP * HEAD_DIM
DILATIONS = (1, 4, 16)
N_GROUPS = len(DILATIONS)
ATTN_W = N_GROUPS * GROUP_W
QKV_W = 3 * ATTN_W
GMLP_W = 512
GMLP_GROUPS = 4
CHUNK = 128
REST_W = 2 * GMLP_W + 2 * D_MODEL
IN_W = QKV_W + REST_W
D_FF = 4096
QBLK = 128
ROPE_THETA = 10000.0
EPS = 1e-6
NEG = -1e30
SCALE = HEAD_DIM ** -0.5
N_CHIPS = 4

ADAM_LR = 0.001
ADAM_B1 = 0.9
ADAM_B2 = 0.999
ADAM_EPS = 1e-08
ADAM_WD = 0.01
ADAM_STEP = 10

MIB = 1024 * 1024
HBM_SPEC = pl.BlockSpec(memory_space=pltpu.HBM)
VMEM_SPEC = pl.BlockSpec(memory_space=pltpu.VMEM)


MLP_TM = 256


def _params(semantics, vmem_mib):
    return pltpu.CompilerParams(dimension_semantics=semantics, vmem_limit_bytes=vmem_mib * MIB)


def _in_hbm(a):
    return pltpu.with_memory_space_constraint(a, pltpu.HBM) if a.size * a.dtype.itemsize >= MIB else a


def _pallas(body, **kwargs):
    return pl.pallas_call(body, **kwargs)


def _resident(shape):
    return pl.BlockSpec(shape, lambda *_: (0,) * len(shape), pipeline_mode=pl.Buffered(1))


def _dot(a, b):
    return jnp.dot(a, b, preferred_element_type=F32)


def _dot_nt(a, b):
    return lax.dot_general(a, b, (((1,), (1,)), ((), ())), preferred_element_type=F32)


def _dot_tn(a, b):
    return lax.dot_general(a, b, (((0,), (0,)), ((), ())), preferred_element_type=F32)


_GELU_C = math.sqrt(2.0 / math.pi)


def _gelu(x):
    return x * (0.5 * (1.0 + jnp.tanh(_GELU_C * (x + 0.044715 * (x * x * x)))))


def _gelu_grad(x):
    t = jnp.tanh(_GELU_C * (x + 0.044715 * (x * x * x)))
    return 0.5 * (1.0 + t) + 0.5 * x * (1.0 - t * t) * (_GELU_C * (1.0 + 3.0 * 0.044715 * (x * x)))


def _rsqrt_ms(v):
    return lax.rsqrt(jnp.mean(v * v, axis=-1, keepdims=True) + EPS)


def _rmsnorm_bwd(dn, src, gain):
    r = _rsqrt_ms(src)
    t = gain * dn
    dgain = jnp.sum(dn * (src * r), axis=0, keepdims=True)
    dsrc = r * t - src * ((r * r * r) * jnp.mean(t * src, axis=-1, keepdims=True))
    return dsrc, dgain


def _rot_half(v):
    w = v.shape[-1]
    lane = lax.broadcasted_iota(jnp.int32, v.shape, v.ndim - 1)
    return jnp.where((lane % HEAD_DIM) < HEAD_DIM // 2, pltpu.roll(v, w - HEAD_DIM // 2, v.ndim - 1),
                     pltpu.roll(v, HEAD_DIM // 2, v.ndim - 1))


def _head_masks(shape):
    lane = lax.broadcasted_iota(jnp.int32, shape, 1)
    return [(lane >= h * HEAD_DIM) & (lane < (h + 1) * HEAD_DIM) for h in range(HEADS_PER_GROUP)]


def _head_stack(block, hmask):
    zero = jnp.zeros((), block.dtype)
    return jnp.concatenate([jnp.where(hm, block, zero) for hm in hmask], axis=0)


LANES = 128


def _put_residue(slab, val, out_ref, dil, width, col0):
    tm, w = val.shape
    if dil == 1:
        out_ref[:, col0:col0 + w] = val.astype(out_ref.dtype)
        return
    for k in range(w // LANES):
        slab[k] = val[:, k * LANES:(k + 1) * LANES]
    for r in range(dil):
        for k in range(w // LANES):
            c = r * width + col0 + k * LANES
            out_ref[:, c:c + LANES] = slab[k, pl.ds(r, tm // dil, stride=dil), :].astype(out_ref.dtype)


def _get_tokens(slab, in_ref, dil, width, col0, w):
    if dil == 1:
        return in_ref[:, col0:col0 + w].astype(F32)
    rows = in_ref.shape[0]
    for r in range(dil):
        for k in range(w // LANES):
            c = r * width + col0 + k * LANES
            slab[k, pl.ds(r, rows, stride=dil), :] = in_ref[:, c:c + LANES].astype(F32)
    return jnp.concatenate([slab[k] for k in range(w // LANES)], axis=1)


def _rope_tables(seq, rider=None):
    half = HEAD_DIM // 2
    inv_freq = ROPE_THETA ** (-jnp.arange(half, dtype=F32) / half)
    freq = jnp.tile(inv_freq, LANES // half).reshape(1, LANES)
    tm = 512

    def body(f_ref, *refs):
        outs, slab_c, slab_s = refs[:-2], refs[-2], refs[-1]
        row = lax.broadcasted_iota(jnp.int32, (tm, LANES), 0) + pl.program_id(0) * tm
        lane = lax.broadcasted_iota(jnp.int32, (tm, LANES), 1)
        ang = row.astype(F32) * f_ref[...]
        cos = jnp.cos(ang)
        sin = jnp.where((lane % HEAD_DIM) < half, -jnp.sin(ang), jnp.sin(ang))
        slab_c[0] = cos
        slab_s[0] = sin
        for i, dil in enumerate(DILATIONS):
            for tab, slab in ((outs[2 * i], slab_c), (outs[2 * i + 1], slab_s)):
                for r in range(dil):
                    piece = slab[0, pl.ds(r, tm // dil, stride=dil), :] if dil > 1 else slab[0]
                    for k in range(GROUP_W // LANES):
                        tab[:, r * GROUP_W + k * LANES:r * GROUP_W + (k + 1) * LANES] = piece

    outs, riding = _call(
        body, name="rope_tables", grid=(seq // tm,),
        in_specs=[pl.BlockSpec((1, LANES), lambda i: (0, 0))],
        out_specs=[pl.BlockSpec((tm // d, d * GROUP_W), lambda i: (i, 0)) for d in DILATIONS for _ in range(2)],
        out_shape=[jax.ShapeDtypeStruct((seq // d, d * GROUP_W), F32) for d in DILATIONS for _ in range(2)],
        scratch_shapes=[pltpu.VMEM((1, tm, LANES), F32)] * 2,
        params=_params(("arbitrary",), 32), args=(freq,), rider=rider)
    return {d: (outs[2 * i], outs[2 * i + 1]) for i, d in enumerate(DILATIONS)}, riding


def _in_proj(x, g0, w_in, cos_t, sin_t, rider=None):
    seq = x.shape[0]
    tm, tn = 256, GROUP_W
    n_qk = 2 * ATTN_W // tn
    n_qkv = QKV_W // tn

    def body(x_ref, g_ref, w_ref, cos_ref, sin_ref, *refs):
        h_refs, qkv_refs, rest_ref, slab = refs[:N_GROUPS], refs[N_GROUPS:2 * N_GROUPS], refs[2 * N_GROUPS], refs[-1]
        xv = x_ref[...]
        hf = (xv * _rsqrt_ms(xv)) * g_ref[...]
        hb = hf.astype(BF16)
        for g, dil in enumerate(DILATIONS):
            _put_residue(slab, hf, h_refs[g], dil, D_MODEL, 0)
        cos, sin = cos_ref[...], sin_ref[...]
        for j in range(IN_W // tn):
            p = _dot_nt(hb, w_ref[j * tn:(j + 1) * tn, :])
            if j < n_qkv:
                if j < n_qk:
                    p = p * cos + _rot_half(p) * sin
                section, g = divmod(j, N_GROUPS)
                _put_residue(slab, p, qkv_refs[g], DILATIONS[g], 3 * GROUP_W, section * GROUP_W)
            else:
                rest_ref[:, (j - n_qkv) * tn:(j - n_qkv + 1) * tn] = p.astype(BF16)

    return _call(
        body, name="in_proj", grid=(seq // tm,),
        in_specs=[pl.BlockSpec((tm, D_MODEL), lambda i: (i, 0)),
                  pl.BlockSpec((1, D_MODEL), lambda i: (0, 0)),
                  _resident((IN_W, D_MODEL)),
                  pl.BlockSpec((tm, GROUP_W), lambda i: (i, 0)),
                  pl.BlockSpec((tm, GROUP_W), lambda i: (i, 0))],
        out_specs=[pl.BlockSpec((tm // d, d * D_MODEL), lambda i: (i, 0)) for d in DILATIONS]
        + [pl.BlockSpec((tm // d, d * 3 * GROUP_W), lambda i: (i, 0)) for d in DILATIONS]
        + [pl.BlockSpec((tm, REST_W), lambda i: (i, 0))],
        out_shape=[jax.ShapeDtypeStruct((seq // d, d * D_MODEL), BF16) for d in DILATIONS]
        + [jax.ShapeDtypeStruct((seq // d, d * 3 * GROUP_W), BF16) for d in DILATIONS]
        + [jax.ShapeDtypeStruct((seq, REST_W), BF16)],
        scratch_shapes=[pltpu.VMEM((D_MODEL // LANES, tm, LANES), F32)],
        params=_params(("arbitrary",), 56), args=(x, g0, w_in, cos_t, sin_t), rider=rider)


def _band_masks():
    qi = lax.broadcasted_iota(jnp.int32, (QBLK, QBLK), 0)
    kj = lax.broadcasted_iota(jnp.int32, (QBLK, QBLK), 1)
    return kj <= qi, kj >= qi


def _attn_tile(length):
    return min(512, length)


def _attn_fwd(qkv, dil, rider=None):
    length = qkv.shape[0]
    tq = _attn_tile(length)
    nsub = tq // QBLK
    nblk = length // tq

    def body(q_ref, k_ref, v_ref, kp_ref, vp_ref, o_ref, l_ref):
        n = pl.program_id(1)
        mask_c, mask_p0 = _band_masks()
        hmask = _head_masks((QBLK, GROUP_W))
        zero = jnp.zeros((), BF16)
        for b in range(nsub):
            rows = slice(b * QBLK, (b + 1) * QBLK)
            q = q_ref[rows, :]
            kc, vc = k_ref[rows, :], v_ref[rows, :]
            if b == 0:
                kp, vp = kp_ref[...], vp_ref[...]
                mask_p = mask_p0 & (n > 0)
            else:
                prow = slice((b - 1) * QBLK, b * QBLK)
                kp, vp = k_ref[prow, :], v_ref[prow, :]
                mask_p = mask_p0
            o_acc = jnp.zeros((QBLK, GROUP_W), F32)
            l_acc = jnp.zeros((QBLK, GROUP_W), F32)
            for h in range(HEADS_PER_GROUP):
                hm = hmask[h]
                sc = jnp.where(mask_c, _dot_nt(q, jnp.where(hm, kc, zero)) * SCALE, NEG)
                sp = jnp.where(mask_p, _dot_nt(q, jnp.where(hm, kp, zero)) * SCALE, NEG)
                m = jnp.maximum(jnp.max(sc, axis=-1, keepdims=True), jnp.max(sp, axis=-1, keepdims=True))
                pc, pp = jnp.exp(sc - m), jnp.exp(sp - m)
                den = jnp.sum(pc, axis=-1, keepdims=True) + jnp.sum(pp, axis=-1, keepdims=True)
                pv = _dot(pc.astype(BF16), jnp.where(hm, vc, zero)) + _dot(pp.astype(BF16), jnp.where(hm, vp, zero))
                o_acc = o_acc + pv / den
                l_acc = l_acc + jnp.where(hm, m + jnp.log(den), 0.0)
            o_ref[rows, :] = o_acc
            l_ref[rows, :] = l_acc

    cur = lambda sec: pl.BlockSpec((tq, GROUP_W), lambda r, n: (n, r * 3 + sec))
    prev = lambda sec: pl.BlockSpec((QBLK, GROUP_W), lambda r, n: (jnp.maximum(n * nsub - 1, 0), r * 3 + sec))
    return _call(
        body, name=f"attn_fwd_d{dil}", grid=(dil, nblk),
        in_specs=[cur(0), cur(1), cur(2), prev(1), prev(2)],
        out_specs=[pl.BlockSpec((tq, GROUP_W), lambda r, n: (n, r))] * 2,
        out_shape=[jax.ShapeDtypeStruct((length, dil * GROUP_W), F32)] * 2, scratch_shapes=[],
        params=_params(("arbitrary", "arbitrary"), 32), args=(qkv, qkv, qkv, qkv, qkv), rider=rider)


def _attn_bwd(qkv, dy, y, lse, cos_t, sin_t, dil, rider=None):
    length = qkv.shape[0]
    tq = _attn_tile(length)
    nsub = tq // QBLK
    nblk = length // tq

    def body(q_ref, k_ref, v_ref, kp_ref, vp_ref, qn_ref, dy_ref, y_ref, l_ref, dyn_ref, yn_ref, ln_ref,
             cos_ref, sin_ref, out_ref, dq_s, dk_s, dv_s):
        n = pl.program_id(1)
        mask_c, mask_p0 = _band_masks()
        hmask = _head_masks((QBLK, GROUP_W))
        sub = lambda ref, b: ref[b * QBLK:(b + 1) * QBLK, :]
        kbd = [_head_stack(kp_ref[...], hmask)] + [_head_stack(sub(k_ref, b), hmask) for b in range(nsub)]
        vbd = [_head_stack(vp_ref[...], hmask)] + [_head_stack(sub(v_ref, b), hmask) for b in range(nsub)]
        dk_s[...] = jnp.zeros(dk_s.shape, F32)
        dv_s[...] = jnp.zeros(dv_s.shape, F32)

        def query_block(q, dyv, yv, lv, key_blocks):
            dyb = dyv.astype(BF16)
            qbd = _head_stack(q, hmask)
            dybd = jnp.concatenate([jnp.where(hm, dyv, 0.0).astype(BF16) for hm in hmask], axis=0)
            prod = dyv * yv
            deltas = [jnp.sum(jnp.where(hm, prod, 0.0), axis=-1, keepdims=True) for hm in hmask]
            lses = [jnp.max(jnp.where(hm, lv, NEG), axis=-1, keepdims=True) for hm in hmask]
            dq = jnp.zeros((QBLK, GROUP_W), F32)
            for kb, mask in key_blocks:
                s = _dot_nt(q, kbd[kb]) * SCALE
                dp = _dot_nt(dyb, vbd[kb])
                ps, dss = [], []
                for h in range(HEADS_PER_GROUP):
                    cols = slice(h * QBLK, (h + 1) * QBLK)
                    p = jnp.exp(jnp.where(mask, s[:, cols] - lses[h], NEG))
                    ps.append(p.astype(BF16))
                    dss.append((p * (dp[:, cols] - deltas[h])).astype(BF16))
                dq = dq + _dot(jnp.concatenate(dss, axis=1), kbd[kb])
                if kb >= 1:
                    krows = slice((kb - 1) * QBLK, kb * QBLK)
                    dv_s[krows, :] += _dot_tn(jnp.concatenate(ps, axis=0), dybd)
                    dk_s[krows, :] += _dot_tn(jnp.concatenate(dss, axis=0), qbd) * SCALE
            return dq * SCALE

        for b in range(nsub):
            mask_p = mask_p0 & (n > 0) if b == 0 else mask_p0
            dq_s[b * QBLK:(b + 1) * QBLK, :] = query_block(sub(q_ref, b), sub(dy_ref, b), sub(y_ref, b), sub(l_ref, b),
                                                            [(b, mask_p), (b + 1, mask_c)])
        query_block(qn_ref[...], dyn_ref[...], yn_ref[...], ln_ref[...], [(nsub, mask_p0 & (n < nblk - 1))])
        cos, sin = cos_ref[...], sin_ref[...]
        dq, dk = dq_s[...], dk_s[...]
        out_ref[:, 0:GROUP_W] = (dq * cos - _rot_half(dq) * sin).astype(BF16)
        out_ref[:, GROUP_W:2 * GROUP_W] = (dk * cos - _rot_half(dk) * sin).astype(BF16)
        out_ref[:, 2 * GROUP_W:3 * GROUP_W] = dv_s[...].astype(BF16)

    cur = lambda sec: pl.BlockSpec((tq, GROUP_W), lambda r, n: (n, r * 3 + sec))
    prev = lambda sec: pl.BlockSpec((QBLK, GROUP_W), lambda r, n: (jnp.maximum(n * nsub - 1, 0), r * 3 + sec))
    nxt_q = pl.BlockSpec((QBLK, GROUP_W), lambda r, n: (jnp.minimum((n + 1) * nsub, nblk * nsub - 1), r * 3))
    tok = pl.BlockSpec((tq, GROUP_W), lambda r, n: (n, r))
    tok_next = pl.BlockSpec((QBLK, GROUP_W), lambda r, n: (jnp.minimum((n + 1) * nsub, nblk * nsub - 1), r))
    (out,), riding = _call(
        body, name=f"attn_bwd_d{dil}", grid=(dil, nblk),
        in_specs=[cur(0), cur(1), cur(2), prev(1), prev(2), nxt_q,
                  tok, tok, tok, tok_next, tok_next, tok_next, tok, tok],
        out_specs=[pl.BlockSpec((tq, 3 * GROUP_W), lambda r, n: (n, r))],
        out_shape=[jax.ShapeDtypeStruct((length, dil * 3 * GROUP_W), BF16)],
        scratch_shapes=[pltpu.VMEM((tq, GROUP_W), F32)] * 3,
        params=_params(("arbitrary", "arbitrary"), 32),
        args=(qkv, qkv, qkv, qkv, qkv, qkv, dy, y, lse, dy, y, lse, cos_t, sin_t), rider=rider)
    return out, riding


def _layernorm_stats(z):
    mu = jnp.mean(z, axis=-1, keepdims=True)
    zc = z - mu
    rstd = lax.rsqrt(jnp.mean(zc * zc, axis=-1, keepdims=True) + EPS)
    return zc * rstd, rstd


def _tril_mask():
    row = lax.broadcasted_iota(jnp.int32, (CHUNK, CHUNK), 0)
    col = lax.broadcasted_iota(jnp.int32, (CHUNK, CHUNK), 1)
    return col <= row


def _mix_fwd(o_l, rest, x, w_sp, b_col, ln_g, ln_b, w_ba, w_bg, w_out, g1):
    seq = x.shape[0]
    tm = 256

    def body(o0, l0, o1, l1, o2, l2, up_ref, zp_ref, gap_ref, gbp_ref, x_ref, wsp_ref, bcol_ref, lg_ref, lb_ref,
             wba_ref, wbg_ref, wout_ref, g1_ref, ya0, lj0, ya1, lj1, ya2, lj2, yg_ref, mg_ref, y_ref, x1_ref, slab):
        outs = [_get_tokens(slab, o, d, GROUP_W, 0, GROUP_W) for o, d in zip((o0, o1, o2), DILATIONS)]
        lses = [_get_tokens(slab, l, d, GROUP_W, 0, GROUP_W) for l, d in zip((l0, l1, l2), DILATIONS)]
        m = jnp.maximum(jnp.maximum(lses[0], lses[1]), lses[2])
        es = [jnp.exp(l - m) for l in lses]
        tot = es[0] + es[1] + es[2]
        ya = (es[0] * outs[0] + es[1] * outs[1] + es[2] * outs[2]) / tot
        lj = m + jnp.log(tot)
        for ya_ref, lj_ref, d in zip((ya0, ya1, ya2), (lj0, lj1, lj2), DILATIONS):
            _put_residue(slab, ya, ya_ref, d, GROUP_W, 0)
            _put_residue(slab, lj, lj_ref, d, GROUP_W, 0)
        zhat, _ = _layernorm_stats(_gelu(zp_ref[...].astype(F32)))
        zln = (zhat * lg_ref[...] + lb_ref[...]).astype(BF16)
        u = _gelu(up_ref[...].astype(F32))
        tril = _tril_mask()
        for g in range(GMLP_GROUPS):
            wm = jnp.where(tril, wsp_ref[g], 0.0).astype(BF16)
            cols = slice(g * CHUNK, (g + 1) * CHUNK)
            for c in range(tm // CHUNK):
                rows = slice(c * CHUNK, (c + 1) * CHUNK)
                sz = _dot(wm, zln[rows, cols]) + bcol_ref[g]
                yg_ref[rows, cols] = (u[rows, cols] * sz).astype(BF16)
        a = _dot(ya.astype(BF16), wba_ref[...])
        bm = _dot(yg_ref[...], wbg_ref[...])
        merged = (jax.nn.sigmoid(gap_ref[...].astype(F32)) * a + jax.nn.sigmoid(gbp_ref[...].astype(F32)) * bm).astype(BF16)
        mg_ref[...] = merged
        yv = _dot(merged, wout_ref[...])
        y_ref[...] = yv
        x1_ref[...] = x_ref[...] + (yv * _rsqrt_ms(yv)) * g1_ref[...]

    tok = lambda w: pl.BlockSpec((tm, w), lambda i: (i, 0))
    res = lambda d: pl.BlockSpec((tm // d, d * GROUP_W), lambda i: (i, 0))
    full = lambda *s: pl.BlockSpec(s, lambda i: (0,) * len(s))
    res_specs = [res(d) for d in DILATIONS for _ in range(2)]
    return _pallas(
        body, name="mix_fwd", grid=(seq // tm,),
        in_specs=res_specs + [
            pl.BlockSpec((tm, GMLP_W), lambda i: (i, 0)), pl.BlockSpec((tm, GMLP_W), lambda i: (i, 1)),
            pl.BlockSpec((tm, D_MODEL), lambda i: (i, 1)), pl.BlockSpec((tm, D_MODEL), lambda i: (i, 2)),
            tok(D_MODEL), full(GMLP_GROUPS, CHUNK, CHUNK), full(GMLP_GROUPS, CHUNK, 1), full(1, GMLP_W), full(1, GMLP_W),
            full(GROUP_W, D_MODEL), full(GMLP_W, D_MODEL), full(D_MODEL, D_MODEL), full(1, D_MODEL)],
        out_specs=res_specs + [tok(GMLP_W), tok(D_MODEL), tok(D_MODEL), tok(D_MODEL)],
        out_shape=[jax.ShapeDtypeStruct((seq // d, d * GROUP_W), F32) for d in DILATIONS for _ in range(2)]
        + [jax.ShapeDtypeStruct((seq, GMLP_W), BF16), jax.ShapeDtypeStruct((seq, D_MODEL), BF16),
           jax.ShapeDtypeStruct((seq, D_MODEL), F32), jax.ShapeDtypeStruct((seq, D_MODEL), F32)],
        scratch_shapes=[pltpu.VMEM((GROUP_W // LANES, tm, LANES), F32)],
        compiler_params=_params(("arbitrary",), 48),
    )(*map(_in_hbm, (*o_l, rest, rest, rest, rest, x, w_sp, b_col, ln_g, ln_b, w_ba, w_bg, w_out, g1)))


def _mlp_fwd(x1, g2, g3, w_mi, w_mo, target):
    seq = x1.shape[0]
    tm, tf = MLP_TM, 512

    def body(x1_ref, g2_ref, g3_ref, wmi_ref, wmo_ref, t_ref, h2_ref, a_ref, dy2_ref, dout_ref, loss_ref, dg3_ref, sq_s):
        @pl.when(pl.program_id(0) == 0)
        def _():
            loss_ref[...] = jnp.zeros(loss_ref.shape, F32)
            dg3_ref[...] = jnp.zeros(dg3_ref.shape, F32)

        xv = x1_ref[...]
        hb = ((xv * _rsqrt_ms(xv)) * g2_ref[...]).astype(BF16)
        h2_ref[...] = hb
        for j in range(D_FF // tf):
            cols = slice(j * tf, (j + 1) * tf)
            a = jnp.maximum(_dot(hb, wmi_ref[:, cols]), 0.0)
            a_ref[:, cols] = a.astype(BF16)
            sq_s[:, cols] = (a * a).astype(BF16)
        y2 = _dot(sq_s[...], wmo_ref[...])
        r3 = _rsqrt_ms(y2)
        out = xv + (y2 * r3) * g3_ref[...]
        diff = out - t_ref[...]
        tile_loss = 0.5 * jnp.sum(jnp.mean(diff * diff, axis=-1, keepdims=True), axis=0, keepdims=True)
        loss_ref[...] += jnp.broadcast_to(tile_loss, loss_ref.shape)
        dout = diff * (1.0 / D_MODEL)
        dout_ref[...] = dout
        dy2, dg3 = _rmsnorm_bwd(dout, y2, g3_ref[...])
        dy2_ref[...] = dy2.astype(BF16)
        dg3_ref[...] += dg3

    tok = lambda w: pl.BlockSpec((tm, w), lambda i: (i, 0))
    vec = pl.BlockSpec((1, D_MODEL), lambda i: (0, 0))
    return _pallas(
        body, name="mlp_fwd", grid=(seq // tm,),
        in_specs=[tok(D_MODEL), vec, vec, _resident((D_MODEL, D_FF)), _resident((D_FF, D_MODEL)), tok(D_MODEL)],
        out_specs=[tok(D_MODEL), tok(D_FF), tok(D_MODEL), tok(D_MODEL), pl.BlockSpec((8, 128), lambda i: (0, 0)), vec],
        out_shape=[jax.ShapeDtypeStruct((seq, D_MODEL), BF16), jax.ShapeDtypeStruct((seq, D_FF), BF16),
                   jax.ShapeDtypeStruct((seq, D_MODEL), BF16), jax.ShapeDtypeStruct((seq, D_MODEL), F32),
                   jax.ShapeDtypeStruct((8, 128), F32), jax.ShapeDtypeStruct((1, D_MODEL), F32)],
        scratch_shapes=[pltpu.VMEM((tm, D_FF), BF16)],
        compiler_params=_params(("arbitrary",), 56),
    )(*map(_in_hbm, (x1, g2, g3, w_mi, w_mo, target)))


def _mlp_bwd(dy2, a, w_mo, w_mi, dout, x1, y, g2, g1):
    seq = x1.shape[0]
    tm, tf = MLP_TM, 512

    def body(dy2_ref, a_ref, wmo_ref, wmi_ref, dout_ref, x1_ref, y_ref, g2_ref, g1_ref,
             dap_ref, dx1_ref, dy_ref, dg2_ref, dg1_ref):
        @pl.when(pl.program_id(0) == 0)
        def _():
            dg2_ref[...] = jnp.zeros(dg2_ref.shape, F32)
            dg1_ref[...] = jnp.zeros(dg1_ref.shape, F32)

        dy2v = dy2_ref[...]
        for j in range(D_FF // tf):
            cols = slice(j * tf, (j + 1) * tf)
            da2 = _dot_nt(dy2v, wmo_ref[cols, :])
            dap_ref[:, cols] = (da2 * (2.0 * a_ref[:, cols].astype(F32))).astype(BF16)
        dh2 = _dot_nt(dap_ref[...], wmi_ref[...])
        dres, dg2 = _rmsnorm_bwd(dh2, x1_ref[...], g2_ref[...])
        dx1 = dout_ref[...] + dres
        dx1_ref[...] = dx1
        dg2_ref[...] += dg2
        dyv, dg1 = _rmsnorm_bwd(dx1, y_ref[...], g1_ref[...])
        dy_ref[...] = dyv.astype(BF16)
        dg1_ref[...] += dg1

    tok = lambda w: pl.BlockSpec((tm, w), lambda i: (i, 0))
    vec = pl.BlockSpec((1, D_MODEL), lambda i: (0, 0))
    return _pallas(
        body, name="mlp_bwd", grid=(seq // tm,),
        in_specs=[tok(D_MODEL), tok(D_FF), _resident((D_FF, D_MODEL)), _resident((D_MODEL, D_FF)),
                  tok(D_MODEL), tok(D_MODEL), tok(D_MODEL), vec, vec],
        out_specs=[tok(D_FF), tok(D_MODEL), tok(D_MODEL), vec, vec],
        out_shape=[jax.ShapeDtypeStruct((seq, D_FF), BF16), jax.ShapeDtypeStruct((seq, D_MODEL), F32),
                   jax.ShapeDtypeStruct((seq, D_MODEL), BF16), jax.ShapeDtypeStruct((1, D_MODEL), F32),
                   jax.ShapeDtypeStruct((1, D_MODEL), F32)],
        compiler_params=_params(("arbitrary",), 56),
    )(*map(_in_hbm, (dy2, a, w_mo, w_mi, dout, x1, y, g2, g1)))


def _tn_matmul(a, b, name, bm, bn, square_a=False, column_shards=False):
    seq, m = a.shape
    n = b.shape[1]
    ts = 512

    def body(a_ref, b_ref, o_ref):
        @pl.when(pl.program_id(2) == 0)
        def _():
            o_ref[...] = jnp.zeros(o_ref.shape, F32)

        av = a_ref[...]
        if square_a:
            af = av.astype(F32)
            av = (af * af).astype(BF16)
        o_ref[...] += _dot_tn(av, b_ref[...])

    if column_shards:
        out_spec = pl.BlockSpec((None, bm, bn), lambda mi, ni, s: (ni, mi, 0))
        out_shape = jax.ShapeDtypeStruct((n // bn, m, bn), F32)
    else:
        out_spec = pl.BlockSpec((bm, bn), lambda mi, ni, s: (mi, ni))
        out_shape = jax.ShapeDtypeStruct((m, n), F32)
    return _pallas(
        body, name=name, grid=(m // bm, n // bn, seq // ts),
        in_specs=[pl.BlockSpec((ts, bm), lambda mi, ni, s: (s, mi)), pl.BlockSpec((ts, bn), lambda mi, ni, s: (s, ni))],
        out_specs=out_spec, out_shape=out_shape,
        compiler_params=_params(("arbitrary", "arbitrary", "arbitrary"), 40),
    )(_in_hbm(a), _in_hbm(b))


def _tn_matmul_residue(a, b, dil, name):
    length = a.shape[0]
    m, n = a.shape[1] // dil, b.shape[1] // dil
    ts = min(512, length)

    def body(a_ref, b_ref, o_ref):
        @pl.when((pl.program_id(0) == 0) & (pl.program_id(1) == 0))
        def _():
            o_ref[...] = jnp.zeros(o_ref.shape, F32)

        o_ref[...] += _dot_tn(a_ref[...], b_ref[...])

    return _pallas(
        body, name=name, grid=(dil, length // ts),
        in_specs=[pl.BlockSpec((ts, m), lambda r, s: (s, r)), pl.BlockSpec((ts, n), lambda r, s: (s, r))],
        out_specs=pl.BlockSpec((m, n), lambda r, s: (0, 0)),
        out_shape=jax.ShapeDtypeStruct((m, n), F32),
        compiler_params=_params(("arbitrary", "arbitrary"), 40),
    )(_in_hbm(a), _in_hbm(b))


def _mix_bwd(dy, ya, yg, mg, rest, w_out, w_ba, w_bg, w_sp, b_col, ln_g, ln_b, rider=None):
    seq = dy.shape[0]
    tm = 256

    def body(dy_ref, ya_ref, yg_ref, mg_ref, up_ref, zp_ref, gap_ref, gbp_ref, wout_ref, wba_ref, wbg_ref,
             wsp_ref, bcol_ref, lg_ref, lb_ref,
             dya0, dya1, dya2, dpr_ref, dwout_ref, dwba_ref, dwbg_ref, dwsp_ref, dbb_ref, dlg_ref, dlb_ref,
             dzln_s, du_s, slab):
        @pl.when(pl.program_id(0) == 0)
        def _():
            for ref in (dwout_ref, dwba_ref, dwbg_ref, dwsp_ref, dbb_ref, dlg_ref, dlb_ref):
                ref[...] = jnp.zeros(ref.shape, F32)

        dyv = dy_ref[...]
        dm = _dot_nt(dyv, wout_ref[...])
        dwout_ref[...] += _dot_tn(mg_ref[...], dyv)
        yab = ya_ref[...].astype(BF16)
        ygb = yg_ref[...]
        a = _dot(yab, wba_ref[...])
        bm = _dot(ygb, wbg_ref[...])
        ga = jax.nn.sigmoid(gap_ref[...].astype(F32))
        gb = jax.nn.sigmoid(gbp_ref[...].astype(F32))
        dpr_ref[:, 2 * GMLP_W:2 * GMLP_W + D_MODEL] = (dm * a * (ga * (1.0 - ga))).astype(BF16)
        dpr_ref[:, 2 * GMLP_W + D_MODEL:REST_W] = (dm * bm * (gb * (1.0 - gb))).astype(BF16)
        da = (dm * ga).astype(BF16)
        db = (dm * gb).astype(BF16)
        dwba = _dot_tn(yab, da)
        dwbg = _dot_tn(ygb, db)
        shard_w = D_MODEL // N_CHIPS
        for j in range(N_CHIPS):
            dwba_ref[j] += dwba[:, j * shard_w:(j + 1) * shard_w]
            dwbg_ref[j] += dwbg[:, j * shard_w:(j + 1) * shard_w]
        dya = _dot_nt(da, wba_ref[...])
        for dya_ref, d in zip((dya0, dya1, dya2), DILATIONS):
            _put_residue(slab, dya, dya_ref, d, GROUP_W, 0)
        dyg = _dot_nt(db, wbg_ref[...])

        zp = zp_ref[...].astype(F32)
        zhat, rstd = _layernorm_stats(_gelu(zp))
        lg = lg_ref[...]
        zln = (zhat * lg + lb_ref[...]).astype(BF16)
        up = up_ref[...].astype(F32)
        u = _gelu(up)
        tril = _tril_mask()
        for g in range(GMLP_GROUPS):
            wm = jnp.where(tril, wsp_ref[g], 0.0).astype(BF16)
            cols = slice(g * CHUNK, (g + 1) * CHUNK)
            for c in range(tm // CHUNK):
                rows = slice(c * CHUNK, (c + 1) * CHUNK)
                zb = zln[rows, cols]
                sz = _dot(wm, zb) + bcol_ref[g]
                dyg_cg = dyg[rows, cols]
                du_s[rows, cols] = dyg_cg * sz
                dsz = dyg_cg * u[rows, cols]
                dszb = dsz.astype(BF16)
                dbb_ref[g] += jnp.broadcast_to(jnp.sum(dsz, axis=-1, keepdims=True), (CHUNK, CHUNK))
                dwsp_ref[g] += jnp.where(tril, _dot_nt(dszb, zb), 0.0)
                dzln_s[rows, cols] = _dot_tn(wm, dszb)
        dzln = dzln_s[...]
        dlg_ref[...] += jnp.sum(dzln * zhat, axis=0, keepdims=True)
        dlb_ref[...] += jnp.sum(dzln, axis=0, keepdims=True)
        dzh = dzln * lg
        dz = rstd * (dzh - jnp.mean(dzh, axis=-1, keepdims=True) - zhat * jnp.mean(dzh * zhat, axis=-1, keepdims=True))
        dpr_ref[:, GMLP_W:2 * GMLP_W] = (dz * _gelu_grad(zp)).astype(BF16)
        dpr_ref[:, 0:GMLP_W] = (du_s[...] * _gelu_grad(up)).astype(BF16)

    tok = lambda w: pl.BlockSpec((tm, w), lambda i: (i, 0))
    full = lambda *s: pl.BlockSpec(s, lambda i: (0,) * len(s))
    return _call(
        body, name="mix_bwd", grid=(seq // tm,),
        in_specs=[tok(D_MODEL), tok(GROUP_W), tok(GMLP_W), tok(D_MODEL),
                  pl.BlockSpec((tm, GMLP_W), lambda i: (i, 0)), pl.BlockSpec((tm, GMLP_W), lambda i: (i, 1)),
                  pl.BlockSpec((tm, D_MODEL), lambda i: (i, 1)), pl.BlockSpec((tm, D_MODEL), lambda i: (i, 2)),
                  full(D_MODEL, D_MODEL), full(GROUP_W, D_MODEL), full(GMLP_W, D_MODEL),
                  full(GMLP_GROUPS, CHUNK, CHUNK), full(GMLP_GROUPS, CHUNK, 1), full(1, GMLP_W), full(1, GMLP_W)],
        out_specs=[pl.BlockSpec((tm // d, d * GROUP_W), lambda i: (i, 0)) for d in DILATIONS]
        + [tok(REST_W), full(D_MODEL, D_MODEL), full(N_CHIPS, GROUP_W, D_MODEL // N_CHIPS),
           full(N_CHIPS, GMLP_W, D_MODEL // N_CHIPS),
           full(GMLP_GROUPS, CHUNK, CHUNK), full(GMLP_GROUPS, CHUNK, CHUNK), full(1, GMLP_W), full(1, GMLP_W)],
        out_shape=[jax.ShapeDtypeStruct((seq // d, d * GROUP_W), F32) for d in DILATIONS]
        + [jax.ShapeDtypeStruct((seq, REST_W), BF16),
           jax.ShapeDtypeStruct((D_MODEL, D_MODEL), F32), jax.ShapeDtypeStruct((N_CHIPS, GROUP_W, D_MODEL // N_CHIPS), F32),
           jax.ShapeDtypeStruct((N_CHIPS, GMLP_W, D_MODEL // N_CHIPS), F32),
           jax.ShapeDtypeStruct((GMLP_GROUPS, CHUNK, CHUNK), F32),
           jax.ShapeDtypeStruct((GMLP_GROUPS, CHUNK, CHUNK), F32), jax.ShapeDtypeStruct((1, GMLP_W), F32),
           jax.ShapeDtypeStruct((1, GMLP_W), F32)],
        scratch_shapes=[pltpu.VMEM((tm, GMLP_W), F32), pltpu.VMEM((tm, GMLP_W), F32),
                        pltpu.VMEM((GROUP_W // LANES, tm, LANES), F32)],
        params=_params(("arbitrary",), 56),
        args=(dy, ya, yg, mg, rest, rest, rest, rest, w_out, w_ba, w_bg, w_sp, b_col, ln_g, ln_b), rider=rider)


IN_PROJ_BWD_TM = 256


def _in_proj_bwd(dqkv, drest, w_qkv, w_rest, x, dx1, g0, so_far, span, rider=None):
    seq = x.shape[0]
    tm = IN_PROJ_BWD_TM
    off, steps = span
    gx_so_far, dg_so_far = so_far

    def body(d0, d1, d2, dr_ref, w0, w1, w2, wr_ref, x_ref, dx1_ref, g_ref, dg_in_ref, gx_in_ref, gx_ref, dg_ref, slab):
        @pl.when(pl.program_id(0) == 0)
        def _():
            dg_ref[...] = dg_in_ref[...]

        dh = _dot(dr_ref[...], wr_ref[...])
        for d_ref, w_ref, dil in zip((d0, d1, d2), (w0, w1, w2), DILATIONS):
            piece = d_ref[...] if dil == 1 else _get_tokens(slab, d_ref, dil, 3 * GROUP_W, 0, 3 * GROUP_W).astype(BF16)
            dh = dh + _dot(piece, w_ref[...])
        dres, dg = _rmsnorm_bwd(dh, x_ref[...], g_ref[...])
        gx_ref[...] = dx1_ref[...] + dres
        dg_ref[...] += dg

    tok = lambda w: pl.BlockSpec((tm, w), lambda i: (i + off, 0))
    full = lambda *s: pl.BlockSpec(s, lambda i: (0,) * len(s))
    in_specs = ([pl.BlockSpec((tm // d, d * 3 * GROUP_W), lambda i: (i + off, 0)) for d in DILATIONS] + [tok(REST_W)]
                + [_resident((3 * GROUP_W, D_MODEL))] * 3 + [_resident((REST_W, D_MODEL))]
                + [tok(D_MODEL), tok(D_MODEL), full(1, D_MODEL), full(1, D_MODEL), HBM_SPEC])
    return _call(
        body, name=f"in_proj_bwd_{off}", grid=(steps,), in_specs=in_specs,
        out_specs=[tok(D_MODEL), full(1, D_MODEL)],
        out_shape=[jax.ShapeDtypeStruct((seq, D_MODEL), F32), jax.ShapeDtypeStruct((1, D_MODEL), F32)],
        scratch_shapes=[pltpu.VMEM((3 * GROUP_W // LANES, tm, LANES), F32)],
        params=_params(("arbitrary",), 48), args=(*dqkv, drest, *w_qkv, w_rest, x, dx1, g0, dg_so_far, gx_so_far),
        rider=rider, aliases={len(in_specs) - 1: 0})


def _adamw(w, g, m, v, name):
    rows, cols = w.shape
    tr = _row_tile(rows) if rows % 16 == 0 else rows
    c1 = 1.0 - ADAM_B1 ** ADAM_STEP
    c2 = 1.0 - ADAM_B2 ** ADAM_STEP

    def body(w_ref, g_ref, m_ref, v_ref, go_ref, d_ref, nm_ref, nv_ref):
        gv = g_ref[...]
        go_ref[...] = gv
        nm = ADAM_B1 * m_ref[...] + (1.0 - ADAM_B1) * gv
        nv = ADAM_B2 * v_ref[...] + (1.0 - ADAM_B2) * (gv * gv)
        d_ref[...] = -ADAM_LR * ((nm / c1) / (jnp.sqrt(nv / c2) + ADAM_EPS) + ADAM_WD * w_ref[...])
        nm_ref[...] = nm
        nv_ref[...] = nv

    spec = pl.BlockSpec((tr, cols), lambda i: (i, 0))
    return _pallas(
        body, name=name, grid=(rows // tr,),
        in_specs=[spec] * 4, out_specs=[spec] * 4,
        out_shape=[jax.ShapeDtypeStruct((rows, cols), F32)] * 4,
        compiler_params=_params(("arbitrary",), 40),
    )(*map(_in_hbm, (w, g, m, v)))


def _place():
    x, y, c = lax.axis_index("x"), lax.axis_index("y"), lax.axis_index("c")
    chips = [(1 - x, y), (x, 1 - y), (1 - x, 1 - y)]
    return x, y, c, chips


class _Exchange:
    def __init__(self, inputs, out_shapes, n_sems, start, finish, aliases=None):
        self.inputs, self.out_shapes, self.n_sems = list(inputs), list(out_shapes), n_sems
        self.start, self.finish, self.aliases = start, finish, dict(aliases or {})

    def scratch(self):
        return [pltpu.SemaphoreType.DMA((self.n_sems,)), pltpu.SemaphoreType.DMA((self.n_sems,))]


def _together(*parts):
    ins = [len(p.inputs) for p in parts]
    outs = [len(p.out_shapes) for p in parts]

    def split(refs, counts):
        pos, pieces = 0, []
        for cnt in counts:
            pieces.append(refs[pos:pos + cnt])
            pos += cnt
        return pieces

    def run(which):
        def go(in_refs, out_refs, *sems):
            for k, (p, i, o) in enumerate(zip(parts, split(in_refs, ins), split(out_refs, outs))):
                getattr(p, which)(i, o, sems[2 * k], sems[2 * k + 1])
        return go

    both = _Exchange([a for p in parts for a in p.inputs], [s for p in parts for s in p.out_shapes], 0, run("start"),
                     run("finish"))
    both.aliases = {sum(ins[:k]) + i: sum(outs[:k]) + o for k, p in enumerate(parts) for i, o in p.aliases.items()}
    both.scratch = lambda: [s for p in parts for s in p.scratch()]
    return both


def _run_exchange(ex, name):
    n_in, n_out = len(ex.inputs), len(ex.out_shapes)

    def body(*refs):
        ins, outs, sems = refs[:n_in], refs[n_in:n_in + n_out], refs[n_in + n_out:]
        ex.start(ins, outs, *sems)
        ex.finish(ins, outs, *sems)

    return _pallas(
        body, name=name, in_specs=[HBM_SPEC] * n_in, out_specs=[HBM_SPEC] * n_out, out_shape=ex.out_shapes,
        scratch_shapes=ex.scratch(), input_output_aliases=ex.aliases,
    )(*ex.inputs)


def _call(body, *, name, grid, in_specs, out_specs, out_shape, scratch_shapes, params, args, rider=None, aliases=None):
    in_specs, out_specs, out_shape, scratch_shapes = list(in_specs), list(out_specs), list(out_shape), list(scratch_shapes)
    aliases = dict(aliases or {})
    args = [_in_hbm(a) for a in args]
    if rider is None:
        outs = _pallas(body, name=name, grid=grid, in_specs=in_specs, out_specs=out_specs, out_shape=out_shape,
                              scratch_shapes=scratch_shapes, input_output_aliases=aliases, compiler_params=params)(*args)
        return list(outs), []
    n_in, n_out, n_scr = len(in_specs), len(out_specs), len(scratch_shapes)
    r_in, r_out = len(rider.inputs), len(rider.out_shapes)

    def wrapped(*refs):
        ins, r_ins = refs[:n_in], refs[n_in:n_in + r_in]
        pos = n_in + r_in
        outs, r_outs = refs[pos:pos + n_out], refs[pos + n_out:pos + n_out + r_out]
        pos += n_out + r_out
        scr, sems = refs[pos:pos + n_scr], refs[pos + n_scr:]
        ids = [pl.program_id(k) for k in range(len(grid))]
        first, last = ids[0] == 0, ids[0] == grid[0] - 1
        for k in range(1, len(grid)):
            first, last = first & (ids[k] == 0), last & (ids[k] == grid[k] - 1)

        @pl.when(first)
        def _():
            rider.start(r_ins, r_outs, *sems)

        body(*ins, *outs, *scr)

        @pl.when(last)
        def _():
            rider.finish(r_ins, r_outs, *sems)

    outs = _pallas(
        wrapped, name=name, grid=grid, in_specs=in_specs + [HBM_SPEC] * r_in, out_specs=out_specs + [HBM_SPEC] * r_out,
        out_shape=out_shape + rider.out_shapes, scratch_shapes=scratch_shapes + rider.scratch(),
        input_output_aliases={**aliases, **{n_in + i: n_out + o for i, o in rider.aliases.items()}}, compiler_params=params,
    )(*args, *rider.inputs)
    return list(outs[:n_out]), list(outs[n_out:])


def _stage_weights(shards):
    n = len(shards)

    def body(*refs):
        ins, outs, stages, sems = refs[:n], refs[n:2 * n], refs[2 * n:3 * n], refs[3 * n]
        x, y, _, _ = _place()
        copies = []
        for t in range(n):
            stages[t][...] = ins[t][...].astype(BF16)
            copies.append(pltpu.make_async_copy(stages[t], outs[t].at[2 * x + y], sems.at[t]))
            copies[-1].start()
        for cp in copies:
            cp.wait()

    stage_bytes = sum(s.size * 6 for s in shards)
    return _pallas(
        body, name="stage_weights", in_specs=[VMEM_SPEC] * n, out_specs=[HBM_SPEC] * n,
        out_shape=[jax.ShapeDtypeStruct((N_CHIPS,) + s.shape, BF16) for s in shards],
        scratch_shapes=[pltpu.VMEM(s.shape, BF16) for s in shards] + [pltpu.SemaphoreType.DMA((n,))],
        compiler_params=pltpu.CompilerParams(vmem_limit_bytes=stage_bytes + 8 * MIB),
    )(*shards)


def _gather(buffers, stage="both"):
    n = len(buffers)
    halves = [b.shape[1] // 2 for b in buffers]

    def half_of(outs, t, chip, which):
        return outs[t].at[chip, pl.ds(which * halves[t], halves[t]), :]

    def copy(outs, sems, t, k, chip, which, to):
        rows = half_of(outs, t, chip, which)
        return pltpu.make_async_remote_copy(src_ref=rows, dst_ref=rows, send_sem=sems[0].at[6 * t + k],
                                            recv_sem=sems[1].at[6 * t + k], device_id=to, device_id_type=MESH)

    def to_chips(outs, sems, what):
        x, y, c, chips = _place()
        for t in range(n):
            for j, (px, py) in enumerate(chips):
                if what == "start":
                    copy(outs, sems, t, j, 2 * x + y, c, (px, py, c)).start()
                else:
                    copy(outs, sems, t, j, 2 * px + py, c, (px, py, c)).wait_recv()
                    copy(outs, sems, t, j, 2 * x + y, c, (px, py, c)).wait_send()

    def to_sibling(outs, sems, what):
        x, y, c, chips = _place()
        for t in range(n):
            for j, (px, py) in enumerate(chips):
                if what == "start":
                    copy(outs, sems, t, 3 + j, 2 * px + py, c, (x, y, 1 - c)).start()
                else:
                    copy(outs, sems, t, 3 + j, 2 * px + py, 1 - c, (x, y, 1 - c)).wait_recv()
                    copy(outs, sems, t, 3 + j, 2 * px + py, c, (x, y, 1 - c)).wait_send()

    def start(ins, outs, *sems):
        (to_sibling if stage == "pair" else to_chips)(outs, sems, "start")

    def finish(ins, outs, *sems):
        if stage != "pair":
            to_chips(outs, sems, "finish")
        if stage == "both":
            to_sibling(outs, sems, "start")
        if stage != "chips":
            to_sibling(outs, sems, "finish")

    return _Exchange(buffers, [jax.ShapeDtypeStruct(b.shape, b.dtype) for b in buffers], 6 * n, start, finish,
                     aliases={t: t for t in range(n)})


def _pair_exchange(grads):
    n = len(grads)
    halves = [g.shape[1] // 2 for g in grads]

    def copies(ins, outs, send_sems, recv_sems):
        x, y, c, _ = _place()
        return [pltpu.make_async_remote_copy(
            src_ref=ins[t].at[:, pl.ds((1 - c) * halves[t], halves[t]), :], dst_ref=outs[t],
            send_sem=send_sems.at[t], recv_sem=recv_sems.at[t], device_id=(x, y, 1 - c), device_id_type=MESH)
            for t in range(n)]

    def start(*refs):
        for cp in copies(*refs):
            cp.start()

    def finish(*refs):
        for cp in copies(*refs):
            cp.wait()

    return _Exchange(grads, [jax.ShapeDtypeStruct((N_CHIPS, h, g.shape[2]), F32) for g, h in zip(grads, halves)], n,
                     start, finish)


def _row_tile(rows):
    return max(t for t in range(16, 257, 16) if rows % t == 0)


def _pair_add(grad, other, place, name):
    _, rows, cols = grad.shape
    rh = rows // 2
    tr = _row_tile(rh)
    nb = rh // tr

    def body(p_ref, g_ref, a_ref, wire_ref, own_ref):
        s = g_ref[...] + a_ref[...]
        wire_ref[...] = s.astype(BF16)

        @pl.when(pl.program_id(1) == p_ref[1])
        def _():
            own_ref[...] = s

    blk = (None, tr, cols)
    return _pallas(
        body, name=name,
        grid_spec=pltpu.PrefetchScalarGridSpec(
            num_scalar_prefetch=1, grid=(nb, N_CHIPS),
            in_specs=[pl.BlockSpec(blk, lambda i, j, p: (j, p[0] * nb + i, 0)), pl.BlockSpec(blk, lambda i, j, p: (j, i, 0))],
            out_specs=[pl.BlockSpec(blk, lambda i, j, p: (j, i, 0)), pl.BlockSpec((tr, cols), lambda i, j, p: (i, 0))]),
        out_shape=[jax.ShapeDtypeStruct((N_CHIPS, rh, cols), BF16), jax.ShapeDtypeStruct((rh, cols), F32)],
        compiler_params=_params(("arbitrary", "arbitrary"), 32),
    )(place, grad, other)


def _chip_exchange(wires):
    n = len(wires)

    def copies(ins, outs, send_sems, recv_sems):
        x, y, c, chips = _place()
        return [pltpu.make_async_remote_copy(
            src_ref=ins[t].at[2 * px + py], dst_ref=outs[t].at[j], send_sem=send_sems.at[3 * t + j],
            recv_sem=recv_sems.at[3 * t + j], device_id=(px, py, c), device_id_type=MESH)
            for t in range(n) for j, (px, py) in enumerate(chips)]

    def start(*refs):
        for cp in copies(*refs):
            cp.start()

    def finish(*refs):
        for cp in copies(*refs):
            cp.wait()

    return _Exchange(wires, [jax.ShapeDtypeStruct((3,) + w.shape[1:], BF16) for w in wires], 3 * n, start, finish)


def _chip_add(own, arrived, place, name):
    rh, cols = own.shape
    tr = _row_tile(rh)
    nb = rh // tr

    def body(p_ref, s_ref, b0, b1, b2, o_ref):
        o_ref[...] = ((s_ref[...] + b0[...].astype(F32)) + b1[...].astype(F32)) + b2[...].astype(F32)

    blk = (None, tr, cols)
    return _pallas(
        body, name=name,
        grid_spec=pltpu.PrefetchScalarGridSpec(
            num_scalar_prefetch=1, grid=(nb,),
            in_specs=[pl.BlockSpec((tr, cols), lambda i, p: (i, 0)), pl.BlockSpec(blk, lambda i, p: (0, i, 0)),
                      pl.BlockSpec(blk, lambda i, p: (1, i, 0)), pl.BlockSpec(blk, lambda i, p: (2, i, 0))],
            out_specs=pl.BlockSpec((tr, cols), lambda i, p: (p[0] * nb + i, 0))),
        out_shape=jax.ShapeDtypeStruct((2 * rh, cols), F32),
        compiler_params=_params(("arbitrary",), 32),
    )(place, own, arrived, arrived, arrived)


def _pair_share(halves):
    n = len(halves)
    rhs = [h.shape[0] // 2 for h in halves]

    def copy(outs, send_sems, recv_sems, t, which):
        x, y, c, _ = _place()
        rows = outs[t].at[pl.ds(which * rhs[t], rhs[t]), :]
        return pltpu.make_async_remote_copy(src_ref=rows, dst_ref=rows, send_sem=send_sems.at[t], recv_sem=recv_sems.at[t],
                                            device_id=(x, y, 1 - c), device_id_type=MESH)

    def start(ins, outs, send_sems, recv_sems):
        c = lax.axis_index("c")
        for t in range(n):
            copy(outs, send_sems, recv_sems, t, c).start()

    def finish(ins, outs, send_sems, recv_sems):
        c = lax.axis_index("c")
        for t in range(n):
            copy(outs, send_sems, recv_sems, t, c).wait_send()
            copy(outs, send_sems, recv_sems, t, 1 - c).wait_recv()

    return _Exchange(halves, [jax.ShapeDtypeStruct(h.shape, F32) for h in halves], n, start, finish,
                     aliases={t: t for t in range(n)})


class _GradReduction:
    def __init__(self, grads, place, tag):
        self.names, self.grads, self.place, self.tag = list(grads), grads, place, tag

    def pair_exchange(self):
        return _pair_exchange([self.grads[n] for n in self.names])

    def chip_exchange(self, others):
        sums = [_pair_add(self.grads[n], o, self.place, f"{self.tag}_pair_add_{n}") for n, o in zip(self.names, others)]
        self.owns = [own for _, own in sums]
        return _chip_exchange([wire for wire, _ in sums])

    def pair_share(self, arrived):
        return _pair_share([_chip_add(own, arr, self.place, f"{self.tag}_chip_add_{n}")
                            for n, own, arr in zip(self.names, self.owns, arrived)])

    def result(self, shared):
        return dict(zip(self.names, shared))


def _all_reduce_small(p):
    rows, lanes = p.shape
    flips = [(fx, fy, fc) for fx in (0, 1) for fy in (0, 1) for fc in (0, 1)][1:]

    def body(p_ref, o_ref, buf, send_sems, recv_sems):
        x, y, c, _ = _place()
        me = 4 * x + 2 * y + c
        buf[me] = p_ref[...]
        peers = [((1 - x) if fx else x, (1 - y) if fy else y, (1 - c) if fc else c) for fx, fy, fc in flips]
        cps = []
        for k, peer in enumerate(peers):
            cp = pltpu.make_async_remote_copy(
                src_ref=p_ref, dst_ref=buf.at[me], send_sem=send_sems.at[k], recv_sem=recv_sems.at[k],
                device_id=peer, device_id_type=MESH)
            cp.start()
            cps.append(cp)
        for k, (px, py, pc) in enumerate(peers):
            pltpu.make_async_remote_copy(
                src_ref=p_ref, dst_ref=buf.at[4 * px + 2 * py + pc], send_sem=send_sems.at[k], recv_sem=recv_sems.at[k],
                device_id=(px, py, pc), device_id_type=MESH).wait_recv()
        for cp in cps:
            cp.wait_send()
        acc = buf[0]
        for s in range(1, 8):
            acc = acc + buf[s]
        o_ref[...] = acc

    return _pallas(
        body, name="small_all_reduce", in_specs=[VMEM_SPEC], out_specs=VMEM_SPEC,
        out_shape=jax.ShapeDtypeStruct((rows, lanes), F32),
        scratch_shapes=[pltpu.VMEM((8, rows, lanes), F32), pltpu.SemaphoreType.DMA((7,)), pltpu.SemaphoreType.DMA((7,))],
        compiler_params=pltpu.CompilerParams(vmem_limit_bytes=32 * MIB),
    )(p)


BIG = ("w_in", "w_branch_attn", "w_branch_gmlp", "w_out", "w_mlp_in", "w_mlp_out")
COLUMN_SHARDED = ("w_branch_attn", "w_branch_gmlp", "w_mlp_in")
SMALL = ("norm_pre_mix", "w_spatial", "b_spatial", "ln_v_gain", "ln_v_bias", "norm_post_mix", "norm_pre_mlp", "norm_post_mlp")
ORDER = ("norm_pre_mix", "w_in", "w_spatial", "b_spatial", "ln_v_gain", "ln_v_bias", "w_branch_attn", "w_branch_gmlp",
         "w_out", "norm_post_mix", "norm_pre_mlp", "w_mlp_in", "w_mlp_out", "norm_post_mlp")


def _full_weight(name, gathered):
    if name in COLUMN_SHARDED:
        return jnp.transpose(gathered, (1, 0, 2)).reshape(gathered.shape[1], -1)
    return gathered.reshape(-1, gathered.shape[2])


def _rows8(a):
    a = a.reshape(-1, 128)
    pad = (-a.shape[0]) % 8
    return jnp.pad(a, ((0, pad), (0, 0))) if pad else a


def _qkv_columns(group):
    return [(sec * ATTN_W + group * GROUP_W, sec * ATTN_W + (group + 1) * GROUP_W) for sec in range(3)]


def _device_step(x, target, small, shards, place):
    seq = x.shape[0]
    g0, g1, g2, g3 = small["norm_pre_mix"], small["norm_post_mix"], small["norm_pre_mlp"], small["norm_post_mlp"]
    w_sp = small["w_spatial"]
    b_col = small["b_spatial"].reshape(GMLP_GROUPS, CHUNK, 1)
    ln_g, ln_b = small["ln_v_gain"], small["ln_v_bias"]

    staged = _stage_weights(shards)
    tables, (w_in,) = _rope_tables(seq, rider=_gather(staged[:1]))
    w_in = _full_weight("w_in", w_in)
    (*hq, rest), landed = _in_proj(x, g0, w_in, *tables[1], rider=_gather(staged[1:], "chips"))
    h, qkv = hq[:N_GROUPS], hq[N_GROUPS:]

    o_l, gathered = _attn_fwd(qkv[0], DILATIONS[0], rider=_gather(landed, "pair"))
    full = {n: _full_weight(n, gw) for n, gw in zip(BIG[1:], gathered)}
    for g in range(1, N_GROUPS):
        o_l.extend(_attn_fwd(qkv[g], DILATIONS[g])[0])
    *ya_l, yg, mg, y, x1 = _mix_fwd(o_l, rest, x, w_sp, b_col, ln_g, ln_b, full["w_branch_attn"], full["w_branch_gmlp"],
                                    full["w_out"], g1)
    ya, lse = ya_l[0::2], ya_l[1::2]
    h2, a, dy2, dout, loss8, dg3 = _mlp_fwd(x1, g2, g3, full["w_mlp_in"], full["w_mlp_out"], target)
    dap, dx1, dy, dg2, dg1 = _mlp_bwd(dy2, a, full["w_mlp_out"], full["w_mlp_in"], dout, x1, y, g2, g1)
    d_wmo = _tn_matmul(a, dy2, "grad_w_mlp_out", 1024, 1024, square_a=True)
    d_wmi = _tn_matmul(h2, dap, "grad_w_mlp_in", 1024, 1024, column_shards=True)

    mlp = _GradReduction({"w_mlp_in": d_wmi, "w_mlp_out": d_wmo.reshape(N_CHIPS, D_FF // N_CHIPS, D_MODEL)}, place, "mlp")
    (*dya, drest, d_wout, d_wba, d_wbg, d_wsp, d_bb, d_lg, d_lb), mlp_others = _mix_bwd(
        dy, ya[0], yg, mg, rest, full["w_out"], full["w_branch_attn"], full["w_branch_gmlp"], w_sp, b_col, ln_g, ln_b,
        rider=mlp.pair_exchange())
    mix = _GradReduction({"w_branch_attn": d_wba, "w_branch_gmlp": d_wbg,
                          "w_out": d_wout.reshape(N_CHIPS, D_MODEL // N_CHIPS, D_MODEL)}, place, "mix")
    n_mlp = len(mlp.names)
    attn = lambda g, rider: _attn_bwd(qkv[g], dya[g], ya[g], lse[g], *tables[DILATIONS[g]], DILATIONS[g], rider=rider)
    dqkv0, riding = attn(0, _together(mlp.chip_exchange(mlp_others), mix.pair_exchange()))
    dqkv1, riding = attn(1, _together(mlp.pair_share(riding[:n_mlp]), mix.chip_exchange(riding[n_mlp:])))
    reduced = mlp.result(riding[:n_mlp])
    dqkv2, riding = attn(2, mix.pair_share(riding[n_mlp:]))
    reduced.update(mix.result(riding))
    dqkv = [dqkv0, dqkv1, dqkv2]

    d_qkv = [_tn_matmul_residue(dqkv[g], h[g], dil, f"grad_w_in_qkv{g}") for g, dil in enumerate(DILATIONS)]
    d_rest = _tn_matmul(drest, h[0], "grad_w_in_rest", 1024, 1024)
    d_win = jnp.concatenate([d_qkv[g][s * GROUP_W:(s + 1) * GROUP_W] for s in range(3) for g in range(N_GROUPS)]
                            + [d_rest], axis=0)
    first = _GradReduction({"w_in": d_win.reshape(N_CHIPS, IN_W // N_CHIPS, D_MODEL)}, place, "w_in")
    w_qkv = [jnp.concatenate([w_in[lo:hi] for lo, hi in _qkv_columns(g)], axis=0) for g in range(N_GROUPS)]
    w_rest = w_in[QKV_W:]
    tiles = seq // IN_PROJ_BWD_TM
    so_far = (lax.empty((seq, D_MODEL), F32), jnp.zeros((1, D_MODEL), F32))
    in_bwd = lambda so_far, span, rider: _in_proj_bwd(dqkv, drest, w_qkv, w_rest, x, dx1, g0, so_far, span, rider=rider)
    so_far, riding = in_bwd(so_far, (0, tiles // 4), first.pair_exchange())
    so_far, riding = in_bwd(so_far, (tiles // 4, tiles // 2), first.chip_exchange(riding))
    shared = _run_exchange(first.pair_share(riding), "w_in_pair_share")
    (grad_x, dg0), _ = in_bwd(so_far, (3 * tiles // 4, tiles // 4), None)
    reduced.update(first.result(shared))
    little = {"norm_pre_mix": dg0, "w_spatial": d_wsp, "b_spatial": d_bb[:, :, 0], "ln_v_gain": d_lg, "ln_v_bias": d_lb,
              "norm_post_mix": dg1, "norm_pre_mlp": dg2, "norm_post_mlp": dg3}
    return loss8[0, 0], grad_x, reduced, little


def kernel(x, norm_pre_mix, w_in, w_spatial, b_spatial, ln_v_gain, ln_v_bias, w_branch_attn, w_branch_gmlp, w_out, norm_post_mix, norm_pre_mlp, w_mlp_in, w_mlp_out, norm_post_mlp, loss_target, m_norm_pre_mix, m_w_in, m_w_spatial, m_b_spatial, m_ln_v_gain, m_ln_v_bias, m_w_branch_attn, m_w_branch_gmlp, m_w_out, m_norm_post_mix, m_norm_pre_mlp, m_w_mlp_in, m_w_mlp_out, m_norm_post_mlp, v_norm_pre_mix, v_w_in, v_w_spatial, v_b_spatial, v_ln_v_gain, v_ln_v_bias, v_w_branch_attn, v_w_branch_gmlp, v_w_out, v_norm_post_mix, v_norm_pre_mlp, v_w_mlp_in, v_w_mlp_out, v_norm_post_mlp):
    given = dict(norm_pre_mix=norm_pre_mix, w_in=w_in, w_spatial=w_spatial, b_spatial=b_spatial, ln_v_gain=ln_v_gain,
                 ln_v_bias=ln_v_bias, w_branch_attn=w_branch_attn, w_branch_gmlp=w_branch_gmlp, w_out=w_out,
                 norm_post_mix=norm_post_mix, norm_pre_mlp=norm_pre_mlp, w_mlp_in=w_mlp_in, w_mlp_out=w_mlp_out,
                 norm_post_mlp=norm_post_mlp)
    moments_m = dict(norm_pre_mix=m_norm_pre_mix, w_in=m_w_in, w_spatial=m_w_spatial, b_spatial=m_b_spatial,
                     ln_v_gain=m_ln_v_gain, ln_v_bias=m_ln_v_bias, w_branch_attn=m_w_branch_attn,
                     w_branch_gmlp=m_w_branch_gmlp, w_out=m_w_out, norm_post_mix=m_norm_post_mix,
                     norm_pre_mlp=m_norm_pre_mlp, w_mlp_in=m_w_mlp_in, w_mlp_out=m_w_mlp_out, norm_post_mlp=m_norm_post_mlp)
    moments_v = dict(norm_pre_mix=v_norm_pre_mix, w_in=v_w_in, w_spatial=v_w_spatial, b_spatial=v_b_spatial,
                     ln_v_gain=v_ln_v_gain, ln_v_bias=v_ln_v_bias, w_branch_attn=v_w_branch_attn,
                     w_branch_gmlp=v_w_branch_gmlp, w_out=v_w_out, norm_post_mix=v_norm_post_mix,
                     norm_pre_mlp=v_norm_pre_mlp, w_mlp_in=v_w_mlp_in, w_mlp_out=v_w_mlp_out, norm_post_mlp=v_norm_post_mlp)
    cx, cy, cc = lax.axis_index("x"), lax.axis_index("y"), lax.axis_index("c")

    shards = [given[n][0].T if n == "w_in" else given[n][0] for n in BIG]
    small = {n: given[n][0] if given[n].ndim > 2 else given[n] for n in SMALL}
    place = jnp.stack([cc, 2 * cx + cy]).astype(jnp.int32)
    loss, grad_x, grad_shard, grads = _device_step(x[0], loss_target[0], small, shards, place)
    loss = lax.psum(loss, ("x", "y", "c"))

    packed = jnp.concatenate([_rows8(grads[n]) for n in SMALL], axis=0)
    summed = _all_reduce_small(packed)
    row = 0
    for n in SMALL:
        shape = given[n][0].shape
        cnt = -(-(given[n][0].size // 128) // 8) * 8
        grad_shard[n] = summed[row:row + given[n][0].size // 128].reshape(shape)
        row += cnt

    grad_out, deltas, new_m, new_v = {}, {}, {}, {}
    for n in ORDER:
        shape = given[n].shape
        if n == "w_in":
            outs = _adamw(given[n][0].T, grad_shard[n], moments_m[n][0].T, moments_v[n][0].T, "adamw_" + n)
            outs = [o.T for o in outs]
        else:
            two_d = (-1, shape[-1])
            outs = _adamw(given[n].reshape(two_d), grad_shard[n].reshape(two_d), moments_m[n].reshape(two_d),
                          moments_v[n].reshape(two_d), "adamw_" + n)
        grad_out[n], deltas[n], new_m[n], new_v[n] = [o.reshape(shape) for o in outs]
    return (loss, grad_x[None], *[grad_out[n] for n in ORDER], *[deltas[n] for n in ORDER], *[new_m[n] for n in ORDER],
            *[new_v[n] for n in ORDER])
```

```python
import math

import jax
import jax.numpy as jnp
from jax import lax
from jax.experimental import pallas as pl
from jax.experimental.pallas import tpu as pltpu

F32 = jnp.float32
BF16 = jnp.bfloat16
MESH = pl.DeviceIdType.MESH

D_MODEL = 1024
HEAD_DIM = 64
HEADS_PER_GROUP = 4
GROUP_W = HEADS_PER_GROUP * HEAD_DIM
DILATIONS = (1, 4, 16)
N_GROUPS = len(DILATIONS)
ATTN_W = N_GROUPS * GROUP_W
QKV_W = 3 * ATTN_W
GMLP_W = 512
GMLP_GROUPS = 4
CHUNK = 128
REST_W = 2 * GMLP_W + 2 * D_MODEL
IN_W = QKV_W + REST_W
D_FF = 4096
QBLK = 128
ROPE_THETA = 10000.0
EPS = 1e-6
NEG = -1e30
SCALE = HEAD_DIM ** -0.5
N_CHIPS = 4

ADAM_LR = 0.001
ADAM_B1 = 0.9
ADAM_B2 = 0.999
ADAM_EPS = 1e-08
ADAM_WD = 0.01
ADAM_STEP = 10

MIB = 1024 * 1024
HBM_SPEC = pl.BlockSpec(memory_space=pltpu.HBM)
VMEM_SPEC = pl.BlockSpec(memory_space=pltpu.VMEM)


MLP_TM = 256


def _params(semantics, vmem_mib):
    return pltpu.CompilerParams(dimension_semantics=semantics, vmem_limit_bytes=vmem_mib * MIB)


def _in_hbm(a):
    return pltpu.with_memory_space_constraint(a, pltpu.HBM) if a.size * a.dtype.itemsize >= MIB else a


def _pallas(body, **kwargs):
    return pl.pallas_call(body, **kwargs)


def _resident(shape):
    return pl.BlockSpec(shape, lambda *_: (0,) * len(shape), pipeline_mode=pl.Buffered(1))


def _dot(a, b):
    return jnp.dot(a, b, preferred_element_type=F32)


def _dot_nt(a, b):
    return lax.dot_general(a, b, (((1,), (1,)), ((), ())), preferred_element_type=F32)


def _dot_tn(a, b):
    return lax.dot_general(a, b, (((0,), (0,)), ((), ())), preferred_element_type=F32)


_GELU_C = math.sqrt(2.0 / math.pi)


def _gelu(x):
    return x * (0.5 * (1.0 + jnp.tanh(_GELU_C * (x + 0.044715 * (x * x * x)))))


def _gelu_grad(x):
    t = jnp.tanh(_GELU_C * (x + 0.044715 * (x * x * x)))
    return 0.5 * (1.0 + t) + 0.5 * x * (1.0 - t * t) * (_GELU_C * (1.0 + 3.0 * 0.044715 * (x * x)))


def _rsqrt_ms(v):
    return lax.rsqrt(jnp.mean(v * v, axis=-1, keepdims=True) + EPS)


def _rmsnorm_bwd(dn, src, gain):
    r = _rsqrt_ms(src)
    t = gain * dn
    dgain = jnp.sum(dn * (src * r), axis=0, keepdims=True)
    dsrc = r * t - src * ((r * r * r) * jnp.mean(t * src, axis=-1, keepdims=True))
    return dsrc, dgain


def _rot_half(v):
    w = v.shape[-1]
    lane = lax.broadcasted_iota(jnp.int32, v.shape, v.ndim - 1)
    return jnp.where((lane % HEAD_DIM) < HEAD_DIM // 2, pltpu.roll(v, w - HEAD_DIM // 2, v.ndim - 1),
                     pltpu.roll(v, HEAD_DIM // 2, v.ndim - 1))


def _head_masks(shape):
    lane = lax.broadcasted_iota(jnp.int32, shape, 1)
    return [(lane >= h * HEAD_DIM) & (lane < (h + 1) * HEAD_DIM) for h in range(HEADS_PER_GROUP)]


def _head_stack(block, hmask):
    zero = jnp.zeros((), block.dtype)
    return jnp.concatenate([jnp.where(hm, block, zero) for hm in hmask], axis=0)


LANES = 128


def _put_residue(slab, val, out_ref, dil, width, col0):
    tm, w = val.shape
    if dil == 1:
        out_ref[:, col0:col0 + w] = val.astype(out_ref.dtype)
        return
    for k in range(w // LANES):
        slab[k] = val[:, k * LANES:(k + 1) * LANES]
    for r in range(dil):
        for k in range(w // LANES):
            c = r * width + col0 + k * LANES
            out_ref[:, c:c + LANES] = slab[k, pl.ds(r, tm // dil, stride=dil), :].astype(out_ref.dtype)


def _get_tokens(slab, in_ref, dil, width, col0, w):
    if dil == 1:
        return in_ref[:, col0:col0 + w].astype(F32)
    rows = in_ref.shape[0]
    for r in range(dil):
        for k in range(w // LANES):
            c = r * width + col0 + k * LANES
            slab[k, pl.ds(r, rows, stride=dil), :] = in_ref[:, c:c + LANES].astype(F32)
    return jnp.concatenate([slab[k] for k in range(w // LANES)], axis=1)


def _rope_tables(seq, rider=None):
    half = HEAD_DIM // 2
    inv_freq = ROPE_THETA ** (-jnp.arange(half, dtype=F32) / half)
    freq = jnp.tile(inv_freq, LANES // half).reshape(1, LANES)
    tm = 512

    def body(f_ref, *refs):
        outs, slab_c, slab_s = refs[:-2], refs[-2], refs[-1]
        row = lax.broadcasted_iota(jnp.int32, (tm, LANES), 0) + pl.program_id(0) * tm
        lane = lax.broadcasted_iota(jnp.int32, (tm, LANES), 1)
        ang = row.astype(F32) * f_ref[...]
        cos = jnp.cos(ang)
        sin = jnp.where((lane % HEAD_DIM) < half, -jnp.sin(ang), jnp.sin(ang))
        slab_c[0] = cos
        slab_s[0] = sin
        for i, dil in enumerate(DILATIONS):
            for tab, slab in ((outs[2 * i], slab_c), (outs[2 * i + 1], slab_s)):
                for r in range(dil):
                    piece = slab[0, pl.ds(r, tm // dil, stride=dil), :] if dil > 1 else slab[0]
                    for k in range(GROUP_W // LANES):
                        tab[:, r * GROUP_W + k * LANES:r * GROUP_W + (k + 1) * LANES] = piece

    outs, riding = _call(
        body, name="rope_tables", grid=(seq // tm,),
        in_specs=[pl.BlockSpec((1, LANES), lambda i: (0, 0))],
        out_specs=[pl.BlockSpec((tm // d, d * GROUP_W), lambda i: (i, 0)) for d in DILATIONS for _ in range(2)],
        out_shape=[jax.ShapeDtypeStruct((seq // d, d * GROUP_W), F32) for d in DILATIONS for _ in range(2)],
        scratch_shapes=[pltpu.VMEM((1, tm, LANES), F32)] * 2,
        params=_params(("arbitrary",), 32), args=(freq,), rider=rider)
    return {d: (outs[2 * i], outs[2 * i + 1]) for i, d in enumerate(DILATIONS)}, riding


def _in_proj(x, g0, w_in, cos_t, sin_t, rider=None):
    seq = x.shape[0]
    tm, tn = 256, GROUP_W
    n_qk = 2 * ATTN_W // tn
    n_qkv = QKV_W // tn

    def body(x_ref, g_ref, w_ref, cos_ref, sin_ref, *refs):
        h_refs, qkv_refs, rest_ref, slab = refs[:N_GROUPS], refs[N_GROUPS:2 * N_GROUPS], refs[2 * N_GROUPS], refs[-1]
        xv = x_ref[...]
        hf = (xv * _rsqrt_ms(xv)) * g_ref[...]
        hb = hf.astype(BF16)
        for g, dil in enumerate(DILATIONS):
            _put_residue(slab, hf, h_refs[g], dil, D_MODEL, 0)
        cos, sin = cos_ref[...], sin_ref[...]
        for j in range(IN_W // tn):
            p = _dot_nt(hb, w_ref[j * tn:(j + 1) * tn, :])
            if j < n_qkv:
                if j < n_qk:
                    p = p * cos + _rot_half(p) * sin
                section, g = divmod(j, N_GROUPS)
                _put_residue(slab, p, qkv_refs[g], DILATIONS[g], 3 * GROUP_W, section * GROUP_W)
            else:
                rest_ref[:, (j - n_qkv) * tn:(j - n_qkv + 1) * tn] = p.astype(BF16)

    return _call(
        body, name="in_proj", grid=(seq // tm,),
        in_specs=[pl.BlockSpec((tm, D_MODEL), lambda i: (i, 0)),
                  pl.BlockSpec((1, D_MODEL), lambda i: (0, 0)),
                  _resident((IN_W, D_MODEL)),
                  pl.BlockSpec((tm, GROUP_W), lambda i: (i, 0)),
                  pl.BlockSpec((tm, GROUP_W), lambda i: (i, 0))],
        out_specs=[pl.BlockSpec((tm // d, d * D_MODEL), lambda i: (i, 0)) for d in DILATIONS]
        + [pl.BlockSpec((tm // d, d * 3 * GROUP_W), lambda i: (i, 0)) for d in DILATIONS]
        + [pl.BlockSpec((tm, REST_W), lambda i: (i, 0))],
        out_shape=[jax.ShapeDtypeStruct((seq // d, d * D_MODEL), BF16) for d in DILATIONS]
        + [jax.ShapeDtypeStruct((seq // d, d * 3 * GROUP_W), BF16) for d in DILATIONS]
        + [jax.ShapeDtypeStruct((seq, REST_W), BF16)],
        scratch_shapes=[pltpu.VMEM((D_MODEL // LANES, tm, LANES), F32)],
        params=_params(("arbitrary",), 56), args=(x, g0, w_in, cos_t, sin_t), rider=rider)


def _band_masks():
    qi = lax.broadcasted_iota(jnp.int32, (QBLK, QBLK), 0)
    kj = lax.broadcasted_iota(jnp.int32, (QBLK, QBLK), 1)
    return kj <= qi, kj >= qi


def _attn_tile(length):
    return min(512, length)


def _attn_fwd(qkv, dil, rider=None):
    length = qkv.shape[0]
    tq = _attn_tile(length)
    nsub = tq // QBLK
    nblk = length // tq

    def body(q_ref, k_ref, v_ref, kp_ref, vp_ref, o_ref, l_ref):
        n = pl.program_id(1)
        mask_c, mask_p0 = _band_masks()
        hmask = _head_masks((QBLK, GROUP_W))
        zero = jnp.zeros((), BF16)
        for b in range(nsub):
            rows = slice(b * QBLK, (b + 1) * QBLK)
            q = q_ref[rows, :]
            kc, vc = k_ref[rows, :], v_ref[rows, :]
            if b == 0:
                kp, vp = kp_ref[...], vp_ref[...]
                mask_p = mask_p0 & (n > 0)
            else:
                prow = slice((b - 1) * QBLK, b * QBLK)
                kp, vp = k_ref[prow, :], v_ref[prow, :]
                mask_p = mask_p0
            o_acc = jnp.zeros((QBLK, GROUP_W), F32)
            l_acc = jnp.zeros((QBLK, GROUP_W), F32)
            for h in range(HEADS_PER_GROUP):
                hm = hmask[h]
                sc = jnp.where(mask_c, _dot_nt(q, jnp.where(hm, kc, zero)) * SCALE, NEG)
                sp = jnp.where(mask_p, _dot_nt(q, jnp.where(hm, kp, zero)) * SCALE, NEG)
                m = jnp.maximum(jnp.max(sc, axis=-1, keepdims=True), jnp.max(sp, axis=-1, keepdims=True))
                pc, pp = jnp.exp(sc - m), jnp.exp(sp - m)
                den = jnp.sum(pc, axis=-1, keepdims=True) + jnp.sum(pp, axis=-1, keepdims=True)
                pv = _dot(pc.astype(BF16), jnp.where(hm, vc, zero)) + _dot(pp.astype(BF16), jnp.where(hm, vp, zero))
                o_acc = o_acc + pv / den
                l_acc = l_acc + jnp.where(hm, m + jnp.log(den), 0.0)
            o_ref[rows, :] = o_acc
            l_ref[rows, :] = l_acc

    cur = lambda sec: pl.BlockSpec((tq, GROUP_W), lambda r, n: (n, r * 3 + sec))
    prev = lambda sec: pl.BlockSpec((QBLK, GROUP_W), lambda r, n: (jnp.maximum(n * nsub - 1, 0), r * 3 + sec))
    return _call(
        body, name=f"attn_fwd_d{dil}", grid=(dil, nblk),
        in_specs=[cur(0), cur(1), cur(2), prev(1), prev(2)],
        out_specs=[pl.BlockSpec((tq, GROUP_W), lambda r, n: (n, r))] * 2,
        out_shape=[jax.ShapeDtypeStruct((length, dil * GROUP_W), F32)] * 2, scratch_shapes=[],
        params=_params(("arbitrary", "arbitrary"), 32), args=(qkv, qkv, qkv, qkv, qkv), rider=rider)


def _attn_bwd(qkv, dy, y, lse, cos_t, sin_t, dil, rider=None):
    length = qkv.shape[0]
    tq = _attn_tile(length)
    nsub = tq // QBLK
    nblk = length // tq

    def body(q_ref, k_ref, v_ref, kp_ref, vp_ref, qn_ref, dy_ref, y_ref, l_ref, dyn_ref, yn_ref, ln_ref,
             cos_ref, sin_ref, out_ref, dq_s, dk_s, dv_s):
        n = pl.program_id(1)
        mask_c, mask_p0 = _band_masks()
        hmask = _head_masks((QBLK, GROUP_W))
        sub = lambda ref, b: ref[b * QBLK:(b + 1) * QBLK, :]
        kbd = [_head_stack(kp_ref[...], hmask)] + [_head_stack(sub(k_ref, b), hmask) for b in range(nsub)]
        vbd = [_head_stack(vp_ref[...], hmask)] + [_head_stack(sub(v_ref, b), hmask) for b in range(nsub)]
        dk_s[...] = jnp.zeros(dk_s.shape, F32)
        dv_s[...] = jnp.zeros(dv_s.shape, F32)

        def query_block(q, dyv, yv, lv, key_blocks):
            dyb = dyv.astype(BF16)
            qbd = _head_stack(q, hmask)
            dybd = jnp.concatenate([jnp.where(hm, dyv, 0.0).astype(BF16) for hm in hmask], axis=0)
            prod = dyv * yv
            deltas = [jnp.sum(jnp.where(hm, prod, 0.0), axis=-1, keepdims=True) for hm in hmask]
            lses = [jnp.max(jnp.where(hm, lv, NEG), axis=-1, keepdims=True) for hm in hmask]
            dq = jnp.zeros((QBLK, GROUP_W), F32)
            for kb, mask in key_blocks:
                s = _dot_nt(q, kbd[kb]) * SCALE
                dp = _dot_nt(dyb, vbd[kb])
                ps, dss = [], []
                for h in range(HEADS_PER_GROUP):
                    cols = slice(h * QBLK, (h + 1) * QBLK)
                    p = jnp.exp(jnp.where(mask, s[:, cols] - lses[h], NEG))
                    ps.append(p.astype(BF16))
                    dss.append((p * (dp[:, cols] - deltas[h])).astype(BF16))
                dq = dq + _dot(jnp.concatenate(dss, axis=1), kbd[kb])
                if kb >= 1:
                    krows = slice((kb - 1) * QBLK, kb * QBLK)
                    dv_s[krows, :] += _dot_tn(jnp.concatenate(ps, axis=0), dybd)
                    dk_s[krows, :] += _dot_tn(jnp.concatenate(dss, axis=0), qbd) * SCALE
            return dq * SCALE

        for b in range(nsub):
            mask_p = mask_p0 & (n > 0) if b == 0 else mask_p0
            dq_s[b * QBLK:(b + 1) * QBLK, :] = query_block(sub(q_ref, b), sub(dy_ref, b), sub(y_ref, b), sub(l_ref, b),
                                                            [(b, mask_p), (b + 1, mask_c)])
        query_block(qn_ref[...], dyn_ref[...], yn_ref[...], ln_ref[...], [(nsub, mask_p0 & (n < nblk - 1))])
        cos, sin = cos_ref[...], sin_ref[...]
        dq, dk = dq_s[...], dk_s[...]
        out_ref[:, 0:GROUP_W] = (dq * cos - _rot_half(dq) * sin).astype(BF16)
        out_ref[:, GROUP_W:2 * GROUP_W] = (dk * cos - _rot_half(dk) * sin).astype(BF16)
        out_ref[:, 2 * GROUP_W:3 * GROUP_W] = dv_s[...].astype(BF16)

    cur = lambda sec: pl.BlockSpec((tq, GROUP_W), lambda r, n: (n, r * 3 + sec))
    prev = lambda sec: pl.BlockSpec((QBLK, GROUP_W), lambda r, n: (jnp.maximum(n * nsub - 1, 0), r * 3 + sec))
    nxt_q = pl.BlockSpec((QBLK, GROUP_W), lambda r, n: (jnp.minimum((n + 1) * nsub, nblk * nsub - 1), r * 3))
    tok = pl.BlockSpec((tq, GROUP_W), lambda r, n: (n, r))
    tok_next = pl.BlockSpec((QBLK, GROUP_W), lambda r, n: (jnp.minimum((n + 1) * nsub, nblk * nsub - 1), r))
    (out,), riding = _call(
        body, name=f"attn_bwd_d{dil}", grid=(dil, nblk),
        in_specs=[cur(0), cur(1), cur(2), prev(1), prev(2), nxt_q,
                  tok, tok, tok, tok_next, tok_next, tok_next, tok, tok],
        out_specs=[pl.BlockSpec((tq, 3 * GROUP_W), lambda r, n: (n, r))],
        out_shape=[jax.ShapeDtypeStruct((length, dil * 3 * GROUP_W), BF16)],
        scratch_shapes=[pltpu.VMEM((tq, GROUP_W), F32)] * 3,
        params=_params(("arbitrary", "arbitrary"), 32),
        args=(qkv, qkv, qkv, qkv, qkv, qkv, dy, y, lse, dy, y, lse, cos_t, sin_t), rider=rider)
    return out, riding


def _layernorm_stats(z):
    mu = jnp.mean(z, axis=-1, keepdims=True)
    zc = z - mu
    rstd = lax.rsqrt(jnp.mean(zc * zc, axis=-1, keepdims=True) + EPS)
    return zc * rstd, rstd


def _tril_mask():
    row = lax.broadcasted_iota(jnp.int32, (CHUNK, CHUNK), 0)
    col = lax.broadcasted_iota(jnp.int32, (CHUNK, CHUNK), 1)
    return col <= row


def _mix_fwd(o_l, rest, x, w_sp, b_col, ln_g, ln_b, w_ba, w_bg, w_out, g1):
    seq = x.shape[0]
    tm = 256

    def body(o0, l0, o1, l1, o2, l2, up_ref, zp_ref, gap_ref, gbp_ref, x_ref, wsp_ref, bcol_ref, lg_ref, lb_ref,
             wba_ref, wbg_ref, wout_ref, g1_ref, ya0, lj0, ya1, lj1, ya2, lj2, yg_ref, mg_ref, y_ref, x1_ref, slab):
        outs = [_get_tokens(slab, o, d, GROUP_W, 0, GROUP_W) for o, d in zip((o0, o1, o2), DILATIONS)]
        lses = [_get_tokens(slab, l, d, GROUP_W, 0, GROUP_W) for l, d in zip((l0, l1, l2), DILATIONS)]
        m = jnp.maximum(jnp.maximum(lses[0], lses[1]), lses[2])
        es = [jnp.exp(l - m) for l in lses]
        tot = es[0] + es[1] + es[2]
        ya = (es[0] * outs[0] + es[1] * outs[1] + es[2] * outs[2]) / tot
        lj = m + jnp.log(tot)
        for ya_ref, lj_ref, d in zip((ya0, ya1, ya2), (lj0, lj1, lj2), DILATIONS):
            _put_residue(slab, ya, ya_ref, d, GROUP_W, 0)
            _put_residue(slab, lj, lj_ref, d, GROUP_W, 0)
        zhat, _ = _layernorm_stats(_gelu(zp_ref[...].astype(F32)))
        zln = (zhat * lg_ref[...] + lb_ref[...]).astype(BF16)
        u = _gelu(up_ref[...].astype(F32))
        tril = _tril_mask()
        for g in range(GMLP_GROUPS):
            wm = jnp.where(tril, wsp_ref[g], 0.0).astype(BF16)
            cols = slice(g * CHUNK, (g + 1) * CHUNK)
            for c in range(tm // CHUNK):
                rows = slice(c * CHUNK, (c + 1) * CHUNK)
                sz = _dot(wm, zln[rows, cols]) + bcol_ref[g]
                yg_ref[rows, cols] = (u[rows, cols] * sz).astype(BF16)
        a = _dot(ya.astype(BF16), wba_ref[...])
        bm = _dot(yg_ref[...], wbg_ref[...])
        merged = (jax.nn.sigmoid(gap_ref[...].astype(F32)) * a + jax.nn.sigmoid(gbp_ref[...].astype(F32)) * bm).astype(BF16)
        mg_ref[...] = merged
        yv = _dot(merged, wout_ref[...])
        y_ref[...] = yv
        x1_ref[...] = x_ref[...] + (yv * _rsqrt_ms(yv)) * g1_ref[...]

    tok = lambda w: pl.BlockSpec((tm, w), lambda i: (i, 0))
    res = lambda d: pl.BlockSpec((tm // d, d * GROUP_W), lambda i: (i, 0))
    full = lambda *s: pl.BlockSpec(s, lambda i: (0,) * len(s))
    res_specs = [res(d) for d in DILATIONS for _ in range(2)]
    return _pallas(
        body, name="mix_fwd", grid=(seq // tm,),
        in_specs=res_specs + [
            pl.BlockSpec((tm, GMLP_W), lambda i: (i, 0)), pl.BlockSpec((tm, GMLP_W), lambda i: (i, 1)),
            pl.BlockSpec((tm, D_MODEL), lambda i: (i, 1)), pl.BlockSpec((tm, D_MODEL), lambda i: (i, 2)),
            tok(D_MODEL), full(GMLP_GROUPS, CHUNK, CHUNK), full(GMLP_GROUPS, CHUNK, 1), full(1, GMLP_W), full(1, GMLP_W),
            full(GROUP_W, D_MODEL), full(GMLP_W, D_MODEL), full(D_MODEL, D_MODEL), full(1, D_MODEL)],
        out_specs=res_specs + [tok(GMLP_W), tok(D_MODEL), tok(D_MODEL), tok(D_MODEL)],
        out_shape=[jax.ShapeDtypeStruct((seq // d, d * GROUP_W), F32) for d in DILATIONS for _ in range(2)]
        + [jax.ShapeDtypeStruct((seq, GMLP_W), BF16), jax.ShapeDtypeStruct((seq, D_MODEL), BF16),
           jax.ShapeDtypeStruct((seq, D_MODEL), F32), jax.ShapeDtypeStruct((seq, D_MODEL), F32)],
        scratch_shapes=[pltpu.VMEM((GROUP_W // LANES, tm, LANES), F32)],
        compiler_params=_params(("arbitrary",), 48),
    )(*map(_in_hbm, (*o_l, rest, rest, rest, rest, x, w_sp, b_col, ln_g, ln_b, w_ba, w_bg, w_out, g1)))


def _mlp_fwd(x1, g2, g3, w_mi, w_mo, target):
    seq = x1.shape[0]
    tm, tf = MLP_TM, 512

    def body(x1_ref, g2_ref, g3_ref, wmi_ref, wmo_ref, t_ref, h2_ref, a_ref, dy2_ref, dout_ref, loss_ref, dg3_ref, sq_s):
        @pl.when(pl.program_id(0) == 0)
        def _():
            loss_ref[...] = jnp.zeros(loss_ref.shape, F32)
            dg3_ref[...] = jnp.zeros(dg3_ref.shape, F32)

        xv = x1_ref[...]
        hb = ((xv * _rsqrt_ms(xv)) * g2_ref[...]).astype(BF16)
        h2_ref[...] = hb
        for j in range(D_FF // tf):
            cols = slice(j * tf, (j + 1) * tf)
            a = jnp.maximum(_dot(hb, wmi_ref[:, cols]), 0.0)
            a_ref[:, cols] = a.astype(BF16)
            sq_s[:, cols] = (a * a).astype(BF16)
        y2 = _dot(sq_s[...], wmo_ref[...])
        r3 = _rsqrt_ms(y2)
        out = xv + (y2 * r3) * g3_ref[...]
        diff = out - t_ref[...]
        tile_loss = 0.5 * jnp.sum(jnp.mean(diff * diff, axis=-1, keepdims=True), axis=0, keepdims=True)
        loss_ref[...] += jnp.broadcast_to(tile_loss, loss_ref.shape)
        dout = diff * (1.0 / D_MODEL)
        dout_ref[...] = dout
        dy2, dg3 = _rmsnorm_bwd(dout, y2, g3_ref[...])
        dy2_ref[...] = dy2.astype(BF16)
        dg3_ref[...] += dg3

    tok = lambda w: pl.BlockSpec((tm, w), lambda i: (i, 0))
    vec = pl.BlockSpec((1, D_MODEL), lambda i: (0, 0))
    return _pallas(
        body, name="mlp_fwd", grid=(seq // tm,),
        in_specs=[tok(D_MODEL), vec, vec, _resident((D_MODEL, D_FF)), _resident((D_FF, D_MODEL)), tok(D_MODEL)],
        out_specs=[tok(D_MODEL), tok(D_FF), tok(D_MODEL), tok(D_MODEL), pl.BlockSpec((8, 128), lambda i: (0, 0)), vec],
        out_shape=[jax.ShapeDtypeStruct((seq, D_MODEL), BF16), jax.ShapeDtypeStruct((seq, D_FF), BF16),
                   jax.ShapeDtypeStruct((seq, D_MODEL), BF16), jax.ShapeDtypeStruct((seq, D_MODEL), F32),
                   jax.ShapeDtypeStruct((8, 128), F32), jax.ShapeDtypeStruct((1, D_MODEL), F32)],
        scratch_shapes=[pltpu.VMEM((tm, D_FF), BF16)],
        compiler_params=_params(("arbitrary",), 56),
    )(*map(_in_hbm, (x1, g2, g3, w_mi, w_mo, target)))


def _mlp_bwd(dy2, a, w_mo, w_mi, dout, x1, y, g2, g1, rider=None):
    seq = x1.shape[0]
    tm, tf = MLP_TM, 512

    def body(dy2_ref, a_ref, wmo_ref, wmi_ref, dout_ref, x1_ref, y_ref, g2_ref, g1_ref,
             dap_ref, dx1_ref, dy_ref, dg2_ref, dg1_ref):
        @pl.when(pl.program_id(0) == 0)
        def _():
            dg2_ref[...] = jnp.zeros(dg2_ref.shape, F32)
            dg1_ref[...] = jnp.zeros(dg1_ref.shape, F32)

        dy2v = dy2_ref[...]
        for j in range(D_FF // tf):
            cols = slice(j * tf, (j + 1) * tf)
            da2 = _dot_nt(dy2v, wmo_ref[cols, :])
            dap_ref[:, cols] = (da2 * (2.0 * a_ref[:, cols].astype(F32))).astype(BF16)
        dh2 = _dot_nt(dap_ref[...], wmi_ref[...])
        dres, dg2 = _rmsnorm_bwd(dh2, x1_ref[...], g2_ref[...])
        dx1 = dout_ref[...] + dres
        dx1_ref[...] = dx1
        dg2_ref[...] += dg2
        dyv, dg1 = _rmsnorm_bwd(dx1, y_ref[...], g1_ref[...])
        dy_ref[...] = dyv.astype(BF16)
        dg1_ref[...] += dg1

    tok = lambda w: pl.BlockSpec((tm, w), lambda i: (i, 0))
    vec = pl.BlockSpec((1, D_MODEL), lambda i: (0, 0))
    return _call(
        body, name="mlp_bwd", grid=(seq // tm,),
        in_specs=[tok(D_MODEL), tok(D_FF), _resident((D_FF, D_MODEL)), _resident((D_MODEL, D_FF)),
                  tok(D_MODEL), tok(D_MODEL), tok(D_MODEL), vec, vec],
        out_specs=[tok(D_FF), tok(D_MODEL), tok(D_MODEL), vec, vec],
        out_shape=[jax.ShapeDtypeStruct((seq, D_FF), BF16), jax.ShapeDtypeStruct((seq, D_MODEL), F32),
                   jax.ShapeDtypeStruct((seq, D_MODEL), BF16), jax.ShapeDtypeStruct((1, D_MODEL), F32),
                   jax.ShapeDtypeStruct((1, D_MODEL), F32)], scratch_shapes=[],
        params=_params(("arbitrary",), 56), args=(dy2, a, w_mo, w_mi, dout, x1, y, g2, g1), rider=rider)


def _tn_matmul(a, b, name, bm, bn, square_a=False, column_shards=False, rider=None):
    seq, m = a.shape
    n = b.shape[1]
    ts = 1024

    def body(a_ref, b_ref, o_ref):
        @pl.when(pl.program_id(2) == 0)
        def _():
            o_ref[...] = jnp.zeros(o_ref.shape, F32)

        av = a_ref[...]
        if square_a:
            af = av.astype(F32)
            av = (af * af).astype(BF16)
        o_ref[...] += _dot_tn(av, b_ref[...])

    if column_shards:
        out_spec = pl.BlockSpec((None, bm, bn), lambda mi, ni, s: (ni, mi, 0))
        out_shape = jax.ShapeDtypeStruct((n // bn, m, bn), F32)
    else:
        out_spec = pl.BlockSpec((bm, bn), lambda mi, ni, s: (mi, ni))
        out_shape = jax.ShapeDtypeStruct((m, n), F32)
    (out,), riding = _call(
        body, name=name, grid=(m // bm, n // bn, seq // ts),
        in_specs=[pl.BlockSpec((ts, bm), lambda mi, ni, s: (s, mi)), pl.BlockSpec((ts, bn), lambda mi, ni, s: (s, ni))],
        out_specs=[out_spec], out_shape=[out_shape], scratch_shapes=[],
        params=_params(("arbitrary", "arbitrary", "arbitrary"), 40), args=(a, b), rider=rider)
    return out, riding


def _tn_matmul_residue(a, b, dil, name):
    length = a.shape[0]
    m, n = a.shape[1] // dil, b.shape[1] // dil
    ts = min(1024, length)

    def body(a_ref, b_ref, o_ref):
        @pl.when((pl.program_id(0) == 0) & (pl.program_id(1) == 0))
        def _():
            o_ref[...] = jnp.zeros(o_ref.shape, F32)

        o_ref[...] += _dot_tn(a_ref[...], b_ref[...])

    return _pallas(
        body, name=name, grid=(dil, length // ts),
        in_specs=[pl.BlockSpec((ts, m), lambda r, s: (s, r)), pl.BlockSpec((ts, n), lambda r, s: (s, r))],
        out_specs=pl.BlockSpec((m, n), lambda r, s: (0, 0)),
        out_shape=jax.ShapeDtypeStruct((m, n), F32),
        compiler_params=_params(("arbitrary", "arbitrary"), 40),
    )(_in_hbm(a), _in_hbm(b))


def _mix_bwd(dy, ya, yg, mg, rest, w_out, w_ba, w_bg, w_sp, b_col, ln_g, ln_b, rider=None):
    seq = dy.shape[0]
    tm = 256

    def body(dy_ref, ya_ref, yg_ref, mg_ref, up_ref, zp_ref, gap_ref, gbp_ref, wout_ref, wba_ref, wbg_ref,
             wsp_ref, bcol_ref, lg_ref, lb_ref,
             dya0, dya1, dya2, dpr_ref, dwout_ref, dwba_ref, dwbg_ref, dwsp_ref, dbb_ref, dlg_ref, dlb_ref,
             dzln_s, du_s, slab):
        @pl.when(pl.program_id(0) == 0)
        def _():
            for ref in (dwout_ref, dwba_ref, dwbg_ref, dwsp_ref, dbb_ref, dlg_ref, dlb_ref):
                ref[...] = jnp.zeros(ref.shape, F32)

        dyv = dy_ref[...]
        dm = _dot_nt(dyv, wout_ref[...])
        dwout_ref[...] += _dot_tn(mg_ref[...], dyv)
        yab = ya_ref[...].astype(BF16)
        ygb = yg_ref[...]
        a = _dot(yab, wba_ref[...])
        bm = _dot(ygb, wbg_ref[...])
        ga = jax.nn.sigmoid(gap_ref[...].astype(F32))
        gb = jax.nn.sigmoid(gbp_ref[...].astype(F32))
        dpr_ref[:, 2 * GMLP_W:2 * GMLP_W + D_MODEL] = (dm * a * (ga * (1.0 - ga))).astype(BF16)
        dpr_ref[:, 2 * GMLP_W + D_MODEL:REST_W] = (dm * bm * (gb * (1.0 - gb))).astype(BF16)
        da = (dm * ga).astype(BF16)
        db = (dm * gb).astype(BF16)
        dwba = _dot_tn(yab, da)
        dwbg = _dot_tn(ygb, db)
        shard_w = D_MODEL // N_CHIPS
        for j in range(N_CHIPS):
            dwba_ref[j] += dwba[:, j * shard_w:(j + 1) * shard_w]
            dwbg_ref[j] += dwbg[:, j * shard_w:(j + 1) * shard_w]
        dya = _dot_nt(da, wba_ref[...])
        for dya_ref, d in zip((dya0, dya1, dya2), DILATIONS):
            _put_residue(slab, dya, dya_ref, d, GROUP_W, 0)
        dyg = _dot_nt(db, wbg_ref[...])

        zp = zp_ref[...].astype(F32)
        zhat, rstd = _layernorm_stats(_gelu(zp))
        lg = lg_ref[...]
        zln = (zhat * lg + lb_ref[...]).astype(BF16)
        up = up_ref[...].astype(F32)
        u = _gelu(up)
        tril = _tril_mask()
        for g in range(GMLP_GROUPS):
            wm = jnp.where(tril, wsp_ref[g], 0.0).astype(BF16)
            cols = slice(g * CHUNK, (g + 1) * CHUNK)
            for c in range(tm // CHUNK):
                rows = slice(c * CHUNK, (c + 1) * CHUNK)
                zb = zln[rows, cols]
                sz = _dot(wm, zb) + bcol_ref[g]
                dyg_cg = dyg[rows, cols]
                du_s[rows, cols] = dyg_cg * sz
                dsz = dyg_cg * u[rows, cols]
                dszb = dsz.astype(BF16)
                dbb_ref[g] += jnp.broadcast_to(jnp.sum(dsz, axis=-1, keepdims=True), (CHUNK, CHUNK))
                dwsp_ref[g] += jnp.where(tril, _dot_nt(dszb, zb), 0.0)
                dzln_s[rows, cols] = _dot_tn(wm, dszb)
        dzln = dzln_s[...]
        dlg_ref[...] += jnp.sum(dzln * zhat, axis=0, keepdims=True)
        dlb_ref[...] += jnp.sum(dzln, axis=0, keepdims=True)
        dzh = dzln * lg
        dz = rstd * (dzh - jnp.mean(dzh, axis=-1, keepdims=True) - zhat * jnp.mean(dzh * zhat, axis=-1, keepdims=True))
        dpr_ref[:, GMLP_W:2 * GMLP_W] = (dz * _gelu_grad(zp)).astype(BF16)
        dpr_ref[:, 0:GMLP_W] = (du_s[...] * _gelu_grad(up)).astype(BF16)

    tok = lambda w: pl.BlockSpec((tm, w), lambda i: (i, 0))
    full = lambda *s: pl.BlockSpec(s, lambda i: (0,) * len(s))
    return _call(
        body, name="mix_bwd", grid=(seq // tm,),
        in_specs=[tok(D_MODEL), tok(GROUP_W), tok(GMLP_W), tok(D_MODEL),
                  pl.BlockSpec((tm, GMLP_W), lambda i: (i, 0)), pl.BlockSpec((tm, GMLP_W), lambda i: (i, 1)),
                  pl.BlockSpec((tm, D_MODEL), lambda i: (i, 1)), pl.BlockSpec((tm, D_MODEL), lambda i: (i, 2)),
                  full(D_MODEL, D_MODEL), full(GROUP_W, D_MODEL), full(GMLP_W, D_MODEL),
                  full(GMLP_GROUPS, CHUNK, CHUNK), full(GMLP_GROUPS, CHUNK, 1), full(1, GMLP_W), full(1, GMLP_W)],
        out_specs=[pl.BlockSpec((tm // d, d * GROUP_W), lambda i: (i, 0)) for d in DILATIONS]
        + [tok(REST_W), full(D_MODEL, D_MODEL), full(N_CHIPS, GROUP_W, D_MODEL // N_CHIPS),
           full(N_CHIPS, GMLP_W, D_MODEL // N_CHIPS),
           full(GMLP_GROUPS, CHUNK, CHUNK), full(GMLP_GROUPS, CHUNK, CHUNK), full(1, GMLP_W), full(1, GMLP_W)],
        out_shape=[jax.ShapeDtypeStruct((seq // d, d * GROUP_W), F32) for d in DILATIONS]
        + [jax.ShapeDtypeStruct((seq, REST_W), BF16),
           jax.ShapeDtypeStruct((D_MODEL, D_MODEL), F32), jax.ShapeDtypeStruct((N_CHIPS, GROUP_W, D_MODEL // N_CHIPS), F32),
           jax.ShapeDtypeStruct((N_CHIPS, GMLP_W, D_MODEL // N_CHIPS), F32),
           jax.ShapeDtypeStruct((GMLP_GROUPS, CHUNK, CHUNK), F32),
           jax.ShapeDtypeStruct((GMLP_GROUPS, CHUNK, CHUNK), F32), jax.ShapeDtypeStruct((1, GMLP_W), F32),
           jax.ShapeDtypeStruct((1, GMLP_W), F32)],
        scratch_shapes=[pltpu.VMEM((tm, GMLP_W), F32), pltpu.VMEM((tm, GMLP_W), F32),
                        pltpu.VMEM((GROUP_W // LANES, tm, LANES), F32)],
        params=_params(("arbitrary",), 56),
        args=(dy, ya, yg, mg, rest, rest, rest, rest, w_out, w_ba, w_bg, w_sp, b_col, ln_g, ln_b), rider=rider)


IN_PROJ_BWD_TM = 256


def _in_proj_bwd(dqkv, drest, w_qkv, w_rest, x, dx1, g0, so_far, span, rider=None):
    seq = x.shape[0]
    tm = IN_PROJ_BWD_TM
    off, steps = span
    gx_so_far, dg_so_far = so_far

    def body(d0, d1, d2, dr_ref, w0, w1, w2, wr_ref, x_ref, dx1_ref, g_ref, dg_in_ref, gx_in_ref, gx_ref, dg_ref, slab):
        @pl.when(pl.program_id(0) == 0)
        def _():
            dg_ref[...] = dg_in_ref[...]

        dh = _dot(dr_ref[...], wr_ref[...])
        for d_ref, w_ref, dil in zip((d0, d1, d2), (w0, w1, w2), DILATIONS):
            piece = d_ref[...] if dil == 1 else _get_tokens(slab, d_ref, dil, 3 * GROUP_W, 0, 3 * GROUP_W).astype(BF16)
            dh = dh + _dot(piece, w_ref[...])
        dres, dg = _rmsnorm_bwd(dh, x_ref[...], g_ref[...])
        gx_ref[...] = dx1_ref[...] + dres
        dg_ref[...] += dg

    tok = lambda w: pl.BlockSpec((tm, w), lambda i: (i + off, 0))
    full = lambda *s: pl.BlockSpec(s, lambda i: (0,) * len(s))
    in_specs = ([pl.BlockSpec((tm // d, d * 3 * GROUP_W), lambda i: (i + off, 0)) for d in DILATIONS] + [tok(REST_W)]
                + [_resident((3 * GROUP_W, D_MODEL))] * 3 + [_resident((REST_W, D_MODEL))]
                + [tok(D_MODEL), tok(D_MODEL), full(1, D_MODEL), full(1, D_MODEL), HBM_SPEC])
    return _call(
        body, name=f"in_proj_bwd_{off}", grid=(steps,), in_specs=in_specs,
        out_specs=[tok(D_MODEL), full(1, D_MODEL)],
        out_shape=[jax.ShapeDtypeStruct((seq, D_MODEL), F32), jax.ShapeDtypeStruct((1, D_MODEL), F32)],
        scratch_shapes=[pltpu.VMEM((3 * GROUP_W // LANES, tm, LANES), F32)],
        params=_params(("arbitrary",), 48), args=(*dqkv, drest, *w_qkv, w_rest, x, dx1, g0, dg_so_far, gx_so_far),
        rider=rider, aliases={len(in_specs) - 1: 0})


def _adamw(w, g, m, v, name):
    rows, cols = w.shape
    tr = _row_tile(rows) if rows % 16 == 0 else rows
    c1 = 1.0 - ADAM_B1 ** ADAM_STEP
    c2 = 1.0 - ADAM_B2 ** ADAM_STEP

    def body(w_ref, g_ref, m_ref, v_ref, go_ref, d_ref, nm_ref, nv_ref):
        gv = g_ref[...]
        go_ref[...] = gv
        nm = ADAM_B1 * m_ref[...] + (1.0 - ADAM_B1) * gv
        nv = ADAM_B2 * v_ref[...] + (1.0 - ADAM_B2) * (gv * gv)
        d_ref[...] = -ADAM_LR * ((nm / c1) / (jnp.sqrt(nv / c2) + ADAM_EPS) + ADAM_WD * w_ref[...])
        nm_ref[...] = nm
        nv_ref[...] = nv

    spec = pl.BlockSpec((tr, cols), lambda i: (i, 0))
    return _pallas(
        body, name=name, grid=(rows // tr,),
        in_specs=[spec] * 4, out_specs=[spec] * 4,
        out_shape=[jax.ShapeDtypeStruct((rows, cols), F32)] * 4,
        compiler_params=_params(("arbitrary",), 40),
    )(*map(_in_hbm, (w, g, m, v)))


def _place():
    x, y, c = lax.axis_index("x"), lax.axis_index("y"), lax.axis_index("c")
    chips = [(1 - x, y), (x, 1 - y), (1 - x, 1 - y)]
    return x, y, c, chips


class _Exchange:
    def __init__(self, inputs, out_shapes, n_sems, start, finish, aliases=None):
        self.inputs, self.out_shapes, self.n_sems = list(inputs), list(out_shapes), n_sems
        self.start, self.finish, self.aliases = start, finish, dict(aliases or {})

    def scratch(self):
        return [pltpu.SemaphoreType.DMA((self.n_sems,)), pltpu.SemaphoreType.DMA((self.n_sems,))]


def _together(*parts):
    ins = [len(p.inputs) for p in parts]
    outs = [len(p.out_shapes) for p in parts]

    def split(refs, counts):
        pos, pieces = 0, []
        for cnt in counts:
            pieces.append(refs[pos:pos + cnt])
            pos += cnt
        return pieces

    def run(which):
        def go(in_refs, out_refs, *sems):
            for k, (p, i, o) in enumerate(zip(parts, split(in_refs, ins), split(out_refs, outs))):
                getattr(p, which)(i, o, sems[2 * k], sems[2 * k + 1])
        return go

    both = _Exchange([a for p in parts for a in p.inputs], [s for p in parts for s in p.out_shapes], 0, run("start"),
                     run("finish"))
    both.aliases = {sum(ins[:k]) + i: sum(outs[:k]) + o for k, p in enumerate(parts) for i, o in p.aliases.items()}
    both.scratch = lambda: [s for p in parts for s in p.scratch()]
    return both


def _run_exchange(ex, name):
    n_in, n_out = len(ex.inputs), len(ex.out_shapes)

    def body(*refs):
        ins, outs, sems = refs[:n_in], refs[n_in:n_in + n_out], refs[n_in + n_out:]
        ex.start(ins, outs, *sems)
        ex.finish(ins, outs, *sems)

    return _pallas(
        body, name=name, in_specs=[HBM_SPEC] * n_in, out_specs=[HBM_SPEC] * n_out, out_shape=ex.out_shapes,
        scratch_shapes=ex.scratch(), input_output_aliases=ex.aliases,
    )(*ex.inputs)


def _call(body, *, name, grid, in_specs, out_specs, out_shape, scratch_shapes, params, args, rider=None, aliases=None):
    in_specs, out_specs, out_shape, scratch_shapes = list(in_specs), list(out_specs), list(out_shape), list(scratch_shapes)
    aliases = dict(aliases or {})
    args = [_in_hbm(a) for a in args]
    if rider is None:
        outs = _pallas(body, name=name, grid=grid, in_specs=in_specs, out_specs=out_specs, out_shape=out_shape,
                              scratch_shapes=scratch_shapes, input_output_aliases=aliases, compiler_params=params)(*args)
        return list(outs), []
    n_in, n_out, n_scr = len(in_specs), len(out_specs), len(scratch_shapes)
    r_in, r_out = len(rider.inputs), len(rider.out_shapes)

    def wrapped(*refs):
        ins, r_ins = refs[:n_in], refs[n_in:n_in + r_in]
        pos = n_in + r_in
        outs, r_outs = refs[pos:pos + n_out], refs[pos + n_out:pos + n_out + r_out]
        pos += n_out + r_out
        scr, sems = refs[pos:pos + n_scr], refs[pos + n_scr:]
        ids = [pl.program_id(k) for k in range(len(grid))]
        first, last = ids[0] == 0, ids[0] == grid[0] - 1
        for k in range(1, len(grid)):
            first, last = first & (ids[k] == 0), last & (ids[k] == grid[k] - 1)

        @pl.when(first)
        def _():
            rider.start(r_ins, r_outs, *sems)

        body(*ins, *outs, *scr)

        @pl.when(last)
        def _():
            rider.finish(r_ins, r_outs, *sems)

    outs = _pallas(
        wrapped, name=name, grid=grid, in_specs=in_specs + [HBM_SPEC] * r_in, out_specs=out_specs + [HBM_SPEC] * r_out,
        out_shape=out_shape + rider.out_shapes, scratch_shapes=scratch_shapes + rider.scratch(),
        input_output_aliases={**aliases, **{n_in + i: n_out + o for i, o in rider.aliases.items()}}, compiler_params=params,
    )(*args, *rider.inputs)
    return list(outs[:n_out]), list(outs[n_out:])


def _stage_weights(shards):
    n = len(shards)

    def body(*refs):
        ins, outs, stages, sems = refs[:n], refs[n:2 * n], refs[2 * n:3 * n], refs[3 * n]
        x, y, _, _ = _place()
        copies = []
        for t in range(n):
            stages[t][...] = ins[t][...].astype(BF16)
            copies.append(pltpu.make_async_copy(stages[t], outs[t].at[2 * x + y], sems.at[t]))
            copies[-1].start()
        for cp in copies:
            cp.wait()

    stage_bytes = sum(s.size * 6 for s in shards)
    return _pallas(
        body, name="stage_weights", in_specs=[VMEM_SPEC] * n, out_specs=[HBM_SPEC] * n,
        out_shape=[jax.ShapeDtypeStruct((N_CHIPS,) + s.shape, BF16) for s in shards],
        scratch_shapes=[pltpu.VMEM(s.shape, BF16) for s in shards] + [pltpu.SemaphoreType.DMA((n,))],
        compiler_params=pltpu.CompilerParams(vmem_limit_bytes=stage_bytes + 8 * MIB),
    )(*shards)


def _gather(buffers, stage="both"):
    n = len(buffers)
    halves = [b.shape[1] // 2 for b in buffers]

    def half_of(outs, t, chip, which):
        return outs[t].at[chip, pl.ds(which * halves[t], halves[t]), :]

    def copy(outs, sems, t, k, chip, which, to):
        rows = half_of(outs, t, chip, which)
        return pltpu.make_async_remote_copy(src_ref=rows, dst_ref=rows, send_sem=sems[0].at[6 * t + k],
                                            recv_sem=sems[1].at[6 * t + k], device_id=to, device_id_type=MESH)

    def to_chips(outs, sems, what):
        x, y, c, chips = _place()
        for t in range(n):
            for j, (px, py) in enumerate(chips):
                if what == "start":
                    copy(outs, sems, t, j, 2 * x + y, c, (px, py, c)).start()
                else:
                    copy(outs, sems, t, j, 2 * px + py, c, (px, py, c)).wait_recv()
                    copy(outs, sems, t, j, 2 * x + y, c, (px, py, c)).wait_send()

    def to_sibling(outs, sems, what):
        x, y, c, chips = _place()
        for t in range(n):
            for j, (px, py) in enumerate(chips):
                if what == "start":
                    copy(outs, sems, t, 3 + j, 2 * px + py, c, (x, y, 1 - c)).start()
                else:
                    copy(outs, sems, t, 3 + j, 2 * px + py, 1 - c, (x, y, 1 - c)).wait_recv()
                    copy(outs, sems, t, 3 + j, 2 * px + py, c, (x, y, 1 - c)).wait_send()

    def start(ins, outs, *sems):
        (to_sibling if stage == "pair" else to_chips)(outs, sems, "start")

    def finish(ins, outs, *sems):
        if stage != "pair":
            to_chips(outs, sems, "finish")
        if stage == "both":
            to_sibling(outs, sems, "start")
        if stage != "chips":
            to_sibling(outs, sems, "finish")

    return _Exchange(buffers, [jax.ShapeDtypeStruct(b.shape, b.dtype) for b in buffers], 6 * n, start, finish,
                     aliases={t: t for t in range(n)})


def _pair_exchange(grads):
    n = len(grads)
    halves = [g.shape[1] // 2 for g in grads]

    def copies(ins, outs, send_sems, recv_sems):
        x, y, c, _ = _place()
        return [pltpu.make_async_remote_copy(
            src_ref=ins[t].at[:, pl.ds((1 - c) * halves[t], halves[t]), :], dst_ref=outs[t],
            send_sem=send_sems.at[t], recv_sem=recv_sems.at[t], device_id=(x, y, 1 - c), device_id_type=MESH)
            for t in range(n)]

    def start(*refs):
        for cp in copies(*refs):
            cp.start()

    def finish(*refs):
        for cp in copies(*refs):
            cp.wait()

    return _Exchange(grads, [jax.ShapeDtypeStruct((N_CHIPS, h, g.shape[2]), F32) for g, h in zip(grads, halves)], n,
                     start, finish)


def _row_tile(rows):
    return max(t for t in range(16, 257, 16) if rows % t == 0)


def _pair_add(grad, other, place, name):
    _, rows, cols = grad.shape
    rh = rows // 2
    tr = _row_tile(rh)
    nb = rh // tr

    def body(p_ref, g_ref, a_ref, wire_ref, own_ref):
        s = g_ref[...] + a_ref[...]
        wire_ref[...] = s.astype(BF16)

        @pl.when(pl.program_id(1) == p_ref[1])
        def _():
            own_ref[...] = s

    blk = (None, tr, cols)
    return _pallas(
        body, name=name,
        grid_spec=pltpu.PrefetchScalarGridSpec(
            num_scalar_prefetch=1, grid=(nb, N_CHIPS),
            in_specs=[pl.BlockSpec(blk, lambda i, j, p: (j, p[0] * nb + i, 0)), pl.BlockSpec(blk, lambda i, j, p: (j, i, 0))],
            out_specs=[pl.BlockSpec(blk, lambda i, j, p: (j, i, 0)), pl.BlockSpec((tr, cols), lambda i, j, p: (i, 0))]),
        out_shape=[jax.ShapeDtypeStruct((N_CHIPS, rh, cols), BF16), jax.ShapeDtypeStruct((rh, cols), F32)],
        compiler_params=_params(("arbitrary", "arbitrary"), 32),
    )(place, grad, other)


def _chip_exchange(wires):
    n = len(wires)

    def copies(ins, outs, send_sems, recv_sems):
        x, y, c, chips = _place()
        return [pltpu.make_async_remote_copy(
            src_ref=ins[t].at[2 * px + py], dst_ref=outs[t].at[j], send_sem=send_sems.at[3 * t + j],
            recv_sem=recv_sems.at[3 * t + j], device_id=(px, py, c), device_id_type=MESH)
            for t in range(n) for j, (px, py) in enumerate(chips)]

    def start(*refs):
        for cp in copies(*refs):
            cp.start()

    def finish(*refs):
        for cp in copies(*refs):
            cp.wait()

    return _Exchange(wires, [jax.ShapeDtypeStruct((3,) + w.shape[1:], BF16) for w in wires], 3 * n, start, finish)


def _chip_add(own, arrived, place, name):
    rh, cols = own.shape
    tr = _row_tile(rh)
    nb = rh // tr

    def body(p_ref, s_ref, b0, b1, b2, o_ref):
        o_ref[...] = ((s_ref[...] + b0[...].astype(F32)) + b1[...].astype(F32)) + b2[...].astype(F32)

    blk = (None, tr, cols)
    return _pallas(
        body, name=name,
        grid_spec=pltpu.PrefetchScalarGridSpec(
            num_scalar_prefetch=1, grid=(nb,),
            in_specs=[pl.BlockSpec((tr, cols), lambda i, p: (i, 0)), pl.BlockSpec(blk, lambda i, p: (0, i, 0)),
                      pl.BlockSpec(blk, lambda i, p: (1, i, 0)), pl.BlockSpec(blk, lambda i, p: (2, i, 0))],
            out_specs=pl.BlockSpec((tr, cols), lambda i, p: (p[0] * nb + i, 0))),
        out_shape=jax.ShapeDtypeStruct((2 * rh, cols), F32),
        compiler_params=_params(("arbitrary",), 32),
    )(place, own, arrived, arrived, arrived)


def _pair_share(halves):
    n = len(halves)
    rhs = [h.shape[0] // 2 for h in halves]

    def copy(outs, send_sems, recv_sems, t, which):
        x, y, c, _ = _place()
        rows = outs[t].at[pl.ds(which * rhs[t], rhs[t]), :]
        return pltpu.make_async_remote_copy(src_ref=rows, dst_ref=rows, send_sem=send_sems.at[t], recv_sem=recv_sems.at[t],
                                            device_id=(x, y, 1 - c), device_id_type=MESH)

    def start(ins, outs, send_sems, recv_sems):
        c = lax.axis_index("c")
        for t in range(n):
            copy(outs, send_sems, recv_sems, t, c).start()

    def finish(ins, outs, send_sems, recv_sems):
        c = lax.axis_index("c")
        for t in range(n):
            copy(outs, send_sems, recv_sems, t, c).wait_send()
            copy(outs, send_sems, recv_sems, t, 1 - c).wait_recv()

    return _Exchange(halves, [jax.ShapeDtypeStruct(h.shape, F32) for h in halves], n, start, finish,
                     aliases={t: t for t in range(n)})


class _GradReduction:
    def __init__(self, grads, place, tag):
        self.names, self.grads, self.place, self.tag = list(grads), grads, place, tag

    def pair_exchange(self):
        return _pair_exchange([self.grads[n] for n in self.names])

    def chip_exchange(self, others):
        sums = [_pair_add(self.grads[n], o, self.place, f"{self.tag}_pair_add_{n}") for n, o in zip(self.names, others)]
        self.owns = [own for _, own in sums]
        return _chip_exchange([wire for wire, _ in sums])

    def pair_share(self, arrived):
        return _pair_share([_chip_add(own, arr, self.place, f"{self.tag}_chip_add_{n}")
                            for n, own, arr in zip(self.names, self.owns, arrived)])

    def result(self, shared):
        return dict(zip(self.names, shared))


def _all_reduce_small(p):
    rows, lanes = p.shape
    flips = [(fx, fy, fc) for fx in (0, 1) for fy in (0, 1) for fc in (0, 1)][1:]

    def body(p_ref, o_ref, buf, send_sems, recv_sems):
        x, y, c, _ = _place()
        me = 4 * x + 2 * y + c
        buf[me] = p_ref[...]
        peers = [((1 - x) if fx else x, (1 - y) if fy else y, (1 - c) if fc else c) for fx, fy, fc in flips]
        cps = []
        for k, peer in enumerate(peers):
            cp = pltpu.make_async_remote_copy(
                src_ref=p_ref, dst_ref=buf.at[me], send_sem=send_sems.at[k], recv_sem=recv_sems.at[k],
                device_id=peer, device_id_type=MESH)
            cp.start()
            cps.append(cp)
        for k, (px, py, pc) in enumerate(peers):
            pltpu.make_async_remote_copy(
                src_ref=p_ref, dst_ref=buf.at[4 * px + 2 * py + pc], send_sem=send_sems.at[k], recv_sem=recv_sems.at[k],
                device_id=(px, py, pc), device_id_type=MESH).wait_recv()
        for cp in cps:
            cp.wait_send()
        acc = buf[0]
        for s in range(1, 8):
            acc = acc + buf[s]
        o_ref[...] = acc

    return _pallas(
        body, name="small_all_reduce", in_specs=[VMEM_SPEC], out_specs=VMEM_SPEC,
        out_shape=jax.ShapeDtypeStruct((rows, lanes), F32),
        scratch_shapes=[pltpu.VMEM((8, rows, lanes), F32), pltpu.SemaphoreType.DMA((7,)), pltpu.SemaphoreType.DMA((7,))],
        compiler_params=pltpu.CompilerParams(vmem_limit_bytes=32 * MIB),
    )(p)


BIG = ("w_in", "w_branch_attn", "w_branch_gmlp", "w_out", "w_mlp_in", "w_mlp_out")
COLUMN_SHARDED = ("w_branch_attn", "w_branch_gmlp", "w_mlp_in")
SMALL = ("norm_pre_mix", "w_spatial", "b_spatial", "ln_v_gain", "ln_v_bias", "norm_post_mix", "norm_pre_mlp", "norm_post_mlp")
ORDER = ("norm_pre_mix", "w_in", "w_spatial", "b_spatial", "ln_v_gain", "ln_v_bias", "w_branch_attn", "w_branch_gmlp",
         "w_out", "norm_post_mix", "norm_pre_mlp", "w_mlp_in", "w_mlp_out", "norm_post_mlp")


def _full_weight(name, gathered):
    if name in COLUMN_SHARDED:
        return jnp.transpose(gathered, (1, 0, 2)).reshape(gathered.shape[1], -1)
    return gathered.reshape(-1, gathered.shape[2])


def _rows8(a):
    a = a.reshape(-1, 128)
    pad = (-a.shape[0]) % 8
    return jnp.pad(a, ((0, pad), (0, 0))) if pad else a


def _qkv_columns(group):
    return [(sec * ATTN_W + group * GROUP_W, sec * ATTN_W + (group + 1) * GROUP_W) for sec in range(3)]


def _device_step(x, target, small, shards, place):
    seq = x.shape[0]
    g0, g1, g2, g3 = small["norm_pre_mix"], small["norm_post_mix"], small["norm_pre_mlp"], small["norm_post_mlp"]
    w_sp = small["w_spatial"]
    b_col = small["b_spatial"].reshape(GMLP_GROUPS, CHUNK, 1)
    ln_g, ln_b = small["ln_v_gain"], small["ln_v_bias"]

    staged = _stage_weights(shards)
    tables, (w_in,) = _rope_tables(seq, rider=_gather(staged[:1]))
    w_in = _full_weight("w_in", w_in)
    (*hq, rest), landed = _in_proj(x, g0, w_in, *tables[1], rider=_gather(staged[1:], "chips"))
    h, qkv = hq[:N_GROUPS], hq[N_GROUPS:]

    o_l, gathered = _attn_fwd(qkv[0], DILATIONS[0], rider=_gather(landed, "pair"))
    full = {n: _full_weight(n, gw) for n, gw in zip(BIG[1:], gathered)}
    for g in range(1, N_GROUPS):
        o_l.extend(_attn_fwd(qkv[g], DILATIONS[g])[0])
    *ya_l, yg, mg, y, x1 = _mix_fwd(o_l, rest, x, w_sp, b_col, ln_g, ln_b, full["w_branch_attn"], full["w_branch_gmlp"],
                                    full["w_out"], g1)
    ya, lse = ya_l[0::2], ya_l[1::2]
    h2, a, dy2, dout, loss8, dg3 = _mlp_fwd(x1, g2, g3, full["w_mlp_in"], full["w_mlp_out"], target)
    d_wmo, _ = _tn_matmul(a, dy2, "grad_w_mlp_out", 1024, 1024, square_a=True)
    mlp_out = _GradReduction({"w_mlp_out": d_wmo.reshape(N_CHIPS, D_FF // N_CHIPS, D_MODEL)}, place, "mlp_out")
    (dap, dx1, dy, dg2, dg1), riding = _mlp_bwd(dy2, a, full["w_mlp_out"], full["w_mlp_in"], dout, x1, y, g2, g1,
                                                 rider=mlp_out.pair_exchange())
    d_wmi, riding = _tn_matmul(h2, dap, "grad_w_mlp_in", 1024, 1024, column_shards=True,
                               rider=mlp_out.chip_exchange(riding))
    mlp_in = _GradReduction({"w_mlp_in": d_wmi}, place, "mlp_in")
    (*dya, drest, d_wout, d_wba, d_wbg, d_wsp, d_bb, d_lg, d_lb), riding = _mix_bwd(
        dy, ya[0], yg, mg, rest, full["w_out"], full["w_branch_attn"], full["w_branch_gmlp"], w_sp, b_col, ln_g, ln_b,
        rider=_together(mlp_out.pair_share(riding), mlp_in.pair_exchange()))
    reduced = mlp_out.result(riding[:1])
    mix = _GradReduction({"w_branch_attn": d_wba, "w_branch_gmlp": d_wbg,
                          "w_out": d_wout.reshape(N_CHIPS, D_MODEL // N_CHIPS, D_MODEL)}, place, "mix")
    attn = lambda g, rider: _attn_bwd(qkv[g], dya[g], ya[g], lse[g], *tables[DILATIONS[g]], DILATIONS[g], rider=rider)
    dqkv0, riding = attn(0, _together(mlp_in.chip_exchange(riding[1:]), mix.pair_exchange()))
    dqkv1, riding = attn(1, _together(mlp_in.pair_share(riding[:1]), mix.chip_exchange(riding[1:])))
    reduced.update(mlp_in.result(riding[:1]))
    dqkv2, riding = attn(2, mix.pair_share(riding[1:]))
    reduced.update(mix.result(riding))
    dqkv = [dqkv0, dqkv1, dqkv2]

    d_qkv = [_tn_matmul_residue(dqkv[g], h[g], dil, f"grad_w_in_qkv{g}") for g, dil in enumerate(DILATIONS)]
    d_rest, _ = _tn_matmul(drest, h[0], "grad_w_in_rest", 1024, 1024)
    d_win = jnp.concatenate([d_qkv[g][s * GROUP_W:(s + 1) * GROUP_W] for s in range(3) for g in range(N_GROUPS)]
                            + [d_rest], axis=0)
    first = _GradReduction({"w_in": d_win.reshape(N_CHIPS, IN_W // N_CHIPS, D_MODEL)}, place, "w_in")
    w_qkv = [jnp.concatenate([w_in[lo:hi] for lo, hi in _qkv_columns(g)], axis=0) for g in range(N_GROUPS)]
    w_rest = w_in[QKV_W:]
    tiles = seq // IN_PROJ_BWD_TM
    so_far = (lax.empty((seq, D_MODEL), F32), jnp.zeros((1, D_MODEL), F32))
    in_bwd = lambda so_far, span, rider: _in_proj_bwd(dqkv, drest, w_qkv, w_rest, x, dx1, g0, so_far, span, rider=rider)
    so_far, riding = in_bwd(so_far, (0, tiles // 4), first.pair_exchange())
    so_far, riding = in_bwd(so_far, (tiles // 4, tiles // 2), first.chip_exchange(riding))
    shared = _run_exchange(first.pair_share(riding), "w_in_pair_share")
    (grad_x, dg0), _ = in_bwd(so_far, (3 * tiles // 4, tiles // 4), None)
    reduced.update(first.result(shared))
    little = {"norm_pre_mix": dg0, "w_spatial": d_wsp, "b_spatial": d_bb[:, :, 0], "ln_v_gain": d_lg, "ln_v_bias": d_lb,
              "norm_post_mix": dg1, "norm_pre_mlp": dg2, "norm_post_mlp": dg3}
    return loss8[0, 0], grad_x, reduced, little


def kernel(x, norm_pre_mix, w_in, w_spatial, b_spatial, ln_v_gain, ln_v_bias, w_branch_attn, w_branch_gmlp, w_out, norm_post_mix, norm_pre_mlp, w_mlp_in, w_mlp_out, norm_post_mlp, loss_target, m_norm_pre_mix, m_w_in, m_w_spatial, m_b_spatial, m_ln_v_gain, m_ln_v_bias, m_w_branch_attn, m_w_branch_gmlp, m_w_out, m_norm_post_mix, m_norm_pre_mlp, m_w_mlp_in, m_w_mlp_out, m_norm_post_mlp, v_norm_pre_mix, v_w_in, v_w_spatial, v_b_spatial, v_ln_v_gain, v_ln_v_bias, v_w_branch_attn, v_w_branch_gmlp, v_w_out, v_norm_post_mix, v_norm_pre_mlp, v_w_mlp_in, v_w_mlp_out, v_norm_post_mlp):
    given = dict(norm_pre_mix=norm_pre_mix, w_in=w_in, w_spatial=w_spatial, b_spatial=b_spatial, ln_v_gain=ln_v_gain,
                 ln_v_bias=ln_v_bias, w_branch_attn=w_branch_attn, w_branch_gmlp=w_branch_gmlp, w_out=w_out,
                 norm_post_mix=norm_post_mix, norm_pre_mlp=norm_pre_mlp, w_mlp_in=w_mlp_in, w_mlp_out=w_mlp_out,
                 norm_post_mlp=norm_post_mlp)
    moments_m = dict(norm_pre_mix=m_norm_pre_mix, w_in=m_w_in, w_spatial=m_w_spatial, b_spatial=m_b_spatial,
                     ln_v_gain=m_ln_v_gain, ln_v_bias=m_ln_v_bias, w_branch_attn=m_w_branch_attn,
                     w_branch_gmlp=m_w_branch_gmlp, w_out=m_w_out, norm_post_mix=m_norm_post_mix,
                     norm_pre_mlp=m_norm_pre_mlp, w_mlp_in=m_w_mlp_in, w_mlp_out=m_w_mlp_out, norm_post_mlp=m_norm_post_mlp)
    moments_v = dict(norm_pre_mix=v_norm_pre_mix, w_in=v_w_in, w_spatial=v_w_spatial, b_spatial=v_b_spatial,
                     ln_v_gain=v_ln_v_gain, ln_v_bias=v_ln_v_bias, w_branch_attn=v_w_branch_attn,
                     w_branch_gmlp=v_w_branch_gmlp, w_out=v_w_out, norm_post_mix=v_norm_post_mix,
                     norm_pre_mlp=v_norm_pre_mlp, w_mlp_in=v_w_mlp_in, w_mlp_out=v_w_mlp_out, norm_post_mlp=v_norm_post_mlp)
    cx, cy, cc = lax.axis_index("x"), lax.axis_index("y"), lax.axis_index("c")

    shards = [given[n][0].T if n == "w_in" else given[n][0] for n in BIG]
    small = {n: given[n][0] if given[n].ndim > 2 else given[n] for n in SMALL}
    place = jnp.stack([cc, 2 * cx + cy]).astype(jnp.int32)
    loss, grad_x, grad_shard, grads = _device_step(x[0], loss_target[0], small, shards, place)
    loss = lax.psum(loss, ("x", "y", "c"))

    packed = jnp.concatenate([_rows8(grads[n]) for n in SMALL], axis=0)
    summed = _all_reduce_small(packed)
    row = 0
    for n in SMALL:
        shape = given[n][0].shape
        cnt = -(-(given[n][0].size // 128) // 8) * 8
        grad_shard[n] = summed[row:row + given[n][0].size // 128].reshape(shape)
        row += cnt

    grad_out, deltas, new_m, new_v = {}, {}, {}, {}
    for n in ORDER:
        shape = given[n].shape
        if n == "w_in":
            outs = _adamw(given[n][0].T, grad_shard[n], moments_m[n][0].T, moments_v[n][0].T, "adamw_" + n)
            outs = [o.T for o in outs]
        else:
            two_d = (-1, shape[-1])
            outs = _adamw(given[n].reshape(two_d), grad_shard[n].reshape(two_d), moments_m[n].reshape(two_d),
                          moments_v[n].reshape(two_d), "adamw_" + n)
        grad_out[n], deltas[n], new_m[n], new_v[n] = [o.reshape(shape) for o in outs]
    return (loss, grad_x[None], *[grad_out[n] for n in ORDER], *[deltas[n] for n in ORDER], *[new_m[n] for n in ORDER],
            *[new_v[n] for n in ORDER])
```

```python
import math

import jax
import jax.numpy as jnp
from jax import lax
from jax.experimental import pallas as pl
from jax.experimental.pallas import tpu as pltpu

F32 = jnp.float32
BF16 = jnp.bfloat16
MESH = pl.DeviceIdType.MESH

D_MODEL = 1024
HEAD_DIM = 64
HEADS_PER_GROUP = 4
GROUP_W = HEADS_PER_GROUP * HEAD_DIM
DILATIONS = (1, 4, 16)
N_GROUPS = len(DILATIONS)
ATTN_W = N_GROUPS * GROUP_W
QKV_W = 3 * ATTN_W
GMLP_W = 512
GMLP_GROUPS = 4
CHUNK = 128
REST_W = 2 * GMLP_W + 2 * D_MODEL
IN_W = QKV_W + REST_W
D_FF = 4096
QBLK = 128
ROPE_THETA = 10000.0
EPS = 1e-6
NEG = -1e30
SCALE = HEAD_DIM ** -0.5
N_CHIPS = 4

ADAM_LR = 0.001
ADAM_B1 = 0.9
ADAM_B2 = 0.999
ADAM_EPS = 1e-08
ADAM_WD = 0.01
ADAM_STEP = 10

MIB = 1024 * 1024
HBM_SPEC = pl.BlockSpec(memory_space=pltpu.HBM)
VMEM_SPEC = pl.BlockSpec(memory_space=pltpu.VMEM)


MLP_TM = 256


def _params(semantics, vmem_mib):
    return pltpu.CompilerParams(dimension_semantics=semantics, vmem_limit_bytes=vmem_mib * MIB)


def _in_hbm(a):
    return pltpu.with_memory_space_constraint(a, pltpu.HBM) if a.size * a.dtype.itemsize >= MIB else a


def _pallas(body, **kwargs):
    return pl.pallas_call(body, **kwargs)


def _resident(shape):
    return pl.BlockSpec(shape, lambda *_: (0,) * len(shape), pipeline_mode=pl.Buffered(1))


def _dot(a, b):
    return jnp.dot(a, b, preferred_element_type=F32)


def _dot_nt(a, b):
    return lax.dot_general(a, b, (((1,), (1,)), ((), ())), preferred_element_type=F32)


def _dot_tn(a, b):
    return lax.dot_general(a, b, (((0,), (0,)), ((), ())), preferred_element_type=F32)


_GELU_C = math.sqrt(2.0 / math.pi)


def _gelu(x):
    return x * (0.5 * (1.0 + jnp.tanh(_GELU_C * (x + 0.044715 * (x * x * x)))))


def _gelu_grad(x):
    t = jnp.tanh(_GELU_C * (x + 0.044715 * (x * x * x)))
    return 0.5 * (1.0 + t) + 0.5 * x * (1.0 - t * t) * (_GELU_C * (1.0 + 3.0 * 0.044715 * (x * x)))


def _rsqrt_ms(v):
    return lax.rsqrt(jnp.mean(v * v, axis=-1, keepdims=True) + EPS)


def _rmsnorm_bwd(dn, src, gain):
    r = _rsqrt_ms(src)
    t = gain * dn
    dgain = jnp.sum(dn * (src * r), axis=0, keepdims=True)
    dsrc = r * t - src * ((r * r * r) * jnp.mean(t * src, axis=-1, keepdims=True))
    return dsrc, dgain


def _rot_half(v):
    w = v.shape[-1]
    lane = lax.broadcasted_iota(jnp.int32, v.shape, v.ndim - 1)
    return jnp.where((lane % HEAD_DIM) < HEAD_DIM // 2, pltpu.roll(v, w - HEAD_DIM // 2, v.ndim - 1),
                     pltpu.roll(v, HEAD_DIM // 2, v.ndim - 1))


def _head_masks(shape):
    lane = lax.broadcasted_iota(jnp.int32, shape, 1)
    return [(lane >= h * HEAD_DIM) & (lane < (h + 1) * HEAD_DIM) for h in range(HEADS_PER_GROUP)]


def _head_stack(block, hmask):
    zero = jnp.zeros((), block.dtype)
    return jnp.concatenate([jnp.where(hm, block, zero) for hm in hmask], axis=0)


LANES = 128


def _put_residue(slab, val, out_ref, dil, width, col0):
    tm, w = val.shape
    if dil == 1:
        out_ref[:, col0:col0 + w] = val.astype(out_ref.dtype)
        return
    for k in range(w // LANES):
        slab[k] = val[:, k * LANES:(k + 1) * LANES]
    for r in range(dil):
        for k in range(w // LANES):
            c = r * width + col0 + k * LANES
            out_ref[:, c:c + LANES] = slab[k, pl.ds(r, tm // dil, stride=dil), :].astype(out_ref.dtype)


def _get_tokens(slab, in_ref, dil, width, col0, w):
    if dil == 1:
        return in_ref[:, col0:col0 + w].astype(F32)
    rows = in_ref.shape[0]
    for r in range(dil):
        for k in range(w // LANES):
            c = r * width + col0 + k * LANES
            slab[k, pl.ds(r, rows, stride=dil), :] = in_ref[:, c:c + LANES].astype(F32)
    return jnp.concatenate([slab[k] for k in range(w // LANES)], axis=1)


def _rope_tables(seq, rider=None):
    half = HEAD_DIM // 2
    inv_freq = ROPE_THETA ** (-jnp.arange(half, dtype=F32) / half)
    freq = jnp.tile(inv_freq, LANES // half).reshape(1, LANES)
    tm = 512

    def body(f_ref, *refs):
        outs, slab_c, slab_s = refs[:-2], refs[-2], refs[-1]
        row = lax.broadcasted_iota(jnp.int32, (tm, LANES), 0) + pl.program_id(0) * tm
        lane = lax.broadcasted_iota(jnp.int32, (tm, LANES), 1)
        ang = row.astype(F32) * f_ref[...]
        cos = jnp.cos(ang)
        sin = jnp.where((lane % HEAD_DIM) < half, -jnp.sin(ang), jnp.sin(ang))
        slab_c[0] = cos
        slab_s[0] = sin
        for i, dil in enumerate(DILATIONS):
            for tab, slab in ((outs[2 * i], slab_c), (outs[2 * i + 1], slab_s)):
                for r in range(dil):
                    piece = slab[0, pl.ds(r, tm // dil, stride=dil), :] if dil > 1 else slab[0]
                    for k in range(GROUP_W // LANES):
                        tab[:, r * GROUP_W + k * LANES:r * GROUP_W + (k + 1) * LANES] = piece

    outs, riding = _call(
        body, name="rope_tables", grid=(seq // tm,),
        in_specs=[pl.BlockSpec((1, LANES), lambda i: (0, 0))],
        out_specs=[pl.BlockSpec((tm // d, d * GROUP_W), lambda i: (i, 0)) for d in DILATIONS for _ in range(2)],
        out_shape=[jax.ShapeDtypeStruct((seq // d, d * GROUP_W), F32) for d in DILATIONS for _ in range(2)],
        scratch_shapes=[pltpu.VMEM((1, tm, LANES), F32)] * 2,
        params=_params(("arbitrary",), 32), args=(freq,), rider=rider)
    return {d: (outs[2 * i], outs[2 * i + 1]) for i, d in enumerate(DILATIONS)}, riding


def _norm_in(x, g0, rider=None):
    seq = x.shape[0]
    tm = 256

    def body(x_ref, g_ref, *refs):
        h_refs, slab = refs[:N_GROUPS], refs[-1]
        xv = x_ref[...]
        hf = (xv * _rsqrt_ms(xv)) * g_ref[...]
        for g, dil in enumerate(DILATIONS):
            _put_residue(slab, hf, h_refs[g], dil, D_MODEL, 0)

    return _call(
        body, name="norm_in", grid=(seq // tm,),
        in_specs=[pl.BlockSpec((tm, D_MODEL), lambda i: (i, 0)), pl.BlockSpec((1, D_MODEL), lambda i: (0, 0))],
        out_specs=[pl.BlockSpec((tm // d, d * D_MODEL), lambda i: (i, 0)) for d in DILATIONS],
        out_shape=[jax.ShapeDtypeStruct((seq // d, d * D_MODEL), BF16) for d in DILATIONS],
        scratch_shapes=[pltpu.VMEM((D_MODEL // LANES, tm, LANES), F32)],
        params=_params(("arbitrary",), 32), args=(x, g0), rider=rider)


def _in_proj(h, w_in, cos_t, sin_t, rider=None):
    seq = h.shape[0]
    tm, tn = 512, GROUP_W
    n_qk = 2 * ATTN_W // tn
    n_qkv = QKV_W // tn

    def body(h_ref, w_ref, cos_ref, sin_ref, *refs):
        qkv_refs, rest_ref, slab = refs[:N_GROUPS], refs[N_GROUPS], refs[-1]
        hb = h_ref[...]
        cos, sin = cos_ref[...], sin_ref[...]
        for j in range(IN_W // tn):
            p = _dot_nt(hb, w_ref[j * tn:(j + 1) * tn, :])
            if j < n_qkv:
                if j < n_qk:
                    p = p * cos + _rot_half(p) * sin
                section, g = divmod(j, N_GROUPS)
                _put_residue(slab, p, qkv_refs[g], DILATIONS[g], 3 * GROUP_W, section * GROUP_W)
            else:
                rest_ref[:, (j - n_qkv) * tn:(j - n_qkv + 1) * tn] = p.astype(BF16)

    return _call(
        body, name="in_proj", grid=(seq // tm,),
        in_specs=[pl.BlockSpec((tm, D_MODEL), lambda i: (i, 0)),
                  _resident((IN_W, D_MODEL)),
                  pl.BlockSpec((tm, GROUP_W), lambda i: (i, 0)),
                  pl.BlockSpec((tm, GROUP_W), lambda i: (i, 0))],
        out_specs=[pl.BlockSpec((tm // d, d * 3 * GROUP_W), lambda i: (i, 0)) for d in DILATIONS]
        + [pl.BlockSpec((tm, REST_W), lambda i: (i, 0))],
        out_shape=[jax.ShapeDtypeStruct((seq // d, d * 3 * GROUP_W), BF16) for d in DILATIONS]
        + [jax.ShapeDtypeStruct((seq, REST_W), BF16)],
        scratch_shapes=[pltpu.VMEM((GROUP_W // LANES, tm, LANES), F32)],
        params=_params(("arbitrary",), 48), args=(h, w_in, cos_t, sin_t), rider=rider)


def _band_masks():
    qi = lax.broadcasted_iota(jnp.int32, (QBLK, QBLK), 0)
    kj = lax.broadcasted_iota(jnp.int32, (QBLK, QBLK), 1)
    return kj <= qi, kj >= qi


def _attn_tile(length):
    return min(512, length)


def _attn_fwd(qkv, dil, rider=None):
    length = qkv.shape[0]
    tq = _attn_tile(length)
    nsub = tq // QBLK
    nblk = length // tq

    def body(q_ref, k_ref, v_ref, kp_ref, vp_ref, o_ref, l_ref):
        n = pl.program_id(1)
        mask_c, mask_p0 = _band_masks()
        hmask = _head_masks((QBLK, GROUP_W))
        zero = jnp.zeros((), BF16)
        for b in range(nsub):
            rows = slice(b * QBLK, (b + 1) * QBLK)
            q = q_ref[rows, :]
            kc, vc = k_ref[rows, :], v_ref[rows, :]
            if b == 0:
                kp, vp = kp_ref[...], vp_ref[...]
                mask_p = mask_p0 & (n > 0)
            else:
                prow = slice((b - 1) * QBLK, b * QBLK)
                kp, vp = k_ref[prow, :], v_ref[prow, :]
                mask_p = mask_p0
            o_acc = jnp.zeros((QBLK, GROUP_W), F32)
            l_acc = jnp.zeros((QBLK, GROUP_W), F32)
            for h in range(HEADS_PER_GROUP):
                hm = hmask[h]
                sc = jnp.where(mask_c, _dot_nt(q, jnp.where(hm, kc, zero)) * SCALE, NEG)
                sp = jnp.where(mask_p, _dot_nt(q, jnp.where(hm, kp, zero)) * SCALE, NEG)
                m = jnp.maximum(jnp.max(sc, axis=-1, keepdims=True), jnp.max(sp, axis=-1, keepdims=True))
                pc, pp = jnp.exp(sc - m), jnp.exp(sp - m)
                den = jnp.sum(pc, axis=-1, keepdims=True) + jnp.sum(pp, axis=-1, keepdims=True)
                pv = _dot(pc.astype(BF16), jnp.where(hm, vc, zero)) + _dot(pp.astype(BF16), jnp.where(hm, vp, zero))
                o_acc = o_acc + pv / den
                l_acc = l_acc + jnp.where(hm, m + jnp.log(den), 0.0)
            o_ref[rows, :] = o_acc
            l_ref[rows, :] = l_acc

    cur = lambda sec: pl.BlockSpec((tq, GROUP_W), lambda r, n: (n, r * 3 + sec))
    prev = lambda sec: pl.BlockSpec((QBLK, GROUP_W), lambda r, n: (jnp.maximum(n * nsub - 1, 0), r * 3 + sec))
    return _call(
        body, name=f"attn_fwd_d{dil}", grid=(dil, nblk),
        in_specs=[cur(0), cur(1), cur(2), prev(1), prev(2)],
        out_specs=[pl.BlockSpec((tq, GROUP_W), lambda r, n: (n, r))] * 2,
        out_shape=[jax.ShapeDtypeStruct((length, dil * GROUP_W), F32)] * 2, scratch_shapes=[],
        params=_params(("arbitrary", "arbitrary"), 32), args=(qkv, qkv, qkv, qkv, qkv), rider=rider)


def _attn_bwd(qkv, dy, y, lse, cos_t, sin_t, dil, rider=None):
    length = qkv.shape[0]
    tq = _attn_tile(length)
    nsub = tq // QBLK
    nblk = length // tq

    def body(q_ref, k_ref, v_ref, kp_ref, vp_ref, qn_ref, dy_ref, y_ref, l_ref, dyn_ref, yn_ref, ln_ref,
             cos_ref, sin_ref, out_ref, dq_s, dk_s, dv_s):
        n = pl.program_id(1)
        mask_c, mask_p0 = _band_masks()
        hmask = _head_masks((QBLK, GROUP_W))
        sub = lambda ref, b: ref[b * QBLK:(b + 1) * QBLK, :]
        kbd = [_head_stack(kp_ref[...], hmask)] + [_head_stack(sub(k_ref, b), hmask) for b in range(nsub)]
        vbd = [_head_stack(vp_ref[...], hmask)] + [_head_stack(sub(v_ref, b), hmask) for b in range(nsub)]
        dk_s[...] = jnp.zeros(dk_s.shape, F32)
        dv_s[...] = jnp.zeros(dv_s.shape, F32)

        def query_block(q, dyv, yv, lv, key_blocks):
            dyb = dyv.astype(BF16)
            qbd = _head_stack(q, hmask)
            dybd = jnp.concatenate([jnp.where(hm, dyv, 0.0).astype(BF16) for hm in hmask], axis=0)
            prod = dyv * yv
            deltas = [jnp.sum(jnp.where(hm, prod, 0.0), axis=-1, keepdims=True) for hm in hmask]
            lses = [jnp.max(jnp.where(hm, lv, NEG), axis=-1, keepdims=True) for hm in hmask]
            dq = jnp.zeros((QBLK, GROUP_W), F32)
            for kb, mask in key_blocks:
                s = _dot_nt(q, kbd[kb]) * SCALE
                dp = _dot_nt(dyb, vbd[kb])
                ps, dss = [], []
                for h in range(HEADS_PER_GROUP):
                    cols = slice(h * QBLK, (h + 1) * QBLK)
                    p = jnp.exp(jnp.where(mask, s[:, cols] - lses[h], NEG))
                    ps.append(p.astype(BF16))
                    dss.append((p * (dp[:, cols] - deltas[h])).astype(BF16))
                dq = dq + _dot(jnp.concatenate(dss, axis=1), kbd[kb])
                if kb >= 1:
                    krows = slice((kb - 1) * QBLK, kb * QBLK)
                    dv_s[krows, :] += _dot_tn(jnp.concatenate(ps, axis=0), dybd)
                    dk_s[krows, :] += _dot_tn(jnp.concatenate(dss, axis=0), qbd) * SCALE
            return dq * SCALE

        for b in range(nsub):
            mask_p = mask_p0 & (n > 0) if b == 0 else mask_p0
            dq_s[b * QBLK:(b + 1) * QBLK, :] = query_block(sub(q_ref, b), sub(dy_ref, b), sub(y_ref, b), sub(l_ref, b),
                                                            [(b, mask_p), (b + 1, mask_c)])
        query_block(qn_ref[...], dyn_ref[...], yn_ref[...], ln_ref[...], [(nsub, mask_p0 & (n < nblk - 1))])
        cos, sin = cos_ref[...], sin_ref[...]
        dq, dk = dq_s[...], dk_s[...]
        out_ref[:, 0:GROUP_W] = (dq * cos - _rot_half(dq) * sin).astype(BF16)
        out_ref[:, GROUP_W:2 * GROUP_W] = (dk * cos - _rot_half(dk) * sin).astype(BF16)
        out_ref[:, 2 * GROUP_W:3 * GROUP_W] = dv_s[...].astype(BF16)

    cur = lambda sec: pl.BlockSpec((tq, GROUP_W), lambda r, n: (n, r * 3 + sec))
    prev = lambda sec: pl.BlockSpec((QBLK, GROUP_W), lambda r, n: (jnp.maximum(n * nsub - 1, 0), r * 3 + sec))
    nxt_q = pl.BlockSpec((QBLK, GROUP_W), lambda r, n: (jnp.minimum((n + 1) * nsub, nblk * nsub - 1), r * 3))
    tok = pl.BlockSpec((tq, GROUP_W), lambda r, n: (n, r))
    tok_next = pl.BlockSpec((QBLK, GROUP_W), lambda r, n: (jnp.minimum((n + 1) * nsub, nblk * nsub - 1), r))
    (out,), riding = _call(
        body, name=f"attn_bwd_d{dil}", grid=(dil, nblk),
        in_specs=[cur(0), cur(1), cur(2), prev(1), prev(2), nxt_q,
                  tok, tok, tok, tok_next, tok_next, tok_next, tok, tok],
        out_specs=[pl.BlockSpec((tq, 3 * GROUP_W), lambda r, n: (n, r))],
        out_shape=[jax.ShapeDtypeStruct((length, dil * 3 * GROUP_W), BF16)],
        scratch_shapes=[pltpu.VMEM((tq, GROUP_W), F32)] * 3,
        params=_params(("arbitrary", "arbitrary"), 32),
        args=(qkv, qkv, qkv, qkv, qkv, qkv, dy, y, lse, dy, y, lse, cos_t, sin_t), rider=rider)
    return out, riding


def _layernorm_stats(z):
    mu = jnp.mean(z, axis=-1, keepdims=True)
    zc = z - mu
    rstd = lax.rsqrt(jnp.mean(zc * zc, axis=-1, keepdims=True) + EPS)
    return zc * rstd, rstd


def _tril_mask():
    row = lax.broadcasted_iota(jnp.int32, (CHUNK, CHUNK), 0)
    col = lax.broadcasted_iota(jnp.int32, (CHUNK, CHUNK), 1)
    return col <= row


def _mix_fwd(o_l, rest, x, w_sp, b_col, ln_g, ln_b, w_ba, w_bg, w_out, g1, rider=None):
    seq = x.shape[0]
    tm = 256

    def body(o0, l0, o1, l1, o2, l2, up_ref, zp_ref, gap_ref, gbp_ref, x_ref, wsp_ref, bcol_ref, lg_ref, lb_ref,
             wba_ref, wbg_ref, wout_ref, g1_ref, ya0, lj0, ya1, lj1, ya2, lj2, yg_ref, mg_ref, y_ref, x1_ref, slab):
        outs = [_get_tokens(slab, o, d, GROUP_W, 0, GROUP_W) for o, d in zip((o0, o1, o2), DILATIONS)]
        lses = [_get_tokens(slab, l, d, GROUP_W, 0, GROUP_W) for l, d in zip((l0, l1, l2), DILATIONS)]
        m = jnp.maximum(jnp.maximum(lses[0], lses[1]), lses[2])
        es = [jnp.exp(l - m) for l in lses]
        tot = es[0] + es[1] + es[2]
        ya = (es[0] * outs[0] + es[1] * outs[1] + es[2] * outs[2]) / tot
        lj = m + jnp.log(tot)
        for ya_ref, lj_ref, d in zip((ya0, ya1, ya2), (lj0, lj1, lj2), DILATIONS):
            _put_residue(slab, ya, ya_ref, d, GROUP_W, 0)
            _put_residue(slab, lj, lj_ref, d, GROUP_W, 0)
        zhat, _ = _layernorm_stats(_gelu(zp_ref[...].astype(F32)))
        zln = (zhat * lg_ref[...] + lb_ref[...]).astype(BF16)
        u = _gelu(up_ref[...].astype(F32))
        tril = _tril_mask()
        for g in range(GMLP_GROUPS):
            wm = jnp.where(tril, wsp_ref[g], 0.0).astype(BF16)
            cols = slice(g * CHUNK, (g + 1) * CHUNK)
            for c in range(tm // CHUNK):
                rows = slice(c * CHUNK, (c + 1) * CHUNK)
                sz = _dot(wm, zln[rows, cols]) + bcol_ref[g]
                yg_ref[rows, cols] = (u[rows, cols] * sz).astype(BF16)
        a = _dot(ya.astype(BF16), wba_ref[...])
        bm = _dot(yg_ref[...], wbg_ref[...])
        merged = (jax.nn.sigmoid(gap_ref[...].astype(F32)) * a + jax.nn.sigmoid(gbp_ref[...].astype(F32)) * bm).astype(BF16)
        mg_ref[...] = merged
        yv = _dot(merged, wout_ref[...])
        y_ref[...] = yv
        x1_ref[...] = x_ref[...] + (yv * _rsqrt_ms(yv)) * g1_ref[...]

    tok = lambda w: pl.BlockSpec((tm, w), lambda i: (i, 0))
    res = lambda d: pl.BlockSpec((tm // d, d * GROUP_W), lambda i: (i, 0))
    full = lambda *s: pl.BlockSpec(s, lambda i: (0,) * len(s))
    res_specs = [res(d) for d in DILATIONS for _ in range(2)]
    return _call(
        body, name="mix_fwd", grid=(seq // tm,),
        in_specs=res_specs + [
            pl.BlockSpec((tm, GMLP_W), lambda i: (i, 0)), pl.BlockSpec((tm, GMLP_W), lambda i: (i, 1)),
            pl.BlockSpec((tm, D_MODEL), lambda i: (i, 1)), pl.BlockSpec((tm, D_MODEL), lambda i: (i, 2)),
            tok(D_MODEL), full(GMLP_GROUPS, CHUNK, CHUNK), full(GMLP_GROUPS, CHUNK, 1), full(1, GMLP_W), full(1, GMLP_W),
            full(GROUP_W, D_MODEL), full(GMLP_W, D_MODEL), full(D_MODEL, D_MODEL), full(1, D_MODEL)],
        out_specs=res_specs + [tok(GMLP_W), tok(D_MODEL), tok(D_MODEL), tok(D_MODEL)],
        out_shape=[jax.ShapeDtypeStruct((seq // d, d * GROUP_W), F32) for d in DILATIONS for _ in range(2)]
        + [jax.ShapeDtypeStruct((seq, GMLP_W), BF16), jax.ShapeDtypeStruct((seq, D_MODEL), BF16),
           jax.ShapeDtypeStruct((seq, D_MODEL), F32), jax.ShapeDtypeStruct((seq, D_MODEL), F32)],
        scratch_shapes=[pltpu.VMEM((GROUP_W // LANES, tm, LANES), F32)],
        params=_params(("arbitrary",), 48),
        args=(*o_l, rest, rest, rest, rest, x, w_sp, b_col, ln_g, ln_b, w_ba, w_bg, w_out, g1), rider=rider)


def _mlp_fwd(x1, g2, g3, w_mi, w_mo, target):
    seq = x1.shape[0]
    tm, tf = MLP_TM, 512

    def body(x1_ref, g2_ref, g3_ref, wmi_ref, wmo_ref, t_ref, h2_ref, a_ref, dy2_ref, dout_ref, loss_ref, dg3_ref, sq_s):
        @pl.when(pl.program_id(0) == 0)
        def _():
            loss_ref[...] = jnp.zeros(loss_ref.shape, F32)
            dg3_ref[...] = jnp.zeros(dg3_ref.shape, F32)

        xv = x1_ref[...]
        hb = ((xv * _rsqrt_ms(xv)) * g2_ref[...]).astype(BF16)
        h2_ref[...] = hb
        for j in range(D_FF // tf):
            cols = slice(j * tf, (j + 1) * tf)
            a = jnp.maximum(_dot(hb, wmi_ref[:, cols]), 0.0)
            a_ref[:, cols] = a.astype(BF16)
            sq_s[:, cols] = (a * a).astype(BF16)
        y2 = _dot(sq_s[...], wmo_ref[...])
        r3 = _rsqrt_ms(y2)
        out = xv + (y2 * r3) * g3_ref[...]
        diff = out - t_ref[...]
        tile_loss = 0.5 * jnp.sum(jnp.mean(diff * diff, axis=-1, keepdims=True), axis=0, keepdims=True)
        loss_ref[...] += jnp.broadcast_to(tile_loss, loss_ref.shape)
        dout = diff * (1.0 / D_MODEL)
        dout_ref[...] = dout
        dy2, dg3 = _rmsnorm_bwd(dout, y2, g3_ref[...])
        dy2_ref[...] = dy2.astype(BF16)
        dg3_ref[...] += dg3

    tok = lambda w: pl.BlockSpec((tm, w), lambda i: (i, 0))
    vec = pl.BlockSpec((1, D_MODEL), lambda i: (0, 0))
    return _pallas(
        body, name="mlp_fwd", grid=(seq // tm,),
        in_specs=[tok(D_MODEL), vec, vec, _resident((D_MODEL, D_FF)), _resident((D_FF, D_MODEL)), tok(D_MODEL)],
        out_specs=[tok(D_MODEL), tok(D_FF), tok(D_MODEL), tok(D_MODEL), pl.BlockSpec((8, 128), lambda i: (0, 0)), vec],
        out_shape=[jax.ShapeDtypeStruct((seq, D_MODEL), BF16), jax.ShapeDtypeStruct((seq, D_FF), BF16),
                   jax.ShapeDtypeStruct((seq, D_MODEL), BF16), jax.ShapeDtypeStruct((seq, D_MODEL), F32),
                   jax.ShapeDtypeStruct((8, 128), F32), jax.ShapeDtypeStruct((1, D_MODEL), F32)],
        scratch_shapes=[pltpu.VMEM((tm, D_FF), BF16)],
        compiler_params=_params(("arbitrary",), 56),
    )(*map(_in_hbm, (x1, g2, g3, w_mi, w_mo, target)))


def _mlp_bwd(dy2, a, w_mo, w_mi, dout, x1, y, g2, g1, rider=None):
    seq = x1.shape[0]
    tm, tf = MLP_TM, 512

    def body(dy2_ref, a_ref, wmo_ref, wmi_ref, dout_ref, x1_ref, y_ref, g2_ref, g1_ref,
             dap_ref, dx1_ref, dy_ref, dg2_ref, dg1_ref):
        @pl.when(pl.program_id(0) == 0)
        def _():
            dg2_ref[...] = jnp.zeros(dg2_ref.shape, F32)
            dg1_ref[...] = jnp.zeros(dg1_ref.shape, F32)

        dy2v = dy2_ref[...]
        for j in range(D_FF // tf):
            cols = slice(j * tf, (j + 1) * tf)
            da2 = _dot_nt(dy2v, wmo_ref[cols, :])
            dap_ref[:, cols] = (da2 * (2.0 * a_ref[:, cols].astype(F32))).astype(BF16)
        dh2 = _dot_nt(dap_ref[...], wmi_ref[...])
        dres, dg2 = _rmsnorm_bwd(dh2, x1_ref[...], g2_ref[...])
        dx1 = dout_ref[...] + dres
        dx1_ref[...] = dx1
        dg2_ref[...] += dg2
        dyv, dg1 = _rmsnorm_bwd(dx1, y_ref[...], g1_ref[...])
        dy_ref[...] = dyv.astype(BF16)
        dg1_ref[...] += dg1

    tok = lambda w: pl.BlockSpec((tm, w), lambda i: (i, 0))
    vec = pl.BlockSpec((1, D_MODEL), lambda i: (0, 0))
    return _call(
        body, name="mlp_bwd", grid=(seq // tm,),
        in_specs=[tok(D_MODEL), tok(D_FF), _resident((D_FF, D_MODEL)), _resident((D_MODEL, D_FF)),
                  tok(D_MODEL), tok(D_MODEL), tok(D_MODEL), vec, vec],
        out_specs=[tok(D_FF), tok(D_MODEL), tok(D_MODEL), vec, vec],
        out_shape=[jax.ShapeDtypeStruct((seq, D_FF), BF16), jax.ShapeDtypeStruct((seq, D_MODEL), F32),
                   jax.ShapeDtypeStruct((seq, D_MODEL), BF16), jax.ShapeDtypeStruct((1, D_MODEL), F32),
                   jax.ShapeDtypeStruct((1, D_MODEL), F32)], scratch_shapes=[],
        params=_params(("arbitrary",), 56), args=(dy2, a, w_mo, w_mi, dout, x1, y, g2, g1), rider=rider)


def _tn_matmul(a, b, name, bm, bn, square_a=False, column_shards=False, rider=None):
    seq, m = a.shape
    n = b.shape[1]
    ts = 1024

    def body(a_ref, b_ref, o_ref):
        @pl.when(pl.program_id(2) == 0)
        def _():
            o_ref[...] = jnp.zeros(o_ref.shape, F32)

        av = a_ref[...]
        if square_a:
            af = av.astype(F32)
            av = (af * af).astype(BF16)
        o_ref[...] += _dot_tn(av, b_ref[...])

    if column_shards:
        out_spec = pl.BlockSpec((None, bm, bn), lambda mi, ni, s: (ni, mi, 0))
        out_shape = jax.ShapeDtypeStruct((n // bn, m, bn), F32)
    else:
        out_spec = pl.BlockSpec((bm, bn), lambda mi, ni, s: (mi, ni))
        out_shape = jax.ShapeDtypeStruct((m, n), F32)
    (out,), riding = _call(
        body, name=name, grid=(m // bm, n // bn, seq // ts),
        in_specs=[pl.BlockSpec((ts, bm), lambda mi, ni, s: (s, mi)), pl.BlockSpec((ts, bn), lambda mi, ni, s: (s, ni))],
        out_specs=[out_spec], out_shape=[out_shape], scratch_shapes=[],
        params=_params(("arbitrary", "arbitrary", "arbitrary"), 40), args=(a, b), rider=rider)
    return out, riding


def _tn_matmul_residue(a, b, dil, name):
    length = a.shape[0]
    m, n = a.shape[1] // dil, b.shape[1] // dil
    ts = min(1024, length)

    def body(a_ref, b_ref, o_ref):
        @pl.when((pl.program_id(0) == 0) & (pl.program_id(1) == 0))
        def _():
            o_ref[...] = jnp.zeros(o_ref.shape, F32)

        o_ref[...] += _dot_tn(a_ref[...], b_ref[...])

    return _pallas(
        body, name=name, grid=(dil, length // ts),
        in_specs=[pl.BlockSpec((ts, m), lambda r, s: (s, r)), pl.BlockSpec((ts, n), lambda r, s: (s, r))],
        out_specs=pl.BlockSpec((m, n), lambda r, s: (0, 0)),
        out_shape=jax.ShapeDtypeStruct((m, n), F32),
        compiler_params=_params(("arbitrary", "arbitrary"), 40),
    )(_in_hbm(a), _in_hbm(b))


def _mix_bwd(dy, ya, yg, mg, rest, w_out, w_ba, w_bg, w_sp, b_col, ln_g, ln_b, rider=None):
    seq = dy.shape[0]
    tm = 256

    def body(dy_ref, ya_ref, yg_ref, mg_ref, up_ref, zp_ref, gap_ref, gbp_ref, wout_ref, wba_ref, wbg_ref,
             wsp_ref, bcol_ref, lg_ref, lb_ref,
             dya0, dya1, dya2, dpr_ref, dwout_ref, dwba_ref, dwbg_ref, dwsp_ref, dbb_ref, dlg_ref, dlb_ref,
             dzln_s, du_s, slab):
        @pl.when(pl.program_id(0) == 0)
        def _():
            for ref in (dwout_ref, dwba_ref, dwbg_ref, dwsp_ref, dbb_ref, dlg_ref, dlb_ref):
                ref[...] = jnp.zeros(ref.shape, F32)

        dyv = dy_ref[...]
        dm = _dot_nt(dyv, wout_ref[...])
        dwout_ref[...] += _dot_tn(mg_ref[...], dyv)
        yab = ya_ref[...].astype(BF16)
        ygb = yg_ref[...]
        a = _dot(yab, wba_ref[...])
        bm = _dot(ygb, wbg_ref[...])
        ga = jax.nn.sigmoid(gap_ref[...].astype(F32))
        gb = jax.nn.sigmoid(gbp_ref[...].astype(F32))
        dpr_ref[:, 2 * GMLP_W:2 * GMLP_W + D_MODEL] = (dm * a * (ga * (1.0 - ga))).astype(BF16)
        dpr_ref[:, 2 * GMLP_W + D_MODEL:REST_W] = (dm * bm * (gb * (1.0 - gb))).astype(BF16)
        da = (dm * ga).astype(BF16)
        db = (dm * gb).astype(BF16)
        dwba = _dot_tn(yab, da)
        dwbg = _dot_tn(ygb, db)
        shard_w = D_MODEL // N_CHIPS
        for j in range(N_CHIPS):
            dwba_ref[j] += dwba[:, j * shard_w:(j + 1) * shard_w]
            dwbg_ref[j] += dwbg[:, j * shard_w:(j + 1) * shard_w]
        dya = _dot_nt(da, wba_ref[...])
        for dya_ref, d in zip((dya0, dya1, dya2), DILATIONS):
            _put_residue(slab, dya, dya_ref, d, GROUP_W, 0)
        dyg = _dot_nt(db, wbg_ref[...])

        zp = zp_ref[...].astype(F32)
        zhat, rstd = _layernorm_stats(_gelu(zp))
        lg = lg_ref[...]
        zln = (zhat * lg + lb_ref[...]).astype(BF16)
        up = up_ref[...].astype(F32)
        u = _gelu(up)
        tril = _tril_mask()
        for g in range(GMLP_GROUPS):
            wm = jnp.where(tril, wsp_ref[g], 0.0).astype(BF16)
            cols = slice(g * CHUNK, (g + 1) * CHUNK)
            for c in range(tm // CHUNK):
                rows = slice(c * CHUNK, (c + 1) * CHUNK)
                zb = zln[rows, cols]
                sz = _dot(wm, zb) + bcol_ref[g]
                dyg_cg = dyg[rows, cols]
                du_s[rows, cols] = dyg_cg * sz
                dsz = dyg_cg * u[rows, cols]
                dszb = dsz.astype(BF16)
                dbb_ref[g] += jnp.broadcast_to(jnp.sum(dsz, axis=-1, keepdims=True), (CHUNK, CHUNK))
                dwsp_ref[g] += jnp.where(tril, _dot_nt(dszb, zb), 0.0)
                dzln_s[rows, cols] = _dot_tn(wm, dszb)
        dzln = dzln_s[...]
        dlg_ref[...] += jnp.sum(dzln * zhat, axis=0, keepdims=True)
        dlb_ref[...] += jnp.sum(dzln, axis=0, keepdims=True)
        dzh = dzln * lg
        dz = rstd * (dzh - jnp.mean(dzh, axis=-1, keepdims=True) - zhat * jnp.mean(dzh * zhat, axis=-1, keepdims=True))
        dpr_ref[:, GMLP_W:2 * GMLP_W] = (dz * _gelu_grad(zp)).astype(BF16)
        dpr_ref[:, 0:GMLP_W] = (du_s[...] * _gelu_grad(up)).astype(BF16)

    tok = lambda w: pl.BlockSpec((tm, w), lambda i: (i, 0))
    full = lambda *s: pl.BlockSpec(s, lambda i: (0,) * len(s))
    return _call(
        body, name="mix_bwd", grid=(seq // tm,),
        in_specs=[tok(D_MODEL), tok(GROUP_W), tok(GMLP_W), tok(D_MODEL),
                  pl.BlockSpec((tm, GMLP_W), lambda i: (i, 0)), pl.BlockSpec((tm, GMLP_W), lambda i: (i, 1)),
                  pl.BlockSpec((tm, D_MODEL), lambda i: (i, 1)), pl.BlockSpec((tm, D_MODEL), lambda i: (i, 2)),
                  full(D_MODEL, D_MODEL), full(GROUP_W, D_MODEL), full(GMLP_W, D_MODEL),
                  full(GMLP_GROUPS, CHUNK, CHUNK), full(GMLP_GROUPS, CHUNK, 1), full(1, GMLP_W), full(1, GMLP_W)],
        out_specs=[pl.BlockSpec((tm // d, d * GROUP_W), lambda i: (i, 0)) for d in DILATIONS]
        + [tok(REST_W), full(D_MODEL, D_MODEL), full(N_CHIPS, GROUP_W, D_MODEL // N_CHIPS),
           full(N_CHIPS, GMLP_W, D_MODEL // N_CHIPS),
           full(GMLP_GROUPS, CHUNK, CHUNK), full(GMLP_GROUPS, CHUNK, CHUNK), full(1, GMLP_W), full(1, GMLP_W)],
        out_shape=[jax.ShapeDtypeStruct((seq // d, d * GROUP_W), F32) for d in DILATIONS]
        + [jax.ShapeDtypeStruct((seq, REST_W), BF16),
           jax.ShapeDtypeStruct((D_MODEL, D_MODEL), F32), jax.ShapeDtypeStruct((N_CHIPS, GROUP_W, D_MODEL // N_CHIPS), F32),
           jax.ShapeDtypeStruct((N_CHIPS, GMLP_W, D_MODEL // N_CHIPS), F32),
           jax.ShapeDtypeStruct((GMLP_GROUPS, CHUNK, CHUNK), F32),
           jax.ShapeDtypeStruct((GMLP_GROUPS, CHUNK, CHUNK), F32), jax.ShapeDtypeStruct((1, GMLP_W), F32),
           jax.ShapeDtypeStruct((1, GMLP_W), F32)],
        scratch_shapes=[pltpu.VMEM((tm, GMLP_W), F32), pltpu.VMEM((tm, GMLP_W), F32),
                        pltpu.VMEM((GROUP_W // LANES, tm, LANES), F32)],
        params=_params(("arbitrary",), 56),
        args=(dy, ya, yg, mg, rest, rest, rest, rest, w_out, w_ba, w_bg, w_sp, b_col, ln_g, ln_b), rider=rider)


IN_PROJ_BWD_TM = 256


def _in_proj_bwd(dqkv, drest, w_qkv, w_rest, x, dx1, g0, so_far, span, rider=None):
    seq = x.shape[0]
    tm = IN_PROJ_BWD_TM
    off, steps = span
    gx_so_far, dg_so_far = so_far

    def body(d0, d1, d2, dr_ref, w0, w1, w2, wr_ref, x_ref, dx1_ref, g_ref, dg_in_ref, gx_in_ref, gx_ref, dg_ref, slab):
        @pl.when(pl.program_id(0) == 0)
        def _():
            dg_ref[...] = dg_in_ref[...]

        dh = _dot(dr_ref[...], wr_ref[...])
        for d_ref, w_ref, dil in zip((d0, d1, d2), (w0, w1, w2), DILATIONS):
            piece = d_ref[...] if dil == 1 else _get_tokens(slab, d_ref, dil, 3 * GROUP_W, 0, 3 * GROUP_W).astype(BF16)
            dh = dh + _dot(piece, w_ref[...])
        dres, dg = _rmsnorm_bwd(dh, x_ref[...], g_ref[...])
        gx_ref[...] = dx1_ref[...] + dres
        dg_ref[...] += dg

    tok = lambda w: pl.BlockSpec((tm, w), lambda i: (i + off, 0))
    full = lambda *s: pl.BlockSpec(s, lambda i: (0,) * len(s))
    in_specs = ([pl.BlockSpec((tm // d, d * 3 * GROUP_W), lambda i: (i + off, 0)) for d in DILATIONS] + [tok(REST_W)]
                + [_resident((3 * GROUP_W, D_MODEL))] * 3 + [_resident((REST_W, D_MODEL))]
                + [tok(D_MODEL), tok(D_MODEL), full(1, D_MODEL), full(1, D_MODEL), HBM_SPEC])
    return _call(
        body, name=f"in_proj_bwd_{off}", grid=(steps,), in_specs=in_specs,
        out_specs=[tok(D_MODEL), full(1, D_MODEL)],
        out_shape=[jax.ShapeDtypeStruct((seq, D_MODEL), F32), jax.ShapeDtypeStruct((1, D_MODEL), F32)],
        scratch_shapes=[pltpu.VMEM((3 * GROUP_W // LANES, tm, LANES), F32)],
        params=_params(("arbitrary",), 48), args=(*dqkv, drest, *w_qkv, w_rest, x, dx1, g0, dg_so_far, gx_so_far),
        rider=rider, aliases={len(in_specs) - 1: 0})


def _adamw(w, g, m, v, name):
    rows, cols = w.shape
    tr = _row_tile(rows) if rows % 16 == 0 else rows
    c1 = 1.0 - ADAM_B1 ** ADAM_STEP
    c2 = 1.0 - ADAM_B2 ** ADAM_STEP

    def body(w_ref, g_ref, m_ref, v_ref, go_ref, d_ref, nm_ref, nv_ref):
        gv = g_ref[...]
        go_ref[...] = gv
        nm = ADAM_B1 * m_ref[...] + (1.0 - ADAM_B1) * gv
        nv = ADAM_B2 * v_ref[...] + (1.0 - ADAM_B2) * (gv * gv)
        d_ref[...] = -ADAM_LR * ((nm / c1) / (jnp.sqrt(nv / c2) + ADAM_EPS) + ADAM_WD * w_ref[...])
        nm_ref[...] = nm
        nv_ref[...] = nv

    spec = pl.BlockSpec((tr, cols), lambda i: (i, 0))
    return _pallas(
        body, name=name, grid=(rows // tr,),
        in_specs=[spec] * 4, out_specs=[spec] * 4,
        out_shape=[jax.ShapeDtypeStruct((rows, cols), F32)] * 4,
        compiler_params=_params(("arbitrary",), 40),
    )(*map(_in_hbm, (w, g, m, v)))


def _place():
    x, y, c = lax.axis_index("x"), lax.axis_index("y"), lax.axis_index("c")
    chips = [(1 - x, y), (x, 1 - y), (1 - x, 1 - y)]
    return x, y, c, chips


class _Exchange:
    def __init__(self, inputs, out_shapes, n_sems, start, finish, aliases=None):
        self.inputs, self.out_shapes, self.n_sems = list(inputs), list(out_shapes), n_sems
        self.start, self.finish, self.aliases = start, finish, dict(aliases or {})

    def scratch(self):
        return [pltpu.SemaphoreType.DMA((self.n_sems,)), pltpu.SemaphoreType.DMA((self.n_sems,))]


def _together(*parts):
    ins = [len(p.inputs) for p in parts]
    outs = [len(p.out_shapes) for p in parts]

    def split(refs, counts):
        pos, pieces = 0, []
        for cnt in counts:
            pieces.append(refs[pos:pos + cnt])
            pos += cnt
        return pieces

    def run(which):
        def go(in_refs, out_refs, *sems):
            for k, (p, i, o) in enumerate(zip(parts, split(in_refs, ins), split(out_refs, outs))):
                getattr(p, which)(i, o, sems[2 * k], sems[2 * k + 1])
        return go

    both = _Exchange([a for p in parts for a in p.inputs], [s for p in parts for s in p.out_shapes], 0, run("start"),
                     run("finish"))
    both.aliases = {sum(ins[:k]) + i: sum(outs[:k]) + o for k, p in enumerate(parts) for i, o in p.aliases.items()}
    both.scratch = lambda: [s for p in parts for s in p.scratch()]
    return both


def _run_exchange(ex, name):
    n_in, n_out = len(ex.inputs), len(ex.out_shapes)

    def body(*refs):
        ins, outs, sems = refs[:n_in], refs[n_in:n_in + n_out], refs[n_in + n_out:]
        ex.start(ins, outs, *sems)
        ex.finish(ins, outs, *sems)

    return _pallas(
        body, name=name, in_specs=[HBM_SPEC] * n_in, out_specs=[HBM_SPEC] * n_out, out_shape=ex.out_shapes,
        scratch_shapes=ex.scratch(), input_output_aliases=ex.aliases,
    )(*ex.inputs)


def _call(body, *, name, grid, in_specs, out_specs, out_shape, scratch_shapes, params, args, rider=None, aliases=None):
    in_specs, out_specs, out_shape, scratch_shapes = list(in_specs), list(out_specs), list(out_shape), list(scratch_shapes)
    aliases = dict(aliases or {})
    args = [_in_hbm(a) for a in args]
    if rider is None:
        outs = _pallas(body, name=name, grid=grid, in_specs=in_specs, out_specs=out_specs, out_shape=out_shape,
                              scratch_shapes=scratch_shapes, input_output_aliases=aliases, compiler_params=params)(*args)
        return list(outs), []
    n_in, n_out, n_scr = len(in_specs), len(out_specs), len(scratch_shapes)
    r_in, r_out = len(rider.inputs), len(rider.out_shapes)

    def wrapped(*refs):
        ins, r_ins = refs[:n_in], refs[n_in:n_in + r_in]
        pos = n_in + r_in
        outs, r_outs = refs[pos:pos + n_out], refs[pos + n_out:pos + n_out + r_out]
        pos += n_out + r_out
        scr, sems = refs[pos:pos + n_scr], refs[pos + n_scr:]
        ids = [pl.program_id(k) for k in range(len(grid))]
        first, last = ids[0] == 0, ids[0] == grid[0] - 1
        for k in range(1, len(grid)):
            first, last = first & (ids[k] == 0), last & (ids[k] == grid[k] - 1)

        @pl.when(first)
        def _():
            rider.start(r_ins, r_outs, *sems)

        body(*ins, *outs, *scr)

        @pl.when(last)
        def _():
            rider.finish(r_ins, r_outs, *sems)

    outs = _pallas(
        wrapped, name=name, grid=grid, in_specs=in_specs + [HBM_SPEC] * r_in, out_specs=out_specs + [HBM_SPEC] * r_out,
        out_shape=out_shape + rider.out_shapes, scratch_shapes=scratch_shapes + rider.scratch(),
        input_output_aliases={**aliases, **{n_in + i: n_out + o for i, o in rider.aliases.items()}}, compiler_params=params,
    )(*args, *rider.inputs)
    return list(outs[:n_out]), list(outs[n_out:])


def _stage_weights(shards):
    n = len(shards)

    def body(*refs):
        ins, outs, stages, sems = refs[:n], refs[n:2 * n], refs[2 * n:3 * n], refs[3 * n]
        x, y, _, _ = _place()
        copies = []
        for t in range(n):
            stages[t][...] = ins[t][...].astype(BF16)
            copies.append(pltpu.make_async_copy(stages[t], outs[t].at[2 * x + y], sems.at[t]))
            copies[-1].start()
        for cp in copies:
            cp.wait()

    stage_bytes = sum(s.size * 6 for s in shards)
    return _pallas(
        body, name="stage_weights", in_specs=[VMEM_SPEC] * n, out_specs=[HBM_SPEC] * n,
        out_shape=[jax.ShapeDtypeStruct((N_CHIPS,) + s.shape, BF16) for s in shards],
        scratch_shapes=[pltpu.VMEM(s.shape, BF16) for s in shards] + [pltpu.SemaphoreType.DMA((n,))],
        compiler_params=pltpu.CompilerParams(vmem_limit_bytes=stage_bytes + 8 * MIB),
    )(*shards)


def _gather(buffers, stage="both", part=(0, 1)):
    n = len(buffers)
    halves = [b.shape[1] // part[1] // 2 for b in buffers]

    def half_of(outs, t, chip, which):
        return outs[t].at[chip, pl.ds((2 * part[0] + which) * halves[t], halves[t]), :]

    def copy(outs, sems, t, k, chip, which, to):
        rows = half_of(outs, t, chip, which)
        return pltpu.make_async_remote_copy(src_ref=rows, dst_ref=rows, send_sem=sems[0].at[6 * t + k],
                                            recv_sem=sems[1].at[6 * t + k], device_id=to, device_id_type=MESH)

    def to_chips(outs, sems, what):
        x, y, c, chips = _place()
        for t in range(n):
            for j, (px, py) in enumerate(chips):
                if what == "start":
                    copy(outs, sems, t, j, 2 * x + y, c, (px, py, c)).start()
                else:
                    copy(outs, sems, t, j, 2 * px + py, c, (px, py, c)).wait_recv()
                    copy(outs, sems, t, j, 2 * x + y, c, (px, py, c)).wait_send()

    def to_sibling(outs, sems, what):
        x, y, c, chips = _place()
        for t in range(n):
            for j, (px, py) in enumerate(chips):
                if what == "start":
                    copy(outs, sems, t, 3 + j, 2 * px + py, c, (x, y, 1 - c)).start()
                else:
                    copy(outs, sems, t, 3 + j, 2 * px + py, 1 - c, (x, y, 1 - c)).wait_recv()
                    copy(outs, sems, t, 3 + j, 2 * px + py, c, (x, y, 1 - c)).wait_send()

    def start(ins, outs, *sems):
        (to_sibling if stage == "pair" else to_chips)(outs, sems, "start")

    def finish(ins, outs, *sems):
        if stage != "pair":
            to_chips(outs, sems, "finish")
        if stage == "both":
            to_sibling(outs, sems, "start")
        if stage != "chips":
            to_sibling(outs, sems, "finish")

    return _Exchange(buffers, [jax.ShapeDtypeStruct(b.shape, b.dtype) for b in buffers], 6 * n, start, finish,
                     aliases={t: t for t in range(n)})


def _pair_exchange(grads):
    n = len(grads)
    halves = [g.shape[1] // 2 for g in grads]

    def copies(ins, outs, send_sems, recv_sems):
        x, y, c, _ = _place()
        return [pltpu.make_async_remote_copy(
            src_ref=ins[t].at[:, pl.ds((1 - c) * halves[t], halves[t]), :], dst_ref=outs[t],
            send_sem=send_sems.at[t], recv_sem=recv_sems.at[t], device_id=(x, y, 1 - c), device_id_type=MESH)
            for t in range(n)]

    def start(*refs):
        for cp in copies(*refs):
            cp.start()

    def finish(*refs):
        for cp in copies(*refs):
            cp.wait()

    return _Exchange(grads, [jax.ShapeDtypeStruct((N_CHIPS, h, g.shape[2]), F32) for g, h in zip(grads, halves)], n,
                     start, finish)


def _row_tile(rows):
    return max(t for t in range(16, 257, 16) if rows % t == 0)


def _pair_add(grad, other, place, name):
    _, rows, cols = grad.shape
    rh = rows // 2
    tr = _row_tile(rh)
    nb = rh // tr

    def body(p_ref, g_ref, a_ref, wire_ref, own_ref):
        s = g_ref[...] + a_ref[...]
        wire_ref[...] = s.astype(BF16)

        @pl.when(pl.program_id(1) == p_ref[1])
        def _():
            own_ref[...] = s

    blk = (None, tr, cols)
    return _pallas(
        body, name=name,
        grid_spec=pltpu.PrefetchScalarGridSpec(
            num_scalar_prefetch=1, grid=(nb, N_CHIPS),
            in_specs=[pl.BlockSpec(blk, lambda i, j, p: (j, p[0] * nb + i, 0)), pl.BlockSpec(blk, lambda i, j, p: (j, i, 0))],
            out_specs=[pl.BlockSpec(blk, lambda i, j, p: (j, i, 0)), pl.BlockSpec((tr, cols), lambda i, j, p: (i, 0))]),
        out_shape=[jax.ShapeDtypeStruct((N_CHIPS, rh, cols), BF16), jax.ShapeDtypeStruct((rh, cols), F32)],
        compiler_params=_params(("arbitrary", "arbitrary"), 32),
    )(place, grad, other)


def _chip_exchange(wires):
    n = len(wires)

    def copies(ins, outs, send_sems, recv_sems):
        x, y, c, chips = _place()
        return [pltpu.make_async_remote_copy(
            src_ref=ins[t].at[2 * px + py], dst_ref=outs[t].at[j], send_sem=send_sems.at[3 * t + j],
            recv_sem=recv_sems.at[3 * t + j], device_id=(px, py, c), device_id_type=MESH)
            for t in range(n) for j, (px, py) in enumerate(chips)]

    def start(*refs):
        for cp in copies(*refs):
            cp.start()

    def finish(*refs):
        for cp in copies(*refs):
            cp.wait()

    return _Exchange(wires, [jax.ShapeDtypeStruct((3,) + w.shape[1:], BF16) for w in wires], 3 * n, start, finish)


def _chip_add(own, arrived, place, name):
    rh, cols = own.shape
    tr = _row_tile(rh)
    nb = rh // tr

    def body(p_ref, s_ref, b0, b1, b2, o_ref):
        o_ref[...] = ((s_ref[...] + b0[...].astype(F32)) + b1[...].astype(F32)) + b2[...].astype(F32)

    blk = (None, tr, cols)
    return _pallas(
        body, name=name,
        grid_spec=pltpu.PrefetchScalarGridSpec(
            num_scalar_prefetch=1, grid=(nb,),
            in_specs=[pl.BlockSpec((tr, cols), lambda i, p: (i, 0)), pl.BlockSpec(blk, lambda i, p: (0, i, 0)),
                      pl.BlockSpec(blk, lambda i, p: (1, i, 0)), pl.BlockSpec(blk, lambda i, p: (2, i, 0))],
            out_specs=pl.BlockSpec((tr, cols), lambda i, p: (p[0] * nb + i, 0))),
        out_shape=jax.ShapeDtypeStruct((2 * rh, cols), F32),
        compiler_params=_params(("arbitrary",), 32),
    )(place, own, arrived, arrived, arrived)


def _pair_share(halves):
    n = len(halves)
    rhs = [h.shape[0] // 2 for h in halves]

    def copy(outs, send_sems, recv_sems, t, which):
        x, y, c, _ = _place()
        rows = outs[t].at[pl.ds(which * rhs[t], rhs[t]), :]
        return pltpu.make_async_remote_copy(src_ref=rows, dst_ref=rows, send_sem=send_sems.at[t], recv_sem=recv_sems.at[t],
                                            device_id=(x, y, 1 - c), device_id_type=MESH)

    def start(ins, outs, send_sems, recv_sems):
        c = lax.axis_index("c")
        for t in range(n):
            copy(outs, send_sems, recv_sems, t, c).start()

    def finish(ins, outs, send_sems, recv_sems):
        c = lax.axis_index("c")
        for t in range(n):
            copy(outs, send_sems, recv_sems, t, c).wait_send()
            copy(outs, send_sems, recv_sems, t, 1 - c).wait_recv()

    return _Exchange(halves, [jax.ShapeDtypeStruct(h.shape, F32) for h in halves], n, start, finish,
                     aliases={t: t for t in range(n)})


class _GradReduction:
    def __init__(self, grads, place, tag):
        self.names, self.grads, self.place, self.tag = list(grads), grads, place, tag

    def pair_exchange(self):
        return _pair_exchange([self.grads[n] for n in self.names])

    def chip_exchange(self, others):
        sums = [_pair_add(self.grads[n], o, self.place, f"{self.tag}_pair_add_{n}") for n, o in zip(self.names, others)]
        self.owns = [own for _, own in sums]
        return _chip_exchange([wire for wire, _ in sums])

    def pair_share(self, arrived):
        return _pair_share([_chip_add(own, arr, self.place, f"{self.tag}_chip_add_{n}")
                            for n, own, arr in zip(self.names, self.owns, arrived)])

    def result(self, shared):
        return dict(zip(self.names, shared))


def _all_reduce_small(p):
    rows, lanes = p.shape
    flips = [(fx, fy, fc) for fx in (0, 1) for fy in (0, 1) for fc in (0, 1)][1:]

    def body(p_ref, o_ref, buf, send_sems, recv_sems):
        x, y, c, _ = _place()
        me = 4 * x + 2 * y + c
        buf[me] = p_ref[...]
        peers = [((1 - x) if fx else x, (1 - y) if fy else y, (1 - c) if fc else c) for fx, fy, fc in flips]
        cps = []
        for k, peer in enumerate(peers):
            cp = pltpu.make_async_remote_copy(
                src_ref=p_ref, dst_ref=buf.at[me], send_sem=send_sems.at[k], recv_sem=recv_sems.at[k],
                device_id=peer, device_id_type=MESH)
            cp.start()
            cps.append(cp)
        for k, (px, py, pc) in enumerate(peers):
            pltpu.make_async_remote_copy(
                src_ref=p_ref, dst_ref=buf.at[4 * px + 2 * py + pc], send_sem=send_sems.at[k], recv_sem=recv_sems.at[k],
                device_id=(px, py, pc), device_id_type=MESH).wait_recv()
        for cp in cps:
            cp.wait_send()
        acc = buf[0]
        for s in range(1, 8):
            acc = acc + buf[s]
        o_ref[...] = acc

    return _pallas(
        body, name="small_all_reduce", in_specs=[VMEM_SPEC], out_specs=VMEM_SPEC,
        out_shape=jax.ShapeDtypeStruct((rows, lanes), F32),
        scratch_shapes=[pltpu.VMEM((8, rows, lanes), F32), pltpu.SemaphoreType.DMA((7,)), pltpu.SemaphoreType.DMA((7,))],
        compiler_params=pltpu.CompilerParams(vmem_limit_bytes=32 * MIB),
    )(p)


BIG = ("w_in", "w_branch_attn", "w_branch_gmlp", "w_out", "w_mlp_in", "w_mlp_out")
COLUMN_SHARDED = ("w_branch_attn", "w_branch_gmlp", "w_mlp_in")
SMALL = ("norm_pre_mix", "w_spatial", "b_spatial", "ln_v_gain", "ln_v_bias", "norm_post_mix", "norm_pre_mlp", "norm_post_mlp")
ORDER = ("norm_pre_mix", "w_in", "w_spatial", "b_spatial", "ln_v_gain", "ln_v_bias", "w_branch_attn", "w_branch_gmlp",
         "w_out", "norm_post_mix", "norm_pre_mlp", "w_mlp_in", "w_mlp_out", "norm_post_mlp")


def _full_weight(name, gathered):
    if name in COLUMN_SHARDED:
        return jnp.transpose(gathered, (1, 0, 2)).reshape(gathered.shape[1], -1)
    return gathered.reshape(-1, gathered.shape[2])


def _rows8(a):
    a = a.reshape(-1, 128)
    pad = (-a.shape[0]) % 8
    return jnp.pad(a, ((0, pad), (0, 0))) if pad else a


def _qkv_columns(group):
    return [(sec * ATTN_W + group * GROUP_W, sec * ATTN_W + (group + 1) * GROUP_W) for sec in range(3)]


def _device_step(x, target, small, shards, place):
    seq = x.shape[0]
    g0, g1, g2, g3 = small["norm_pre_mix"], small["norm_post_mix"], small["norm_pre_mlp"], small["norm_post_mlp"]
    w_sp = small["w_spatial"]
    b_col = small["b_spatial"].reshape(GMLP_GROUPS, CHUNK, 1)
    ln_g, ln_b = small["ln_v_gain"], small["ln_v_bias"]

    staged = _stage_weights(shards)
    tables, w_in = _rope_tables(seq, rider=_gather(staged[:1], part=(0, 2)))
    h, (w_in,) = _norm_in(x, g0, rider=_gather(w_in, part=(1, 2)))
    w_in = _full_weight("w_in", w_in)
    early, late = staged[1:-1], staged[-1:]
    (*qkv, rest), landed = _in_proj(h[0], w_in, *tables[1], rider=_gather(early, "chips"))

    o_l, gathered = _attn_fwd(qkv[0], DILATIONS[0], rider=_gather(landed, "pair"))
    full = {n: _full_weight(n, gw) for n, gw in zip(BIG[1:-1], gathered)}
    for g in range(1, N_GROUPS):
        o_l.extend(_attn_fwd(qkv[g], DILATIONS[g])[0])
    (*ya_l, yg, mg, y, x1), late = _mix_fwd(o_l, rest, x, w_sp, b_col, ln_g, ln_b, full["w_branch_attn"],
                                           full["w_branch_gmlp"], full["w_out"], g1, rider=_gather(late))
    full[BIG[-1]] = _full_weight(BIG[-1], late[0])
    ya, lse = ya_l[0::2], ya_l[1::2]
    h2, a, dy2, dout, loss8, dg3 = _mlp_fwd(x1, g2, g3, full["w_mlp_in"], full["w_mlp_out"], target)
    d_wmo, _ = _tn_matmul(a, dy2, "grad_w_mlp_out", 1024, 1024, square_a=True)
    mlp_out = _GradReduction({"w_mlp_out": d_wmo.reshape(N_CHIPS, D_FF // N_CHIPS, D_MODEL)}, place, "mlp_out")
    (dap, dx1, dy, dg2, dg1), riding = _mlp_bwd(dy2, a, full["w_mlp_out"], full["w_mlp_in"], dout, x1, y, g2, g1,
                                                 rider=mlp_out.pair_exchange())
    d_wmi, riding = _tn_matmul(h2, dap, "grad_w_mlp_in", 1024, 1024, column_shards=True,
                               rider=mlp_out.chip_exchange(riding))
    mlp_in = _GradReduction({"w_mlp_in": d_wmi}, place, "mlp_in")
    (*dya, drest, d_wout, d_wba, d_wbg, d_wsp, d_bb, d_lg, d_lb), riding = _mix_bwd(
        dy, ya[0], yg, mg, rest, full["w_out"], full["w_branch_attn"], full["w_branch_gmlp"], w_sp, b_col, ln_g, ln_b,
        rider=_together(mlp_out.pair_share(riding), mlp_in.pair_exchange()))
    reduced = mlp_out.result(riding[:1])
    mix = _GradReduction({"w_branch_attn": d_wba, "w_branch_gmlp": d_wbg,
                          "w_out": d_wout.reshape(N_CHIPS, D_MODEL // N_CHIPS, D_MODEL)}, place, "mix")
    attn = lambda g, rider: _attn_bwd(qkv[g], dya[g], ya[g], lse[g], *tables[DILATIONS[g]], DILATIONS[g], rider=rider)
    dqkv0, riding = attn(0, _together(mlp_in.chip_exchange(riding[1:]), mix.pair_exchange()))
    dqkv1, riding = attn(1, _together(mlp_in.pair_share(riding[:1]), mix.chip_exchange(riding[1:])))
    reduced.update(mlp_in.result(riding[:1]))
    dqkv2, riding = attn(2, mix.pair_share(riding[1:]))
    reduced.update(mix.result(riding))
    dqkv = [dqkv0, dqkv1, dqkv2]

    d_qkv = [_tn_matmul_residue(dqkv[g], h[g], dil, f"grad_w_in_qkv{g}") for g, dil in enumerate(DILATIONS)]
    d_rest, _ = _tn_matmul(drest, h[0], "grad_w_in_rest", 1024, 1024)
    d_win = jnp.concatenate([d_qkv[g][s * GROUP_W:(s + 1) * GROUP_W] for s in range(3) for g in range(N_GROUPS)]
                            + [d_rest], axis=0)
    first = _GradReduction({"w_in": d_win.reshape(N_CHIPS, IN_W // N_CHIPS, D_MODEL)}, place, "w_in")
    w_qkv = [jnp.concatenate([w_in[lo:hi] for lo, hi in _qkv_columns(g)], axis=0) for g in range(N_GROUPS)]
    w_rest = w_in[QKV_W:]
    tiles = seq // IN_PROJ_BWD_TM
    so_far = (lax.empty((seq, D_MODEL), F32), jnp.zeros((1, D_MODEL), F32))
    in_bwd = lambda so_far, span, rider: _in_proj_bwd(dqkv, drest, w_qkv, w_rest, x, dx1, g0, so_far, span, rider=rider)
    so_far, riding = in_bwd(so_far, (0, tiles // 4), first.pair_exchange())
    so_far, riding = in_bwd(so_far, (tiles // 4, tiles // 2), first.chip_exchange(riding))
    shared = _run_exchange(first.pair_share(riding), "w_in_pair_share")
    (grad_x, dg0), _ = in_bwd(so_far, (3 * tiles // 4, tiles // 4), None)
    reduced.update(first.result(shared))
    little = {"norm_pre_mix": dg0, "w_spatial": d_wsp, "b_spatial": d_bb[:, :, 0], "ln_v_gain": d_lg, "ln_v_bias": d_lb,
              "norm_post_mix": dg1, "norm_pre_mlp": dg2, "norm_post_mlp": dg3}
    return loss8, grad_x, reduced, little


def kernel(x, norm_pre_mix, w_in, w_spatial, b_spatial, ln_v_gain, ln_v_bias, w_branch_attn, w_branch_gmlp, w_out, norm_post_mix, norm_pre_mlp, w_mlp_in, w_mlp_out, norm_post_mlp, loss_target, m_norm_pre_mix, m_w_in, m_w_spatial, m_b_spatial, m_ln_v_gain, m_ln_v_bias, m_w_branch_attn, m_w_branch_gmlp, m_w_out, m_norm_post_mix, m_norm_pre_mlp, m_w_mlp_in, m_w_mlp_out, m_norm_post_mlp, v_norm_pre_mix, v_w_in, v_w_spatial, v_b_spatial, v_ln_v_gain, v_ln_v_bias, v_w_branch_attn, v_w_branch_gmlp, v_w_out, v_norm_post_mix, v_norm_pre_mlp, v_w_mlp_in, v_w_mlp_out, v_norm_post_mlp):
    given = dict(norm_pre_mix=norm_pre_mix, w_in=w_in, w_spatial=w_spatial, b_spatial=b_spatial, ln_v_gain=ln_v_gain,
                 ln_v_bias=ln_v_bias, w_branch_attn=w_branch_attn, w_branch_gmlp=w_branch_gmlp, w_out=w_out,
                 norm_post_mix=norm_post_mix, norm_pre_mlp=norm_pre_mlp, w_mlp_in=w_mlp_in, w_mlp_out=w_mlp_out,
                 norm_post_mlp=norm_post_mlp)
    moments_m = dict(norm_pre_mix=m_norm_pre_mix, w_in=m_w_in, w_spatial=m_w_spatial, b_spatial=m_b_spatial,
                     ln_v_gain=m_ln_v_gain, ln_v_bias=m_ln_v_bias, w_branch_attn=m_w_branch_attn,
                     w_branch_gmlp=m_w_branch_gmlp, w_out=m_w_out, norm_post_mix=m_norm_post_mix,
                     norm_pre_mlp=m_norm_pre_mlp, w_mlp_in=m_w_mlp_in, w_mlp_out=m_w_mlp_out, norm_post_mlp=m_norm_post_mlp)
    moments_v = dict(norm_pre_mix=v_norm_pre_mix, w_in=v_w_in, w_spatial=v_w_spatial, b_spatial=v_b_spatial,
                     ln_v_gain=v_ln_v_gain, ln_v_bias=v_ln_v_bias, w_branch_attn=v_w_branch_attn,
                     w_branch_gmlp=v_w_branch_gmlp, w_out=v_w_out, norm_post_mix=v_norm_post_mix,
                     norm_pre_mlp=v_norm_pre_mlp, w_mlp_in=v_w_mlp_in, w_mlp_out=v_w_mlp_out, norm_post_mlp=v_norm_post_mlp)
    cx, cy, cc = lax.axis_index("x"), lax.axis_index("y"), lax.axis_index("c")

    shards = [given[n][0].T if n == "w_in" else given[n][0] for n in BIG]
    small = {n: given[n][0] if given[n].ndim > 2 else given[n] for n in SMALL}
    place = jnp.stack([cc, 2 * cx + cy]).astype(jnp.int32)
    loss8, grad_x, grad_shard, grads = _device_step(x[0], loss_target[0], small, shards, place)

    packed = jnp.concatenate([_rows8(grads[n]) for n in SMALL] + [loss8], axis=0)
    summed = _all_reduce_small(packed)
    loss = summed[packed.shape[0] - loss8.shape[0], 0]
    row = 0
    for n in SMALL:
        shape = given[n][0].shape
        cnt = -(-(given[n][0].size // 128) // 8) * 8
        grad_shard[n] = summed[row:row + given[n][0].size // 128].reshape(shape)
        row += cnt

    grad_out, deltas, new_m, new_v = {}, {}, {}, {}
    for n in ORDER:
        shape = given[n].shape
        if n == "w_in":
            outs = _adamw(given[n][0].T, grad_shard[n], moments_m[n][0].T, moments_v[n][0].T, "adamw_" + n)
            outs = [o.T for o in outs]
        else:
            two_d = (-1, shape[-1])
            outs = _adamw(given[n].reshape(two_d), grad_shard[n].reshape(two_d), moments_m[n].reshape(two_d),
                          moments_v[n].reshape(two_d), "adamw_" + n)
        grad_out[n], deltas[n], new_m[n], new_v[n] = [o.reshape(shape) for o in outs]
    return (loss, grad_x[None], *[grad_out[n] for n in ORDER], *[deltas[n] for n in ORDER], *[new_m[n] for n in ORDER],
            *[new_v[n] for n in ORDER])
```

```python
import math

import jax
import jax.numpy as jnp
from jax import lax
from jax.experimental import pallas as pl
from jax.experimental.pallas import tpu as pltpu

F32 = jnp.float32
BF16 = jnp.bfloat16
MESH = pl.DeviceIdType.MESH

D_MODEL = 1024
HEAD_DIM = 64
HEADS_PER_GROUP = 4
GROUP_W = HEADS_PER_GROUP * HEAD_DIM
DILATIONS = (1, 4, 16)
N_GROUPS = len(DILATIONS)
ATTN_W = N_GROUPS * GROUP_W
QKV_W = 3 * ATTN_W
GMLP_W = 512
GMLP_GROUPS = 4
CHUNK = 128
REST_W = 2 * GMLP_W + 2 * D_MODEL
IN_W = QKV_W + REST_W
D_FF = 4096
QBLK = 128
ROPE_THETA = 10000.0
EPS = 1e-6
NEG = -1e30
SCALE = HEAD_DIM ** -0.5
N_CHIPS = 4

ADAM_LR = 0.001
ADAM_B1 = 0.9
ADAM_B2 = 0.999
ADAM_EPS = 1e-08
ADAM_WD = 0.01
ADAM_STEP = 10

MIB = 1024 * 1024
HBM_SPEC = pl.BlockSpec(memory_space=pltpu.HBM)
VMEM_SPEC = pl.BlockSpec(memory_space=pltpu.VMEM)


MLP_TM = 256


def _params(semantics, vmem_mib):
    return pltpu.CompilerParams(dimension_semantics=semantics, vmem_limit_bytes=vmem_mib * MIB)


def _in_hbm(a):
    return pltpu.with_memory_space_constraint(a, pltpu.HBM) if a.size * a.dtype.itemsize >= MIB else a


def _pallas(body, **kwargs):
    return pl.pallas_call(body, **kwargs)


def _resident(shape):
    return pl.BlockSpec(shape, lambda *_: (0,) * len(shape), pipeline_mode=pl.Buffered(1))


def _dot(a, b):
    return jnp.dot(a, b, preferred_element_type=F32)


def _dot_nt(a, b):
    return lax.dot_general(a, b, (((1,), (1,)), ((), ())), preferred_element_type=F32)


def _dot_tn(a, b):
    return lax.dot_general(a, b, (((0,), (0,)), ((), ())), preferred_element_type=F32)


_GELU_C = math.sqrt(2.0 / math.pi)


def _gelu(x):
    return x * (0.5 * (1.0 + jnp.tanh(_GELU_C * (x + 0.044715 * (x * x * x)))))


def _gelu_grad(x):
    t = jnp.tanh(_GELU_C * (x + 0.044715 * (x * x * x)))
    return 0.5 * (1.0 + t) + 0.5 * x * (1.0 - t * t) * (_GELU_C * (1.0 + 3.0 * 0.044715 * (x * x)))


def _rsqrt_ms(v):
    return lax.rsqrt(jnp.mean(v * v, axis=-1, keepdims=True) + EPS)


def _rmsnorm_bwd(dn, src, gain):
    r = _rsqrt_ms(src)
    t = gain * dn
    dgain = jnp.sum(dn * (src * r), axis=0, keepdims=True)
    dsrc = r * t - src * ((r * r * r) * jnp.mean(t * src, axis=-1, keepdims=True))
    return dsrc, dgain


def _rot_half(v):
    w = v.shape[-1]
    lane = lax.broadcasted_iota(jnp.int32, v.shape, v.ndim - 1)
    return jnp.where((lane % HEAD_DIM) < HEAD_DIM // 2, pltpu.roll(v, w - HEAD_DIM // 2, v.ndim - 1),
                     pltpu.roll(v, HEAD_DIM // 2, v.ndim - 1))


def _head_masks(shape):
    lane = lax.broadcasted_iota(jnp.int32, shape, 1)
    return [(lane >= h * HEAD_DIM) & (lane < (h + 1) * HEAD_DIM) for h in range(HEADS_PER_GROUP)]


def _head_stack(block, hmask):
    zero = jnp.zeros((), block.dtype)
    return jnp.concatenate([jnp.where(hm, block, zero) for hm in hmask], axis=0)


LANES = 128


def _put_residue(slab, val, out_ref, dil, width, col0):
    tm, w = val.shape
    if dil == 1:
        out_ref[:, col0:col0 + w] = val.astype(out_ref.dtype)
        return
    for k in range(w // LANES):
        slab[k] = val[:, k * LANES:(k + 1) * LANES]
    for r in range(dil):
        for k in range(w // LANES):
            c = r * width + col0 + k * LANES
            out_ref[:, c:c + LANES] = slab[k, pl.ds(r, tm // dil, stride=dil), :].astype(out_ref.dtype)


def _get_tokens(slab, in_ref, dil, width, col0, w):
    if dil == 1:
        return in_ref[:, col0:col0 + w].astype(F32)
    rows = in_ref.shape[0]
    for r in range(dil):
        for k in range(w // LANES):
            c = r * width + col0 + k * LANES
            slab[k, pl.ds(r, rows, stride=dil), :] = in_ref[:, c:c + LANES].astype(F32)
    return jnp.concatenate([slab[k] for k in range(w // LANES)], axis=1)


def _rope_tables(seq, rider=None):
    half = HEAD_DIM // 2
    inv_freq = ROPE_THETA ** (-jnp.arange(half, dtype=F32) / half)
    freq = jnp.tile(inv_freq, LANES // half).reshape(1, LANES)
    tm = 512

    def body(f_ref, *refs):
        outs, slab_c, slab_s = refs[:-2], refs[-2], refs[-1]
        row = lax.broadcasted_iota(jnp.int32, (tm, LANES), 0) + pl.program_id(0) * tm
        lane = lax.broadcasted_iota(jnp.int32, (tm, LANES), 1)
        ang = row.astype(F32) * f_ref[...]
        cos = jnp.cos(ang)
        sin = jnp.where((lane % HEAD_DIM) < half, -jnp.sin(ang), jnp.sin(ang))
        slab_c[0] = cos
        slab_s[0] = sin
        for i, dil in enumerate(DILATIONS):
            for tab, slab in ((outs[2 * i], slab_c), (outs[2 * i + 1], slab_s)):
                for r in range(dil):
                    piece = slab[0, pl.ds(r, tm // dil, stride=dil), :] if dil > 1 else slab[0]
                    for k in range(GROUP_W // LANES):
                        tab[:, r * GROUP_W + k * LANES:r * GROUP_W + (k + 1) * LANES] = piece

    outs, riding = _call(
        body, name="rope_tables", grid=(seq // tm,),
        in_specs=[pl.BlockSpec((1, LANES), lambda i: (0, 0))],
        out_specs=[pl.BlockSpec((tm // d, d * GROUP_W), lambda i: (i, 0)) for d in DILATIONS for _ in range(2)],
        out_shape=[jax.ShapeDtypeStruct((seq // d, d * GROUP_W), F32) for d in DILATIONS for _ in range(2)],
        scratch_shapes=[pltpu.VMEM((1, tm, LANES), F32)] * 2,
        params=_params(("arbitrary",), 32), args=(freq,), rider=rider)
    return {d: (outs[2 * i], outs[2 * i + 1]) for i, d in enumerate(DILATIONS)}, riding


def _norm_in(x, g0, rider=None):
    seq = x.shape[0]
    tm = 256

    def body(x_ref, g_ref, *refs):
        h_refs, slab = refs[:N_GROUPS], refs[-1]
        xv = x_ref[...]
        hf = (xv * _rsqrt_ms(xv)) * g_ref[...]
        for g, dil in enumerate(DILATIONS):
            _put_residue(slab, hf, h_refs[g], dil, D_MODEL, 0)

    return _call(
        body, name="norm_in", grid=(seq // tm,),
        in_specs=[pl.BlockSpec((tm, D_MODEL), lambda i: (i, 0)), pl.BlockSpec((1, D_MODEL), lambda i: (0, 0))],
        out_specs=[pl.BlockSpec((tm // d, d * D_MODEL), lambda i: (i, 0)) for d in DILATIONS],
        out_shape=[jax.ShapeDtypeStruct((seq // d, d * D_MODEL), BF16) for d in DILATIONS],
        scratch_shapes=[pltpu.VMEM((D_MODEL // LANES, tm, LANES), F32)],
        params=_params(("arbitrary",), 32), args=(x, g0), rider=rider)


def _in_proj(h, w_in, cos_t, sin_t, rider=None):
    seq = h.shape[0]
    tm, tn = 512, GROUP_W
    n_qk = 2 * ATTN_W // tn
    n_qkv = QKV_W // tn

    def body(h_ref, w_ref, cos_ref, sin_ref, *refs):
        qkv_refs, rest_ref, slab = refs[:N_GROUPS], refs[N_GROUPS], refs[-1]
        hb = h_ref[...]
        cos, sin = cos_ref[...], sin_ref[...]
        for j in range(IN_W // tn):
            p = _dot_nt(hb, w_ref[j * tn:(j + 1) * tn, :])
            if j < n_qkv:
                if j < n_qk:
                    p = p * cos + _rot_half(p) * sin
                section, g = divmod(j, N_GROUPS)
                _put_residue(slab, p, qkv_refs[g], DILATIONS[g], 3 * GROUP_W, section * GROUP_W)
            else:
                rest_ref[:, (j - n_qkv) * tn:(j - n_qkv + 1) * tn] = p.astype(BF16)

    return _call(
        body, name="in_proj", grid=(seq // tm,),
        in_specs=[pl.BlockSpec((tm, D_MODEL), lambda i: (i, 0)),
                  _resident((IN_W, D_MODEL)),
                  pl.BlockSpec((tm, GROUP_W), lambda i: (i, 0)),
                  pl.BlockSpec((tm, GROUP_W), lambda i: (i, 0))],
        out_specs=[pl.BlockSpec((tm // d, d * 3 * GROUP_W), lambda i: (i, 0)) for d in DILATIONS]
        + [pl.BlockSpec((tm, REST_W), lambda i: (i, 0))],
        out_shape=[jax.ShapeDtypeStruct((seq // d, d * 3 * GROUP_W), BF16) for d in DILATIONS]
        + [jax.ShapeDtypeStruct((seq, REST_W), BF16)],
        scratch_shapes=[pltpu.VMEM((GROUP_W // LANES, tm, LANES), F32)],
        params=_params(("arbitrary",), 48), args=(h, w_in, cos_t, sin_t), rider=rider)


def _band_masks():
    qi = lax.broadcasted_iota(jnp.int32, (QBLK, QBLK), 0)
    kj = lax.broadcasted_iota(jnp.int32, (QBLK, QBLK), 1)
    return kj <= qi, kj >= qi


def _attn_tile(length):
    return min(512, length)


def _attn_fwd(qkv, dil, rider=None):
    length = qkv.shape[0]
    tq = _attn_tile(length)
    nsub = tq // QBLK
    nblk = length // tq

    def body(q_ref, k_ref, v_ref, kp_ref, vp_ref, o_ref, l_ref):
        n = pl.program_id(1)
        mask_c, mask_p0 = _band_masks()
        hmask = _head_masks((QBLK, GROUP_W))
        zero = jnp.zeros((), BF16)
        for b in range(nsub):
            rows = slice(b * QBLK, (b + 1) * QBLK)
            q = q_ref[rows, :]
            kc, vc = k_ref[rows, :], v_ref[rows, :]
            if b == 0:
                kp, vp = kp_ref[...], vp_ref[...]
                mask_p = mask_p0 & (n > 0)
            else:
                prow = slice((b - 1) * QBLK, b * QBLK)
                kp, vp = k_ref[prow, :], v_ref[prow, :]
                mask_p = mask_p0
            o_acc = jnp.zeros((QBLK, GROUP_W), F32)
            l_acc = jnp.zeros((QBLK, GROUP_W), F32)
            for h in range(HEADS_PER_GROUP):
                hm = hmask[h]
                sc = jnp.where(mask_c, _dot_nt(q, jnp.where(hm, kc, zero)) * SCALE, NEG)
                sp = jnp.where(mask_p, _dot_nt(q, jnp.where(hm, kp, zero)) * SCALE, NEG)
                m = jnp.maximum(jnp.max(sc, axis=-1, keepdims=True), jnp.max(sp, axis=-1, keepdims=True))
                pc, pp = jnp.exp(sc - m), jnp.exp(sp - m)
                den = jnp.sum(pc, axis=-1, keepdims=True) + jnp.sum(pp, axis=-1, keepdims=True)
                pv = _dot(pc.astype(BF16), jnp.where(hm, vc, zero)) + _dot(pp.astype(BF16), jnp.where(hm, vp, zero))
                o_acc = o_acc + pv / den
                l_acc = l_acc + jnp.where(hm, m + jnp.log(den), 0.0)
            o_ref[rows, :] = o_acc
            l_ref[rows, :] = l_acc

    cur = lambda sec: pl.BlockSpec((tq, GROUP_W), lambda r, n: (n, r * 3 + sec))
    prev = lambda sec: pl.BlockSpec((QBLK, GROUP_W), lambda r, n: (jnp.maximum(n * nsub - 1, 0), r * 3 + sec))
    return _call(
        body, name=f"attn_fwd_d{dil}", grid=(dil, nblk),
        in_specs=[cur(0), cur(1), cur(2), prev(1), prev(2)],
        out_specs=[pl.BlockSpec((tq, GROUP_W), lambda r, n: (n, r))] * 2,
        out_shape=[jax.ShapeDtypeStruct((length, dil * GROUP_W), F32)] * 2, scratch_shapes=[],
        params=_params(("arbitrary", "arbitrary"), 32), args=(qkv, qkv, qkv, qkv, qkv), rider=rider)


def _attn_bwd(qkv, dy, y, lse, cos_t, sin_t, dil, rider=None):
    length = qkv.shape[0]
    tq = _attn_tile(length)
    nsub = tq // QBLK
    nblk = length // tq

    def body(q_ref, k_ref, v_ref, kp_ref, vp_ref, qn_ref, dy_ref, y_ref, l_ref, dyn_ref, yn_ref, ln_ref,
             cos_ref, sin_ref, out_ref, dq_s, dk_s, dv_s):
        n = pl.program_id(1)
        mask_c, mask_p0 = _band_masks()
        hmask = _head_masks((QBLK, GROUP_W))
        sub = lambda ref, b: ref[b * QBLK:(b + 1) * QBLK, :]
        kbd = [_head_stack(kp_ref[...], hmask)] + [_head_stack(sub(k_ref, b), hmask) for b in range(nsub)]
        vbd = [_head_stack(vp_ref[...], hmask)] + [_head_stack(sub(v_ref, b), hmask) for b in range(nsub)]
        dk_s[...] = jnp.zeros(dk_s.shape, F32)
        dv_s[...] = jnp.zeros(dv_s.shape, F32)

        def query_block(q, dyv, yv, lv, key_blocks):
            dyb = dyv.astype(BF16)
            qbd = _head_stack(q, hmask)
            dybd = jnp.concatenate([jnp.where(hm, dyv, 0.0).astype(BF16) for hm in hmask], axis=0)
            prod = dyv * yv
            deltas = [jnp.sum(jnp.where(hm, prod, 0.0), axis=-1, keepdims=True) for hm in hmask]
            lses = [jnp.max(jnp.where(hm, lv, NEG), axis=-1, keepdims=True) for hm in hmask]
            dq = jnp.zeros((QBLK, GROUP_W), F32)
            for kb, mask in key_blocks:
                s = _dot_nt(q, kbd[kb]) * SCALE
                dp = _dot_nt(dyb, vbd[kb])
                ps, dss = [], []
                for h in range(HEADS_PER_GROUP):
                    cols = slice(h * QBLK, (h + 1) * QBLK)
                    p = jnp.exp(jnp.where(mask, s[:, cols] - lses[h], NEG))
                    ps.append(p.astype(BF16))
                    dss.append((p * (dp[:, cols] - deltas[h])).astype(BF16))
                dq = dq + _dot(jnp.concatenate(dss, axis=1), kbd[kb])
                if kb >= 1:
                    krows = slice((kb - 1) * QBLK, kb * QBLK)
                    dv_s[krows, :] += _dot_tn(jnp.concatenate(ps, axis=0), dybd)
                    dk_s[krows, :] += _dot_tn(jnp.concatenate(dss, axis=0), qbd) * SCALE
            return dq * SCALE

        for b in range(nsub):
            mask_p = mask_p0 & (n > 0) if b == 0 else mask_p0
            dq_s[b * QBLK:(b + 1) * QBLK, :] = query_block(sub(q_ref, b), sub(dy_ref, b), sub(y_ref, b), sub(l_ref, b),
                                                            [(b, mask_p), (b + 1, mask_c)])
        query_block(qn_ref[...], dyn_ref[...], yn_ref[...], ln_ref[...], [(nsub, mask_p0 & (n < nblk - 1))])
        cos, sin = cos_ref[...], sin_ref[...]
        dq, dk = dq_s[...], dk_s[...]
        out_ref[:, 0:GROUP_W] = (dq * cos - _rot_half(dq) * sin).astype(BF16)
        out_ref[:, GROUP_W:2 * GROUP_W] = (dk * cos - _rot_half(dk) * sin).astype(BF16)
        out_ref[:, 2 * GROUP_W:3 * GROUP_W] = dv_s[...].astype(BF16)

    cur = lambda sec: pl.BlockSpec((tq, GROUP_W), lambda r, n: (n, r * 3 + sec))
    prev = lambda sec: pl.BlockSpec((QBLK, GROUP_W), lambda r, n: (jnp.maximum(n * nsub - 1, 0), r * 3 + sec))
    nxt_q = pl.BlockSpec((QBLK, GROUP_W), lambda r, n: (jnp.minimum((n + 1) * nsub, nblk * nsub - 1), r * 3))
    tok = pl.BlockSpec((tq, GROUP_W), lambda r, n: (n, r))
    tok_next = pl.BlockSpec((QBLK, GROUP_W), lambda r, n: (jnp.minimum((n + 1) * nsub, nblk * nsub - 1), r))
    (out,), riding = _call(
        body, name=f"attn_bwd_d{dil}", grid=(dil, nblk),
        in_specs=[cur(0), cur(1), cur(2), prev(1), prev(2), nxt_q,
                  tok, tok, tok, tok_next, tok_next, tok_next, tok, tok],
        out_specs=[pl.BlockSpec((tq, 3 * GROUP_W), lambda r, n: (n, r))],
        out_shape=[jax.ShapeDtypeStruct((length, dil * 3 * GROUP_W), BF16)],
        scratch_shapes=[pltpu.VMEM((tq, GROUP_W), F32)] * 3,
        params=_params(("arbitrary", "arbitrary"), 32),
        args=(qkv, qkv, qkv, qkv, qkv, qkv, dy, y, lse, dy, y, lse, cos_t, sin_t), rider=rider)
    return out, riding


def _layernorm_stats(z):
    mu = jnp.mean(z, axis=-1, keepdims=True)
    zc = z - mu
    rstd = lax.rsqrt(jnp.mean(zc * zc, axis=-1, keepdims=True) + EPS)
    return zc * rstd, rstd


def _tril_mask():
    row = lax.broadcasted_iota(jnp.int32, (CHUNK, CHUNK), 0)
    col = lax.broadcasted_iota(jnp.int32, (CHUNK, CHUNK), 1)
    return col <= row


def _mix_fwd(o_l, rest, x, w_sp, b_col, ln_g, ln_b, w_ba, w_bg, w_out, g1, rider=None):
    seq = x.shape[0]
    tm = 256

    def body(o0, l0, o1, l1, o2, l2, up_ref, zp_ref, gap_ref, gbp_ref, x_ref, wsp_ref, bcol_ref, lg_ref, lb_ref,
             wba_ref, wbg_ref, wout_ref, g1_ref, ya0, lj0, ya1, lj1, ya2, lj2, yg_ref, mg_ref, y_ref, x1_ref, slab):
        outs = [_get_tokens(slab, o, d, GROUP_W, 0, GROUP_W) for o, d in zip((o0, o1, o2), DILATIONS)]
        lses = [_get_tokens(slab, l, d, GROUP_W, 0, GROUP_W) for l, d in zip((l0, l1, l2), DILATIONS)]
        m = jnp.maximum(jnp.maximum(lses[0], lses[1]), lses[2])
        es = [jnp.exp(l - m) for l in lses]
        tot = es[0] + es[1] + es[2]
        ya = (es[0] * outs[0] + es[1] * outs[1] + es[2] * outs[2]) / tot
        lj = m + jnp.log(tot)
        for ya_ref, lj_ref, d in zip((ya0, ya1, ya2), (lj0, lj1, lj2), DILATIONS):
            _put_residue(slab, ya, ya_ref, d, GROUP_W, 0)
            _put_residue(slab, lj, lj_ref, d, GROUP_W, 0)
        zhat, _ = _layernorm_stats(_gelu(zp_ref[...].astype(F32)))
        zln = (zhat * lg_ref[...] + lb_ref[...]).astype(BF16)
        u = _gelu(up_ref[...].astype(F32))
        tril = _tril_mask()
        for g in range(GMLP_GROUPS):
            wm = jnp.where(tril, wsp_ref[g], 0.0).astype(BF16)
            cols = slice(g * CHUNK, (g + 1) * CHUNK)
            for c in range(tm // CHUNK):
                rows = slice(c * CHUNK, (c + 1) * CHUNK)
                sz = _dot(wm, zln[rows, cols]) + bcol_ref[g]
                yg_ref[rows, cols] = (u[rows, cols] * sz).astype(BF16)
        a = _dot(ya.astype(BF16), wba_ref[...])
        bm = _dot(yg_ref[...], wbg_ref[...])
        merged = (jax.nn.sigmoid(gap_ref[...].astype(F32)) * a + jax.nn.sigmoid(gbp_ref[...].astype(F32)) * bm).astype(BF16)
        mg_ref[...] = merged
        yv = _dot(merged, wout_ref[...])
        y_ref[...] = yv
        x1_ref[...] = x_ref[...] + (yv * _rsqrt_ms(yv)) * g1_ref[...]

    tok = lambda w: pl.BlockSpec((tm, w), lambda i: (i, 0))
    res = lambda d: pl.BlockSpec((tm // d, d * GROUP_W), lambda i: (i, 0))
    full = lambda *s: pl.BlockSpec(s, lambda i: (0,) * len(s))
    res_specs = [res(d) for d in DILATIONS for _ in range(2)]
    return _call(
        body, name="mix_fwd", grid=(seq // tm,),
        in_specs=res_specs + [
            pl.BlockSpec((tm, GMLP_W), lambda i: (i, 0)), pl.BlockSpec((tm, GMLP_W), lambda i: (i, 1)),
            pl.BlockSpec((tm, D_MODEL), lambda i: (i, 1)), pl.BlockSpec((tm, D_MODEL), lambda i: (i, 2)),
            tok(D_MODEL), full(GMLP_GROUPS, CHUNK, CHUNK), full(GMLP_GROUPS, CHUNK, 1), full(1, GMLP_W), full(1, GMLP_W),
            full(GROUP_W, D_MODEL), full(GMLP_W, D_MODEL), full(D_MODEL, D_MODEL), full(1, D_MODEL)],
        out_specs=res_specs + [tok(GMLP_W), tok(D_MODEL), tok(D_MODEL), tok(D_MODEL)],
        out_shape=[jax.ShapeDtypeStruct((seq // d, d * GROUP_W), F32) for d in DILATIONS for _ in range(2)]
        + [jax.ShapeDtypeStruct((seq, GMLP_W), BF16), jax.ShapeDtypeStruct((seq, D_MODEL), BF16),
           jax.ShapeDtypeStruct((seq, D_MODEL), F32), jax.ShapeDtypeStruct((seq, D_MODEL), F32)],
        scratch_shapes=[pltpu.VMEM((GROUP_W // LANES, tm, LANES), F32)],
        params=_params(("arbitrary",), 48),
        args=(*o_l, rest, rest, rest, rest, x, w_sp, b_col, ln_g, ln_b, w_ba, w_bg, w_out, g1), rider=rider)


def _mlp_fwd(x1, g2, g3, w_mi, w_mo, target):
    seq = x1.shape[0]
    tm, tf = MLP_TM, 512

    def body(x1_ref, g2_ref, g3_ref, wmi_ref, wmo_ref, t_ref, h2_ref, a_ref, dy2_ref, dout_ref, loss_ref, dg3_ref, sq_s):
        @pl.when(pl.program_id(0) == 0)
        def _():
            loss_ref[...] = jnp.zeros(loss_ref.shape, F32)
            dg3_ref[...] = jnp.zeros(dg3_ref.shape, F32)

        xv = x1_ref[...]
        hb = ((xv * _rsqrt_ms(xv)) * g2_ref[...]).astype(BF16)
        h2_ref[...] = hb
        for j in range(D_FF // tf):
            cols = slice(j * tf, (j + 1) * tf)
            a = jnp.maximum(_dot(hb, wmi_ref[:, cols]), 0.0)
            a_ref[:, cols] = a.astype(BF16)
            sq_s[:, cols] = (a * a).astype(BF16)
        y2 = _dot(sq_s[...], wmo_ref[...])
        r3 = _rsqrt_ms(y2)
        out = xv + (y2 * r3) * g3_ref[...]
        diff = out - t_ref[...]
        tile_loss = 0.5 * jnp.sum(jnp.mean(diff * diff, axis=-1, keepdims=True), axis=0, keepdims=True)
        loss_ref[...] += jnp.broadcast_to(tile_loss, loss_ref.shape)
        dout = diff * (1.0 / D_MODEL)
        dout_ref[...] = dout
        dy2, dg3 = _rmsnorm_bwd(dout, y2, g3_ref[...])
        dy2_ref[...] = dy2.astype(BF16)
        dg3_ref[...] += dg3

    tok = lambda w: pl.BlockSpec((tm, w), lambda i: (i, 0))
    vec = pl.BlockSpec((1, D_MODEL), lambda i: (0, 0))
    return _pallas(
        body, name="mlp_fwd", grid=(seq // tm,),
        in_specs=[tok(D_MODEL), vec, vec, _resident((D_MODEL, D_FF)), _resident((D_FF, D_MODEL)), tok(D_MODEL)],
        out_specs=[tok(D_MODEL), tok(D_FF), tok(D_MODEL), tok(D_MODEL), pl.BlockSpec((8, 128), lambda i: (0, 0)), vec],
        out_shape=[jax.ShapeDtypeStruct((seq, D_MODEL), BF16), jax.ShapeDtypeStruct((seq, D_FF), BF16),
                   jax.ShapeDtypeStruct((seq, D_MODEL), BF16), jax.ShapeDtypeStruct((seq, D_MODEL), F32),
                   jax.ShapeDtypeStruct((8, 128), F32), jax.ShapeDtypeStruct((1, D_MODEL), F32)],
        scratch_shapes=[pltpu.VMEM((tm, D_FF), BF16)],
        compiler_params=_params(("arbitrary",), 56),
    )(*map(_in_hbm, (x1, g2, g3, w_mi, w_mo, target)))


def _mlp_bwd(dy2, a, w_mo, w_mi, dout, x1, y, g2, g1, rider=None):
    seq = x1.shape[0]
    tm, tf = MLP_TM, 512

    def body(dy2_ref, a_ref, wmo_ref, wmi_ref, dout_ref, x1_ref, y_ref, g2_ref, g1_ref,
             dap_ref, dx1_ref, dy_ref, dg2_ref, dg1_ref):
        @pl.when(pl.program_id(0) == 0)
        def _():
            dg2_ref[...] = jnp.zeros(dg2_ref.shape, F32)
            dg1_ref[...] = jnp.zeros(dg1_ref.shape, F32)

        dy2v = dy2_ref[...]
        for j in range(D_FF // tf):
            cols = slice(j * tf, (j + 1) * tf)
            da2 = _dot_nt(dy2v, wmo_ref[cols, :])
            dap_ref[:, cols] = (da2 * (2.0 * a_ref[:, cols].astype(F32))).astype(BF16)
        dh2 = _dot_nt(dap_ref[...], wmi_ref[...])
        dres, dg2 = _rmsnorm_bwd(dh2, x1_ref[...], g2_ref[...])
        dx1 = dout_ref[...] + dres
        dx1_ref[...] = dx1
        dg2_ref[...] += dg2
        dyv, dg1 = _rmsnorm_bwd(dx1, y_ref[...], g1_ref[...])
        dy_ref[...] = dyv.astype(BF16)
        dg1_ref[...] += dg1

    tok = lambda w: pl.BlockSpec((tm, w), lambda i: (i, 0))
    vec = pl.BlockSpec((1, D_MODEL), lambda i: (0, 0))
    return _call(
        body, name="mlp_bwd", grid=(seq // tm,),
        in_specs=[tok(D_MODEL), tok(D_FF), _resident((D_FF, D_MODEL)), _resident((D_MODEL, D_FF)),
                  tok(D_MODEL), tok(D_MODEL), tok(D_MODEL), vec, vec],
        out_specs=[tok(D_FF), tok(D_MODEL), tok(D_MODEL), vec, vec],
        out_shape=[jax.ShapeDtypeStruct((seq, D_FF), BF16), jax.ShapeDtypeStruct((seq, D_MODEL), F32),
                   jax.ShapeDtypeStruct((seq, D_MODEL), BF16), jax.ShapeDtypeStruct((1, D_MODEL), F32),
                   jax.ShapeDtypeStruct((1, D_MODEL), F32)], scratch_shapes=[],
        params=_params(("arbitrary",), 56), args=(dy2, a, w_mo, w_mi, dout, x1, y, g2, g1), rider=rider)


def _tn_matmul(a, b, name, bm, bn, square_a=False, column_shards=False, rider=None):
    seq, m = a.shape
    n = b.shape[1]
    ts = 2048

    def body(a_ref, b_ref, o_ref):
        @pl.when(pl.program_id(2) == 0)
        def _():
            o_ref[...] = jnp.zeros(o_ref.shape, F32)

        av = a_ref[...]
        if square_a:
            af = av.astype(F32)
            av = (af * af).astype(BF16)
        o_ref[...] += _dot_tn(av, b_ref[...])

    if column_shards:
        out_spec = pl.BlockSpec((None, bm, bn), lambda mi, ni, s: (ni, mi, 0))
        out_shape = jax.ShapeDtypeStruct((n // bn, m, bn), F32)
    else:
        out_spec = pl.BlockSpec((bm, bn), lambda mi, ni, s: (mi, ni))
        out_shape = jax.ShapeDtypeStruct((m, n), F32)
    (out,), riding = _call(
        body, name=name, grid=(m // bm, n // bn, seq // ts),
        in_specs=[pl.BlockSpec((ts, bm), lambda mi, ni, s: (s, mi)), pl.BlockSpec((ts, bn), lambda mi, ni, s: (s, ni))],
        out_specs=[out_spec], out_shape=[out_shape], scratch_shapes=[],
        params=_params(("arbitrary", "arbitrary", "arbitrary"), 40), args=(a, b), rider=rider)
    return out, riding


def _tn_matmul_residue(a, b, dil, name):
    length = a.shape[0]
    m, n = a.shape[1] // dil, b.shape[1] // dil
    ts = min(1024, length)

    def body(a_ref, b_ref, o_ref):
        @pl.when((pl.program_id(0) == 0) & (pl.program_id(1) == 0))
        def _():
            o_ref[...] = jnp.zeros(o_ref.shape, F32)

        o_ref[...] += _dot_tn(a_ref[...], b_ref[...])

    return _pallas(
        body, name=name, grid=(dil, length // ts),
        in_specs=[pl.BlockSpec((ts, m), lambda r, s: (s, r)), pl.BlockSpec((ts, n), lambda r, s: (s, r))],
        out_specs=pl.BlockSpec((m, n), lambda r, s: (0, 0)),
        out_shape=jax.ShapeDtypeStruct((m, n), F32),
        compiler_params=_params(("arbitrary", "arbitrary"), 40),
    )(_in_hbm(a), _in_hbm(b))


def _mix_bwd(dy, ya, yg, mg, rest, w_out, w_ba, w_bg, w_sp, b_col, ln_g, ln_b, rider=None):
    seq = dy.shape[0]
    tm = 256

    def body(dy_ref, ya_ref, yg_ref, mg_ref, up_ref, zp_ref, gap_ref, gbp_ref, wout_ref, wba_ref, wbg_ref,
             wsp_ref, bcol_ref, lg_ref, lb_ref,
             dya0, dya1, dya2, dpr_ref, dwout_ref, dwba_ref, dwbg_ref, dwsp_ref, dbb_ref, dlg_ref, dlb_ref,
             dzln_s, du_s, slab):
        @pl.when(pl.program_id(0) == 0)
        def _():
            for ref in (dwout_ref, dwba_ref, dwbg_ref, dwsp_ref, dbb_ref, dlg_ref, dlb_ref):
                ref[...] = jnp.zeros(ref.shape, F32)

        dyv = dy_ref[...]
        dm = _dot_nt(dyv, wout_ref[...])
        dwout_ref[...] += _dot_tn(mg_ref[...], dyv)
        yab = ya_ref[...].astype(BF16)
        ygb = yg_ref[...]
        a = _dot(yab, wba_ref[...])
        bm = _dot(ygb, wbg_ref[...])
        ga = jax.nn.sigmoid(gap_ref[...].astype(F32))
        gb = jax.nn.sigmoid(gbp_ref[...].astype(F32))
        dpr_ref[:, 2 * GMLP_W:2 * GMLP_W + D_MODEL] = (dm * a * (ga * (1.0 - ga))).astype(BF16)
        dpr_ref[:, 2 * GMLP_W + D_MODEL:REST_W] = (dm * bm * (gb * (1.0 - gb))).astype(BF16)
        da = (dm * ga).astype(BF16)
        db = (dm * gb).astype(BF16)
        dwba = _dot_tn(yab, da)
        dwbg = _dot_tn(ygb, db)
        shard_w = D_MODEL // N_CHIPS
        for j in range(N_CHIPS):
            dwba_ref[j] += dwba[:, j * shard_w:(j + 1) * shard_w]
            dwbg_ref[j] += dwbg[:, j * shard_w:(j + 1) * shard_w]
        dya = _dot_nt(da, wba_ref[...])
        for dya_ref, d in zip((dya0, dya1, dya2), DILATIONS):
            _put_residue(slab, dya, dya_ref, d, GROUP_W, 0)
        dyg = _dot_nt(db, wbg_ref[...])

        zp = zp_ref[...].astype(F32)
        zhat, rstd = _layernorm_stats(_gelu(zp))
        lg = lg_ref[...]
        zln = (zhat * lg + lb_ref[...]).astype(BF16)
        up = up_ref[...].astype(F32)
        u = _gelu(up)
        tril = _tril_mask()
        for g in range(GMLP_GROUPS):
            wm = jnp.where(tril, wsp_ref[g], 0.0).astype(BF16)
            cols = slice(g * CHUNK, (g + 1) * CHUNK)
            for c in range(tm // CHUNK):
                rows = slice(c * CHUNK, (c + 1) * CHUNK)
                zb = zln[rows, cols]
                sz = _dot(wm, zb) + bcol_ref[g]
                dyg_cg = dyg[rows, cols]
                du_s[rows, cols] = dyg_cg * sz
                dsz = dyg_cg * u[rows, cols]
                dszb = dsz.astype(BF16)
                dbb_ref[g] += jnp.broadcast_to(jnp.sum(dsz, axis=-1, keepdims=True), (CHUNK, CHUNK))
                dwsp_ref[g] += jnp.where(tril, _dot_nt(dszb, zb), 0.0)
                dzln_s[rows, cols] = _dot_tn(wm, dszb)
        dzln = dzln_s[...]
        dlg_ref[...] += jnp.sum(dzln * zhat, axis=0, keepdims=True)
        dlb_ref[...] += jnp.sum(dzln, axis=0, keepdims=True)
        dzh = dzln * lg
        dz = rstd * (dzh - jnp.mean(dzh, axis=-1, keepdims=True) - zhat * jnp.mean(dzh * zhat, axis=-1, keepdims=True))
        dpr_ref[:, GMLP_W:2 * GMLP_W] = (dz * _gelu_grad(zp)).astype(BF16)
        dpr_ref[:, 0:GMLP_W] = (du_s[...] * _gelu_grad(up)).astype(BF16)

    tok = lambda w: pl.BlockSpec((tm, w), lambda i: (i, 0))
    full = lambda *s: pl.BlockSpec(s, lambda i: (0,) * len(s))
    return _call(
        body, name="mix_bwd", grid=(seq // tm,),
        in_specs=[tok(D_MODEL), tok(GROUP_W), tok(GMLP_W), tok(D_MODEL),
                  pl.BlockSpec((tm, GMLP_W), lambda i: (i, 0)), pl.BlockSpec((tm, GMLP_W), lambda i: (i, 1)),
                  pl.BlockSpec((tm, D_MODEL), lambda i: (i, 1)), pl.BlockSpec((tm, D_MODEL), lambda i: (i, 2)),
                  full(D_MODEL, D_MODEL), full(GROUP_W, D_MODEL), full(GMLP_W, D_MODEL),
                  full(GMLP_GROUPS, CHUNK, CHUNK), full(GMLP_GROUPS, CHUNK, 1), full(1, GMLP_W), full(1, GMLP_W)],
        out_specs=[pl.BlockSpec((tm // d, d * GROUP_W), lambda i: (i, 0)) for d in DILATIONS]
        + [tok(REST_W), full(D_MODEL, D_MODEL), full(N_CHIPS, GROUP_W, D_MODEL // N_CHIPS),
           full(N_CHIPS, GMLP_W, D_MODEL // N_CHIPS),
           full(GMLP_GROUPS, CHUNK, CHUNK), full(GMLP_GROUPS, CHUNK, CHUNK), full(1, GMLP_W), full(1, GMLP_W)],
        out_shape=[jax.ShapeDtypeStruct((seq // d, d * GROUP_W), F32) for d in DILATIONS]
        + [jax.ShapeDtypeStruct((seq, REST_W), BF16),
           jax.ShapeDtypeStruct((D_MODEL, D_MODEL), F32), jax.ShapeDtypeStruct((N_CHIPS, GROUP_W, D_MODEL // N_CHIPS), F32),
           jax.ShapeDtypeStruct((N_CHIPS, GMLP_W, D_MODEL // N_CHIPS), F32),
           jax.ShapeDtypeStruct((GMLP_GROUPS, CHUNK, CHUNK), F32),
           jax.ShapeDtypeStruct((GMLP_GROUPS, CHUNK, CHUNK), F32), jax.ShapeDtypeStruct((1, GMLP_W), F32),
           jax.ShapeDtypeStruct((1, GMLP_W), F32)],
        scratch_shapes=[pltpu.VMEM((tm, GMLP_W), F32), pltpu.VMEM((tm, GMLP_W), F32),
                        pltpu.VMEM((GROUP_W // LANES, tm, LANES), F32)],
        params=_params(("arbitrary",), 56),
        args=(dy, ya, yg, mg, rest, rest, rest, rest, w_out, w_ba, w_bg, w_sp, b_col, ln_g, ln_b), rider=rider)


IN_PROJ_BWD_TM = 256


def _in_proj_bwd(dqkv, drest, w_qkv, w_rest, x, dx1, g0, so_far, span, rider=None):
    seq = x.shape[0]
    tm = IN_PROJ_BWD_TM
    off, steps = span
    gx_so_far, dg_so_far = so_far

    def body(d0, d1, d2, dr_ref, w0, w1, w2, wr_ref, x_ref, dx1_ref, g_ref, dg_in_ref, gx_in_ref, gx_ref, dg_ref, slab):
        @pl.when(pl.program_id(0) == 0)
        def _():
            dg_ref[...] = dg_in_ref[...]

        dh = _dot(dr_ref[...], wr_ref[...])
        for d_ref, w_ref, dil in zip((d0, d1, d2), (w0, w1, w2), DILATIONS):
            piece = d_ref[...] if dil == 1 else _get_tokens(slab, d_ref, dil, 3 * GROUP_W, 0, 3 * GROUP_W).astype(BF16)
            dh = dh + _dot(piece, w_ref[...])
        dres, dg = _rmsnorm_bwd(dh, x_ref[...], g_ref[...])
        gx_ref[...] = dx1_ref[...] + dres
        dg_ref[...] += dg

    tok = lambda w: pl.BlockSpec((tm, w), lambda i: (i + off, 0))
    full = lambda *s: pl.BlockSpec(s, lambda i: (0,) * len(s))
    in_specs = ([pl.BlockSpec((tm // d, d * 3 * GROUP_W), lambda i: (i + off, 0)) for d in DILATIONS] + [tok(REST_W)]
                + [_resident((3 * GROUP_W, D_MODEL))] * 3 + [_resident((REST_W, D_MODEL))]
                + [tok(D_MODEL), tok(D_MODEL), full(1, D_MODEL), full(1, D_MODEL), HBM_SPEC])
    return _call(
        body, name=f"in_proj_bwd_{off}", grid=(steps,), in_specs=in_specs,
        out_specs=[tok(D_MODEL), full(1, D_MODEL)],
        out_shape=[jax.ShapeDtypeStruct((seq, D_MODEL), F32), jax.ShapeDtypeStruct((1, D_MODEL), F32)],
        scratch_shapes=[pltpu.VMEM((3 * GROUP_W // LANES, tm, LANES), F32)],
        params=_params(("arbitrary",), 48), args=(*dqkv, drest, *w_qkv, w_rest, x, dx1, g0, dg_so_far, gx_so_far),
        rider=rider, aliases={len(in_specs) - 1: 0})


def _adamw(w, g, m, v, name):
    rows, cols = w.shape
    tr = _row_tile(rows) if rows % 16 == 0 else rows
    c1 = 1.0 - ADAM_B1 ** ADAM_STEP
    c2 = 1.0 - ADAM_B2 ** ADAM_STEP

    def body(w_ref, g_ref, m_ref, v_ref, go_ref, d_ref, nm_ref, nv_ref):
        gv = g_ref[...]
        go_ref[...] = gv
        nm = ADAM_B1 * m_ref[...] + (1.0 - ADAM_B1) * gv
        nv = ADAM_B2 * v_ref[...] + (1.0 - ADAM_B2) * (gv * gv)
        d_ref[...] = -ADAM_LR * ((nm / c1) / (jnp.sqrt(nv / c2) + ADAM_EPS) + ADAM_WD * w_ref[...])
        nm_ref[...] = nm
        nv_ref[...] = nv

    spec = pl.BlockSpec((tr, cols), lambda i: (i, 0))
    return _pallas(
        body, name=name, grid=(rows // tr,),
        in_specs=[spec] * 4, out_specs=[spec] * 4,
        out_shape=[jax.ShapeDtypeStruct((rows, cols), F32)] * 4,
        compiler_params=_params(("arbitrary",), 40),
    )(*map(_in_hbm, (w, g, m, v)))


def _place():
    x, y, c = lax.axis_index("x"), lax.axis_index("y"), lax.axis_index("c")
    chips = [(1 - x, y), (x, 1 - y), (1 - x, 1 - y)]
    return x, y, c, chips


class _Exchange:
    def __init__(self, inputs, out_shapes, n_sems, start, finish, aliases=None):
        self.inputs, self.out_shapes, self.n_sems = list(inputs), list(out_shapes), n_sems
        self.start, self.finish, self.aliases = start, finish, dict(aliases or {})

    def scratch(self):
        return [pltpu.SemaphoreType.DMA((self.n_sems,)), pltpu.SemaphoreType.DMA((self.n_sems,))]


def _together(*parts):
    ins = [len(p.inputs) for p in parts]
    outs = [len(p.out_shapes) for p in parts]

    def split(refs, counts):
        pos, pieces = 0, []
        for cnt in counts:
            pieces.append(refs[pos:pos + cnt])
            pos += cnt
        return pieces

    def run(which):
        def go(in_refs, out_refs, *sems):
            for k, (p, i, o) in enumerate(zip(parts, split(in_refs, ins), split(out_refs, outs))):
                getattr(p, which)(i, o, sems[2 * k], sems[2 * k + 1])
        return go

    both = _Exchange([a for p in parts for a in p.inputs], [s for p in parts for s in p.out_shapes], 0, run("start"),
                     run("finish"))
    both.aliases = {sum(ins[:k]) + i: sum(outs[:k]) + o for k, p in enumerate(parts) for i, o in p.aliases.items()}
    both.scratch = lambda: [s for p in parts for s in p.scratch()]
    return both


def _run_exchange(ex, name):
    n_in, n_out = len(ex.inputs), len(ex.out_shapes)

    def body(*refs):
        ins, outs, sems = refs[:n_in], refs[n_in:n_in + n_out], refs[n_in + n_out:]
        ex.start(ins, outs, *sems)
        ex.finish(ins, outs, *sems)

    return _pallas(
        body, name=name, in_specs=[HBM_SPEC] * n_in, out_specs=[HBM_SPEC] * n_out, out_shape=ex.out_shapes,
        scratch_shapes=ex.scratch(), input_output_aliases=ex.aliases,
    )(*ex.inputs)


def _call(body, *, name, grid, in_specs, out_specs, out_shape, scratch_shapes, params, args, rider=None, aliases=None):
    in_specs, out_specs, out_shape, scratch_shapes = list(in_specs), list(out_specs), list(out_shape), list(scratch_shapes)
    aliases = dict(aliases or {})
    args = [_in_hbm(a) for a in args]
    if rider is None:
        outs = _pallas(body, name=name, grid=grid, in_specs=in_specs, out_specs=out_specs, out_shape=out_shape,
                              scratch_shapes=scratch_shapes, input_output_aliases=aliases, compiler_params=params)(*args)
        return list(outs), []
    n_in, n_out, n_scr = len(in_specs), len(out_specs), len(scratch_shapes)
    r_in, r_out = len(rider.inputs), len(rider.out_shapes)

    def wrapped(*refs):
        ins, r_ins = refs[:n_in], refs[n_in:n_in + r_in]
        pos = n_in + r_in
        outs, r_outs = refs[pos:pos + n_out], refs[pos + n_out:pos + n_out + r_out]
        pos += n_out + r_out
        scr, sems = refs[pos:pos + n_scr], refs[pos + n_scr:]
        ids = [pl.program_id(k) for k in range(len(grid))]
        first, last = ids[0] == 0, ids[0] == grid[0] - 1
        for k in range(1, len(grid)):
            first, last = first & (ids[k] == 0), last & (ids[k] == grid[k] - 1)

        @pl.when(first)
        def _():
            rider.start(r_ins, r_outs, *sems)

        body(*ins, *outs, *scr)

        @pl.when(last)
        def _():
            rider.finish(r_ins, r_outs, *sems)

    outs = _pallas(
        wrapped, name=name, grid=grid, in_specs=in_specs + [HBM_SPEC] * r_in, out_specs=out_specs + [HBM_SPEC] * r_out,
        out_shape=out_shape + rider.out_shapes, scratch_shapes=scratch_shapes + rider.scratch(),
        input_output_aliases={**aliases, **{n_in + i: n_out + o for i, o in rider.aliases.items()}}, compiler_params=params,
    )(*args, *rider.inputs)
    return list(outs[:n_out]), list(outs[n_out:])


def _stage_weights(shards):
    n = len(shards)

    def body(*refs):
        ins, outs, stages, sems = refs[:n], refs[n:2 * n], refs[2 * n:3 * n], refs[3 * n]
        x, y, _, _ = _place()
        copies = []
        for t in range(n):
            stages[t][...] = ins[t][...].astype(BF16)
            copies.append(pltpu.make_async_copy(stages[t], outs[t].at[2 * x + y], sems.at[t]))
            copies[-1].start()
        for cp in copies:
            cp.wait()

    stage_bytes = sum(s.size * 6 for s in shards)
    return _pallas(
        body, name="stage_weights", in_specs=[VMEM_SPEC] * n, out_specs=[HBM_SPEC] * n,
        out_shape=[jax.ShapeDtypeStruct((N_CHIPS,) + s.shape, BF16) for s in shards],
        scratch_shapes=[pltpu.VMEM(s.shape, BF16) for s in shards] + [pltpu.SemaphoreType.DMA((n,))],
        compiler_params=pltpu.CompilerParams(vmem_limit_bytes=stage_bytes + 8 * MIB),
    )(*shards)


def _gather(buffers, stage="both", part=(0, 1)):
    n = len(buffers)
    halves = [b.shape[1] // part[1] // 2 for b in buffers]

    def half_of(outs, t, chip, which):
        return outs[t].at[chip, pl.ds((2 * part[0] + which) * halves[t], halves[t]), :]

    def copy(outs, sems, t, k, chip, which, to):
        rows = half_of(outs, t, chip, which)
        return pltpu.make_async_remote_copy(src_ref=rows, dst_ref=rows, send_sem=sems[0].at[6 * t + k],
                                            recv_sem=sems[1].at[6 * t + k], device_id=to, device_id_type=MESH)

    def to_chips(outs, sems, what):
        x, y, c, chips = _place()
        for t in range(n):
            for j, (px, py) in enumerate(chips):
                if what == "start":
                    copy(outs, sems, t, j, 2 * x + y, c, (px, py, c)).start()
                else:
                    copy(outs, sems, t, j, 2 * px + py, c, (px, py, c)).wait_recv()
                    copy(outs, sems, t, j, 2 * x + y, c, (px, py, c)).wait_send()

    def to_sibling(outs, sems, what):
        x, y, c, chips = _place()
        for t in range(n):
            for j, (px, py) in enumerate(chips):
                if what == "start":
                    copy(outs, sems, t, 3 + j, 2 * px + py, c, (x, y, 1 - c)).start()
                else:
                    copy(outs, sems, t, 3 + j, 2 * px + py, 1 - c, (x, y, 1 - c)).wait_recv()
                    copy(outs, sems, t, 3 + j, 2 * px + py, c, (x, y, 1 - c)).wait_send()

    def start(ins, outs, *sems):
        (to_sibling if stage == "pair" else to_chips)(outs, sems, "start")

    def finish(ins, outs, *sems):
        if stage != "pair":
            to_chips(outs, sems, "finish")
        if stage == "both":
            to_sibling(outs, sems, "start")
        if stage != "chips":
            to_sibling(outs, sems, "finish")

    return _Exchange(buffers, [jax.ShapeDtypeStruct(b.shape, b.dtype) for b in buffers], 6 * n, start, finish,
                     aliases={t: t for t in range(n)})


def _pair_exchange(grads):
    n = len(grads)
    halves = [g.shape[1] // 2 for g in grads]

    def copies(ins, outs, send_sems, recv_sems):
        x, y, c, _ = _place()
        return [pltpu.make_async_remote_copy(
            src_ref=ins[t].at[:, pl.ds((1 - c) * halves[t], halves[t]), :], dst_ref=outs[t],
            send_sem=send_sems.at[t], recv_sem=recv_sems.at[t], device_id=(x, y, 1 - c), device_id_type=MESH)
            for t in range(n)]

    def start(*refs):
        for cp in copies(*refs):
            cp.start()

    def finish(*refs):
        for cp in copies(*refs):
            cp.wait()

    return _Exchange(grads, [jax.ShapeDtypeStruct((N_CHIPS, h, g.shape[2]), F32) for g, h in zip(grads, halves)], n,
                     start, finish)


def _row_tile(rows):
    return max(t for t in range(16, 257, 16) if rows % t == 0)


def _pair_add(grad, other, place, name):
    _, rows, cols = grad.shape
    rh = rows // 2
    tr = _row_tile(rh)
    nb = rh // tr

    def body(p_ref, g_ref, a_ref, wire_ref, own_ref):
        s = g_ref[...] + a_ref[...]
        wire_ref[...] = s.astype(BF16)

        @pl.when(pl.program_id(1) == p_ref[1])
        def _():
            own_ref[...] = s

    blk = (None, tr, cols)
    return _pallas(
        body, name=name,
        grid_spec=pltpu.PrefetchScalarGridSpec(
            num_scalar_prefetch=1, grid=(nb, N_CHIPS),
            in_specs=[pl.BlockSpec(blk, lambda i, j, p: (j, p[0] * nb + i, 0)), pl.BlockSpec(blk, lambda i, j, p: (j, i, 0))],
            out_specs=[pl.BlockSpec(blk, lambda i, j, p: (j, i, 0)), pl.BlockSpec((tr, cols), lambda i, j, p: (i, 0))]),
        out_shape=[jax.ShapeDtypeStruct((N_CHIPS, rh, cols), BF16), jax.ShapeDtypeStruct((rh, cols), F32)],
        compiler_params=_params(("arbitrary", "arbitrary"), 32),
    )(place, grad, other)


def _chip_exchange(wires):
    n = len(wires)

    def copies(ins, outs, send_sems, recv_sems):
        x, y, c, chips = _place()
        return [pltpu.make_async_remote_copy(
            src_ref=ins[t].at[2 * px + py], dst_ref=outs[t].at[j], send_sem=send_sems.at[3 * t + j],
            recv_sem=recv_sems.at[3 * t + j], device_id=(px, py, c), device_id_type=MESH)
            for t in range(n) for j, (px, py) in enumerate(chips)]

    def start(*refs):
        for cp in copies(*refs):
            cp.start()

    def finish(*refs):
        for cp in copies(*refs):
            cp.wait()

    return _Exchange(wires, [jax.ShapeDtypeStruct((3,) + w.shape[1:], BF16) for w in wires], 3 * n, start, finish)


def _chip_add(own, arrived, place, name):
    rh, cols = own.shape
    tr = _row_tile(rh)
    nb = rh // tr

    def body(p_ref, s_ref, b0, b1, b2, o_ref):
        o_ref[...] = ((s_ref[...] + b0[...].astype(F32)) + b1[...].astype(F32)) + b2[...].astype(F32)

    blk = (None, tr, cols)
    return _pallas(
        body, name=name,
        grid_spec=pltpu.PrefetchScalarGridSpec(
            num_scalar_prefetch=1, grid=(nb,),
            in_specs=[pl.BlockSpec((tr, cols), lambda i, p: (i, 0)), pl.BlockSpec(blk, lambda i, p: (0, i, 0)),
                      pl.BlockSpec(blk, lambda i, p: (1, i, 0)), pl.BlockSpec(blk, lambda i, p: (2, i, 0))],
            out_specs=pl.BlockSpec((tr, cols), lambda i, p: (p[0] * nb + i, 0))),
        out_shape=jax.ShapeDtypeStruct((2 * rh, cols), F32),
        compiler_params=_params(("arbitrary",), 32),
    )(place, own, arrived, arrived, arrived)


def _pair_share(halves):
    n = len(halves)
    rhs = [h.shape[0] // 2 for h in halves]

    def copy(outs, send_sems, recv_sems, t, which):
        x, y, c, _ = _place()
        rows = outs[t].at[pl.ds(which * rhs[t], rhs[t]), :]
        return pltpu.make_async_remote_copy(src_ref=rows, dst_ref=rows, send_sem=send_sems.at[t], recv_sem=recv_sems.at[t],
                                            device_id=(x, y, 1 - c), device_id_type=MESH)

    def start(ins, outs, send_sems, recv_sems):
        c = lax.axis_index("c")
        for t in range(n):
            copy(outs, send_sems, recv_sems, t, c).start()

    def finish(ins, outs, send_sems, recv_sems):
        c = lax.axis_index("c")
        for t in range(n):
            copy(outs, send_sems, recv_sems, t, c).wait_send()
            copy(outs, send_sems, recv_sems, t, 1 - c).wait_recv()

    return _Exchange(halves, [jax.ShapeDtypeStruct(h.shape, F32) for h in halves], n, start, finish,
                     aliases={t: t for t in range(n)})


class _GradReduction:
    def __init__(self, grads, place, tag):
        self.names, self.grads, self.place, self.tag = list(grads), grads, place, tag

    def pair_exchange(self):
        return _pair_exchange([self.grads[n] for n in self.names])

    def chip_exchange(self, others):
        sums = [_pair_add(self.grads[n], o, self.place, f"{self.tag}_pair_add_{n}") for n, o in zip(self.names, others)]
        self.owns = [own for _, own in sums]
        return _chip_exchange([wire for wire, _ in sums])

    def pair_share(self, arrived):
        return _pair_share([_chip_add(own, arr, self.place, f"{self.tag}_chip_add_{n}")
                            for n, own, arr in zip(self.names, self.owns, arrived)])

    def result(self, shared):
        return dict(zip(self.names, shared))


def _all_reduce_small(p):
    rows, lanes = p.shape
    half = rows // 2

    def body(p_ref, o_ref, sib, sums, send_sems, recv_sems):
        x, y, c, chips = _place()
        mine, sibling = 2 * x + y, (x, y, 1 - c)
        swap = pltpu.make_async_remote_copy(src_ref=p_ref, dst_ref=sib, send_sem=send_sems.at[0], recv_sem=recv_sems.at[0],
                                            device_id=sibling, device_id_type=MESH)
        swap.start()
        swap.wait()
        sums[mine] = p_ref[...] + sib[...]

        def copy(k, chip, which, to):
            part = sums.at[chip, pl.ds(which * half, half), :]
            return pltpu.make_async_remote_copy(src_ref=part, dst_ref=part, send_sem=send_sems.at[k], recv_sem=recv_sems.at[k],
                                                device_id=to, device_id_type=MESH)

        for j, (px, py) in enumerate(chips):
            copy(1 + j, mine, c, (px, py, c)).start()
        for j, (px, py) in enumerate(chips):
            copy(1 + j, 2 * px + py, c, (px, py, c)).wait_recv()
            copy(4 + j, 2 * px + py, c, sibling).start()
        for j, (px, py) in enumerate(chips):
            copy(4 + j, 2 * px + py, 1 - c, sibling).wait_recv()
        for j, (px, py) in enumerate(chips):
            copy(1 + j, mine, c, (px, py, c)).wait_send()
            copy(4 + j, 2 * px + py, c, sibling).wait_send()
        o_ref[...] = ((sums[0] + sums[1]) + sums[2]) + sums[3]

    return _pallas(
        body, name="small_all_reduce", in_specs=[VMEM_SPEC], out_specs=VMEM_SPEC,
        out_shape=jax.ShapeDtypeStruct((rows, lanes), F32),
        scratch_shapes=[pltpu.VMEM((rows, lanes), F32), pltpu.VMEM((N_CHIPS, rows, lanes), F32),
                        pltpu.SemaphoreType.DMA((7,)), pltpu.SemaphoreType.DMA((7,))],
        compiler_params=pltpu.CompilerParams(vmem_limit_bytes=32 * MIB),
    )(p)


BIG = ("w_in", "w_branch_attn", "w_branch_gmlp", "w_out", "w_mlp_in", "w_mlp_out")
COLUMN_SHARDED = ("w_branch_attn", "w_branch_gmlp", "w_mlp_in")
SMALL = ("norm_pre_mix", "w_spatial", "b_spatial", "ln_v_gain", "ln_v_bias", "norm_post_mix", "norm_pre_mlp", "norm_post_mlp")
ORDER = ("norm_pre_mix", "w_in", "w_spatial", "b_spatial", "ln_v_gain", "ln_v_bias", "w_branch_attn", "w_branch_gmlp",
         "w_out", "norm_post_mix", "norm_pre_mlp", "w_mlp_in", "w_mlp_out", "norm_post_mlp")


def _full_weight(name, gathered):
    if name in COLUMN_SHARDED:
        return jnp.transpose(gathered, (1, 0, 2)).reshape(gathered.shape[1], -1)
    return gathered.reshape(-1, gathered.shape[2])


def _rows8(a):
    a = a.reshape(-1, 128)
    pad = (-a.shape[0]) % 8
    return jnp.pad(a, ((0, pad), (0, 0))) if pad else a


def _qkv_columns(group):
    return [(sec * ATTN_W + group * GROUP_W, sec * ATTN_W + (group + 1) * GROUP_W) for sec in range(3)]


def _device_step(x, target, small, shards, place):
    seq = x.shape[0]
    g0, g1, g2, g3 = small["norm_pre_mix"], small["norm_post_mix"], small["norm_pre_mlp"], small["norm_post_mlp"]
    w_sp = small["w_spatial"]
    b_col = small["b_spatial"].reshape(GMLP_GROUPS, CHUNK, 1)
    ln_g, ln_b = small["ln_v_gain"], small["ln_v_bias"]

    staged = _stage_weights(shards)
    tables, w_in = _rope_tables(seq, rider=_gather(staged[:1], part=(0, 2)))
    h, (w_in,) = _norm_in(x, g0, rider=_gather(w_in, part=(1, 2)))
    w_in = _full_weight("w_in", w_in)
    (*qkv, rest), landed = _in_proj(h[0], w_in, *tables[1], rider=_gather(staged[1:], "chips"))

    o_l, gathered = _attn_fwd(qkv[0], DILATIONS[0], rider=_gather(landed, "pair"))
    full = {n: _full_weight(n, gw) for n, gw in zip(BIG[1:], gathered)}
    for g in range(1, N_GROUPS):
        o_l.extend(_attn_fwd(qkv[g], DILATIONS[g])[0])
    (*ya_l, yg, mg, y, x1), _ = _mix_fwd(o_l, rest, x, w_sp, b_col, ln_g, ln_b, full["w_branch_attn"],
                                        full["w_branch_gmlp"], full["w_out"], g1)
    ya, lse = ya_l[0::2], ya_l[1::2]
    h2, a, dy2, dout, loss8, dg3 = _mlp_fwd(x1, g2, g3, full["w_mlp_in"], full["w_mlp_out"], target)
    d_wmo, _ = _tn_matmul(a, dy2, "grad_w_mlp_out", 1024, 1024, square_a=True)
    mlp_out = _GradReduction({"w_mlp_out": d_wmo.reshape(N_CHIPS, D_FF // N_CHIPS, D_MODEL)}, place, "mlp_out")
    (dap, dx1, dy, dg2, dg1), riding = _mlp_bwd(dy2, a, full["w_mlp_out"], full["w_mlp_in"], dout, x1, y, g2, g1,
                                                 rider=mlp_out.pair_exchange())
    d_wmi, riding = _tn_matmul(h2, dap, "grad_w_mlp_in", 1024, 1024, column_shards=True,
                               rider=mlp_out.chip_exchange(riding))
    mlp_in = _GradReduction({"w_mlp_in": d_wmi}, place, "mlp_in")
    (*dya, drest, d_wout, d_wba, d_wbg, d_wsp, d_bb, d_lg, d_lb), riding = _mix_bwd(
        dy, ya[0], yg, mg, rest, full["w_out"], full["w_branch_attn"], full["w_branch_gmlp"], w_sp, b_col, ln_g, ln_b,
        rider=_together(mlp_out.pair_share(riding), mlp_in.pair_exchange()))
    reduced = mlp_out.result(riding[:1])
    mix = _GradReduction({"w_branch_attn": d_wba, "w_branch_gmlp": d_wbg,
                          "w_out": d_wout.reshape(N_CHIPS, D_MODEL // N_CHIPS, D_MODEL)}, place, "mix")
    attn = lambda g, rider: _attn_bwd(qkv[g], dya[g], ya[g], lse[g], *tables[DILATIONS[g]], DILATIONS[g], rider=rider)
    dqkv0, riding = attn(0, _together(mlp_in.chip_exchange(riding[1:]), mix.pair_exchange()))
    dqkv1, riding = attn(1, _together(mlp_in.pair_share(riding[:1]), mix.chip_exchange(riding[1:])))
    reduced.update(mlp_in.result(riding[:1]))
    dqkv2, riding = attn(2, mix.pair_share(riding[1:]))
    reduced.update(mix.result(riding))
    dqkv = [dqkv0, dqkv1, dqkv2]

    d_qkv = [_tn_matmul_residue(dqkv[g], h[g], dil, f"grad_w_in_qkv{g}") for g, dil in enumerate(DILATIONS)]
    d_rest, _ = _tn_matmul(drest, h[0], "grad_w_in_rest", 1024, 1024)
    d_win = jnp.concatenate([d_qkv[g][s * GROUP_W:(s + 1) * GROUP_W] for s in range(3) for g in range(N_GROUPS)]
                            + [d_rest], axis=0)
    first = _GradReduction({"w_in": d_win.reshape(N_CHIPS, IN_W // N_CHIPS, D_MODEL)}, place, "w_in")
    w_qkv = [jnp.concatenate([w_in[lo:hi] for lo, hi in _qkv_columns(g)], axis=0) for g in range(N_GROUPS)]
    w_rest = w_in[QKV_W:]
    tiles = seq // IN_PROJ_BWD_TM
    so_far = (lax.empty((seq, D_MODEL), F32), jnp.zeros((1, D_MODEL), F32))
    in_bwd = lambda so_far, span, rider: _in_proj_bwd(dqkv, drest, w_qkv, w_rest, x, dx1, g0, so_far, span, rider=rider)
    so_far, riding = in_bwd(so_far, (0, 3 * tiles // 8), first.pair_exchange())
    (grad_x, dg0), riding = in_bwd(so_far, (3 * tiles // 8, 5 * tiles // 8), first.chip_exchange(riding))
    reduced.update(first.result(_run_exchange(first.pair_share(riding), "w_in_pair_share")))
    little = {"norm_pre_mix": dg0, "w_spatial": d_wsp, "b_spatial": d_bb[:, :, 0], "ln_v_gain": d_lg, "ln_v_bias": d_lb,
              "norm_post_mix": dg1, "norm_pre_mlp": dg2, "norm_post_mlp": dg3}
    return loss8, grad_x, reduced, little


def kernel(x, norm_pre_mix, w_in, w_spatial, b_spatial, ln_v_gain, ln_v_bias, w_branch_attn, w_branch_gmlp, w_out, norm_post_mix, norm_pre_mlp, w_mlp_in, w_mlp_out, norm_post_mlp, loss_target, m_norm_pre_mix, m_w_in, m_w_spatial, m_b_spatial, m_ln_v_gain, m_ln_v_bias, m_w_branch_attn, m_w_branch_gmlp, m_w_out, m_norm_post_mix, m_norm_pre_mlp, m_w_mlp_in, m_w_mlp_out, m_norm_post_mlp, v_norm_pre_mix, v_w_in, v_w_spatial, v_b_spatial, v_ln_v_gain, v_ln_v_bias, v_w_branch_attn, v_w_branch_gmlp, v_w_out, v_norm_post_mix, v_norm_pre_mlp, v_w_mlp_in, v_w_mlp_out, v_norm_post_mlp):
    given = dict(norm_pre_mix=norm_pre_mix, w_in=w_in, w_spatial=w_spatial, b_spatial=b_spatial, ln_v_gain=ln_v_gain,
                 ln_v_bias=ln_v_bias, w_branch_attn=w_branch_attn, w_branch_gmlp=w_branch_gmlp, w_out=w_out,
                 norm_post_mix=norm_post_mix, norm_pre_mlp=norm_pre_mlp, w_mlp_in=w_mlp_in, w_mlp_out=w_mlp_out,
                 norm_post_mlp=norm_post_mlp)
    moments_m = dict(norm_pre_mix=m_norm_pre_mix, w_in=m_w_in, w_spatial=m_w_spatial, b_spatial=m_b_spatial,
                     ln_v_gain=m_ln_v_gain, ln_v_bias=m_ln_v_bias, w_branch_attn=m_w_branch_attn,
                     w_branch_gmlp=m_w_branch_gmlp, w_out=m_w_out, norm_post_mix=m_norm_post_mix,
                     norm_pre_mlp=m_norm_pre_mlp, w_mlp_in=m_w_mlp_in, w_mlp_out=m_w_mlp_out, norm_post_mlp=m_norm_post_mlp)
    moments_v = dict(norm_pre_mix=v_norm_pre_mix, w_in=v_w_in, w_spatial=v_w_spatial, b_spatial=v_b_spatial,
                     ln_v_gain=v_ln_v_gain, ln_v_bias=v_ln_v_bias, w_branch_attn=v_w_branch_attn,
                     w_branch_gmlp=v_w_branch_gmlp, w_out=v_w_out, norm_post_mix=v_norm_post_mix,
                     norm_pre_mlp=v_norm_pre_mlp, w_mlp_in=v_w_mlp_in, w_mlp_out=v_w_mlp_out, norm_post_mlp=v_norm_post_mlp)
    cx, cy, cc = lax.axis_index("x"), lax.axis_index("y"), lax.axis_index("c")

    shards = [given[n][0].T if n == "w_in" else given[n][0] for n in BIG]
    small = {n: given[n][0] if given[n].ndim > 2 else given[n] for n in SMALL}
    place = jnp.stack([cc, 2 * cx + cy]).astype(jnp.int32)
    loss8, grad_x, grad_shard, grads = _device_step(x[0], loss_target[0], small, shards, place)

    packed = jnp.concatenate([_rows8(grads[n]) for n in SMALL] + [loss8], axis=0)
    summed = _all_reduce_small(packed)
    loss = summed[packed.shape[0] - loss8.shape[0], 0]
    row = 0
    for n in SMALL:
        shape = given[n][0].shape
        cnt = -(-(given[n][0].size // 128) // 8) * 8
        grad_shard[n] = summed[row:row + given[n][0].size // 128].reshape(shape)
        row += cnt

    grad_out, deltas, new_m, new_v = {}, {}, {}, {}
    for n in ORDER:
        shape = given[n].shape
        if n == "w_in":
            outs = _adamw(given[n][0].T, grad_shard[n], moments_m[n][0].T, moments_v[n][0].T, "adamw_" + n)
            outs = [o.T for o in outs]
        else:
            two_d = (-1, shape[-1])
            outs = _adamw(given[n].reshape(two_d), grad_shard[n].reshape(two_d), moments_m[n].reshape(two_d),
                          moments_v[n].reshape(two_d), "adamw_" + n)
        grad_out[n], deltas[n], new_m[n], new_v[n] = [o.reshape(shape) for o in outs]
    return (loss, grad_x[None], *[grad_out[n] for n in ORDER], *[deltas[n] for n in ORDER], *[new_m[n] for n in ORDER],
            *[new_v[n] for n in ORDER])
```

```python
import math

import jax
import jax.numpy as jnp
from jax import lax
from jax.experimental import pallas as pl
from jax.experimental.pallas import tpu as pltpu

F32 = jnp.float32
BF16 = jnp.bfloat16
MESH = pl.DeviceIdType.MESH

D_MODEL = 1024
HEAD_DIM = 64
HEADS_PER_GROUP = 4
GROUP_W = HEADS_PER_GROUP * HEAD_DIM
DILATIONS = (1, 4, 16)
N_GROUPS = len(DILATIONS)
ATTN_W = N_GROUPS * GROUP_W
QKV_W = 3 * ATTN_W
GMLP_W = 512
GMLP_GROUPS = 4
CHUNK = 128
REST_W = 2 * GMLP_W + 2 * D_MODEL
IN_W = QKV_W + REST_W
D_FF = 4096
QBLK = 128
ROPE_THETA = 10000.0
EPS = 1e-6
NEG = -1e30
SCALE = HEAD_DIM ** -0.5
N_CHIPS = 4

ADAM_LR = 0.001
ADAM_B1 = 0.9
ADAM_B2 = 0.999
ADAM_EPS = 1e-08
ADAM_WD = 0.01
ADAM_STEP = 10

MIB = 1024 * 1024
HBM_SPEC = pl.BlockSpec(memory_space=pltpu.HBM)
VMEM_SPEC = pl.BlockSpec(memory_space=pltpu.VMEM)


MLP_TM = 256


CALL_VMEM_MIB = 56


def _params(semantics, vmem_mib):
    assert vmem_mib <= CALL_VMEM_MIB
    return pltpu.CompilerParams(dimension_semantics=semantics, vmem_limit_bytes=CALL_VMEM_MIB * MIB)


def _in_hbm(a):
    return pltpu.with_memory_space_constraint(a, pltpu.HBM) if a.size * a.dtype.itemsize >= MIB else a


def _pallas(body, **kwargs):
    return pl.pallas_call(body, **kwargs)


def _resident(shape):
    return pl.BlockSpec(shape, lambda *_: (0,) * len(shape), pipeline_mode=pl.Buffered(1))


def _dot(a, b):
    return jnp.dot(a, b, preferred_element_type=F32)


def _dot_nt(a, b):
    return lax.dot_general(a, b, (((1,), (1,)), ((), ())), preferred_element_type=F32)


def _dot_tn(a, b):
    return lax.dot_general(a, b, (((0,), (0,)), ((), ())), preferred_element_type=F32)


_GELU_C = math.sqrt(2.0 / math.pi)


def _gelu(x):
    return x * (0.5 * (1.0 + jnp.tanh(_GELU_C * (x + 0.044715 * (x * x * x)))))


def _gelu_grad(x):
    t = jnp.tanh(_GELU_C * (x + 0.044715 * (x * x * x)))
    return 0.5 * (1.0 + t) + 0.5 * x * (1.0 - t * t) * (_GELU_C * (1.0 + 3.0 * 0.044715 * (x * x)))


def _rsqrt_ms(v):
    return lax.rsqrt(jnp.mean(v * v, axis=-1, keepdims=True) + EPS)


def _rmsnorm_bwd(dn, src, gain):
    r = _rsqrt_ms(src)
    t = gain * dn
    dgain = jnp.sum(dn * (src * r), axis=0, keepdims=True)
    dsrc = r * t - src * ((r * r * r) * jnp.mean(t * src, axis=-1, keepdims=True))
    return dsrc, dgain


def _rot_half(v):
    w = v.shape[-1]
    lane = lax.broadcasted_iota(jnp.int32, v.shape, v.ndim - 1)
    return jnp.where((lane % HEAD_DIM) < HEAD_DIM // 2, pltpu.roll(v, w - HEAD_DIM // 2, v.ndim - 1),
                     pltpu.roll(v, HEAD_DIM // 2, v.ndim - 1))


def _head_masks(shape):
    lane = lax.broadcasted_iota(jnp.int32, shape, 1)
    return [(lane >= h * HEAD_DIM) & (lane < (h + 1) * HEAD_DIM) for h in range(HEADS_PER_GROUP)]


def _head_stack(block, hmask):
    zero = jnp.zeros((), block.dtype)
    return jnp.concatenate([jnp.where(hm, block, zero) for hm in hmask], axis=0)


LANES = 128


def _put_residue(slab, val, out_ref, dil, width, col0):
    tm, w = val.shape
    if dil == 1:
        out_ref[:, col0:col0 + w] = val.astype(out_ref.dtype)
        return
    for k in range(w // LANES):
        slab[k] = val[:, k * LANES:(k + 1) * LANES]
    for r in range(dil):
        for k in range(w // LANES):
            c = r * width + col0 + k * LANES
            out_ref[:, c:c + LANES] = slab[k, pl.ds(r, tm // dil, stride=dil), :].astype(out_ref.dtype)


def _get_tokens(slab, in_ref, dil, width, col0, w):
    if dil == 1:
        return in_ref[:, col0:col0 + w].astype(F32)
    rows = in_ref.shape[0]
    for r in range(dil):
        for k in range(w // LANES):
            c = r * width + col0 + k * LANES
            slab[k, pl.ds(r, rows, stride=dil), :] = in_ref[:, c:c + LANES].astype(F32)
    return jnp.concatenate([slab[k] for k in range(w // LANES)], axis=1)


def _rope_tables(seq, rider=None):
    half = HEAD_DIM // 2
    inv_freq = ROPE_THETA ** (-jnp.arange(half, dtype=F32) / half)
    freq = jnp.tile(inv_freq, LANES // half).reshape(1, LANES)
    tm = 512

    def body(f_ref, *refs):
        outs, slab_c, slab_s = refs[:-2], refs[-2], refs[-1]
        row = lax.broadcasted_iota(jnp.int32, (tm, LANES), 0) + pl.program_id(0) * tm
        lane = lax.broadcasted_iota(jnp.int32, (tm, LANES), 1)
        ang = row.astype(F32) * f_ref[...]
        cos = jnp.cos(ang)
        sin = jnp.where((lane % HEAD_DIM) < half, -jnp.sin(ang), jnp.sin(ang))
        slab_c[0] = cos
        slab_s[0] = sin
        for i, dil in enumerate(DILATIONS):
            for tab, slab in ((outs[2 * i], slab_c), (outs[2 * i + 1], slab_s)):
                for r in range(dil):
                    piece = slab[0, pl.ds(r, tm // dil, stride=dil), :] if dil > 1 else slab[0]
                    for k in range(GROUP_W // LANES):
                        tab[:, r * GROUP_W + k * LANES:r * GROUP_W + (k + 1) * LANES] = piece

    outs, riding = _call(
        body, name="rope_tables", grid=(seq // tm,),
        in_specs=[pl.BlockSpec((1, LANES), lambda i: (0, 0))],
        out_specs=[pl.BlockSpec((tm // d, d * GROUP_W), lambda i: (i, 0)) for d in DILATIONS for _ in range(2)],
        out_shape=[jax.ShapeDtypeStruct((seq // d, d * GROUP_W), F32) for d in DILATIONS for _ in range(2)],
        scratch_shapes=[pltpu.VMEM((1, tm, LANES), F32)] * 2,
        params=_params(("arbitrary",), 32), args=(freq,), rider=rider)
    return {d: (outs[2 * i], outs[2 * i + 1]) for i, d in enumerate(DILATIONS)}, riding


def _norm_in(x, g0, rider=None):
    seq = x.shape[0]
    tm = 256

    def body(x_ref, g_ref, *refs):
        h_refs, slab = refs[:N_GROUPS], refs[-1]
        xv = x_ref[...]
        hf = (xv * _rsqrt_ms(xv)) * g_ref[...]
        for g, dil in enumerate(DILATIONS):
            _put_residue(slab, hf, h_refs[g], dil, D_MODEL, 0)

    return _call(
        body, name="norm_in", grid=(seq // tm,),
        in_specs=[pl.BlockSpec((tm, D_MODEL), lambda i: (i, 0)), pl.BlockSpec((1, D_MODEL), lambda i: (0, 0))],
        out_specs=[pl.BlockSpec((tm // d, d * D_MODEL), lambda i: (i, 0)) for d in DILATIONS],
        out_shape=[jax.ShapeDtypeStruct((seq // d, d * D_MODEL), BF16) for d in DILATIONS],
        scratch_shapes=[pltpu.VMEM((D_MODEL // LANES, tm, LANES), F32)],
        params=_params(("arbitrary",), 32), args=(x, g0), rider=rider)


def _in_proj(h, w_in, cos_t, sin_t, rider=None):
    seq = h.shape[0]
    tm, tn = 512, GROUP_W
    n_qk = 2 * ATTN_W // tn
    n_qkv = QKV_W // tn

    def body(h_ref, w_ref, cos_ref, sin_ref, *refs):
        qkv_refs, rest_ref, slab = refs[:N_GROUPS], refs[N_GROUPS], refs[-1]
        hb = h_ref[...]
        cos, sin = cos_ref[...], sin_ref[...]
        for j in range(IN_W // tn):
            p = _dot_nt(hb, w_ref[j * tn:(j + 1) * tn, :])
            if j < n_qkv:
                if j < n_qk:
                    p = p * cos + _rot_half(p) * sin
                section, g = divmod(j, N_GROUPS)
                _put_residue(slab, p, qkv_refs[g], DILATIONS[g], 3 * GROUP_W, section * GROUP_W)
            else:
                rest_ref[:, (j - n_qkv) * tn:(j - n_qkv + 1) * tn] = p.astype(BF16)

    return _call(
        body, name="in_proj", grid=(seq // tm,),
        in_specs=[pl.BlockSpec((tm, D_MODEL), lambda i: (i, 0)),
                  _resident((IN_W, D_MODEL)),
                  pl.BlockSpec((tm, GROUP_W), lambda i: (i, 0)),
                  pl.BlockSpec((tm, GROUP_W), lambda i: (i, 0))],
        out_specs=[pl.BlockSpec((tm // d, d * 3 * GROUP_W), lambda i: (i, 0)) for d in DILATIONS]
        + [pl.BlockSpec((tm, REST_W), lambda i: (i, 0))],
        out_shape=[jax.ShapeDtypeStruct((seq // d, d * 3 * GROUP_W), BF16) for d in DILATIONS]
        + [jax.ShapeDtypeStruct((seq, REST_W), BF16)],
        scratch_shapes=[pltpu.VMEM((GROUP_W // LANES, tm, LANES), F32)],
        params=_params(("arbitrary",), 48), args=(h, w_in, cos_t, sin_t), rider=rider)


def _band_masks():
    qi = lax.broadcasted_iota(jnp.int32, (QBLK, QBLK), 0)
    kj = lax.broadcasted_iota(jnp.int32, (QBLK, QBLK), 1)
    return kj <= qi, kj >= qi


def _attn_tile(length):
    return min(512, length)


def _attn_fwd(qkv, dil, rider=None):
    length = qkv.shape[0]
    tq = _attn_tile(length)
    nsub = tq // QBLK
    nblk = length // tq

    def body(q_ref, k_ref, v_ref, kp_ref, vp_ref, o_ref, l_ref):
        n = pl.program_id(1)
        mask_c, mask_p0 = _band_masks()
        hmask = _head_masks((QBLK, GROUP_W))
        zero = jnp.zeros((), BF16)
        for b in range(nsub):
            rows = slice(b * QBLK, (b + 1) * QBLK)
            q = q_ref[rows, :]
            kc, vc = k_ref[rows, :], v_ref[rows, :]
            if b == 0:
                kp, vp = kp_ref[...], vp_ref[...]
                mask_p = mask_p0 & (n > 0)
            else:
                prow = slice((b - 1) * QBLK, b * QBLK)
                kp, vp = k_ref[prow, :], v_ref[prow, :]
                mask_p = mask_p0
            o_acc = jnp.zeros((QBLK, GROUP_W), F32)
            l_acc = jnp.zeros((QBLK, GROUP_W), F32)
            for h in range(HEADS_PER_GROUP):
                hm = hmask[h]
                sc = jnp.where(mask_c, _dot_nt(q, jnp.where(hm, kc, zero)) * SCALE, NEG)
                sp = jnp.where(mask_p, _dot_nt(q, jnp.where(hm, kp, zero)) * SCALE, NEG)
                m = jnp.maximum(jnp.max(sc, axis=-1, keepdims=True), jnp.max(sp, axis=-1, keepdims=True))
                pc, pp = jnp.exp(sc - m), jnp.exp(sp - m)
                den = jnp.sum(pc, axis=-1, keepdims=True) + jnp.sum(pp, axis=-1, keepdims=True)
                pv = _dot(pc.astype(BF16), jnp.where(hm, vc, zero)) + _dot(pp.astype(BF16), jnp.where(hm, vp, zero))
                o_acc = o_acc + pv / den
                l_acc = l_acc + jnp.where(hm, m + jnp.log(den), 0.0)
            o_ref[rows, :] = o_acc
            l_ref[rows, :] = l_acc

    cur = lambda sec: pl.BlockSpec((tq, GROUP_W), lambda r, n: (n, r * 3 + sec))
    prev = lambda sec: pl.BlockSpec((QBLK, GROUP_W), lambda r, n: (jnp.maximum(n * nsub - 1, 0), r * 3 + sec))
    return _call(
        body, name=f"attn_fwd_d{dil}", grid=(dil, nblk),
        in_specs=[cur(0), cur(1), cur(2), prev(1), prev(2)],
        out_specs=[pl.BlockSpec((tq, GROUP_W), lambda r, n: (n, r))] * 2,
        out_shape=[jax.ShapeDtypeStruct((length, dil * GROUP_W), F32)] * 2, scratch_shapes=[],
        params=_params(("arbitrary", "arbitrary"), 32), args=(qkv, qkv, qkv, qkv, qkv), rider=rider)


def _attn_bwd(qkv, dy, y, lse, cos_t, sin_t, dil, rider=None):
    length = qkv.shape[0]
    tq = _attn_tile(length)
    nsub = tq // QBLK
    nblk = length // tq

    def body(q_ref, k_ref, v_ref, kp_ref, vp_ref, qn_ref, dy_ref, y_ref, l_ref, dyn_ref, yn_ref, ln_ref,
             cos_ref, sin_ref, out_ref, dq_s, dk_s, dv_s):
        n = pl.program_id(1)
        mask_c, mask_p0 = _band_masks()
        hmask = _head_masks((QBLK, GROUP_W))
        sub = lambda ref, b: ref[b * QBLK:(b + 1) * QBLK, :]
        kbd = [_head_stack(kp_ref[...], hmask)] + [_head_stack(sub(k_ref, b), hmask) for b in range(nsub)]
        vbd = [_head_stack(vp_ref[...], hmask)] + [_head_stack(sub(v_ref, b), hmask) for b in range(nsub)]
        dk_s[...] = jnp.zeros(dk_s.shape, F32)
        dv_s[...] = jnp.zeros(dv_s.shape, F32)

        def query_block(q, dyv, yv, lv, key_blocks):
            dyb = dyv.astype(BF16)
            qbd = _head_stack(q, hmask)
            dybd = jnp.concatenate([jnp.where(hm, dyv, 0.0).astype(BF16) for hm in hmask], axis=0)
            prod = dyv * yv
            deltas = [jnp.sum(jnp.where(hm, prod, 0.0), axis=-1, keepdims=True) for hm in hmask]
            lses = [jnp.max(jnp.where(hm, lv, NEG), axis=-1, keepdims=True) for hm in hmask]
            dq = jnp.zeros((QBLK, GROUP_W), F32)
            for kb, mask in key_blocks:
                s = _dot_nt(q, kbd[kb]) * SCALE
                dp = _dot_nt(dyb, vbd[kb])
                ps, dss = [], []
                for h in range(HEADS_PER_GROUP):
                    cols = slice(h * QBLK, (h + 1) * QBLK)
                    p = jnp.exp(jnp.where(mask, s[:, cols] - lses[h], NEG))
                    ps.append(p.astype(BF16))
                    dss.append((p * (dp[:, cols] - deltas[h])).astype(BF16))
                dq = dq + _dot(jnp.concatenate(dss, axis=1), kbd[kb])
                if kb >= 1:
                    krows = slice((kb - 1) * QBLK, kb * QBLK)
                    dv_s[krows, :] += _dot_tn(jnp.concatenate(ps, axis=0), dybd)
                    dk_s[krows, :] += _dot_tn(jnp.concatenate(dss, axis=0), qbd) * SCALE
            return dq * SCALE

        for b in range(nsub):
            mask_p = mask_p0 & (n > 0) if b == 0 else mask_p0
            dq_s[b * QBLK:(b + 1) * QBLK, :] = query_block(sub(q_ref, b), sub(dy_ref, b), sub(y_ref, b), sub(l_ref, b),
                                                            [(b, mask_p), (b + 1, mask_c)])
        query_block(qn_ref[...], dyn_ref[...], yn_ref[...], ln_ref[...], [(nsub, mask_p0 & (n < nblk - 1))])
        cos, sin = cos_ref[...], sin_ref[...]
        dq, dk = dq_s[...], dk_s[...]
        out_ref[:, 0:GROUP_W] = (dq * cos - _rot_half(dq) * sin).astype(BF16)
        out_ref[:, GROUP_W:2 * GROUP_W] = (dk * cos - _rot_half(dk) * sin).astype(BF16)
        out_ref[:, 2 * GROUP_W:3 * GROUP_W] = dv_s[...].astype(BF16)

    cur = lambda sec: pl.BlockSpec((tq, GROUP_W), lambda r, n: (n, r * 3 + sec))
    prev = lambda sec: pl.BlockSpec((QBLK, GROUP_W), lambda r, n: (jnp.maximum(n * nsub - 1, 0), r * 3 + sec))
    nxt_q = pl.BlockSpec((QBLK, GROUP_W), lambda r, n: (jnp.minimum((n + 1) * nsub, nblk * nsub - 1), r * 3))
    tok = pl.BlockSpec((tq, GROUP_W), lambda r, n: (n, r))
    tok_next = pl.BlockSpec((QBLK, GROUP_W), lambda r, n: (jnp.minimum((n + 1) * nsub, nblk * nsub - 1), r))
    (out,), riding = _call(
        body, name=f"attn_bwd_d{dil}", grid=(dil, nblk),
        in_specs=[cur(0), cur(1), cur(2), prev(1), prev(2), nxt_q,
                  tok, tok, tok, tok_next, tok_next, tok_next, tok, tok],
        out_specs=[pl.BlockSpec((tq, 3 * GROUP_W), lambda r, n: (n, r))],
        out_shape=[jax.ShapeDtypeStruct((length, dil * 3 * GROUP_W), BF16)],
        scratch_shapes=[pltpu.VMEM((tq, GROUP_W), F32)] * 3,
        params=_params(("arbitrary", "arbitrary"), 32),
        args=(qkv, qkv, qkv, qkv, qkv, qkv, dy, y, lse, dy, y, lse, cos_t, sin_t), rider=rider)
    return out, riding


def _layernorm_stats(z):
    mu = jnp.mean(z, axis=-1, keepdims=True)
    zc = z - mu
    rstd = lax.rsqrt(jnp.mean(zc * zc, axis=-1, keepdims=True) + EPS)
    return zc * rstd, rstd


def _tril_mask():
    row = lax.broadcasted_iota(jnp.int32, (CHUNK, CHUNK), 0)
    col = lax.broadcasted_iota(jnp.int32, (CHUNK, CHUNK), 1)
    return col <= row


def _mix_fwd(o_l, rest, x, w_sp, b_col, ln_g, ln_b, w_ba, w_bg, w_out, g1, rider=None):
    seq = x.shape[0]
    tm = 256

    def body(o0, l0, o1, l1, o2, l2, up_ref, zp_ref, gap_ref, gbp_ref, x_ref, wsp_ref, bcol_ref, lg_ref, lb_ref,
             wba_ref, wbg_ref, wout_ref, g1_ref, ya0, lj0, ya1, lj1, ya2, lj2, yg_ref, mg_ref, y_ref, x1_ref, slab):
        outs = [_get_tokens(slab, o, d, GROUP_W, 0, GROUP_W) for o, d in zip((o0, o1, o2), DILATIONS)]
        lses = [_get_tokens(slab, l, d, GROUP_W, 0, GROUP_W) for l, d in zip((l0, l1, l2), DILATIONS)]
        m = jnp.maximum(jnp.maximum(lses[0], lses[1]), lses[2])
        es = [jnp.exp(l - m) for l in lses]
        tot = es[0] + es[1] + es[2]
        ya = (es[0] * outs[0] + es[1] * outs[1] + es[2] * outs[2]) / tot
        lj = m + jnp.log(tot)
        for ya_ref, lj_ref, d in zip((ya0, ya1, ya2), (lj0, lj1, lj2), DILATIONS):
            _put_residue(slab, ya, ya_ref, d, GROUP_W, 0)
            _put_residue(slab, lj, lj_ref, d, GROUP_W, 0)
        zhat, _ = _layernorm_stats(_gelu(zp_ref[...].astype(F32)))
        zln = (zhat * lg_ref[...] + lb_ref[...]).astype(BF16)
        u = _gelu(up_ref[...].astype(F32))
        tril = _tril_mask()
        for g in range(GMLP_GROUPS):
            wm = jnp.where(tril, wsp_ref[g], 0.0).astype(BF16)
            cols = slice(g * CHUNK, (g + 1) * CHUNK)
            for c in range(tm // CHUNK):
                rows = slice(c * CHUNK, (c + 1) * CHUNK)
                sz = _dot(wm, zln[rows, cols]) + bcol_ref[g]
                yg_ref[rows, cols] = (u[rows, cols] * sz).astype(BF16)
        a = _dot(ya.astype(BF16), wba_ref[...])
        bm = _dot(yg_ref[...], wbg_ref[...])
        merged = (jax.nn.sigmoid(gap_ref[...].astype(F32)) * a + jax.nn.sigmoid(gbp_ref[...].astype(F32)) * bm).astype(BF16)
        mg_ref[...] = merged
        yv = _dot(merged, wout_ref[...])
        y_ref[...] = yv
        x1_ref[...] = x_ref[...] + (yv * _rsqrt_ms(yv)) * g1_ref[...]

    tok = lambda w: pl.BlockSpec((tm, w), lambda i: (i, 0))
    res = lambda d: pl.BlockSpec((tm // d, d * GROUP_W), lambda i: (i, 0))
    full = lambda *s: pl.BlockSpec(s, lambda i: (0,) * len(s))
    res_specs = [res(d) for d in DILATIONS for _ in range(2)]
    return _call(
        body, name="mix_fwd", grid=(seq // tm,),
        in_specs=res_specs + [
            pl.BlockSpec((tm, GMLP_W), lambda i: (i, 0)), pl.BlockSpec((tm, GMLP_W), lambda i: (i, 1)),
            pl.BlockSpec((tm, D_MODEL), lambda i: (i, 1)), pl.BlockSpec((tm, D_MODEL), lambda i: (i, 2)),
            tok(D_MODEL), full(GMLP_GROUPS, CHUNK, CHUNK), full(GMLP_GROUPS, CHUNK, 1), full(1, GMLP_W), full(1, GMLP_W),
            full(GROUP_W, D_MODEL), full(GMLP_W, D_MODEL), full(D_MODEL, D_MODEL), full(1, D_MODEL)],
        out_specs=res_specs + [tok(GMLP_W), tok(D_MODEL), tok(D_MODEL), tok(D_MODEL)],
        out_shape=[jax.ShapeDtypeStruct((seq // d, d * GROUP_W), F32) for d in DILATIONS for _ in range(2)]
        + [jax.ShapeDtypeStruct((seq, GMLP_W), BF16), jax.ShapeDtypeStruct((seq, D_MODEL), BF16),
           jax.ShapeDtypeStruct((seq, D_MODEL), F32), jax.ShapeDtypeStruct((seq, D_MODEL), F32)],
        scratch_shapes=[pltpu.VMEM((GROUP_W // LANES, tm, LANES), F32)],
        params=_params(("arbitrary",), 48),
        args=(*o_l, rest, rest, rest, rest, x, w_sp, b_col, ln_g, ln_b, w_ba, w_bg, w_out, g1), rider=rider)


def _mlp_fwd(x1, g2, g3, w_mi, w_mo, target):
    seq = x1.shape[0]
    tm, tf = MLP_TM, 512

    def body(x1_ref, g2_ref, g3_ref, wmi_ref, wmo_ref, t_ref, h2_ref, a_ref, dy2_ref, dout_ref, loss_ref, dg3_ref, sq_s):
        @pl.when(pl.program_id(0) == 0)
        def _():
            loss_ref[...] = jnp.zeros(loss_ref.shape, F32)
            dg3_ref[...] = jnp.zeros(dg3_ref.shape, F32)

        xv = x1_ref[...]
        hb = ((xv * _rsqrt_ms(xv)) * g2_ref[...]).astype(BF16)
        h2_ref[...] = hb
        for j in range(D_FF // tf):
            cols = slice(j * tf, (j + 1) * tf)
            a = jnp.maximum(_dot(hb, wmi_ref[:, cols]), 0.0)
            a_ref[:, cols] = a.astype(BF16)
            sq_s[:, cols] = (a * a).astype(BF16)
        y2 = _dot(sq_s[...], wmo_ref[...])
        r3 = _rsqrt_ms(y2)
        out = xv + (y2 * r3) * g3_ref[...]
        diff = out - t_ref[...]
        tile_loss = 0.5 * jnp.sum(jnp.mean(diff * diff, axis=-1, keepdims=True), axis=0, keepdims=True)
        loss_ref[...] += jnp.broadcast_to(tile_loss, loss_ref.shape)
        dout = diff * (1.0 / D_MODEL)
        dout_ref[...] = dout
        dy2, dg3 = _rmsnorm_bwd(dout, y2, g3_ref[...])
        dy2_ref[...] = dy2.astype(BF16)
        dg3_ref[...] += dg3

    tok = lambda w: pl.BlockSpec((tm, w), lambda i: (i, 0))
    vec = pl.BlockSpec((1, D_MODEL), lambda i: (0, 0))
    return _pallas(
        body, name="mlp_fwd", grid=(seq // tm,),
        in_specs=[tok(D_MODEL), vec, vec, _resident((D_MODEL, D_FF)), _resident((D_FF, D_MODEL)), tok(D_MODEL)],
        out_specs=[tok(D_MODEL), tok(D_FF), tok(D_MODEL), tok(D_MODEL), pl.BlockSpec((8, 128), lambda i: (0, 0)), vec],
        out_shape=[jax.ShapeDtypeStruct((seq, D_MODEL), BF16), jax.ShapeDtypeStruct((seq, D_FF), BF16),
                   jax.ShapeDtypeStruct((seq, D_MODEL), BF16), jax.ShapeDtypeStruct((seq, D_MODEL), F32),
                   jax.ShapeDtypeStruct((8, 128), F32), jax.ShapeDtypeStruct((1, D_MODEL), F32)],
        scratch_shapes=[pltpu.VMEM((tm, D_FF), BF16)],
        compiler_params=_params(("arbitrary",), 56),
    )(*map(_in_hbm, (x1, g2, g3, w_mi, w_mo, target)))


def _mlp_bwd(dy2, a, w_mo, w_mi, dout, x1, y, g2, g1, rider=None):
    seq = x1.shape[0]
    tm, tf = MLP_TM, 512

    def body(dy2_ref, a_ref, wmo_ref, wmi_ref, dout_ref, x1_ref, y_ref, g2_ref, g1_ref,
             dap_ref, dx1_ref, dy_ref, dg2_ref, dg1_ref):
        @pl.when(pl.program_id(0) == 0)
        def _():
            dg2_ref[...] = jnp.zeros(dg2_ref.shape, F32)
            dg1_ref[...] = jnp.zeros(dg1_ref.shape, F32)

        dy2v = dy2_ref[...]
        for j in range(D_FF // tf):
            cols = slice(j * tf, (j + 1) * tf)
            da2 = _dot_nt(dy2v, wmo_ref[cols, :])
            dap_ref[:, cols] = (da2 * (2.0 * a_ref[:, cols].astype(F32))).astype(BF16)
        dh2 = _dot_nt(dap_ref[...], wmi_ref[...])
        dres, dg2 = _rmsnorm_bwd(dh2, x1_ref[...], g2_ref[...])
        dx1 = dout_ref[...] + dres
        dx1_ref[...] = dx1
        dg2_ref[...] += dg2
        dyv, dg1 = _rmsnorm_bwd(dx1, y_ref[...], g1_ref[...])
        dy_ref[...] = dyv.astype(BF16)
        dg1_ref[...] += dg1

    tok = lambda w: pl.BlockSpec((tm, w), lambda i: (i, 0))
    vec = pl.BlockSpec((1, D_MODEL), lambda i: (0, 0))
    return _call(
        body, name="mlp_bwd", grid=(seq // tm,),
        in_specs=[tok(D_MODEL), tok(D_FF), _resident((D_FF, D_MODEL)), _resident((D_MODEL, D_FF)),
                  tok(D_MODEL), tok(D_MODEL), tok(D_MODEL), vec, vec],
        out_specs=[tok(D_FF), tok(D_MODEL), tok(D_MODEL), vec, vec],
        out_shape=[jax.ShapeDtypeStruct((seq, D_FF), BF16), jax.ShapeDtypeStruct((seq, D_MODEL), F32),
                   jax.ShapeDtypeStruct((seq, D_MODEL), BF16), jax.ShapeDtypeStruct((1, D_MODEL), F32),
                   jax.ShapeDtypeStruct((1, D_MODEL), F32)], scratch_shapes=[],
        params=_params(("arbitrary",), 56), args=(dy2, a, w_mo, w_mi, dout, x1, y, g2, g1), rider=rider)


def _tn_matmul(a, b, name, bm, bn, square_a=False, column_shards=False, rider=None):
    seq, m = a.shape
    n = b.shape[1]
    ts = 2048

    def body(a_ref, b_ref, o_ref):
        @pl.when(pl.program_id(2) == 0)
        def _():
            o_ref[...] = jnp.zeros(o_ref.shape, F32)

        av = a_ref[...]
        if square_a:
            af = av.astype(F32)
            av = (af * af).astype(BF16)
        o_ref[...] += _dot_tn(av, b_ref[...])

    if column_shards:
        out_spec = pl.BlockSpec((None, bm, bn), lambda mi, ni, s: (ni, mi, 0))
        out_shape = jax.ShapeDtypeStruct((n // bn, m, bn), F32)
    else:
        out_spec = pl.BlockSpec((bm, bn), lambda mi, ni, s: (mi, ni))
        out_shape = jax.ShapeDtypeStruct((m, n), F32)
    (out,), riding = _call(
        body, name=name, grid=(m // bm, n // bn, seq // ts),
        in_specs=[pl.BlockSpec((ts, bm), lambda mi, ni, s: (s, mi)), pl.BlockSpec((ts, bn), lambda mi, ni, s: (s, ni))],
        out_specs=[out_spec], out_shape=[out_shape], scratch_shapes=[],
        params=_params(("arbitrary", "arbitrary", "arbitrary"), 40), args=(a, b), rider=rider)
    return out, riding


def _tn_matmul_residue(a, b, dil, name):
    length = a.shape[0]
    m, n = a.shape[1] // dil, b.shape[1] // dil
    ts = min(1024, length)

    def body(a_ref, b_ref, o_ref):
        @pl.when((pl.program_id(0) == 0) & (pl.program_id(1) == 0))
        def _():
            o_ref[...] = jnp.zeros(o_ref.shape, F32)

        o_ref[...] += _dot_tn(a_ref[...], b_ref[...])

    return _pallas(
        body, name=name, grid=(dil, length // ts),
        in_specs=[pl.BlockSpec((ts, m), lambda r, s: (s, r)), pl.BlockSpec((ts, n), lambda r, s: (s, r))],
        out_specs=pl.BlockSpec((m, n), lambda r, s: (0, 0)),
        out_shape=jax.ShapeDtypeStruct((m, n), F32),
        compiler_params=_params(("arbitrary", "arbitrary"), 40),
    )(_in_hbm(a), _in_hbm(b))


def _mix_bwd(dy, ya, yg, mg, rest, w_out, w_ba, w_bg, w_sp, b_col, ln_g, ln_b, rider=None):
    seq = dy.shape[0]
    tm = 256

    def body(dy_ref, ya_ref, yg_ref, mg_ref, up_ref, zp_ref, gap_ref, gbp_ref, wout_ref, wba_ref, wbg_ref,
             wsp_ref, bcol_ref, lg_ref, lb_ref,
             dya0, dya1, dya2, dpr_ref, dwout_ref, dwba_ref, dwbg_ref, dwsp_ref, dbb_ref, dlg_ref, dlb_ref,
             dzln_s, du_s, slab):
        @pl.when(pl.program_id(0) == 0)
        def _():
            for ref in (dwout_ref, dwba_ref, dwbg_ref, dwsp_ref, dbb_ref, dlg_ref, dlb_ref):
                ref[...] = jnp.zeros(ref.shape, F32)

        dyv = dy_ref[...]
        dm = _dot_nt(dyv, wout_ref[...])
        dwout_ref[...] += _dot_tn(mg_ref[...], dyv)
        yab = ya_ref[...].astype(BF16)
        ygb = yg_ref[...]
        a = _dot(yab, wba_ref[...])
        bm = _dot(ygb, wbg_ref[...])
        ga = jax.nn.sigmoid(gap_ref[...].astype(F32))
        gb = jax.nn.sigmoid(gbp_ref[...].astype(F32))
        dpr_ref[:, 2 * GMLP_W:2 * GMLP_W + D_MODEL] = (dm * a * (ga * (1.0 - ga))).astype(BF16)
        dpr_ref[:, 2 * GMLP_W + D_MODEL:REST_W] = (dm * bm * (gb * (1.0 - gb))).astype(BF16)
        da = (dm * ga).astype(BF16)
        db = (dm * gb).astype(BF16)
        dwba = _dot_tn(yab, da)
        dwbg = _dot_tn(ygb, db)
        shard_w = D_MODEL // N_CHIPS
        for j in range(N_CHIPS):
            dwba_ref[j] += dwba[:, j * shard_w:(j + 1) * shard_w]
            dwbg_ref[j] += dwbg[:, j * shard_w:(j + 1) * shard_w]
        dya = _dot_nt(da, wba_ref[...])
        for dya_ref, d in zip((dya0, dya1, dya2), DILATIONS):
            _put_residue(slab, dya, dya_ref, d, GROUP_W, 0)
        dyg = _dot_nt(db, wbg_ref[...])

        zp = zp_ref[...].astype(F32)
        zhat, rstd = _layernorm_stats(_gelu(zp))
        lg = lg_ref[...]
        zln = (zhat * lg + lb_ref[...]).astype(BF16)
        up = up_ref[...].astype(F32)
        u = _gelu(up)
        tril = _tril_mask()
        for g in range(GMLP_GROUPS):
            wm = jnp.where(tril, wsp_ref[g], 0.0).astype(BF16)
            cols = slice(g * CHUNK, (g + 1) * CHUNK)
            for c in range(tm // CHUNK):
                rows = slice(c * CHUNK, (c + 1) * CHUNK)
                zb = zln[rows, cols]
                sz = _dot(wm, zb) + bcol_ref[g]
                dyg_cg = dyg[rows, cols]
                du_s[rows, cols] = dyg_cg * sz
                dsz = dyg_cg * u[rows, cols]
                dszb = dsz.astype(BF16)
                dbb_ref[g] += jnp.broadcast_to(jnp.sum(dsz, axis=-1, keepdims=True), (CHUNK, CHUNK))
                dwsp_ref[g] += jnp.where(tril, _dot_nt(dszb, zb), 0.0)
                dzln_s[rows, cols] = _dot_tn(wm, dszb)
        dzln = dzln_s[...]
        dlg_ref[...] += jnp.sum(dzln * zhat, axis=0, keepdims=True)
        dlb_ref[...] += jnp.sum(dzln, axis=0, keepdims=True)
        dzh = dzln * lg
        dz = rstd * (dzh - jnp.mean(dzh, axis=-1, keepdims=True) - zhat * jnp.mean(dzh * zhat, axis=-1, keepdims=True))
        dpr_ref[:, GMLP_W:2 * GMLP_W] = (dz * _gelu_grad(zp)).astype(BF16)
        dpr_ref[:, 0:GMLP_W] = (du_s[...] * _gelu_grad(up)).astype(BF16)

    tok = lambda w: pl.BlockSpec((tm, w), lambda i: (i, 0))
    full = lambda *s: pl.BlockSpec(s, lambda i: (0,) * len(s))
    return _call(
        body, name="mix_bwd", grid=(seq // tm,),
        in_specs=[tok(D_MODEL), tok(GROUP_W), tok(GMLP_W), tok(D_MODEL),
                  pl.BlockSpec((tm, GMLP_W), lambda i: (i, 0)), pl.BlockSpec((tm, GMLP_W), lambda i: (i, 1)),
                  pl.BlockSpec((tm, D_MODEL), lambda i: (i, 1)), pl.BlockSpec((tm, D_MODEL), lambda i: (i, 2)),
                  full(D_MODEL, D_MODEL), full(GROUP_W, D_MODEL), full(GMLP_W, D_MODEL),
                  full(GMLP_GROUPS, CHUNK, CHUNK), full(GMLP_GROUPS, CHUNK, 1), full(1, GMLP_W), full(1, GMLP_W)],
        out_specs=[pl.BlockSpec((tm // d, d * GROUP_W), lambda i: (i, 0)) for d in DILATIONS]
        + [tok(REST_W), full(D_MODEL, D_MODEL), full(N_CHIPS, GROUP_W, D_MODEL // N_CHIPS),
           full(N_CHIPS, GMLP_W, D_MODEL // N_CHIPS),
           full(GMLP_GROUPS, CHUNK, CHUNK), full(GMLP_GROUPS, CHUNK, CHUNK), full(1, GMLP_W), full(1, GMLP_W)],
        out_shape=[jax.ShapeDtypeStruct((seq // d, d * GROUP_W), F32) for d in DILATIONS]
        + [jax.ShapeDtypeStruct((seq, REST_W), BF16),
           jax.ShapeDtypeStruct((D_MODEL, D_MODEL), F32), jax.ShapeDtypeStruct((N_CHIPS, GROUP_W, D_MODEL // N_CHIPS), F32),
           jax.ShapeDtypeStruct((N_CHIPS, GMLP_W, D_MODEL // N_CHIPS), F32),
           jax.ShapeDtypeStruct((GMLP_GROUPS, CHUNK, CHUNK), F32),
           jax.ShapeDtypeStruct((GMLP_GROUPS, CHUNK, CHUNK), F32), jax.ShapeDtypeStruct((1, GMLP_W), F32),
           jax.ShapeDtypeStruct((1, GMLP_W), F32)],
        scratch_shapes=[pltpu.VMEM((tm, GMLP_W), F32), pltpu.VMEM((tm, GMLP_W), F32),
                        pltpu.VMEM((GROUP_W // LANES, tm, LANES), F32)],
        params=_params(("arbitrary",), 56),
        args=(dy, ya, yg, mg, rest, rest, rest, rest, w_out, w_ba, w_bg, w_sp, b_col, ln_g, ln_b), rider=rider)


IN_PROJ_BWD_TM = 256


def _in_proj_bwd(dqkv, drest, w_qkv, w_rest, x, dx1, g0, so_far, span, rider=None):
    seq = x.shape[0]
    tm = IN_PROJ_BWD_TM
    off, steps = span
    gx_so_far, dg_so_far = so_far

    def body(d0, d1, d2, dr_ref, w0, w1, w2, wr_ref, x_ref, dx1_ref, g_ref, dg_in_ref, gx_in_ref, gx_ref, dg_ref, slab):
        @pl.when(pl.program_id(0) == 0)
        def _():
            dg_ref[...] = dg_in_ref[...]

        dh = _dot(dr_ref[...], wr_ref[...])
        for d_ref, w_ref, dil in zip((d0, d1, d2), (w0, w1, w2), DILATIONS):
            piece = d_ref[...] if dil == 1 else _get_tokens(slab, d_ref, dil, 3 * GROUP_W, 0, 3 * GROUP_W).astype(BF16)
            dh = dh + _dot(piece, w_ref[...])
        dres, dg = _rmsnorm_bwd(dh, x_ref[...], g_ref[...])
        gx_ref[...] = dx1_ref[...] + dres
        dg_ref[...] += dg

    tok = lambda w: pl.BlockSpec((tm, w), lambda i: (i + off, 0))
    full = lambda *s: pl.BlockSpec(s, lambda i: (0,) * len(s))
    in_specs = ([pl.BlockSpec((tm // d, d * 3 * GROUP_W), lambda i: (i + off, 0)) for d in DILATIONS] + [tok(REST_W)]
                + [_resident((3 * GROUP_W, D_MODEL))] * 3 + [_resident((REST_W, D_MODEL))]
                + [tok(D_MODEL), tok(D_MODEL), full(1, D_MODEL), full(1, D_MODEL), HBM_SPEC])
    return _call(
        body, name=f"in_proj_bwd_{off}", grid=(steps,), in_specs=in_specs,
        out_specs=[tok(D_MODEL), full(1, D_MODEL)],
        out_shape=[jax.ShapeDtypeStruct((seq, D_MODEL), F32), jax.ShapeDtypeStruct((1, D_MODEL), F32)],
        scratch_shapes=[pltpu.VMEM((3 * GROUP_W // LANES, tm, LANES), F32)],
        params=_params(("arbitrary",), 48), args=(*dqkv, drest, *w_qkv, w_rest, x, dx1, g0, dg_so_far, gx_so_far),
        rider=rider, aliases={len(in_specs) - 1: 0})


def _adamw(w, g, m, v, name):
    rows, cols = w.shape
    tr = _row_tile(rows) if rows % 16 == 0 else rows
    c1 = 1.0 - ADAM_B1 ** ADAM_STEP
    c2 = 1.0 - ADAM_B2 ** ADAM_STEP

    def body(w_ref, g_ref, m_ref, v_ref, go_ref, d_ref, nm_ref, nv_ref):
        gv = g_ref[...]
        go_ref[...] = gv
        nm = ADAM_B1 * m_ref[...] + (1.0 - ADAM_B1) * gv
        nv = ADAM_B2 * v_ref[...] + (1.0 - ADAM_B2) * (gv * gv)
        d_ref[...] = -ADAM_LR * ((nm / c1) / (jnp.sqrt(nv / c2) + ADAM_EPS) + ADAM_WD * w_ref[...])
        nm_ref[...] = nm
        nv_ref[...] = nv

    spec = pl.BlockSpec((tr, cols), lambda i: (i, 0))
    return _pallas(
        body, name=name, grid=(rows // tr,),
        in_specs=[spec] * 4, out_specs=[spec] * 4,
        out_shape=[jax.ShapeDtypeStruct((rows, cols), F32)] * 4,
        compiler_params=_params(("arbitrary",), 40),
    )(*map(_in_hbm, (w, g, m, v)))


def _place():
    x, y, c = lax.axis_index("x"), lax.axis_index("y"), lax.axis_index("c")
    chips = [(1 - x, y), (x, 1 - y), (1 - x, 1 - y)]
    return x, y, c, chips


class _Exchange:
    def __init__(self, inputs, out_shapes, n_sems, start, finish, aliases=None):
        self.inputs, self.out_shapes, self.n_sems = list(inputs), list(out_shapes), n_sems
        self.start, self.finish, self.aliases = start, finish, dict(aliases or {})

    def scratch(self):
        return [pltpu.SemaphoreType.DMA((self.n_sems,)), pltpu.SemaphoreType.DMA((self.n_sems,))]


def _together(*parts):
    ins = [len(p.inputs) for p in parts]
    outs = [len(p.out_shapes) for p in parts]

    def split(refs, counts):
        pos, pieces = 0, []
        for cnt in counts:
            pieces.append(refs[pos:pos + cnt])
            pos += cnt
        return pieces

    def run(which):
        def go(in_refs, out_refs, *sems):
            for k, (p, i, o) in enumerate(zip(parts, split(in_refs, ins), split(out_refs, outs))):
                getattr(p, which)(i, o, sems[2 * k], sems[2 * k + 1])
        return go

    both = _Exchange([a for p in parts for a in p.inputs], [s for p in parts for s in p.out_shapes], 0, run("start"),
                     run("finish"))
    both.aliases = {sum(ins[:k]) + i: sum(outs[:k]) + o for k, p in enumerate(parts) for i, o in p.aliases.items()}
    both.scratch = lambda: [s for p in parts for s in p.scratch()]
    return both


def _run_exchange(ex, name):
    n_in, n_out = len(ex.inputs), len(ex.out_shapes)

    def body(*refs):
        ins, outs, sems = refs[:n_in], refs[n_in:n_in + n_out], refs[n_in + n_out:]
        ex.start(ins, outs, *sems)
        ex.finish(ins, outs, *sems)

    return _pallas(
        body, name=name, in_specs=[HBM_SPEC] * n_in, out_specs=[HBM_SPEC] * n_out, out_shape=ex.out_shapes,
        scratch_shapes=ex.scratch(), input_output_aliases=ex.aliases,
    )(*ex.inputs)


def _call(body, *, name, grid, in_specs, out_specs, out_shape, scratch_shapes, params, args, rider=None, aliases=None):
    in_specs, out_specs, out_shape, scratch_shapes = list(in_specs), list(out_specs), list(out_shape), list(scratch_shapes)
    aliases = dict(aliases or {})
    args = [_in_hbm(a) for a in args]
    if rider is None:
        outs = _pallas(body, name=name, grid=grid, in_specs=in_specs, out_specs=out_specs, out_shape=out_shape,
                              scratch_shapes=scratch_shapes, input_output_aliases=aliases, compiler_params=params)(*args)
        return list(outs), []
    n_in, n_out, n_scr = len(in_specs), len(out_specs), len(scratch_shapes)
    r_in, r_out = len(rider.inputs), len(rider.out_shapes)

    def wrapped(*refs):
        ins, r_ins = refs[:n_in], refs[n_in:n_in + r_in]
        pos = n_in + r_in
        outs, r_outs = refs[pos:pos + n_out], refs[pos + n_out:pos + n_out + r_out]
        pos += n_out + r_out
        scr, sems = refs[pos:pos + n_scr], refs[pos + n_scr:]
        ids = [pl.program_id(k) for k in range(len(grid))]
        first, last = ids[0] == 0, ids[0] == grid[0] - 1
        for k in range(1, len(grid)):
            first, last = first & (ids[k] == 0), last & (ids[k] == grid[k] - 1)

        @pl.when(first)
        def _():
            rider.start(r_ins, r_outs, *sems)

        body(*ins, *outs, *scr)

        @pl.when(last)
        def _():
            rider.finish(r_ins, r_outs, *sems)

    outs = _pallas(
        wrapped, name=name, grid=grid, in_specs=in_specs + [HBM_SPEC] * r_in, out_specs=out_specs + [HBM_SPEC] * r_out,
        out_shape=out_shape + rider.out_shapes, scratch_shapes=scratch_shapes + rider.scratch(),
        input_output_aliases={**aliases, **{n_in + i: n_out + o for i, o in rider.aliases.items()}}, compiler_params=params,
    )(*args, *rider.inputs)
    return list(outs[:n_out]), list(outs[n_out:])


def _stage_weights(shards):
    n = len(shards)

    def body(*refs):
        ins, outs, stages, sems = refs[:n], refs[n:2 * n], refs[2 * n:3 * n], refs[3 * n]
        x, y, _, _ = _place()
        copies = []
        for t in range(n):
            stages[t][...] = ins[t][...].astype(BF16)
            copies.append(pltpu.make_async_copy(stages[t], outs[t].at[2 * x + y], sems.at[t]))
            copies[-1].start()
        for cp in copies:
            cp.wait()

    assert sum(s.size * 6 for s in shards) <= (CALL_VMEM_MIB - 8) * MIB
    return _pallas(
        body, name="stage_weights", in_specs=[VMEM_SPEC] * n, out_specs=[HBM_SPEC] * n,
        out_shape=[jax.ShapeDtypeStruct((N_CHIPS,) + s.shape, BF16) for s in shards],
        scratch_shapes=[pltpu.VMEM(s.shape, BF16) for s in shards] + [pltpu.SemaphoreType.DMA((n,))],
        compiler_params=pltpu.CompilerParams(vmem_limit_bytes=CALL_VMEM_MIB * MIB),
    )(*shards)


def _gather(buffers, stage="both", part=(0, 1)):
    n = len(buffers)
    halves = [b.shape[1] // part[1] // 2 for b in buffers]

    def half_of(outs, t, chip, which):
        return outs[t].at[chip, pl.ds((2 * part[0] + which) * halves[t], halves[t]), :]

    def copy(outs, sems, t, k, chip, which, to):
        rows = half_of(outs, t, chip, which)
        return pltpu.make_async_remote_copy(src_ref=rows, dst_ref=rows, send_sem=sems[0].at[6 * t + k],
                                            recv_sem=sems[1].at[6 * t + k], device_id=to, device_id_type=MESH)

    def to_chips(outs, sems, what):
        x, y, c, chips = _place()
        for t in range(n):
            for j, (px, py) in enumerate(chips):
                if what == "start":
                    copy(outs, sems, t, j, 2 * x + y, c, (px, py, c)).start()
                else:
                    copy(outs, sems, t, j, 2 * px + py, c, (px, py, c)).wait_recv()
                    copy(outs, sems, t, j, 2 * x + y, c, (px, py, c)).wait_send()

    def to_sibling(outs, sems, what):
        x, y, c, chips = _place()
        for t in range(n):
            for j, (px, py) in enumerate(chips):
                if what == "start":
                    copy(outs, sems, t, 3 + j, 2 * px + py, c, (x, y, 1 - c)).start()
                else:
                    copy(outs, sems, t, 3 + j, 2 * px + py, 1 - c, (x, y, 1 - c)).wait_recv()
                    copy(outs, sems, t, 3 + j, 2 * px + py, c, (x, y, 1 - c)).wait_send()

    def start(ins, outs, *sems):
        (to_sibling if stage == "pair" else to_chips)(outs, sems, "start")

    def finish(ins, outs, *sems):
        if stage != "pair":
            to_chips(outs, sems, "finish")
        if stage == "both":
            to_sibling(outs, sems, "start")
        if stage != "chips":
            to_sibling(outs, sems, "finish")

    return _Exchange(buffers, [jax.ShapeDtypeStruct(b.shape, b.dtype) for b in buffers], 6 * n, start, finish,
                     aliases={t: t for t in range(n)})


def _pair_exchange(grads):
    n = len(grads)
    halves = [g.shape[1] // 2 for g in grads]

    def copies(ins, outs, send_sems, recv_sems):
        x, y, c, _ = _place()
        return [pltpu.make_async_remote_copy(
            src_ref=ins[t].at[:, pl.ds((1 - c) * halves[t], halves[t]), :], dst_ref=outs[t],
            send_sem=send_sems.at[t], recv_sem=recv_sems.at[t], device_id=(x, y, 1 - c), device_id_type=MESH)
            for t in range(n)]

    def start(*refs):
        for cp in copies(*refs):
            cp.start()

    def finish(*refs):
        for cp in copies(*refs):
            cp.wait()

    return _Exchange(grads, [jax.ShapeDtypeStruct((N_CHIPS, h, g.shape[2]), F32) for g, h in zip(grads, halves)], n,
                     start, finish)


def _row_tile(rows):
    return max(t for t in range(16, 257, 16) if rows % t == 0)


def _pair_add(grad, other, place, name):
    _, rows, cols = grad.shape
    rh = rows // 2
    tr = _row_tile(rh)
    nb = rh // tr

    def body(p_ref, g_ref, a_ref, wire_ref, own_ref):
        s = g_ref[...] + a_ref[...]
        wire_ref[...] = s.astype(BF16)

        @pl.when(pl.program_id(1) == p_ref[1])
        def _():
            own_ref[...] = s

    blk = (None, tr, cols)
    return _pallas(
        body, name=name,
        grid_spec=pltpu.PrefetchScalarGridSpec(
            num_scalar_prefetch=1, grid=(nb, N_CHIPS),
            in_specs=[pl.BlockSpec(blk, lambda i, j, p: (j, p[0] * nb + i, 0)), pl.BlockSpec(blk, lambda i, j, p: (j, i, 0))],
            out_specs=[pl.BlockSpec(blk, lambda i, j, p: (j, i, 0)), pl.BlockSpec((tr, cols), lambda i, j, p: (i, 0))]),
        out_shape=[jax.ShapeDtypeStruct((N_CHIPS, rh, cols), BF16), jax.ShapeDtypeStruct((rh, cols), F32)],
        compiler_params=_params(("arbitrary", "arbitrary"), 32),
    )(place, grad, other)


def _chip_exchange(wires):
    n = len(wires)

    def copies(ins, outs, send_sems, recv_sems):
        x, y, c, chips = _place()
        return [pltpu.make_async_remote_copy(
            src_ref=ins[t].at[2 * px + py], dst_ref=outs[t].at[j], send_sem=send_sems.at[3 * t + j],
            recv_sem=recv_sems.at[3 * t + j], device_id=(px, py, c), device_id_type=MESH)
            for t in range(n) for j, (px, py) in enumerate(chips)]

    def start(*refs):
        for cp in copies(*refs):
            cp.start()

    def finish(*refs):
        for cp in copies(*refs):
            cp.wait()

    return _Exchange(wires, [jax.ShapeDtypeStruct((3,) + w.shape[1:], BF16) for w in wires], 3 * n, start, finish)


def _chip_add(own, arrived, place, name):
    rh, cols = own.shape
    tr = _row_tile(rh)
    nb = rh // tr

    def body(p_ref, s_ref, b0, b1, b2, o_ref):
        o_ref[...] = ((s_ref[...] + b0[...].astype(F32)) + b1[...].astype(F32)) + b2[...].astype(F32)

    blk = (None, tr, cols)
    return _pallas(
        body, name=name,
        grid_spec=pltpu.PrefetchScalarGridSpec(
            num_scalar_prefetch=1, grid=(nb,),
            in_specs=[pl.BlockSpec((tr, cols), lambda i, p: (i, 0)), pl.BlockSpec(blk, lambda i, p: (0, i, 0)),
                      pl.BlockSpec(blk, lambda i, p: (1, i, 0)), pl.BlockSpec(blk, lambda i, p: (2, i, 0))],
            out_specs=pl.BlockSpec((tr, cols), lambda i, p: (p[0] * nb + i, 0))),
        out_shape=jax.ShapeDtypeStruct((2 * rh, cols), F32),
        compiler_params=_params(("arbitrary",), 32),
    )(place, own, arrived, arrived, arrived)


def _pair_share(halves):
    n = len(halves)
    rhs = [h.shape[0] // 2 for h in halves]

    def copy(outs, send_sems, recv_sems, t, which):
        x, y, c, _ = _place()
        rows = outs[t].at[pl.ds(which * rhs[t], rhs[t]), :]
        return pltpu.make_async_remote_copy(src_ref=rows, dst_ref=rows, send_sem=send_sems.at[t], recv_sem=recv_sems.at[t],
                                            device_id=(x, y, 1 - c), device_id_type=MESH)

    def start(ins, outs, send_sems, recv_sems):
        c = lax.axis_index("c")
        for t in range(n):
            copy(outs, send_sems, recv_sems, t, c).start()

    def finish(ins, outs, send_sems, recv_sems):
        c = lax.axis_index("c")
        for t in range(n):
            copy(outs, send_sems, recv_sems, t, c).wait_send()
            copy(outs, send_sems, recv_sems, t, 1 - c).wait_recv()

    return _Exchange(halves, [jax.ShapeDtypeStruct(h.shape, F32) for h in halves], n, start, finish,
                     aliases={t: t for t in range(n)})


class _GradReduction:
    def __init__(self, grads, place, tag):
        self.names, self.grads, self.place, self.tag = list(grads), grads, place, tag

    def pair_exchange(self):
        return _pair_exchange([self.grads[n] for n in self.names])

    def chip_exchange(self, others):
        sums = [_pair_add(self.grads[n], o, self.place, f"{self.tag}_pair_add_{n}") for n, o in zip(self.names, others)]
        self.owns = [own for _, own in sums]
        return _chip_exchange([wire for wire, _ in sums])

    def pair_share(self, arrived):
        return _pair_share([_chip_add(own, arr, self.place, f"{self.tag}_chip_add_{n}")
                            for n, own, arr in zip(self.names, self.owns, arrived)])

    def result(self, shared):
        return dict(zip(self.names, shared))


def _all_reduce_small(p):
    rows, lanes = p.shape
    half = rows // 2

    def body(p_ref, o_ref, sib, sums, send_sems, recv_sems):
        x, y, c, chips = _place()
        mine, sibling = 2 * x + y, (x, y, 1 - c)
        swap = pltpu.make_async_remote_copy(src_ref=p_ref, dst_ref=sib, send_sem=send_sems.at[0], recv_sem=recv_sems.at[0],
                                            device_id=sibling, device_id_type=MESH)
        swap.start()
        swap.wait()
        sums[mine] = p_ref[...] + sib[...]

        def copy(k, chip, which, to):
            part = sums.at[chip, pl.ds(which * half, half), :]
            return pltpu.make_async_remote_copy(src_ref=part, dst_ref=part, send_sem=send_sems.at[k], recv_sem=recv_sems.at[k],
                                                device_id=to, device_id_type=MESH)

        for j, (px, py) in enumerate(chips):
            copy(1 + j, mine, c, (px, py, c)).start()
        for j, (px, py) in enumerate(chips):
            copy(1 + j, 2 * px + py, c, (px, py, c)).wait_recv()
            copy(4 + j, 2 * px + py, c, sibling).start()
        for j, (px, py) in enumerate(chips):
            copy(4 + j, 2 * px + py, 1 - c, sibling).wait_recv()
        for j, (px, py) in enumerate(chips):
            copy(1 + j, mine, c, (px, py, c)).wait_send()
            copy(4 + j, 2 * px + py, c, sibling).wait_send()
        o_ref[...] = ((sums[0] + sums[1]) + sums[2]) + sums[3]

    return _pallas(
        body, name="small_all_reduce", in_specs=[VMEM_SPEC], out_specs=VMEM_SPEC,
        out_shape=jax.ShapeDtypeStruct((rows, lanes), F32),
        scratch_shapes=[pltpu.VMEM((rows, lanes), F32), pltpu.VMEM((N_CHIPS, rows, lanes), F32),
                        pltpu.SemaphoreType.DMA((7,)), pltpu.SemaphoreType.DMA((7,))],
        compiler_params=pltpu.CompilerParams(vmem_limit_bytes=CALL_VMEM_MIB * MIB),
    )(p)


BIG = ("w_in", "w_branch_attn", "w_branch_gmlp", "w_out", "w_mlp_in", "w_mlp_out")
COLUMN_SHARDED = ("w_branch_attn", "w_branch_gmlp", "w_mlp_in")
SMALL = ("norm_pre_mix", "w_spatial", "b_spatial", "ln_v_gain", "ln_v_bias", "norm_post_mix", "norm_pre_mlp", "norm_post_mlp")
ORDER = ("norm_pre_mix", "w_in", "w_spatial", "b_spatial", "ln_v_gain", "ln_v_bias", "w_branch_attn", "w_branch_gmlp",
         "w_out", "norm_post_mix", "norm_pre_mlp", "w_mlp_in", "w_mlp_out", "norm_post_mlp")


def _full_weight(name, gathered):
    if name in COLUMN_SHARDED:
        return jnp.transpose(gathered, (1, 0, 2)).reshape(gathered.shape[1], -1)
    return gathered.reshape(-1, gathered.shape[2])


def _rows8(a):
    a = a.reshape(-1, 128)
    pad = (-a.shape[0]) % 8
    return jnp.pad(a, ((0, pad), (0, 0))) if pad else a


def _qkv_columns(group):
    return [(sec * ATTN_W + group * GROUP_W, sec * ATTN_W + (group + 1) * GROUP_W) for sec in range(3)]


def _device_step(x, target, small, shards, place):
    seq = x.shape[0]
    g0, g1, g2, g3 = small["norm_pre_mix"], small["norm_post_mix"], small["norm_pre_mlp"], small["norm_post_mlp"]
    w_sp = small["w_spatial"]
    b_col = small["b_spatial"].reshape(GMLP_GROUPS, CHUNK, 1)
    ln_g, ln_b = small["ln_v_gain"], small["ln_v_bias"]

    staged = _stage_weights(shards)
    tables, w_in = _rope_tables(seq, rider=_gather(staged[:1], part=(0, 2)))
    h, (w_in,) = _norm_in(x, g0, rider=_gather(w_in, part=(1, 2)))
    w_in = _full_weight("w_in", w_in)
    (*qkv, rest), landed = _in_proj(h[0], w_in, *tables[1], rider=_gather(staged[1:], "chips"))

    o_l, gathered = _attn_fwd(qkv[0], DILATIONS[0], rider=_gather(landed, "pair"))
    full = {n: _full_weight(n, gw) for n, gw in zip(BIG[1:], gathered)}
    for g in range(1, N_GROUPS):
        o_l.extend(_attn_fwd(qkv[g], DILATIONS[g])[0])
    (*ya_l, yg, mg, y, x1), _ = _mix_fwd(o_l, rest, x, w_sp, b_col, ln_g, ln_b, full["w_branch_attn"],
                                        full["w_branch_gmlp"], full["w_out"], g1)
    ya, lse = ya_l[0::2], ya_l[1::2]
    h2, a, dy2, dout, loss8, dg3 = _mlp_fwd(x1, g2, g3, full["w_mlp_in"], full["w_mlp_out"], target)
    d_wmo, _ = _tn_matmul(a, dy2, "grad_w_mlp_out", 1024, 1024, square_a=True)
    mlp_out = _GradReduction({"w_mlp_out": d_wmo.reshape(N_CHIPS, D_FF // N_CHIPS, D_MODEL)}, place, "mlp_out")
    (dap, dx1, dy, dg2, dg1), riding = _mlp_bwd(dy2, a, full["w_mlp_out"], full["w_mlp_in"], dout, x1, y, g2, g1,
                                                 rider=mlp_out.pair_exchange())
    d_wmi, riding = _tn_matmul(h2, dap, "grad_w_mlp_in", 1024, 1024, column_shards=True,
                               rider=mlp_out.chip_exchange(riding))
    mlp_in = _GradReduction({"w_mlp_in": d_wmi}, place, "mlp_in")
    (*dya, drest, d_wout, d_wba, d_wbg, d_wsp, d_bb, d_lg, d_lb), riding = _mix_bwd(
        dy, ya[0], yg, mg, rest, full["w_out"], full["w_branch_attn"], full["w_branch_gmlp"], w_sp, b_col, ln_g, ln_b,
        rider=_together(mlp_out.pair_share(riding), mlp_in.pair_exchange()))
    reduced = mlp_out.result(riding[:1])
    mix = _GradReduction({"w_branch_attn": d_wba, "w_branch_gmlp": d_wbg,
                          "w_out": d_wout.reshape(N_CHIPS, D_MODEL // N_CHIPS, D_MODEL)}, place, "mix")
    attn = lambda g, rider: _attn_bwd(qkv[g], dya[g], ya[g], lse[g], *tables[DILATIONS[g]], DILATIONS[g], rider=rider)
    dqkv0, riding = attn(0, _together(mlp_in.chip_exchange(riding[1:]), mix.pair_exchange()))
    dqkv1, riding = attn(1, _together(mlp_in.pair_share(riding[:1]), mix.chip_exchange(riding[1:])))
    reduced.update(mlp_in.result(riding[:1]))
    dqkv2, riding = attn(2, mix.pair_share(riding[1:]))
    reduced.update(mix.result(riding))
    dqkv = [dqkv0, dqkv1, dqkv2]

    d_qkv = [_tn_matmul_residue(dqkv[g], h[g], dil, f"grad_w_in_qkv{g}") for g, dil in enumerate(DILATIONS)]
    d_rest, _ = _tn_matmul(drest, h[0], "grad_w_in_rest", 1024, 1024)
    d_win = jnp.concatenate([d_qkv[g][s * GROUP_W:(s + 1) * GROUP_W] for s in range(3) for g in range(N_GROUPS)]
                            + [d_rest], axis=0)
    first = _GradReduction({"w_in": d_win.reshape(N_CHIPS, IN_W // N_CHIPS, D_MODEL)}, place, "w_in")
    w_qkv = [jnp.concatenate([w_in[lo:hi] for lo, hi in _qkv_columns(g)], axis=0) for g in range(N_GROUPS)]
    w_rest = w_in[QKV_W:]
    tiles = seq // IN_PROJ_BWD_TM
    so_far = (lax.empty((seq, D_MODEL), F32), jnp.zeros((1, D_MODEL), F32))
    in_bwd = lambda so_far, span, rider: _in_proj_bwd(dqkv, drest, w_qkv, w_rest, x, dx1, g0, so_far, span, rider=rider)
    so_far, riding = in_bwd(so_far, (0, 3 * tiles // 8), first.pair_exchange())
    (grad_x, dg0), riding = in_bwd(so_far, (3 * tiles // 8, 5 * tiles // 8), first.chip_exchange(riding))
    reduced.update(first.result(_run_exchange(first.pair_share(riding), "w_in_pair_share")))
    little = {"norm_pre_mix": dg0, "w_spatial": d_wsp, "b_spatial": d_bb[:, :, 0], "ln_v_gain": d_lg, "ln_v_bias": d_lb,
              "norm_post_mix": dg1, "norm_pre_mlp": dg2, "norm_post_mlp": dg3}
    return loss8, grad_x, reduced, little


def kernel(x, norm_pre_mix, w_in, w_spatial, b_spatial, ln_v_gain, ln_v_bias, w_branch_attn, w_branch_gmlp, w_out, norm_post_mix, norm_pre_mlp, w_mlp_in, w_mlp_out, norm_post_mlp, loss_target, m_norm_pre_mix, m_w_in, m_w_spatial, m_b_spatial, m_ln_v_gain, m_ln_v_bias, m_w_branch_attn, m_w_branch_gmlp, m_w_out, m_norm_post_mix, m_norm_pre_mlp, m_w_mlp_in, m_w_mlp_out, m_norm_post_mlp, v_norm_pre_mix, v_w_in, v_w_spatial, v_b_spatial, v_ln_v_gain, v_ln_v_bias, v_w_branch_attn, v_w_branch_gmlp, v_w_out, v_norm_post_mix, v_norm_pre_mlp, v_w_mlp_in, v_w_mlp_out, v_norm_post_mlp):
    given = dict(norm_pre_mix=norm_pre_mix, w_in=w_in, w_spatial=w_spatial, b_spatial=b_spatial, ln_v_gain=ln_v_gain,
                 ln_v_bias=ln_v_bias, w_branch_attn=w_branch_attn, w_branch_gmlp=w_branch_gmlp, w_out=w_out,
                 norm_post_mix=norm_post_mix, norm_pre_mlp=norm_pre_mlp, w_mlp_in=w_mlp_in, w_mlp_out=w_mlp_out,
                 norm_post_mlp=norm_post_mlp)
    moments_m = dict(norm_pre_mix=m_norm_pre_mix, w_in=m_w_in, w_spatial=m_w_spatial, b_spatial=m_b_spatial,
                     ln_v_gain=m_ln_v_gain, ln_v_bias=m_ln_v_bias, w_branch_attn=m_w_branch_attn,
                     w_branch_gmlp=m_w_branch_gmlp, w_out=m_w_out, norm_post_mix=m_norm_post_mix,
                     norm_pre_mlp=m_norm_pre_mlp, w_mlp_in=m_w_mlp_in, w_mlp_out=m_w_mlp_out, norm_post_mlp=m_norm_post_mlp)
    moments_v = dict(norm_pre_mix=v_norm_pre_mix, w_in=v_w_in, w_spatial=v_w_spatial, b_spatial=v_b_spatial,
                     ln_v_gain=v_ln_v_gain, ln_v_bias=v_ln_v_bias, w_branch_attn=v_w_branch_attn,
                     w_branch_gmlp=v_w_branch_gmlp, w_out=v_w_out, norm_post_mix=v_norm_post_mix,
                     norm_pre_mlp=v_norm_pre_mlp, w_mlp_in=v_w_mlp_in, w_mlp_out=v_w_mlp_out, norm_post_mlp=v_norm_post_mlp)
    cx, cy, cc = lax.axis_index("x"), lax.axis_index("y"), lax.axis_index("c")

    shards = [given[n][0].T if n == "w_in" else given[n][0] for n in BIG]
    small = {n: given[n][0] if given[n].ndim > 2 else given[n] for n in SMALL}
    place = jnp.stack([cc, 2 * cx + cy]).astype(jnp.int32)
    loss8, grad_x, grad_shard, grads = _device_step(x[0], loss_target[0], small, shards, place)

    packed = jnp.concatenate([_rows8(grads[n]) for n in SMALL] + [loss8], axis=0)
    summed = _all_reduce_small(packed)
    loss = summed[packed.shape[0] - loss8.shape[0], 0]
    row = 0
    for n in SMALL:
        shape = given[n][0].shape
        cnt = -(-(given[n][0].size // 128) // 8) * 8
        grad_shard[n] = summed[row:row + given[n][0].size // 128].reshape(shape)
        row += cnt

    grad_out, deltas, new_m, new_v = {}, {}, {}, {}
    for n in ORDER:
        shape = given[n].shape
        if n == "w_in":
            outs = _adamw(given[n][0].T, grad_shard[n], moments_m[n][0].T, moments_v[n][0].T, "adamw_" + n)
            outs = [o.T for o in outs]
        else:
            two_d = (-1, shape[-1])
            outs = _adamw(given[n].reshape(two_d), grad_shard[n].reshape(two_d), moments_m[n].reshape(two_d),
                          moments_v[n].reshape(two_d), "adamw_" + n)
        grad_out[n], deltas[n], new_m[n], new_v[n] = [o.reshape(shape) for o in outs]
    return (loss, grad_x[None], *[grad_out[n] for n in ORDER], *[deltas[n] for n in ORDER], *[new_m[n] for n in ORDER],
            *[new_v[n] for n in ORDER])
```

```python
import math

import jax
import jax.numpy as jnp
from jax import lax
from jax.experimental import pallas as pl
from jax.experimental.pallas import tpu as pltpu

F32 = jnp.float32
BF16 = jnp.bfloat16
MESH = pl.DeviceIdType.MESH

D_MODEL = 1024
HEAD_DIM = 64
HEADS_PER_GROUP = 4
GROUP_W = HEADS_PER_GROUP * HEAD_DIM
DILATIONS = (1, 4, 16)
N_GROUPS = len(DILATIONS)
ATTN_W = N_GROUPS * GROUP_W
QKV_W = 3 * ATTN_W
GMLP_W = 512
GMLP_GROUPS = 4
CHUNK = 128
REST_W = 2 * GMLP_W + 2 * D_MODEL
IN_W = QKV_W + REST_W
D_FF = 4096
QBLK = 128
ROPE_THETA = 10000.0
EPS = 1e-6
NEG = -1e30
SCALE = HEAD_DIM ** -0.5
N_CHIPS = 4

ADAM_LR = 0.001
ADAM_B1 = 0.9
ADAM_B2 = 0.999
ADAM_EPS = 1e-08
ADAM_WD = 0.01
ADAM_STEP = 10

MIB = 1024 * 1024
HBM_SPEC = pl.BlockSpec(memory_space=pltpu.HBM)
VMEM_SPEC = pl.BlockSpec(memory_space=pltpu.VMEM)


MLP_FWD_TM = 512
MLP_TM = 256


CALL_VMEM_MIB = 56
SMALL_VMEM_MIB = 32


def _params(semantics, vmem_mib, small=False):
    assert vmem_mib <= CALL_VMEM_MIB
    return pltpu.CompilerParams(dimension_semantics=semantics,
                                vmem_limit_bytes=(SMALL_VMEM_MIB if small else CALL_VMEM_MIB) * MIB)


def _in_hbm(a):
    return pltpu.with_memory_space_constraint(a, pltpu.HBM) if a.size * a.dtype.itemsize >= MIB else a


def _pallas(body, **kwargs):
    return pl.pallas_call(body, **kwargs)


def _resident(shape):
    return pl.BlockSpec(shape, lambda *_: (0,) * len(shape), pipeline_mode=pl.Buffered(1))


def _dot(a, b):
    return jnp.dot(a, b, preferred_element_type=F32)


def _dot_nt(a, b):
    return lax.dot_general(a, b, (((1,), (1,)), ((), ())), preferred_element_type=F32)


def _dot_tn(a, b):
    return lax.dot_general(a, b, (((0,), (0,)), ((), ())), preferred_element_type=F32)


_GELU_C = math.sqrt(2.0 / math.pi)


def _gelu(x):
    return x * (0.5 * (1.0 + jnp.tanh(_GELU_C * (x + 0.044715 * (x * x * x)))))


def _gelu_grad(x):
    t = jnp.tanh(_GELU_C * (x + 0.044715 * (x * x * x)))
    return 0.5 * (1.0 + t) + 0.5 * x * (1.0 - t * t) * (_GELU_C * (1.0 + 3.0 * 0.044715 * (x * x)))


def _rsqrt_ms(v):
    return lax.rsqrt(jnp.mean(v * v, axis=-1, keepdims=True) + EPS)


def _rmsnorm_bwd(dn, src, gain):
    r = _rsqrt_ms(src)
    t = gain * dn
    dgain = jnp.sum(dn * (src * r), axis=0, keepdims=True)
    dsrc = r * t - src * ((r * r * r) * jnp.mean(t * src, axis=-1, keepdims=True))
    return dsrc, dgain


def _rot_half(v):
    w = v.shape[-1]
    lane = lax.broadcasted_iota(jnp.int32, v.shape, v.ndim - 1)
    return jnp.where((lane % HEAD_DIM) < HEAD_DIM // 2, pltpu.roll(v, w - HEAD_DIM // 2, v.ndim - 1),
                     pltpu.roll(v, HEAD_DIM // 2, v.ndim - 1))


def _head_masks(shape):
    lane = lax.broadcasted_iota(jnp.int32, shape, 1)
    return [(lane >= h * HEAD_DIM) & (lane < (h + 1) * HEAD_DIM) for h in range(HEADS_PER_GROUP)]


def _head_stack(block, hmask):
    zero = jnp.zeros((), block.dtype)
    return jnp.concatenate([jnp.where(hm, block, zero) for hm in hmask], axis=0)


LANES = 128


def _put_residue(slab, val, out_ref, dil, width, col0):
    tm, w = val.shape
    if dil == 1:
        out_ref[:, col0:col0 + w] = val.astype(out_ref.dtype)
        return
    for k in range(w // LANES):
        slab[k] = val[:, k * LANES:(k + 1) * LANES]
    for r in range(dil):
        for k in range(w // LANES):
            c = r * width + col0 + k * LANES
            out_ref[:, c:c + LANES] = slab[k, pl.ds(r, tm // dil, stride=dil), :].astype(out_ref.dtype)


def _get_tokens(slab, in_ref, dil, width, col0, w):
    if dil == 1:
        return in_ref[:, col0:col0 + w].astype(F32)
    rows = in_ref.shape[0]
    for r in range(dil):
        for k in range(w // LANES):
            c = r * width + col0 + k * LANES
            slab[k, pl.ds(r, rows, stride=dil), :] = in_ref[:, c:c + LANES].astype(F32)
    return jnp.concatenate([slab[k] for k in range(w // LANES)], axis=1)


def _rope_tables(seq, rider=None):
    half = HEAD_DIM // 2
    inv_freq = ROPE_THETA ** (-jnp.arange(half, dtype=F32) / half)
    freq = jnp.tile(inv_freq, LANES // half).reshape(1, LANES)
    tm = 512

    def body(f_ref, *refs):
        outs, slab_c, slab_s = refs[:-2], refs[-2], refs[-1]
        row = lax.broadcasted_iota(jnp.int32, (tm, LANES), 0) + pl.program_id(0) * tm
        lane = lax.broadcasted_iota(jnp.int32, (tm, LANES), 1)
        ang = row.astype(F32) * f_ref[...]
        cos = jnp.cos(ang)
        sin = jnp.where((lane % HEAD_DIM) < half, -jnp.sin(ang), jnp.sin(ang))
        slab_c[0] = cos
        slab_s[0] = sin
        for i, dil in enumerate(DILATIONS):
            for tab, slab in ((outs[2 * i], slab_c), (outs[2 * i + 1], slab_s)):
                for r in range(dil):
                    piece = slab[0, pl.ds(r, tm // dil, stride=dil), :] if dil > 1 else slab[0]
                    for k in range(GROUP_W // LANES):
                        tab[:, r * GROUP_W + k * LANES:r * GROUP_W + (k + 1) * LANES] = piece

    outs, riding = _call(
        body, name="rope_tables", grid=(seq // tm,),
        in_specs=[pl.BlockSpec((1, LANES), lambda i: (0, 0))],
        out_specs=[pl.BlockSpec((tm // d, d * GROUP_W), lambda i: (i, 0)) for d in DILATIONS for _ in range(2)],
        out_shape=[jax.ShapeDtypeStruct((seq // d, d * GROUP_W), F32) for d in DILATIONS for _ in range(2)],
        scratch_shapes=[pltpu.VMEM((1, tm, LANES), F32)] * 2,
        params=_params(("arbitrary",), 32), args=(freq,), rider=rider)
    return {d: (outs[2 * i], outs[2 * i + 1]) for i, d in enumerate(DILATIONS)}, riding


def _norm_in(x, g0, rider=None):
    seq = x.shape[0]
    tm = 256

    def body(x_ref, g_ref, *refs):
        h_refs, slab = refs[:N_GROUPS], refs[-1]
        xv = x_ref[...]
        hf = (xv * _rsqrt_ms(xv)) * g_ref[...]
        for g, dil in enumerate(DILATIONS):
            _put_residue(slab, hf, h_refs[g], dil, D_MODEL, 0)

    return _call(
        body, name="norm_in", grid=(seq // tm,),
        in_specs=[pl.BlockSpec((tm, D_MODEL), lambda i: (i, 0)), pl.BlockSpec((1, D_MODEL), lambda i: (0, 0))],
        out_specs=[pl.BlockSpec((tm // d, d * D_MODEL), lambda i: (i, 0)) for d in DILATIONS],
        out_shape=[jax.ShapeDtypeStruct((seq // d, d * D_MODEL), BF16) for d in DILATIONS],
        scratch_shapes=[pltpu.VMEM((D_MODEL // LANES, tm, LANES), F32)],
        params=_params(("arbitrary",), 32), args=(x, g0), rider=rider)


def _in_proj(h, w_in, cos_t, sin_t, rider=None):
    seq = h.shape[0]
    tm, tn = 512, GROUP_W
    n_qk = 2 * ATTN_W // tn
    n_qkv = QKV_W // tn

    def body(h_ref, w_ref, cos_ref, sin_ref, *refs):
        qkv_refs, rest_ref, slab = refs[:N_GROUPS], refs[N_GROUPS], refs[-1]
        hb = h_ref[...]
        cos, sin = cos_ref[...], sin_ref[...]
        for j in range(IN_W // tn):
            p = _dot_nt(hb, w_ref[j * tn:(j + 1) * tn, :])
            if j < n_qkv:
                if j < n_qk:
                    p = p * cos + _rot_half(p) * sin
                section, g = divmod(j, N_GROUPS)
                _put_residue(slab, p, qkv_refs[g], DILATIONS[g], 3 * GROUP_W, section * GROUP_W)
            else:
                rest_ref[:, (j - n_qkv) * tn:(j - n_qkv + 1) * tn] = p.astype(BF16)

    return _call(
        body, name="in_proj", grid=(seq // tm,),
        in_specs=[pl.BlockSpec((tm, D_MODEL), lambda i: (i, 0)),
                  _resident((IN_W, D_MODEL)),
                  pl.BlockSpec((tm, GROUP_W), lambda i: (i, 0)),
                  pl.BlockSpec((tm, GROUP_W), lambda i: (i, 0))],
        out_specs=[pl.BlockSpec((tm // d, d * 3 * GROUP_W), lambda i: (i, 0)) for d in DILATIONS]
        + [pl.BlockSpec((tm, REST_W), lambda i: (i, 0))],
        out_shape=[jax.ShapeDtypeStruct((seq // d, d * 3 * GROUP_W), BF16) for d in DILATIONS]
        + [jax.ShapeDtypeStruct((seq, REST_W), BF16)],
        scratch_shapes=[pltpu.VMEM((GROUP_W // LANES, tm, LANES), F32)],
        params=_params(("arbitrary",), 48), args=(h, w_in, cos_t, sin_t), rider=rider)


def _band_masks():
    qi = lax.broadcasted_iota(jnp.int32, (QBLK, QBLK), 0)
    kj = lax.broadcasted_iota(jnp.int32, (QBLK, QBLK), 1)
    return kj <= qi, kj >= qi


def _attn_tile(length):
    return min(512, length)


def _attn_fwd(qkv, dil, rider=None):
    length = qkv.shape[0]
    tq = _attn_tile(length)
    nsub = tq // QBLK
    nblk = length // tq

    def body(q_ref, k_ref, v_ref, kp_ref, vp_ref, o_ref, l_ref):
        n = pl.program_id(1)
        mask_c, mask_p0 = _band_masks()
        hmask = _head_masks((QBLK, GROUP_W))
        zero = jnp.zeros((), BF16)
        for b in range(nsub):
            rows = slice(b * QBLK, (b + 1) * QBLK)
            q = q_ref[rows, :]
            kc, vc = k_ref[rows, :], v_ref[rows, :]
            if b == 0:
                kp, vp = kp_ref[...], vp_ref[...]
                mask_p = mask_p0 & (n > 0)
            else:
                prow = slice((b - 1) * QBLK, b * QBLK)
                kp, vp = k_ref[prow, :], v_ref[prow, :]
                mask_p = mask_p0
            o_acc = jnp.zeros((QBLK, GROUP_W), F32)
            l_acc = jnp.zeros((QBLK, GROUP_W), F32)
            for h in range(HEADS_PER_GROUP):
                hm = hmask[h]
                sc = jnp.where(mask_c, _dot_nt(q, jnp.where(hm, kc, zero)) * SCALE, NEG)
                sp = jnp.where(mask_p, _dot_nt(q, jnp.where(hm, kp, zero)) * SCALE, NEG)
                m = jnp.maximum(jnp.max(sc, axis=-1, keepdims=True), jnp.max(sp, axis=-1, keepdims=True))
                pc, pp = jnp.exp(sc - m), jnp.exp(sp - m)
                den = jnp.sum(pc, axis=-1, keepdims=True) + jnp.sum(pp, axis=-1, keepdims=True)
                pv = _dot(pc.astype(BF16), jnp.where(hm, vc, zero)) + _dot(pp.astype(BF16), jnp.where(hm, vp, zero))
                o_acc = o_acc + pv / den
                l_acc = l_acc + jnp.where(hm, m + jnp.log(den), 0.0)
            o_ref[rows, :] = o_acc
            l_ref[rows, :] = l_acc

    cur = lambda sec: pl.BlockSpec((tq, GROUP_W), lambda r, n: (n, r * 3 + sec))
    prev = lambda sec: pl.BlockSpec((QBLK, GROUP_W), lambda r, n: (jnp.maximum(n * nsub - 1, 0), r * 3 + sec))
    return _call(
        body, name=f"attn_fwd_d{dil}", grid=(dil, nblk),
        in_specs=[cur(0), cur(1), cur(2), prev(1), prev(2)],
        out_specs=[pl.BlockSpec((tq, GROUP_W), lambda r, n: (n, r))] * 2,
        out_shape=[jax.ShapeDtypeStruct((length, dil * GROUP_W), F32)] * 2, scratch_shapes=[],
        params=_params(("arbitrary", "arbitrary"), 32), args=(qkv, qkv, qkv, qkv, qkv), rider=rider)


def _attn_bwd(qkv, dy, y, lse, cos_t, sin_t, dil, rider=None):
    length = qkv.shape[0]
    tq = _attn_tile(length)
    nsub = tq // QBLK
    nblk = length // tq

    def body(q_ref, k_ref, v_ref, kp_ref, vp_ref, qn_ref, dy_ref, y_ref, l_ref, dyn_ref, yn_ref, ln_ref,
             cos_ref, sin_ref, out_ref, dq_s, dk_s, dv_s):
        n = pl.program_id(1)
        mask_c, mask_p0 = _band_masks()
        hmask = _head_masks((QBLK, GROUP_W))
        sub = lambda ref, b: ref[b * QBLK:(b + 1) * QBLK, :]
        kbd = [_head_stack(kp_ref[...], hmask)] + [_head_stack(sub(k_ref, b), hmask) for b in range(nsub)]
        vbd = [_head_stack(vp_ref[...], hmask)] + [_head_stack(sub(v_ref, b), hmask) for b in range(nsub)]
        dk_s[...] = jnp.zeros(dk_s.shape, F32)
        dv_s[...] = jnp.zeros(dv_s.shape, F32)

        def query_block(q, dyv, yv, lv, key_blocks):
            dyb = dyv.astype(BF16)
            qbd = _head_stack(q, hmask)
            dybd = jnp.concatenate([jnp.where(hm, dyv, 0.0).astype(BF16) for hm in hmask], axis=0)
            prod = dyv * yv
            deltas = [jnp.sum(jnp.where(hm, prod, 0.0), axis=-1, keepdims=True) for hm in hmask]
            lses = [jnp.max(jnp.where(hm, lv, NEG), axis=-1, keepdims=True) for hm in hmask]
            dq = jnp.zeros((QBLK, GROUP_W), F32)
            for kb, mask in key_blocks:
                s = _dot_nt(q, kbd[kb]) * SCALE
                dp = _dot_nt(dyb, vbd[kb])
                ps, dss = [], []
                for h in range(HEADS_PER_GROUP):
                    cols = slice(h * QBLK, (h + 1) * QBLK)
                    p = jnp.exp(jnp.where(mask, s[:, cols] - lses[h], NEG))
                    ps.append(p.astype(BF16))
                    dss.append((p * (dp[:, cols] - deltas[h])).astype(BF16))
                dq = dq + _dot(jnp.concatenate(dss, axis=1), kbd[kb])
                if kb >= 1:
                    krows = slice((kb - 1) * QBLK, kb * QBLK)
                    dv_s[krows, :] += _dot_tn(jnp.concatenate(ps, axis=0), dybd)
                    dk_s[krows, :] += _dot_tn(jnp.concatenate(dss, axis=0), qbd) * SCALE
            return dq * SCALE

        for b in range(nsub):
            mask_p = mask_p0 & (n > 0) if b == 0 else mask_p0
            dq_s[b * QBLK:(b + 1) * QBLK, :] = query_block(sub(q_ref, b), sub(dy_ref, b), sub(y_ref, b), sub(l_ref, b),
                                                            [(b, mask_p), (b + 1, mask_c)])
        query_block(qn_ref[...], dyn_ref[...], yn_ref[...], ln_ref[...], [(nsub, mask_p0 & (n < nblk - 1))])
        cos, sin = cos_ref[...], sin_ref[...]
        dq, dk = dq_s[...], dk_s[...]
        out_ref[:, 0:GROUP_W] = (dq * cos - _rot_half(dq) * sin).astype(BF16)
        out_ref[:, GROUP_W:2 * GROUP_W] = (dk * cos - _rot_half(dk) * sin).astype(BF16)
        out_ref[:, 2 * GROUP_W:3 * GROUP_W] = dv_s[...].astype(BF16)

    cur = lambda sec: pl.BlockSpec((tq, GROUP_W), lambda r, n: (n, r * 3 + sec))
    prev = lambda sec: pl.BlockSpec((QBLK, GROUP_W), lambda r, n: (jnp.maximum(n * nsub - 1, 0), r * 3 + sec))
    nxt_q = pl.BlockSpec((QBLK, GROUP_W), lambda r, n: (jnp.minimum((n + 1) * nsub, nblk * nsub - 1), r * 3))
    tok = pl.BlockSpec((tq, GROUP_W), lambda r, n: (n, r))
    tok_next = pl.BlockSpec((QBLK, GROUP_W), lambda r, n: (jnp.minimum((n + 1) * nsub, nblk * nsub - 1), r))
    (out,), riding = _call(
        body, name=f"attn_bwd_d{dil}", grid=(dil, nblk),
        in_specs=[cur(0), cur(1), cur(2), prev(1), prev(2), nxt_q,
                  tok, tok, tok, tok_next, tok_next, tok_next, tok, tok],
        out_specs=[pl.BlockSpec((tq, 3 * GROUP_W), lambda r, n: (n, r))],
        out_shape=[jax.ShapeDtypeStruct((length, dil * 3 * GROUP_W), BF16)],
        scratch_shapes=[pltpu.VMEM((tq, GROUP_W), F32)] * 3,
        params=_params(("arbitrary", "arbitrary"), 32),
        args=(qkv, qkv, qkv, qkv, qkv, qkv, dy, y, lse, dy, y, lse, cos_t, sin_t), rider=rider)
    return out, riding


def _layernorm_stats(z):
    mu = jnp.mean(z, axis=-1, keepdims=True)
    zc = z - mu
    rstd = lax.rsqrt(jnp.mean(zc * zc, axis=-1, keepdims=True) + EPS)
    return zc * rstd, rstd


def _tril_mask():
    row = lax.broadcasted_iota(jnp.int32, (CHUNK, CHUNK), 0)
    col = lax.broadcasted_iota(jnp.int32, (CHUNK, CHUNK), 1)
    return col <= row


def _mix_fwd(o_l, rest, x, w_sp, b_col, ln_g, ln_b, w_ba, w_bg, w_out, g1, rider=None):
    seq = x.shape[0]
    tm = 256

    def body(o0, l0, o1, l1, o2, l2, up_ref, zp_ref, gap_ref, gbp_ref, x_ref, wsp_ref, bcol_ref, lg_ref, lb_ref,
             wba_ref, wbg_ref, wout_ref, g1_ref, ya0, lj0, ya1, lj1, ya2, lj2, yg_ref, mg_ref, y_ref, x1_ref, slab):
        outs = [_get_tokens(slab, o, d, GROUP_W, 0, GROUP_W) for o, d in zip((o0, o1, o2), DILATIONS)]
        lses = [_get_tokens(slab, l, d, GROUP_W, 0, GROUP_W) for l, d in zip((l0, l1, l2), DILATIONS)]
        m = jnp.maximum(jnp.maximum(lses[0], lses[1]), lses[2])
        es = [jnp.exp(l - m) for l in lses]
        tot = es[0] + es[1] + es[2]
        ya = (es[0] * outs[0] + es[1] * outs[1] + es[2] * outs[2]) / tot
        lj = m + jnp.log(tot)
        for ya_ref, lj_ref, d in zip((ya0, ya1, ya2), (lj0, lj1, lj2), DILATIONS):
            _put_residue(slab, ya, ya_ref, d, GROUP_W, 0)
            _put_residue(slab, lj, lj_ref, d, GROUP_W, 0)
        zhat, _ = _layernorm_stats(_gelu(zp_ref[...].astype(F32)))
        zln = (zhat * lg_ref[...] + lb_ref[...]).astype(BF16)
        u = _gelu(up_ref[...].astype(F32))
        tril = _tril_mask()
        for g in range(GMLP_GROUPS):
            wm = jnp.where(tril, wsp_ref[g], 0.0).astype(BF16)
            cols = slice(g * CHUNK, (g + 1) * CHUNK)
            for c in range(tm // CHUNK):
                rows = slice(c * CHUNK, (c + 1) * CHUNK)
                sz = _dot(wm, zln[rows, cols]) + bcol_ref[g]
                yg_ref[rows, cols] = (u[rows, cols] * sz).astype(BF16)
        a = _dot(ya.astype(BF16), wba_ref[...])
        bm = _dot(yg_ref[...], wbg_ref[...])
        merged = (jax.nn.sigmoid(gap_ref[...].astype(F32)) * a + jax.nn.sigmoid(gbp_ref[...].astype(F32)) * bm).astype(BF16)
        mg_ref[...] = merged
        yv = _dot(merged, wout_ref[...])
        y_ref[...] = yv
        x1_ref[...] = x_ref[...] + (yv * _rsqrt_ms(yv)) * g1_ref[...]

    tok = lambda w: pl.BlockSpec((tm, w), lambda i: (i, 0))
    res = lambda d: pl.BlockSpec((tm // d, d * GROUP_W), lambda i: (i, 0))
    full = lambda *s: pl.BlockSpec(s, lambda i: (0,) * len(s))
    res_specs = [res(d) for d in DILATIONS for _ in range(2)]
    return _call(
        body, name="mix_fwd", grid=(seq // tm,),
        in_specs=res_specs + [
            pl.BlockSpec((tm, GMLP_W), lambda i: (i, 0)), pl.BlockSpec((tm, GMLP_W), lambda i: (i, 1)),
            pl.BlockSpec((tm, D_MODEL), lambda i: (i, 1)), pl.BlockSpec((tm, D_MODEL), lambda i: (i, 2)),
            tok(D_MODEL), full(GMLP_GROUPS, CHUNK, CHUNK), full(GMLP_GROUPS, CHUNK, 1), full(1, GMLP_W), full(1, GMLP_W),
            full(GROUP_W, D_MODEL), full(GMLP_W, D_MODEL), full(D_MODEL, D_MODEL), full(1, D_MODEL)],
        out_specs=res_specs + [tok(GMLP_W), tok(D_MODEL), tok(D_MODEL), tok(D_MODEL)],
        out_shape=[jax.ShapeDtypeStruct((seq // d, d * GROUP_W), F32) for d in DILATIONS for _ in range(2)]
        + [jax.ShapeDtypeStruct((seq, GMLP_W), BF16), jax.ShapeDtypeStruct((seq, D_MODEL), BF16),
           jax.ShapeDtypeStruct((seq, D_MODEL), F32), jax.ShapeDtypeStruct((seq, D_MODEL), F32)],
        scratch_shapes=[pltpu.VMEM((GROUP_W // LANES, tm, LANES), F32)],
        params=_params(("arbitrary",), 48),
        args=(*o_l, rest, rest, rest, rest, x, w_sp, b_col, ln_g, ln_b, w_ba, w_bg, w_out, g1), rider=rider)


def _mlp_fwd(x1, g2, g3, w_mi, w_mo, target):
    seq = x1.shape[0]
    tm, tf = MLP_FWD_TM, 512

    def body(x1_ref, g2_ref, g3_ref, wmi_ref, wmo_ref, t_ref, h2_ref, a_ref, dy2_ref, dout_ref, loss_ref, dg3_ref, sq_s):
        @pl.when(pl.program_id(0) == 0)
        def _():
            loss_ref[...] = jnp.zeros(loss_ref.shape, F32)
            dg3_ref[...] = jnp.zeros(dg3_ref.shape, F32)

        xv = x1_ref[...]
        hb = ((xv * _rsqrt_ms(xv)) * g2_ref[...]).astype(BF16)
        h2_ref[...] = hb
        for j in range(D_FF // tf):
            cols = slice(j * tf, (j + 1) * tf)
            a = jnp.maximum(_dot(hb, wmi_ref[:, cols]), 0.0)
            a_ref[:, cols] = a.astype(BF16)
            sq_s[:, cols] = (a * a).astype(BF16)
        y2 = _dot(sq_s[...], wmo_ref[...])
        r3 = _rsqrt_ms(y2)
        out = xv + (y2 * r3) * g3_ref[...]
        diff = out - t_ref[...]
        tile_loss = 0.5 * jnp.sum(jnp.mean(diff * diff, axis=-1, keepdims=True), axis=0, keepdims=True)
        loss_ref[...] += jnp.broadcast_to(tile_loss, loss_ref.shape)
        dout = diff * (1.0 / D_MODEL)
        dout_ref[...] = dout
        dy2, dg3 = _rmsnorm_bwd(dout, y2, g3_ref[...])
        dy2_ref[...] = dy2.astype(BF16)
        dg3_ref[...] += dg3

    tok = lambda w: pl.BlockSpec((tm, w), lambda i: (i, 0))
    vec = pl.BlockSpec((1, D_MODEL), lambda i: (0, 0))
    return _pallas(
        body, name="mlp_fwd", grid=(seq // tm,),
        in_specs=[tok(D_MODEL), vec, vec, _resident((D_MODEL, D_FF)), _resident((D_FF, D_MODEL)), tok(D_MODEL)],
        out_specs=[tok(D_MODEL), tok(D_FF), tok(D_MODEL), tok(D_MODEL), pl.BlockSpec((8, 128), lambda i: (0, 0)), vec],
        out_shape=[jax.ShapeDtypeStruct((seq, D_MODEL), BF16), jax.ShapeDtypeStruct((seq, D_FF), BF16),
                   jax.ShapeDtypeStruct((seq, D_MODEL), BF16), jax.ShapeDtypeStruct((seq, D_MODEL), F32),
                   jax.ShapeDtypeStruct((8, 128), F32), jax.ShapeDtypeStruct((1, D_MODEL), F32)],
        scratch_shapes=[pltpu.VMEM((tm, D_FF), BF16)],
        compiler_params=_params(("arbitrary",), 56),
    )(*map(_in_hbm, (x1, g2, g3, w_mi, w_mo, target)))


def _mlp_bwd(dy2, a, w_mo, w_mi, dout, x1, y, g2, g1, rider=None):
    seq = x1.shape[0]
    tm, tf = MLP_TM, 512

    def body(dy2_ref, a_ref, wmo_ref, wmi_ref, dout_ref, x1_ref, y_ref, g2_ref, g1_ref,
             dap_ref, dx1_ref, dy_ref, dg2_ref, dg1_ref):
        @pl.when(pl.program_id(0) == 0)
        def _():
            dg2_ref[...] = jnp.zeros(dg2_ref.shape, F32)
            dg1_ref[...] = jnp.zeros(dg1_ref.shape, F32)

        dy2v = dy2_ref[...]
        for j in range(D_FF // tf):
            cols = slice(j * tf, (j + 1) * tf)
            da2 = _dot_nt(dy2v, wmo_ref[cols, :])
            dap_ref[:, cols] = (da2 * (2.0 * a_ref[:, cols].astype(F32))).astype(BF16)
        dh2 = _dot_nt(dap_ref[...], wmi_ref[...])
        dres, dg2 = _rmsnorm_bwd(dh2, x1_ref[...], g2_ref[...])
        dx1 = dout_ref[...] + dres
        dx1_ref[...] = dx1
        dg2_ref[...] += dg2
        dyv, dg1 = _rmsnorm_bwd(dx1, y_ref[...], g1_ref[...])
        dy_ref[...] = dyv.astype(BF16)
        dg1_ref[...] += dg1

    tok = lambda w: pl.BlockSpec((tm, w), lambda i: (i, 0))
    vec = pl.BlockSpec((1, D_MODEL), lambda i: (0, 0))
    return _call(
        body, name="mlp_bwd", grid=(seq // tm,),
        in_specs=[tok(D_MODEL), tok(D_FF), _resident((D_FF, D_MODEL)), _resident((D_MODEL, D_FF)),
                  tok(D_MODEL), tok(D_MODEL), tok(D_MODEL), vec, vec],
        out_specs=[tok(D_FF), tok(D_MODEL), tok(D_MODEL), vec, vec],
        out_shape=[jax.ShapeDtypeStruct((seq, D_FF), BF16), jax.ShapeDtypeStruct((seq, D_MODEL), F32),
                   jax.ShapeDtypeStruct((seq, D_MODEL), BF16), jax.ShapeDtypeStruct((1, D_MODEL), F32),
                   jax.ShapeDtypeStruct((1, D_MODEL), F32)], scratch_shapes=[],
        params=_params(("arbitrary",), 56), args=(dy2, a, w_mo, w_mi, dout, x1, y, g2, g1), rider=rider)


def _tn_matmul(a, b, name, bm, bn, square_a=False, column_shards=False, rider=None):
    seq, m = a.shape
    n = b.shape[1]
    ts = 2048

    def body(a_ref, b_ref, o_ref):
        @pl.when(pl.program_id(2) == 0)
        def _():
            o_ref[...] = jnp.zeros(o_ref.shape, F32)

        av = a_ref[...]
        if square_a:
            af = av.astype(F32)
            av = (af * af).astype(BF16)
        o_ref[...] += _dot_tn(av, b_ref[...])

    if column_shards:
        out_spec = pl.BlockSpec((None, bm, bn), lambda mi, ni, s: (ni, mi, 0))
        out_shape = jax.ShapeDtypeStruct((n // bn, m, bn), F32)
    else:
        out_spec = pl.BlockSpec((bm, bn), lambda mi, ni, s: (mi, ni))
        out_shape = jax.ShapeDtypeStruct((m, n), F32)
    (out,), riding = _call(
        body, name=name, grid=(m // bm, n // bn, seq // ts),
        in_specs=[pl.BlockSpec((ts, bm), lambda mi, ni, s: (s, mi)), pl.BlockSpec((ts, bn), lambda mi, ni, s: (s, ni))],
        out_specs=[out_spec], out_shape=[out_shape], scratch_shapes=[],
        params=_params(("arbitrary", "arbitrary", "arbitrary"), 40), args=(a, b), rider=rider)
    return out, riding


def _tn_matmul_residue(a, b, dil, name):
    length = a.shape[0]
    m, n = a.shape[1] // dil, b.shape[1] // dil
    ts = min(1024, length)

    def body(a_ref, b_ref, o_ref):
        @pl.when((pl.program_id(0) == 0) & (pl.program_id(1) == 0))
        def _():
            o_ref[...] = jnp.zeros(o_ref.shape, F32)

        o_ref[...] += _dot_tn(a_ref[...], b_ref[...])

    return _pallas(
        body, name=name, grid=(dil, length // ts),
        in_specs=[pl.BlockSpec((ts, m), lambda r, s: (s, r)), pl.BlockSpec((ts, n), lambda r, s: (s, r))],
        out_specs=pl.BlockSpec((m, n), lambda r, s: (0, 0)),
        out_shape=jax.ShapeDtypeStruct((m, n), F32),
        compiler_params=_params(("arbitrary", "arbitrary"), 40),
    )(_in_hbm(a), _in_hbm(b))


def _mix_bwd(dy, ya, yg, mg, rest, w_out, w_ba, w_bg, w_sp, b_col, ln_g, ln_b, rider=None):
    seq = dy.shape[0]
    tm = 256

    def body(dy_ref, ya_ref, yg_ref, mg_ref, up_ref, zp_ref, gap_ref, gbp_ref, wout_ref, wba_ref, wbg_ref,
             wsp_ref, bcol_ref, lg_ref, lb_ref,
             dya0, dya1, dya2, dpr_ref, dwout_ref, dwba_ref, dwbg_ref, dwsp_ref, dbb_ref, dlg_ref, dlb_ref,
             dzln_s, du_s, slab):
        @pl.when(pl.program_id(0) == 0)
        def _():
            for ref in (dwout_ref, dwba_ref, dwbg_ref, dwsp_ref, dbb_ref, dlg_ref, dlb_ref):
                ref[...] = jnp.zeros(ref.shape, F32)

        dyv = dy_ref[...]
        dm = _dot_nt(dyv, wout_ref[...])
        dwout_ref[...] += _dot_tn(mg_ref[...], dyv)
        yab = ya_ref[...].astype(BF16)
        ygb = yg_ref[...]
        a = _dot(yab, wba_ref[...])
        bm = _dot(ygb, wbg_ref[...])
        ga = jax.nn.sigmoid(gap_ref[...].astype(F32))
        gb = jax.nn.sigmoid(gbp_ref[...].astype(F32))
        dpr_ref[:, 2 * GMLP_W:2 * GMLP_W + D_MODEL] = (dm * a * (ga * (1.0 - ga))).astype(BF16)
        dpr_ref[:, 2 * GMLP_W + D_MODEL:REST_W] = (dm * bm * (gb * (1.0 - gb))).astype(BF16)
        da = (dm * ga).astype(BF16)
        db = (dm * gb).astype(BF16)
        dwba = _dot_tn(yab, da)
        dwbg = _dot_tn(ygb, db)
        shard_w = D_MODEL // N_CHIPS
        for j in range(N_CHIPS):
            dwba_ref[j] += dwba[:, j * shard_w:(j + 1) * shard_w]
            dwbg_ref[j] += dwbg[:, j * shard_w:(j + 1) * shard_w]
        dya = _dot_nt(da, wba_ref[...])
        for dya_ref, d in zip((dya0, dya1, dya2), DILATIONS):
            _put_residue(slab, dya, dya_ref, d, GROUP_W, 0)
        dyg = _dot_nt(db, wbg_ref[...])

        zp = zp_ref[...].astype(F32)
        zhat, rstd = _layernorm_stats(_gelu(zp))
        lg = lg_ref[...]
        zln = (zhat * lg + lb_ref[...]).astype(BF16)
        up = up_ref[...].astype(F32)
        u = _gelu(up)
        tril = _tril_mask()
        for g in range(GMLP_GROUPS):
            wm = jnp.where(tril, wsp_ref[g], 0.0).astype(BF16)
            cols = slice(g * CHUNK, (g + 1) * CHUNK)
            for c in range(tm // CHUNK):
                rows = slice(c * CHUNK, (c + 1) * CHUNK)
                zb = zln[rows, cols]
                sz = _dot(wm, zb) + bcol_ref[g]
                dyg_cg = dyg[rows, cols]
                du_s[rows, cols] = dyg_cg * sz
                dsz = dyg_cg * u[rows, cols]
                dszb = dsz.astype(BF16)
                dbb_ref[g] += jnp.broadcast_to(jnp.sum(dsz, axis=-1, keepdims=True), (CHUNK, CHUNK))
                dwsp_ref[g] += jnp.where(tril, _dot_nt(dszb, zb), 0.0)
                dzln_s[rows, cols] = _dot_tn(wm, dszb)
        dzln = dzln_s[...]
        dlg_ref[...] += jnp.sum(dzln * zhat, axis=0, keepdims=True)
        dlb_ref[...] += jnp.sum(dzln, axis=0, keepdims=True)
        dzh = dzln * lg
        dz = rstd * (dzh - jnp.mean(dzh, axis=-1, keepdims=True) - zhat * jnp.mean(dzh * zhat, axis=-1, keepdims=True))
        dpr_ref[:, GMLP_W:2 * GMLP_W] = (dz * _gelu_grad(zp)).astype(BF16)
        dpr_ref[:, 0:GMLP_W] = (du_s[...] * _gelu_grad(up)).astype(BF16)

    tok = lambda w: pl.BlockSpec((tm, w), lambda i: (i, 0))
    full = lambda *s: pl.BlockSpec(s, lambda i: (0,) * len(s))
    return _call(
        body, name="mix_bwd", grid=(seq // tm,),
        in_specs=[tok(D_MODEL), tok(GROUP_W), tok(GMLP_W), tok(D_MODEL),
                  pl.BlockSpec((tm, GMLP_W), lambda i: (i, 0)), pl.BlockSpec((tm, GMLP_W), lambda i: (i, 1)),
                  pl.BlockSpec((tm, D_MODEL), lambda i: (i, 1)), pl.BlockSpec((tm, D_MODEL), lambda i: (i, 2)),
                  full(D_MODEL, D_MODEL), full(GROUP_W, D_MODEL), full(GMLP_W, D_MODEL),
                  full(GMLP_GROUPS, CHUNK, CHUNK), full(GMLP_GROUPS, CHUNK, 1), full(1, GMLP_W), full(1, GMLP_W)],
        out_specs=[pl.BlockSpec((tm // d, d * GROUP_W), lambda i: (i, 0)) for d in DILATIONS]
        + [tok(REST_W), full(D_MODEL, D_MODEL), full(N_CHIPS, GROUP_W, D_MODEL // N_CHIPS),
           full(N_CHIPS, GMLP_W, D_MODEL // N_CHIPS),
           full(GMLP_GROUPS, CHUNK, CHUNK), full(GMLP_GROUPS, CHUNK, CHUNK), full(1, GMLP_W), full(1, GMLP_W)],
        out_shape=[jax.ShapeDtypeStruct((seq // d, d * GROUP_W), F32) for d in DILATIONS]
        + [jax.ShapeDtypeStruct((seq, REST_W), BF16),
           jax.ShapeDtypeStruct((D_MODEL, D_MODEL), F32), jax.ShapeDtypeStruct((N_CHIPS, GROUP_W, D_MODEL // N_CHIPS), F32),
           jax.ShapeDtypeStruct((N_CHIPS, GMLP_W, D_MODEL // N_CHIPS), F32),
           jax.ShapeDtypeStruct((GMLP_GROUPS, CHUNK, CHUNK), F32),
           jax.ShapeDtypeStruct((GMLP_GROUPS, CHUNK, CHUNK), F32), jax.ShapeDtypeStruct((1, GMLP_W), F32),
           jax.ShapeDtypeStruct((1, GMLP_W), F32)],
        scratch_shapes=[pltpu.VMEM((tm, GMLP_W), F32), pltpu.VMEM((tm, GMLP_W), F32),
                        pltpu.VMEM((GROUP_W // LANES, tm, LANES), F32)],
        params=_params(("arbitrary",), 56),
        args=(dy, ya, yg, mg, rest, rest, rest, rest, w_out, w_ba, w_bg, w_sp, b_col, ln_g, ln_b), rider=rider)


IN_PROJ_BWD_TM = 256


def _in_proj_bwd(dqkv, drest, w_qkv, w_rest, x, dx1, g0, so_far, span, rider=None):
    seq = x.shape[0]
    tm = IN_PROJ_BWD_TM
    off, steps = span
    gx_so_far, dg_so_far = so_far

    def body(d0, d1, d2, dr_ref, w0, w1, w2, wr_ref, x_ref, dx1_ref, g_ref, dg_in_ref, gx_in_ref, gx_ref, dg_ref, slab):
        @pl.when(pl.program_id(0) == 0)
        def _():
            dg_ref[...] = dg_in_ref[...]

        dh = _dot(dr_ref[...], wr_ref[...])
        for d_ref, w_ref, dil in zip((d0, d1, d2), (w0, w1, w2), DILATIONS):
            piece = d_ref[...] if dil == 1 else _get_tokens(slab, d_ref, dil, 3 * GROUP_W, 0, 3 * GROUP_W).astype(BF16)
            dh = dh + _dot(piece, w_ref[...])
        dres, dg = _rmsnorm_bwd(dh, x_ref[...], g_ref[...])
        gx_ref[...] = dx1_ref[...] + dres
        dg_ref[...] += dg

    tok = lambda w: pl.BlockSpec((tm, w), lambda i: (i + off, 0))
    full = lambda *s: pl.BlockSpec(s, lambda i: (0,) * len(s))
    in_specs = ([pl.BlockSpec((tm // d, d * 3 * GROUP_W), lambda i: (i + off, 0)) for d in DILATIONS] + [tok(REST_W)]
                + [_resident((3 * GROUP_W, D_MODEL))] * 3 + [_resident((REST_W, D_MODEL))]
                + [tok(D_MODEL), tok(D_MODEL), full(1, D_MODEL), full(1, D_MODEL), HBM_SPEC])
    return _call(
        body, name=f"in_proj_bwd_{off}", grid=(steps,), in_specs=in_specs,
        out_specs=[tok(D_MODEL), full(1, D_MODEL)],
        out_shape=[jax.ShapeDtypeStruct((seq, D_MODEL), F32), jax.ShapeDtypeStruct((1, D_MODEL), F32)],
        scratch_shapes=[pltpu.VMEM((3 * GROUP_W // LANES, tm, LANES), F32)],
        params=_params(("arbitrary",), 48), args=(*dqkv, drest, *w_qkv, w_rest, x, dx1, g0, dg_so_far, gx_so_far),
        rider=rider, aliases={len(in_specs) - 1: 0})


def _adamw(w, g, m, v, name):
    rows, cols = w.shape
    tr = _row_tile(rows) if rows % 16 == 0 else rows
    c1 = 1.0 - ADAM_B1 ** ADAM_STEP
    c2 = 1.0 - ADAM_B2 ** ADAM_STEP

    def body(w_ref, g_ref, m_ref, v_ref, go_ref, d_ref, nm_ref, nv_ref):
        gv = g_ref[...]
        go_ref[...] = gv
        nm = ADAM_B1 * m_ref[...] + (1.0 - ADAM_B1) * gv
        nv = ADAM_B2 * v_ref[...] + (1.0 - ADAM_B2) * (gv * gv)
        d_ref[...] = -ADAM_LR * ((nm / c1) / (jnp.sqrt(nv / c2) + ADAM_EPS) + ADAM_WD * w_ref[...])
        nm_ref[...] = nm
        nv_ref[...] = nv

    spec = pl.BlockSpec((tr, cols), lambda i: (i, 0))
    return _pallas(
        body, name=name, grid=(rows // tr,),
        in_specs=[spec] * 4, out_specs=[spec] * 4,
        out_shape=[jax.ShapeDtypeStruct((rows, cols), F32)] * 4,
        compiler_params=_params(("arbitrary",), 32, small=True),
    )(w, g, m, v)


def _place():
    x, y, c = lax.axis_index("x"), lax.axis_index("y"), lax.axis_index("c")
    chips = [(1 - x, y), (x, 1 - y), (1 - x, 1 - y)]
    return x, y, c, chips


class _Exchange:
    def __init__(self, inputs, out_shapes, n_sems, start, finish, aliases=None):
        self.inputs, self.out_shapes, self.n_sems = list(inputs), list(out_shapes), n_sems
        self.start, self.finish, self.aliases = start, finish, dict(aliases or {})

    def scratch(self):
        return [pltpu.SemaphoreType.DMA((self.n_sems,)), pltpu.SemaphoreType.DMA((self.n_sems,))]


def _together(*parts):
    ins = [len(p.inputs) for p in parts]
    outs = [len(p.out_shapes) for p in parts]

    def split(refs, counts):
        pos, pieces = 0, []
        for cnt in counts:
            pieces.append(refs[pos:pos + cnt])
            pos += cnt
        return pieces

    def run(which):
        def go(in_refs, out_refs, *sems):
            for k, (p, i, o) in enumerate(zip(parts, split(in_refs, ins), split(out_refs, outs))):
                getattr(p, which)(i, o, sems[2 * k], sems[2 * k + 1])
        return go

    both = _Exchange([a for p in parts for a in p.inputs], [s for p in parts for s in p.out_shapes], 0, run("start"),
                     run("finish"))
    both.aliases = {sum(ins[:k]) + i: sum(outs[:k]) + o for k, p in enumerate(parts) for i, o in p.aliases.items()}
    both.scratch = lambda: [s for p in parts for s in p.scratch()]
    return both


def _run_exchange(ex, name):
    n_in, n_out = len(ex.inputs), len(ex.out_shapes)

    def body(*refs):
        ins, outs, sems = refs[:n_in], refs[n_in:n_in + n_out], refs[n_in + n_out:]
        ex.start(ins, outs, *sems)
        ex.finish(ins, outs, *sems)

    return _pallas(
        body, name=name, in_specs=[HBM_SPEC] * n_in, out_specs=[HBM_SPEC] * n_out, out_shape=ex.out_shapes,
        scratch_shapes=ex.scratch(), input_output_aliases=ex.aliases,
    )(*ex.inputs)


def _call(body, *, name, grid, in_specs, out_specs, out_shape, scratch_shapes, params, args, rider=None, aliases=None):
    in_specs, out_specs, out_shape, scratch_shapes = list(in_specs), list(out_specs), list(out_shape), list(scratch_shapes)
    aliases = dict(aliases or {})
    args = [_in_hbm(a) for a in args]
    if rider is None:
        outs = _pallas(body, name=name, grid=grid, in_specs=in_specs, out_specs=out_specs, out_shape=out_shape,
                              scratch_shapes=scratch_shapes, input_output_aliases=aliases, compiler_params=params)(*args)
        return list(outs), []
    n_in, n_out, n_scr = len(in_specs), len(out_specs), len(scratch_shapes)
    r_in, r_out = len(rider.inputs), len(rider.out_shapes)

    def wrapped(*refs):
        ins, r_ins = refs[:n_in], refs[n_in:n_in + r_in]
        pos = n_in + r_in
        outs, r_outs = refs[pos:pos + n_out], refs[pos + n_out:pos + n_out + r_out]
        pos += n_out + r_out
        scr, sems = refs[pos:pos + n_scr], refs[pos + n_scr:]
        ids = [pl.program_id(k) for k in range(len(grid))]
        first, last = ids[0] == 0, ids[0] == grid[0] - 1
        for k in range(1, len(grid)):
            first, last = first & (ids[k] == 0), last & (ids[k] == grid[k] - 1)

        @pl.when(first)
        def _():
            rider.start(r_ins, r_outs, *sems)

        body(*ins, *outs, *scr)

        @pl.when(last)
        def _():
            rider.finish(r_ins, r_outs, *sems)

    outs = _pallas(
        wrapped, name=name, grid=grid, in_specs=in_specs + [HBM_SPEC] * r_in, out_specs=out_specs + [HBM_SPEC] * r_out,
        out_shape=out_shape + rider.out_shapes, scratch_shapes=scratch_shapes + rider.scratch(),
        input_output_aliases={**aliases, **{n_in + i: n_out + o for i, o in rider.aliases.items()}}, compiler_params=params,
    )(*args, *rider.inputs)
    return list(outs[:n_out]), list(outs[n_out:])


def _stage_weights(shards):
    n = len(shards)

    def body(*refs):
        ins, outs, stages, sems = refs[:n], refs[n:2 * n], refs[2 * n:3 * n], refs[3 * n]
        x, y, _, _ = _place()
        copies = []
        for t in range(n):
            stages[t][...] = ins[t][...].astype(BF16)
            copies.append(pltpu.make_async_copy(stages[t], outs[t].at[2 * x + y], sems.at[t]))
            copies[-1].start()
        for cp in copies:
            cp.wait()

    assert sum(s.size * 6 for s in shards) <= (CALL_VMEM_MIB - 8) * MIB
    return _pallas(
        body, name="stage_weights", in_specs=[VMEM_SPEC] * n, out_specs=[HBM_SPEC] * n,
        out_shape=[jax.ShapeDtypeStruct((N_CHIPS,) + s.shape, BF16) for s in shards],
        scratch_shapes=[pltpu.VMEM(s.shape, BF16) for s in shards] + [pltpu.SemaphoreType.DMA((n,))],
        compiler_params=pltpu.CompilerParams(vmem_limit_bytes=SMALL_VMEM_MIB * MIB),
    )(*shards)


def _gather(buffers, stage="both", part=(0, 1)):
    n = len(buffers)
    halves = [b.shape[1] // part[1] // 2 for b in buffers]

    def half_of(outs, t, chip, which):
        return outs[t].at[chip, pl.ds((2 * part[0] + which) * halves[t], halves[t]), :]

    def copy(outs, sems, t, k, chip, which, to):
        rows = half_of(outs, t, chip, which)
        return pltpu.make_async_remote_copy(src_ref=rows, dst_ref=rows, send_sem=sems[0].at[6 * t + k],
                                            recv_sem=sems[1].at[6 * t + k], device_id=to, device_id_type=MESH)

    def to_chips(outs, sems, what):
        x, y, c, chips = _place()
        for t in range(n):
            for j, (px, py) in enumerate(chips):
                if what == "start":
                    copy(outs, sems, t, j, 2 * x + y, c, (px, py, c)).start()
                else:
                    copy(outs, sems, t, j, 2 * px + py, c, (px, py, c)).wait_recv()
                    copy(outs, sems, t, j, 2 * x + y, c, (px, py, c)).wait_send()

    def to_sibling(outs, sems, what):
        x, y, c, chips = _place()
        for t in range(n):
            for j, (px, py) in enumerate(chips):
                if what == "start":
                    copy(outs, sems, t, 3 + j, 2 * px + py, c, (x, y, 1 - c)).start()
                else:
                    copy(outs, sems, t, 3 + j, 2 * px + py, 1 - c, (x, y, 1 - c)).wait_recv()
                    copy(outs, sems, t, 3 + j, 2 * px + py, c, (x, y, 1 - c)).wait_send()

    def start(ins, outs, *sems):
        (to_sibling if stage == "pair" else to_chips)(outs, sems, "start")

    def finish(ins, outs, *sems):
        if stage != "pair":
            to_chips(outs, sems, "finish")
        if stage == "both":
            to_sibling(outs, sems, "start")
        if stage != "chips":
            to_sibling(outs, sems, "finish")

    return _Exchange(buffers, [jax.ShapeDtypeStruct(b.shape, b.dtype) for b in buffers], 6 * n, start, finish,
                     aliases={t: t for t in range(n)})


def _pair_exchange(grads):
    n = len(grads)
    halves = [g.shape[1] // 2 for g in grads]

    def copies(ins, outs, send_sems, recv_sems):
        x, y, c, _ = _place()
        return [pltpu.make_async_remote_copy(
            src_ref=ins[t].at[:, pl.ds((1 - c) * halves[t], halves[t]), :], dst_ref=outs[t],
            send_sem=send_sems.at[t], recv_sem=recv_sems.at[t], device_id=(x, y, 1 - c), device_id_type=MESH)
            for t in range(n)]

    def start(*refs):
        for cp in copies(*refs):
            cp.start()

    def finish(*refs):
        for cp in copies(*refs):
            cp.wait()

    return _Exchange(grads, [jax.ShapeDtypeStruct((N_CHIPS, h, g.shape[2]), F32) for g, h in zip(grads, halves)], n,
                     start, finish)


def _row_tile(rows):
    return max(t for t in range(16, 257, 16) if rows % t == 0)


def _pair_add(grad, other, place, name):
    _, rows, cols = grad.shape
    rh = rows // 2
    tr = _row_tile(rh)
    nb = rh // tr

    def body(p_ref, g_ref, a_ref, wire_ref, own_ref):
        s = g_ref[...] + a_ref[...]
        wire_ref[...] = s.astype(BF16)

        @pl.when(pl.program_id(1) == p_ref[1])
        def _():
            own_ref[...] = s

    blk = (None, tr, cols)
    return _pallas(
        body, name=name,
        grid_spec=pltpu.PrefetchScalarGridSpec(
            num_scalar_prefetch=1, grid=(nb, N_CHIPS),
            in_specs=[pl.BlockSpec(blk, lambda i, j, p: (j, p[0] * nb + i, 0)), pl.BlockSpec(blk, lambda i, j, p: (j, i, 0))],
            out_specs=[pl.BlockSpec(blk, lambda i, j, p: (j, i, 0)), pl.BlockSpec((tr, cols), lambda i, j, p: (i, 0))]),
        out_shape=[jax.ShapeDtypeStruct((N_CHIPS, rh, cols), BF16), jax.ShapeDtypeStruct((rh, cols), F32)],
        compiler_params=_params(("arbitrary", "arbitrary"), 32, small=True),
    )(place, grad, other)


def _chip_exchange(wires):
    n = len(wires)

    def copies(ins, outs, send_sems, recv_sems):
        x, y, c, chips = _place()
        return [pltpu.make_async_remote_copy(
            src_ref=ins[t].at[2 * px + py], dst_ref=outs[t].at[j], send_sem=send_sems.at[3 * t + j],
            recv_sem=recv_sems.at[3 * t + j], device_id=(px, py, c), device_id_type=MESH)
            for t in range(n) for j, (px, py) in enumerate(chips)]

    def start(*refs):
        for cp in copies(*refs):
            cp.start()

    def finish(*refs):
        for cp in copies(*refs):
            cp.wait()

    return _Exchange(wires, [jax.ShapeDtypeStruct((3,) + w.shape[1:], BF16) for w in wires], 3 * n, start, finish)


def _chip_add(own, arrived, place, name):
    rh, cols = own.shape
    tr = _row_tile(rh)
    nb = rh // tr

    def body(p_ref, s_ref, b0, b1, b2, o_ref):
        o_ref[...] = ((s_ref[...] + b0[...].astype(F32)) + b1[...].astype(F32)) + b2[...].astype(F32)

    blk = (None, tr, cols)
    return _pallas(
        body, name=name,
        grid_spec=pltpu.PrefetchScalarGridSpec(
            num_scalar_prefetch=1, grid=(nb,),
            in_specs=[pl.BlockSpec((tr, cols), lambda i, p: (i, 0)), pl.BlockSpec(blk, lambda i, p: (0, i, 0)),
                      pl.BlockSpec(blk, lambda i, p: (1, i, 0)), pl.BlockSpec(blk, lambda i, p: (2, i, 0))],
            out_specs=pl.BlockSpec((tr, cols), lambda i, p: (p[0] * nb + i, 0))),
        out_shape=jax.ShapeDtypeStruct((2 * rh, cols), F32),
        compiler_params=_params(("arbitrary",), 32, small=True),
    )(place, own, arrived, arrived, arrived)


def _pair_share(halves):
    n = len(halves)
    rhs = [h.shape[0] // 2 for h in halves]

    def copy(outs, send_sems, recv_sems, t, which):
        x, y, c, _ = _place()
        rows = outs[t].at[pl.ds(which * rhs[t], rhs[t]), :]
        return pltpu.make_async_remote_copy(src_ref=rows, dst_ref=rows, send_sem=send_sems.at[t], recv_sem=recv_sems.at[t],
                                            device_id=(x, y, 1 - c), device_id_type=MESH)

    def start(ins, outs, send_sems, recv_sems):
        c = lax.axis_index("c")
        for t in range(n):
            copy(outs, send_sems, recv_sems, t, c).start()

    def finish(ins, outs, send_sems, recv_sems):
        c = lax.axis_index("c")
        for t in range(n):
            copy(outs, send_sems, recv_sems, t, c).wait_send()
            copy(outs, send_sems, recv_sems, t, 1 - c).wait_recv()

    return _Exchange(halves, [jax.ShapeDtypeStruct(h.shape, F32) for h in halves], n, start, finish,
                     aliases={t: t for t in range(n)})


class _GradReduction:
    def __init__(self, grads, place, tag):
        self.names, self.grads, self.place, self.tag = list(grads), grads, place, tag

    def pair_exchange(self):
        return _pair_exchange([self.grads[n] for n in self.names])

    def chip_exchange(self, others):
        sums = [_pair_add(self.grads[n], o, self.place, f"{self.tag}_pair_add_{n}") for n, o in zip(self.names, others)]
        self.owns = [own for _, own in sums]
        return _chip_exchange([wire for wire, _ in sums])

    def pair_share(self, arrived):
        return _pair_share([_chip_add(own, arr, self.place, f"{self.tag}_chip_add_{n}")
                            for n, own, arr in zip(self.names, self.owns, arrived)])

    def result(self, shared):
        return dict(zip(self.names, shared))


def _all_reduce_small(p):
    rows, lanes = p.shape
    half = rows // 2

    def body(p_ref, o_ref, sib, sums, send_sems, recv_sems):
        x, y, c, chips = _place()
        mine, sibling = 2 * x + y, (x, y, 1 - c)
        swap = pltpu.make_async_remote_copy(src_ref=p_ref, dst_ref=sib, send_sem=send_sems.at[0], recv_sem=recv_sems.at[0],
                                            device_id=sibling, device_id_type=MESH)
        swap.start()
        swap.wait()
        sums[mine] = p_ref[...] + sib[...]

        def copy(k, chip, which, to):
            part = sums.at[chip, pl.ds(which * half, half), :]
            return pltpu.make_async_remote_copy(src_ref=part, dst_ref=part, send_sem=send_sems.at[k], recv_sem=recv_sems.at[k],
                                                device_id=to, device_id_type=MESH)

        for j, (px, py) in enumerate(chips):
            copy(1 + j, mine, c, (px, py, c)).start()
        for j, (px, py) in enumerate(chips):
            copy(1 + j, 2 * px + py, c, (px, py, c)).wait_recv()
            copy(4 + j, 2 * px + py, c, sibling).start()
        for j, (px, py) in enumerate(chips):
            copy(4 + j, 2 * px + py, 1 - c, sibling).wait_recv()
        for j, (px, py) in enumerate(chips):
            copy(1 + j, mine, c, (px, py, c)).wait_send()
            copy(4 + j, 2 * px + py, c, sibling).wait_send()
        o_ref[...] = ((sums[0] + sums[1]) + sums[2]) + sums[3]

    return _pallas(
        body, name="small_all_reduce", in_specs=[VMEM_SPEC], out_specs=VMEM_SPEC,
        out_shape=jax.ShapeDtypeStruct((rows, lanes), F32),
        scratch_shapes=[pltpu.VMEM((rows, lanes), F32), pltpu.VMEM((N_CHIPS, rows, lanes), F32),
                        pltpu.SemaphoreType.DMA((7,)), pltpu.SemaphoreType.DMA((7,))],
        compiler_params=pltpu.CompilerParams(vmem_limit_bytes=SMALL_VMEM_MIB * MIB),
    )(p)


BIG = ("w_in", "w_branch_attn", "w_branch_gmlp", "w_out", "w_mlp_in", "w_mlp_out")
COLUMN_SHARDED = ("w_branch_attn", "w_branch_gmlp", "w_mlp_in")
SMALL = ("norm_pre_mix", "w_spatial", "b_spatial", "ln_v_gain", "ln_v_bias", "norm_post_mix", "norm_pre_mlp", "norm_post_mlp")
ORDER = ("norm_pre_mix", "w_in", "w_spatial", "b_spatial", "ln_v_gain", "ln_v_bias", "w_branch_attn", "w_branch_gmlp",
         "w_out", "norm_post_mix", "norm_pre_mlp", "w_mlp_in", "w_mlp_out", "norm_post_mlp")


def _full_weight(name, gathered):
    if name in COLUMN_SHARDED:
        return jnp.transpose(gathered, (1, 0, 2)).reshape(gathered.shape[1], -1)
    return gathered.reshape(-1, gathered.shape[2])


def _rows8(a):
    a = a.reshape(-1, 128)
    pad = (-a.shape[0]) % 8
    return jnp.pad(a, ((0, pad), (0, 0))) if pad else a


def _qkv_columns(group):
    return [(sec * ATTN_W + group * GROUP_W, sec * ATTN_W + (group + 1) * GROUP_W) for sec in range(3)]


def _device_step(x, target, small, shards, place):
    seq = x.shape[0]
    g0, g1, g2, g3 = small["norm_pre_mix"], small["norm_post_mix"], small["norm_pre_mlp"], small["norm_post_mlp"]
    w_sp = small["w_spatial"]
    b_col = small["b_spatial"].reshape(GMLP_GROUPS, CHUNK, 1)
    ln_g, ln_b = small["ln_v_gain"], small["ln_v_bias"]

    staged = _stage_weights(shards)
    tables, w_in = _rope_tables(seq, rider=_gather(staged[:1], part=(0, 2)))
    h, (w_in,) = _norm_in(x, g0, rider=_gather(w_in, part=(1, 2)))
    w_in = _full_weight("w_in", w_in)
    (*qkv, rest), landed = _in_proj(h[0], w_in, *tables[1], rider=_gather(staged[1:], "chips"))

    o_l, gathered = _attn_fwd(qkv[0], DILATIONS[0], rider=_gather(landed, "pair"))
    full = {n: _full_weight(n, gw) for n, gw in zip(BIG[1:], gathered)}
    for g in range(1, N_GROUPS):
        o_l.extend(_attn_fwd(qkv[g], DILATIONS[g])[0])
    (*ya_l, yg, mg, y, x1), _ = _mix_fwd(o_l, rest, x, w_sp, b_col, ln_g, ln_b, full["w_branch_attn"],
                                        full["w_branch_gmlp"], full["w_out"], g1)
    ya, lse = ya_l[0::2], ya_l[1::2]
    h2, a, dy2, dout, loss8, dg3 = _mlp_fwd(x1, g2, g3, full["w_mlp_in"], full["w_mlp_out"], target)
    d_wmo, _ = _tn_matmul(a, dy2, "grad_w_mlp_out", 1024, 1024, square_a=True)
    mlp_out = _GradReduction({"w_mlp_out": d_wmo.reshape(N_CHIPS, D_FF // N_CHIPS, D_MODEL)}, place, "mlp_out")
    (dap, dx1, dy, dg2, dg1), riding = _mlp_bwd(dy2, a, full["w_mlp_out"], full["w_mlp_in"], dout, x1, y, g2, g1,
                                                 rider=mlp_out.pair_exchange())
    d_wmi, riding = _tn_matmul(h2, dap, "grad_w_mlp_in", 1024, 1024, column_shards=True,
                               rider=mlp_out.chip_exchange(riding))
    mlp_in = _GradReduction({"w_mlp_in": d_wmi}, place, "mlp_in")
    (*dya, drest, d_wout, d_wba, d_wbg, d_wsp, d_bb, d_lg, d_lb), riding = _mix_bwd(
        dy, ya[0], yg, mg, rest, full["w_out"], full["w_branch_attn"], full["w_branch_gmlp"], w_sp, b_col, ln_g, ln_b,
        rider=_together(mlp_out.pair_share(riding), mlp_in.pair_exchange()))
    reduced = mlp_out.result(riding[:1])
    mix = _GradReduction({"w_branch_attn": d_wba, "w_branch_gmlp": d_wbg,
                          "w_out": d_wout.reshape(N_CHIPS, D_MODEL // N_CHIPS, D_MODEL)}, place, "mix")
    attn = lambda g, rider: _attn_bwd(qkv[g], dya[g], ya[g], lse[g], *tables[DILATIONS[g]], DILATIONS[g], rider=rider)
    dqkv0, riding = attn(0, _together(mlp_in.chip_exchange(riding[1:]), mix.pair_exchange()))
    dqkv1, riding = attn(1, _together(mlp_in.pair_share(riding[:1]), mix.chip_exchange(riding[1:])))
    reduced.update(mlp_in.result(riding[:1]))
    dqkv2, riding = attn(2, mix.pair_share(riding[1:]))
    reduced.update(mix.result(riding))
    dqkv = [dqkv0, dqkv1, dqkv2]

    d_qkv = [_tn_matmul_residue(dqkv[g], h[g], dil, f"grad_w_in_qkv{g}") for g, dil in enumerate(DILATIONS)]
    d_rest, _ = _tn_matmul(drest, h[0], "grad_w_in_rest", 1024, 1024)
    d_win = jnp.concatenate([d_qkv[g][s * GROUP_W:(s + 1) * GROUP_W] for s in range(3) for g in range(N_GROUPS)]
                            + [d_rest], axis=0)
    first = _GradReduction({"w_in": d_win.reshape(N_CHIPS, IN_W // N_CHIPS, D_MODEL)}, place, "w_in")
    w_qkv = [jnp.concatenate([w_in[lo:hi] for lo, hi in _qkv_columns(g)], axis=0) for g in range(N_GROUPS)]
    w_rest = w_in[QKV_W:]
    tiles = seq // IN_PROJ_BWD_TM
    so_far = (lax.empty((seq, D_MODEL), F32), jnp.zeros((1, D_MODEL), F32))
    in_bwd = lambda so_far, span, rider: _in_proj_bwd(dqkv, drest, w_qkv, w_rest, x, dx1, g0, so_far, span, rider=rider)
    so_far, riding = in_bwd(so_far, (0, 3 * tiles // 8), first.pair_exchange())
    (grad_x, dg0), riding = in_bwd(so_far, (3 * tiles // 8, 5 * tiles // 8), first.chip_exchange(riding))
    reduced.update(first.result(_run_exchange(first.pair_share(riding), "w_in_pair_share")))
    little = {"norm_pre_mix": dg0, "w_spatial": d_wsp, "b_spatial": d_bb[:, :, 0], "ln_v_gain": d_lg, "ln_v_bias": d_lb,
              "norm_post_mix": dg1, "norm_pre_mlp": dg2, "norm_post_mlp": dg3}
    return loss8, grad_x, reduced, little


def kernel(x, norm_pre_mix, w_in, w_spatial, b_spatial, ln_v_gain, ln_v_bias, w_branch_attn, w_branch_gmlp, w_out, norm_post_mix, norm_pre_mlp, w_mlp_in, w_mlp_out, norm_post_mlp, loss_target, m_norm_pre_mix, m_w_in, m_w_spatial, m_b_spatial, m_ln_v_gain, m_ln_v_bias, m_w_branch_attn, m_w_branch_gmlp, m_w_out, m_norm_post_mix, m_norm_pre_mlp, m_w_mlp_in, m_w_mlp_out, m_norm_post_mlp, v_norm_pre_mix, v_w_in, v_w_spatial, v_b_spatial, v_ln_v_gain, v_ln_v_bias, v_w_branch_attn, v_w_branch_gmlp, v_w_out, v_norm_post_mix, v_norm_pre_mlp, v_w_mlp_in, v_w_mlp_out, v_norm_post_mlp):
    given = dict(norm_pre_mix=norm_pre_mix, w_in=w_in, w_spatial=w_spatial, b_spatial=b_spatial, ln_v_gain=ln_v_gain,
                 ln_v_bias=ln_v_bias, w_branch_attn=w_branch_attn, w_branch_gmlp=w_branch_gmlp, w_out=w_out,
                 norm_post_mix=norm_post_mix, norm_pre_mlp=norm_pre_mlp, w_mlp_in=w_mlp_in, w_mlp_out=w_mlp_out,
                 norm_post_mlp=norm_post_mlp)
    moments_m = dict(norm_pre_mix=m_norm_pre_mix, w_in=m_w_in, w_spatial=m_w_spatial, b_spatial=m_b_spatial,
                     ln_v_gain=m_ln_v_gain, ln_v_bias=m_ln_v_bias, w_branch_attn=m_w_branch_attn,
                     w_branch_gmlp=m_w_branch_gmlp, w_out=m_w_out, norm_post_mix=m_norm_post_mix,
                     norm_pre_mlp=m_norm_pre_mlp, w_mlp_in=m_w_mlp_in, w_mlp_out=m_w_mlp_out, norm_post_mlp=m_norm_post_mlp)
    moments_v = dict(norm_pre_mix=v_norm_pre_mix, w_in=v_w_in, w_spatial=v_w_spatial, b_spatial=v_b_spatial,
                     ln_v_gain=v_ln_v_gain, ln_v_bias=v_ln_v_bias, w_branch_attn=v_w_branch_attn,
                     w_branch_gmlp=v_w_branch_gmlp, w_out=v_w_out, norm_post_mix=v_norm_post_mix,
                     norm_pre_mlp=v_norm_pre_mlp, w_mlp_in=v_w_mlp_in, w_mlp_out=v_w_mlp_out, norm_post_mlp=v_norm_post_mlp)
    cx, cy, cc = lax.axis_index("x"), lax.axis_index("y"), lax.axis_index("c")

    shards = [given[n][0].T if n == "w_in" else given[n][0] for n in BIG]
    small = {n: given[n][0] if given[n].ndim > 2 else given[n] for n in SMALL}
    place = jnp.stack([cc, 2 * cx + cy]).astype(jnp.int32)
    loss8, grad_x, grad_shard, grads = _device_step(x[0], loss_target[0], small, shards, place)

    packed = jnp.concatenate([_rows8(grads[n]) for n in SMALL] + [loss8], axis=0)
    summed = _all_reduce_small(packed)
    loss = summed[packed.shape[0] - loss8.shape[0], 0]
    row = 0
    for n in SMALL:
        shape = given[n][0].shape
        cnt = -(-(given[n][0].size // 128) // 8) * 8
        grad_shard[n] = summed[row:row + given[n][0].size // 128].reshape(shape)
        row += cnt

    grad_out, deltas, new_m, new_v = {}, {}, {}, {}
    for n in ORDER:
        shape = given[n].shape
        if n == "w_in":
            outs = _adamw(given[n][0].T, grad_shard[n], moments_m[n][0].T, moments_v[n][0].T, "adamw_" + n)
            outs = [o.T for o in outs]
        else:
            two_d = (-1, shape[-1])
            outs = _adamw(given[n].reshape(two_d), grad_shard[n].reshape(two_d), moments_m[n].reshape(two_d),
                          moments_v[n].reshape(two_d), "adamw_" + n)
        grad_out[n], deltas[n], new_m[n], new_v[n] = [o.reshape(shape) for o in outs]
    return (loss, grad_x[None], *[grad_out[n] for n in ORDER], *[deltas[n] for n in ORDER], *[new_m[n] for n in ORDER],
            *[new_v[n] for n in ORDER])
```

```python
import math

import jax
import jax.numpy as jnp
from jax import lax
from jax.experimental import pallas as pl
from jax.experimental.pallas import tpu as pltpu

F32 = jnp.float32
BF16 = jnp.bfloat16
MESH = pl.DeviceIdType.MESH

D_MODEL = 1024
HEAD_DIM = 64
HEADS_PER_GROUP = 4
GROUP_W = HEADS_PER_GROUP * HEAD_DIM
DILATIONS = (1, 4, 16)
N_GROUPS = len(DILATIONS)
ATTN_W = N_GROUPS * GROUP_W
QKV_W = 3 * ATTN_W
GMLP_W = 512
GMLP_GROUPS = 4
CHUNK = 128
REST_W = 2 * GMLP_W + 2 * D_MODEL
IN_W = QKV_W + REST_W
D_FF = 4096
QBLK = 128
ROPE_THETA = 10000.0
EPS = 1e-6
NEG = -1e30
SCALE = HEAD_DIM ** -0.5
N_CHIPS = 4

ADAM_LR = 0.001
ADAM_B1 = 0.9
ADAM_B2 = 0.999
ADAM_EPS = 1e-08
ADAM_WD = 0.01
ADAM_STEP = 10

MIB = 1024 * 1024
HBM_SPEC = pl.BlockSpec(memory_space=pltpu.HBM)
VMEM_SPEC = pl.BlockSpec(memory_space=pltpu.VMEM)


MLP_FWD_TM = 512
MLP_TM = 256


CALL_VMEM_MIB = 56
SMALL_VMEM_MIB = 32


def _params(semantics, vmem_mib, small=False):
    assert vmem_mib <= CALL_VMEM_MIB
    return pltpu.CompilerParams(dimension_semantics=semantics,
                                vmem_limit_bytes=(SMALL_VMEM_MIB if small else CALL_VMEM_MIB) * MIB)


def _in_hbm(a):
    return pltpu.with_memory_space_constraint(a, pltpu.HBM) if a.size * a.dtype.itemsize >= MIB else a


def _pallas(body, **kwargs):
    return pl.pallas_call(body, **kwargs)


def _resident(shape):
    return pl.BlockSpec(shape, lambda *_: (0,) * len(shape), pipeline_mode=pl.Buffered(1))


def _dot(a, b):
    return jnp.dot(a, b, preferred_element_type=F32)


def _dot_nt(a, b):
    return lax.dot_general(a, b, (((1,), (1,)), ((), ())), preferred_element_type=F32)


def _dot_tn(a, b):
    return lax.dot_general(a, b, (((0,), (0,)), ((), ())), preferred_element_type=F32)


_GELU_C = math.sqrt(2.0 / math.pi)


def _gelu(x):
    return x * (0.5 * (1.0 + jnp.tanh(_GELU_C * (x + 0.044715 * (x * x * x)))))


def _gelu_grad(x):
    t = jnp.tanh(_GELU_C * (x + 0.044715 * (x * x * x)))
    return 0.5 * (1.0 + t) + 0.5 * x * (1.0 - t * t) * (_GELU_C * (1.0 + 3.0 * 0.044715 * (x * x)))


def _rsqrt_ms(v):
    return lax.rsqrt(jnp.mean(v * v, axis=-1, keepdims=True) + EPS)


def _rmsnorm_bwd(dn, src, gain):
    r = _rsqrt_ms(src)
    t = gain * dn
    dgain = jnp.sum(dn * (src * r), axis=0, keepdims=True)
    dsrc = r * t - src * ((r * r * r) * jnp.mean(t * src, axis=-1, keepdims=True))
    return dsrc, dgain


def _rot_half(v):
    w = v.shape[-1]
    lane = lax.broadcasted_iota(jnp.int32, v.shape, v.ndim - 1)
    return jnp.where((lane % HEAD_DIM) < HEAD_DIM // 2, pltpu.roll(v, w - HEAD_DIM // 2, v.ndim - 1),
                     pltpu.roll(v, HEAD_DIM // 2, v.ndim - 1))


def _head_masks(shape):
    lane = lax.broadcasted_iota(jnp.int32, shape, 1)
    return [(lane >= h * HEAD_DIM) & (lane < (h + 1) * HEAD_DIM) for h in range(HEADS_PER_GROUP)]


def _head_stack(block, hmask):
    zero = jnp.zeros((), block.dtype)
    return jnp.concatenate([jnp.where(hm, block, zero) for hm in hmask], axis=0)


LANES = 128


def _put_residue(slab, val, out_ref, dil, width, col0):
    tm, w = val.shape
    if dil == 1:
        out_ref[:, col0:col0 + w] = val.astype(out_ref.dtype)
        return
    for k in range(w // LANES):
        slab[k] = val[:, k * LANES:(k + 1) * LANES]
    for r in range(dil):
        for k in range(w // LANES):
            c = r * width + col0 + k * LANES
            out_ref[:, c:c + LANES] = slab[k, pl.ds(r, tm // dil, stride=dil), :].astype(out_ref.dtype)


def _get_tokens(slab, in_ref, dil, width, col0, w):
    if dil == 1:
        return in_ref[:, col0:col0 + w].astype(F32)
    rows = in_ref.shape[0]
    for r in range(dil):
        for k in range(w // LANES):
            c = r * width + col0 + k * LANES
            slab[k, pl.ds(r, rows, stride=dil), :] = in_ref[:, c:c + LANES].astype(F32)
    return jnp.concatenate([slab[k] for k in range(w // LANES)], axis=1)


def _rope_tables(seq, rider=None):
    half = HEAD_DIM // 2
    inv_freq = ROPE_THETA ** (-jnp.arange(half, dtype=F32) / half)
    freq = jnp.tile(inv_freq, LANES // half).reshape(1, LANES)
    tm = 512

    def body(f_ref, *refs):
        outs, slab_c, slab_s = refs[:-2], refs[-2], refs[-1]
        row = lax.broadcasted_iota(jnp.int32, (tm, LANES), 0) + pl.program_id(0) * tm
        lane = lax.broadcasted_iota(jnp.int32, (tm, LANES), 1)
        ang = row.astype(F32) * f_ref[...]
        cos = jnp.cos(ang)
        sin = jnp.where((lane % HEAD_DIM) < half, -jnp.sin(ang), jnp.sin(ang))
        slab_c[0] = cos
        slab_s[0] = sin
        for i, dil in enumerate(DILATIONS):
            for tab, slab in ((outs[2 * i], slab_c), (outs[2 * i + 1], slab_s)):
                for r in range(dil):
                    piece = slab[0, pl.ds(r, tm // dil, stride=dil), :] if dil > 1 else slab[0]
                    for k in range(GROUP_W // LANES):
                        tab[:, r * GROUP_W + k * LANES:r * GROUP_W + (k + 1) * LANES] = piece

    outs, riding = _call(
        body, name="rope_tables", grid=(seq // tm,),
        in_specs=[pl.BlockSpec((1, LANES), lambda i: (0, 0))],
        out_specs=[pl.BlockSpec((tm // d, d * GROUP_W), lambda i: (i, 0)) for d in DILATIONS for _ in range(2)],
        out_shape=[jax.ShapeDtypeStruct((seq // d, d * GROUP_W), F32) for d in DILATIONS for _ in range(2)],
        scratch_shapes=[pltpu.VMEM((1, tm, LANES), F32)] * 2,
        params=_params(("arbitrary",), 32), args=(freq,), rider=rider)
    return {d: (outs[2 * i], outs[2 * i + 1]) for i, d in enumerate(DILATIONS)}, riding


def _norm_in(x, g0, rider=None):
    seq = x.shape[0]
    tm = 256

    def body(x_ref, g_ref, *refs):
        h_refs, slab = refs[:N_GROUPS], refs[-1]
        xv = x_ref[...]
        hf = (xv * _rsqrt_ms(xv)) * g_ref[...]
        for g, dil in enumerate(DILATIONS):
            _put_residue(slab, hf, h_refs[g], dil, D_MODEL, 0)

    return _call(
        body, name="norm_in", grid=(seq // tm,),
        in_specs=[pl.BlockSpec((tm, D_MODEL), lambda i: (i, 0)), pl.BlockSpec((1, D_MODEL), lambda i: (0, 0))],
        out_specs=[pl.BlockSpec((tm // d, d * D_MODEL), lambda i: (i, 0)) for d in DILATIONS],
        out_shape=[jax.ShapeDtypeStruct((seq // d, d * D_MODEL), BF16) for d in DILATIONS],
        scratch_shapes=[pltpu.VMEM((D_MODEL // LANES, tm, LANES), F32)],
        params=_params(("arbitrary",), 32), args=(x, g0), rider=rider)


def _in_proj(h, w_in, cos_t, sin_t, rider=None):
    seq = h.shape[0]
    tm, tn = 512, GROUP_W
    n_qk = 2 * ATTN_W // tn
    n_qkv = QKV_W // tn

    def body(h_ref, w_ref, cos_ref, sin_ref, *refs):
        qkv_refs, rest_ref, slab = refs[:N_GROUPS], refs[N_GROUPS], refs[-1]
        hb = h_ref[...]
        cos, sin = cos_ref[...], sin_ref[...]
        for j in range(IN_W // tn):
            p = _dot_nt(hb, w_ref[j * tn:(j + 1) * tn, :])
            if j < n_qkv:
                if j < n_qk:
                    p = p * cos + _rot_half(p) * sin
                section, g = divmod(j, N_GROUPS)
                _put_residue(slab, p, qkv_refs[g], DILATIONS[g], 3 * GROUP_W, section * GROUP_W)
            else:
                rest_ref[:, (j - n_qkv) * tn:(j - n_qkv + 1) * tn] = p.astype(BF16)

    return _call(
        body, name="in_proj", grid=(seq // tm,),
        in_specs=[pl.BlockSpec((tm, D_MODEL), lambda i: (i, 0)),
                  _resident((IN_W, D_MODEL)),
                  pl.BlockSpec((tm, GROUP_W), lambda i: (i, 0)),
                  pl.BlockSpec((tm, GROUP_W), lambda i: (i, 0))],
        out_specs=[pl.BlockSpec((tm // d, d * 3 * GROUP_W), lambda i: (i, 0)) for d in DILATIONS]
        + [pl.BlockSpec((tm, REST_W), lambda i: (i, 0))],
        out_shape=[jax.ShapeDtypeStruct((seq // d, d * 3 * GROUP_W), BF16) for d in DILATIONS]
        + [jax.ShapeDtypeStruct((seq, REST_W), BF16)],
        scratch_shapes=[pltpu.VMEM((GROUP_W // LANES, tm, LANES), F32)],
        params=_params(("arbitrary",), 48), args=(h, w_in, cos_t, sin_t), rider=rider)


def _band_masks():
    qi = lax.broadcasted_iota(jnp.int32, (QBLK, QBLK), 0)
    kj = lax.broadcasted_iota(jnp.int32, (QBLK, QBLK), 1)
    return kj <= qi, kj >= qi


def _attn_tile(length):
    return min(512, length)


def _attn_fwd(qkv, dil, rider=None):
    length = qkv.shape[0]
    tq = _attn_tile(length)
    nsub = tq // QBLK
    nblk = length // tq

    def body(q_ref, k_ref, v_ref, kp_ref, vp_ref, o_ref, l_ref):
        n = pl.program_id(1)
        mask_c, mask_p0 = _band_masks()
        hmask = _head_masks((QBLK, GROUP_W))
        zero = jnp.zeros((), BF16)
        for b in range(nsub):
            rows = slice(b * QBLK, (b + 1) * QBLK)
            q = q_ref[rows, :]
            kc, vc = k_ref[rows, :], v_ref[rows, :]
            if b == 0:
                kp, vp = kp_ref[...], vp_ref[...]
                mask_p = mask_p0 & (n > 0)
            else:
                prow = slice((b - 1) * QBLK, b * QBLK)
                kp, vp = k_ref[prow, :], v_ref[prow, :]
                mask_p = mask_p0
            o_acc = jnp.zeros((QBLK, GROUP_W), F32)
            l_acc = jnp.zeros((QBLK, GROUP_W), F32)
            for h in range(HEADS_PER_GROUP):
                hm = hmask[h]
                sc = jnp.where(mask_c, _dot_nt(q, jnp.where(hm, kc, zero)) * SCALE, NEG)
                sp = jnp.where(mask_p, _dot_nt(q, jnp.where(hm, kp, zero)) * SCALE, NEG)
                m = jnp.maximum(jnp.max(sc, axis=-1, keepdims=True), jnp.max(sp, axis=-1, keepdims=True))
                pc, pp = jnp.exp(sc - m), jnp.exp(sp - m)
                den = jnp.sum(pc, axis=-1, keepdims=True) + jnp.sum(pp, axis=-1, keepdims=True)
                pv = _dot(pc.astype(BF16), jnp.where(hm, vc, zero)) + _dot(pp.astype(BF16), jnp.where(hm, vp, zero))
                o_acc = o_acc + pv / den
                l_acc = l_acc + jnp.where(hm, m + jnp.log(den), 0.0)
            o_ref[rows, :] = o_acc
            l_ref[rows, :] = l_acc

    cur = lambda sec: pl.BlockSpec((tq, GROUP_W), lambda r, n: (n, r * 3 + sec))
    prev = lambda sec: pl.BlockSpec((QBLK, GROUP_W), lambda r, n: (jnp.maximum(n * nsub - 1, 0), r * 3 + sec))
    return _call(
        body, name=f"attn_fwd_d{dil}", grid=(dil, nblk),
        in_specs=[cur(0), cur(1), cur(2), prev(1), prev(2)],
        out_specs=[pl.BlockSpec((tq, GROUP_W), lambda r, n: (n, r))] * 2,
        out_shape=[jax.ShapeDtypeStruct((length, dil * GROUP_W), F32)] * 2, scratch_shapes=[],
        params=_params(("arbitrary", "arbitrary"), 32), args=(qkv, qkv, qkv, qkv, qkv), rider=rider)


def _attn_bwd(qkv, dy, y, lse, cos_t, sin_t, dil, rider=None):
    length = qkv.shape[0]
    tq = _attn_tile(length)
    nsub = tq // QBLK
    nblk = length // tq

    def body(q_ref, k_ref, v_ref, kp_ref, vp_ref, qn_ref, dy_ref, y_ref, l_ref, dyn_ref, yn_ref, ln_ref,
             cos_ref, sin_ref, out_ref, dq_s, dk_s, dv_s):
        n = pl.program_id(1)
        mask_c, mask_p0 = _band_masks()
        hmask = _head_masks((QBLK, GROUP_W))
        sub = lambda ref, b: ref[b * QBLK:(b + 1) * QBLK, :]
        kbd = [_head_stack(kp_ref[...], hmask)] + [_head_stack(sub(k_ref, b), hmask) for b in range(nsub)]
        vbd = [_head_stack(vp_ref[...], hmask)] + [_head_stack(sub(v_ref, b), hmask) for b in range(nsub)]
        dq_s[...] = jnp.zeros(dq_s.shape, F32)

        def query_terms(q, dyv, yv, lv):
            prod = dyv * yv
            return dict(
                q=q, dy=dyv.astype(BF16), q_heads=[jnp.where(hm, q, jnp.zeros((), BF16)) for hm in hmask],
                dy_heads=[jnp.where(hm, dyv, 0.0).astype(BF16) for hm in hmask],
                delta=[jnp.sum(jnp.where(hm, prod, 0.0), axis=-1, keepdims=True) for hm in hmask],
                lse=[jnp.max(jnp.where(hm, lv, NEG), axis=-1, keepdims=True) for hm in hmask])

        queries = [query_terms(sub(q_ref, b), sub(dy_ref, b), sub(y_ref, b), sub(l_ref, b)) for b in range(nsub)]
        queries.append(query_terms(qn_ref[...], dyn_ref[...], yn_ref[...], ln_ref[...]))
        rows_of = lambda items: items[0] if len(items) == 1 else jnp.concatenate(items, axis=0)
        for kb in range(nsub + 1):
            seen = [(kb - 1, mask_c)] if kb >= 1 else []
            if kb == 0:
                seen.append((0, mask_p0 & (n > 0)))
            elif kb < nsub:
                seen.append((kb, mask_p0))
            else:
                seen.append((nsub, mask_p0 & (n < nblk - 1)))
            qs = [queries[b] for b, _ in seen]
            mask = rows_of([m for _, m in seen])
            s = _dot_nt(rows_of([t["q"] for t in qs]), kbd[kb]) * SCALE
            dp = _dot_nt(rows_of([t["dy"] for t in qs]), vbd[kb])
            ps, dss = [], []
            for h in range(HEADS_PER_GROUP):
                cols = slice(h * QBLK, (h + 1) * QBLK)
                p = jnp.exp(jnp.where(mask, s[:, cols] - rows_of([t["lse"][h] for t in qs]), NEG))
                ps.append(p.astype(BF16))
                dss.append((p * (dp[:, cols] - rows_of([t["delta"][h] for t in qs]))).astype(BF16))
            dq = _dot(jnp.concatenate(dss, axis=1), kbd[kb]) * SCALE
            for i, (b, _) in enumerate(seen):
                if b < nsub:
                    dq_s[b * QBLK:(b + 1) * QBLK, :] += dq[i * QBLK:(i + 1) * QBLK, :]
            if kb >= 1:
                krows = slice((kb - 1) * QBLK, kb * QBLK)
                head_rows = lambda key: jnp.concatenate([t[key][h] for h in range(HEADS_PER_GROUP) for t in qs], axis=0)
                dv_s[krows, :] = _dot_tn(jnp.concatenate(ps, axis=0), head_rows("dy_heads"))
                dk_s[krows, :] = _dot_tn(jnp.concatenate(dss, axis=0), head_rows("q_heads")) * SCALE
        cos, sin = cos_ref[...], sin_ref[...]
        dq, dk = dq_s[...], dk_s[...]
        out_ref[:, 0:GROUP_W] = (dq * cos - _rot_half(dq) * sin).astype(BF16)
        out_ref[:, GROUP_W:2 * GROUP_W] = (dk * cos - _rot_half(dk) * sin).astype(BF16)
        out_ref[:, 2 * GROUP_W:3 * GROUP_W] = dv_s[...].astype(BF16)

    cur = lambda sec: pl.BlockSpec((tq, GROUP_W), lambda r, n: (n, r * 3 + sec))
    prev = lambda sec: pl.BlockSpec((QBLK, GROUP_W), lambda r, n: (jnp.maximum(n * nsub - 1, 0), r * 3 + sec))
    nxt_q = pl.BlockSpec((QBLK, GROUP_W), lambda r, n: (jnp.minimum((n + 1) * nsub, nblk * nsub - 1), r * 3))
    tok = pl.BlockSpec((tq, GROUP_W), lambda r, n: (n, r))
    tok_next = pl.BlockSpec((QBLK, GROUP_W), lambda r, n: (jnp.minimum((n + 1) * nsub, nblk * nsub - 1), r))
    (out,), riding = _call(
        body, name=f"attn_bwd_d{dil}", grid=(dil, nblk),
        in_specs=[cur(0), cur(1), cur(2), prev(1), prev(2), nxt_q,
                  tok, tok, tok, tok_next, tok_next, tok_next, tok, tok],
        out_specs=[pl.BlockSpec((tq, 3 * GROUP_W), lambda r, n: (n, r))],
        out_shape=[jax.ShapeDtypeStruct((length, dil * 3 * GROUP_W), BF16)],
        scratch_shapes=[pltpu.VMEM((tq, GROUP_W), F32)] * 3,
        params=_params(("arbitrary", "arbitrary"), 32),
        args=(qkv, qkv, qkv, qkv, qkv, qkv, dy, y, lse, dy, y, lse, cos_t, sin_t), rider=rider)
    return out, riding


def _layernorm_stats(z):
    mu = jnp.mean(z, axis=-1, keepdims=True)
    zc = z - mu
    rstd = lax.rsqrt(jnp.mean(zc * zc, axis=-1, keepdims=True) + EPS)
    return zc * rstd, rstd


def _tril_mask():
    row = lax.broadcasted_iota(jnp.int32, (CHUNK, CHUNK), 0)
    col = lax.broadcasted_iota(jnp.int32, (CHUNK, CHUNK), 1)
    return col <= row


def _mix_fwd(o_l, rest, x, w_sp, b_col, ln_g, ln_b, w_ba, w_bg, w_out, g1, rider=None):
    seq = x.shape[0]
    tm = 256

    def body(o0, l0, o1, l1, o2, l2, up_ref, zp_ref, gap_ref, gbp_ref, x_ref, wsp_ref, bcol_ref, lg_ref, lb_ref,
             wba_ref, wbg_ref, wout_ref, g1_ref, ya0, lj0, ya1, lj1, ya2, lj2, yg_ref, mg_ref, y_ref, x1_ref, slab):
        outs = [_get_tokens(slab, o, d, GROUP_W, 0, GROUP_W) for o, d in zip((o0, o1, o2), DILATIONS)]
        lses = [_get_tokens(slab, l, d, GROUP_W, 0, GROUP_W) for l, d in zip((l0, l1, l2), DILATIONS)]
        m = jnp.maximum(jnp.maximum(lses[0], lses[1]), lses[2])
        es = [jnp.exp(l - m) for l in lses]
        tot = es[0] + es[1] + es[2]
        ya = (es[0] * outs[0] + es[1] * outs[1] + es[2] * outs[2]) / tot
        lj = m + jnp.log(tot)
        for ya_ref, lj_ref, d in zip((ya0, ya1, ya2), (lj0, lj1, lj2), DILATIONS):
            _put_residue(slab, ya, ya_ref, d, GROUP_W, 0)
            _put_residue(slab, lj, lj_ref, d, GROUP_W, 0)
        zhat, _ = _layernorm_stats(_gelu(zp_ref[...].astype(F32)))
        zln = (zhat * lg_ref[...] + lb_ref[...]).astype(BF16)
        u = _gelu(up_ref[...].astype(F32))
        tril = _tril_mask()
        for g in range(GMLP_GROUPS):
            wm = jnp.where(tril, wsp_ref[g], 0.0).astype(BF16)
            cols = slice(g * CHUNK, (g + 1) * CHUNK)
            for c in range(tm // CHUNK):
                rows = slice(c * CHUNK, (c + 1) * CHUNK)
                sz = _dot(wm, zln[rows, cols]) + bcol_ref[g]
                yg_ref[rows, cols] = (u[rows, cols] * sz).astype(BF16)
        a = _dot(ya.astype(BF16), wba_ref[...])
        bm = _dot(yg_ref[...], wbg_ref[...])
        merged = (jax.nn.sigmoid(gap_ref[...].astype(F32)) * a + jax.nn.sigmoid(gbp_ref[...].astype(F32)) * bm).astype(BF16)
        mg_ref[...] = merged
        yv = _dot(merged, wout_ref[...])
        y_ref[...] = yv
        x1_ref[...] = x_ref[...] + (yv * _rsqrt_ms(yv)) * g1_ref[...]

    tok = lambda w: pl.BlockSpec((tm, w), lambda i: (i, 0))
    res = lambda d: pl.BlockSpec((tm // d, d * GROUP_W), lambda i: (i, 0))
    full = lambda *s: pl.BlockSpec(s, lambda i: (0,) * len(s))
    res_specs = [res(d) for d in DILATIONS for _ in range(2)]
    return _call(
        body, name="mix_fwd", grid=(seq // tm,),
        in_specs=res_specs + [
            pl.BlockSpec((tm, GMLP_W), lambda i: (i, 0)), pl.BlockSpec((tm, GMLP_W), lambda i: (i, 1)),
            pl.BlockSpec((tm, D_MODEL), lambda i: (i, 1)), pl.BlockSpec((tm, D_MODEL), lambda i: (i, 2)),
            tok(D_MODEL), full(GMLP_GROUPS, CHUNK, CHUNK), full(GMLP_GROUPS, CHUNK, 1), full(1, GMLP_W), full(1, GMLP_W),
            full(GROUP_W, D_MODEL), full(GMLP_W, D_MODEL), full(D_MODEL, D_MODEL), full(1, D_MODEL)],
        out_specs=res_specs + [tok(GMLP_W), tok(D_MODEL), tok(D_MODEL), tok(D_MODEL)],
        out_shape=[jax.ShapeDtypeStruct((seq // d, d * GROUP_W), F32) for d in DILATIONS for _ in range(2)]
        + [jax.ShapeDtypeStruct((seq, GMLP_W), BF16), jax.ShapeDtypeStruct((seq, D_MODEL), BF16),
           jax.ShapeDtypeStruct((seq, D_MODEL), F32), jax.ShapeDtypeStruct((seq, D_MODEL), F32)],
        scratch_shapes=[pltpu.VMEM((GROUP_W // LANES, tm, LANES), F32)],
        params=_params(("arbitrary",), 48),
        args=(*o_l, rest, rest, rest, rest, x, w_sp, b_col, ln_g, ln_b, w_ba, w_bg, w_out, g1), rider=rider)


def _mlp_fwd(x1, g2, g3, w_mi, w_mo, target):
    seq = x1.shape[0]
    tm, tf = MLP_FWD_TM, 512

    def body(x1_ref, g2_ref, g3_ref, wmi_ref, wmo_ref, t_ref, h2_ref, a_ref, dy2_ref, dout_ref, loss_ref, dg3_ref, sq_s):
        @pl.when(pl.program_id(0) == 0)
        def _():
            loss_ref[...] = jnp.zeros(loss_ref.shape, F32)
            dg3_ref[...] = jnp.zeros(dg3_ref.shape, F32)

        xv = x1_ref[...]
        hb = ((xv * _rsqrt_ms(xv)) * g2_ref[...]).astype(BF16)
        h2_ref[...] = hb
        for j in range(D_FF // tf):
            cols = slice(j * tf, (j + 1) * tf)
            a = jnp.maximum(_dot(hb, wmi_ref[:, cols]), 0.0)
            a_ref[:, cols] = a.astype(BF16)
            sq_s[:, cols] = (a * a).astype(BF16)
        y2 = _dot(sq_s[...], wmo_ref[...])
        r3 = _rsqrt_ms(y2)
        out = xv + (y2 * r3) * g3_ref[...]
        diff = out - t_ref[...]
        tile_loss = 0.5 * jnp.sum(jnp.mean(diff * diff, axis=-1, keepdims=True), axis=0, keepdims=True)
        loss_ref[...] += jnp.broadcast_to(tile_loss, loss_ref.shape)
        dout = diff * (1.0 / D_MODEL)
        dout_ref[...] = dout
        dy2, dg3 = _rmsnorm_bwd(dout, y2, g3_ref[...])
        dy2_ref[...] = dy2.astype(BF16)
        dg3_ref[...] += dg3

    tok = lambda w: pl.BlockSpec((tm, w), lambda i: (i, 0))
    vec = pl.BlockSpec((1, D_MODEL), lambda i: (0, 0))
    return _pallas(
        body, name="mlp_fwd", grid=(seq // tm,),
        in_specs=[tok(D_MODEL), vec, vec, _resident((D_MODEL, D_FF)), _resident((D_FF, D_MODEL)), tok(D_MODEL)],
        out_specs=[tok(D_MODEL), tok(D_FF), tok(D_MODEL), tok(D_MODEL), pl.BlockSpec((8, 128), lambda i: (0, 0)), vec],
        out_shape=[jax.ShapeDtypeStruct((seq, D_MODEL), BF16), jax.ShapeDtypeStruct((seq, D_FF), BF16),
                   jax.ShapeDtypeStruct((seq, D_MODEL), BF16), jax.ShapeDtypeStruct((seq, D_MODEL), F32),
                   jax.ShapeDtypeStruct((8, 128), F32), jax.ShapeDtypeStruct((1, D_MODEL), F32)],
        scratch_shapes=[pltpu.VMEM((tm, D_FF), BF16)],
        compiler_params=_params(("arbitrary",), 56),
    )(*map(_in_hbm, (x1, g2, g3, w_mi, w_mo, target)))


def _mlp_bwd(dy2, a, w_mo, w_mi, dout, x1, y, g2, g1, rider=None):
    seq = x1.shape[0]
    tm, tf = MLP_TM, 512

    def body(dy2_ref, a_ref, wmo_ref, wmi_ref, dout_ref, x1_ref, y_ref, g2_ref, g1_ref,
             dap_ref, dx1_ref, dy_ref, dg2_ref, dg1_ref):
        @pl.when(pl.program_id(0) == 0)
        def _():
            dg2_ref[...] = jnp.zeros(dg2_ref.shape, F32)
            dg1_ref[...] = jnp.zeros(dg1_ref.shape, F32)

        dy2v = dy2_ref[...]
        for j in range(D_FF // tf):
            cols = slice(j * tf, (j + 1) * tf)
            da2 = _dot_nt(dy2v, wmo_ref[cols, :])
            dap_ref[:, cols] = (da2 * (2.0 * a_ref[:, cols].astype(F32))).astype(BF16)
        dh2 = _dot_nt(dap_ref[...], wmi_ref[...])
        dres, dg2 = _rmsnorm_bwd(dh2, x1_ref[...], g2_ref[...])
        dx1 = dout_ref[...] + dres
        dx1_ref[...] = dx1
        dg2_ref[...] += dg2
        dyv, dg1 = _rmsnorm_bwd(dx1, y_ref[...], g1_ref[...])
        dy_ref[...] = dyv.astype(BF16)
        dg1_ref[...] += dg1

    tok = lambda w: pl.BlockSpec((tm, w), lambda i: (i, 0))
    vec = pl.BlockSpec((1, D_MODEL), lambda i: (0, 0))
    return _call(
        body, name="mlp_bwd", grid=(seq // tm,),
        in_specs=[tok(D_MODEL), tok(D_FF), _resident((D_FF, D_MODEL)), _resident((D_MODEL, D_FF)),
                  tok(D_MODEL), tok(D_MODEL), tok(D_MODEL), vec, vec],
        out_specs=[tok(D_FF), tok(D_MODEL), tok(D_MODEL), vec, vec],
        out_shape=[jax.ShapeDtypeStruct((seq, D_FF), BF16), jax.ShapeDtypeStruct((seq, D_MODEL), F32),
                   jax.ShapeDtypeStruct((seq, D_MODEL), BF16), jax.ShapeDtypeStruct((1, D_MODEL), F32),
                   jax.ShapeDtypeStruct((1, D_MODEL), F32)], scratch_shapes=[],
        params=_params(("arbitrary",), 56), args=(dy2, a, w_mo, w_mi, dout, x1, y, g2, g1), rider=rider)


def _tn_matmul(a, b, name, bm, bn, square_a=False, column_shards=False, rider=None):
    seq, m = a.shape
    n = b.shape[1]
    ts = 2048

    def body(a_ref, b_ref, o_ref):
        @pl.when(pl.program_id(2) == 0)
        def _():
            o_ref[...] = jnp.zeros(o_ref.shape, F32)

        av = a_ref[...]
        if square_a:
            af = av.astype(F32)
            av = (af * af).astype(BF16)
        o_ref[...] += _dot_tn(av, b_ref[...])

    if column_shards:
        out_spec = pl.BlockSpec((None, bm, bn), lambda mi, ni, s: (ni, mi, 0))
        out_shape = jax.ShapeDtypeStruct((n // bn, m, bn), F32)
    else:
        out_spec = pl.BlockSpec((bm, bn), lambda mi, ni, s: (mi, ni))
        out_shape = jax.ShapeDtypeStruct((m, n), F32)
    (out,), riding = _call(
        body, name=name, grid=(m // bm, n // bn, seq // ts),
        in_specs=[pl.BlockSpec((ts, bm), lambda mi, ni, s: (s, mi)), pl.BlockSpec((ts, bn), lambda mi, ni, s: (s, ni))],
        out_specs=[out_spec], out_shape=[out_shape], scratch_shapes=[],
        params=_params(("arbitrary", "arbitrary", "arbitrary"), 40), args=(a, b), rider=rider)
    return out, riding


def _tn_matmul_residue(a, b, dil, name):
    length = a.shape[0]
    m, n = a.shape[1] // dil, b.shape[1] // dil
    ts = min(1024, length)

    def body(a_ref, b_ref, o_ref):
        @pl.when((pl.program_id(0) == 0) & (pl.program_id(1) == 0))
        def _():
            o_ref[...] = jnp.zeros(o_ref.shape, F32)

        o_ref[...] += _dot_tn(a_ref[...], b_ref[...])

    return _pallas(
        body, name=name, grid=(dil, length // ts),
        in_specs=[pl.BlockSpec((ts, m), lambda r, s: (s, r)), pl.BlockSpec((ts, n), lambda r, s: (s, r))],
        out_specs=pl.BlockSpec((m, n), lambda r, s: (0, 0)),
        out_shape=jax.ShapeDtypeStruct((m, n), F32),
        compiler_params=_params(("arbitrary", "arbitrary"), 40),
    )(_in_hbm(a), _in_hbm(b))


def _mix_bwd(dy, ya, yg, mg, rest, w_out, w_ba, w_bg, w_sp, b_col, ln_g, ln_b, rider=None):
    seq = dy.shape[0]
    tm = 256

    def body(dy_ref, ya_ref, yg_ref, mg_ref, up_ref, zp_ref, gap_ref, gbp_ref, wout_ref, wba_ref, wbg_ref,
             wsp_ref, bcol_ref, lg_ref, lb_ref,
             dya0, dya1, dya2, dpr_ref, dwout_ref, dwba_ref, dwbg_ref, dwsp_ref, dbb_ref, dlg_ref, dlb_ref,
             dzln_s, du_s, slab):
        @pl.when(pl.program_id(0) == 0)
        def _():
            for ref in (dwout_ref, dwba_ref, dwbg_ref, dwsp_ref, dbb_ref, dlg_ref, dlb_ref):
                ref[...] = jnp.zeros(ref.shape, F32)

        dyv = dy_ref[...]
        dm = _dot_nt(dyv, wout_ref[...])
        dwout_ref[...] += _dot_tn(mg_ref[...], dyv)
        yab = ya_ref[...].astype(BF16)
        ygb = yg_ref[...]
        a = _dot(yab, wba_ref[...])
        bm = _dot(ygb, wbg_ref[...])
        ga = jax.nn.sigmoid(gap_ref[...].astype(F32))
        gb = jax.nn.sigmoid(gbp_ref[...].astype(F32))
        dpr_ref[:, 2 * GMLP_W:2 * GMLP_W + D_MODEL] = (dm * a * (ga * (1.0 - ga))).astype(BF16)
        dpr_ref[:, 2 * GMLP_W + D_MODEL:REST_W] = (dm * bm * (gb * (1.0 - gb))).astype(BF16)
        da = (dm * ga).astype(BF16)
        db = (dm * gb).astype(BF16)
        dwba = _dot_tn(yab, da)
        dwbg = _dot_tn(ygb, db)
        shard_w = D_MODEL // N_CHIPS
        for j in range(N_CHIPS):
            dwba_ref[j] += dwba[:, j * shard_w:(j + 1) * shard_w]
            dwbg_ref[j] += dwbg[:, j * shard_w:(j + 1) * shard_w]
        dya = _dot_nt(da, wba_ref[...])
        for dya_ref, d in zip((dya0, dya1, dya2), DILATIONS):
            _put_residue(slab, dya, dya_ref, d, GROUP_W, 0)
        dyg = _dot_nt(db, wbg_ref[...])

        zp = zp_ref[...].astype(F32)
        zhat, rstd = _layernorm_stats(_gelu(zp))
        lg = lg_ref[...]
        zln = (zhat * lg + lb_ref[...]).astype(BF16)
        up = up_ref[...].astype(F32)
        u = _gelu(up)
        tril = _tril_mask()
        for g in range(GMLP_GROUPS):
            wm = jnp.where(tril, wsp_ref[g], 0.0).astype(BF16)
            cols = slice(g * CHUNK, (g + 1) * CHUNK)
            for c in range(tm // CHUNK):
                rows = slice(c * CHUNK, (c + 1) * CHUNK)
                zb = zln[rows, cols]
                sz = _dot(wm, zb) + bcol_ref[g]
                dyg_cg = dyg[rows, cols]
                du_s[rows, cols] = dyg_cg * sz
                dsz = dyg_cg * u[rows, cols]
                dszb = dsz.astype(BF16)
                dbb_ref[g] += jnp.broadcast_to(jnp.sum(dsz, axis=-1, keepdims=True), (CHUNK, CHUNK))
                dwsp_ref[g] += jnp.where(tril, _dot_nt(dszb, zb), 0.0)
                dzln_s[rows, cols] = _dot_tn(wm, dszb)
        dzln = dzln_s[...]
        dlg_ref[...] += jnp.sum(dzln * zhat, axis=0, keepdims=True)
        dlb_ref[...] += jnp.sum(dzln, axis=0, keepdims=True)
        dzh = dzln * lg
        dz = rstd * (dzh - jnp.mean(dzh, axis=-1, keepdims=True) - zhat * jnp.mean(dzh * zhat, axis=-1, keepdims=True))
        dpr_ref[:, GMLP_W:2 * GMLP_W] = (dz * _gelu_grad(zp)).astype(BF16)
        dpr_ref[:, 0:GMLP_W] = (du_s[...] * _gelu_grad(up)).astype(BF16)

    tok = lambda w: pl.BlockSpec((tm, w), lambda i: (i, 0))
    full = lambda *s: pl.BlockSpec(s, lambda i: (0,) * len(s))
    return _call(
        body, name="mix_bwd", grid=(seq // tm,),
        in_specs=[tok(D_MODEL), tok(GROUP_W), tok(GMLP_W), tok(D_MODEL),
                  pl.BlockSpec((tm, GMLP_W), lambda i: (i, 0)), pl.BlockSpec((tm, GMLP_W), lambda i: (i, 1)),
                  pl.BlockSpec((tm, D_MODEL), lambda i: (i, 1)), pl.BlockSpec((tm, D_MODEL), lambda i: (i, 2)),
                  full(D_MODEL, D_MODEL), full(GROUP_W, D_MODEL), full(GMLP_W, D_MODEL),
                  full(GMLP_GROUPS, CHUNK, CHUNK), full(GMLP_GROUPS, CHUNK, 1), full(1, GMLP_W), full(1, GMLP_W)],
        out_specs=[pl.BlockSpec((tm // d, d * GROUP_W), lambda i: (i, 0)) for d in DILATIONS]
        + [tok(REST_W), full(D_MODEL, D_MODEL), full(N_CHIPS, GROUP_W, D_MODEL // N_CHIPS),
           full(N_CHIPS, GMLP_W, D_MODEL // N_CHIPS),
           full(GMLP_GROUPS, CHUNK, CHUNK), full(GMLP_GROUPS, CHUNK, CHUNK), full(1, GMLP_W), full(1, GMLP_W)],
        out_shape=[jax.ShapeDtypeStruct((seq // d, d * GROUP_W), F32) for d in DILATIONS]
        + [jax.ShapeDtypeStruct((seq, REST_W), BF16),
           jax.ShapeDtypeStruct((D_MODEL, D_MODEL), F32), jax.ShapeDtypeStruct((N_CHIPS, GROUP_W, D_MODEL // N_CHIPS), F32),
           jax.ShapeDtypeStruct((N_CHIPS, GMLP_W, D_MODEL // N_CHIPS), F32),
           jax.ShapeDtypeStruct((GMLP_GROUPS, CHUNK, CHUNK), F32),
           jax.ShapeDtypeStruct((GMLP_GROUPS, CHUNK, CHUNK), F32), jax.ShapeDtypeStruct((1, GMLP_W), F32),
           jax.ShapeDtypeStruct((1, GMLP_W), F32)],
        scratch_shapes=[pltpu.VMEM((tm, GMLP_W), F32), pltpu.VMEM((tm, GMLP_W), F32),
                        pltpu.VMEM((GROUP_W // LANES, tm, LANES), F32)],
        params=_params(("arbitrary",), 56),
        args=(dy, ya, yg, mg, rest, rest, rest, rest, w_out, w_ba, w_bg, w_sp, b_col, ln_g, ln_b), rider=rider)


IN_PROJ_BWD_TM = 256


def _in_proj_bwd(dqkv, drest, w_qkv, w_rest, x, dx1, g0, so_far, span, rider=None):
    seq = x.shape[0]
    tm = IN_PROJ_BWD_TM
    off, steps = span
    gx_so_far, dg_so_far = so_far

    def body(d0, d1, d2, dr_ref, w0, w1, w2, wr_ref, x_ref, dx1_ref, g_ref, dg_in_ref, gx_in_ref, gx_ref, dg_ref, slab):
        @pl.when(pl.program_id(0) == 0)
        def _():
            dg_ref[...] = dg_in_ref[...]

        dh = _dot(dr_ref[...], wr_ref[...])
        for d_ref, w_ref, dil in zip((d0, d1, d2), (w0, w1, w2), DILATIONS):
            piece = d_ref[...] if dil == 1 else _get_tokens(slab, d_ref, dil, 3 * GROUP_W, 0, 3 * GROUP_W).astype(BF16)
            dh = dh + _dot(piece, w_ref[...])
        dres, dg = _rmsnorm_bwd(dh, x_ref[...], g_ref[...])
        gx_ref[...] = dx1_ref[...] + dres
        dg_ref[...] += dg

    tok = lambda w: pl.BlockSpec((tm, w), lambda i: (i + off, 0))
    full = lambda *s: pl.BlockSpec(s, lambda i: (0,) * len(s))
    in_specs = ([pl.BlockSpec((tm // d, d * 3 * GROUP_W), lambda i: (i + off, 0)) for d in DILATIONS] + [tok(REST_W)]
                + [_resident((3 * GROUP_W, D_MODEL))] * 3 + [_resident((REST_W, D_MODEL))]
                + [tok(D_MODEL), tok(D_MODEL), full(1, D_MODEL), full(1, D_MODEL), HBM_SPEC])
    return _call(
        body, name=f"in_proj_bwd_{off}", grid=(steps,), in_specs=in_specs,
        out_specs=[tok(D_MODEL), full(1, D_MODEL)],
        out_shape=[jax.ShapeDtypeStruct((seq, D_MODEL), F32), jax.ShapeDtypeStruct((1, D_MODEL), F32)],
        scratch_shapes=[pltpu.VMEM((3 * GROUP_W // LANES, tm, LANES), F32)],
        params=_params(("arbitrary",), 48), args=(*dqkv, drest, *w_qkv, w_rest, x, dx1, g0, dg_so_far, gx_so_far),
        rider=rider, aliases={len(in_specs) - 1: 0})


def _adamw(w, g, m, v, name):
    rows, cols = w.shape
    tr = _row_tile(rows) if rows % 16 == 0 else rows
    c1 = 1.0 - ADAM_B1 ** ADAM_STEP
    c2 = 1.0 - ADAM_B2 ** ADAM_STEP

    def body(w_ref, g_ref, m_ref, v_ref, go_ref, d_ref, nm_ref, nv_ref):
        gv = g_ref[...]
        go_ref[...] = gv
        nm = ADAM_B1 * m_ref[...] + (1.0 - ADAM_B1) * gv
        nv = ADAM_B2 * v_ref[...] + (1.0 - ADAM_B2) * (gv * gv)
        d_ref[...] = -ADAM_LR * ((nm / c1) / (jnp.sqrt(nv / c2) + ADAM_EPS) + ADAM_WD * w_ref[...])
        nm_ref[...] = nm
        nv_ref[...] = nv

    spec = pl.BlockSpec((tr, cols), lambda i: (i, 0))
    return _pallas(
        body, name=name, grid=(rows // tr,),
        in_specs=[spec] * 4, out_specs=[spec] * 4,
        out_shape=[jax.ShapeDtypeStruct((rows, cols), F32)] * 4,
        compiler_params=_params(("arbitrary",), 32, small=True),
    )(w, g, m, v)


def _place():
    x, y, c = lax.axis_index("x"), lax.axis_index("y"), lax.axis_index("c")
    chips = [(1 - x, y), (x, 1 - y), (1 - x, 1 - y)]
    return x, y, c, chips


class _Exchange:
    def __init__(self, inputs, out_shapes, n_sems, start, finish, aliases=None):
        self.inputs, self.out_shapes, self.n_sems = list(inputs), list(out_shapes), n_sems
        self.start, self.finish, self.aliases = start, finish, dict(aliases or {})

    def scratch(self):
        return [pltpu.SemaphoreType.DMA((self.n_sems,)), pltpu.SemaphoreType.DMA((self.n_sems,))]


def _together(*parts):
    ins = [len(p.inputs) for p in parts]
    outs = [len(p.out_shapes) for p in parts]

    def split(refs, counts):
        pos, pieces = 0, []
        for cnt in counts:
            pieces.append(refs[pos:pos + cnt])
            pos += cnt
        return pieces

    def run(which):
        def go(in_refs, out_refs, *sems):
            for k, (p, i, o) in enumerate(zip(parts, split(in_refs, ins), split(out_refs, outs))):
                getattr(p, which)(i, o, sems[2 * k], sems[2 * k + 1])
        return go

    both = _Exchange([a for p in parts for a in p.inputs], [s for p in parts for s in p.out_shapes], 0, run("start"),
                     run("finish"))
    both.aliases = {sum(ins[:k]) + i: sum(outs[:k]) + o for k, p in enumerate(parts) for i, o in p.aliases.items()}
    both.scratch = lambda: [s for p in parts for s in p.scratch()]
    return both


def _run_exchange(ex, name):
    n_in, n_out = len(ex.inputs), len(ex.out_shapes)

    def body(*refs):
        ins, outs, sems = refs[:n_in], refs[n_in:n_in + n_out], refs[n_in + n_out:]
        ex.start(ins, outs, *sems)
        ex.finish(ins, outs, *sems)

    return _pallas(
        body, name=name, in_specs=[HBM_SPEC] * n_in, out_specs=[HBM_SPEC] * n_out, out_shape=ex.out_shapes,
        scratch_shapes=ex.scratch(), input_output_aliases=ex.aliases,
    )(*ex.inputs)


def _call(body, *, name, grid, in_specs, out_specs, out_shape, scratch_shapes, params, args, rider=None, aliases=None):
    in_specs, out_specs, out_shape, scratch_shapes = list(in_specs), list(out_specs), list(out_shape), list(scratch_shapes)
    aliases = dict(aliases or {})
    args = [_in_hbm(a) for a in args]
    if rider is None:
        outs = _pallas(body, name=name, grid=grid, in_specs=in_specs, out_specs=out_specs, out_shape=out_shape,
                              scratch_shapes=scratch_shapes, input_output_aliases=aliases, compiler_params=params)(*args)
        return list(outs), []
    n_in, n_out, n_scr = len(in_specs), len(out_specs), len(scratch_shapes)
    r_in, r_out = len(rider.inputs), len(rider.out_shapes)

    def wrapped(*refs):
        ins, r_ins = refs[:n_in], refs[n_in:n_in + r_in]
        pos = n_in + r_in
        outs, r_outs = refs[pos:pos + n_out], refs[pos + n_out:pos + n_out + r_out]
        pos += n_out + r_out
        scr, sems = refs[pos:pos + n_scr], refs[pos + n_scr:]
        ids = [pl.program_id(k) for k in range(len(grid))]
        first, last = ids[0] == 0, ids[0] == grid[0] - 1
        for k in range(1, len(grid)):
            first, last = first & (ids[k] == 0), last & (ids[k] == grid[k] - 1)

        @pl.when(first)
        def _():
            rider.start(r_ins, r_outs, *sems)

        body(*ins, *outs, *scr)

        @pl.when(last)
        def _():
            rider.finish(r_ins, r_outs, *sems)

    outs = _pallas(
        wrapped, name=name, grid=grid, in_specs=in_specs + [HBM_SPEC] * r_in, out_specs=out_specs + [HBM_SPEC] * r_out,
        out_shape=out_shape + rider.out_shapes, scratch_shapes=scratch_shapes + rider.scratch(),
        input_output_aliases={**aliases, **{n_in + i: n_out + o for i, o in rider.aliases.items()}}, compiler_params=params,
    )(*args, *rider.inputs)
    return list(outs[:n_out]), list(outs[n_out:])


def _stage_weights(shards):
    n = len(shards)

    def body(*refs):
        ins, outs, stages, sems = refs[:n], refs[n:2 * n], refs[2 * n:3 * n], refs[3 * n]
        x, y, _, _ = _place()
        copies = []
        for t in range(n):
            stages[t][...] = ins[t][...].astype(BF16)
            copies.append(pltpu.make_async_copy(stages[t], outs[t].at[2 * x + y], sems.at[t]))
            copies[-1].start()
        for cp in copies:
            cp.wait()

    assert sum(s.size * 6 for s in shards) <= (CALL_VMEM_MIB - 8) * MIB
    return _pallas(
        body, name="stage_weights", in_specs=[VMEM_SPEC] * n, out_specs=[HBM_SPEC] * n,
        out_shape=[jax.ShapeDtypeStruct((N_CHIPS,) + s.shape, BF16) for s in shards],
        scratch_shapes=[pltpu.VMEM(s.shape, BF16) for s in shards] + [pltpu.SemaphoreType.DMA((n,))],
        compiler_params=pltpu.CompilerParams(vmem_limit_bytes=SMALL_VMEM_MIB * MIB),
    )(*shards)


def _gather(buffers, stage="both", part=(0, 1)):
    n = len(buffers)
    halves = [b.shape[1] // part[1] // 2 for b in buffers]

    def half_of(outs, t, chip, which):
        return outs[t].at[chip, pl.ds((2 * part[0] + which) * halves[t], halves[t]), :]

    def copy(outs, sems, t, k, chip, which, to):
        rows = half_of(outs, t, chip, which)
        return pltpu.make_async_remote_copy(src_ref=rows, dst_ref=rows, send_sem=sems[0].at[6 * t + k],
                                            recv_sem=sems[1].at[6 * t + k], device_id=to, device_id_type=MESH)

    def to_chips(outs, sems, what):
        x, y, c, chips = _place()
        for t in range(n):
            for j, (px, py) in enumerate(chips):
                if what == "start":
                    copy(outs, sems, t, j, 2 * x + y, c, (px, py, c)).start()
                else:
                    copy(outs, sems, t, j, 2 * px + py, c, (px, py, c)).wait_recv()
                    copy(outs, sems, t, j, 2 * x + y, c, (px, py, c)).wait_send()

    def to_sibling(outs, sems, what):
        x, y, c, chips = _place()
        for t in range(n):
            for j, (px, py) in enumerate(chips):
                if what == "start":
                    copy(outs, sems, t, 3 + j, 2 * px + py, c, (x, y, 1 - c)).start()
                else:
                    copy(outs, sems, t, 3 + j, 2 * px + py, 1 - c, (x, y, 1 - c)).wait_recv()
                    copy(outs, sems, t, 3 + j, 2 * px + py, c, (x, y, 1 - c)).wait_send()

    def start(ins, outs, *sems):
        (to_sibling if stage == "pair" else to_chips)(outs, sems, "start")

    def finish(ins, outs, *sems):
        if stage != "pair":
            to_chips(outs, sems, "finish")
        if stage == "both":
            to_sibling(outs, sems, "start")
        if stage != "chips":
            to_sibling(outs, sems, "finish")

    return _Exchange(buffers, [jax.ShapeDtypeStruct(b.shape, b.dtype) for b in buffers], 6 * n, start, finish,
                     aliases={t: t for t in range(n)})


def _pair_exchange(grads):
    n = len(grads)
    halves = [g.shape[1] // 2 for g in grads]

    def copies(ins, outs, send_sems, recv_sems):
        x, y, c, _ = _place()
        return [pltpu.make_async_remote_copy(
            src_ref=ins[t].at[:, pl.ds((1 - c) * halves[t], halves[t]), :], dst_ref=outs[t],
            send_sem=send_sems.at[t], recv_sem=recv_sems.at[t], device_id=(x, y, 1 - c), device_id_type=MESH)
            for t in range(n)]

    def start(*refs):
        for cp in copies(*refs):
            cp.start()

    def finish(*refs):
        for cp in copies(*refs):
            cp.wait()

    return _Exchange(grads, [jax.ShapeDtypeStruct((N_CHIPS, h, g.shape[2]), F32) for g, h in zip(grads, halves)], n,
                     start, finish)


def _row_tile(rows):
    return max(t for t in range(16, 257, 16) if rows % t == 0)


def _pair_add(grad, other, place, name):
    _, rows, cols = grad.shape
    rh = rows // 2
    tr = _row_tile(rh)
    nb = rh // tr

    def body(p_ref, g_ref, a_ref, wire_ref, own_ref):
        s = g_ref[...] + a_ref[...]
        wire_ref[...] = s.astype(BF16)

        @pl.when(pl.program_id(1) == p_ref[1])
        def _():
            own_ref[...] = s

    blk = (None, tr, cols)
    return _pallas(
        body, name=name,
        grid_spec=pltpu.PrefetchScalarGridSpec(
            num_scalar_prefetch=1, grid=(nb, N_CHIPS),
            in_specs=[pl.BlockSpec(blk, lambda i, j, p: (j, p[0] * nb + i, 0)), pl.BlockSpec(blk, lambda i, j, p: (j, i, 0))],
            out_specs=[pl.BlockSpec(blk, lambda i, j, p: (j, i, 0)), pl.BlockSpec((tr, cols), lambda i, j, p: (i, 0))]),
        out_shape=[jax.ShapeDtypeStruct((N_CHIPS, rh, cols), BF16), jax.ShapeDtypeStruct((rh, cols), F32)],
        compiler_params=_params(("arbitrary", "arbitrary"), 32, small=True),
    )(place, grad, other)


def _chip_exchange(wires):
    n = len(wires)

    def copies(ins, outs, send_sems, recv_sems):
        x, y, c, chips = _place()
        return [pltpu.make_async_remote_copy(
            src_ref=ins[t].at[2 * px + py], dst_ref=outs[t].at[j], send_sem=send_sems.at[3 * t + j],
            recv_sem=recv_sems.at[3 * t + j], device_id=(px, py, c), device_id_type=MESH)
            for t in range(n) for j, (px, py) in enumerate(chips)]

    def start(*refs):
        for cp in copies(*refs):
            cp.start()

    def finish(*refs):
        for cp in copies(*refs):
            cp.wait()

    return _Exchange(wires, [jax.ShapeDtypeStruct((3,) + w.shape[1:], BF16) for w in wires], 3 * n, start, finish)


def _chip_add(own, arrived, place, name):
    rh, cols = own.shape
    tr = _row_tile(rh)
    nb = rh // tr

    def body(p_ref, s_ref, b0, b1, b2, o_ref):
        o_ref[...] = ((s_ref[...] + b0[...].astype(F32)) + b1[...].astype(F32)) + b2[...].astype(F32)

    blk = (None, tr, cols)
    return _pallas(
        body, name=name,
        grid_spec=pltpu.PrefetchScalarGridSpec(
            num_scalar_prefetch=1, grid=(nb,),
            in_specs=[pl.BlockSpec((tr, cols), lambda i, p: (i, 0)), pl.BlockSpec(blk, lambda i, p: (0, i, 0)),
                      pl.BlockSpec(blk, lambda i, p: (1, i, 0)), pl.BlockSpec(blk, lambda i, p: (2, i, 0))],
            out_specs=pl.BlockSpec((tr, cols), lambda i, p: (p[0] * nb + i, 0))),
        out_shape=jax.ShapeDtypeStruct((2 * rh, cols), F32),
        compiler_params=_params(("arbitrary",), 32, small=True),
    )(place, own, arrived, arrived, arrived)


def _pair_share(halves):
    n = len(halves)
    rhs = [h.shape[0] // 2 for h in halves]

    def copy(outs, send_sems, recv_sems, t, which):
        x, y, c, _ = _place()
        rows = outs[t].at[pl.ds(which * rhs[t], rhs[t]), :]
        return pltpu.make_async_remote_copy(src_ref=rows, dst_ref=rows, send_sem=send_sems.at[t], recv_sem=recv_sems.at[t],
                                            device_id=(x, y, 1 - c), device_id_type=MESH)

    def start(ins, outs, send_sems, recv_sems):
        c = lax.axis_index("c")
        for t in range(n):
            copy(outs, send_sems, recv_sems, t, c).start()

    def finish(ins, outs, send_sems, recv_sems):
        c = lax.axis_index("c")
        for t in range(n):
            copy(outs, send_sems, recv_sems, t, c).wait_send()
            copy(outs, send_sems, recv_sems, t, 1 - c).wait_recv()

    return _Exchange(halves, [jax.ShapeDtypeStruct(h.shape, F32) for h in halves], n, start, finish,
                     aliases={t: t for t in range(n)})


class _GradReduction:
    def __init__(self, grads, place, tag):
        self.names, self.grads, self.place, self.tag = list(grads), grads, place, tag

    def pair_exchange(self):
        return _pair_exchange([self.grads[n] for n in self.names])

    def chip_exchange(self, others):
        sums = [_pair_add(self.grads[n], o, self.place, f"{self.tag}_pair_add_{n}") for n, o in zip(self.names, others)]
        self.owns = [own for _, own in sums]
        return _chip_exchange([wire for wire, _ in sums])

    def pair_share(self, arrived):
        return _pair_share([_chip_add(own, arr, self.place, f"{self.tag}_chip_add_{n}")
                            for n, own, arr in zip(self.names, self.owns, arrived)])

    def result(self, shared):
        return dict(zip(self.names, shared))


def _all_reduce_small(p):
    rows, lanes = p.shape
    half = rows // 2

    def body(p_ref, o_ref, sib, sums, send_sems, recv_sems):
        x, y, c, chips = _place()
        mine, sibling = 2 * x + y, (x, y, 1 - c)
        swap = pltpu.make_async_remote_copy(src_ref=p_ref, dst_ref=sib, send_sem=send_sems.at[0], recv_sem=recv_sems.at[0],
                                            device_id=sibling, device_id_type=MESH)
        swap.start()
        swap.wait()
        sums[mine] = p_ref[...] + sib[...]

        def copy(k, chip, which, to):
            part = sums.at[chip, pl.ds(which * half, half), :]
            return pltpu.make_async_remote_copy(src_ref=part, dst_ref=part, send_sem=send_sems.at[k], recv_sem=recv_sems.at[k],
                                                device_id=to, device_id_type=MESH)

        for j, (px, py) in enumerate(chips):
            copy(1 + j, mine, c, (px, py, c)).start()
        for j, (px, py) in enumerate(chips):
            copy(1 + j, 2 * px + py, c, (px, py, c)).wait_recv()
            copy(4 + j, 2 * px + py, c, sibling).start()
        for j, (px, py) in enumerate(chips):
            copy(4 + j, 2 * px + py, 1 - c, sibling).wait_recv()
        for j, (px, py) in enumerate(chips):
            copy(1 + j, mine, c, (px, py, c)).wait_send()
            copy(4 + j, 2 * px + py, c, sibling).wait_send()
        o_ref[...] = ((sums[0] + sums[1]) + sums[2]) + sums[3]

    return _pallas(
        body, name="small_all_reduce", in_specs=[VMEM_SPEC], out_specs=VMEM_SPEC,
        out_shape=jax.ShapeDtypeStruct((rows, lanes), F32),
        scratch_shapes=[pltpu.VMEM((rows, lanes), F32), pltpu.VMEM((N_CHIPS, rows, lanes), F32),
                        pltpu.SemaphoreType.DMA((7,)), pltpu.SemaphoreType.DMA((7,))],
        compiler_params=pltpu.CompilerParams(vmem_limit_bytes=SMALL_VMEM_MIB * MIB),
    )(p)


BIG = ("w_in", "w_branch_attn", "w_branch_gmlp", "w_out", "w_mlp_in", "w_mlp_out")
COLUMN_SHARDED = ("w_branch_attn", "w_branch_gmlp", "w_mlp_in")
SMALL = ("norm_pre_mix", "w_spatial", "b_spatial", "ln_v_gain", "ln_v_bias", "norm_post_mix", "norm_pre_mlp", "norm_post_mlp")
ORDER = ("norm_pre_mix", "w_in", "w_spatial", "b_spatial", "ln_v_gain", "ln_v_bias", "w_branch_attn", "w_branch_gmlp",
         "w_out", "norm_post_mix", "norm_pre_mlp", "w_mlp_in", "w_mlp_out", "norm_post_mlp")


def _full_weight(name, gathered):
    if name in COLUMN_SHARDED:
        return jnp.transpose(gathered, (1, 0, 2)).reshape(gathered.shape[1], -1)
    return gathered.reshape(-1, gathered.shape[2])


def _rows8(a):
    a = a.reshape(-1, 128)
    pad = (-a.shape[0]) % 8
    return jnp.pad(a, ((0, pad), (0, 0))) if pad else a


def _qkv_columns(group):
    return [(sec * ATTN_W + group * GROUP_W, sec * ATTN_W + (group + 1) * GROUP_W) for sec in range(3)]


def _device_step(x, target, small, shards, place):
    seq = x.shape[0]
    g0, g1, g2, g3 = small["norm_pre_mix"], small["norm_post_mix"], small["norm_pre_mlp"], small["norm_post_mlp"]
    w_sp = small["w_spatial"]
    b_col = small["b_spatial"].reshape(GMLP_GROUPS, CHUNK, 1)
    ln_g, ln_b = small["ln_v_gain"], small["ln_v_bias"]

    staged = _stage_weights(shards)
    tables, w_in = _rope_tables(seq, rider=_gather(staged[:1], part=(0, 2)))
    h, (w_in,) = _norm_in(x, g0, rider=_gather(w_in, part=(1, 2)))
    w_in = _full_weight("w_in", w_in)
    (*qkv, rest), landed = _in_proj(h[0], w_in, *tables[1], rider=_gather(staged[1:], "chips"))

    o_l, gathered = _attn_fwd(qkv[0], DILATIONS[0], rider=_gather(landed, "pair"))
    full = {n: _full_weight(n, gw) for n, gw in zip(BIG[1:], gathered)}
    for g in range(1, N_GROUPS):
        o_l.extend(_attn_fwd(qkv[g], DILATIONS[g])[0])
    (*ya_l, yg, mg, y, x1), _ = _mix_fwd(o_l, rest, x, w_sp, b_col, ln_g, ln_b, full["w_branch_attn"],
                                        full["w_branch_gmlp"], full["w_out"], g1)
    ya, lse = ya_l[0::2], ya_l[1::2]
    h2, a, dy2, dout, loss8, dg3 = _mlp_fwd(x1, g2, g3, full["w_mlp_in"], full["w_mlp_out"], target)
    d_wmo, _ = _tn_matmul(a, dy2, "grad_w_mlp_out", 1024, 1024, square_a=True)
    mlp_out = _GradReduction({"w_mlp_out": d_wmo.reshape(N_CHIPS, D_FF // N_CHIPS, D_MODEL)}, place, "mlp_out")
    (dap, dx1, dy, dg2, dg1), riding = _mlp_bwd(dy2, a, full["w_mlp_out"], full["w_mlp_in"], dout, x1, y, g2, g1,
                                                 rider=mlp_out.pair_exchange())
    d_wmi, riding = _tn_matmul(h2, dap, "grad_w_mlp_in", 1024, 1024, column_shards=True,
                               rider=mlp_out.chip_exchange(riding))
    mlp_in = _GradReduction({"w_mlp_in": d_wmi}, place, "mlp_in")
    (*dya, drest, d_wout, d_wba, d_wbg, d_wsp, d_bb, d_lg, d_lb), riding = _mix_bwd(
        dy, ya[0], yg, mg, rest, full["w_out"], full["w_branch_attn"], full["w_branch_gmlp"], w_sp, b_col, ln_g, ln_b,
        rider=_together(mlp_out.pair_share(riding), mlp_in.pair_exchange()))
    reduced = mlp_out.result(riding[:1])
    mix = _GradReduction({"w_branch_attn": d_wba, "w_branch_gmlp": d_wbg,
                          "w_out": d_wout.reshape(N_CHIPS, D_MODEL // N_CHIPS, D_MODEL)}, place, "mix")
    attn = lambda g, rider: _attn_bwd(qkv[g], dya[g], ya[g], lse[g], *tables[DILATIONS[g]], DILATIONS[g], rider=rider)
    dqkv0, riding = attn(0, _together(mlp_in.chip_exchange(riding[1:]), mix.pair_exchange()))
    dqkv1, riding = attn(1, _together(mlp_in.pair_share(riding[:1]), mix.chip_exchange(riding[1:])))
    reduced.update(mlp_in.result(riding[:1]))
    dqkv2, riding = attn(2, mix.pair_share(riding[1:]))
    reduced.update(mix.result(riding))
    dqkv = [dqkv0, dqkv1, dqkv2]

    d_qkv = [_tn_matmul_residue(dqkv[g], h[g], dil, f"grad_w_in_qkv{g}") for g, dil in enumerate(DILATIONS)]
    d_rest, _ = _tn_matmul(drest, h[0], "grad_w_in_rest", 1024, 1024)
    d_win = jnp.concatenate([d_qkv[g][s * GROUP_W:(s + 1) * GROUP_W] for s in range(3) for g in range(N_GROUPS)]
                            + [d_rest], axis=0)
    first = _GradReduction({"w_in": d_win.reshape(N_CHIPS, IN_W // N_CHIPS, D_MODEL)}, place, "w_in")
    w_qkv = [jnp.concatenate([w_in[lo:hi] for lo, hi in _qkv_columns(g)], axis=0) for g in range(N_GROUPS)]
    w_rest = w_in[QKV_W:]
    tiles = seq // IN_PROJ_BWD_TM
    so_far = (lax.empty((seq, D_MODEL), F32), jnp.zeros((1, D_MODEL), F32))
    in_bwd = lambda so_far, span, rider: _in_proj_bwd(dqkv, drest, w_qkv, w_rest, x, dx1, g0, so_far, span, rider=rider)
    so_far, riding = in_bwd(so_far, (0, 3 * tiles // 8), first.pair_exchange())
    (grad_x, dg0), riding = in_bwd(so_far, (3 * tiles // 8, 5 * tiles // 8), first.chip_exchange(riding))
    reduced.update(first.result(_run_exchange(first.pair_share(riding), "w_in_pair_share")))
    little = {"norm_pre_mix": dg0, "w_spatial": d_wsp, "b_spatial": d_bb[:, :, 0], "ln_v_gain": d_lg, "ln_v_bias": d_lb,
              "norm_post_mix": dg1, "norm_pre_mlp": dg2, "norm_post_mlp": dg3}
    return loss8, grad_x, reduced, little


def kernel(x, norm_pre_mix, w_in, w_spatial, b_spatial, ln_v_gain, ln_v_bias, w_branch_attn, w_branch_gmlp, w_out, norm_post_mix, norm_pre_mlp, w_mlp_in, w_mlp_out, norm_post_mlp, loss_target, m_norm_pre_mix, m_w_in, m_w_spatial, m_b_spatial, m_ln_v_gain, m_ln_v_bias, m_w_branch_attn, m_w_branch_gmlp, m_w_out, m_norm_post_mix, m_norm_pre_mlp, m_w_mlp_in, m_w_mlp_out, m_norm_post_mlp, v_norm_pre_mix, v_w_in, v_w_spatial, v_b_spatial, v_ln_v_gain, v_ln_v_bias, v_w_branch_attn, v_w_branch_gmlp, v_w_out, v_norm_post_mix, v_norm_pre_mlp, v_w_mlp_in, v_w_mlp_out, v_norm_post_mlp):
    given = dict(norm_pre_mix=norm_pre_mix, w_in=w_in, w_spatial=w_spatial, b_spatial=b_spatial, ln_v_gain=ln_v_gain,
                 ln_v_bias=ln_v_bias, w_branch_attn=w_branch_attn, w_branch_gmlp=w_branch_gmlp, w_out=w_out,
                 norm_post_mix=norm_post_mix, norm_pre_mlp=norm_pre_mlp, w_mlp_in=w_mlp_in, w_mlp_out=w_mlp_out,
                 norm_post_mlp=norm_post_mlp)
    moments_m = dict(norm_pre_mix=m_norm_pre_mix, w_in=m_w_in, w_spatial=m_w_spatial, b_spatial=m_b_spatial,
                     ln_v_gain=m_ln_v_gain, ln_v_bias=m_ln_v_bias, w_branch_attn=m_w_branch_attn,
                     w_branch_gmlp=m_w_branch_gmlp, w_out=m_w_out, norm_post_mix=m_norm_post_mix,
                     norm_pre_mlp=m_norm_pre_mlp, w_mlp_in=m_w_mlp_in, w_mlp_out=m_w_mlp_out, norm_post_mlp=m_norm_post_mlp)
    moments_v = dict(norm_pre_mix=v_norm_pre_mix, w_in=v_w_in, w_spatial=v_w_spatial, b_spatial=v_b_spatial,
                     ln_v_gain=v_ln_v_gain, ln_v_bias=v_ln_v_bias, w_branch_attn=v_w_branch_attn,
                     w_branch_gmlp=v_w_branch_gmlp, w_out=v_w_out, norm_post_mix=v_norm_post_mix,
                     norm_pre_mlp=v_norm_pre_mlp, w_mlp_in=v_w_mlp_in, w_mlp_out=v_w_mlp_out, norm_post_mlp=v_norm_post_mlp)
    cx, cy, cc = lax.axis_index("x"), lax.axis_index("y"), lax.axis_index("c")

    shards = [given[n][0].T if n == "w_in" else given[n][0] for n in BIG]
    small = {n: given[n][0] if given[n].ndim > 2 else given[n] for n in SMALL}
    place = jnp.stack([cc, 2 * cx + cy]).astype(jnp.int32)
    loss8, grad_x, grad_shard, grads = _device_step(x[0], loss_target[0], small, shards, place)

    packed = jnp.concatenate([_rows8(grads[n]) for n in SMALL] + [loss8], axis=0)
    summed = _all_reduce_small(packed)
    loss = summed[packed.shape[0] - loss8.shape[0], 0]
    row = 0
    for n in SMALL:
        shape = given[n][0].shape
        cnt = -(-(given[n][0].size // 128) // 8) * 8
        grad_shard[n] = summed[row:row + given[n][0].size // 128].reshape(shape)
        row += cnt

    grad_out, deltas, new_m, new_v = {}, {}, {}, {}
    for n in ORDER:
        shape = given[n].shape
        if n == "w_in":
            outs = _adamw(given[n][0].T, grad_shard[n], moments_m[n][0].T, moments_v[n][0].T, "adamw_" + n)
            outs = [o.T for o in outs]
        else:
            two_d = (-1, shape[-1])
            outs = _adamw(given[n].reshape(two_d), grad_shard[n].reshape(two_d), moments_m[n].reshape(two_d),
                          moments_v[n].reshape(two_d), "adamw_" + n)
        grad_out[n], deltas[n], new_m[n], new_v[n] = [o.reshape(shape) for o in outs]
    return (loss, grad_x[None], *[grad_out[n] for n in ORDER], *[deltas[n] for n in ORDER], *[new_m[n] for n in ORDER],
            *[new_v[n] for n in ORDER])
```

```python
import math

import jax
import jax.numpy as jnp
from jax import lax
from jax.experimental import pallas as pl
from jax.experimental.pallas import tpu as pltpu

F32 = jnp.float32
BF16 = jnp.bfloat16
MESH = pl.DeviceIdType.MESH

D_MODEL = 1024
HEAD_DIM = 64
HEADS_PER_GROUP = 4
GROUP_W = HEADS_PER_GROUP * HEAD_DIM
DILATIONS = (1, 4, 16)
N_GROUPS = len(DILATIONS)
ATTN_W = N_GROUPS * GROUP_W
QKV_W = 3 * ATTN_W
GMLP_W = 512
GMLP_GROUPS = 4
CHUNK = 128
REST_W = 2 * GMLP_W + 2 * D_MODEL
IN_W = QKV_W + REST_W
D_FF = 4096
QBLK = 128
ROPE_THETA = 10000.0
EPS = 1e-6
NEG = -1e30
SCALE = HEAD_DIM ** -0.5
N_CHIPS = 4

ADAM_LR = 0.001
ADAM_B1 = 0.9
ADAM_B2 = 0.999
ADAM_EPS = 1e-08
ADAM_WD = 0.01
ADAM_STEP = 10

MIB = 1024 * 1024
HBM_SPEC = pl.BlockSpec(memory_space=pltpu.HBM)
VMEM_SPEC = pl.BlockSpec(memory_space=pltpu.VMEM)


MLP_FWD_TM = 512
MLP_TM = 256


CALL_VMEM_MIB = 56
SMALL_VMEM_MIB = 32


def _params(semantics, vmem_mib, small=False):
    assert vmem_mib <= CALL_VMEM_MIB
    return pltpu.CompilerParams(dimension_semantics=semantics,
                                vmem_limit_bytes=(SMALL_VMEM_MIB if small else CALL_VMEM_MIB) * MIB)


def _in_hbm(a):
    return pltpu.with_memory_space_constraint(a, pltpu.HBM) if a.size * a.dtype.itemsize >= MIB else a


def _pallas(body, **kwargs):
    return pl.pallas_call(body, **kwargs)


def _resident(shape):
    return pl.BlockSpec(shape, lambda *_: (0,) * len(shape), pipeline_mode=pl.Buffered(1))


def _dot(a, b):
    return jnp.dot(a, b, preferred_element_type=F32)


def _dot_nt(a, b):
    return lax.dot_general(a, b, (((1,), (1,)), ((), ())), preferred_element_type=F32)


def _dot_tn(a, b):
    return lax.dot_general(a, b, (((0,), (0,)), ((), ())), preferred_element_type=F32)


_GELU_C = math.sqrt(2.0 / math.pi)


def _gelu(x):
    return x * (0.5 * (1.0 + jnp.tanh(_GELU_C * (x + 0.044715 * (x * x * x)))))


def _gelu_grad(x):
    t = jnp.tanh(_GELU_C * (x + 0.044715 * (x * x * x)))
    return 0.5 * (1.0 + t) + 0.5 * x * (1.0 - t * t) * (_GELU_C * (1.0 + 3.0 * 0.044715 * (x * x)))


def _rsqrt_ms(v):
    return lax.rsqrt(jnp.mean(v * v, axis=-1, keepdims=True) + EPS)


def _rmsnorm_bwd(dn, src, gain):
    r = _rsqrt_ms(src)
    t = gain * dn
    dgain = jnp.sum(dn * (src * r), axis=0, keepdims=True)
    dsrc = r * t - src * ((r * r * r) * jnp.mean(t * src, axis=-1, keepdims=True))
    return dsrc, dgain


def _rot_half(v):
    w = v.shape[-1]
    lane = lax.broadcasted_iota(jnp.int32, v.shape, v.ndim - 1)
    return jnp.where((lane % HEAD_DIM) < HEAD_DIM // 2, pltpu.roll(v, w - HEAD_DIM // 2, v.ndim - 1),
                     pltpu.roll(v, HEAD_DIM // 2, v.ndim - 1))


def _head_masks(shape):
    lane = lax.broadcasted_iota(jnp.int32, shape, 1)
    return [(lane >= h * HEAD_DIM) & (lane < (h + 1) * HEAD_DIM) for h in range(HEADS_PER_GROUP)]


def _head_stack(block, hmask):
    zero = jnp.zeros((), block.dtype)
    return jnp.concatenate([jnp.where(hm, block, zero) for hm in hmask], axis=0)


LANES = 128


def _put_residue(slab, val, out_ref, dil, width, col0):
    tm, w = val.shape
    if dil == 1:
        out_ref[:, col0:col0 + w] = val.astype(out_ref.dtype)
        return
    for k in range(w // LANES):
        slab[k] = val[:, k * LANES:(k + 1) * LANES]
    for r in range(dil):
        for k in range(w // LANES):
            c = r * width + col0 + k * LANES
            out_ref[:, c:c + LANES] = slab[k, pl.ds(r, tm // dil, stride=dil), :].astype(out_ref.dtype)


def _get_tokens(slab, in_ref, dil, width, col0, w):
    if dil == 1:
        return in_ref[:, col0:col0 + w].astype(F32)
    rows = in_ref.shape[0]
    for r in range(dil):
        for k in range(w // LANES):
            c = r * width + col0 + k * LANES
            slab[k, pl.ds(r, rows, stride=dil), :] = in_ref[:, c:c + LANES].astype(F32)
    return jnp.concatenate([slab[k] for k in range(w // LANES)], axis=1)


def _prepare(x, g0, rider=None):
    seq = x.shape[0]
    half = HEAD_DIM // 2
    inv_freq = ROPE_THETA ** (-jnp.arange(half, dtype=F32) / half)
    freq = jnp.tile(inv_freq, LANES // half).reshape(1, LANES)
    tm = 256

    def body(x_ref, g_ref, f_ref, *refs):
        h_refs, tabs, slab = refs[:N_GROUPS], refs[N_GROUPS:3 * N_GROUPS], refs[-1]
        xv = x_ref[...]
        hf = (xv * _rsqrt_ms(xv)) * g_ref[...]
        for g, dil in enumerate(DILATIONS):
            _put_residue(slab, hf, h_refs[g], dil, D_MODEL, 0)
        row = lax.broadcasted_iota(jnp.int32, (tm, LANES), 0) + pl.program_id(0) * tm
        lane = lax.broadcasted_iota(jnp.int32, (tm, LANES), 1)
        ang = row.astype(F32) * f_ref[...]
        cos = jnp.cos(ang)
        sin = jnp.where((lane % HEAD_DIM) < half, -jnp.sin(ang), jnp.sin(ang))
        for i, dil in enumerate(DILATIONS):
            for tab, val in ((tabs[2 * i], cos), (tabs[2 * i + 1], sin)):
                slab[0] = val
                for r in range(dil):
                    piece = slab[0, pl.ds(r, tm // dil, stride=dil), :] if dil > 1 else val
                    for k in range(GROUP_W // LANES):
                        tab[:, r * GROUP_W + k * LANES:r * GROUP_W + (k + 1) * LANES] = piece

    outs, riding = _call(
        body, name="prepare", grid=(seq // tm,),
        in_specs=[pl.BlockSpec((tm, D_MODEL), lambda i: (i, 0)), pl.BlockSpec((1, D_MODEL), lambda i: (0, 0)),
                  pl.BlockSpec((1, LANES), lambda i: (0, 0))],
        out_specs=[pl.BlockSpec((tm // d, d * D_MODEL), lambda i: (i, 0)) for d in DILATIONS]
        + [pl.BlockSpec((tm // d, d * GROUP_W), lambda i: (i, 0)) for d in DILATIONS for _ in range(2)],
        out_shape=[jax.ShapeDtypeStruct((seq // d, d * D_MODEL), BF16) for d in DILATIONS]
        + [jax.ShapeDtypeStruct((seq // d, d * GROUP_W), F32) for d in DILATIONS for _ in range(2)],
        scratch_shapes=[pltpu.VMEM((D_MODEL // LANES, tm, LANES), F32)],
        params=_params(("arbitrary",), 32), args=(x, g0, freq), rider=rider)
    tabs = outs[N_GROUPS:]
    return outs[:N_GROUPS], {d: (tabs[2 * i], tabs[2 * i + 1]) for i, d in enumerate(DILATIONS)}, riding


def _in_proj(h, w_in, cos_t, sin_t, rider=None):
    seq = h.shape[0]
    tm, tn = 512, GROUP_W
    n_qk = 2 * ATTN_W // tn
    n_qkv = QKV_W // tn

    def body(h_ref, w_ref, cos_ref, sin_ref, *refs):
        qkv_refs, rest_ref, slab = refs[:N_GROUPS], refs[N_GROUPS], refs[-1]
        hb = h_ref[...]
        cos, sin = cos_ref[...], sin_ref[...]
        for j in range(IN_W // tn):
            p = _dot_nt(hb, w_ref[j * tn:(j + 1) * tn, :])
            if j < n_qkv:
                if j < n_qk:
                    p = p * cos + _rot_half(p) * sin
                section, g = divmod(j, N_GROUPS)
                _put_residue(slab, p, qkv_refs[g], DILATIONS[g], 3 * GROUP_W, section * GROUP_W)
            else:
                rest_ref[:, (j - n_qkv) * tn:(j - n_qkv + 1) * tn] = p.astype(BF16)

    return _call(
        body, name="in_proj", grid=(seq // tm,),
        in_specs=[pl.BlockSpec((tm, D_MODEL), lambda i: (i, 0)),
                  _resident((IN_W, D_MODEL)),
                  pl.BlockSpec((tm, GROUP_W), lambda i: (i, 0)),
                  pl.BlockSpec((tm, GROUP_W), lambda i: (i, 0))],
        out_specs=[pl.BlockSpec((tm // d, d * 3 * GROUP_W), lambda i: (i, 0)) for d in DILATIONS]
        + [pl.BlockSpec((tm, REST_W), lambda i: (i, 0))],
        out_shape=[jax.ShapeDtypeStruct((seq // d, d * 3 * GROUP_W), BF16) for d in DILATIONS]
        + [jax.ShapeDtypeStruct((seq, REST_W), BF16)],
        scratch_shapes=[pltpu.VMEM((GROUP_W // LANES, tm, LANES), F32)],
        params=_params(("arbitrary",), 48), args=(h, w_in, cos_t, sin_t), rider=rider)


def _band_masks():
    qi = lax.broadcasted_iota(jnp.int32, (QBLK, QBLK), 0)
    kj = lax.broadcasted_iota(jnp.int32, (QBLK, QBLK), 1)
    return kj <= qi, kj >= qi


def _attn_tile(length):
    return min(512, length)


def _attn_fwd(qkv, dil, rider=None):
    length = qkv.shape[0]
    tq = _attn_tile(length)
    nsub = tq // QBLK
    nblk = length // tq

    def body(q_ref, k_ref, v_ref, kp_ref, vp_ref, o_ref, l_ref):
        n = pl.program_id(1)
        mask_c, mask_p0 = _band_masks()
        hmask = _head_masks((QBLK, GROUP_W))
        zero = jnp.zeros((), BF16)
        for b in range(nsub):
            rows = slice(b * QBLK, (b + 1) * QBLK)
            q = q_ref[rows, :]
            kc, vc = k_ref[rows, :], v_ref[rows, :]
            if b == 0:
                kp, vp = kp_ref[...], vp_ref[...]
                mask_p = mask_p0 & (n > 0)
            else:
                prow = slice((b - 1) * QBLK, b * QBLK)
                kp, vp = k_ref[prow, :], v_ref[prow, :]
                mask_p = mask_p0
            o_acc = jnp.zeros((QBLK, GROUP_W), F32)
            l_acc = jnp.zeros((QBLK, GROUP_W), F32)
            for h in range(HEADS_PER_GROUP):
                hm = hmask[h]
                sc = jnp.where(mask_c, _dot_nt(q, jnp.where(hm, kc, zero)) * SCALE, NEG)
                sp = jnp.where(mask_p, _dot_nt(q, jnp.where(hm, kp, zero)) * SCALE, NEG)
                m = jnp.maximum(jnp.max(sc, axis=-1, keepdims=True), jnp.max(sp, axis=-1, keepdims=True))
                pc, pp = jnp.exp(sc - m), jnp.exp(sp - m)
                den = jnp.sum(pc, axis=-1, keepdims=True) + jnp.sum(pp, axis=-1, keepdims=True)
                pv = _dot(pc.astype(BF16), jnp.where(hm, vc, zero)) + _dot(pp.astype(BF16), jnp.where(hm, vp, zero))
                o_acc = o_acc + pv / den
                l_acc = l_acc + jnp.where(hm, m + jnp.log(den), 0.0)
            o_ref[rows, :] = o_acc
            l_ref[rows, :] = l_acc

    cur = lambda sec: pl.BlockSpec((tq, GROUP_W), lambda r, n: (n, r * 3 + sec))
    prev = lambda sec: pl.BlockSpec((QBLK, GROUP_W), lambda r, n: (jnp.maximum(n * nsub - 1, 0), r * 3 + sec))
    return _call(
        body, name=f"attn_fwd_d{dil}", grid=(dil, nblk),
        in_specs=[cur(0), cur(1), cur(2), prev(1), prev(2)],
        out_specs=[pl.BlockSpec((tq, GROUP_W), lambda r, n: (n, r))] * 2,
        out_shape=[jax.ShapeDtypeStruct((length, dil * GROUP_W), F32)] * 2, scratch_shapes=[],
        params=_params(("arbitrary", "arbitrary"), 32), args=(qkv, qkv, qkv, qkv, qkv), rider=rider)


def _attn_bwd(qkv, dy, y, lse, cos_t, sin_t, dil, rider=None):
    length = qkv.shape[0]
    tq = _attn_tile(length)
    nsub = tq // QBLK
    nblk = length // tq

    def body(q_ref, k_ref, v_ref, kp_ref, vp_ref, qn_ref, dy_ref, y_ref, l_ref, dyn_ref, yn_ref, ln_ref,
             cos_ref, sin_ref, out_ref, dq_s, dk_s, dv_s):
        n = pl.program_id(1)
        mask_c, mask_p0 = _band_masks()
        hmask = _head_masks((QBLK, GROUP_W))
        sub = lambda ref, b: ref[b * QBLK:(b + 1) * QBLK, :]
        kbd = [_head_stack(kp_ref[...], hmask)] + [_head_stack(sub(k_ref, b), hmask) for b in range(nsub)]
        vbd = [_head_stack(vp_ref[...], hmask)] + [_head_stack(sub(v_ref, b), hmask) for b in range(nsub)]
        dq_s[...] = jnp.zeros(dq_s.shape, F32)

        def query_terms(q, dyv, yv, lv):
            prod = dyv * yv
            return dict(
                q=q, dy=dyv.astype(BF16), q_heads=[jnp.where(hm, q, jnp.zeros((), BF16)) for hm in hmask],
                dy_heads=[jnp.where(hm, dyv, 0.0).astype(BF16) for hm in hmask],
                delta=[jnp.sum(jnp.where(hm, prod, 0.0), axis=-1, keepdims=True) for hm in hmask],
                lse=[jnp.max(jnp.where(hm, lv, NEG), axis=-1, keepdims=True) for hm in hmask])

        queries = [query_terms(sub(q_ref, b), sub(dy_ref, b), sub(y_ref, b), sub(l_ref, b)) for b in range(nsub)]
        queries.append(query_terms(qn_ref[...], dyn_ref[...], yn_ref[...], ln_ref[...]))
        rows_of = lambda items: items[0] if len(items) == 1 else jnp.concatenate(items, axis=0)
        for kb in range(nsub + 1):
            seen = [(kb - 1, mask_c)] if kb >= 1 else []
            if kb == 0:
                seen.append((0, mask_p0 & (n > 0)))
            elif kb < nsub:
                seen.append((kb, mask_p0))
            else:
                seen.append((nsub, mask_p0 & (n < nblk - 1)))
            qs = [queries[b] for b, _ in seen]
            mask = rows_of([m for _, m in seen])
            s = _dot_nt(rows_of([t["q"] for t in qs]), kbd[kb]) * SCALE
            dp = _dot_nt(rows_of([t["dy"] for t in qs]), vbd[kb])
            ps, dss = [], []
            for h in range(HEADS_PER_GROUP):
                cols = slice(h * QBLK, (h + 1) * QBLK)
                p = jnp.exp(jnp.where(mask, s[:, cols] - rows_of([t["lse"][h] for t in qs]), NEG))
                ps.append(p.astype(BF16))
                dss.append((p * (dp[:, cols] - rows_of([t["delta"][h] for t in qs]))).astype(BF16))
            dq = _dot(jnp.concatenate(dss, axis=1), kbd[kb]) * SCALE
            for i, (b, _) in enumerate(seen):
                if b < nsub:
                    dq_s[b * QBLK:(b + 1) * QBLK, :] += dq[i * QBLK:(i + 1) * QBLK, :]
            if kb >= 1:
                krows = slice((kb - 1) * QBLK, kb * QBLK)
                head_rows = lambda key: jnp.concatenate([t[key][h] for h in range(HEADS_PER_GROUP) for t in qs], axis=0)
                dv_s[krows, :] = _dot_tn(jnp.concatenate(ps, axis=0), head_rows("dy_heads"))
                dk_s[krows, :] = _dot_tn(jnp.concatenate(dss, axis=0), head_rows("q_heads")) * SCALE
        cos, sin = cos_ref[...], sin_ref[...]
        dq, dk = dq_s[...], dk_s[...]
        out_ref[:, 0:GROUP_W] = (dq * cos - _rot_half(dq) * sin).astype(BF16)
        out_ref[:, GROUP_W:2 * GROUP_W] = (dk * cos - _rot_half(dk) * sin).astype(BF16)
        out_ref[:, 2 * GROUP_W:3 * GROUP_W] = dv_s[...].astype(BF16)

    cur = lambda sec: pl.BlockSpec((tq, GROUP_W), lambda r, n: (n, r * 3 + sec))
    prev = lambda sec: pl.BlockSpec((QBLK, GROUP_W), lambda r, n: (jnp.maximum(n * nsub - 1, 0), r * 3 + sec))
    nxt_q = pl.BlockSpec((QBLK, GROUP_W), lambda r, n: (jnp.minimum((n + 1) * nsub, nblk * nsub - 1), r * 3))
    tok = pl.BlockSpec((tq, GROUP_W), lambda r, n: (n, r))
    tok_next = pl.BlockSpec((QBLK, GROUP_W), lambda r, n: (jnp.minimum((n + 1) * nsub, nblk * nsub - 1), r))
    (out,), riding = _call(
        body, name=f"attn_bwd_d{dil}", grid=(dil, nblk),
        in_specs=[cur(0), cur(1), cur(2), prev(1), prev(2), nxt_q,
                  tok, tok, tok, tok_next, tok_next, tok_next, tok, tok],
        out_specs=[pl.BlockSpec((tq, 3 * GROUP_W), lambda r, n: (n, r))],
        out_shape=[jax.ShapeDtypeStruct((length, dil * 3 * GROUP_W), BF16)],
        scratch_shapes=[pltpu.VMEM((tq, GROUP_W), F32)] * 3,
        params=_params(("arbitrary", "arbitrary"), 32),
        args=(qkv, qkv, qkv, qkv, qkv, qkv, dy, y, lse, dy, y, lse, cos_t, sin_t), rider=rider)
    return out, riding


def _layernorm_stats(z):
    mu = jnp.mean(z, axis=-1, keepdims=True)
    zc = z - mu
    rstd = lax.rsqrt(jnp.mean(zc * zc, axis=-1, keepdims=True) + EPS)
    return zc * rstd, rstd


def _tril_mask():
    row = lax.broadcasted_iota(jnp.int32, (CHUNK, CHUNK), 0)
    col = lax.broadcasted_iota(jnp.int32, (CHUNK, CHUNK), 1)
    return col <= row


def _mix_fwd(o_l, rest, x, w_sp, b_col, ln_g, ln_b, w_ba, w_bg, w_out, g1, rider=None):
    seq = x.shape[0]
    tm = 256

    def body(o0, l0, o1, l1, o2, l2, up_ref, zp_ref, gap_ref, gbp_ref, x_ref, wsp_ref, bcol_ref, lg_ref, lb_ref,
             wba_ref, wbg_ref, wout_ref, g1_ref, ya0, lj0, ya1, lj1, ya2, lj2, yg_ref, mg_ref, y_ref, x1_ref, slab):
        outs = [_get_tokens(slab, o, d, GROUP_W, 0, GROUP_W) for o, d in zip((o0, o1, o2), DILATIONS)]
        lses = [_get_tokens(slab, l, d, GROUP_W, 0, GROUP_W) for l, d in zip((l0, l1, l2), DILATIONS)]
        m = jnp.maximum(jnp.maximum(lses[0], lses[1]), lses[2])
        es = [jnp.exp(l - m) for l in lses]
        tot = es[0] + es[1] + es[2]
        ya = (es[0] * outs[0] + es[1] * outs[1] + es[2] * outs[2]) / tot
        lj = m + jnp.log(tot)
        for ya_ref, lj_ref, d in zip((ya0, ya1, ya2), (lj0, lj1, lj2), DILATIONS):
            _put_residue(slab, ya, ya_ref, d, GROUP_W, 0)
            _put_residue(slab, lj, lj_ref, d, GROUP_W, 0)
        zhat, _ = _layernorm_stats(_gelu(zp_ref[...].astype(F32)))
        zln = (zhat * lg_ref[...] + lb_ref[...]).astype(BF16)
        u = _gelu(up_ref[...].astype(F32))
        tril = _tril_mask()
        for g in range(GMLP_GROUPS):
            wm = jnp.where(tril, wsp_ref[g], 0.0).astype(BF16)
            cols = slice(g * CHUNK, (g + 1) * CHUNK)
            for c in range(tm // CHUNK):
                rows = slice(c * CHUNK, (c + 1) * CHUNK)
                sz = _dot(wm, zln[rows, cols]) + bcol_ref[g]
                yg_ref[rows, cols] = (u[rows, cols] * sz).astype(BF16)
        a = _dot(ya.astype(BF16), wba_ref[...])
        bm = _dot(yg_ref[...], wbg_ref[...])
        merged = (jax.nn.sigmoid(gap_ref[...].astype(F32)) * a + jax.nn.sigmoid(gbp_ref[...].astype(F32)) * bm).astype(BF16)
        mg_ref[...] = merged
        yv = _dot(merged, wout_ref[...])
        y_ref[...] = yv
        x1_ref[...] = x_ref[...] + (yv * _rsqrt_ms(yv)) * g1_ref[...]

    tok = lambda w: pl.BlockSpec((tm, w), lambda i: (i, 0))
    res = lambda d: pl.BlockSpec((tm // d, d * GROUP_W), lambda i: (i, 0))
    full = lambda *s: pl.BlockSpec(s, lambda i: (0,) * len(s))
    res_specs = [res(d) for d in DILATIONS for _ in range(2)]
    return _call(
        body, name="mix_fwd", grid=(seq // tm,),
        in_specs=res_specs + [
            pl.BlockSpec((tm, GMLP_W), lambda i: (i, 0)), pl.BlockSpec((tm, GMLP_W), lambda i: (i, 1)),
            pl.BlockSpec((tm, D_MODEL), lambda i: (i, 1)), pl.BlockSpec((tm, D_MODEL), lambda i: (i, 2)),
            tok(D_MODEL), full(GMLP_GROUPS, CHUNK, CHUNK), full(GMLP_GROUPS, CHUNK, 1), full(1, GMLP_W), full(1, GMLP_W),
            full(GROUP_W, D_MODEL), full(GMLP_W, D_MODEL), full(D_MODEL, D_MODEL), full(1, D_MODEL)],
        out_specs=res_specs + [tok(GMLP_W), tok(D_MODEL), tok(D_MODEL), tok(D_MODEL)],
        out_shape=[jax.ShapeDtypeStruct((seq // d, d * GROUP_W), F32) for d in DILATIONS for _ in range(2)]
        + [jax.ShapeDtypeStruct((seq, GMLP_W), BF16), jax.ShapeDtypeStruct((seq, D_MODEL), BF16),
           jax.ShapeDtypeStruct((seq, D_MODEL), F32), jax.ShapeDtypeStruct((seq, D_MODEL), F32)],
        scratch_shapes=[pltpu.VMEM((GROUP_W // LANES, tm, LANES), F32)],
        params=_params(("arbitrary",), 48),
        args=(*o_l, rest, rest, rest, rest, x, w_sp, b_col, ln_g, ln_b, w_ba, w_bg, w_out, g1), rider=rider)


def _mlp_fwd(x1, g2, g3, w_mi, w_mo, target):
    seq = x1.shape[0]
    tm, tf = MLP_FWD_TM, 512

    def body(x1_ref, g2_ref, g3_ref, wmi_ref, wmo_ref, t_ref, h2_ref, a_ref, dy2_ref, dout_ref, loss_ref, dg3_ref, sq_s):
        @pl.when(pl.program_id(0) == 0)
        def _():
            loss_ref[...] = jnp.zeros(loss_ref.shape, F32)
            dg3_ref[...] = jnp.zeros(dg3_ref.shape, F32)

        xv = x1_ref[...]
        hb = ((xv * _rsqrt_ms(xv)) * g2_ref[...]).astype(BF16)
        h2_ref[...] = hb
        for j in range(D_FF // tf):
            cols = slice(j * tf, (j + 1) * tf)
            a = jnp.maximum(_dot(hb, wmi_ref[:, cols]), 0.0)
            a_ref[:, cols] = a.astype(BF16)
            sq_s[:, cols] = (a * a).astype(BF16)
        y2 = _dot(sq_s[...], wmo_ref[...])
        r3 = _rsqrt_ms(y2)
        out = xv + (y2 * r3) * g3_ref[...]
        diff = out - t_ref[...]
        tile_loss = 0.5 * jnp.sum(jnp.mean(diff * diff, axis=-1, keepdims=True), axis=0, keepdims=True)
        loss_ref[...] += jnp.broadcast_to(tile_loss, loss_ref.shape)
        dout = diff * (1.0 / D_MODEL)
        dout_ref[...] = dout
        dy2, dg3 = _rmsnorm_bwd(dout, y2, g3_ref[...])
        dy2_ref[...] = dy2.astype(BF16)
        dg3_ref[...] += dg3

    tok = lambda w: pl.BlockSpec((tm, w), lambda i: (i, 0))
    vec = pl.BlockSpec((1, D_MODEL), lambda i: (0, 0))
    return _pallas(
        body, name="mlp_fwd", grid=(seq // tm,),
        in_specs=[tok(D_MODEL), vec, vec, _resident((D_MODEL, D_FF)), _resident((D_FF, D_MODEL)), tok(D_MODEL)],
        out_specs=[tok(D_MODEL), tok(D_FF), tok(D_MODEL), tok(D_MODEL), pl.BlockSpec((8, 128), lambda i: (0, 0)), vec],
        out_shape=[jax.ShapeDtypeStruct((seq, D_MODEL), BF16), jax.ShapeDtypeStruct((seq, D_FF), BF16),
                   jax.ShapeDtypeStruct((seq, D_MODEL), BF16), jax.ShapeDtypeStruct((seq, D_MODEL), F32),
                   jax.ShapeDtypeStruct((8, 128), F32), jax.ShapeDtypeStruct((1, D_MODEL), F32)],
        scratch_shapes=[pltpu.VMEM((tm, D_FF), BF16)],
        compiler_params=_params(("arbitrary",), 56),
    )(*map(_in_hbm, (x1, g2, g3, w_mi, w_mo, target)))


def _mlp_bwd(dy2, a, w_mo, w_mi, dout, x1, y, g2, g1, rider=None):
    seq = x1.shape[0]
    tm, tf = MLP_TM, 512

    def body(dy2_ref, a_ref, wmo_ref, wmi_ref, dout_ref, x1_ref, y_ref, g2_ref, g1_ref,
             dap_ref, dx1_ref, dy_ref, dg2_ref, dg1_ref):
        @pl.when(pl.program_id(0) == 0)
        def _():
            dg2_ref[...] = jnp.zeros(dg2_ref.shape, F32)
            dg1_ref[...] = jnp.zeros(dg1_ref.shape, F32)

        dy2v = dy2_ref[...]
        for j in range(D_FF // tf):
            cols = slice(j * tf, (j + 1) * tf)
            da2 = _dot_nt(dy2v, wmo_ref[cols, :])
            dap_ref[:, cols] = (da2 * (2.0 * a_ref[:, cols].astype(F32))).astype(BF16)
        dh2 = _dot_nt(dap_ref[...], wmi_ref[...])
        dres, dg2 = _rmsnorm_bwd(dh2, x1_ref[...], g2_ref[...])
        dx1 = dout_ref[...] + dres
        dx1_ref[...] = dx1
        dg2_ref[...] += dg2
        dyv, dg1 = _rmsnorm_bwd(dx1, y_ref[...], g1_ref[...])
        dy_ref[...] = dyv.astype(BF16)
        dg1_ref[...] += dg1

    tok = lambda w: pl.BlockSpec((tm, w), lambda i: (i, 0))
    vec = pl.BlockSpec((1, D_MODEL), lambda i: (0, 0))
    return _call(
        body, name="mlp_bwd", grid=(seq // tm,),
        in_specs=[tok(D_MODEL), tok(D_FF), _resident((D_FF, D_MODEL)), _resident((D_MODEL, D_FF)),
                  tok(D_MODEL), tok(D_MODEL), tok(D_MODEL), vec, vec],
        out_specs=[tok(D_FF), tok(D_MODEL), tok(D_MODEL), vec, vec],
        out_shape=[jax.ShapeDtypeStruct((seq, D_FF), BF16), jax.ShapeDtypeStruct((seq, D_MODEL), F32),
                   jax.ShapeDtypeStruct((seq, D_MODEL), BF16), jax.ShapeDtypeStruct((1, D_MODEL), F32),
                   jax.ShapeDtypeStruct((1, D_MODEL), F32)], scratch_shapes=[],
        params=_params(("arbitrary",), 56), args=(dy2, a, w_mo, w_mi, dout, x1, y, g2, g1), rider=rider)


def _tn_matmul(a, b, name, bm, bn, square_a=False, column_shards=False, rider=None):
    seq, m = a.shape
    n = b.shape[1]
    ts = 2048

    def body(a_ref, b_ref, o_ref):
        @pl.when(pl.program_id(2) == 0)
        def _():
            o_ref[...] = jnp.zeros(o_ref.shape, F32)

        av = a_ref[...]
        if square_a:
            af = av.astype(F32)
            av = (af * af).astype(BF16)
        o_ref[...] += _dot_tn(av, b_ref[...])

    if column_shards:
        out_spec = pl.BlockSpec((None, bm, bn), lambda mi, ni, s: (ni, mi, 0))
        out_shape = jax.ShapeDtypeStruct((n // bn, m, bn), F32)
    else:
        out_spec = pl.BlockSpec((bm, bn), lambda mi, ni, s: (mi, ni))
        out_shape = jax.ShapeDtypeStruct((m, n), F32)
    (out,), riding = _call(
        body, name=name, grid=(m // bm, n // bn, seq // ts),
        in_specs=[pl.BlockSpec((ts, bm), lambda mi, ni, s: (s, mi)), pl.BlockSpec((ts, bn), lambda mi, ni, s: (s, ni))],
        out_specs=[out_spec], out_shape=[out_shape], scratch_shapes=[],
        params=_params(("arbitrary", "arbitrary", "arbitrary"), 40), args=(a, b), rider=rider)
    return out, riding


def _tn_matmul_residue(a, b, dil, name):
    length = a.shape[0]
    m, n = a.shape[1] // dil, b.shape[1] // dil
    ts = min(1024, length)

    def body(a_ref, b_ref, o_ref):
        @pl.when((pl.program_id(0) == 0) & (pl.program_id(1) == 0))
        def _():
            o_ref[...] = jnp.zeros(o_ref.shape, F32)

        o_ref[...] += _dot_tn(a_ref[...], b_ref[...])

    return _pallas(
        body, name=name, grid=(dil, length // ts),
        in_specs=[pl.BlockSpec((ts, m), lambda r, s: (s, r)), pl.BlockSpec((ts, n), lambda r, s: (s, r))],
        out_specs=pl.BlockSpec((m, n), lambda r, s: (0, 0)),
        out_shape=jax.ShapeDtypeStruct((m, n), F32),
        compiler_params=_params(("arbitrary", "arbitrary"), 40),
    )(_in_hbm(a), _in_hbm(b))


def _mix_bwd(dy, ya, yg, mg, rest, w_out, w_ba, w_bg, w_sp, b_col, ln_g, ln_b, rider=None):
    seq = dy.shape[0]
    tm = 256

    def body(dy_ref, ya_ref, yg_ref, mg_ref, up_ref, zp_ref, gap_ref, gbp_ref, wout_ref, wba_ref, wbg_ref,
             wsp_ref, bcol_ref, lg_ref, lb_ref,
             dya0, dya1, dya2, dpr_ref, dwout_ref, dwba_ref, dwbg_ref, dwsp_ref, dbb_ref, dlg_ref, dlb_ref,
             dzln_s, du_s, slab):
        @pl.when(pl.program_id(0) == 0)
        def _():
            for ref in (dwout_ref, dwba_ref, dwbg_ref, dwsp_ref, dbb_ref, dlg_ref, dlb_ref):
                ref[...] = jnp.zeros(ref.shape, F32)

        dyv = dy_ref[...]
        dm = _dot_nt(dyv, wout_ref[...])
        dwout_ref[...] += _dot_tn(mg_ref[...], dyv)
        yab = ya_ref[...].astype(BF16)
        ygb = yg_ref[...]
        a = _dot(yab, wba_ref[...])
        bm = _dot(ygb, wbg_ref[...])
        ga = jax.nn.sigmoid(gap_ref[...].astype(F32))
        gb = jax.nn.sigmoid(gbp_ref[...].astype(F32))
        dpr_ref[:, 2 * GMLP_W:2 * GMLP_W + D_MODEL] = (dm * a * (ga * (1.0 - ga))).astype(BF16)
        dpr_ref[:, 2 * GMLP_W + D_MODEL:REST_W] = (dm * bm * (gb * (1.0 - gb))).astype(BF16)
        da = (dm * ga).astype(BF16)
        db = (dm * gb).astype(BF16)
        dwba = _dot_tn(yab, da)
        dwbg = _dot_tn(ygb, db)
        shard_w = D_MODEL // N_CHIPS
        for j in range(N_CHIPS):
            dwba_ref[j] += dwba[:, j * shard_w:(j + 1) * shard_w]
            dwbg_ref[j] += dwbg[:, j * shard_w:(j + 1) * shard_w]
        dya = _dot_nt(da, wba_ref[...])
        for dya_ref, d in zip((dya0, dya1, dya2), DILATIONS):
            _put_residue(slab, dya, dya_ref, d, GROUP_W, 0)
        dyg = _dot_nt(db, wbg_ref[...])

        zp = zp_ref[...].astype(F32)
        zhat, rstd = _layernorm_stats(_gelu(zp))
        lg = lg_ref[...]
        zln = (zhat * lg + lb_ref[...]).astype(BF16)
        up = up_ref[...].astype(F32)
        u = _gelu(up)
        tril = _tril_mask()
        for g in range(GMLP_GROUPS):
            wm = jnp.where(tril, wsp_ref[g], 0.0).astype(BF16)
            cols = slice(g * CHUNK, (g + 1) * CHUNK)
            for c in range(tm // CHUNK):
                rows = slice(c * CHUNK, (c + 1) * CHUNK)
                zb = zln[rows, cols]
                sz = _dot(wm, zb) + bcol_ref[g]
                dyg_cg = dyg[rows, cols]
                du_s[rows, cols] = dyg_cg * sz
                dsz = dyg_cg * u[rows, cols]
                dszb = dsz.astype(BF16)
                dbb_ref[g] += jnp.broadcast_to(jnp.sum(dsz, axis=-1, keepdims=True), (CHUNK, CHUNK))
                dwsp_ref[g] += jnp.where(tril, _dot_nt(dszb, zb), 0.0)
                dzln_s[rows, cols] = _dot_tn(wm, dszb)
        dzln = dzln_s[...]
        dlg_ref[...] += jnp.sum(dzln * zhat, axis=0, keepdims=True)
        dlb_ref[...] += jnp.sum(dzln, axis=0, keepdims=True)
        dzh = dzln * lg
        dz = rstd * (dzh - jnp.mean(dzh, axis=-1, keepdims=True) - zhat * jnp.mean(dzh * zhat, axis=-1, keepdims=True))
        dpr_ref[:, GMLP_W:2 * GMLP_W] = (dz * _gelu_grad(zp)).astype(BF16)
        dpr_ref[:, 0:GMLP_W] = (du_s[...] * _gelu_grad(up)).astype(BF16)

    tok = lambda w: pl.BlockSpec((tm, w), lambda i: (i, 0))
    full = lambda *s: pl.BlockSpec(s, lambda i: (0,) * len(s))
    return _call(
        body, name="mix_bwd", grid=(seq // tm,),
        in_specs=[tok(D_MODEL), tok(GROUP_W), tok(GMLP_W), tok(D_MODEL),
                  pl.BlockSpec((tm, GMLP_W), lambda i: (i, 0)), pl.BlockSpec((tm, GMLP_W), lambda i: (i, 1)),
                  pl.BlockSpec((tm, D_MODEL), lambda i: (i, 1)), pl.BlockSpec((tm, D_MODEL), lambda i: (i, 2)),
                  full(D_MODEL, D_MODEL), full(GROUP_W, D_MODEL), full(GMLP_W, D_MODEL),
                  full(GMLP_GROUPS, CHUNK, CHUNK), full(GMLP_GROUPS, CHUNK, 1), full(1, GMLP_W), full(1, GMLP_W)],
        out_specs=[pl.BlockSpec((tm // d, d * GROUP_W), lambda i: (i, 0)) for d in DILATIONS]
        + [tok(REST_W), full(D_MODEL, D_MODEL), full(N_CHIPS, GROUP_W, D_MODEL // N_CHIPS),
           full(N_CHIPS, GMLP_W, D_MODEL // N_CHIPS),
           full(GMLP_GROUPS, CHUNK, CHUNK), full(GMLP_GROUPS, CHUNK, CHUNK), full(1, GMLP_W), full(1, GMLP_W)],
        out_shape=[jax.ShapeDtypeStruct((seq // d, d * GROUP_W), F32) for d in DILATIONS]
        + [jax.ShapeDtypeStruct((seq, REST_W), BF16),
           jax.ShapeDtypeStruct((D_MODEL, D_MODEL), F32), jax.ShapeDtypeStruct((N_CHIPS, GROUP_W, D_MODEL // N_CHIPS), F32),
           jax.ShapeDtypeStruct((N_CHIPS, GMLP_W, D_MODEL // N_CHIPS), F32),
           jax.ShapeDtypeStruct((GMLP_GROUPS, CHUNK, CHUNK), F32),
           jax.ShapeDtypeStruct((GMLP_GROUPS, CHUNK, CHUNK), F32), jax.ShapeDtypeStruct((1, GMLP_W), F32),
           jax.ShapeDtypeStruct((1, GMLP_W), F32)],
        scratch_shapes=[pltpu.VMEM((tm, GMLP_W), F32), pltpu.VMEM((tm, GMLP_W), F32),
                        pltpu.VMEM((GROUP_W // LANES, tm, LANES), F32)],
        params=_params(("arbitrary",), 56),
        args=(dy, ya, yg, mg, rest, rest, rest, rest, w_out, w_ba, w_bg, w_sp, b_col, ln_g, ln_b), rider=rider)


IN_PROJ_BWD_TM = 256


def _in_proj_bwd(dqkv, drest, w_in, x, dx1, g0, so_far, span, rider=None):
    seq = x.shape[0]
    tm = IN_PROJ_BWD_TM
    off, steps = span
    gx_so_far, dg_so_far = so_far

    def body(d0, d1, d2, dr_ref, w_ref, x_ref, dx1_ref, g_ref, dg_in_ref, gx_in_ref, gx_ref, dg_ref, slab):
        @pl.when(pl.program_id(0) == 0)
        def _():
            dg_ref[...] = dg_in_ref[...]

        dh = _dot(dr_ref[...], w_ref[QKV_W:, :])
        for g, (d_ref, dil) in enumerate(zip((d0, d1, d2), DILATIONS)):
            piece = d_ref[...] if dil == 1 else _get_tokens(slab, d_ref, dil, 3 * GROUP_W, 0, 3 * GROUP_W).astype(BF16)
            for section, (lo, hi) in enumerate(_qkv_columns(g)):
                dh = dh + _dot(piece[:, section * GROUP_W:(section + 1) * GROUP_W], w_ref[lo:hi, :])
        dres, dg = _rmsnorm_bwd(dh, x_ref[...], g_ref[...])
        gx_ref[...] = dx1_ref[...] + dres
        dg_ref[...] += dg

    tok = lambda w: pl.BlockSpec((tm, w), lambda i: (i + off, 0))
    full = lambda *s: pl.BlockSpec(s, lambda i: (0,) * len(s))
    in_specs = ([pl.BlockSpec((tm // d, d * 3 * GROUP_W), lambda i: (i + off, 0)) for d in DILATIONS] + [tok(REST_W)]
                + [_resident((IN_W, D_MODEL))]
                + [tok(D_MODEL), tok(D_MODEL), full(1, D_MODEL), full(1, D_MODEL), HBM_SPEC])
    return _call(
        body, name=f"in_proj_bwd_{off}", grid=(steps,), in_specs=in_specs,
        out_specs=[tok(D_MODEL), full(1, D_MODEL)],
        out_shape=[jax.ShapeDtypeStruct((seq, D_MODEL), F32), jax.ShapeDtypeStruct((1, D_MODEL), F32)],
        scratch_shapes=[pltpu.VMEM((3 * GROUP_W // LANES, tm, LANES), F32)],
        params=_params(("arbitrary",), 48), args=(*dqkv, drest, w_in, x, dx1, g0, dg_so_far, gx_so_far),
        rider=rider, aliases={len(in_specs) - 1: 0})


def _adamw(w, g, m, v, name):
    rows, cols = w.shape
    tr = _row_tile(rows) if rows % 16 == 0 else rows
    c1 = 1.0 - ADAM_B1 ** ADAM_STEP
    c2 = 1.0 - ADAM_B2 ** ADAM_STEP

    def body(w_ref, g_ref, m_ref, v_ref, go_ref, d_ref, nm_ref, nv_ref):
        gv = g_ref[...]
        go_ref[...] = gv
        nm = ADAM_B1 * m_ref[...] + (1.0 - ADAM_B1) * gv
        nv = ADAM_B2 * v_ref[...] + (1.0 - ADAM_B2) * (gv * gv)
        d_ref[...] = -ADAM_LR * ((nm / c1) / (jnp.sqrt(nv / c2) + ADAM_EPS) + ADAM_WD * w_ref[...])
        nm_ref[...] = nm
        nv_ref[...] = nv

    spec = pl.BlockSpec((tr, cols), lambda i: (i, 0))
    return _pallas(
        body, name=name, grid=(rows // tr,),
        in_specs=[spec] * 4, out_specs=[spec] * 4,
        out_shape=[jax.ShapeDtypeStruct((rows, cols), F32)] * 4,
        compiler_params=_params(("arbitrary",), 32, small=True),
    )(w, g, m, v)


def _place():
    x, y, c = lax.axis_index("x"), lax.axis_index("y"), lax.axis_index("c")
    chips = [(1 - x, y), (x, 1 - y), (1 - x, 1 - y)]
    return x, y, c, chips


class _Exchange:
    def __init__(self, inputs, out_shapes, n_sems, start, finish, aliases=None):
        self.inputs, self.out_shapes, self.n_sems = list(inputs), list(out_shapes), n_sems
        self.start, self.finish, self.aliases = start, finish, dict(aliases or {})

    def scratch(self):
        return [pltpu.SemaphoreType.DMA((self.n_sems,)), pltpu.SemaphoreType.DMA((self.n_sems,))]


def _together(*parts):
    ins = [len(p.inputs) for p in parts]
    outs = [len(p.out_shapes) for p in parts]

    def split(refs, counts):
        pos, pieces = 0, []
        for cnt in counts:
            pieces.append(refs[pos:pos + cnt])
            pos += cnt
        return pieces

    def run(which):
        def go(in_refs, out_refs, *sems):
            for k, (p, i, o) in enumerate(zip(parts, split(in_refs, ins), split(out_refs, outs))):
                getattr(p, which)(i, o, sems[2 * k], sems[2 * k + 1])
        return go

    both = _Exchange([a for p in parts for a in p.inputs], [s for p in parts for s in p.out_shapes], 0, run("start"),
                     run("finish"))
    both.aliases = {sum(ins[:k]) + i: sum(outs[:k]) + o for k, p in enumerate(parts) for i, o in p.aliases.items()}
    both.scratch = lambda: [s for p in parts for s in p.scratch()]
    return both


def _run_exchange(ex, name):
    n_in, n_out = len(ex.inputs), len(ex.out_shapes)

    def body(*refs):
        ins, outs, sems = refs[:n_in], refs[n_in:n_in + n_out], refs[n_in + n_out:]
        ex.start(ins, outs, *sems)
        ex.finish(ins, outs, *sems)

    return _pallas(
        body, name=name, in_specs=[HBM_SPEC] * n_in, out_specs=[HBM_SPEC] * n_out, out_shape=ex.out_shapes,
        scratch_shapes=ex.scratch(), input_output_aliases=ex.aliases,
    )(*ex.inputs)


def _call(body, *, name, grid, in_specs, out_specs, out_shape, scratch_shapes, params, args, rider=None, aliases=None):
    in_specs, out_specs, out_shape, scratch_shapes = list(in_specs), list(out_specs), list(out_shape), list(scratch_shapes)
    aliases = dict(aliases or {})
    args = [_in_hbm(a) for a in args]
    if rider is None:
        outs = _pallas(body, name=name, grid=grid, in_specs=in_specs, out_specs=out_specs, out_shape=out_shape,
                              scratch_shapes=scratch_shapes, input_output_aliases=aliases, compiler_params=params)(*args)
        return list(outs), []
    n_in, n_out, n_scr = len(in_specs), len(out_specs), len(scratch_shapes)
    r_in, r_out = len(rider.inputs), len(rider.out_shapes)

    def wrapped(*refs):
        ins, r_ins = refs[:n_in], refs[n_in:n_in + r_in]
        pos = n_in + r_in
        outs, r_outs = refs[pos:pos + n_out], refs[pos + n_out:pos + n_out + r_out]
        pos += n_out + r_out
        scr, sems = refs[pos:pos + n_scr], refs[pos + n_scr:]
        ids = [pl.program_id(k) for k in range(len(grid))]
        first, last = ids[0] == 0, ids[0] == grid[0] - 1
        for k in range(1, len(grid)):
            first, last = first & (ids[k] == 0), last & (ids[k] == grid[k] - 1)

        @pl.when(first)
        def _():
            rider.start(r_ins, r_outs, *sems)

        body(*ins, *outs, *scr)

        @pl.when(last)
        def _():
            rider.finish(r_ins, r_outs, *sems)

    outs = _pallas(
        wrapped, name=name, grid=grid, in_specs=in_specs + [HBM_SPEC] * r_in, out_specs=out_specs + [HBM_SPEC] * r_out,
        out_shape=out_shape + rider.out_shapes, scratch_shapes=scratch_shapes + rider.scratch(),
        input_output_aliases={**aliases, **{n_in + i: n_out + o for i, o in rider.aliases.items()}}, compiler_params=params,
    )(*args, *rider.inputs)
    return list(outs[:n_out]), list(outs[n_out:])


def _stage_weights(shards):
    n = len(shards)

    def body(*refs):
        ins, outs, stages, sems = refs[:n], refs[n:2 * n], refs[2 * n:3 * n], refs[3 * n]
        x, y, _, _ = _place()
        copies = []
        for t in range(n):
            stages[t][...] = ins[t][...].astype(BF16)
            copies.append(pltpu.make_async_copy(stages[t], outs[t].at[2 * x + y], sems.at[t]))
            copies[-1].start()
        for cp in copies:
            cp.wait()

    assert sum(s.size * 6 for s in shards) <= (CALL_VMEM_MIB - 8) * MIB
    return _pallas(
        body, name="stage_weights", in_specs=[VMEM_SPEC] * n, out_specs=[HBM_SPEC] * n,
        out_shape=[jax.ShapeDtypeStruct((N_CHIPS,) + s.shape, BF16) for s in shards],
        scratch_shapes=[pltpu.VMEM(s.shape, BF16) for s in shards] + [pltpu.SemaphoreType.DMA((n,))],
        compiler_params=pltpu.CompilerParams(vmem_limit_bytes=SMALL_VMEM_MIB * MIB),
    )(*shards)


def _gather(buffers, stage="both", part=(0, 1)):
    n = len(buffers)
    halves = [b.shape[1] // part[1] // 2 for b in buffers]

    def half_of(outs, t, chip, which):
        return outs[t].at[chip, pl.ds((2 * part[0] + which) * halves[t], halves[t]), :]

    def copy(outs, sems, t, k, chip, which, to):
        rows = half_of(outs, t, chip, which)
        return pltpu.make_async_remote_copy(src_ref=rows, dst_ref=rows, send_sem=sems[0].at[6 * t + k],
                                            recv_sem=sems[1].at[6 * t + k], device_id=to, device_id_type=MESH)

    def to_chips(outs, sems, what):
        x, y, c, chips = _place()
        for t in range(n):
            for j, (px, py) in enumerate(chips):
                if what == "start":
                    copy(outs, sems, t, j, 2 * x + y, c, (px, py, c)).start()
                else:
                    copy(outs, sems, t, j, 2 * px + py, c, (px, py, c)).wait_recv()
                    copy(outs, sems, t, j, 2 * x + y, c, (px, py, c)).wait_send()

    def to_sibling(outs, sems, what):
        x, y, c, chips = _place()
        for t in range(n):
            for j, (px, py) in enumerate(chips):
                if what == "start":
                    copy(outs, sems, t, 3 + j, 2 * px + py, c, (x, y, 1 - c)).start()
                else:
                    copy(outs, sems, t, 3 + j, 2 * px + py, 1 - c, (x, y, 1 - c)).wait_recv()
                    copy(outs, sems, t, 3 + j, 2 * px + py, c, (x, y, 1 - c)).wait_send()

    def start(ins, outs, *sems):
        (to_sibling if stage == "pair" else to_chips)(outs, sems, "start")

    def finish(ins, outs, *sems):
        if stage == "both":
            x, y, c, chips = _place()
            for j, (px, py) in enumerate(chips):
                for t in range(n):
                    copy(outs, sems, t, j, 2 * px + py, c, (px, py, c)).wait_recv()
                    copy(outs, sems, t, 3 + j, 2 * px + py, c, (x, y, 1 - c)).start()
            for j, (px, py) in enumerate(chips):
                for t in range(n):
                    copy(outs, sems, t, j, 2 * x + y, c, (px, py, c)).wait_send()
            to_sibling(outs, sems, "finish")
        elif stage == "chips":
            to_chips(outs, sems, "finish")
        else:
            to_sibling(outs, sems, "finish")

    return _Exchange(buffers, [jax.ShapeDtypeStruct(b.shape, b.dtype) for b in buffers], 6 * n, start, finish,
                     aliases={t: t for t in range(n)})


def _pair_exchange(grads):
    n = len(grads)
    halves = [g.shape[1] // 2 for g in grads]

    def copies(ins, outs, send_sems, recv_sems):
        x, y, c, _ = _place()
        return [pltpu.make_async_remote_copy(
            src_ref=ins[t].at[:, pl.ds((1 - c) * halves[t], halves[t]), :], dst_ref=outs[t],
            send_sem=send_sems.at[t], recv_sem=recv_sems.at[t], device_id=(x, y, 1 - c), device_id_type=MESH)
            for t in range(n)]

    def start(*refs):
        for cp in copies(*refs):
            cp.start()

    def finish(*refs):
        for cp in copies(*refs):
            cp.wait()

    return _Exchange(grads, [jax.ShapeDtypeStruct((N_CHIPS, h, g.shape[2]), F32) for g, h in zip(grads, halves)], n,
                     start, finish)


def _row_tile(rows):
    return max(t for t in range(16, 257, 16) if rows % t == 0)


def _pair_add(grad, other, place, name):
    _, rows, cols = grad.shape
    rh = rows // 2
    tr = _row_tile(rh)
    nb = rh // tr

    def body(p_ref, g_ref, a_ref, wire_ref, own_ref):
        s = g_ref[...] + a_ref[...]
        wire_ref[...] = s.astype(BF16)

        @pl.when(pl.program_id(1) == p_ref[1])
        def _():
            own_ref[...] = s

    blk = (None, tr, cols)
    return _pallas(
        body, name=name,
        grid_spec=pltpu.PrefetchScalarGridSpec(
            num_scalar_prefetch=1, grid=(nb, N_CHIPS),
            in_specs=[pl.BlockSpec(blk, lambda i, j, p: (j, p[0] * nb + i, 0)), pl.BlockSpec(blk, lambda i, j, p: (j, i, 0))],
            out_specs=[pl.BlockSpec(blk, lambda i, j, p: (j, i, 0)), pl.BlockSpec((tr, cols), lambda i, j, p: (i, 0))]),
        out_shape=[jax.ShapeDtypeStruct((N_CHIPS, rh, cols), BF16), jax.ShapeDtypeStruct((rh, cols), F32)],
        compiler_params=_params(("arbitrary", "arbitrary"), 32, small=True),
    )(place, grad, other)


def _chip_exchange(wires):
    n = len(wires)

    def copies(ins, outs, send_sems, recv_sems):
        x, y, c, chips = _place()
        return [pltpu.make_async_remote_copy(
            src_ref=ins[t].at[2 * px + py], dst_ref=outs[t].at[j], send_sem=send_sems.at[3 * t + j],
            recv_sem=recv_sems.at[3 * t + j], device_id=(px, py, c), device_id_type=MESH)
            for t in range(n) for j, (px, py) in enumerate(chips)]

    def start(*refs):
        for cp in copies(*refs):
            cp.start()

    def finish(*refs):
        for cp in copies(*refs):
            cp.wait()

    return _Exchange(wires, [jax.ShapeDtypeStruct((3,) + w.shape[1:], BF16) for w in wires], 3 * n, start, finish)


def _chip_add(own, arrived, place, name):
    rh, cols = own.shape
    tr = _row_tile(rh)
    nb = rh // tr

    def body(p_ref, s_ref, b0, b1, b2, o_ref):
        o_ref[...] = ((s_ref[...] + b0[...].astype(F32)) + b1[...].astype(F32)) + b2[...].astype(F32)

    blk = (None, tr, cols)
    return _pallas(
        body, name=name,
        grid_spec=pltpu.PrefetchScalarGridSpec(
            num_scalar_prefetch=1, grid=(nb,),
            in_specs=[pl.BlockSpec((tr, cols), lambda i, p: (i, 0)), pl.BlockSpec(blk, lambda i, p: (0, i, 0)),
                      pl.BlockSpec(blk, lambda i, p: (1, i, 0)), pl.BlockSpec(blk, lambda i, p: (2, i, 0))],
            out_specs=pl.BlockSpec((tr, cols), lambda i, p: (p[0] * nb + i, 0))),
        out_shape=jax.ShapeDtypeStruct((2 * rh, cols), F32),
        compiler_params=_params(("arbitrary",), 32, small=True),
    )(place, own, arrived, arrived, arrived)


def _pair_share(halves):
    n = len(halves)
    rhs = [h.shape[0] // 2 for h in halves]

    def copy(outs, send_sems, recv_sems, t, which):
        x, y, c, _ = _place()
        rows = outs[t].at[pl.ds(which * rhs[t], rhs[t]), :]
        return pltpu.make_async_remote_copy(src_ref=rows, dst_ref=rows, send_sem=send_sems.at[t], recv_sem=recv_sems.at[t],
                                            device_id=(x, y, 1 - c), device_id_type=MESH)

    def start(ins, outs, send_sems, recv_sems):
        c = lax.axis_index("c")
        for t in range(n):
            copy(outs, send_sems, recv_sems, t, c).start()

    def finish(ins, outs, send_sems, recv_sems):
        c = lax.axis_index("c")
        for t in range(n):
            copy(outs, send_sems, recv_sems, t, c).wait_send()
            copy(outs, send_sems, recv_sems, t, 1 - c).wait_recv()

    return _Exchange(halves, [jax.ShapeDtypeStruct(h.shape, F32) for h in halves], n, start, finish,
                     aliases={t: t for t in range(n)})


class _GradReduction:
    def __init__(self, grads, place, tag):
        self.names, self.grads, self.place, self.tag = list(grads), grads, place, tag

    def pair_exchange(self):
        return _pair_exchange([self.grads[n] for n in self.names])

    def chip_exchange(self, others):
        sums = [_pair_add(self.grads[n], o, self.place, f"{self.tag}_pair_add_{n}") for n, o in zip(self.names, others)]
        self.owns = [own for _, own in sums]
        return _chip_exchange([wire for wire, _ in sums])

    def pair_share(self, arrived):
        return _pair_share([_chip_add(own, arr, self.place, f"{self.tag}_chip_add_{n}")
                            for n, own, arr in zip(self.names, self.owns, arrived)])

    def result(self, shared):
        return dict(zip(self.names, shared))


def _all_reduce_small(p):
    rows, lanes = p.shape
    half = rows // 2

    def body(p_ref, o_ref, sib, sums, send_sems, recv_sems):
        x, y, c, chips = _place()
        mine, sibling = 2 * x + y, (x, y, 1 - c)
        swap = pltpu.make_async_remote_copy(src_ref=p_ref, dst_ref=sib, send_sem=send_sems.at[0], recv_sem=recv_sems.at[0],
                                            device_id=sibling, device_id_type=MESH)
        swap.start()
        swap.wait()
        sums[mine] = p_ref[...] + sib[...]

        def copy(k, chip, which, to):
            part = sums.at[chip, pl.ds(which * half, half), :]
            return pltpu.make_async_remote_copy(src_ref=part, dst_ref=part, send_sem=send_sems.at[k], recv_sem=recv_sems.at[k],
                                                device_id=to, device_id_type=MESH)

        for j, (px, py) in enumerate(chips):
            copy(1 + j, mine, c, (px, py, c)).start()
        for j, (px, py) in enumerate(chips):
            copy(1 + j, 2 * px + py, c, (px, py, c)).wait_recv()
            copy(4 + j, 2 * px + py, c, sibling).start()
        for j, (px, py) in enumerate(chips):
            copy(4 + j, 2 * px + py, 1 - c, sibling).wait_recv()
        for j, (px, py) in enumerate(chips):
            copy(1 + j, mine, c, (px, py, c)).wait_send()
            copy(4 + j, 2 * px + py, c, sibling).wait_send()
        o_ref[...] = ((sums[0] + sums[1]) + sums[2]) + sums[3]

    return _pallas(
        body, name="small_all_reduce", in_specs=[VMEM_SPEC], out_specs=VMEM_SPEC,
        out_shape=jax.ShapeDtypeStruct((rows, lanes), F32),
        scratch_shapes=[pltpu.VMEM((rows, lanes), F32), pltpu.VMEM((N_CHIPS, rows, lanes), F32),
                        pltpu.SemaphoreType.DMA((7,)), pltpu.SemaphoreType.DMA((7,))],
        compiler_params=pltpu.CompilerParams(vmem_limit_bytes=SMALL_VMEM_MIB * MIB),
    )(p)


BIG = ("w_in", "w_branch_attn", "w_branch_gmlp", "w_out", "w_mlp_in", "w_mlp_out")
COLUMN_SHARDED = ("w_branch_attn", "w_branch_gmlp", "w_mlp_in")
SMALL = ("norm_pre_mix", "w_spatial", "b_spatial", "ln_v_gain", "ln_v_bias", "norm_post_mix", "norm_pre_mlp", "norm_post_mlp")
ORDER = ("norm_pre_mix", "w_in", "w_spatial", "b_spatial", "ln_v_gain", "ln_v_bias", "w_branch_attn", "w_branch_gmlp",
         "w_out", "norm_post_mix", "norm_pre_mlp", "w_mlp_in", "w_mlp_out", "norm_post_mlp")


def _full_weight(name, gathered):
    if name in COLUMN_SHARDED:
        return jnp.transpose(gathered, (1, 0, 2)).reshape(gathered.shape[1], -1)
    return gathered.reshape(-1, gathered.shape[2])


def _rows8(a):
    a = a.reshape(-1, 128)
    pad = (-a.shape[0]) % 8
    return jnp.pad(a, ((0, pad), (0, 0))) if pad else a


def _qkv_columns(group):
    return [(sec * ATTN_W + group * GROUP_W, sec * ATTN_W + (group + 1) * GROUP_W) for sec in range(3)]


def _device_step(x, target, small, shards, place):
    seq = x.shape[0]
    g0, g1, g2, g3 = small["norm_pre_mix"], small["norm_post_mix"], small["norm_pre_mlp"], small["norm_post_mlp"]
    w_sp = small["w_spatial"]
    b_col = small["b_spatial"].reshape(GMLP_GROUPS, CHUNK, 1)
    ln_g, ln_b = small["ln_v_gain"], small["ln_v_bias"]

    staged = _stage_weights(shards)
    h, tables, (w_in,) = _prepare(x, g0, rider=_gather(staged[:1]))
    w_in = _full_weight("w_in", w_in)
    (*qkv, rest), landed = _in_proj(h[0], w_in, *tables[1], rider=_gather(staged[1:], "chips"))

    o_l, gathered = _attn_fwd(qkv[0], DILATIONS[0], rider=_gather(landed, "pair"))
    full = {n: _full_weight(n, gw) for n, gw in zip(BIG[1:], gathered)}
    for g in range(1, N_GROUPS):
        o_l.extend(_attn_fwd(qkv[g], DILATIONS[g])[0])
    (*ya_l, yg, mg, y, x1), _ = _mix_fwd(o_l, rest, x, w_sp, b_col, ln_g, ln_b, full["w_branch_attn"],
                                        full["w_branch_gmlp"], full["w_out"], g1)
    ya, lse = ya_l[0::2], ya_l[1::2]
    h2, a, dy2, dout, loss8, dg3 = _mlp_fwd(x1, g2, g3, full["w_mlp_in"], full["w_mlp_out"], target)
    d_wmo, _ = _tn_matmul(a, dy2, "grad_w_mlp_out", 1024, 1024, square_a=True)
    mlp_out = _GradReduction({"w_mlp_out": d_wmo.reshape(N_CHIPS, D_FF // N_CHIPS, D_MODEL)}, place, "mlp_out")
    (dap, dx1, dy, dg2, dg1), riding = _mlp_bwd(dy2, a, full["w_mlp_out"], full["w_mlp_in"], dout, x1, y, g2, g1,
                                                 rider=mlp_out.pair_exchange())
    d_wmi, riding = _tn_matmul(h2, dap, "grad_w_mlp_in", 1024, 1024, column_shards=True,
                               rider=mlp_out.chip_exchange(riding))
    mlp_in = _GradReduction({"w_mlp_in": d_wmi}, place, "mlp_in")
    (*dya, drest, d_wout, d_wba, d_wbg, d_wsp, d_bb, d_lg, d_lb), riding = _mix_bwd(
        dy, ya[0], yg, mg, rest, full["w_out"], full["w_branch_attn"], full["w_branch_gmlp"], w_sp, b_col, ln_g, ln_b,
        rider=_together(mlp_out.pair_share(riding), mlp_in.pair_exchange()))
    reduced = mlp_out.result(riding[:1])
    mix = _GradReduction({"w_branch_attn": d_wba, "w_branch_gmlp": d_wbg,
                          "w_out": d_wout.reshape(N_CHIPS, D_MODEL // N_CHIPS, D_MODEL)}, place, "mix")
    attn = lambda g, rider: _attn_bwd(qkv[g], dya[g], ya[g], lse[g], *tables[DILATIONS[g]], DILATIONS[g], rider=rider)
    dqkv0, riding = attn(0, _together(mlp_in.chip_exchange(riding[1:]), mix.pair_exchange()))
    dqkv1, riding = attn(1, _together(mlp_in.pair_share(riding[:1]), mix.chip_exchange(riding[1:])))
    reduced.update(mlp_in.result(riding[:1]))
    dqkv2, riding = attn(2, mix.pair_share(riding[1:]))
    reduced.update(mix.result(riding))
    dqkv = [dqkv0, dqkv1, dqkv2]

    d_qkv = [_tn_matmul_residue(dqkv[g], h[g], dil, f"grad_w_in_qkv{g}") for g, dil in enumerate(DILATIONS)]
    d_rest, _ = _tn_matmul(drest, h[0], "grad_w_in_rest", 1024, 1024)
    d_win = jnp.concatenate([d_qkv[g][s * GROUP_W:(s + 1) * GROUP_W] for s in range(3) for g in range(N_GROUPS)]
                            + [d_rest], axis=0)
    first = _GradReduction({"w_in": d_win.reshape(N_CHIPS, IN_W // N_CHIPS, D_MODEL)}, place, "w_in")
    tiles = seq // IN_PROJ_BWD_TM
    so_far = (lax.empty((seq, D_MODEL), F32), jnp.zeros((1, D_MODEL), F32))
    in_bwd = lambda so_far, span, rider: _in_proj_bwd(dqkv, drest, w_in, x, dx1, g0, so_far, span, rider=rider)
    so_far, riding = in_bwd(so_far, (0, 3 * tiles // 8), first.pair_exchange())
    (grad_x, dg0), riding = in_bwd(so_far, (3 * tiles // 8, 5 * tiles // 8), first.chip_exchange(riding))
    reduced.update(first.result(_run_exchange(first.pair_share(riding), "w_in_pair_share")))
    little = {"norm_pre_mix": dg0, "w_spatial": d_wsp, "b_spatial": d_bb[:, :, 0], "ln_v_gain": d_lg, "ln_v_bias": d_lb,
              "norm_post_mix": dg1, "norm_pre_mlp": dg2, "norm_post_mlp": dg3}
    return loss8, grad_x, reduced, little


def kernel(x, norm_pre_mix, w_in, w_spatial, b_spatial, ln_v_gain, ln_v_bias, w_branch_attn, w_branch_gmlp, w_out, norm_post_mix, norm_pre_mlp, w_mlp_in, w_mlp_out, norm_post_mlp, loss_target, m_norm_pre_mix, m_w_in, m_w_spatial, m_b_spatial, m_ln_v_gain, m_ln_v_bias, m_w_branch_attn, m_w_branch_gmlp, m_w_out, m_norm_post_mix, m_norm_pre_mlp, m_w_mlp_in, m_w_mlp_out, m_norm_post_mlp, v_norm_pre_mix, v_w_in, v_w_spatial, v_b_spatial, v_ln_v_gain, v_ln_v_bias, v_w_branch_attn, v_w_branch_gmlp, v_w_out, v_norm_post_mix, v_norm_pre_mlp, v_w_mlp_in, v_w_mlp_out, v_norm_post_mlp):
    given = dict(norm_pre_mix=norm_pre_mix, w_in=w_in, w_spatial=w_spatial, b_spatial=b_spatial, ln_v_gain=ln_v_gain,
                 ln_v_bias=ln_v_bias, w_branch_attn=w_branch_attn, w_branch_gmlp=w_branch_gmlp, w_out=w_out,
                 norm_post_mix=norm_post_mix, norm_pre_mlp=norm_pre_mlp, w_mlp_in=w_mlp_in, w_mlp_out=w_mlp_out,
                 norm_post_mlp=norm_post_mlp)
    moments_m = dict(norm_pre_mix=m_norm_pre_mix, w_in=m_w_in, w_spatial=m_w_spatial, b_spatial=m_b_spatial,
                     ln_v_gain=m_ln_v_gain, ln_v_bias=m_ln_v_bias, w_branch_attn=m_w_branch_attn,
                     w_branch_gmlp=m_w_branch_gmlp, w_out=m_w_out, norm_post_mix=m_norm_post_mix,
                     norm_pre_mlp=m_norm_pre_mlp, w_mlp_in=m_w_mlp_in, w_mlp_out=m_w_mlp_out, norm_post_mlp=m_norm_post_mlp)
    moments_v = dict(norm_pre_mix=v_norm_pre_mix, w_in=v_w_in, w_spatial=v_w_spatial, b_spatial=v_b_spatial,
                     ln_v_gain=v_ln_v_gain, ln_v_bias=v_ln_v_bias, w_branch_attn=v_w_branch_attn,
                     w_branch_gmlp=v_w_branch_gmlp, w_out=v_w_out, norm_post_mix=v_norm_post_mix,
                     norm_pre_mlp=v_norm_pre_mlp, w_mlp_in=v_w_mlp_in, w_mlp_out=v_w_mlp_out, norm_post_mlp=v_norm_post_mlp)
    cx, cy, cc = lax.axis_index("x"), lax.axis_index("y"), lax.axis_index("c")

    shards = [given[n][0].T if n == "w_in" else given[n][0] for n in BIG]
    small = {n: given[n][0] if given[n].ndim > 2 else given[n] for n in SMALL}
    place = jnp.stack([cc, 2 * cx + cy]).astype(jnp.int32)
    loss8, grad_x, grad_shard, grads = _device_step(x[0], loss_target[0], small, shards, place)

    packed = jnp.concatenate([_rows8(grads[n]) for n in SMALL] + [loss8], axis=0)
    summed = _all_reduce_small(packed)
    loss = summed[packed.shape[0] - loss8.shape[0], 0]
    row = 0
    for n in SMALL:
        shape = given[n][0].shape
        cnt = -(-(given[n][0].size // 128) // 8) * 8
        grad_shard[n] = summed[row:row + given[n][0].size // 128].reshape(shape)
        row += cnt

    grad_out, deltas, new_m, new_v = {}, {}, {}, {}
    for n in ORDER:
        shape = given[n].shape
        if n == "w_in":
            outs = _adamw(given[n][0].T, grad_shard[n], moments_m[n][0].T, moments_v[n][0].T, "adamw_" + n)
            outs = [o.T for o in outs]
        else:
            two_d = (-1, shape[-1])
            outs = _adamw(given[n].reshape(two_d), grad_shard[n].reshape(two_d), moments_m[n].reshape(two_d),
                          moments_v[n].reshape(two_d), "adamw_" + n)
        grad_out[n], deltas[n], new_m[n], new_v[n] = [o.reshape(shape) for o in outs]
    return (loss, grad_x[None], *[grad_out[n] for n in ORDER], *[deltas[n] for n in ORDER], *[new_m[n] for n in ORDER],
            *[new_v[n] for n in ORDER])
```

```python
import math

import jax
import jax.numpy as jnp
from jax import lax
from jax.experimental import pallas as pl
from jax.experimental.pallas import tpu as pltpu

F32 = jnp.float32
BF16 = jnp.bfloat16
MESH = pl.DeviceIdType.MESH

D_MODEL = 1024
HEAD_DIM = 64
HEADS_PER_GROUP = 4
GROUP_W = HEADS_PER_GROUP * HEAD_DIM
DILATIONS = (1, 4, 16)
N_GROUPS = len(DILATIONS)
ATTN_W = N_GROUPS * GROUP_W
QKV_W = 3 * ATTN_W
GMLP_W = 512
GMLP_GROUPS = 4
CHUNK = 128
REST_W = 2 * GMLP_W + 2 * D_MODEL
IN_W = QKV_W + REST_W
D_FF = 4096
QBLK = 128
ROPE_THETA = 10000.0
EPS = 1e-6
NEG = -1e30
SCALE = HEAD_DIM ** -0.5
N_CHIPS = 4

ADAM_LR = 0.001
ADAM_B1 = 0.9
ADAM_B2 = 0.999
ADAM_EPS = 1e-08
ADAM_WD = 0.01
ADAM_STEP = 10

MIB = 1024 * 1024
HBM_SPEC = pl.BlockSpec(memory_space=pltpu.HBM)
VMEM_SPEC = pl.BlockSpec(memory_space=pltpu.VMEM)


MLP_FWD_TM = 512
MLP_TM = 256


CALL_VMEM_MIB = 56
SMALL_VMEM_MIB = 32


def _params(semantics, vmem_mib, small=False):
    assert vmem_mib <= CALL_VMEM_MIB
    return pltpu.CompilerParams(dimension_semantics=semantics,
                                vmem_limit_bytes=(SMALL_VMEM_MIB if small else CALL_VMEM_MIB) * MIB)


def _in_hbm(a):
    return pltpu.with_memory_space_constraint(a, pltpu.HBM) if a.size * a.dtype.itemsize >= MIB else a


def _pallas(body, **kwargs):
    return pl.pallas_call(body, **kwargs)


def _resident(shape):
    return pl.BlockSpec(shape, lambda *_: (0,) * len(shape), pipeline_mode=pl.Buffered(1))


def _dot(a, b):
    return jnp.dot(a, b, preferred_element_type=F32)


def _dot_nt(a, b):
    return lax.dot_general(a, b, (((1,), (1,)), ((), ())), preferred_element_type=F32)


def _dot_tn(a, b):
    return lax.dot_general(a, b, (((0,), (0,)), ((), ())), preferred_element_type=F32)


_GELU_C = math.sqrt(2.0 / math.pi)


def _gelu(x):
    return x * (0.5 * (1.0 + jnp.tanh(_GELU_C * (x + 0.044715 * (x * x * x)))))


def _gelu_grad(x):
    t = jnp.tanh(_GELU_C * (x + 0.044715 * (x * x * x)))
    return 0.5 * (1.0 + t) + 0.5 * x * (1.0 - t * t) * (_GELU_C * (1.0 + 3.0 * 0.044715 * (x * x)))


def _rsqrt_ms(v):
    return lax.rsqrt(jnp.mean(v * v, axis=-1, keepdims=True) + EPS)


def _rmsnorm_bwd(dn, src, gain):
    r = _rsqrt_ms(src)
    t = gain * dn
    dgain = jnp.sum(dn * (src * r), axis=0, keepdims=True)
    dsrc = r * t - src * ((r * r * r) * jnp.mean(t * src, axis=-1, keepdims=True))
    return dsrc, dgain


def _rot_half(v):
    w = v.shape[-1]
    lane = lax.broadcasted_iota(jnp.int32, v.shape, v.ndim - 1)
    return jnp.where((lane % HEAD_DIM) < HEAD_DIM // 2, pltpu.roll(v, w - HEAD_DIM // 2, v.ndim - 1),
                     pltpu.roll(v, HEAD_DIM // 2, v.ndim - 1))


def _head_masks(shape):
    lane = lax.broadcasted_iota(jnp.int32, shape, 1)
    return [(lane >= h * HEAD_DIM) & (lane < (h + 1) * HEAD_DIM) for h in range(HEADS_PER_GROUP)]


def _head_stack(block, hmask):
    zero = jnp.zeros((), block.dtype)
    return jnp.concatenate([jnp.where(hm, block, zero) for hm in hmask], axis=0)


LANES = 128


def _put_residue(slab, val, out_ref, dil, width, col0):
    tm, w = val.shape
    if dil == 1:
        out_ref[:, col0:col0 + w] = val.astype(out_ref.dtype)
        return
    for k in range(w // LANES):
        slab[k] = val[:, k * LANES:(k + 1) * LANES]
    for r in range(dil):
        for k in range(w // LANES):
            c = r * width + col0 + k * LANES
            out_ref[:, c:c + LANES] = slab[k, pl.ds(r, tm // dil, stride=dil), :].astype(out_ref.dtype)


def _get_tokens(slab, in_ref, dil, width, col0, w):
    if dil == 1:
        return in_ref[:, col0:col0 + w].astype(F32)
    rows = in_ref.shape[0]
    for r in range(dil):
        for k in range(w // LANES):
            c = r * width + col0 + k * LANES
            slab[k, pl.ds(r, rows, stride=dil), :] = in_ref[:, c:c + LANES].astype(F32)
    return jnp.concatenate([slab[k] for k in range(w // LANES)], axis=1)


def _prepare(x, g0, rider=None):
    seq = x.shape[0]
    half = HEAD_DIM // 2
    inv_freq = ROPE_THETA ** (-jnp.arange(half, dtype=F32) / half)
    freq = jnp.tile(inv_freq, LANES // half).reshape(1, LANES)
    tm = 256

    def body(x_ref, g_ref, f_ref, *refs):
        h_refs, tabs, slab = refs[:N_GROUPS], refs[N_GROUPS:3 * N_GROUPS], refs[-1]
        xv = x_ref[...]
        hf = (xv * _rsqrt_ms(xv)) * g_ref[...]
        for g, dil in enumerate(DILATIONS):
            _put_residue(slab, hf, h_refs[g], dil, D_MODEL, 0)
        row = lax.broadcasted_iota(jnp.int32, (tm, LANES), 0) + pl.program_id(0) * tm
        lane = lax.broadcasted_iota(jnp.int32, (tm, LANES), 1)
        ang = row.astype(F32) * f_ref[...]
        cos = jnp.cos(ang)
        sin = jnp.where((lane % HEAD_DIM) < half, -jnp.sin(ang), jnp.sin(ang))
        for i, dil in enumerate(DILATIONS):
            for tab, val in ((tabs[2 * i], cos), (tabs[2 * i + 1], sin)):
                slab[0] = val
                for r in range(dil):
                    piece = slab[0, pl.ds(r, tm // dil, stride=dil), :] if dil > 1 else val
                    for k in range(GROUP_W // LANES):
                        tab[:, r * GROUP_W + k * LANES:r * GROUP_W + (k + 1) * LANES] = piece

    outs, riding = _call(
        body, name="prepare", grid=(seq // tm,),
        in_specs=[pl.BlockSpec((tm, D_MODEL), lambda i: (i, 0)), pl.BlockSpec((1, D_MODEL), lambda i: (0, 0)),
                  pl.BlockSpec((1, LANES), lambda i: (0, 0))],
        out_specs=[pl.BlockSpec((tm // d, d * D_MODEL), lambda i: (i, 0)) for d in DILATIONS]
        + [pl.BlockSpec((tm // d, d * GROUP_W), lambda i: (i, 0)) for d in DILATIONS for _ in range(2)],
        out_shape=[jax.ShapeDtypeStruct((seq // d, d * D_MODEL), BF16) for d in DILATIONS]
        + [jax.ShapeDtypeStruct((seq // d, d * GROUP_W), F32) for d in DILATIONS for _ in range(2)],
        scratch_shapes=[pltpu.VMEM((D_MODEL // LANES, tm, LANES), F32)],
        params=_params(("arbitrary",), 32), args=(x, g0, freq), rider=rider)
    tabs = outs[N_GROUPS:]
    return outs[:N_GROUPS], {d: (tabs[2 * i], tabs[2 * i + 1]) for i, d in enumerate(DILATIONS)}, riding


def _in_proj(h, w_in, cos_t, sin_t, rider=None):
    seq = h.shape[0]
    tm, tn = 512, GROUP_W
    n_qk = 2 * ATTN_W // tn
    n_qkv = QKV_W // tn

    def body(h_ref, w_ref, cos_ref, sin_ref, *refs):
        qkv_refs, rest_ref, slab = refs[:N_GROUPS], refs[N_GROUPS], refs[-1]
        hb = h_ref[...]
        cos, sin = cos_ref[...], sin_ref[...]
        for j in range(IN_W // tn):
            p = _dot_nt(hb, w_ref[j * tn:(j + 1) * tn, :])
            if j < n_qkv:
                if j < n_qk:
                    p = p * cos + _rot_half(p) * sin
                section, g = divmod(j, N_GROUPS)
                _put_residue(slab, p, qkv_refs[g], DILATIONS[g], 3 * GROUP_W, section * GROUP_W)
            else:
                rest_ref[:, (j - n_qkv) * tn:(j - n_qkv + 1) * tn] = p.astype(BF16)

    return _call(
        body, name="in_proj", grid=(seq // tm,),
        in_specs=[pl.BlockSpec((tm, D_MODEL), lambda i: (i, 0)),
                  _resident((IN_W, D_MODEL)),
                  pl.BlockSpec((tm, GROUP_W), lambda i: (i, 0)),
                  pl.BlockSpec((tm, GROUP_W), lambda i: (i, 0))],
        out_specs=[pl.BlockSpec((tm // d, d * 3 * GROUP_W), lambda i: (i, 0)) for d in DILATIONS]
        + [pl.BlockSpec((tm, REST_W), lambda i: (i, 0))],
        out_shape=[jax.ShapeDtypeStruct((seq // d, d * 3 * GROUP_W), BF16) for d in DILATIONS]
        + [jax.ShapeDtypeStruct((seq, REST_W), BF16)],
        scratch_shapes=[pltpu.VMEM((GROUP_W // LANES, tm, LANES), F32)],
        params=_params(("arbitrary",), 48), args=(h, w_in, cos_t, sin_t), rider=rider)


def _band_masks():
    qi = lax.broadcasted_iota(jnp.int32, (QBLK, QBLK), 0)
    kj = lax.broadcasted_iota(jnp.int32, (QBLK, QBLK), 1)
    return kj <= qi, kj >= qi


def _attn_tile(length):
    return min(512, length)


def _attn_fwd(qkv, dil, rider=None):
    length = qkv.shape[0]
    tq = _attn_tile(length)
    nsub = tq // QBLK
    nblk = length // tq

    def body(q_ref, k_ref, v_ref, kp_ref, vp_ref, o_ref, l_ref):
        n = pl.program_id(1)
        mask_c, mask_p0 = _band_masks()
        hmask = _head_masks((QBLK, GROUP_W))
        zero = jnp.zeros((), BF16)
        for b in range(nsub):
            rows = slice(b * QBLK, (b + 1) * QBLK)
            q = q_ref[rows, :]
            kc, vc = k_ref[rows, :], v_ref[rows, :]
            if b == 0:
                kp, vp = kp_ref[...], vp_ref[...]
                mask_p = mask_p0 & (n > 0)
            else:
                prow = slice((b - 1) * QBLK, b * QBLK)
                kp, vp = k_ref[prow, :], v_ref[prow, :]
                mask_p = mask_p0
            o_acc = jnp.zeros((QBLK, GROUP_W), F32)
            l_acc = jnp.zeros((QBLK, GROUP_W), F32)
            for h in range(HEADS_PER_GROUP):
                hm = hmask[h]
                sc = jnp.where(mask_c, _dot_nt(q, jnp.where(hm, kc, zero)) * SCALE, NEG)
                sp = jnp.where(mask_p, _dot_nt(q, jnp.where(hm, kp, zero)) * SCALE, NEG)
                m = jnp.maximum(jnp.max(sc, axis=-1, keepdims=True), jnp.max(sp, axis=-1, keepdims=True))
                pc, pp = jnp.exp(sc - m), jnp.exp(sp - m)
                den = jnp.sum(pc, axis=-1, keepdims=True) + jnp.sum(pp, axis=-1, keepdims=True)
                pv = _dot(pc.astype(BF16), jnp.where(hm, vc, zero)) + _dot(pp.astype(BF16), jnp.where(hm, vp, zero))
                o_acc = o_acc + pv / den
                l_acc = l_acc + jnp.where(hm, m + jnp.log(den), 0.0)
            o_ref[rows, :] = o_acc.astype(BF16)
            l_ref[rows, :] = l_acc

    cur = lambda sec: pl.BlockSpec((tq, GROUP_W), lambda r, n: (n, r * 3 + sec))
    prev = lambda sec: pl.BlockSpec((QBLK, GROUP_W), lambda r, n: (jnp.maximum(n * nsub - 1, 0), r * 3 + sec))
    return _call(
        body, name=f"attn_fwd_d{dil}", grid=(dil, nblk),
        in_specs=[cur(0), cur(1), cur(2), prev(1), prev(2)],
        out_specs=[pl.BlockSpec((tq, GROUP_W), lambda r, n: (n, r))] * 2,
        out_shape=[jax.ShapeDtypeStruct((length, dil * GROUP_W), BF16),
                   jax.ShapeDtypeStruct((length, dil * GROUP_W), F32)], scratch_shapes=[],
        params=_params(("arbitrary", "arbitrary"), 32), args=(qkv, qkv, qkv, qkv, qkv), rider=rider)


def _attn_bwd(qkv, dy, y, lse, cos_t, sin_t, dil, rider=None):
    length = qkv.shape[0]
    tq = _attn_tile(length)
    nsub = tq // QBLK
    nblk = length // tq

    def body(q_ref, k_ref, v_ref, kp_ref, vp_ref, qn_ref, dy_ref, y_ref, l_ref, dyn_ref, yn_ref, ln_ref,
             cos_ref, sin_ref, out_ref, dq_s, dk_s, dv_s):
        n = pl.program_id(1)
        mask_c, mask_p0 = _band_masks()
        hmask = _head_masks((QBLK, GROUP_W))
        sub = lambda ref, b: ref[b * QBLK:(b + 1) * QBLK, :]
        kbd = [_head_stack(kp_ref[...], hmask)] + [_head_stack(sub(k_ref, b), hmask) for b in range(nsub)]
        vbd = [_head_stack(vp_ref[...], hmask)] + [_head_stack(sub(v_ref, b), hmask) for b in range(nsub)]
        dq_s[...] = jnp.zeros(dq_s.shape, F32)

        def query_terms(q, dyv, yv, lv):
            prod = dyv * yv
            return dict(
                q=q, dy=dyv.astype(BF16), q_heads=[jnp.where(hm, q, jnp.zeros((), BF16)) for hm in hmask],
                dy_heads=[jnp.where(hm, dyv, 0.0).astype(BF16) for hm in hmask],
                delta=[jnp.sum(jnp.where(hm, prod, 0.0), axis=-1, keepdims=True) for hm in hmask],
                lse=[jnp.max(jnp.where(hm, lv, NEG), axis=-1, keepdims=True) for hm in hmask])

        queries = [query_terms(sub(q_ref, b), sub(dy_ref, b), sub(y_ref, b), sub(l_ref, b)) for b in range(nsub)]
        queries.append(query_terms(qn_ref[...], dyn_ref[...], yn_ref[...], ln_ref[...]))
        rows_of = lambda items: items[0] if len(items) == 1 else jnp.concatenate(items, axis=0)
        for kb in range(nsub + 1):
            seen = [(kb - 1, mask_c)] if kb >= 1 else []
            if kb == 0:
                seen.append((0, mask_p0 & (n > 0)))
            elif kb < nsub:
                seen.append((kb, mask_p0))
            else:
                seen.append((nsub, mask_p0 & (n < nblk - 1)))
            qs = [queries[b] for b, _ in seen]
            mask = rows_of([m for _, m in seen])
            s = _dot_nt(rows_of([t["q"] for t in qs]), kbd[kb]) * SCALE
            dp = _dot_nt(rows_of([t["dy"] for t in qs]), vbd[kb])
            ps, dss = [], []
            for h in range(HEADS_PER_GROUP):
                cols = slice(h * QBLK, (h + 1) * QBLK)
                p = jnp.exp(jnp.where(mask, s[:, cols] - rows_of([t["lse"][h] for t in qs]), NEG))
                ps.append(p.astype(BF16))
                dss.append((p * (dp[:, cols] - rows_of([t["delta"][h] for t in qs]))).astype(BF16))
            dq = _dot(jnp.concatenate(dss, axis=1), kbd[kb]) * SCALE
            for i, (b, _) in enumerate(seen):
                if b < nsub:
                    dq_s[b * QBLK:(b + 1) * QBLK, :] += dq[i * QBLK:(i + 1) * QBLK, :]
            if kb >= 1:
                krows = slice((kb - 1) * QBLK, kb * QBLK)
                head_rows = lambda key: jnp.concatenate([t[key][h] for h in range(HEADS_PER_GROUP) for t in qs], axis=0)
                dv_s[krows, :] = _dot_tn(jnp.concatenate(ps, axis=0), head_rows("dy_heads"))
                dk_s[krows, :] = _dot_tn(jnp.concatenate(dss, axis=0), head_rows("q_heads")) * SCALE
        cos, sin = cos_ref[...], sin_ref[...]
        dq, dk = dq_s[...], dk_s[...]
        out_ref[:, 0:GROUP_W] = (dq * cos - _rot_half(dq) * sin).astype(BF16)
        out_ref[:, GROUP_W:2 * GROUP_W] = (dk * cos - _rot_half(dk) * sin).astype(BF16)
        out_ref[:, 2 * GROUP_W:3 * GROUP_W] = dv_s[...].astype(BF16)

    cur = lambda sec: pl.BlockSpec((tq, GROUP_W), lambda r, n: (n, r * 3 + sec))
    prev = lambda sec: pl.BlockSpec((QBLK, GROUP_W), lambda r, n: (jnp.maximum(n * nsub - 1, 0), r * 3 + sec))
    nxt_q = pl.BlockSpec((QBLK, GROUP_W), lambda r, n: (jnp.minimum((n + 1) * nsub, nblk * nsub - 1), r * 3))
    tok = pl.BlockSpec((tq, GROUP_W), lambda r, n: (n, r))
    tok_next = pl.BlockSpec((QBLK, GROUP_W), lambda r, n: (jnp.minimum((n + 1) * nsub, nblk * nsub - 1), r))
    (out,), riding = _call(
        body, name=f"attn_bwd_d{dil}", grid=(dil, nblk),
        in_specs=[cur(0), cur(1), cur(2), prev(1), prev(2), nxt_q,
                  tok, tok, tok, tok_next, tok_next, tok_next, tok, tok],
        out_specs=[pl.BlockSpec((tq, 3 * GROUP_W), lambda r, n: (n, r))],
        out_shape=[jax.ShapeDtypeStruct((length, dil * 3 * GROUP_W), BF16)],
        scratch_shapes=[pltpu.VMEM((tq, GROUP_W), F32)] * 3,
        params=_params(("arbitrary", "arbitrary"), 32),
        args=(qkv, qkv, qkv, qkv, qkv, qkv, dy, y, lse, dy, y, lse, cos_t, sin_t), rider=rider)
    return out, riding


def _layernorm_stats(z):
    mu = jnp.mean(z, axis=-1, keepdims=True)
    zc = z - mu
    rstd = lax.rsqrt(jnp.mean(zc * zc, axis=-1, keepdims=True) + EPS)
    return zc * rstd, rstd


def _tril_mask():
    row = lax.broadcasted_iota(jnp.int32, (CHUNK, CHUNK), 0)
    col = lax.broadcasted_iota(jnp.int32, (CHUNK, CHUNK), 1)
    return col <= row


def _mix_fwd(o_l, rest, x, w_sp, b_col, ln_g, ln_b, w_ba, w_bg, w_out, g1, rider=None):
    seq = x.shape[0]
    tm = 256

    def body(o0, l0, o1, l1, o2, l2, up_ref, zp_ref, gap_ref, gbp_ref, x_ref, wsp_ref, bcol_ref, lg_ref, lb_ref,
             wba_ref, wbg_ref, wout_ref, g1_ref, ya0, lj0, ya1, lj1, ya2, lj2, yg_ref, mg_ref, y_ref, x1_ref, slab):
        outs = [_get_tokens(slab, o, d, GROUP_W, 0, GROUP_W) for o, d in zip((o0, o1, o2), DILATIONS)]
        lses = [_get_tokens(slab, l, d, GROUP_W, 0, GROUP_W) for l, d in zip((l0, l1, l2), DILATIONS)]
        m = jnp.maximum(jnp.maximum(lses[0], lses[1]), lses[2])
        es = [jnp.exp(l - m) for l in lses]
        tot = es[0] + es[1] + es[2]
        ya = (es[0] * outs[0] + es[1] * outs[1] + es[2] * outs[2]) / tot
        lj = m + jnp.log(tot)
        for ya_ref, lj_ref, d in zip((ya0, ya1, ya2), (lj0, lj1, lj2), DILATIONS):
            _put_residue(slab, ya, ya_ref, d, GROUP_W, 0)
            _put_residue(slab, lj, lj_ref, d, GROUP_W, 0)
        zhat, _ = _layernorm_stats(_gelu(zp_ref[...].astype(F32)))
        zln = (zhat * lg_ref[...] + lb_ref[...]).astype(BF16)
        u = _gelu(up_ref[...].astype(F32))
        tril = _tril_mask()
        for g in range(GMLP_GROUPS):
            wm = jnp.where(tril, wsp_ref[g], 0.0).astype(BF16)
            cols = slice(g * CHUNK, (g + 1) * CHUNK)
            for c in range(tm // CHUNK):
                rows = slice(c * CHUNK, (c + 1) * CHUNK)
                sz = _dot(wm, zln[rows, cols]) + bcol_ref[g]
                yg_ref[rows, cols] = (u[rows, cols] * sz).astype(BF16)
        a = _dot(ya.astype(BF16), wba_ref[...])
        bm = _dot(yg_ref[...], wbg_ref[...])
        merged = (jax.nn.sigmoid(gap_ref[...].astype(F32)) * a + jax.nn.sigmoid(gbp_ref[...].astype(F32)) * bm).astype(BF16)
        mg_ref[...] = merged
        yv = _dot(merged, wout_ref[...])
        y_ref[...] = yv.astype(BF16)
        x1_ref[...] = x_ref[...] + (yv * _rsqrt_ms(yv)) * g1_ref[...]

    tok = lambda w: pl.BlockSpec((tm, w), lambda i: (i, 0))
    res = lambda d: pl.BlockSpec((tm // d, d * GROUP_W), lambda i: (i, 0))
    full = lambda *s: pl.BlockSpec(s, lambda i: (0,) * len(s))
    res_specs = [res(d) for d in DILATIONS for _ in range(2)]
    return _call(
        body, name="mix_fwd", grid=(seq // tm,),
        in_specs=res_specs + [
            pl.BlockSpec((tm, GMLP_W), lambda i: (i, 0)), pl.BlockSpec((tm, GMLP_W), lambda i: (i, 1)),
            pl.BlockSpec((tm, D_MODEL), lambda i: (i, 1)), pl.BlockSpec((tm, D_MODEL), lambda i: (i, 2)),
            tok(D_MODEL), full(GMLP_GROUPS, CHUNK, CHUNK), full(GMLP_GROUPS, CHUNK, 1), full(1, GMLP_W), full(1, GMLP_W),
            full(GROUP_W, D_MODEL), full(GMLP_W, D_MODEL), full(D_MODEL, D_MODEL), full(1, D_MODEL)],
        out_specs=res_specs + [tok(GMLP_W), tok(D_MODEL), tok(D_MODEL), tok(D_MODEL)],
        out_shape=[jax.ShapeDtypeStruct((seq // d, d * GROUP_W), F32) for d in DILATIONS for _ in range(2)]
        + [jax.ShapeDtypeStruct((seq, GMLP_W), BF16), jax.ShapeDtypeStruct((seq, D_MODEL), BF16),
           jax.ShapeDtypeStruct((seq, D_MODEL), BF16), jax.ShapeDtypeStruct((seq, D_MODEL), F32)],
        scratch_shapes=[pltpu.VMEM((GROUP_W // LANES, tm, LANES), F32)],
        params=_params(("arbitrary",), 48),
        args=(*o_l, rest, rest, rest, rest, x, w_sp, b_col, ln_g, ln_b, w_ba, w_bg, w_out, g1), rider=rider)


def _mlp_fwd(x1, g2, g3, w_mi, w_mo, target):
    seq = x1.shape[0]
    tm, tf = MLP_FWD_TM, 512

    def body(x1_ref, g2_ref, g3_ref, wmi_ref, wmo_ref, t_ref, h2_ref, a_ref, dy2_ref, dout_ref, loss_ref, dg3_ref, sq_s):
        @pl.when(pl.program_id(0) == 0)
        def _():
            loss_ref[...] = jnp.zeros(loss_ref.shape, F32)
            dg3_ref[...] = jnp.zeros(dg3_ref.shape, F32)

        xv = x1_ref[...]
        hb = ((xv * _rsqrt_ms(xv)) * g2_ref[...]).astype(BF16)
        h2_ref[...] = hb
        for j in range(D_FF // tf):
            cols = slice(j * tf, (j + 1) * tf)
            a = jnp.maximum(_dot(hb, wmi_ref[:, cols]), 0.0)
            a_ref[:, cols] = a.astype(BF16)
            sq_s[:, cols] = (a * a).astype(BF16)
        y2 = _dot(sq_s[...], wmo_ref[...])
        r3 = _rsqrt_ms(y2)
        out = xv + (y2 * r3) * g3_ref[...]
        diff = out - t_ref[...]
        tile_loss = 0.5 * jnp.sum(jnp.mean(diff * diff, axis=-1, keepdims=True), axis=0, keepdims=True)
        loss_ref[...] += jnp.broadcast_to(tile_loss, loss_ref.shape)
        dout = diff * (1.0 / D_MODEL)
        dout_ref[...] = dout
        dy2, dg3 = _rmsnorm_bwd(dout, y2, g3_ref[...])
        dy2_ref[...] = dy2.astype(BF16)
        dg3_ref[...] += dg3

    tok = lambda w: pl.BlockSpec((tm, w), lambda i: (i, 0))
    vec = pl.BlockSpec((1, D_MODEL), lambda i: (0, 0))
    return _pallas(
        body, name="mlp_fwd", grid=(seq // tm,),
        in_specs=[tok(D_MODEL), vec, vec, _resident((D_MODEL, D_FF)), _resident((D_FF, D_MODEL)), tok(D_MODEL)],
        out_specs=[tok(D_MODEL), tok(D_FF), tok(D_MODEL), tok(D_MODEL), pl.BlockSpec((8, 128), lambda i: (0, 0)), vec],
        out_shape=[jax.ShapeDtypeStruct((seq, D_MODEL), BF16), jax.ShapeDtypeStruct((seq, D_FF), BF16),
                   jax.ShapeDtypeStruct((seq, D_MODEL), BF16), jax.ShapeDtypeStruct((seq, D_MODEL), F32),
                   jax.ShapeDtypeStruct((8, 128), F32), jax.ShapeDtypeStruct((1, D_MODEL), F32)],
        scratch_shapes=[pltpu.VMEM((tm, D_FF), BF16)],
        compiler_params=_params(("arbitrary",), 56),
    )(*map(_in_hbm, (x1, g2, g3, w_mi, w_mo, target)))


def _mlp_bwd(dy2, a, w_mo, w_mi, dout, x1, y, g2, g1, rider=None):
    seq = x1.shape[0]
    tm, tf = MLP_TM, 512

    def body(dy2_ref, a_ref, wmo_ref, wmi_ref, dout_ref, x1_ref, y_ref, g2_ref, g1_ref,
             dap_ref, dx1_ref, dy_ref, dg2_ref, dg1_ref):
        @pl.when(pl.program_id(0) == 0)
        def _():
            dg2_ref[...] = jnp.zeros(dg2_ref.shape, F32)
            dg1_ref[...] = jnp.zeros(dg1_ref.shape, F32)

        dy2v = dy2_ref[...]
        for j in range(D_FF // tf):
            cols = slice(j * tf, (j + 1) * tf)
            da2 = _dot_nt(dy2v, wmo_ref[cols, :])
            dap_ref[:, cols] = (da2 * (2.0 * a_ref[:, cols].astype(F32))).astype(BF16)
        dh2 = _dot_nt(dap_ref[...], wmi_ref[...])
        dres, dg2 = _rmsnorm_bwd(dh2, x1_ref[...], g2_ref[...])
        dx1 = dout_ref[...] + dres
        dx1_ref[...] = dx1
        dg2_ref[...] += dg2
        dyv, dg1 = _rmsnorm_bwd(dx1, y_ref[...].astype(F32), g1_ref[...])
        dy_ref[...] = dyv.astype(BF16)
        dg1_ref[...] += dg1

    tok = lambda w: pl.BlockSpec((tm, w), lambda i: (i, 0))
    vec = pl.BlockSpec((1, D_MODEL), lambda i: (0, 0))
    return _call(
        body, name="mlp_bwd", grid=(seq // tm,),
        in_specs=[tok(D_MODEL), tok(D_FF), _resident((D_FF, D_MODEL)), _resident((D_MODEL, D_FF)),
                  tok(D_MODEL), tok(D_MODEL), tok(D_MODEL), vec, vec],
        out_specs=[tok(D_FF), tok(D_MODEL), tok(D_MODEL), vec, vec],
        out_shape=[jax.ShapeDtypeStruct((seq, D_FF), BF16), jax.ShapeDtypeStruct((seq, D_MODEL), F32),
                   jax.ShapeDtypeStruct((seq, D_MODEL), BF16), jax.ShapeDtypeStruct((1, D_MODEL), F32),
                   jax.ShapeDtypeStruct((1, D_MODEL), F32)], scratch_shapes=[],
        params=_params(("arbitrary",), 56), args=(dy2, a, w_mo, w_mi, dout, x1, y, g2, g1), rider=rider)


def _tn_matmul(a, b, name, bm, bn, square_a=False, column_shards=False, rider=None):
    seq, m = a.shape
    n = b.shape[1]
    ts = 2048

    def body(a_ref, b_ref, o_ref):
        @pl.when(pl.program_id(2) == 0)
        def _():
            o_ref[...] = jnp.zeros(o_ref.shape, F32)

        av = a_ref[...]
        if square_a:
            af = av.astype(F32)
            av = (af * af).astype(BF16)
        o_ref[...] += _dot_tn(av, b_ref[...])

    if column_shards:
        out_spec = pl.BlockSpec((None, bm, bn), lambda mi, ni, s: (ni, mi, 0))
        out_shape = jax.ShapeDtypeStruct((n // bn, m, bn), F32)
    else:
        out_spec = pl.BlockSpec((bm, bn), lambda mi, ni, s: (mi, ni))
        out_shape = jax.ShapeDtypeStruct((m, n), F32)
    (out,), riding = _call(
        body, name=name, grid=(m // bm, n // bn, seq // ts),
        in_specs=[pl.BlockSpec((ts, bm), lambda mi, ni, s: (s, mi)), pl.BlockSpec((ts, bn), lambda mi, ni, s: (s, ni))],
        out_specs=[out_spec], out_shape=[out_shape], scratch_shapes=[],
        params=_params(("arbitrary", "arbitrary", "arbitrary"), 40), args=(a, b), rider=rider)
    return out, riding


def _tn_matmul_residue(a, b, dil, name):
    length = a.shape[0]
    m, n = a.shape[1] // dil, b.shape[1] // dil
    ts = min(1024, length)

    def body(a_ref, b_ref, o_ref):
        @pl.when((pl.program_id(0) == 0) & (pl.program_id(1) == 0))
        def _():
            o_ref[...] = jnp.zeros(o_ref.shape, F32)

        o_ref[...] += _dot_tn(a_ref[...], b_ref[...])

    return _pallas(
        body, name=name, grid=(dil, length // ts),
        in_specs=[pl.BlockSpec((ts, m), lambda r, s: (s, r)), pl.BlockSpec((ts, n), lambda r, s: (s, r))],
        out_specs=pl.BlockSpec((m, n), lambda r, s: (0, 0)),
        out_shape=jax.ShapeDtypeStruct((m, n), F32),
        compiler_params=_params(("arbitrary", "arbitrary"), 40),
    )(_in_hbm(a), _in_hbm(b))


def _mix_bwd(dy, ya, yg, mg, rest, w_out, w_ba, w_bg, w_sp, b_col, ln_g, ln_b, rider=None):
    seq = dy.shape[0]
    tm = 256

    def body(dy_ref, ya_ref, yg_ref, mg_ref, up_ref, zp_ref, gap_ref, gbp_ref, wout_ref, wba_ref, wbg_ref,
             wsp_ref, bcol_ref, lg_ref, lb_ref,
             dya0, dya1, dya2, dpr_ref, dwout_ref, dwba_ref, dwbg_ref, dwsp_ref, dbb_ref, dlg_ref, dlb_ref,
             dzln_s, du_s, slab):
        @pl.when(pl.program_id(0) == 0)
        def _():
            for ref in (dwout_ref, dwba_ref, dwbg_ref, dwsp_ref, dbb_ref, dlg_ref, dlb_ref):
                ref[...] = jnp.zeros(ref.shape, F32)

        dyv = dy_ref[...]
        dm = _dot_nt(dyv, wout_ref[...])
        dwout_ref[...] += _dot_tn(mg_ref[...], dyv)
        yab = ya_ref[...].astype(BF16)
        ygb = yg_ref[...]
        a = _dot(yab, wba_ref[...])
        bm = _dot(ygb, wbg_ref[...])
        ga = jax.nn.sigmoid(gap_ref[...].astype(F32))
        gb = jax.nn.sigmoid(gbp_ref[...].astype(F32))
        dpr_ref[:, 2 * GMLP_W:2 * GMLP_W + D_MODEL] = (dm * a * (ga * (1.0 - ga))).astype(BF16)
        dpr_ref[:, 2 * GMLP_W + D_MODEL:REST_W] = (dm * bm * (gb * (1.0 - gb))).astype(BF16)
        da = (dm * ga).astype(BF16)
        db = (dm * gb).astype(BF16)
        dwba = _dot_tn(yab, da)
        dwbg = _dot_tn(ygb, db)
        shard_w = D_MODEL // N_CHIPS
        for j in range(N_CHIPS):
            dwba_ref[j] += dwba[:, j * shard_w:(j + 1) * shard_w]
            dwbg_ref[j] += dwbg[:, j * shard_w:(j + 1) * shard_w]
        dya = _dot_nt(da, wba_ref[...])
        for dya_ref, d in zip((dya0, dya1, dya2), DILATIONS):
            _put_residue(slab, dya, dya_ref, d, GROUP_W, 0)
        dyg = _dot_nt(db, wbg_ref[...])

        zp = zp_ref[...].astype(F32)
        zhat, rstd = _layernorm_stats(_gelu(zp))
        lg = lg_ref[...]
        zln = (zhat * lg + lb_ref[...]).astype(BF16)
        up = up_ref[...].astype(F32)
        u = _gelu(up)
        tril = _tril_mask()
        for g in range(GMLP_GROUPS):
            wm = jnp.where(tril, wsp_ref[g], 0.0).astype(BF16)
            cols = slice(g * CHUNK, (g + 1) * CHUNK)
            for c in range(tm // CHUNK):
                rows = slice(c * CHUNK, (c + 1) * CHUNK)
                zb = zln[rows, cols]
                sz = _dot(wm, zb) + bcol_ref[g]
                dyg_cg = dyg[rows, cols]
                du_s[rows, cols] = dyg_cg * sz
                dsz = dyg_cg * u[rows, cols]
                dszb = dsz.astype(BF16)
                dbb_ref[g] += jnp.broadcast_to(jnp.sum(dsz, axis=-1, keepdims=True), (CHUNK, CHUNK))
                dwsp_ref[g] += jnp.where(tril, _dot_nt(dszb, zb), 0.0)
                dzln_s[rows, cols] = _dot_tn(wm, dszb)
        dzln = dzln_s[...]
        dlg_ref[...] += jnp.sum(dzln * zhat, axis=0, keepdims=True)
        dlb_ref[...] += jnp.sum(dzln, axis=0, keepdims=True)
        dzh = dzln * lg
        dz = rstd * (dzh - jnp.mean(dzh, axis=-1, keepdims=True) - zhat * jnp.mean(dzh * zhat, axis=-1, keepdims=True))
        dpr_ref[:, GMLP_W:2 * GMLP_W] = (dz * _gelu_grad(zp)).astype(BF16)
        dpr_ref[:, 0:GMLP_W] = (du_s[...] * _gelu_grad(up)).astype(BF16)

    tok = lambda w: pl.BlockSpec((tm, w), lambda i: (i, 0))
    full = lambda *s: pl.BlockSpec(s, lambda i: (0,) * len(s))
    return _call(
        body, name="mix_bwd", grid=(seq // tm,),
        in_specs=[tok(D_MODEL), tok(GROUP_W), tok(GMLP_W), tok(D_MODEL),
                  pl.BlockSpec((tm, GMLP_W), lambda i: (i, 0)), pl.BlockSpec((tm, GMLP_W), lambda i: (i, 1)),
                  pl.BlockSpec((tm, D_MODEL), lambda i: (i, 1)), pl.BlockSpec((tm, D_MODEL), lambda i: (i, 2)),
                  full(D_MODEL, D_MODEL), full(GROUP_W, D_MODEL), full(GMLP_W, D_MODEL),
                  full(GMLP_GROUPS, CHUNK, CHUNK), full(GMLP_GROUPS, CHUNK, 1), full(1, GMLP_W), full(1, GMLP_W)],
        out_specs=[pl.BlockSpec((tm // d, d * GROUP_W), lambda i: (i, 0)) for d in DILATIONS]
        + [tok(REST_W), full(D_MODEL, D_MODEL), full(N_CHIPS, GROUP_W, D_MODEL // N_CHIPS),
           full(N_CHIPS, GMLP_W, D_MODEL // N_CHIPS),
           full(GMLP_GROUPS, CHUNK, CHUNK), full(GMLP_GROUPS, CHUNK, CHUNK), full(1, GMLP_W), full(1, GMLP_W)],
        out_shape=[jax.ShapeDtypeStruct((seq // d, d * GROUP_W), F32) for d in DILATIONS]
        + [jax.ShapeDtypeStruct((seq, REST_W), BF16),
           jax.ShapeDtypeStruct((D_MODEL, D_MODEL), F32), jax.ShapeDtypeStruct((N_CHIPS, GROUP_W, D_MODEL // N_CHIPS), F32),
           jax.ShapeDtypeStruct((N_CHIPS, GMLP_W, D_MODEL // N_CHIPS), F32),
           jax.ShapeDtypeStruct((GMLP_GROUPS, CHUNK, CHUNK), F32),
           jax.ShapeDtypeStruct((GMLP_GROUPS, CHUNK, CHUNK), F32), jax.ShapeDtypeStruct((1, GMLP_W), F32),
           jax.ShapeDtypeStruct((1, GMLP_W), F32)],
        scratch_shapes=[pltpu.VMEM((tm, GMLP_W), F32), pltpu.VMEM((tm, GMLP_W), F32),
                        pltpu.VMEM((GROUP_W // LANES, tm, LANES), F32)],
        params=_params(("arbitrary",), 56),
        args=(dy, ya, yg, mg, rest, rest, rest, rest, w_out, w_ba, w_bg, w_sp, b_col, ln_g, ln_b), rider=rider)


IN_PROJ_BWD_TM = 256


def _in_proj_bwd(dqkv, drest, w_in, x, dx1, g0, so_far, span, rider=None):
    seq = x.shape[0]
    tm = IN_PROJ_BWD_TM
    off, steps = span
    gx_so_far, dg_so_far = so_far

    def body(d0, d1, d2, dr_ref, w_ref, x_ref, dx1_ref, g_ref, dg_in_ref, gx_in_ref, gx_ref, dg_ref, slab):
        @pl.when(pl.program_id(0) == 0)
        def _():
            dg_ref[...] = dg_in_ref[...]

        dh = _dot(dr_ref[...], w_ref[QKV_W:, :])
        for g, (d_ref, dil) in enumerate(zip((d0, d1, d2), DILATIONS)):
            piece = d_ref[...] if dil == 1 else _get_tokens(slab, d_ref, dil, 3 * GROUP_W, 0, 3 * GROUP_W).astype(BF16)
            for section, (lo, hi) in enumerate(_qkv_columns(g)):
                dh = dh + _dot(piece[:, section * GROUP_W:(section + 1) * GROUP_W], w_ref[lo:hi, :])
        dres, dg = _rmsnorm_bwd(dh, x_ref[...], g_ref[...])
        gx_ref[...] = dx1_ref[...] + dres
        dg_ref[...] += dg

    tok = lambda w: pl.BlockSpec((tm, w), lambda i: (i + off, 0))
    full = lambda *s: pl.BlockSpec(s, lambda i: (0,) * len(s))
    in_specs = ([pl.BlockSpec((tm // d, d * 3 * GROUP_W), lambda i: (i + off, 0)) for d in DILATIONS] + [tok(REST_W)]
                + [_resident((IN_W, D_MODEL))]
                + [tok(D_MODEL), tok(D_MODEL), full(1, D_MODEL), full(1, D_MODEL), HBM_SPEC])
    return _call(
        body, name=f"in_proj_bwd_{off}", grid=(steps,), in_specs=in_specs,
        out_specs=[tok(D_MODEL), full(1, D_MODEL)],
        out_shape=[jax.ShapeDtypeStruct((seq, D_MODEL), F32), jax.ShapeDtypeStruct((1, D_MODEL), F32)],
        scratch_shapes=[pltpu.VMEM((3 * GROUP_W // LANES, tm, LANES), F32)],
        params=_params(("arbitrary",), 48), args=(*dqkv, drest, w_in, x, dx1, g0, dg_so_far, gx_so_far),
        rider=rider, aliases={len(in_specs) - 1: 0})


def _adamw(w, g, m, v, name):
    rows, cols = w.shape
    tr = _row_tile(rows) if rows % 16 == 0 else rows
    c1 = 1.0 - ADAM_B1 ** ADAM_STEP
    c2 = 1.0 - ADAM_B2 ** ADAM_STEP

    def body(w_ref, g_ref, m_ref, v_ref, go_ref, d_ref, nm_ref, nv_ref):
        gv = g_ref[...]
        go_ref[...] = gv
        nm = ADAM_B1 * m_ref[...] + (1.0 - ADAM_B1) * gv
        nv = ADAM_B2 * v_ref[...] + (1.0 - ADAM_B2) * (gv * gv)
        d_ref[...] = -ADAM_LR * ((nm / c1) / (jnp.sqrt(nv / c2) + ADAM_EPS) + ADAM_WD * w_ref[...])
        nm_ref[...] = nm
        nv_ref[...] = nv

    spec = pl.BlockSpec((tr, cols), lambda i: (i, 0))
    return _pallas(
        body, name=name, grid=(rows // tr,),
        in_specs=[spec] * 4, out_specs=[spec] * 4,
        out_shape=[jax.ShapeDtypeStruct((rows, cols), F32)] * 4,
        compiler_params=_params(("arbitrary",), 32, small=True),
    )(w, g, m, v)


def _place():
    x, y, c = lax.axis_index("x"), lax.axis_index("y"), lax.axis_index("c")
    chips = [(1 - x, y), (x, 1 - y), (1 - x, 1 - y)]
    return x, y, c, chips


class _Exchange:
    def __init__(self, inputs, out_shapes, n_sems, start, finish, aliases=None):
        self.inputs, self.out_shapes, self.n_sems = list(inputs), list(out_shapes), n_sems
        self.start, self.finish, self.aliases = start, finish, dict(aliases or {})

    def scratch(self):
        return [pltpu.SemaphoreType.DMA((self.n_sems,)), pltpu.SemaphoreType.DMA((self.n_sems,))]


def _together(*parts):
    ins = [len(p.inputs) for p in parts]
    outs = [len(p.out_shapes) for p in parts]

    def split(refs, counts):
        pos, pieces = 0, []
        for cnt in counts:
            pieces.append(refs[pos:pos + cnt])
            pos += cnt
        return pieces

    def run(which):
        def go(in_refs, out_refs, *sems):
            for k, (p, i, o) in enumerate(zip(parts, split(in_refs, ins), split(out_refs, outs))):
                getattr(p, which)(i, o, sems[2 * k], sems[2 * k + 1])
        return go

    both = _Exchange([a for p in parts for a in p.inputs], [s for p in parts for s in p.out_shapes], 0, run("start"),
                     run("finish"))
    both.aliases = {sum(ins[:k]) + i: sum(outs[:k]) + o for k, p in enumerate(parts) for i, o in p.aliases.items()}
    both.scratch = lambda: [s for p in parts for s in p.scratch()]
    return both


def _run_exchange(ex, name):
    n_in, n_out = len(ex.inputs), len(ex.out_shapes)

    def body(*refs):
        ins, outs, sems = refs[:n_in], refs[n_in:n_in + n_out], refs[n_in + n_out:]
        ex.start(ins, outs, *sems)
        ex.finish(ins, outs, *sems)

    return _pallas(
        body, name=name, in_specs=[HBM_SPEC] * n_in, out_specs=[HBM_SPEC] * n_out, out_shape=ex.out_shapes,
        scratch_shapes=ex.scratch(), input_output_aliases=ex.aliases,
    )(*ex.inputs)


def _call(body, *, name, grid, in_specs, out_specs, out_shape, scratch_shapes, params, args, rider=None, aliases=None):
    in_specs, out_specs, out_shape, scratch_shapes = list(in_specs), list(out_specs), list(out_shape), list(scratch_shapes)
    aliases = dict(aliases or {})
    args = [_in_hbm(a) for a in args]
    if rider is None:
        outs = _pallas(body, name=name, grid=grid, in_specs=in_specs, out_specs=out_specs, out_shape=out_shape,
                              scratch_shapes=scratch_shapes, input_output_aliases=aliases, compiler_params=params)(*args)
        return list(outs), []
    n_in, n_out, n_scr = len(in_specs), len(out_specs), len(scratch_shapes)
    r_in, r_out = len(rider.inputs), len(rider.out_shapes)

    def wrapped(*refs):
        ins, r_ins = refs[:n_in], refs[n_in:n_in + r_in]
        pos = n_in + r_in
        outs, r_outs = refs[pos:pos + n_out], refs[pos + n_out:pos + n_out + r_out]
        pos += n_out + r_out
        scr, sems = refs[pos:pos + n_scr], refs[pos + n_scr:]
        ids = [pl.program_id(k) for k in range(len(grid))]
        first, last = ids[0] == 0, ids[0] == grid[0] - 1
        for k in range(1, len(grid)):
            first, last = first & (ids[k] == 0), last & (ids[k] == grid[k] - 1)

        @pl.when(first)
        def _():
            rider.start(r_ins, r_outs, *sems)

        body(*ins, *outs, *scr)

        @pl.when(last)
        def _():
            rider.finish(r_ins, r_outs, *sems)

    outs = _pallas(
        wrapped, name=name, grid=grid, in_specs=in_specs + [HBM_SPEC] * r_in, out_specs=out_specs + [HBM_SPEC] * r_out,
        out_shape=out_shape + rider.out_shapes, scratch_shapes=scratch_shapes + rider.scratch(),
        input_output_aliases={**aliases, **{n_in + i: n_out + o for i, o in rider.aliases.items()}}, compiler_params=params,
    )(*args, *rider.inputs)
    return list(outs[:n_out]), list(outs[n_out:])


def _stage_weights(shards):
    n = len(shards)

    def body(*refs):
        ins, outs, stages, sems = refs[:n], refs[n:2 * n], refs[2 * n:3 * n], refs[3 * n]
        x, y, _, _ = _place()
        copies = []
        for t in range(n):
            stages[t][...] = ins[t][...].astype(BF16)
            copies.append(pltpu.make_async_copy(stages[t], outs[t].at[2 * x + y], sems.at[t]))
            copies[-1].start()
        for cp in copies:
            cp.wait()

    assert sum(s.size * 6 for s in shards) <= (CALL_VMEM_MIB - 8) * MIB
    return _pallas(
        body, name="stage_weights", in_specs=[VMEM_SPEC] * n, out_specs=[HBM_SPEC] * n,
        out_shape=[jax.ShapeDtypeStruct((N_CHIPS,) + s.shape, BF16) for s in shards],
        scratch_shapes=[pltpu.VMEM(s.shape, BF16) for s in shards] + [pltpu.SemaphoreType.DMA((n,))],
        compiler_params=pltpu.CompilerParams(vmem_limit_bytes=SMALL_VMEM_MIB * MIB),
    )(*shards)


def _gather(buffers, stage="both", part=(0, 1)):
    n = len(buffers)
    halves = [b.shape[1] // part[1] // 2 for b in buffers]

    def half_of(outs, t, chip, which):
        return outs[t].at[chip, pl.ds((2 * part[0] + which) * halves[t], halves[t]), :]

    def copy(outs, sems, t, k, chip, which, to):
        rows = half_of(outs, t, chip, which)
        return pltpu.make_async_remote_copy(src_ref=rows, dst_ref=rows, send_sem=sems[0].at[6 * t + k],
                                            recv_sem=sems[1].at[6 * t + k], device_id=to, device_id_type=MESH)

    def to_chips(outs, sems, what):
        x, y, c, chips = _place()
        for t in range(n):
            for j, (px, py) in enumerate(chips):
                if what == "start":
                    copy(outs, sems, t, j, 2 * x + y, c, (px, py, c)).start()
                else:
                    copy(outs, sems, t, j, 2 * px + py, c, (px, py, c)).wait_recv()
                    copy(outs, sems, t, j, 2 * x + y, c, (px, py, c)).wait_send()

    def to_sibling(outs, sems, what):
        x, y, c, chips = _place()
        for t in range(n):
            for j, (px, py) in enumerate(chips):
                if what == "start":
                    copy(outs, sems, t, 3 + j, 2 * px + py, c, (x, y, 1 - c)).start()
                else:
                    copy(outs, sems, t, 3 + j, 2 * px + py, 1 - c, (x, y, 1 - c)).wait_recv()
                    copy(outs, sems, t, 3 + j, 2 * px + py, c, (x, y, 1 - c)).wait_send()

    def start(ins, outs, *sems):
        (to_sibling if stage == "pair" else to_chips)(outs, sems, "start")

    def finish(ins, outs, *sems):
        if stage == "both":
            x, y, c, chips = _place()
            for j, (px, py) in enumerate(chips):
                for t in range(n):
                    copy(outs, sems, t, j, 2 * px + py, c, (px, py, c)).wait_recv()
                    copy(outs, sems, t, 3 + j, 2 * px + py, c, (x, y, 1 - c)).start()
            for j, (px, py) in enumerate(chips):
                for t in range(n):
                    copy(outs, sems, t, j, 2 * x + y, c, (px, py, c)).wait_send()
            to_sibling(outs, sems, "finish")
        elif stage == "chips":
            to_chips(outs, sems, "finish")
        else:
            to_sibling(outs, sems, "finish")

    return _Exchange(buffers, [jax.ShapeDtypeStruct(b.shape, b.dtype) for b in buffers], 6 * n, start, finish,
                     aliases={t: t for t in range(n)})


def _pair_exchange(grads):
    n = len(grads)
    halves = [g.shape[1] // 2 for g in grads]

    def copies(ins, outs, send_sems, recv_sems):
        x, y, c, _ = _place()
        return [pltpu.make_async_remote_copy(
            src_ref=ins[t].at[:, pl.ds((1 - c) * halves[t], halves[t]), :], dst_ref=outs[t],
            send_sem=send_sems.at[t], recv_sem=recv_sems.at[t], device_id=(x, y, 1 - c), device_id_type=MESH)
            for t in range(n)]

    def start(*refs):
        for cp in copies(*refs):
            cp.start()

    def finish(*refs):
        for cp in copies(*refs):
            cp.wait()

    return _Exchange(grads, [jax.ShapeDtypeStruct((N_CHIPS, h, g.shape[2]), F32) for g, h in zip(grads, halves)], n,
                     start, finish)


def _row_tile(rows):
    return max(t for t in range(16, 257, 16) if rows % t == 0)


def _pair_add(grad, other, place, name):
    _, rows, cols = grad.shape
    rh = rows // 2
    tr = _row_tile(rh)
    nb = rh // tr

    def body(p_ref, g_ref, a_ref, wire_ref, own_ref):
        s = g_ref[...] + a_ref[...]
        wire_ref[...] = s.astype(BF16)

        @pl.when(pl.program_id(1) == p_ref[1])
        def _():
            own_ref[...] = s

    blk = (None, tr, cols)
    return _pallas(
        body, name=name,
        grid_spec=pltpu.PrefetchScalarGridSpec(
            num_scalar_prefetch=1, grid=(nb, N_CHIPS),
            in_specs=[pl.BlockSpec(blk, lambda i, j, p: (j, p[0] * nb + i, 0)), pl.BlockSpec(blk, lambda i, j, p: (j, i, 0))],
            out_specs=[pl.BlockSpec(blk, lambda i, j, p: (j, i, 0)), pl.BlockSpec((tr, cols), lambda i, j, p: (i, 0))]),
        out_shape=[jax.ShapeDtypeStruct((N_CHIPS, rh, cols), BF16), jax.ShapeDtypeStruct((rh, cols), F32)],
        compiler_params=_params(("arbitrary", "arbitrary"), 32, small=True),
    )(place, grad, other)


def _chip_exchange(wires):
    n = len(wires)

    def copies(ins, outs, send_sems, recv_sems):
        x, y, c, chips = _place()
        return [pltpu.make_async_remote_copy(
            src_ref=ins[t].at[2 * px + py], dst_ref=outs[t].at[j], send_sem=send_sems.at[3 * t + j],
            recv_sem=recv_sems.at[3 * t + j], device_id=(px, py, c), device_id_type=MESH)
            for t in range(n) for j, (px, py) in enumerate(chips)]

    def start(*refs):
        for cp in copies(*refs):
            cp.start()

    def finish(*refs):
        for cp in copies(*refs):
            cp.wait()

    return _Exchange(wires, [jax.ShapeDtypeStruct((3,) + w.shape[1:], BF16) for w in wires], 3 * n, start, finish)


def _chip_add(own, arrived, place, name):
    rh, cols = own.shape
    tr = _row_tile(rh)
    nb = rh // tr

    def body(p_ref, s_ref, b0, b1, b2, o_ref):
        o_ref[...] = ((s_ref[...] + b0[...].astype(F32)) + b1[...].astype(F32)) + b2[...].astype(F32)

    blk = (None, tr, cols)
    return _pallas(
        body, name=name,
        grid_spec=pltpu.PrefetchScalarGridSpec(
            num_scalar_prefetch=1, grid=(nb,),
            in_specs=[pl.BlockSpec((tr, cols), lambda i, p: (i, 0)), pl.BlockSpec(blk, lambda i, p: (0, i, 0)),
                      pl.BlockSpec(blk, lambda i, p: (1, i, 0)), pl.BlockSpec(blk, lambda i, p: (2, i, 0))],
            out_specs=pl.BlockSpec((tr, cols), lambda i, p: (p[0] * nb + i, 0))),
        out_shape=jax.ShapeDtypeStruct((2 * rh, cols), F32),
        compiler_params=_params(("arbitrary",), 32, small=True),
    )(place, own, arrived, arrived, arrived)


def _pair_share(halves):
    n = len(halves)
    rhs = [h.shape[0] // 2 for h in halves]

    def copy(outs, send_sems, recv_sems, t, which):
        x, y, c, _ = _place()
        rows = outs[t].at[pl.ds(which * rhs[t], rhs[t]), :]
        return pltpu.make_async_remote_copy(src_ref=rows, dst_ref=rows, send_sem=send_sems.at[t], recv_sem=recv_sems.at[t],
                                            device_id=(x, y, 1 - c), device_id_type=MESH)

    def start(ins, outs, send_sems, recv_sems):
        c = lax.axis_index("c")
        for t in range(n):
            copy(outs, send_sems, recv_sems, t, c).start()

    def finish(ins, outs, send_sems, recv_sems):
        c = lax.axis_index("c")
        for t in range(n):
            copy(outs, send_sems, recv_sems, t, c).wait_send()
            copy(outs, send_sems, recv_sems, t, 1 - c).wait_recv()

    return _Exchange(halves, [jax.ShapeDtypeStruct(h.shape, F32) for h in halves], n, start, finish,
                     aliases={t: t for t in range(n)})


class _GradReduction:
    def __init__(self, grads, place, tag):
        self.names, self.grads, self.place, self.tag = list(grads), grads, place, tag

    def pair_exchange(self):
        return _pair_exchange([self.grads[n] for n in self.names])

    def chip_exchange(self, others):
        sums = [_pair_add(self.grads[n], o, self.place, f"{self.tag}_pair_add_{n}") for n, o in zip(self.names, others)]
        self.owns = [own for _, own in sums]
        return _chip_exchange([wire for wire, _ in sums])

    def pair_share(self, arrived):
        return _pair_share([_chip_add(own, arr, self.place, f"{self.tag}_chip_add_{n}")
                            for n, own, arr in zip(self.names, self.owns, arrived)])

    def result(self, shared):
        return dict(zip(self.names, shared))


def _all_reduce_small(p):
    rows, lanes = p.shape
    half = rows // 2

    def body(p_ref, o_ref, sib, sums, send_sems, recv_sems):
        x, y, c, chips = _place()
        mine, sibling = 2 * x + y, (x, y, 1 - c)
        swap = pltpu.make_async_remote_copy(src_ref=p_ref, dst_ref=sib, send_sem=send_sems.at[0], recv_sem=recv_sems.at[0],
                                            device_id=sibling, device_id_type=MESH)
        swap.start()
        swap.wait()
        sums[mine] = p_ref[...] + sib[...]

        def copy(k, chip, which, to):
            part = sums.at[chip, pl.ds(which * half, half), :]
            return pltpu.make_async_remote_copy(src_ref=part, dst_ref=part, send_sem=send_sems.at[k], recv_sem=recv_sems.at[k],
                                                device_id=to, device_id_type=MESH)

        for j, (px, py) in enumerate(chips):
            copy(1 + j, mine, c, (px, py, c)).start()
        for j, (px, py) in enumerate(chips):
            copy(1 + j, 2 * px + py, c, (px, py, c)).wait_recv()
            copy(4 + j, 2 * px + py, c, sibling).start()
        for j, (px, py) in enumerate(chips):
            copy(4 + j, 2 * px + py, 1 - c, sibling).wait_recv()
        for j, (px, py) in enumerate(chips):
            copy(1 + j, mine, c, (px, py, c)).wait_send()
            copy(4 + j, 2 * px + py, c, sibling).wait_send()
        o_ref[...] = ((sums[0] + sums[1]) + sums[2]) + sums[3]

    return _pallas(
        body, name="small_all_reduce", in_specs=[VMEM_SPEC], out_specs=VMEM_SPEC,
        out_shape=jax.ShapeDtypeStruct((rows, lanes), F32),
        scratch_shapes=[pltpu.VMEM((rows, lanes), F32), pltpu.VMEM((N_CHIPS, rows, lanes), F32),
                        pltpu.SemaphoreType.DMA((7,)), pltpu.SemaphoreType.DMA((7,))],
        compiler_params=pltpu.CompilerParams(vmem_limit_bytes=SMALL_VMEM_MIB * MIB),
    )(p)


BIG = ("w_in", "w_branch_attn", "w_branch_gmlp", "w_out", "w_mlp_in", "w_mlp_out")
COLUMN_SHARDED = ("w_branch_attn", "w_branch_gmlp", "w_mlp_in")
SMALL = ("norm_pre_mix", "w_spatial", "b_spatial", "ln_v_gain", "ln_v_bias", "norm_post_mix", "norm_pre_mlp", "norm_post_mlp")
ORDER = ("norm_pre_mix", "w_in", "w_spatial", "b_spatial", "ln_v_gain", "ln_v_bias", "w_branch_attn", "w_branch_gmlp",
         "w_out", "norm_post_mix", "norm_pre_mlp", "w_mlp_in", "w_mlp_out", "norm_post_mlp")


def _full_weight(name, gathered):
    if name in COLUMN_SHARDED:
        return jnp.transpose(gathered, (1, 0, 2)).reshape(gathered.shape[1], -1)
    return gathered.reshape(-1, gathered.shape[2])


def _rows8(a):
    a = a.reshape(-1, 128)
    pad = (-a.shape[0]) % 8
    return jnp.pad(a, ((0, pad), (0, 0))) if pad else a


def _qkv_columns(group):
    return [(sec * ATTN_W + group * GROUP_W, sec * ATTN_W + (group + 1) * GROUP_W) for sec in range(3)]


def _device_step(x, target, small, shards, place):
    seq = x.shape[0]
    g0, g1, g2, g3 = small["norm_pre_mix"], small["norm_post_mix"], small["norm_pre_mlp"], small["norm_post_mlp"]
    w_sp = small["w_spatial"]
    b_col = small["b_spatial"].reshape(GMLP_GROUPS, CHUNK, 1)
    ln_g, ln_b = small["ln_v_gain"], small["ln_v_bias"]

    staged = _stage_weights(shards)
    h, tables, (w_in,) = _prepare(x, g0, rider=_gather(staged[:1]))
    w_in = _full_weight("w_in", w_in)
    (*qkv, rest), landed = _in_proj(h[0], w_in, *tables[1], rider=_gather(staged[1:], "chips"))

    o_l, gathered = _attn_fwd(qkv[0], DILATIONS[0], rider=_gather(landed, "pair"))
    full = {n: _full_weight(n, gw) for n, gw in zip(BIG[1:], gathered)}
    for g in range(1, N_GROUPS):
        o_l.extend(_attn_fwd(qkv[g], DILATIONS[g])[0])
    (*ya_l, yg, mg, y, x1), _ = _mix_fwd(o_l, rest, x, w_sp, b_col, ln_g, ln_b, full["w_branch_attn"],
                                        full["w_branch_gmlp"], full["w_out"], g1)
    ya, lse = ya_l[0::2], ya_l[1::2]
    h2, a, dy2, dout, loss8, dg3 = _mlp_fwd(x1, g2, g3, full["w_mlp_in"], full["w_mlp_out"], target)
    d_wmo, _ = _tn_matmul(a, dy2, "grad_w_mlp_out", 1024, 1024, square_a=True)
    mlp_out = _GradReduction({"w_mlp_out": d_wmo.reshape(N_CHIPS, D_FF // N_CHIPS, D_MODEL)}, place, "mlp_out")
    (dap, dx1, dy, dg2, dg1), riding = _mlp_bwd(dy2, a, full["w_mlp_out"], full["w_mlp_in"], dout, x1, y, g2, g1,
                                                 rider=mlp_out.pair_exchange())
    d_wmi, riding = _tn_matmul(h2, dap, "grad_w_mlp_in", 1024, 1024, column_shards=True,
                               rider=mlp_out.chip_exchange(riding))
    mlp_in = _GradReduction({"w_mlp_in": d_wmi}, place, "mlp_in")
    (*dya, drest, d_wout, d_wba, d_wbg, d_wsp, d_bb, d_lg, d_lb), riding = _mix_bwd(
        dy, ya[0], yg, mg, rest, full["w_out"], full["w_branch_attn"], full["w_branch_gmlp"], w_sp, b_col, ln_g, ln_b,
        rider=_together(mlp_out.pair_share(riding), mlp_in.pair_exchange()))
    reduced = mlp_out.result(riding[:1])
    mix = _GradReduction({"w_branch_attn": d_wba, "w_branch_gmlp": d_wbg,
                          "w_out": d_wout.reshape(N_CHIPS, D_MODEL // N_CHIPS, D_MODEL)}, place, "mix")
    attn = lambda g, rider: _attn_bwd(qkv[g], dya[g], ya[g], lse[g], *tables[DILATIONS[g]], DILATIONS[g], rider=rider)
    dqkv0, riding = attn(0, _together(mlp_in.chip_exchange(riding[1:]), mix.pair_exchange()))
    dqkv1, riding = attn(1, _together(mlp_in.pair_share(riding[:1]), mix.chip_exchange(riding[1:])))
    reduced.update(mlp_in.result(riding[:1]))
    dqkv2, riding = attn(2, mix.pair_share(riding[1:]))
    reduced.update(mix.result(riding))
    dqkv = [dqkv0, dqkv1, dqkv2]

    d_qkv = [_tn_matmul_residue(dqkv[g], h[g], dil, f"grad_w_in_qkv{g}") for g, dil in enumerate(DILATIONS)]
    d_rest, _ = _tn_matmul(drest, h[0], "grad_w_in_rest", 1024, 1024)
    d_win = jnp.concatenate([d_qkv[g][s * GROUP_W:(s + 1) * GROUP_W] for s in range(3) for g in range(N_GROUPS)]
                            + [d_rest], axis=0)
    first = _GradReduction({"w_in": d_win.reshape(N_CHIPS, IN_W // N_CHIPS, D_MODEL)}, place, "w_in")
    tiles = seq // IN_PROJ_BWD_TM
    so_far = (lax.empty((seq, D_MODEL), F32), jnp.zeros((1, D_MODEL), F32))
    in_bwd = lambda so_far, span, rider: _in_proj_bwd(dqkv, drest, w_in, x, dx1, g0, so_far, span, rider=rider)
    so_far, riding = in_bwd(so_far, (0, 3 * tiles // 8), first.pair_exchange())
    (grad_x, dg0), riding = in_bwd(so_far, (3 * tiles // 8, 5 * tiles // 8), first.chip_exchange(riding))
    reduced.update(first.result(_run_exchange(first.pair_share(riding), "w_in_pair_share")))
    little = {"norm_pre_mix": dg0, "w_spatial": d_wsp, "b_spatial": d_bb[:, :, 0], "ln_v_gain": d_lg, "ln_v_bias": d_lb,
              "norm_post_mix": dg1, "norm_pre_mlp": dg2, "norm_post_mlp": dg3}
    return loss8, grad_x, reduced, little


def kernel(x, norm_pre_mix, w_in, w_spatial, b_spatial, ln_v_gain, ln_v_bias, w_branch_attn, w_branch_gmlp, w_out, norm_post_mix, norm_pre_mlp, w_mlp_in, w_mlp_out, norm_post_mlp, loss_target, m_norm_pre_mix, m_w_in, m_w_spatial, m_b_spatial, m_ln_v_gain, m_ln_v_bias, m_w_branch_attn, m_w_branch_gmlp, m_w_out, m_norm_post_mix, m_norm_pre_mlp, m_w_mlp_in, m_w_mlp_out, m_norm_post_mlp, v_norm_pre_mix, v_w_in, v_w_spatial, v_b_spatial, v_ln_v_gain, v_ln_v_bias, v_w_branch_attn, v_w_branch_gmlp, v_w_out, v_norm_post_mix, v_norm_pre_mlp, v_w_mlp_in, v_w_mlp_out, v_norm_post_mlp):
    given = dict(norm_pre_mix=norm_pre_mix, w_in=w_in, w_spatial=w_spatial, b_spatial=b_spatial, ln_v_gain=ln_v_gain,
                 ln_v_bias=ln_v_bias, w_branch_attn=w_branch_attn, w_branch_gmlp=w_branch_gmlp, w_out=w_out,
                 norm_post_mix=norm_post_mix, norm_pre_mlp=norm_pre_mlp, w_mlp_in=w_mlp_in, w_mlp_out=w_mlp_out,
                 norm_post_mlp=norm_post_mlp)
    moments_m = dict(norm_pre_mix=m_norm_pre_mix, w_in=m_w_in, w_spatial=m_w_spatial, b_spatial=m_b_spatial,
                     ln_v_gain=m_ln_v_gain, ln_v_bias=m_ln_v_bias, w_branch_attn=m_w_branch_attn,
                     w_branch_gmlp=m_w_branch_gmlp, w_out=m_w_out, norm_post_mix=m_norm_post_mix,
                     norm_pre_mlp=m_norm_pre_mlp, w_mlp_in=m_w_mlp_in, w_mlp_out=m_w_mlp_out, norm_post_mlp=m_norm_post_mlp)
    moments_v = dict(norm_pre_mix=v_norm_pre_mix, w_in=v_w_in, w_spatial=v_w_spatial, b_spatial=v_b_spatial,
                     ln_v_gain=v_ln_v_gain, ln_v_bias=v_ln_v_bias, w_branch_attn=v_w_branch_attn,
                     w_branch_gmlp=v_w_branch_gmlp, w_out=v_w_out, norm_post_mix=v_norm_post_mix,
                     norm_pre_mlp=v_norm_pre_mlp, w_mlp_in=v_w_mlp_in, w_mlp_out=v_w_mlp_out, norm_post_mlp=v_norm_post_mlp)
    cx, cy, cc = lax.axis_index("x"), lax.axis_index("y"), lax.axis_index("c")

    shards = [given[n][0].T if n == "w_in" else given[n][0] for n in BIG]
    small = {n: given[n][0] if given[n].ndim > 2 else given[n] for n in SMALL}
    place = jnp.stack([cc, 2 * cx + cy]).astype(jnp.int32)
    loss8, grad_x, grad_shard, grads = _device_step(x[0], loss_target[0], small, shards, place)

    packed = jnp.concatenate([_rows8(grads[n]) for n in SMALL] + [loss8], axis=0)
    summed = _all_reduce_small(packed)
    loss = summed[packed.shape[0] - loss8.shape[0], 0]
    row = 0
    for n in SMALL:
        shape = given[n][0].shape
        cnt = -(-(given[n][0].size // 128) // 8) * 8
        grad_shard[n] = summed[row:row + given[n][0].size // 128].reshape(shape)
        row += cnt

    grad_out, deltas, new_m, new_v = {}, {}, {}, {}
    for n in ORDER:
        shape = given[n].shape
        if n == "w_in":
            outs = _adamw(given[n][0].T, grad_shard[n], moments_m[n][0].T, moments_v[n][0].T, "adamw_" + n)
            outs = [o.T for o in outs]
        else:
            two_d = (-1, shape[-1])
            outs = _adamw(given[n].reshape(two_d), grad_shard[n].reshape(two_d), moments_m[n].reshape(two_d),
                          moments_v[n].reshape(two_d), "adamw_" + n)
        grad_out[n], deltas[n], new_m[n], new_v[n] = [o.reshape(shape) for o in outs]
    return (loss, grad_x[None], *[grad_out[n] for n in ORDER], *[deltas[n] for n in ORDER], *[new_m[n] for n in ORDER],
            *[new_v[n] for n in ORDER])
```

```python
import math

import jax
import jax.numpy as jnp
from jax import lax
from jax.experimental import pallas as pl
from jax.experimental.pallas import tpu as pltpu

F32 = jnp.float32
BF16 = jnp.bfloat16
MESH = pl.DeviceIdType.MESH

D_MODEL = 1024
HEAD_DIM = 64
HEADS_PER_GROUP = 4
GROUP_W = HEADS_PER_GROUP * HEAD_DIM
DILATIONS = (1, 4, 16)
N_GROUPS = len(DILATIONS)
ATTN_W = N_GROUPS * GROUP_W
QKV_W = 3 * ATTN_W
GMLP_W = 512
GMLP_GROUPS = 4
CHUNK = 128
REST_W = 2 * GMLP_W + 2 * D_MODEL
IN_W = QKV_W + REST_W
D_FF = 4096
QBLK = 128
ROPE_THETA = 10000.0
EPS = 1e-6
NEG = -1e30
SCALE = HEAD_DIM ** -0.5
N_CHIPS = 4

ADAM_LR = 0.001
ADAM_B1 = 0.9
ADAM_B2 = 0.999
ADAM_EPS = 1e-08
ADAM_WD = 0.01
ADAM_STEP = 10

MIB = 1024 * 1024
HBM_SPEC = pl.BlockSpec(memory_space=pltpu.HBM)
VMEM_SPEC = pl.BlockSpec(memory_space=pltpu.VMEM)


MLP_FWD_TM = 512
MLP_TM = 256


CALL_VMEM_MIB = 56
SMALL_VMEM_MIB = 32


def _params(semantics, vmem_mib, small=False):
    assert vmem_mib <= CALL_VMEM_MIB
    return pltpu.CompilerParams(dimension_semantics=semantics,
                                vmem_limit_bytes=(SMALL_VMEM_MIB if small else CALL_VMEM_MIB) * MIB)


def _in_hbm(a):
    return pltpu.with_memory_space_constraint(a, pltpu.HBM) if a.size * a.dtype.itemsize >= MIB else a


def _pallas(body, **kwargs):
    return pl.pallas_call(body, **kwargs)


def _resident(shape):
    return pl.BlockSpec(shape, lambda *_: (0,) * len(shape), pipeline_mode=pl.Buffered(1))


def _dot(a, b):
    return jnp.dot(a, b, preferred_element_type=F32)


def _dot_nt(a, b):
    return lax.dot_general(a, b, (((1,), (1,)), ((), ())), preferred_element_type=F32)


def _dot_tn(a, b):
    return lax.dot_general(a, b, (((0,), (0,)), ((), ())), preferred_element_type=F32)


_GELU_C = math.sqrt(2.0 / math.pi)


def _gelu(x):
    return x * (0.5 * (1.0 + jnp.tanh(_GELU_C * (x + 0.044715 * (x * x * x)))))


def _gelu_grad(x):
    t = jnp.tanh(_GELU_C * (x + 0.044715 * (x * x * x)))
    return 0.5 * (1.0 + t) + 0.5 * x * (1.0 - t * t) * (_GELU_C * (1.0 + 3.0 * 0.044715 * (x * x)))


def _rsqrt_ms(v):
    return lax.rsqrt(jnp.mean(v * v, axis=-1, keepdims=True) + EPS)


def _rmsnorm_bwd(dn, src, gain):
    r = _rsqrt_ms(src)
    t = gain * dn
    dgain = jnp.sum(dn * (src * r), axis=0, keepdims=True)
    dsrc = r * t - src * ((r * r * r) * jnp.mean(t * src, axis=-1, keepdims=True))
    return dsrc, dgain


def _rot_half(v):
    w = v.shape[-1]
    lane = lax.broadcasted_iota(jnp.int32, v.shape, v.ndim - 1)
    return jnp.where((lane % HEAD_DIM) < HEAD_DIM // 2, pltpu.roll(v, w - HEAD_DIM // 2, v.ndim - 1),
                     pltpu.roll(v, HEAD_DIM // 2, v.ndim - 1))


def _head_masks(shape):
    lane = lax.broadcasted_iota(jnp.int32, shape, 1)
    return [(lane >= h * HEAD_DIM) & (lane < (h + 1) * HEAD_DIM) for h in range(HEADS_PER_GROUP)]


def _head_stack(block, hmask):
    zero = jnp.zeros((), block.dtype)
    return jnp.concatenate([jnp.where(hm, block, zero) for hm in hmask], axis=0)


LANES = 128


def _put_residue(slab, val, out_ref, dil, width, col0):
    tm, w = val.shape
    if dil == 1:
        out_ref[:, col0:col0 + w] = val.astype(out_ref.dtype)
        return
    for k in range(w // LANES):
        slab[k] = val[:, k * LANES:(k + 1) * LANES]
    for r in range(dil):
        for k in range(w // LANES):
            c = r * width + col0 + k * LANES
            out_ref[:, c:c + LANES] = slab[k, pl.ds(r, tm // dil, stride=dil), :].astype(out_ref.dtype)


def _get_tokens(slab, in_ref, dil, width, col0, w):
    if dil == 1:
        return in_ref[:, col0:col0 + w].astype(F32)
    rows = in_ref.shape[0]
    for r in range(dil):
        for k in range(w // LANES):
            c = r * width + col0 + k * LANES
            slab[k, pl.ds(r, rows, stride=dil), :] = in_ref[:, c:c + LANES].astype(F32)
    return jnp.concatenate([slab[k] for k in range(w // LANES)], axis=1)


def _prepare(x, g0, rider=None):
    seq = x.shape[0]
    half = HEAD_DIM // 2
    inv_freq = ROPE_THETA ** (-jnp.arange(half, dtype=F32) / half)
    freq = jnp.tile(inv_freq, LANES // half).reshape(1, LANES)
    tm = 256

    def body(x_ref, g_ref, f_ref, *refs):
        h_refs, tabs, slab = refs[:N_GROUPS], refs[N_GROUPS:3 * N_GROUPS], refs[-1]
        xv = x_ref[...]
        hf = (xv * _rsqrt_ms(xv)) * g_ref[...]
        for g, dil in enumerate(DILATIONS):
            _put_residue(slab, hf, h_refs[g], dil, D_MODEL, 0)
        row = lax.broadcasted_iota(jnp.int32, (tm, LANES), 0) + pl.program_id(0) * tm
        lane = lax.broadcasted_iota(jnp.int32, (tm, LANES), 1)
        ang = row.astype(F32) * f_ref[...]
        cos = jnp.cos(ang)
        sin = jnp.where((lane % HEAD_DIM) < half, -jnp.sin(ang), jnp.sin(ang))
        for i, dil in enumerate(DILATIONS):
            for tab, val in ((tabs[2 * i], cos), (tabs[2 * i + 1], sin)):
                slab[0] = val
                for r in range(dil):
                    piece = slab[0, pl.ds(r, tm // dil, stride=dil), :] if dil > 1 else val
                    for k in range(GROUP_W // LANES):
                        tab[:, r * GROUP_W + k * LANES:r * GROUP_W + (k + 1) * LANES] = piece

    outs, riding = _call(
        body, name="prepare", grid=(seq // tm,),
        in_specs=[pl.BlockSpec((tm, D_MODEL), lambda i: (i, 0)), pl.BlockSpec((1, D_MODEL), lambda i: (0, 0)),
                  pl.BlockSpec((1, LANES), lambda i: (0, 0))],
        out_specs=[pl.BlockSpec((tm // d, d * D_MODEL), lambda i: (i, 0)) for d in DILATIONS]
        + [pl.BlockSpec((tm // d, d * GROUP_W), lambda i: (i, 0)) for d in DILATIONS for _ in range(2)],
        out_shape=[jax.ShapeDtypeStruct((seq // d, d * D_MODEL), BF16) for d in DILATIONS]
        + [jax.ShapeDtypeStruct((seq // d, d * GROUP_W), F32) for d in DILATIONS for _ in range(2)],
        scratch_shapes=[pltpu.VMEM((D_MODEL // LANES, tm, LANES), F32)],
        params=_params(("arbitrary",), 32), args=(x, g0, freq), rider=rider)
    tabs = outs[N_GROUPS:]
    return outs[:N_GROUPS], {d: (tabs[2 * i], tabs[2 * i + 1]) for i, d in enumerate(DILATIONS)}, riding


def _in_proj(h, w_in, cos_t, sin_t, rider=None):
    seq = h.shape[0]
    tm, tn = 512, GROUP_W
    n_qk = 2 * ATTN_W // tn
    n_qkv = QKV_W // tn

    def body(h_ref, w_ref, cos_ref, sin_ref, *refs):
        qkv_refs, rest_ref, slab = refs[:N_GROUPS], refs[N_GROUPS], refs[-1]
        hb = h_ref[...]
        cos, sin = cos_ref[...], sin_ref[...]
        for j in range(IN_W // tn):
            p = _dot_nt(hb, w_ref[j * tn:(j + 1) * tn, :])
            if j < n_qkv:
                if j < n_qk:
                    p = p * cos + _rot_half(p) * sin
                section, g = divmod(j, N_GROUPS)
                _put_residue(slab, p, qkv_refs[g], DILATIONS[g], 3 * GROUP_W, section * GROUP_W)
            else:
                rest_ref[:, (j - n_qkv) * tn:(j - n_qkv + 1) * tn] = p.astype(BF16)

    return _call(
        body, name="in_proj", grid=(seq // tm,),
        in_specs=[pl.BlockSpec((tm, D_MODEL), lambda i: (i, 0)),
                  _resident((IN_W, D_MODEL)),
                  pl.BlockSpec((tm, GROUP_W), lambda i: (i, 0)),
                  pl.BlockSpec((tm, GROUP_W), lambda i: (i, 0))],
        out_specs=[pl.BlockSpec((tm // d, d * 3 * GROUP_W), lambda i: (i, 0)) for d in DILATIONS]
        + [pl.BlockSpec((tm, REST_W), lambda i: (i, 0))],
        out_shape=[jax.ShapeDtypeStruct((seq // d, d * 3 * GROUP_W), BF16) for d in DILATIONS]
        + [jax.ShapeDtypeStruct((seq, REST_W), BF16)],
        scratch_shapes=[pltpu.VMEM((GROUP_W // LANES, tm, LANES), F32)],
        params=_params(("arbitrary",), 48), args=(h, w_in, cos_t, sin_t), rider=rider)


def _band_masks():
    qi = lax.broadcasted_iota(jnp.int32, (QBLK, QBLK), 0)
    kj = lax.broadcasted_iota(jnp.int32, (QBLK, QBLK), 1)
    return kj <= qi, kj >= qi


def _attn_tile(length):
    return min(512, length)


def _attn_fwd(qkv, dil, rider=None):
    length = qkv.shape[0]
    tq = _attn_tile(length)
    nsub = tq // QBLK
    nblk = length // tq

    def body(q_ref, k_ref, v_ref, kp_ref, vp_ref, o_ref, l_ref):
        n = pl.program_id(1)
        mask_c, mask_p0 = _band_masks()
        hmask = _head_masks((QBLK, GROUP_W))
        zero = jnp.zeros((), BF16)
        for b in range(nsub):
            rows = slice(b * QBLK, (b + 1) * QBLK)
            q = q_ref[rows, :]
            kc, vc = k_ref[rows, :], v_ref[rows, :]
            if b == 0:
                kp, vp = kp_ref[...], vp_ref[...]
                mask_p = mask_p0 & (n > 0)
            else:
                prow = slice((b - 1) * QBLK, b * QBLK)
                kp, vp = k_ref[prow, :], v_ref[prow, :]
                mask_p = mask_p0
            o_acc = jnp.zeros((QBLK, GROUP_W), F32)
            l_acc = jnp.zeros((QBLK, GROUP_W), F32)
            for h in range(HEADS_PER_GROUP):
                hm = hmask[h]
                sc = jnp.where(mask_c, _dot_nt(q, jnp.where(hm, kc, zero)) * SCALE, NEG)
                sp = jnp.where(mask_p, _dot_nt(q, jnp.where(hm, kp, zero)) * SCALE, NEG)
                m = jnp.maximum(jnp.max(sc, axis=-1, keepdims=True), jnp.max(sp, axis=-1, keepdims=True))
                pc, pp = jnp.exp(sc - m), jnp.exp(sp - m)
                den = jnp.sum(pc, axis=-1, keepdims=True) + jnp.sum(pp, axis=-1, keepdims=True)
                pv = _dot(pc.astype(BF16), jnp.where(hm, vc, zero)) + _dot(pp.astype(BF16), jnp.where(hm, vp, zero))
                o_acc = o_acc + pv / den
                l_acc = l_acc + jnp.where(hm, m + jnp.log(den), 0.0)
            o_ref[rows, :] = o_acc.astype(BF16)
            l_ref[rows, :] = l_acc

    cur = lambda sec: pl.BlockSpec((tq, GROUP_W), lambda r, n: (n, r * 3 + sec))
    prev = lambda sec: pl.BlockSpec((QBLK, GROUP_W), lambda r, n: (jnp.maximum(n * nsub - 1, 0), r * 3 + sec))
    return _call(
        body, name=f"attn_fwd_d{dil}", grid=(dil, nblk),
        in_specs=[cur(0), cur(1), cur(2), prev(1), prev(2)],
        out_specs=[pl.BlockSpec((tq, GROUP_W), lambda r, n: (n, r))] * 2,
        out_shape=[jax.ShapeDtypeStruct((length, dil * GROUP_W), BF16),
                   jax.ShapeDtypeStruct((length, dil * GROUP_W), F32)], scratch_shapes=[],
        params=_params(("arbitrary", "arbitrary"), 32), args=(qkv, qkv, qkv, qkv, qkv), rider=rider)


def _attn_bwd(qkv, dy, y, lse, cos_t, sin_t, dil, rider=None):
    length = qkv.shape[0]
    tq = _attn_tile(length)
    nsub = tq // QBLK
    nblk = length // tq

    def body(q_ref, k_ref, v_ref, kp_ref, vp_ref, qn_ref, dy_ref, y_ref, l_ref, dyn_ref, yn_ref, ln_ref,
             cos_ref, sin_ref, out_ref, dq_s, dk_s, dv_s):
        n = pl.program_id(1)
        mask_c, mask_p0 = _band_masks()
        hmask = _head_masks((QBLK, GROUP_W))
        sub = lambda ref, b: ref[b * QBLK:(b + 1) * QBLK, :]
        kbd = [_head_stack(kp_ref[...], hmask)] + [_head_stack(sub(k_ref, b), hmask) for b in range(nsub)]
        vbd = [_head_stack(vp_ref[...], hmask)] + [_head_stack(sub(v_ref, b), hmask) for b in range(nsub)]
        dq_s[...] = jnp.zeros(dq_s.shape, F32)

        def query_terms(q, dyv, yv, lv):
            prod = dyv * yv
            return dict(
                q=q, dy=dyv.astype(BF16), q_heads=[jnp.where(hm, q, jnp.zeros((), BF16)) for hm in hmask],
                dy_heads=[jnp.where(hm, dyv, 0.0).astype(BF16) for hm in hmask],
                delta=[jnp.sum(jnp.where(hm, prod, 0.0), axis=-1, keepdims=True) for hm in hmask],
                lse=[jnp.max(jnp.where(hm, lv, NEG), axis=-1, keepdims=True) for hm in hmask])

        queries = [query_terms(sub(q_ref, b), sub(dy_ref, b), sub(y_ref, b), sub(l_ref, b)) for b in range(nsub)]
        queries.append(query_terms(qn_ref[...], dyn_ref[...], yn_ref[...], ln_ref[...]))
        rows_of = lambda items: items[0] if len(items) == 1 else jnp.concatenate(items, axis=0)
        for kb in range(nsub + 1):
            seen = [(kb - 1, mask_c)] if kb >= 1 else []
            if kb == 0:
                seen.append((0, mask_p0 & (n > 0)))
            elif kb < nsub:
                seen.append((kb, mask_p0))
            else:
                seen.append((nsub, mask_p0 & (n < nblk - 1)))
            qs = [queries[b] for b, _ in seen]
            mask = rows_of([m for _, m in seen])
            s = _dot_nt(rows_of([t["q"] for t in qs]), kbd[kb]) * SCALE
            dp = _dot_nt(rows_of([t["dy"] for t in qs]), vbd[kb])
            ps, dss = [], []
            for h in range(HEADS_PER_GROUP):
                cols = slice(h * QBLK, (h + 1) * QBLK)
                p = jnp.exp(jnp.where(mask, s[:, cols] - rows_of([t["lse"][h] for t in qs]), NEG))
                ps.append(p.astype(BF16))
                dss.append((p * (dp[:, cols] - rows_of([t["delta"][h] for t in qs]))).astype(BF16))
            dq = _dot(jnp.concatenate(dss, axis=1), kbd[kb]) * SCALE
            for i, (b, _) in enumerate(seen):
                if b < nsub:
                    dq_s[b * QBLK:(b + 1) * QBLK, :] += dq[i * QBLK:(i + 1) * QBLK, :]
            if kb >= 1:
                krows = slice((kb - 1) * QBLK, kb * QBLK)
                head_rows = lambda key: jnp.concatenate([t[key][h] for h in range(HEADS_PER_GROUP) for t in qs], axis=0)
                dv_s[krows, :] = _dot_tn(jnp.concatenate(ps, axis=0), head_rows("dy_heads"))
                dk_s[krows, :] = _dot_tn(jnp.concatenate(dss, axis=0), head_rows("q_heads")) * SCALE
        cos, sin = cos_ref[...], sin_ref[...]
        dq, dk = dq_s[...], dk_s[...]
        out_ref[:, 0:GROUP_W] = (dq * cos - _rot_half(dq) * sin).astype(BF16)
        out_ref[:, GROUP_W:2 * GROUP_W] = (dk * cos - _rot_half(dk) * sin).astype(BF16)
        out_ref[:, 2 * GROUP_W:3 * GROUP_W] = dv_s[...].astype(BF16)

    cur = lambda sec: pl.BlockSpec((tq, GROUP_W), lambda r, n: (n, r * 3 + sec))
    prev = lambda sec: pl.BlockSpec((QBLK, GROUP_W), lambda r, n: (jnp.maximum(n * nsub - 1, 0), r * 3 + sec))
    nxt_q = pl.BlockSpec((QBLK, GROUP_W), lambda r, n: (jnp.minimum((n + 1) * nsub, nblk * nsub - 1), r * 3))
    tok = pl.BlockSpec((tq, GROUP_W), lambda r, n: (n, r))
    tok_next = pl.BlockSpec((QBLK, GROUP_W), lambda r, n: (jnp.minimum((n + 1) * nsub, nblk * nsub - 1), r))
    (out,), riding = _call(
        body, name=f"attn_bwd_d{dil}", grid=(dil, nblk),
        in_specs=[cur(0), cur(1), cur(2), prev(1), prev(2), nxt_q,
                  tok, tok, tok, tok_next, tok_next, tok_next, tok, tok],
        out_specs=[pl.BlockSpec((tq, 3 * GROUP_W), lambda r, n: (n, r))],
        out_shape=[jax.ShapeDtypeStruct((length, dil * 3 * GROUP_W), BF16)],
        scratch_shapes=[pltpu.VMEM((tq, GROUP_W), F32)] * 3,
        params=_params(("arbitrary", "arbitrary"), 32),
        args=(qkv, qkv, qkv, qkv, qkv, qkv, dy, y, lse, dy, y, lse, cos_t, sin_t), rider=rider)
    return out, riding


GATE_SLAB = 256


def _layernorm_stats(z):
    mu = jnp.mean(z, axis=-1, keepdims=True)
    zc = z - mu
    rstd = lax.rsqrt(jnp.mean(zc * zc, axis=-1, keepdims=True) + EPS)
    return zc * rstd, rstd


def _tril_mask():
    row = lax.broadcasted_iota(jnp.int32, (CHUNK, CHUNK), 0)
    col = lax.broadcasted_iota(jnp.int32, (CHUNK, CHUNK), 1)
    return col <= row


def _mix_fwd(o_l, rest, x, w_sp, b_col, ln_g, ln_b, w_ba, w_bg, w_out, g1, rider=None):
    seq = x.shape[0]
    tm = 256

    def body(o0, l0, o1, l1, o2, l2, up_ref, zp_ref, gap_ref, gbp_ref, x_ref, wsp_ref, bcol_ref, lg_ref, lb_ref,
             wba_ref, wbg_ref, wout_ref, g1_ref, ya0, lj0, ya1, lj1, ya2, lj2, yg_ref, mg_ref, y_ref, x1_ref, slab):
        outs = [_get_tokens(slab, o, d, GROUP_W, 0, GROUP_W) for o, d in zip((o0, o1, o2), DILATIONS)]
        lses = [_get_tokens(slab, l, d, GROUP_W, 0, GROUP_W) for l, d in zip((l0, l1, l2), DILATIONS)]
        m = jnp.maximum(jnp.maximum(lses[0], lses[1]), lses[2])
        es = [jnp.exp(l - m) for l in lses]
        tot = es[0] + es[1] + es[2]
        ya = (es[0] * outs[0] + es[1] * outs[1] + es[2] * outs[2]) / tot
        lj = m + jnp.log(tot)
        for ya_ref, lj_ref, d in zip((ya0, ya1, ya2), (lj0, lj1, lj2), DILATIONS):
            _put_residue(slab, ya, ya_ref, d, GROUP_W, 0)
            _put_residue(slab, lj, lj_ref, d, GROUP_W, 0)
        zhat, _ = _layernorm_stats(_gelu(zp_ref[...].astype(F32)))
        zln = (zhat * lg_ref[...] + lb_ref[...]).astype(BF16)
        u = _gelu(up_ref[...].astype(F32))
        tril = _tril_mask()
        for g in range(GMLP_GROUPS):
            wm = jnp.where(tril, wsp_ref[g], 0.0).astype(BF16)
            cols = slice(g * CHUNK, (g + 1) * CHUNK)
            for c in range(tm // CHUNK):
                rows = slice(c * CHUNK, (c + 1) * CHUNK)
                sz = _dot(wm, zln[rows, cols]) + bcol_ref[g]
                yg_ref[rows, cols] = (u[rows, cols] * sz).astype(BF16)
        yab, ygb = ya.astype(BF16), yg_ref[...]
        for c in range(D_MODEL // GATE_SLAB):
            cs = slice(c * GATE_SLAB, (c + 1) * GATE_SLAB)
            a = _dot(yab, wba_ref[:, cs])
            bm = _dot(ygb, wbg_ref[:, cs])
            mg_ref[:, cs] = (jax.nn.sigmoid(gap_ref[:, cs].astype(F32)) * a
                             + jax.nn.sigmoid(gbp_ref[:, cs].astype(F32)) * bm).astype(BF16)
        yv = _dot(mg_ref[...], wout_ref[...])
        y_ref[...] = yv.astype(BF16)
        x1_ref[...] = x_ref[...] + (yv * _rsqrt_ms(yv)) * g1_ref[...]

    tok = lambda w: pl.BlockSpec((tm, w), lambda i: (i, 0))
    res = lambda d: pl.BlockSpec((tm // d, d * GROUP_W), lambda i: (i, 0))
    full = lambda *s: pl.BlockSpec(s, lambda i: (0,) * len(s))
    res_specs = [res(d) for d in DILATIONS for _ in range(2)]
    return _call(
        body, name="mix_fwd", grid=(seq // tm,),
        in_specs=res_specs + [
            pl.BlockSpec((tm, GMLP_W), lambda i: (i, 0)), pl.BlockSpec((tm, GMLP_W), lambda i: (i, 1)),
            pl.BlockSpec((tm, D_MODEL), lambda i: (i, 1)), pl.BlockSpec((tm, D_MODEL), lambda i: (i, 2)),
            tok(D_MODEL), full(GMLP_GROUPS, CHUNK, CHUNK), full(GMLP_GROUPS, CHUNK, 1), full(1, GMLP_W), full(1, GMLP_W),
            full(GROUP_W, D_MODEL), full(GMLP_W, D_MODEL), full(D_MODEL, D_MODEL), full(1, D_MODEL)],
        out_specs=res_specs + [tok(GMLP_W), tok(D_MODEL), tok(D_MODEL), tok(D_MODEL)],
        out_shape=[jax.ShapeDtypeStruct((seq // d, d * GROUP_W), F32) for d in DILATIONS for _ in range(2)]
        + [jax.ShapeDtypeStruct((seq, GMLP_W), BF16), jax.ShapeDtypeStruct((seq, D_MODEL), BF16),
           jax.ShapeDtypeStruct((seq, D_MODEL), BF16), jax.ShapeDtypeStruct((seq, D_MODEL), F32)],
        scratch_shapes=[pltpu.VMEM((GROUP_W // LANES, tm, LANES), F32)],
        params=_params(("arbitrary",), 48),
        args=(*o_l, rest, rest, rest, rest, x, w_sp, b_col, ln_g, ln_b, w_ba, w_bg, w_out, g1), rider=rider)


def _mlp_fwd(x1, g2, g3, w_mi, w_mo, target):
    seq = x1.shape[0]
    tm, tf = MLP_FWD_TM, 512

    def body(x1_ref, g2_ref, g3_ref, wmi_ref, wmo_ref, t_ref, h2_ref, a_ref, dy2_ref, dout_ref, loss_ref, dg3_ref, sq_s):
        @pl.when(pl.program_id(0) == 0)
        def _():
            loss_ref[...] = jnp.zeros(loss_ref.shape, F32)
            dg3_ref[...] = jnp.zeros(dg3_ref.shape, F32)

        xv = x1_ref[...]
        hb = ((xv * _rsqrt_ms(xv)) * g2_ref[...]).astype(BF16)
        h2_ref[...] = hb
        for j in range(D_FF // tf):
            cols = slice(j * tf, (j + 1) * tf)
            a = jnp.maximum(_dot(hb, wmi_ref[:, cols]), 0.0)
            a_ref[:, cols] = a.astype(BF16)
            sq_s[:, cols] = (a * a).astype(BF16)
        y2 = _dot(sq_s[...], wmo_ref[...])
        r3 = _rsqrt_ms(y2)
        out = xv + (y2 * r3) * g3_ref[...]
        diff = out - t_ref[...]
        tile_loss = 0.5 * jnp.sum(jnp.mean(diff * diff, axis=-1, keepdims=True), axis=0, keepdims=True)
        loss_ref[...] += jnp.broadcast_to(tile_loss, loss_ref.shape)
        dout = diff * (1.0 / D_MODEL)
        dout_ref[...] = dout
        dy2, dg3 = _rmsnorm_bwd(dout, y2, g3_ref[...])
        dy2_ref[...] = dy2.astype(BF16)
        dg3_ref[...] += dg3

    tok = lambda w: pl.BlockSpec((tm, w), lambda i: (i, 0))
    vec = pl.BlockSpec((1, D_MODEL), lambda i: (0, 0))
    return _pallas(
        body, name="mlp_fwd", grid=(seq // tm,),
        in_specs=[tok(D_MODEL), vec, vec, _resident((D_MODEL, D_FF)), _resident((D_FF, D_MODEL)), tok(D_MODEL)],
        out_specs=[tok(D_MODEL), tok(D_FF), tok(D_MODEL), tok(D_MODEL), pl.BlockSpec((8, 128), lambda i: (0, 0)), vec],
        out_shape=[jax.ShapeDtypeStruct((seq, D_MODEL), BF16), jax.ShapeDtypeStruct((seq, D_FF), BF16),
                   jax.ShapeDtypeStruct((seq, D_MODEL), BF16), jax.ShapeDtypeStruct((seq, D_MODEL), F32),
                   jax.ShapeDtypeStruct((8, 128), F32), jax.ShapeDtypeStruct((1, D_MODEL), F32)],
        scratch_shapes=[pltpu.VMEM((tm, D_FF), BF16)],
        compiler_params=_params(("arbitrary",), 56),
    )(*map(_in_hbm, (x1, g2, g3, w_mi, w_mo, target)))


def _mlp_bwd(dy2, a, w_mo, w_mi, dout, x1, y, g2, g1, rider=None):
    seq = x1.shape[0]
    tm, tf = MLP_TM, 512

    def body(dy2_ref, a_ref, wmo_ref, wmi_ref, dout_ref, x1_ref, y_ref, g2_ref, g1_ref,
             dap_ref, dx1_ref, dy_ref, dg2_ref, dg1_ref):
        @pl.when(pl.program_id(0) == 0)
        def _():
            dg2_ref[...] = jnp.zeros(dg2_ref.shape, F32)
            dg1_ref[...] = jnp.zeros(dg1_ref.shape, F32)

        dy2v = dy2_ref[...]
        for j in range(D_FF // tf):
            cols = slice(j * tf, (j + 1) * tf)
            da2 = _dot_nt(dy2v, wmo_ref[cols, :])
            dap_ref[:, cols] = (da2 * (2.0 * a_ref[:, cols].astype(F32))).astype(BF16)
        dh2 = _dot_nt(dap_ref[...], wmi_ref[...])
        dres, dg2 = _rmsnorm_bwd(dh2, x1_ref[...], g2_ref[...])
        dx1 = dout_ref[...] + dres
        dx1_ref[...] = dx1
        dg2_ref[...] += dg2
        dyv, dg1 = _rmsnorm_bwd(dx1, y_ref[...].astype(F32), g1_ref[...])
        dy_ref[...] = dyv.astype(BF16)
        dg1_ref[...] += dg1

    tok = lambda w: pl.BlockSpec((tm, w), lambda i: (i, 0))
    vec = pl.BlockSpec((1, D_MODEL), lambda i: (0, 0))
    return _call(
        body, name="mlp_bwd", grid=(seq // tm,),
        in_specs=[tok(D_MODEL), tok(D_FF), _resident((D_FF, D_MODEL)), _resident((D_MODEL, D_FF)),
                  tok(D_MODEL), tok(D_MODEL), tok(D_MODEL), vec, vec],
        out_specs=[tok(D_FF), tok(D_MODEL), tok(D_MODEL), vec, vec],
        out_shape=[jax.ShapeDtypeStruct((seq, D_FF), BF16), jax.ShapeDtypeStruct((seq, D_MODEL), F32),
                   jax.ShapeDtypeStruct((seq, D_MODEL), BF16), jax.ShapeDtypeStruct((1, D_MODEL), F32),
                   jax.ShapeDtypeStruct((1, D_MODEL), F32)], scratch_shapes=[],
        params=_params(("arbitrary",), 56), args=(dy2, a, w_mo, w_mi, dout, x1, y, g2, g1), rider=rider)


def _tn_matmul(a, b, name, bm, bn, square_a=False, column_shards=False, rider=None):
    seq, m = a.shape
    n = b.shape[1]
    ts = 2048

    def body(a_ref, b_ref, o_ref):
        @pl.when(pl.program_id(2) == 0)
        def _():
            o_ref[...] = jnp.zeros(o_ref.shape, F32)

        av = a_ref[...]
        if square_a:
            af = av.astype(F32)
            av = (af * af).astype(BF16)
        o_ref[...] += _dot_tn(av, b_ref[...])

    if column_shards:
        out_spec = pl.BlockSpec((None, bm, bn), lambda mi, ni, s: (ni, mi, 0))
        out_shape = jax.ShapeDtypeStruct((n // bn, m, bn), F32)
    else:
        out_spec = pl.BlockSpec((bm, bn), lambda mi, ni, s: (mi, ni))
        out_shape = jax.ShapeDtypeStruct((m, n), F32)
    (out,), riding = _call(
        body, name=name, grid=(m // bm, n // bn, seq // ts),
        in_specs=[pl.BlockSpec((ts, bm), lambda mi, ni, s: (s, mi)), pl.BlockSpec((ts, bn), lambda mi, ni, s: (s, ni))],
        out_specs=[out_spec], out_shape=[out_shape], scratch_shapes=[],
        params=_params(("arbitrary", "arbitrary", "arbitrary"), 40), args=(a, b), rider=rider)
    return out, riding


def _tn_matmul_residue(a, b, dil, name):
    length = a.shape[0]
    m, n = a.shape[1] // dil, b.shape[1] // dil
    ts = min(1024, length)

    def body(a_ref, b_ref, o_ref):
        @pl.when((pl.program_id(0) == 0) & (pl.program_id(1) == 0))
        def _():
            o_ref[...] = jnp.zeros(o_ref.shape, F32)

        o_ref[...] += _dot_tn(a_ref[...], b_ref[...])

    return _pallas(
        body, name=name, grid=(dil, length // ts),
        in_specs=[pl.BlockSpec((ts, m), lambda r, s: (s, r)), pl.BlockSpec((ts, n), lambda r, s: (s, r))],
        out_specs=pl.BlockSpec((m, n), lambda r, s: (0, 0)),
        out_shape=jax.ShapeDtypeStruct((m, n), F32),
        compiler_params=_params(("arbitrary", "arbitrary"), 40),
    )(_in_hbm(a), _in_hbm(b))


def _mix_bwd(dy, ya, yg, mg, rest, w_out, w_ba, w_bg, w_sp, b_col, ln_g, ln_b, rider=None):
    seq = dy.shape[0]
    tm = 256

    def body(dy_ref, ya_ref, yg_ref, mg_ref, up_ref, zp_ref, gap_ref, gbp_ref, wout_ref, wba_ref, wbg_ref,
             wsp_ref, bcol_ref, lg_ref, lb_ref,
             dya0, dya1, dya2, dpr_ref, dwout_ref, dwba_ref, dwbg_ref, dwsp_ref, dbb_ref, dlg_ref, dlb_ref,
             dzln_s, du_s, slab, da_s, db_s):
        @pl.when(pl.program_id(0) == 0)
        def _():
            for ref in (dwout_ref, dwba_ref, dwbg_ref, dwsp_ref, dbb_ref, dlg_ref, dlb_ref):
                ref[...] = jnp.zeros(ref.shape, F32)

        dyv = dy_ref[...]
        dwout_ref[...] += _dot_tn(mg_ref[...], dyv)
        yab = ya_ref[...].astype(BF16)
        ygb = yg_ref[...]
        for c in range(D_MODEL // GATE_SLAB):
            cs = slice(c * GATE_SLAB, (c + 1) * GATE_SLAB)
            dm = _dot_nt(dyv, wout_ref[cs, :])
            a = _dot(yab, wba_ref[:, cs])
            bm = _dot(ygb, wbg_ref[:, cs])
            ga = jax.nn.sigmoid(gap_ref[:, cs].astype(F32))
            gb = jax.nn.sigmoid(gbp_ref[:, cs].astype(F32))
            lo = 2 * GMLP_W + c * GATE_SLAB
            dpr_ref[:, lo:lo + GATE_SLAB] = (dm * a * (ga * (1.0 - ga))).astype(BF16)
            dpr_ref[:, lo + D_MODEL:lo + D_MODEL + GATE_SLAB] = (dm * bm * (gb * (1.0 - gb))).astype(BF16)
            da_s[:, cs] = (dm * ga).astype(BF16)
            db_s[:, cs] = (dm * gb).astype(BF16)
        da, db = da_s[...], db_s[...]
        dwba = _dot_tn(yab, da)
        dwbg = _dot_tn(ygb, db)
        shard_w = D_MODEL // N_CHIPS
        for j in range(N_CHIPS):
            dwba_ref[j] += dwba[:, j * shard_w:(j + 1) * shard_w]
            dwbg_ref[j] += dwbg[:, j * shard_w:(j + 1) * shard_w]
        dya = _dot_nt(da, wba_ref[...])
        for dya_ref, d in zip((dya0, dya1, dya2), DILATIONS):
            _put_residue(slab, dya, dya_ref, d, GROUP_W, 0)
        dyg = _dot_nt(db, wbg_ref[...])

        zp = zp_ref[...].astype(F32)
        zhat, rstd = _layernorm_stats(_gelu(zp))
        lg = lg_ref[...]
        zln = (zhat * lg + lb_ref[...]).astype(BF16)
        up = up_ref[...].astype(F32)
        u = _gelu(up)
        tril = _tril_mask()
        for g in range(GMLP_GROUPS):
            wm = jnp.where(tril, wsp_ref[g], 0.0).astype(BF16)
            cols = slice(g * CHUNK, (g + 1) * CHUNK)
            for c in range(tm // CHUNK):
                rows = slice(c * CHUNK, (c + 1) * CHUNK)
                zb = zln[rows, cols]
                sz = _dot(wm, zb) + bcol_ref[g]
                dyg_cg = dyg[rows, cols]
                du_s[rows, cols] = dyg_cg * sz
                dsz = dyg_cg * u[rows, cols]
                dszb = dsz.astype(BF16)
                dbb_ref[g] += jnp.broadcast_to(jnp.sum(dsz, axis=-1, keepdims=True), (CHUNK, CHUNK))
                dwsp_ref[g] += jnp.where(tril, _dot_nt(dszb, zb), 0.0)
                dzln_s[rows, cols] = _dot_tn(wm, dszb)
        dzln = dzln_s[...]
        dlg_ref[...] += jnp.sum(dzln * zhat, axis=0, keepdims=True)
        dlb_ref[...] += jnp.sum(dzln, axis=0, keepdims=True)
        dzh = dzln * lg
        dz = rstd * (dzh - jnp.mean(dzh, axis=-1, keepdims=True) - zhat * jnp.mean(dzh * zhat, axis=-1, keepdims=True))
        dpr_ref[:, GMLP_W:2 * GMLP_W] = (dz * _gelu_grad(zp)).astype(BF16)
        dpr_ref[:, 0:GMLP_W] = (du_s[...] * _gelu_grad(up)).astype(BF16)

    tok = lambda w: pl.BlockSpec((tm, w), lambda i: (i, 0))
    full = lambda *s: pl.BlockSpec(s, lambda i: (0,) * len(s))
    return _call(
        body, name="mix_bwd", grid=(seq // tm,),
        in_specs=[tok(D_MODEL), tok(GROUP_W), tok(GMLP_W), tok(D_MODEL),
                  pl.BlockSpec((tm, GMLP_W), lambda i: (i, 0)), pl.BlockSpec((tm, GMLP_W), lambda i: (i, 1)),
                  pl.BlockSpec((tm, D_MODEL), lambda i: (i, 1)), pl.BlockSpec((tm, D_MODEL), lambda i: (i, 2)),
                  full(D_MODEL, D_MODEL), full(GROUP_W, D_MODEL), full(GMLP_W, D_MODEL),
                  full(GMLP_GROUPS, CHUNK, CHUNK), full(GMLP_GROUPS, CHUNK, 1), full(1, GMLP_W), full(1, GMLP_W)],
        out_specs=[pl.BlockSpec((tm // d, d * GROUP_W), lambda i: (i, 0)) for d in DILATIONS]
        + [tok(REST_W), full(D_MODEL, D_MODEL), full(N_CHIPS, GROUP_W, D_MODEL // N_CHIPS),
           full(N_CHIPS, GMLP_W, D_MODEL // N_CHIPS),
           full(GMLP_GROUPS, CHUNK, CHUNK), full(GMLP_GROUPS, CHUNK, CHUNK), full(1, GMLP_W), full(1, GMLP_W)],
        out_shape=[jax.ShapeDtypeStruct((seq // d, d * GROUP_W), F32) for d in DILATIONS]
        + [jax.ShapeDtypeStruct((seq, REST_W), BF16),
           jax.ShapeDtypeStruct((D_MODEL, D_MODEL), F32), jax.ShapeDtypeStruct((N_CHIPS, GROUP_W, D_MODEL // N_CHIPS), F32),
           jax.ShapeDtypeStruct((N_CHIPS, GMLP_W, D_MODEL // N_CHIPS), F32),
           jax.ShapeDtypeStruct((GMLP_GROUPS, CHUNK, CHUNK), F32),
           jax.ShapeDtypeStruct((GMLP_GROUPS, CHUNK, CHUNK), F32), jax.ShapeDtypeStruct((1, GMLP_W), F32),
           jax.ShapeDtypeStruct((1, GMLP_W), F32)],
        scratch_shapes=[pltpu.VMEM((tm, GMLP_W), F32), pltpu.VMEM((tm, GMLP_W), F32),
                        pltpu.VMEM((GROUP_W // LANES, tm, LANES), F32),
                        pltpu.VMEM((tm, D_MODEL), BF16), pltpu.VMEM((tm, D_MODEL), BF16)],
        params=_params(("arbitrary",), 56),
        args=(dy, ya, yg, mg, rest, rest, rest, rest, w_out, w_ba, w_bg, w_sp, b_col, ln_g, ln_b), rider=rider)


IN_PROJ_BWD_TM = 256


def _in_proj_bwd(dqkv, drest, w_in, x, dx1, g0, so_far, span, rider=None):
    seq = x.shape[0]
    tm = IN_PROJ_BWD_TM
    off, steps = span
    gx_so_far, dg_so_far = so_far

    def body(d0, d1, d2, dr_ref, w_ref, x_ref, dx1_ref, g_ref, dg_in_ref, gx_in_ref, gx_ref, dg_ref, slab):
        @pl.when(pl.program_id(0) == 0)
        def _():
            dg_ref[...] = dg_in_ref[...]

        dh = _dot(dr_ref[...], w_ref[QKV_W:, :])
        for g, (d_ref, dil) in enumerate(zip((d0, d1, d2), DILATIONS)):
            piece = d_ref[...] if dil == 1 else _get_tokens(slab, d_ref, dil, 3 * GROUP_W, 0, 3 * GROUP_W).astype(BF16)
            for section, (lo, hi) in enumerate(_qkv_columns(g)):
                dh = dh + _dot(piece[:, section * GROUP_W:(section + 1) * GROUP_W], w_ref[lo:hi, :])
        dres, dg = _rmsnorm_bwd(dh, x_ref[...], g_ref[...])
        gx_ref[...] = dx1_ref[...] + dres
        dg_ref[...] += dg

    tok = lambda w: pl.BlockSpec((tm, w), lambda i: (i + off, 0))
    full = lambda *s: pl.BlockSpec(s, lambda i: (0,) * len(s))
    in_specs = ([pl.BlockSpec((tm // d, d * 3 * GROUP_W), lambda i: (i + off, 0)) for d in DILATIONS] + [tok(REST_W)]
                + [_resident((IN_W, D_MODEL))]
                + [tok(D_MODEL), tok(D_MODEL), full(1, D_MODEL), full(1, D_MODEL), HBM_SPEC])
    return _call(
        body, name=f"in_proj_bwd_{off}", grid=(steps,), in_specs=in_specs,
        out_specs=[tok(D_MODEL), full(1, D_MODEL)],
        out_shape=[jax.ShapeDtypeStruct((seq, D_MODEL), F32), jax.ShapeDtypeStruct((1, D_MODEL), F32)],
        scratch_shapes=[pltpu.VMEM((3 * GROUP_W // LANES, tm, LANES), F32)],
        params=_params(("arbitrary",), 48), args=(*dqkv, drest, w_in, x, dx1, g0, dg_so_far, gx_so_far),
        rider=rider, aliases={len(in_specs) - 1: 0})


def _adamw(w, g, m, v, name):
    rows, cols = w.shape
    tr = _row_tile(rows) if rows % 16 == 0 else rows
    c1 = 1.0 - ADAM_B1 ** ADAM_STEP
    c2 = 1.0 - ADAM_B2 ** ADAM_STEP

    def body(w_ref, g_ref, m_ref, v_ref, go_ref, d_ref, nm_ref, nv_ref):
        gv = g_ref[...]
        go_ref[...] = gv
        nm = ADAM_B1 * m_ref[...] + (1.0 - ADAM_B1) * gv
        nv = ADAM_B2 * v_ref[...] + (1.0 - ADAM_B2) * (gv * gv)
        d_ref[...] = -ADAM_LR * ((nm / c1) / (jnp.sqrt(nv / c2) + ADAM_EPS) + ADAM_WD * w_ref[...])
        nm_ref[...] = nm
        nv_ref[...] = nv

    spec = pl.BlockSpec((tr, cols), lambda i: (i, 0))
    return _pallas(
        body, name=name, grid=(rows // tr,),
        in_specs=[spec] * 4, out_specs=[spec] * 4,
        out_shape=[jax.ShapeDtypeStruct((rows, cols), F32)] * 4,
        compiler_params=_params(("arbitrary",), 32, small=True),
    )(w, g, m, v)


def _place():
    x, y, c = lax.axis_index("x"), lax.axis_index("y"), lax.axis_index("c")
    chips = [(1 - x, y), (x, 1 - y), (1 - x, 1 - y)]
    return x, y, c, chips


class _Exchange:
    def __init__(self, inputs, out_shapes, n_sems, start, finish, aliases=None):
        self.inputs, self.out_shapes, self.n_sems = list(inputs), list(out_shapes), n_sems
        self.start, self.finish, self.aliases = start, finish, dict(aliases or {})

    def scratch(self):
        return [pltpu.SemaphoreType.DMA((self.n_sems,)), pltpu.SemaphoreType.DMA((self.n_sems,))]


def _together(*parts):
    ins = [len(p.inputs) for p in parts]
    outs = [len(p.out_shapes) for p in parts]

    def split(refs, counts):
        pos, pieces = 0, []
        for cnt in counts:
            pieces.append(refs[pos:pos + cnt])
            pos += cnt
        return pieces

    def run(which):
        def go(in_refs, out_refs, *sems):
            for k, (p, i, o) in enumerate(zip(parts, split(in_refs, ins), split(out_refs, outs))):
                getattr(p, which)(i, o, sems[2 * k], sems[2 * k + 1])
        return go

    both = _Exchange([a for p in parts for a in p.inputs], [s for p in parts for s in p.out_shapes], 0, run("start"),
                     run("finish"))
    both.aliases = {sum(ins[:k]) + i: sum(outs[:k]) + o for k, p in enumerate(parts) for i, o in p.aliases.items()}
    both.scratch = lambda: [s for p in parts for s in p.scratch()]
    return both


def _run_exchange(ex, name):
    n_in, n_out = len(ex.inputs), len(ex.out_shapes)

    def body(*refs):
        ins, outs, sems = refs[:n_in], refs[n_in:n_in + n_out], refs[n_in + n_out:]
        ex.start(ins, outs, *sems)
        ex.finish(ins, outs, *sems)

    return _pallas(
        body, name=name, in_specs=[HBM_SPEC] * n_in, out_specs=[HBM_SPEC] * n_out, out_shape=ex.out_shapes,
        scratch_shapes=ex.scratch(), input_output_aliases=ex.aliases,
    )(*ex.inputs)


def _call(body, *, name, grid, in_specs, out_specs, out_shape, scratch_shapes, params, args, rider=None, aliases=None):
    in_specs, out_specs, out_shape, scratch_shapes = list(in_specs), list(out_specs), list(out_shape), list(scratch_shapes)
    aliases = dict(aliases or {})
    args = [_in_hbm(a) for a in args]
    if rider is None:
        outs = _pallas(body, name=name, grid=grid, in_specs=in_specs, out_specs=out_specs, out_shape=out_shape,
                              scratch_shapes=scratch_shapes, input_output_aliases=aliases, compiler_params=params)(*args)
        return list(outs), []
    n_in, n_out, n_scr = len(in_specs), len(out_specs), len(scratch_shapes)
    r_in, r_out = len(rider.inputs), len(rider.out_shapes)

    def wrapped(*refs):
        ins, r_ins = refs[:n_in], refs[n_in:n_in + r_in]
        pos = n_in + r_in
        outs, r_outs = refs[pos:pos + n_out], refs[pos + n_out:pos + n_out + r_out]
        pos += n_out + r_out
        scr, sems = refs[pos:pos + n_scr], refs[pos + n_scr:]
        ids = [pl.program_id(k) for k in range(len(grid))]
        first, last = ids[0] == 0, ids[0] == grid[0] - 1
        for k in range(1, len(grid)):
            first, last = first & (ids[k] == 0), last & (ids[k] == grid[k] - 1)

        @pl.when(first)
        def _():
            rider.start(r_ins, r_outs, *sems)

        body(*ins, *outs, *scr)

        @pl.when(last)
        def _():
            rider.finish(r_ins, r_outs, *sems)

    outs = _pallas(
        wrapped, name=name, grid=grid, in_specs=in_specs + [HBM_SPEC] * r_in, out_specs=out_specs + [HBM_SPEC] * r_out,
        out_shape=out_shape + rider.out_shapes, scratch_shapes=scratch_shapes + rider.scratch(),
        input_output_aliases={**aliases, **{n_in + i: n_out + o for i, o in rider.aliases.items()}}, compiler_params=params,
    )(*args, *rider.inputs)
    return list(outs[:n_out]), list(outs[n_out:])


def _stage_weights(shards):
    n = len(shards)

    def body(*refs):
        ins, outs, stages, sems = refs[:n], refs[n:2 * n], refs[2 * n:3 * n], refs[3 * n]
        x, y, _, _ = _place()
        copies = []
        for t in range(n):
            stages[t][...] = ins[t][...].astype(BF16)
            copies.append(pltpu.make_async_copy(stages[t], outs[t].at[2 * x + y], sems.at[t]))
            copies[-1].start()
        for cp in copies:
            cp.wait()

    assert sum(s.size * 6 for s in shards) <= (CALL_VMEM_MIB - 8) * MIB
    return _pallas(
        body, name="stage_weights", in_specs=[VMEM_SPEC] * n, out_specs=[HBM_SPEC] * n,
        out_shape=[jax.ShapeDtypeStruct((N_CHIPS,) + s.shape, BF16) for s in shards],
        scratch_shapes=[pltpu.VMEM(s.shape, BF16) for s in shards] + [pltpu.SemaphoreType.DMA((n,))],
        compiler_params=pltpu.CompilerParams(vmem_limit_bytes=SMALL_VMEM_MIB * MIB),
    )(*shards)


def _gather(buffers, stage="both", part=(0, 1)):
    n = len(buffers)
    halves = [b.shape[1] // part[1] // 2 for b in buffers]

    def half_of(outs, t, chip, which):
        return outs[t].at[chip, pl.ds((2 * part[0] + which) * halves[t], halves[t]), :]

    def copy(outs, sems, t, k, chip, which, to):
        rows = half_of(outs, t, chip, which)
        return pltpu.make_async_remote_copy(src_ref=rows, dst_ref=rows, send_sem=sems[0].at[6 * t + k],
                                            recv_sem=sems[1].at[6 * t + k], device_id=to, device_id_type=MESH)

    def to_chips(outs, sems, what):
        x, y, c, chips = _place()
        for t in range(n):
            for j, (px, py) in enumerate(chips):
                if what == "start":
                    copy(outs, sems, t, j, 2 * x + y, c, (px, py, c)).start()
                else:
                    copy(outs, sems, t, j, 2 * px + py, c, (px, py, c)).wait_recv()
                    copy(outs, sems, t, j, 2 * x + y, c, (px, py, c)).wait_send()

    def to_sibling(outs, sems, what):
        x, y, c, chips = _place()
        for t in range(n):
            for j, (px, py) in enumerate(chips):
                if what == "start":
                    copy(outs, sems, t, 3 + j, 2 * px + py, c, (x, y, 1 - c)).start()
                else:
                    copy(outs, sems, t, 3 + j, 2 * px + py, 1 - c, (x, y, 1 - c)).wait_recv()
                    copy(outs, sems, t, 3 + j, 2 * px + py, c, (x, y, 1 - c)).wait_send()

    def start(ins, outs, *sems):
        (to_sibling if stage == "pair" else to_chips)(outs, sems, "start")

    def finish(ins, outs, *sems):
        if stage == "both":
            x, y, c, chips = _place()
            for j, (px, py) in enumerate(chips):
                for t in range(n):
                    copy(outs, sems, t, j, 2 * px + py, c, (px, py, c)).wait_recv()
                    copy(outs, sems, t, 3 + j, 2 * px + py, c, (x, y, 1 - c)).start()
            for j, (px, py) in enumerate(chips):
                for t in range(n):
                    copy(outs, sems, t, j, 2 * x + y, c, (px, py, c)).wait_send()
            to_sibling(outs, sems, "finish")
        elif stage == "chips":
            to_chips(outs, sems, "finish")
        else:
            to_sibling(outs, sems, "finish")

    return _Exchange(buffers, [jax.ShapeDtypeStruct(b.shape, b.dtype) for b in buffers], 6 * n, start, finish,
                     aliases={t: t for t in range(n)})


def _pair_exchange(grads):
    n = len(grads)
    halves = [g.shape[1] // 2 for g in grads]

    def copies(ins, outs, send_sems, recv_sems):
        x, y, c, _ = _place()
        return [pltpu.make_async_remote_copy(
            src_ref=ins[t].at[:, pl.ds((1 - c) * halves[t], halves[t]), :], dst_ref=outs[t],
            send_sem=send_sems.at[t], recv_sem=recv_sems.at[t], device_id=(x, y, 1 - c), device_id_type=MESH)
            for t in range(n)]

    def start(*refs):
        for cp in copies(*refs):
            cp.start()

    def finish(*refs):
        for cp in copies(*refs):
            cp.wait()

    return _Exchange(grads, [jax.ShapeDtypeStruct((N_CHIPS, h, g.shape[2]), F32) for g, h in zip(grads, halves)], n,
                     start, finish)


def _row_tile(rows):
    return max(t for t in range(16, 257, 16) if rows % t == 0)


def _pair_add(grad, other, place, name):
    _, rows, cols = grad.shape
    rh = rows // 2
    tr = _row_tile(rh)
    nb = rh // tr

    def body(p_ref, g_ref, a_ref, wire_ref, own_ref):
        s = g_ref[...] + a_ref[...]
        wire_ref[...] = s.astype(BF16)

        @pl.when(pl.program_id(1) == p_ref[1])
        def _():
            own_ref[...] = s

    blk = (None, tr, cols)
    return _pallas(
        body, name=name,
        grid_spec=pltpu.PrefetchScalarGridSpec(
            num_scalar_prefetch=1, grid=(nb, N_CHIPS),
            in_specs=[pl.BlockSpec(blk, lambda i, j, p: (j, p[0] * nb + i, 0)), pl.BlockSpec(blk, lambda i, j, p: (j, i, 0))],
            out_specs=[pl.BlockSpec(blk, lambda i, j, p: (j, i, 0)), pl.BlockSpec((tr, cols), lambda i, j, p: (i, 0))]),
        out_shape=[jax.ShapeDtypeStruct((N_CHIPS, rh, cols), BF16), jax.ShapeDtypeStruct((rh, cols), F32)],
        compiler_params=_params(("arbitrary", "arbitrary"), 32, small=True),
    )(place, grad, other)


def _chip_exchange(wires):
    n = len(wires)

    def copies(ins, outs, send_sems, recv_sems):
        x, y, c, chips = _place()
        return [pltpu.make_async_remote_copy(
            src_ref=ins[t].at[2 * px + py], dst_ref=outs[t].at[j], send_sem=send_sems.at[3 * t + j],
            recv_sem=recv_sems.at[3 * t + j], device_id=(px, py, c), device_id_type=MESH)
            for t in range(n) for j, (px, py) in enumerate(chips)]

    def start(*refs):
        for cp in copies(*refs):
            cp.start()

    def finish(*refs):
        for cp in copies(*refs):
            cp.wait()

    return _Exchange(wires, [jax.ShapeDtypeStruct((3,) + w.shape[1:], BF16) for w in wires], 3 * n, start, finish)


def _chip_add(own, arrived, place, name):
    rh, cols = own.shape
    tr = _row_tile(rh)
    nb = rh // tr

    def body(p_ref, s_ref, b0, b1, b2, o_ref):
        o_ref[...] = ((s_ref[...] + b0[...].astype(F32)) + b1[...].astype(F32)) + b2[...].astype(F32)

    blk = (None, tr, cols)
    return _pallas(
        body, name=name,
        grid_spec=pltpu.PrefetchScalarGridSpec(
            num_scalar_prefetch=1, grid=(nb,),
            in_specs=[pl.BlockSpec((tr, cols), lambda i, p: (i, 0)), pl.BlockSpec(blk, lambda i, p: (0, i, 0)),
                      pl.BlockSpec(blk, lambda i, p: (1, i, 0)), pl.BlockSpec(blk, lambda i, p: (2, i, 0))],
            out_specs=pl.BlockSpec((tr, cols), lambda i, p: (p[0] * nb + i, 0))),
        out_shape=jax.ShapeDtypeStruct((2 * rh, cols), F32),
        compiler_params=_params(("arbitrary",), 32, small=True),
    )(place, own, arrived, arrived, arrived)


def _pair_share(halves):
    n = len(halves)
    rhs = [h.shape[0] // 2 for h in halves]

    def copy(outs, send_sems, recv_sems, t, which):
        x, y, c, _ = _place()
        rows = outs[t].at[pl.ds(which * rhs[t], rhs[t]), :]
        return pltpu.make_async_remote_copy(src_ref=rows, dst_ref=rows, send_sem=send_sems.at[t], recv_sem=recv_sems.at[t],
                                            device_id=(x, y, 1 - c), device_id_type=MESH)

    def start(ins, outs, send_sems, recv_sems):
        c = lax.axis_index("c")
        for t in range(n):
            copy(outs, send_sems, recv_sems, t, c).start()

    def finish(ins, outs, send_sems, recv_sems):
        c = lax.axis_index("c")
        for t in range(n):
            copy(outs, send_sems, recv_sems, t, c).wait_send()
            copy(outs, send_sems, recv_sems, t, 1 - c).wait_recv()

    return _Exchange(halves, [jax.ShapeDtypeStruct(h.shape, F32) for h in halves], n, start, finish,
                     aliases={t: t for t in range(n)})


class _GradReduction:
    def __init__(self, grads, place, tag):
        self.names, self.grads, self.place, self.tag = list(grads), grads, place, tag

    def pair_exchange(self):
        return _pair_exchange([self.grads[n] for n in self.names])

    def chip_exchange(self, others):
        sums = [_pair_add(self.grads[n], o, self.place, f"{self.tag}_pair_add_{n}") for n, o in zip(self.names, others)]
        self.owns = [own for _, own in sums]
        return _chip_exchange([wire for wire, _ in sums])

    def pair_share(self, arrived):
        return _pair_share([_chip_add(own, arr, self.place, f"{self.tag}_chip_add_{n}")
                            for n, own, arr in zip(self.names, self.owns, arrived)])

    def result(self, shared):
        return dict(zip(self.names, shared))


def _all_reduce_small(p):
    rows, lanes = p.shape
    half = rows // 2

    def body(p_ref, o_ref, sib, sums, send_sems, recv_sems):
        x, y, c, chips = _place()
        mine, sibling = 2 * x + y, (x, y, 1 - c)
        swap = pltpu.make_async_remote_copy(src_ref=p_ref, dst_ref=sib, send_sem=send_sems.at[0], recv_sem=recv_sems.at[0],
                                            device_id=sibling, device_id_type=MESH)
        swap.start()
        swap.wait()
        sums[mine] = p_ref[...] + sib[...]

        def copy(k, chip, which, to):
            part = sums.at[chip, pl.ds(which * half, half), :]
            return pltpu.make_async_remote_copy(src_ref=part, dst_ref=part, send_sem=send_sems.at[k], recv_sem=recv_sems.at[k],
                                                device_id=to, device_id_type=MESH)

        for j, (px, py) in enumerate(chips):
            copy(1 + j, mine, c, (px, py, c)).start()
        for j, (px, py) in enumerate(chips):
            copy(1 + j, 2 * px + py, c, (px, py, c)).wait_recv()
            copy(4 + j, 2 * px + py, c, sibling).start()
        for j, (px, py) in enumerate(chips):
            copy(4 + j, 2 * px + py, 1 - c, sibling).wait_recv()
        for j, (px, py) in enumerate(chips):
            copy(1 + j, mine, c, (px, py, c)).wait_send()
            copy(4 + j, 2 * px + py, c, sibling).wait_send()
        o_ref[...] = ((sums[0] + sums[1]) + sums[2]) + sums[3]

    return _pallas(
        body, name="small_all_reduce", in_specs=[VMEM_SPEC], out_specs=VMEM_SPEC,
        out_shape=jax.ShapeDtypeStruct((rows, lanes), F32),
        scratch_shapes=[pltpu.VMEM((rows, lanes), F32), pltpu.VMEM((N_CHIPS, rows, lanes), F32),
                        pltpu.SemaphoreType.DMA((7,)), pltpu.SemaphoreType.DMA((7,))],
        compiler_params=pltpu.CompilerParams(vmem_limit_bytes=SMALL_VMEM_MIB * MIB),
    )(p)


BIG = ("w_in", "w_branch_attn", "w_branch_gmlp", "w_out", "w_mlp_in", "w_mlp_out")
COLUMN_SHARDED = ("w_branch_attn", "w_branch_gmlp", "w_mlp_in")
SMALL = ("norm_pre_mix", "w_spatial", "b_spatial", "ln_v_gain", "ln_v_bias", "norm_post_mix", "norm_pre_mlp", "norm_post_mlp")
ORDER = ("norm_pre_mix", "w_in", "w_spatial", "b_spatial", "ln_v_gain", "ln_v_bias", "w_branch_attn", "w_branch_gmlp",
         "w_out", "norm_post_mix", "norm_pre_mlp", "w_mlp_in", "w_mlp_out", "norm_post_mlp")


def _full_weight(name, gathered):
    if name in COLUMN_SHARDED:
        return jnp.transpose(gathered, (1, 0, 2)).reshape(gathered.shape[1], -1)
    return gathered.reshape(-1, gathered.shape[2])


def _rows8(a):
    a = a.reshape(-1, 128)
    pad = (-a.shape[0]) % 8
    return jnp.pad(a, ((0, pad), (0, 0))) if pad else a


def _qkv_columns(group):
    return [(sec * ATTN_W + group * GROUP_W, sec * ATTN_W + (group + 1) * GROUP_W) for sec in range(3)]


def _device_step(x, target, small, shards, place):
    seq = x.shape[0]
    g0, g1, g2, g3 = small["norm_pre_mix"], small["norm_post_mix"], small["norm_pre_mlp"], small["norm_post_mlp"]
    w_sp = small["w_spatial"]
    b_col = small["b_spatial"].reshape(GMLP_GROUPS, CHUNK, 1)
    ln_g, ln_b = small["ln_v_gain"], small["ln_v_bias"]

    staged = _stage_weights(shards)
    h, tables, (w_in,) = _prepare(x, g0, rider=_gather(staged[:1]))
    w_in = _full_weight("w_in", w_in)
    (*qkv, rest), landed = _in_proj(h[0], w_in, *tables[1], rider=_gather(staged[1:], "chips"))

    o_l, gathered = _attn_fwd(qkv[0], DILATIONS[0], rider=_gather(landed, "pair"))
    full = {n: _full_weight(n, gw) for n, gw in zip(BIG[1:], gathered)}
    for g in range(1, N_GROUPS):
        o_l.extend(_attn_fwd(qkv[g], DILATIONS[g])[0])
    (*ya_l, yg, mg, y, x1), _ = _mix_fwd(o_l, rest, x, w_sp, b_col, ln_g, ln_b, full["w_branch_attn"],
                                        full["w_branch_gmlp"], full["w_out"], g1)
    ya, lse = ya_l[0::2], ya_l[1::2]
    h2, a, dy2, dout, loss8, dg3 = _mlp_fwd(x1, g2, g3, full["w_mlp_in"], full["w_mlp_out"], target)
    d_wmo, _ = _tn_matmul(a, dy2, "grad_w_mlp_out", 1024, 1024, square_a=True)
    mlp_out = _GradReduction({"w_mlp_out": d_wmo.reshape(N_CHIPS, D_FF // N_CHIPS, D_MODEL)}, place, "mlp_out")
    (dap, dx1, dy, dg2, dg1), riding = _mlp_bwd(dy2, a, full["w_mlp_out"], full["w_mlp_in"], dout, x1, y, g2, g1,
                                                 rider=mlp_out.pair_exchange())
    d_wmi, riding = _tn_matmul(h2, dap, "grad_w_mlp_in", 1024, 1024, column_shards=True,
                               rider=mlp_out.chip_exchange(riding))
    mlp_in = _GradReduction({"w_mlp_in": d_wmi}, place, "mlp_in")
    (*dya, drest, d_wout, d_wba, d_wbg, d_wsp, d_bb, d_lg, d_lb), riding = _mix_bwd(
        dy, ya[0], yg, mg, rest, full["w_out"], full["w_branch_attn"], full["w_branch_gmlp"], w_sp, b_col, ln_g, ln_b,
        rider=_together(mlp_out.pair_share(riding), mlp_in.pair_exchange()))
    reduced = mlp_out.result(riding[:1])
    mix = _GradReduction({"w_branch_attn": d_wba, "w_branch_gmlp": d_wbg,
                          "w_out": d_wout.reshape(N_CHIPS, D_MODEL // N_CHIPS, D_MODEL)}, place, "mix")
    attn = lambda g, rider: _attn_bwd(qkv[g], dya[g], ya[g], lse[g], *tables[DILATIONS[g]], DILATIONS[g], rider=rider)
    dqkv0, riding = attn(0, _together(mlp_in.chip_exchange(riding[1:]), mix.pair_exchange()))
    dqkv1, riding = attn(1, _together(mlp_in.pair_share(riding[:1]), mix.chip_exchange(riding[1:])))
    reduced.update(mlp_in.result(riding[:1]))
    dqkv2, riding = attn(2, mix.pair_share(riding[1:]))
    reduced.update(mix.result(riding))
    dqkv = [dqkv0, dqkv1, dqkv2]

    d_qkv = [_tn_matmul_residue(dqkv[g], h[g], dil, f"grad_w_in_qkv{g}") for g, dil in enumerate(DILATIONS)]
    d_rest, _ = _tn_matmul(drest, h[0], "grad_w_in_rest", 1024, 1024)
    d_win = jnp.concatenate([d_qkv[g][s * GROUP_W:(s + 1) * GROUP_W] for s in range(3) for g in range(N_GROUPS)]
                            + [d_rest], axis=0)
    first = _GradReduction({"w_in": d_win.reshape(N_CHIPS, IN_W // N_CHIPS, D_MODEL)}, place, "w_in")
    tiles = seq // IN_PROJ_BWD_TM
    so_far = (lax.empty((seq, D_MODEL), F32), jnp.zeros((1, D_MODEL), F32))
    in_bwd = lambda so_far, span, rider: _in_proj_bwd(dqkv, drest, w_in, x, dx1, g0, so_far, span, rider=rider)
    so_far, riding = in_bwd(so_far, (0, 3 * tiles // 8), first.pair_exchange())
    (grad_x, dg0), riding = in_bwd(so_far, (3 * tiles // 8, 5 * tiles // 8), first.chip_exchange(riding))
    reduced.update(first.result(_run_exchange(first.pair_share(riding), "w_in_pair_share")))
    little = {"norm_pre_mix": dg0, "w_spatial": d_wsp, "b_spatial": d_bb[:, :, 0], "ln_v_gain": d_lg, "ln_v_bias": d_lb,
              "norm_post_mix": dg1, "norm_pre_mlp": dg2, "norm_post_mlp": dg3}
    return loss8, grad_x, reduced, little


def kernel(x, norm_pre_mix, w_in, w_spatial, b_spatial, ln_v_gain, ln_v_bias, w_branch_attn, w_branch_gmlp, w_out, norm_post_mix, norm_pre_mlp, w_mlp_in, w_mlp_out, norm_post_mlp, loss_target, m_norm_pre_mix, m_w_in, m_w_spatial, m_b_spatial, m_ln_v_gain, m_ln_v_bias, m_w_branch_attn, m_w_branch_gmlp, m_w_out, m_norm_post_mix, m_norm_pre_mlp, m_w_mlp_in, m_w_mlp_out, m_norm_post_mlp, v_norm_pre_mix, v_w_in, v_w_spatial, v_b_spatial, v_ln_v_gain, v_ln_v_bias, v_w_branch_attn, v_w_branch_gmlp, v_w_out, v_norm_post_mix, v_norm_pre_mlp, v_w_mlp_in, v_w_mlp_out, v_norm_post_mlp):
    given = dict(norm_pre_mix=norm_pre_mix, w_in=w_in, w_spatial=w_spatial, b_spatial=b_spatial, ln_v_gain=ln_v_gain,
                 ln_v_bias=ln_v_bias, w_branch_attn=w_branch_attn, w_branch_gmlp=w_branch_gmlp, w_out=w_out,
                 norm_post_mix=norm_post_mix, norm_pre_mlp=norm_pre_mlp, w_mlp_in=w_mlp_in, w_mlp_out=w_mlp_out,
                 norm_post_mlp=norm_post_mlp)
    moments_m = dict(norm_pre_mix=m_norm_pre_mix, w_in=m_w_in, w_spatial=m_w_spatial, b_spatial=m_b_spatial,
                     ln_v_gain=m_ln_v_gain, ln_v_bias=m_ln_v_bias, w_branch_attn=m_w_branch_attn,
                     w_branch_gmlp=m_w_branch_gmlp, w_out=m_w_out, norm_post_mix=m_norm_post_mix,
                     norm_pre_mlp=m_norm_pre_mlp, w_mlp_in=m_w_mlp_in, w_mlp_out=m_w_mlp_out, norm_post_mlp=m_norm_post_mlp)
    moments_v = dict(norm_pre_mix=v_norm_pre_mix, w_in=v_w_in, w_spatial=v_w_spatial, b_spatial=v_b_spatial,
                     ln_v_gain=v_ln_v_gain, ln_v_bias=v_ln_v_bias, w_branch_attn=v_w_branch_attn,
                     w_branch_gmlp=v_w_branch_gmlp, w_out=v_w_out, norm_post_mix=v_norm_post_mix,
                     norm_pre_mlp=v_norm_pre_mlp, w_mlp_in=v_w_mlp_in, w_mlp_out=v_w_mlp_out, norm_post_mlp=v_norm_post_mlp)
    cx, cy, cc = lax.axis_index("x"), lax.axis_index("y"), lax.axis_index("c")

    shards = [given[n][0].T if n == "w_in" else given[n][0] for n in BIG]
    small = {n: given[n][0] if given[n].ndim > 2 else given[n] for n in SMALL}
    place = jnp.stack([cc, 2 * cx + cy]).astype(jnp.int32)
    loss8, grad_x, grad_shard, grads = _device_step(x[0], loss_target[0], small, shards, place)

    packed = jnp.concatenate([_rows8(grads[n]) for n in SMALL] + [loss8], axis=0)
    summed = _all_reduce_small(packed)
    loss = summed[packed.shape[0] - loss8.shape[0], 0]
    row = 0
    for n in SMALL:
        shape = given[n][0].shape
        cnt = -(-(given[n][0].size // 128) // 8) * 8
        grad_shard[n] = summed[row:row + given[n][0].size // 128].reshape(shape)
        row += cnt

    grad_out, deltas, new_m, new_v = {}, {}, {}, {}
    for n in ORDER:
        shape = given[n].shape
        if n == "w_in":
            outs = _adamw(given[n][0].T, grad_shard[n], moments_m[n][0].T, moments_v[n][0].T, "adamw_" + n)
            outs = [o.T for o in outs]
        else:
            two_d = (-1, shape[-1])
            outs = _adamw(given[n].reshape(two_d), grad_shard[n].reshape(two_d), moments_m[n].reshape(two_d),
                          moments_v[n].reshape(two_d), "adamw_" + n)
        grad_out[n], deltas[n], new_m[n], new_v[n] = [o.reshape(shape) for o in outs]
    return (loss, grad_x[None], *[grad_out[n] for n in ORDER], *[deltas[n] for n in ORDER], *[new_m[n] for n in ORDER],
            *[new_v[n] for n in ORDER])
```

```python
import math

import jax
import jax.numpy as jnp
from jax import lax
from jax.experimental import pallas as pl
from jax.experimental.pallas import tpu as pltpu

F32 = jnp.float32
BF16 = jnp.bfloat16
MESH = pl.DeviceIdType.MESH

D_MODEL = 1024
HEAD_DIM = 64
HEADS_PER_GROUP = 4
GROUP_W = HEADS_PER_GROUP * HEAD_DIM
DILATIONS = (1, 4, 16)
N_GROUPS = len(DILATIONS)
ATTN_W = N_GROUPS * GROUP_W
QKV_W = 3 * ATTN_W
GMLP_W = 512
GMLP_GROUPS = 4
CHUNK = 128
REST_W = 2 * GMLP_W + 2 * D_MODEL
IN_W = QKV_W + REST_W
D_FF = 4096
QBLK = 128
ROPE_THETA = 10000.0
EPS = 1e-6
NEG = -1e30
SCALE = HEAD_DIM ** -0.5
N_CHIPS = 4

ADAM_LR = 0.001
ADAM_B1 = 0.9
ADAM_B2 = 0.999
ADAM_EPS = 1e-08
ADAM_WD = 0.01
ADAM_STEP = 10

MIB = 1024 * 1024
HBM_SPEC = pl.BlockSpec(memory_space=pltpu.HBM)
VMEM_SPEC = pl.BlockSpec(memory_space=pltpu.VMEM)


MLP_FWD_TM = 512
MLP_TM = 512


CALL_VMEM_MIB = 56
SMALL_VMEM_MIB = 32


def _params(semantics, vmem_mib, small=False):
    assert vmem_mib <= CALL_VMEM_MIB
    return pltpu.CompilerParams(dimension_semantics=semantics,
                                vmem_limit_bytes=(SMALL_VMEM_MIB if small else CALL_VMEM_MIB) * MIB)


def _in_hbm(a):
    return pltpu.with_memory_space_constraint(a, pltpu.HBM) if a.size * a.dtype.itemsize >= MIB else a


def _pallas(body, **kwargs):
    return pl.pallas_call(body, **kwargs)


def _resident(shape):
    return pl.BlockSpec(shape, lambda *_: (0,) * len(shape), pipeline_mode=pl.Buffered(1))


def _dot(a, b):
    return jnp.dot(a, b, preferred_element_type=F32)


def _dot_nt(a, b):
    return lax.dot_general(a, b, (((1,), (1,)), ((), ())), preferred_element_type=F32)


def _dot_tn(a, b):
    return lax.dot_general(a, b, (((0,), (0,)), ((), ())), preferred_element_type=F32)


_GELU_C = math.sqrt(2.0 / math.pi)


def _gelu(x):
    return x * (0.5 * (1.0 + jnp.tanh(_GELU_C * (x + 0.044715 * (x * x * x)))))


def _gelu_grad(x):
    t = jnp.tanh(_GELU_C * (x + 0.044715 * (x * x * x)))
    return 0.5 * (1.0 + t) + 0.5 * x * (1.0 - t * t) * (_GELU_C * (1.0 + 3.0 * 0.044715 * (x * x)))


def _rsqrt_ms(v):
    return lax.rsqrt(jnp.mean(v * v, axis=-1, keepdims=True) + EPS)


def _rmsnorm_bwd(dn, src, gain):
    r = _rsqrt_ms(src)
    t = gain * dn
    dgain = jnp.sum(dn * (src * r), axis=0, keepdims=True)
    dsrc = r * t - src * ((r * r * r) * jnp.mean(t * src, axis=-1, keepdims=True))
    return dsrc, dgain


def _rot_half(v):
    w = v.shape[-1]
    lane = lax.broadcasted_iota(jnp.int32, v.shape, v.ndim - 1)
    return jnp.where((lane % HEAD_DIM) < HEAD_DIM // 2, pltpu.roll(v, w - HEAD_DIM // 2, v.ndim - 1),
                     pltpu.roll(v, HEAD_DIM // 2, v.ndim - 1))


def _head_masks(shape):
    lane = lax.broadcasted_iota(jnp.int32, shape, 1)
    return [(lane >= h * HEAD_DIM) & (lane < (h + 1) * HEAD_DIM) for h in range(HEADS_PER_GROUP)]


def _head_stack(block, hmask):
    zero = jnp.zeros((), block.dtype)
    return jnp.concatenate([jnp.where(hm, block, zero) for hm in hmask], axis=0)


LANES = 128


def _put_residue(slab, val, out_ref, dil, width, col0):
    tm, w = val.shape
    if dil == 1:
        out_ref[:, col0:col0 + w] = val.astype(out_ref.dtype)
        return
    for k in range(w // LANES):
        slab[k] = val[:, k * LANES:(k + 1) * LANES]
    for r in range(dil):
        for k in range(w // LANES):
            c = r * width + col0 + k * LANES
            out_ref[:, c:c + LANES] = slab[k, pl.ds(r, tm // dil, stride=dil), :].astype(out_ref.dtype)


def _get_tokens(slab, in_ref, dil, width, col0, w):
    if dil == 1:
        return in_ref[:, col0:col0 + w].astype(F32)
    rows = in_ref.shape[0]
    for r in range(dil):
        for k in range(w // LANES):
            c = r * width + col0 + k * LANES
            slab[k, pl.ds(r, rows, stride=dil), :] = in_ref[:, c:c + LANES].astype(F32)
    return jnp.concatenate([slab[k] for k in range(w // LANES)], axis=1)


def _prepare(x, g0, rider=None):
    seq = x.shape[0]
    half = HEAD_DIM // 2
    inv_freq = ROPE_THETA ** (-jnp.arange(half, dtype=F32) / half)
    freq = jnp.tile(inv_freq, LANES // half).reshape(1, LANES)
    tm = 256

    def body(x_ref, g_ref, f_ref, *refs):
        h_refs, tabs, slab = refs[:N_GROUPS], refs[N_GROUPS:3 * N_GROUPS], refs[-1]
        xv = x_ref[...]
        hf = (xv * _rsqrt_ms(xv)) * g_ref[...]
        for g, dil in enumerate(DILATIONS):
            _put_residue(slab, hf, h_refs[g], dil, D_MODEL, 0)
        row = lax.broadcasted_iota(jnp.int32, (tm, LANES), 0) + pl.program_id(0) * tm
        lane = lax.broadcasted_iota(jnp.int32, (tm, LANES), 1)
        ang = row.astype(F32) * f_ref[...]
        cos = jnp.cos(ang)
        sin = jnp.where((lane % HEAD_DIM) < half, -jnp.sin(ang), jnp.sin(ang))
        for i, dil in enumerate(DILATIONS):
            for tab, val in ((tabs[2 * i], cos), (tabs[2 * i + 1], sin)):
                slab[0] = val
                for r in range(dil):
                    piece = slab[0, pl.ds(r, tm // dil, stride=dil), :] if dil > 1 else val
                    for k in range(GROUP_W // LANES):
                        tab[:, r * GROUP_W + k * LANES:r * GROUP_W + (k + 1) * LANES] = piece

    outs, riding = _call(
        body, name="prepare", grid=(seq // tm,),
        in_specs=[pl.BlockSpec((tm, D_MODEL), lambda i: (i, 0)), pl.BlockSpec((1, D_MODEL), lambda i: (0, 0)),
                  pl.BlockSpec((1, LANES), lambda i: (0, 0))],
        out_specs=[pl.BlockSpec((tm // d, d * D_MODEL), lambda i: (i, 0)) for d in DILATIONS]
        + [pl.BlockSpec((tm // d, d * GROUP_W), lambda i: (i, 0)) for d in DILATIONS for _ in range(2)],
        out_shape=[jax.ShapeDtypeStruct((seq // d, d * D_MODEL), BF16) for d in DILATIONS]
        + [jax.ShapeDtypeStruct((seq // d, d * GROUP_W), F32) for d in DILATIONS for _ in range(2)],
        scratch_shapes=[pltpu.VMEM((D_MODEL // LANES, tm, LANES), F32)],
        params=_params(("arbitrary",), 32), args=(x, g0, freq), rider=rider)
    tabs = outs[N_GROUPS:]
    return outs[:N_GROUPS], {d: (tabs[2 * i], tabs[2 * i + 1]) for i, d in enumerate(DILATIONS)}, riding


def _in_proj(h, w_in, cos_t, sin_t, rider=None):
    seq = h.shape[0]
    tm, tn = 512, GROUP_W
    n_qk = 2 * ATTN_W // tn
    n_qkv = QKV_W // tn

    def body(h_ref, w_ref, cos_ref, sin_ref, *refs):
        qkv_refs, rest_ref, slab = refs[:N_GROUPS], refs[N_GROUPS], refs[-1]
        hb = h_ref[...]
        cos, sin = cos_ref[...], sin_ref[...]
        for j in range(IN_W // tn):
            p = _dot_nt(hb, w_ref[j * tn:(j + 1) * tn, :])
            if j < n_qkv:
                if j < n_qk:
                    p = p * cos + _rot_half(p) * sin
                section, g = divmod(j, N_GROUPS)
                _put_residue(slab, p, qkv_refs[g], DILATIONS[g], 3 * GROUP_W, section * GROUP_W)
            else:
                rest_ref[:, (j - n_qkv) * tn:(j - n_qkv + 1) * tn] = p.astype(BF16)

    return _call(
        body, name="in_proj", grid=(seq // tm,),
        in_specs=[pl.BlockSpec((tm, D_MODEL), lambda i: (i, 0)),
                  _resident((IN_W, D_MODEL)),
                  pl.BlockSpec((tm, GROUP_W), lambda i: (i, 0)),
                  pl.BlockSpec((tm, GROUP_W), lambda i: (i, 0))],
        out_specs=[pl.BlockSpec((tm // d, d * 3 * GROUP_W), lambda i: (i, 0)) for d in DILATIONS]
        + [pl.BlockSpec((tm, REST_W), lambda i: (i, 0))],
        out_shape=[jax.ShapeDtypeStruct((seq // d, d * 3 * GROUP_W), BF16) for d in DILATIONS]
        + [jax.ShapeDtypeStruct((seq, REST_W), BF16)],
        scratch_shapes=[pltpu.VMEM((GROUP_W // LANES, tm, LANES), F32)],
        params=_params(("arbitrary",), 48), args=(h, w_in, cos_t, sin_t), rider=rider)


def _band_masks():
    qi = lax.broadcasted_iota(jnp.int32, (QBLK, QBLK), 0)
    kj = lax.broadcasted_iota(jnp.int32, (QBLK, QBLK), 1)
    return kj <= qi, kj >= qi


def _attn_tile(length):
    return min(512, length)


def _attn_fwd(qkv, dil, rider=None):
    length = qkv.shape[0]
    tq = _attn_tile(length)
    nsub = tq // QBLK
    nblk = length // tq

    def body(q_ref, k_ref, v_ref, kp_ref, vp_ref, o_ref, l_ref):
        n = pl.program_id(1)
        mask_c, mask_p0 = _band_masks()
        hmask = _head_masks((QBLK, GROUP_W))
        zero = jnp.zeros((), BF16)
        for b in range(nsub):
            rows = slice(b * QBLK, (b + 1) * QBLK)
            q = q_ref[rows, :]
            kc, vc = k_ref[rows, :], v_ref[rows, :]
            if b == 0:
                kp, vp = kp_ref[...], vp_ref[...]
                mask_p = mask_p0 & (n > 0)
            else:
                prow = slice((b - 1) * QBLK, b * QBLK)
                kp, vp = k_ref[prow, :], v_ref[prow, :]
                mask_p = mask_p0
            o_acc = jnp.zeros((QBLK, GROUP_W), F32)
            l_acc = jnp.zeros((QBLK, GROUP_W), F32)
            for h in range(HEADS_PER_GROUP):
                hm = hmask[h]
                sc = jnp.where(mask_c, _dot_nt(q, jnp.where(hm, kc, zero)) * SCALE, NEG)
                sp = jnp.where(mask_p, _dot_nt(q, jnp.where(hm, kp, zero)) * SCALE, NEG)
                m = jnp.maximum(jnp.max(sc, axis=-1, keepdims=True), jnp.max(sp, axis=-1, keepdims=True))
                pc, pp = jnp.exp(sc - m), jnp.exp(sp - m)
                den = jnp.sum(pc, axis=-1, keepdims=True) + jnp.sum(pp, axis=-1, keepdims=True)
                pv = _dot(pc.astype(BF16), jnp.where(hm, vc, zero)) + _dot(pp.astype(BF16), jnp.where(hm, vp, zero))
                o_acc = o_acc + pv / den
                l_acc = l_acc + jnp.where(hm, m + jnp.log(den), 0.0)
            o_ref[rows, :] = o_acc.astype(BF16)
            l_ref[rows, :] = l_acc

    cur = lambda sec: pl.BlockSpec((tq, GROUP_W), lambda r, n: (n, r * 3 + sec))
    prev = lambda sec: pl.BlockSpec((QBLK, GROUP_W), lambda r, n: (jnp.maximum(n * nsub - 1, 0), r * 3 + sec))
    return _call(
        body, name=f"attn_fwd_d{dil}", grid=(dil, nblk),
        in_specs=[cur(0), cur(1), cur(2), prev(1), prev(2)],
        out_specs=[pl.BlockSpec((tq, GROUP_W), lambda r, n: (n, r))] * 2,
        out_shape=[jax.ShapeDtypeStruct((length, dil * GROUP_W), BF16),
                   jax.ShapeDtypeStruct((length, dil * GROUP_W), F32)], scratch_shapes=[],
        params=_params(("arbitrary", "arbitrary"), 32), args=(qkv, qkv, qkv, qkv, qkv), rider=rider)


def _attn_bwd(qkv, dy, y, lse, cos_t, sin_t, dil, rider=None):
    length = qkv.shape[0]
    tq = _attn_tile(length)
    nsub = tq // QBLK
    nblk = length // tq

    def body(q_ref, k_ref, v_ref, kp_ref, vp_ref, qn_ref, dy_ref, y_ref, l_ref, dyn_ref, yn_ref, ln_ref,
             cos_ref, sin_ref, out_ref, dq_s, dk_s, dv_s):
        n = pl.program_id(1)
        mask_c, mask_p0 = _band_masks()
        hmask = _head_masks((QBLK, GROUP_W))
        sub = lambda ref, b: ref[b * QBLK:(b + 1) * QBLK, :]
        kbd = [_head_stack(kp_ref[...], hmask)] + [_head_stack(sub(k_ref, b), hmask) for b in range(nsub)]
        vbd = [_head_stack(vp_ref[...], hmask)] + [_head_stack(sub(v_ref, b), hmask) for b in range(nsub)]
        dq_s[...] = jnp.zeros(dq_s.shape, F32)

        def query_terms(q, dyv, yv, lv):
            prod = dyv * yv
            return dict(
                q=q, dy=dyv.astype(BF16), q_heads=[jnp.where(hm, q, jnp.zeros((), BF16)) for hm in hmask],
                dy_heads=[jnp.where(hm, dyv, 0.0).astype(BF16) for hm in hmask],
                delta=[jnp.sum(jnp.where(hm, prod, 0.0), axis=-1, keepdims=True) for hm in hmask],
                lse=[jnp.max(jnp.where(hm, lv, NEG), axis=-1, keepdims=True) for hm in hmask])

        queries = [query_terms(sub(q_ref, b), sub(dy_ref, b), sub(y_ref, b), sub(l_ref, b)) for b in range(nsub)]
        queries.append(query_terms(qn_ref[...], dyn_ref[...], yn_ref[...], ln_ref[...]))
        rows_of = lambda items: items[0] if len(items) == 1 else jnp.concatenate(items, axis=0)
        for kb in range(nsub + 1):
            seen = [(kb - 1, mask_c)] if kb >= 1 else []
            if kb == 0:
                seen.append((0, mask_p0 & (n > 0)))
            elif kb < nsub:
                seen.append((kb, mask_p0))
            else:
                seen.append((nsub, mask_p0 & (n < nblk - 1)))
            qs = [queries[b] for b, _ in seen]
            mask = rows_of([m for _, m in seen])
            s = _dot_nt(rows_of([t["q"] for t in qs]), kbd[kb]) * SCALE
            dp = _dot_nt(rows_of([t["dy"] for t in qs]), vbd[kb])
            ps, dss = [], []
            for h in range(HEADS_PER_GROUP):
                cols = slice(h * QBLK, (h + 1) * QBLK)
                p = jnp.exp(jnp.where(mask, s[:, cols] - rows_of([t["lse"][h] for t in qs]), NEG))
                ps.append(p.astype(BF16))
                dss.append((p * (dp[:, cols] - rows_of([t["delta"][h] for t in qs]))).astype(BF16))
            dq = _dot(jnp.concatenate(dss, axis=1), kbd[kb]) * SCALE
            for i, (b, _) in enumerate(seen):
                if b < nsub:
                    dq_s[b * QBLK:(b + 1) * QBLK, :] += dq[i * QBLK:(i + 1) * QBLK, :]
            if kb >= 1:
                krows = slice((kb - 1) * QBLK, kb * QBLK)
                head_rows = lambda key: jnp.concatenate([t[key][h] for h in range(HEADS_PER_GROUP) for t in qs], axis=0)
                dv_s[krows, :] = _dot_tn(jnp.concatenate(ps, axis=0), head_rows("dy_heads"))
                dk_s[krows, :] = _dot_tn(jnp.concatenate(dss, axis=0), head_rows("q_heads")) * SCALE
        cos, sin = cos_ref[...], sin_ref[...]
        dq, dk = dq_s[...], dk_s[...]
        out_ref[:, 0:GROUP_W] = (dq * cos - _rot_half(dq) * sin).astype(BF16)
        out_ref[:, GROUP_W:2 * GROUP_W] = (dk * cos - _rot_half(dk) * sin).astype(BF16)
        out_ref[:, 2 * GROUP_W:3 * GROUP_W] = dv_s[...].astype(BF16)

    cur = lambda sec: pl.BlockSpec((tq, GROUP_W), lambda r, n: (n, r * 3 + sec))
    prev = lambda sec: pl.BlockSpec((QBLK, GROUP_W), lambda r, n: (jnp.maximum(n * nsub - 1, 0), r * 3 + sec))
    nxt_q = pl.BlockSpec((QBLK, GROUP_W), lambda r, n: (jnp.minimum((n + 1) * nsub, nblk * nsub - 1), r * 3))
    tok = pl.BlockSpec((tq, GROUP_W), lambda r, n: (n, r))
    tok_next = pl.BlockSpec((QBLK, GROUP_W), lambda r, n: (jnp.minimum((n + 1) * nsub, nblk * nsub - 1), r))
    (out,), riding = _call(
        body, name=f"attn_bwd_d{dil}", grid=(dil, nblk),
        in_specs=[cur(0), cur(1), cur(2), prev(1), prev(2), nxt_q,
                  tok, tok, tok, tok_next, tok_next, tok_next, tok, tok],
        out_specs=[pl.BlockSpec((tq, 3 * GROUP_W), lambda r, n: (n, r))],
        out_shape=[jax.ShapeDtypeStruct((length, dil * 3 * GROUP_W), BF16)],
        scratch_shapes=[pltpu.VMEM((tq, GROUP_W), F32)] * 3,
        params=_params(("arbitrary", "arbitrary"), 32),
        args=(qkv, qkv, qkv, qkv, qkv, qkv, dy, y, lse, dy, y, lse, cos_t, sin_t), rider=rider)
    return out, riding


def _layernorm_stats(z):
    mu = jnp.mean(z, axis=-1, keepdims=True)
    zc = z - mu
    rstd = lax.rsqrt(jnp.mean(zc * zc, axis=-1, keepdims=True) + EPS)
    return zc * rstd, rstd


def _tril_mask():
    row = lax.broadcasted_iota(jnp.int32, (CHUNK, CHUNK), 0)
    col = lax.broadcasted_iota(jnp.int32, (CHUNK, CHUNK), 1)
    return col <= row


def _mix_fwd(o_l, rest, x, w_sp, b_col, ln_g, ln_b, w_ba, w_bg, w_out, g1, rider=None):
    seq = x.shape[0]
    tm = 256

    def body(o0, l0, o1, l1, o2, l2, up_ref, zp_ref, gap_ref, gbp_ref, x_ref, wsp_ref, bcol_ref, lg_ref, lb_ref,
             wba_ref, wbg_ref, wout_ref, g1_ref, ya0, lj0, ya1, lj1, ya2, lj2, yg_ref, mg_ref, y_ref, x1_ref, slab):
        outs = [_get_tokens(slab, o, d, GROUP_W, 0, GROUP_W) for o, d in zip((o0, o1, o2), DILATIONS)]
        lses = [_get_tokens(slab, l, d, GROUP_W, 0, GROUP_W) for l, d in zip((l0, l1, l2), DILATIONS)]
        m = jnp.maximum(jnp.maximum(lses[0], lses[1]), lses[2])
        es = [jnp.exp(l - m) for l in lses]
        tot = es[0] + es[1] + es[2]
        ya = (es[0] * outs[0] + es[1] * outs[1] + es[2] * outs[2]) / tot
        lj = m + jnp.log(tot)
        for ya_ref, lj_ref, d in zip((ya0, ya1, ya2), (lj0, lj1, lj2), DILATIONS):
            _put_residue(slab, ya, ya_ref, d, GROUP_W, 0)
            _put_residue(slab, lj, lj_ref, d, GROUP_W, 0)
        zhat, _ = _layernorm_stats(_gelu(zp_ref[...].astype(F32)))
        zln = (zhat * lg_ref[...] + lb_ref[...]).astype(BF16)
        u = _gelu(up_ref[...].astype(F32))
        tril = _tril_mask()
        for g in range(GMLP_GROUPS):
            wm = jnp.where(tril, wsp_ref[g], 0.0).astype(BF16)
            cols = slice(g * CHUNK, (g + 1) * CHUNK)
            for c in range(tm // CHUNK):
                rows = slice(c * CHUNK, (c + 1) * CHUNK)
                sz = _dot(wm, zln[rows, cols]) + bcol_ref[g]
                yg_ref[rows, cols] = (u[rows, cols] * sz).astype(BF16)
        a = _dot(ya.astype(BF16), wba_ref[...])
        bm = _dot(yg_ref[...], wbg_ref[...])
        merged = (jax.nn.sigmoid(gap_ref[...].astype(F32)) * a + jax.nn.sigmoid(gbp_ref[...].astype(F32)) * bm).astype(BF16)
        mg_ref[...] = merged
        yv = _dot(merged, wout_ref[...])
        y_ref[...] = yv.astype(BF16)
        x1_ref[...] = x_ref[...] + (yv * _rsqrt_ms(yv)) * g1_ref[...]

    tok = lambda w: pl.BlockSpec((tm, w), lambda i: (i, 0))
    res = lambda d: pl.BlockSpec((tm // d, d * GROUP_W), lambda i: (i, 0))
    full = lambda *s: pl.BlockSpec(s, lambda i: (0,) * len(s))
    res_specs = [res(d) for d in DILATIONS for _ in range(2)]
    return _call(
        body, name="mix_fwd", grid=(seq // tm,),
        in_specs=res_specs + [
            pl.BlockSpec((tm, GMLP_W), lambda i: (i, 0)), pl.BlockSpec((tm, GMLP_W), lambda i: (i, 1)),
            pl.BlockSpec((tm, D_MODEL), lambda i: (i, 1)), pl.BlockSpec((tm, D_MODEL), lambda i: (i, 2)),
            tok(D_MODEL), full(GMLP_GROUPS, CHUNK, CHUNK), full(GMLP_GROUPS, CHUNK, 1), full(1, GMLP_W), full(1, GMLP_W),
            full(GROUP_W, D_MODEL), full(GMLP_W, D_MODEL), full(D_MODEL, D_MODEL), full(1, D_MODEL)],
        out_specs=res_specs + [tok(GMLP_W), tok(D_MODEL), tok(D_MODEL), tok(D_MODEL)],
        out_shape=[jax.ShapeDtypeStruct((seq // d, d * GROUP_W), F32) for d in DILATIONS for _ in range(2)]
        + [jax.ShapeDtypeStruct((seq, GMLP_W), BF16), jax.ShapeDtypeStruct((seq, D_MODEL), BF16),
           jax.ShapeDtypeStruct((seq, D_MODEL), BF16), jax.ShapeDtypeStruct((seq, D_MODEL), F32)],
        scratch_shapes=[pltpu.VMEM((GROUP_W // LANES, tm, LANES), F32)],
        params=_params(("arbitrary",), 48),
        args=(*o_l, rest, rest, rest, rest, x, w_sp, b_col, ln_g, ln_b, w_ba, w_bg, w_out, g1), rider=rider)


def _mlp_fwd(x1, g2, g3, w_mi, w_mo, target):
    seq = x1.shape[0]
    tm, tf = MLP_FWD_TM, 512

    def body(x1_ref, g2_ref, g3_ref, wmi_ref, wmo_ref, t_ref, h2_ref, a_ref, dy2_ref, dout_ref, loss_ref, dg3_ref, sq_s):
        @pl.when(pl.program_id(0) == 0)
        def _():
            loss_ref[...] = jnp.zeros(loss_ref.shape, F32)
            dg3_ref[...] = jnp.zeros(dg3_ref.shape, F32)

        xv = x1_ref[...]
        hb = ((xv * _rsqrt_ms(xv)) * g2_ref[...]).astype(BF16)
        h2_ref[...] = hb
        for j in range(D_FF // tf):
            cols = slice(j * tf, (j + 1) * tf)
            a = jnp.maximum(_dot(hb, wmi_ref[:, cols]), 0.0)
            a_ref[:, cols] = a.astype(BF16)
            sq_s[:, cols] = (a * a).astype(BF16)
        y2 = _dot(sq_s[...], wmo_ref[...])
        r3 = _rsqrt_ms(y2)
        out = xv + (y2 * r3) * g3_ref[...]
        diff = out - t_ref[...]
        tile_loss = 0.5 * jnp.sum(jnp.mean(diff * diff, axis=-1, keepdims=True), axis=0, keepdims=True)
        loss_ref[...] += jnp.broadcast_to(tile_loss, loss_ref.shape)
        dout = diff * (1.0 / D_MODEL)
        dout_ref[...] = dout
        dy2, dg3 = _rmsnorm_bwd(dout, y2, g3_ref[...])
        dy2_ref[...] = dy2.astype(BF16)
        dg3_ref[...] += dg3

    tok = lambda w: pl.BlockSpec((tm, w), lambda i: (i, 0))
    vec = pl.BlockSpec((1, D_MODEL), lambda i: (0, 0))
    return _pallas(
        body, name="mlp_fwd", grid=(seq // tm,),
        in_specs=[tok(D_MODEL), vec, vec, _resident((D_MODEL, D_FF)), _resident((D_FF, D_MODEL)), tok(D_MODEL)],
        out_specs=[tok(D_MODEL), tok(D_FF), tok(D_MODEL), tok(D_MODEL), pl.BlockSpec((8, 128), lambda i: (0, 0)), vec],
        out_shape=[jax.ShapeDtypeStruct((seq, D_MODEL), BF16), jax.ShapeDtypeStruct((seq, D_FF), BF16),
                   jax.ShapeDtypeStruct((seq, D_MODEL), BF16), jax.ShapeDtypeStruct((seq, D_MODEL), F32),
                   jax.ShapeDtypeStruct((8, 128), F32), jax.ShapeDtypeStruct((1, D_MODEL), F32)],
        scratch_shapes=[pltpu.VMEM((tm, D_FF), BF16)],
        compiler_params=_params(("arbitrary",), 56),
    )(*map(_in_hbm, (x1, g2, g3, w_mi, w_mo, target)))


def _mlp_bwd(dy2, a, w_mo, w_mi, dout, x1, y, g2, g1, rider=None):
    seq = x1.shape[0]
    tm, tf = MLP_TM, 512

    def body(dy2_ref, a_ref, wmo_ref, wmi_ref, dout_ref, x1_ref, y_ref, g2_ref, g1_ref,
             dap_ref, dx1_ref, dy_ref, dg2_ref, dg1_ref):
        @pl.when(pl.program_id(0) == 0)
        def _():
            dg2_ref[...] = jnp.zeros(dg2_ref.shape, F32)
            dg1_ref[...] = jnp.zeros(dg1_ref.shape, F32)

        dy2v = dy2_ref[...]
        for j in range(D_FF // tf):
            cols = slice(j * tf, (j + 1) * tf)
            da2 = _dot_nt(dy2v, wmo_ref[cols, :])
            dap_ref[:, cols] = (da2 * (2.0 * a_ref[:, cols].astype(F32))).astype(BF16)
        dh2 = _dot_nt(dap_ref[...], wmi_ref[...])
        dres, dg2 = _rmsnorm_bwd(dh2, x1_ref[...], g2_ref[...])
        dx1 = dout_ref[...] + dres
        dx1_ref[...] = dx1
        dg2_ref[...] += dg2
        dyv, dg1 = _rmsnorm_bwd(dx1, y_ref[...].astype(F32), g1_ref[...])
        dy_ref[...] = dyv.astype(BF16)
        dg1_ref[...] += dg1

    tok = lambda w: pl.BlockSpec((tm, w), lambda i: (i, 0))
    vec = pl.BlockSpec((1, D_MODEL), lambda i: (0, 0))
    return _call(
        body, name="mlp_bwd", grid=(seq // tm,),
        in_specs=[tok(D_MODEL), tok(D_FF), _resident((D_FF, D_MODEL)), _resident((D_MODEL, D_FF)),
                  tok(D_MODEL), tok(D_MODEL), tok(D_MODEL), vec, vec],
        out_specs=[tok(D_FF), tok(D_MODEL), tok(D_MODEL), vec, vec],
        out_shape=[jax.ShapeDtypeStruct((seq, D_FF), BF16), jax.ShapeDtypeStruct((seq, D_MODEL), F32),
                   jax.ShapeDtypeStruct((seq, D_MODEL), BF16), jax.ShapeDtypeStruct((1, D_MODEL), F32),
                   jax.ShapeDtypeStruct((1, D_MODEL), F32)], scratch_shapes=[],
        params=_params(("arbitrary",), 56), args=(dy2, a, w_mo, w_mi, dout, x1, y, g2, g1), rider=rider)


def _tn_matmul(a, b, name, bm, bn, square_a=False, column_shards=False, rider=None):
    seq, m = a.shape
    n = b.shape[1]
    ts = 2048

    def body(a_ref, b_ref, o_ref):
        @pl.when(pl.program_id(2) == 0)
        def _():
            o_ref[...] = jnp.zeros(o_ref.shape, F32)

        av = a_ref[...]
        if square_a:
            af = av.astype(F32)
            av = (af * af).astype(BF16)
        o_ref[...] += _dot_tn(av, b_ref[...])

    if column_shards:
        out_spec = pl.BlockSpec((None, bm, bn), lambda mi, ni, s: (ni, mi, 0))
        out_shape = jax.ShapeDtypeStruct((n // bn, m, bn), F32)
    else:
        out_spec = pl.BlockSpec((bm, bn), lambda mi, ni, s: (mi, ni))
        out_shape = jax.ShapeDtypeStruct((m, n), F32)
    (out,), riding = _call(
        body, name=name, grid=(m // bm, n // bn, seq // ts),
        in_specs=[pl.BlockSpec((ts, bm), lambda mi, ni, s: (s, mi)), pl.BlockSpec((ts, bn), lambda mi, ni, s: (s, ni))],
        out_specs=[out_spec], out_shape=[out_shape], scratch_shapes=[],
        params=_params(("arbitrary", "arbitrary", "arbitrary"), 40), args=(a, b), rider=rider)
    return out, riding


def _tn_matmul_residue(a, b, dil, name):
    length = a.shape[0]
    m, n = a.shape[1] // dil, b.shape[1] // dil
    ts = min(1024, length)

    def body(a_ref, b_ref, o_ref):
        @pl.when((pl.program_id(0) == 0) & (pl.program_id(1) == 0))
        def _():
            o_ref[...] = jnp.zeros(o_ref.shape, F32)

        o_ref[...] += _dot_tn(a_ref[...], b_ref[...])

    return _pallas(
        body, name=name, grid=(dil, length // ts),
        in_specs=[pl.BlockSpec((ts, m), lambda r, s: (s, r)), pl.BlockSpec((ts, n), lambda r, s: (s, r))],
        out_specs=pl.BlockSpec((m, n), lambda r, s: (0, 0)),
        out_shape=jax.ShapeDtypeStruct((m, n), F32),
        compiler_params=_params(("arbitrary", "arbitrary"), 40),
    )(_in_hbm(a), _in_hbm(b))


def _mix_bwd(dy, ya, yg, mg, rest, w_out, w_ba, w_bg, w_sp, b_col, ln_g, ln_b, rider=None):
    seq = dy.shape[0]
    tm = 256

    def body(dy_ref, ya_ref, yg_ref, mg_ref, up_ref, zp_ref, gap_ref, gbp_ref, wout_ref, wba_ref, wbg_ref,
             wsp_ref, bcol_ref, lg_ref, lb_ref,
             dya0, dya1, dya2, dpr_ref, dwout_ref, dwba_ref, dwbg_ref, dwsp_ref, dbb_ref, dlg_ref, dlb_ref,
             dzln_s, du_s, slab):
        @pl.when(pl.program_id(0) == 0)
        def _():
            for ref in (dwout_ref, dwba_ref, dwbg_ref, dwsp_ref, dbb_ref, dlg_ref, dlb_ref):
                ref[...] = jnp.zeros(ref.shape, F32)

        dyv = dy_ref[...]
        dm = _dot_nt(dyv, wout_ref[...])
        dwout_ref[...] += _dot_tn(mg_ref[...], dyv)
        yab = ya_ref[...].astype(BF16)
        ygb = yg_ref[...]
        a = _dot(yab, wba_ref[...])
        bm = _dot(ygb, wbg_ref[...])
        ga = jax.nn.sigmoid(gap_ref[...].astype(F32))
        gb = jax.nn.sigmoid(gbp_ref[...].astype(F32))
        dpr_ref[:, 2 * GMLP_W:2 * GMLP_W + D_MODEL] = (dm * a * (ga * (1.0 - ga))).astype(BF16)
        dpr_ref[:, 2 * GMLP_W + D_MODEL:REST_W] = (dm * bm * (gb * (1.0 - gb))).astype(BF16)
        da = (dm * ga).astype(BF16)
        db = (dm * gb).astype(BF16)
        dwba = _dot_tn(yab, da)
        dwbg = _dot_tn(ygb, db)
        shard_w = D_MODEL // N_CHIPS
        for j in range(N_CHIPS):
            dwba_ref[j] += dwba[:, j * shard_w:(j + 1) * shard_w]
            dwbg_ref[j] += dwbg[:, j * shard_w:(j + 1) * shard_w]
        dya = _dot_nt(da, wba_ref[...])
        for dya_ref, d in zip((dya0, dya1, dya2), DILATIONS):
            _put_residue(slab, dya, dya_ref, d, GROUP_W, 0)
        dyg = _dot_nt(db, wbg_ref[...])

        zp = zp_ref[...].astype(F32)
        zhat, rstd = _layernorm_stats(_gelu(zp))
        lg = lg_ref[...]
        zln = (zhat * lg + lb_ref[...]).astype(BF16)
        up = up_ref[...].astype(F32)
        u = _gelu(up)
        tril = _tril_mask()
        for g in range(GMLP_GROUPS):
            wm = jnp.where(tril, wsp_ref[g], 0.0).astype(BF16)
            cols = slice(g * CHUNK, (g + 1) * CHUNK)
            for c in range(tm // CHUNK):
                rows = slice(c * CHUNK, (c + 1) * CHUNK)
                zb = zln[rows, cols]
                sz = _dot(wm, zb) + bcol_ref[g]
                dyg_cg = dyg[rows, cols]
                du_s[rows, cols] = dyg_cg * sz
                dsz = dyg_cg * u[rows, cols]
                dszb = dsz.astype(BF16)
                dbb_ref[g] += jnp.broadcast_to(jnp.sum(dsz, axis=-1, keepdims=True), (CHUNK, CHUNK))
                dwsp_ref[g] += jnp.where(tril, _dot_nt(dszb, zb), 0.0)
                dzln_s[rows, cols] = _dot_tn(wm, dszb)
        dzln = dzln_s[...]
        dlg_ref[...] += jnp.sum(dzln * zhat, axis=0, keepdims=True)
        dlb_ref[...] += jnp.sum(dzln, axis=0, keepdims=True)
        dzh = dzln * lg
        dz = rstd * (dzh - jnp.mean(dzh, axis=-1, keepdims=True) - zhat * jnp.mean(dzh * zhat, axis=-1, keepdims=True))
        dpr_ref[:, GMLP_W:2 * GMLP_W] = (dz * _gelu_grad(zp)).astype(BF16)
        dpr_ref[:, 0:GMLP_W] = (du_s[...] * _gelu_grad(up)).astype(BF16)

    tok = lambda w: pl.BlockSpec((tm, w), lambda i: (i, 0))
    full = lambda *s: pl.BlockSpec(s, lambda i: (0,) * len(s))
    return _call(
        body, name="mix_bwd", grid=(seq // tm,),
        in_specs=[tok(D_MODEL), tok(GROUP_W), tok(GMLP_W), tok(D_MODEL),
                  pl.BlockSpec((tm, GMLP_W), lambda i: (i, 0)), pl.BlockSpec((tm, GMLP_W), lambda i: (i, 1)),
                  pl.BlockSpec((tm, D_MODEL), lambda i: (i, 1)), pl.BlockSpec((tm, D_MODEL), lambda i: (i, 2)),
                  full(D_MODEL, D_MODEL), full(GROUP_W, D_MODEL), full(GMLP_W, D_MODEL),
                  full(GMLP_GROUPS, CHUNK, CHUNK), full(GMLP_GROUPS, CHUNK, 1), full(1, GMLP_W), full(1, GMLP_W)],
        out_specs=[pl.BlockSpec((tm // d, d * GROUP_W), lambda i: (i, 0)) for d in DILATIONS]
        + [tok(REST_W), full(D_MODEL, D_MODEL), full(N_CHIPS, GROUP_W, D_MODEL // N_CHIPS),
           full(N_CHIPS, GMLP_W, D_MODEL // N_CHIPS),
           full(GMLP_GROUPS, CHUNK, CHUNK), full(GMLP_GROUPS, CHUNK, CHUNK), full(1, GMLP_W), full(1, GMLP_W)],
        out_shape=[jax.ShapeDtypeStruct((seq // d, d * GROUP_W), F32) for d in DILATIONS]
        + [jax.ShapeDtypeStruct((seq, REST_W), BF16),
           jax.ShapeDtypeStruct((D_MODEL, D_MODEL), F32), jax.ShapeDtypeStruct((N_CHIPS, GROUP_W, D_MODEL // N_CHIPS), F32),
           jax.ShapeDtypeStruct((N_CHIPS, GMLP_W, D_MODEL // N_CHIPS), F32),
           jax.ShapeDtypeStruct((GMLP_GROUPS, CHUNK, CHUNK), F32),
           jax.ShapeDtypeStruct((GMLP_GROUPS, CHUNK, CHUNK), F32), jax.ShapeDtypeStruct((1, GMLP_W), F32),
           jax.ShapeDtypeStruct((1, GMLP_W), F32)],
        scratch_shapes=[pltpu.VMEM((tm, GMLP_W), F32), pltpu.VMEM((tm, GMLP_W), F32),
                        pltpu.VMEM((GROUP_W // LANES, tm, LANES), F32)],
        params=_params(("arbitrary",), 56),
        args=(dy, ya, yg, mg, rest, rest, rest, rest, w_out, w_ba, w_bg, w_sp, b_col, ln_g, ln_b), rider=rider)


IN_PROJ_BWD_TM = 256


def _in_proj_bwd(dqkv, drest, w_in, x, dx1, g0, so_far, span, rider=None):
    seq = x.shape[0]
    tm = IN_PROJ_BWD_TM
    off, steps = span
    gx_so_far, dg_so_far = so_far

    def body(d0, d1, d2, dr_ref, w_ref, x_ref, dx1_ref, g_ref, dg_in_ref, gx_in_ref, gx_ref, dg_ref, slab):
        @pl.when(pl.program_id(0) == 0)
        def _():
            dg_ref[...] = dg_in_ref[...]

        dh = _dot(dr_ref[...], w_ref[QKV_W:, :])
        for g, (d_ref, dil) in enumerate(zip((d0, d1, d2), DILATIONS)):
            piece = d_ref[...] if dil == 1 else _get_tokens(slab, d_ref, dil, 3 * GROUP_W, 0, 3 * GROUP_W).astype(BF16)
            for section, (lo, hi) in enumerate(_qkv_columns(g)):
                dh = dh + _dot(piece[:, section * GROUP_W:(section + 1) * GROUP_W], w_ref[lo:hi, :])
        dres, dg = _rmsnorm_bwd(dh, x_ref[...], g_ref[...])
        gx_ref[...] = dx1_ref[...] + dres
        dg_ref[...] += dg

    tok = lambda w: pl.BlockSpec((tm, w), lambda i: (i + off, 0))
    full = lambda *s: pl.BlockSpec(s, lambda i: (0,) * len(s))
    in_specs = ([pl.BlockSpec((tm // d, d * 3 * GROUP_W), lambda i: (i + off, 0)) for d in DILATIONS] + [tok(REST_W)]
                + [_resident((IN_W, D_MODEL))]
                + [tok(D_MODEL), tok(D_MODEL), full(1, D_MODEL), full(1, D_MODEL), HBM_SPEC])
    return _call(
        body, name=f"in_proj_bwd_{off}", grid=(steps,), in_specs=in_specs,
        out_specs=[tok(D_MODEL), full(1, D_MODEL)],
        out_shape=[jax.ShapeDtypeStruct((seq, D_MODEL), F32), jax.ShapeDtypeStruct((1, D_MODEL), F32)],
        scratch_shapes=[pltpu.VMEM((3 * GROUP_W // LANES, tm, LANES), F32)],
        params=_params(("arbitrary",), 48), args=(*dqkv, drest, w_in, x, dx1, g0, dg_so_far, gx_so_far),
        rider=rider, aliases={len(in_specs) - 1: 0})


def _adamw(w, g, m, v, name):
    rows, cols = w.shape
    tr = _row_tile(rows) if rows % 16 == 0 else rows
    c1 = 1.0 - ADAM_B1 ** ADAM_STEP
    c2 = 1.0 - ADAM_B2 ** ADAM_STEP

    def body(w_ref, g_ref, m_ref, v_ref, go_ref, d_ref, nm_ref, nv_ref):
        gv = g_ref[...]
        go_ref[...] = gv
        nm = ADAM_B1 * m_ref[...] + (1.0 - ADAM_B1) * gv
        nv = ADAM_B2 * v_ref[...] + (1.0 - ADAM_B2) * (gv * gv)
        d_ref[...] = -ADAM_LR * ((nm / c1) / (jnp.sqrt(nv / c2) + ADAM_EPS) + ADAM_WD * w_ref[...])
        nm_ref[...] = nm
        nv_ref[...] = nv

    spec = pl.BlockSpec((tr, cols), lambda i: (i, 0))
    return _pallas(
        body, name=name, grid=(rows // tr,),
        in_specs=[spec] * 4, out_specs=[spec] * 4,
        out_shape=[jax.ShapeDtypeStruct((rows, cols), F32)] * 4,
        compiler_params=_params(("arbitrary",), 32, small=True),
    )(w, g, m, v)


def _place():
    x, y, c = lax.axis_index("x"), lax.axis_index("y"), lax.axis_index("c")
    chips = [(1 - x, y), (x, 1 - y), (1 - x, 1 - y)]
    return x, y, c, chips


class _Exchange:
    def __init__(self, inputs, out_shapes, n_sems, start, finish, aliases=None):
        self.inputs, self.out_shapes, self.n_sems = list(inputs), list(out_shapes), n_sems
        self.start, self.finish, self.aliases = start, finish, dict(aliases or {})

    def scratch(self):
        return [pltpu.SemaphoreType.DMA((self.n_sems,)), pltpu.SemaphoreType.DMA((self.n_sems,))]


def _together(*parts):
    ins = [len(p.inputs) for p in parts]
    outs = [len(p.out_shapes) for p in parts]

    def split(refs, counts):
        pos, pieces = 0, []
        for cnt in counts:
            pieces.append(refs[pos:pos + cnt])
            pos += cnt
        return pieces

    def run(which):
        def go(in_refs, out_refs, *sems):
            for k, (p, i, o) in enumerate(zip(parts, split(in_refs, ins), split(out_refs, outs))):
                getattr(p, which)(i, o, sems[2 * k], sems[2 * k + 1])
        return go

    both = _Exchange([a for p in parts for a in p.inputs], [s for p in parts for s in p.out_shapes], 0, run("start"),
                     run("finish"))
    both.aliases = {sum(ins[:k]) + i: sum(outs[:k]) + o for k, p in enumerate(parts) for i, o in p.aliases.items()}
    both.scratch = lambda: [s for p in parts for s in p.scratch()]
    return both


def _run_exchange(ex, name):
    n_in, n_out = len(ex.inputs), len(ex.out_shapes)

    def body(*refs):
        ins, outs, sems = refs[:n_in], refs[n_in:n_in + n_out], refs[n_in + n_out:]
        ex.start(ins, outs, *sems)
        ex.finish(ins, outs, *sems)

    return _pallas(
        body, name=name, in_specs=[HBM_SPEC] * n_in, out_specs=[HBM_SPEC] * n_out, out_shape=ex.out_shapes,
        scratch_shapes=ex.scratch(), input_output_aliases=ex.aliases,
    )(*ex.inputs)


def _call(body, *, name, grid, in_specs, out_specs, out_shape, scratch_shapes, params, args, rider=None, aliases=None):
    in_specs, out_specs, out_shape, scratch_shapes = list(in_specs), list(out_specs), list(out_shape), list(scratch_shapes)
    aliases = dict(aliases or {})
    args = [_in_hbm(a) for a in args]
    if rider is None:
        outs = _pallas(body, name=name, grid=grid, in_specs=in_specs, out_specs=out_specs, out_shape=out_shape,
                              scratch_shapes=scratch_shapes, input_output_aliases=aliases, compiler_params=params)(*args)
        return list(outs), []
    n_in, n_out, n_scr = len(in_specs), len(out_specs), len(scratch_shapes)
    r_in, r_out = len(rider.inputs), len(rider.out_shapes)

    def wrapped(*refs):
        ins, r_ins = refs[:n_in], refs[n_in:n_in + r_in]
        pos = n_in + r_in
        outs, r_outs = refs[pos:pos + n_out], refs[pos + n_out:pos + n_out + r_out]
        pos += n_out + r_out
        scr, sems = refs[pos:pos + n_scr], refs[pos + n_scr:]
        ids = [pl.program_id(k) for k in range(len(grid))]
        first, last = ids[0] == 0, ids[0] == grid[0] - 1
        for k in range(1, len(grid)):
            first, last = first & (ids[k] == 0), last & (ids[k] == grid[k] - 1)

        @pl.when(first)
        def _():
            rider.start(r_ins, r_outs, *sems)

        body(*ins, *outs, *scr)

        @pl.when(last)
        def _():
            rider.finish(r_ins, r_outs, *sems)

    outs = _pallas(
        wrapped, name=name, grid=grid, in_specs=in_specs + [HBM_SPEC] * r_in, out_specs=out_specs + [HBM_SPEC] * r_out,
        out_shape=out_shape + rider.out_shapes, scratch_shapes=scratch_shapes + rider.scratch(),
        input_output_aliases={**aliases, **{n_in + i: n_out + o for i, o in rider.aliases.items()}}, compiler_params=params,
    )(*args, *rider.inputs)
    return list(outs[:n_out]), list(outs[n_out:])


def _stage_weights(shards):
    n = len(shards)

    def body(*refs):
        ins, outs, stages, sems = refs[:n], refs[n:2 * n], refs[2 * n:3 * n], refs[3 * n]
        x, y, _, _ = _place()
        copies = []
        for t in range(n):
            stages[t][...] = ins[t][...].astype(BF16)
            copies.append(pltpu.make_async_copy(stages[t], outs[t].at[2 * x + y], sems.at[t]))
            copies[-1].start()
        for cp in copies:
            cp.wait()

    assert sum(s.size * 6 for s in shards) <= (CALL_VMEM_MIB - 8) * MIB
    return _pallas(
        body, name="stage_weights", in_specs=[VMEM_SPEC] * n, out_specs=[HBM_SPEC] * n,
        out_shape=[jax.ShapeDtypeStruct((N_CHIPS,) + s.shape, BF16) for s in shards],
        scratch_shapes=[pltpu.VMEM(s.shape, BF16) for s in shards] + [pltpu.SemaphoreType.DMA((n,))],
        compiler_params=pltpu.CompilerParams(vmem_limit_bytes=SMALL_VMEM_MIB * MIB),
    )(*shards)


def _gather(buffers, stage="both", part=(0, 1)):
    n = len(buffers)
    halves = [b.shape[1] // part[1] // 2 for b in buffers]

    def half_of(outs, t, chip, which):
        return outs[t].at[chip, pl.ds((2 * part[0] + which) * halves[t], halves[t]), :]

    def copy(outs, sems, t, k, chip, which, to):
        rows = half_of(outs, t, chip, which)
        return pltpu.make_async_remote_copy(src_ref=rows, dst_ref=rows, send_sem=sems[0].at[6 * t + k],
                                            recv_sem=sems[1].at[6 * t + k], device_id=to, device_id_type=MESH)

    def to_chips(outs, sems, what):
        x, y, c, chips = _place()
        for t in range(n):
            for j, (px, py) in enumerate(chips):
                if what == "start":
                    copy(outs, sems, t, j, 2 * x + y, c, (px, py, c)).start()
                else:
                    copy(outs, sems, t, j, 2 * px + py, c, (px, py, c)).wait_recv()
                    copy(outs, sems, t, j, 2 * x + y, c, (px, py, c)).wait_send()

    def to_sibling(outs, sems, what):
        x, y, c, chips = _place()
        for t in range(n):
            for j, (px, py) in enumerate(chips):
                if what == "start":
                    copy(outs, sems, t, 3 + j, 2 * px + py, c, (x, y, 1 - c)).start()
                else:
                    copy(outs, sems, t, 3 + j, 2 * px + py, 1 - c, (x, y, 1 - c)).wait_recv()
                    copy(outs, sems, t, 3 + j, 2 * px + py, c, (x, y, 1 - c)).wait_send()

    def start(ins, outs, *sems):
        (to_sibling if stage == "pair" else to_chips)(outs, sems, "start")

    def finish(ins, outs, *sems):
        if stage == "both":
            x, y, c, chips = _place()
            for j, (px, py) in enumerate(chips):
                for t in range(n):
                    copy(outs, sems, t, j, 2 * px + py, c, (px, py, c)).wait_recv()
                    copy(outs, sems, t, 3 + j, 2 * px + py, c, (x, y, 1 - c)).start()
            for j, (px, py) in enumerate(chips):
                for t in range(n):
                    copy(outs, sems, t, j, 2 * x + y, c, (px, py, c)).wait_send()
            to_sibling(outs, sems, "finish")
        elif stage == "chips":
            to_chips(outs, sems, "finish")
        else:
            to_sibling(outs, sems, "finish")

    return _Exchange(buffers, [jax.ShapeDtypeStruct(b.shape, b.dtype) for b in buffers], 6 * n, start, finish,
                     aliases={t: t for t in range(n)})


def _pair_exchange(grads):
    n = len(grads)
    halves = [g.shape[1] // 2 for g in grads]

    def copies(ins, outs, send_sems, recv_sems):
        x, y, c, _ = _place()
        return [pltpu.make_async_remote_copy(
            src_ref=ins[t].at[:, pl.ds((1 - c) * halves[t], halves[t]), :], dst_ref=outs[t],
            send_sem=send_sems.at[t], recv_sem=recv_sems.at[t], device_id=(x, y, 1 - c), device_id_type=MESH)
            for t in range(n)]

    def start(*refs):
        for cp in copies(*refs):
            cp.start()

    def finish(*refs):
        for cp in copies(*refs):
            cp.wait()

    return _Exchange(grads, [jax.ShapeDtypeStruct((N_CHIPS, h, g.shape[2]), F32) for g, h in zip(grads, halves)], n,
                     start, finish)


def _row_tile(rows):
    return max(t for t in range(16, 257, 16) if rows % t == 0)


def _pair_add(grad, other, place, name):
    _, rows, cols = grad.shape
    rh = rows // 2
    tr = _row_tile(rh)
    nb = rh // tr

    def body(p_ref, g_ref, a_ref, wire_ref, own_ref):
        s = g_ref[...] + a_ref[...]
        wire_ref[...] = s.astype(BF16)

        @pl.when(pl.program_id(1) == p_ref[1])
        def _():
            own_ref[...] = s

    blk = (None, tr, cols)
    return _pallas(
        body, name=name,
        grid_spec=pltpu.PrefetchScalarGridSpec(
            num_scalar_prefetch=1, grid=(nb, N_CHIPS),
            in_specs=[pl.BlockSpec(blk, lambda i, j, p: (j, p[0] * nb + i, 0)), pl.BlockSpec(blk, lambda i, j, p: (j, i, 0))],
            out_specs=[pl.BlockSpec(blk, lambda i, j, p: (j, i, 0)), pl.BlockSpec((tr, cols), lambda i, j, p: (i, 0))]),
        out_shape=[jax.ShapeDtypeStruct((N_CHIPS, rh, cols), BF16), jax.ShapeDtypeStruct((rh, cols), F32)],
        compiler_params=_params(("arbitrary", "arbitrary"), 32, small=True),
    )(place, grad, other)


def _chip_exchange(wires):
    n = len(wires)

    def copies(ins, outs, send_sems, recv_sems):
        x, y, c, chips = _place()
        return [pltpu.make_async_remote_copy(
            src_ref=ins[t].at[2 * px + py], dst_ref=outs[t].at[j], send_sem=send_sems.at[3 * t + j],
            recv_sem=recv_sems.at[3 * t + j], device_id=(px, py, c), device_id_type=MESH)
            for t in range(n) for j, (px, py) in enumerate(chips)]

    def start(*refs):
        for cp in copies(*refs):
            cp.start()

    def finish(*refs):
        for cp in copies(*refs):
            cp.wait()

    return _Exchange(wires, [jax.ShapeDtypeStruct((3,) + w.shape[1:], BF16) for w in wires], 3 * n, start, finish)


def _chip_add(own, arrived, place, name):
    rh, cols = own.shape
    tr = _row_tile(rh)
    nb = rh // tr

    def body(p_ref, s_ref, b0, b1, b2, o_ref):
        o_ref[...] = ((s_ref[...] + b0[...].astype(F32)) + b1[...].astype(F32)) + b2[...].astype(F32)

    blk = (None, tr, cols)
    return _pallas(
        body, name=name,
        grid_spec=pltpu.PrefetchScalarGridSpec(
            num_scalar_prefetch=1, grid=(nb,),
            in_specs=[pl.BlockSpec((tr, cols), lambda i, p: (i, 0)), pl.BlockSpec(blk, lambda i, p: (0, i, 0)),
                      pl.BlockSpec(blk, lambda i, p: (1, i, 0)), pl.BlockSpec(blk, lambda i, p: (2, i, 0))],
            out_specs=pl.BlockSpec((tr, cols), lambda i, p: (p[0] * nb + i, 0))),
        out_shape=jax.ShapeDtypeStruct((2 * rh, cols), F32),
        compiler_params=_params(("arbitrary",), 32, small=True),
    )(place, own, arrived, arrived, arrived)


def _pair_share(halves):
    n = len(halves)
    rhs = [h.shape[0] // 2 for h in halves]

    def copy(outs, send_sems, recv_sems, t, which):
        x, y, c, _ = _place()
        rows = outs[t].at[pl.ds(which * rhs[t], rhs[t]), :]
        return pltpu.make_async_remote_copy(src_ref=rows, dst_ref=rows, send_sem=send_sems.at[t], recv_sem=recv_sems.at[t],
                                            device_id=(x, y, 1 - c), device_id_type=MESH)

    def start(ins, outs, send_sems, recv_sems):
        c = lax.axis_index("c")
        for t in range(n):
            copy(outs, send_sems, recv_sems, t, c).start()

    def finish(ins, outs, send_sems, recv_sems):
        c = lax.axis_index("c")
        for t in range(n):
            copy(outs, send_sems, recv_sems, t, c).wait_send()
            copy(outs, send_sems, recv_sems, t, 1 - c).wait_recv()

    return _Exchange(halves, [jax.ShapeDtypeStruct(h.shape, F32) for h in halves], n, start, finish,
                     aliases={t: t for t in range(n)})


class _GradReduction:
    def __init__(self, grads, place, tag):
        self.names, self.grads, self.place, self.tag = list(grads), grads, place, tag

    def pair_exchange(self):
        return _pair_exchange([self.grads[n] for n in self.names])

    def chip_exchange(self, others):
        sums = [_pair_add(self.grads[n], o, self.place, f"{self.tag}_pair_add_{n}") for n, o in zip(self.names, others)]
        self.owns = [own for _, own in sums]
        return _chip_exchange([wire for wire, _ in sums])

    def pair_share(self, arrived):
        return _pair_share([_chip_add(own, arr, self.place, f"{self.tag}_chip_add_{n}")
                            for n, own, arr in zip(self.names, self.owns, arrived)])

    def result(self, shared):
        return dict(zip(self.names, shared))


def _all_reduce_small(p):
    rows, lanes = p.shape
    half = rows // 2

    def body(p_ref, o_ref, sib, sums, send_sems, recv_sems):
        x, y, c, chips = _place()
        mine, sibling = 2 * x + y, (x, y, 1 - c)
        swap = pltpu.make_async_remote_copy(src_ref=p_ref, dst_ref=sib, send_sem=send_sems.at[0], recv_sem=recv_sems.at[0],
                                            device_id=sibling, device_id_type=MESH)
        swap.start()
        swap.wait()
        sums[mine] = p_ref[...] + sib[...]

        def copy(k, chip, which, to):
            part = sums.at[chip, pl.ds(which * half, half), :]
            return pltpu.make_async_remote_copy(src_ref=part, dst_ref=part, send_sem=send_sems.at[k], recv_sem=recv_sems.at[k],
                                                device_id=to, device_id_type=MESH)

        for j, (px, py) in enumerate(chips):
            copy(1 + j, mine, c, (px, py, c)).start()
        for j, (px, py) in enumerate(chips):
            copy(1 + j, 2 * px + py, c, (px, py, c)).wait_recv()
            copy(4 + j, 2 * px + py, c, sibling).start()
        for j, (px, py) in enumerate(chips):
            copy(4 + j, 2 * px + py, 1 - c, sibling).wait_recv()
        for j, (px, py) in enumerate(chips):
            copy(1 + j, mine, c, (px, py, c)).wait_send()
            copy(4 + j, 2 * px + py, c, sibling).wait_send()
        o_ref[...] = ((sums[0] + sums[1]) + sums[2]) + sums[3]

    return _pallas(
        body, name="small_all_reduce", in_specs=[VMEM_SPEC], out_specs=VMEM_SPEC,
        out_shape=jax.ShapeDtypeStruct((rows, lanes), F32),
        scratch_shapes=[pltpu.VMEM((rows, lanes), F32), pltpu.VMEM((N_CHIPS, rows, lanes), F32),
                        pltpu.SemaphoreType.DMA((7,)), pltpu.SemaphoreType.DMA((7,))],
        compiler_params=pltpu.CompilerParams(vmem_limit_bytes=SMALL_VMEM_MIB * MIB),
    )(p)


BIG = ("w_in", "w_branch_attn", "w_branch_gmlp", "w_out", "w_mlp_in", "w_mlp_out")
COLUMN_SHARDED = ("w_branch_attn", "w_branch_gmlp", "w_mlp_in")
SMALL = ("norm_pre_mix", "w_spatial", "b_spatial", "ln_v_gain", "ln_v_bias", "norm_post_mix", "norm_pre_mlp", "norm_post_mlp")
ORDER = ("norm_pre_mix", "w_in", "w_spatial", "b_spatial", "ln_v_gain", "ln_v_bias", "w_branch_attn", "w_branch_gmlp",
         "w_out", "norm_post_mix", "norm_pre_mlp", "w_mlp_in", "w_mlp_out", "norm_post_mlp")


def _full_weight(name, gathered):
    if name in COLUMN_SHARDED:
        return jnp.transpose(gathered, (1, 0, 2)).reshape(gathered.shape[1], -1)
    return gathered.reshape(-1, gathered.shape[2])


def _rows8(a):
    a = a.reshape(-1, 128)
    pad = (-a.shape[0]) % 8
    return jnp.pad(a, ((0, pad), (0, 0))) if pad else a


def _qkv_columns(group):
    return [(sec * ATTN_W + group * GROUP_W, sec * ATTN_W + (group + 1) * GROUP_W) for sec in range(3)]


def _device_step(x, target, small, shards, place):
    seq = x.shape[0]
    g0, g1, g2, g3 = small["norm_pre_mix"], small["norm_post_mix"], small["norm_pre_mlp"], small["norm_post_mlp"]
    w_sp = small["w_spatial"]
    b_col = small["b_spatial"].reshape(GMLP_GROUPS, CHUNK, 1)
    ln_g, ln_b = small["ln_v_gain"], small["ln_v_bias"]

    staged = _stage_weights(shards)
    h, tables, (w_in,) = _prepare(x, g0, rider=_gather(staged[:1]))
    w_in = _full_weight("w_in", w_in)
    (*qkv, rest), landed = _in_proj(h[0], w_in, *tables[1], rider=_gather(staged[1:], "chips"))

    o_l, gathered = _attn_fwd(qkv[0], DILATIONS[0], rider=_gather(landed, "pair"))
    full = {n: _full_weight(n, gw) for n, gw in zip(BIG[1:], gathered)}
    for g in range(1, N_GROUPS):
        o_l.extend(_attn_fwd(qkv[g], DILATIONS[g])[0])
    (*ya_l, yg, mg, y, x1), _ = _mix_fwd(o_l, rest, x, w_sp, b_col, ln_g, ln_b, full["w_branch_attn"],
                                        full["w_branch_gmlp"], full["w_out"], g1)
    ya, lse = ya_l[0::2], ya_l[1::2]
    h2, a, dy2, dout, loss8, dg3 = _mlp_fwd(x1, g2, g3, full["w_mlp_in"], full["w_mlp_out"], target)
    d_wmo, _ = _tn_matmul(a, dy2, "grad_w_mlp_out", 1024, 1024, square_a=True)
    mlp_out = _GradReduction({"w_mlp_out": d_wmo.reshape(N_CHIPS, D_FF // N_CHIPS, D_MODEL)}, place, "mlp_out")
    (dap, dx1, dy, dg2, dg1), riding = _mlp_bwd(dy2, a, full["w_mlp_out"], full["w_mlp_in"], dout, x1, y, g2, g1,
                                                 rider=mlp_out.pair_exchange())
    d_wmi, riding = _tn_matmul(h2, dap, "grad_w_mlp_in", 1024, 1024, column_shards=True,
                               rider=mlp_out.chip_exchange(riding))
    mlp_in = _GradReduction({"w_mlp_in": d_wmi}, place, "mlp_in")
    (*dya, drest, d_wout, d_wba, d_wbg, d_wsp, d_bb, d_lg, d_lb), riding = _mix_bwd(
        dy, ya[0], yg, mg, rest, full["w_out"], full["w_branch_attn"], full["w_branch_gmlp"], w_sp, b_col, ln_g, ln_b,
        rider=_together(mlp_out.pair_share(riding), mlp_in.pair_exchange()))
    reduced = mlp_out.result(riding[:1])
    mix = _GradReduction({"w_branch_attn": d_wba, "w_branch_gmlp": d_wbg,
                          "w_out": d_wout.reshape(N_CHIPS, D_MODEL // N_CHIPS, D_MODEL)}, place, "mix")
    attn = lambda g, rider: _attn_bwd(qkv[g], dya[g], ya[g], lse[g], *tables[DILATIONS[g]], DILATIONS[g], rider=rider)
    dqkv0, riding = attn(0, _together(mlp_in.chip_exchange(riding[1:]), mix.pair_exchange()))
    dqkv1, riding = attn(1, _together(mlp_in.pair_share(riding[:1]), mix.chip_exchange(riding[1:])))
    reduced.update(mlp_in.result(riding[:1]))
    dqkv2, riding = attn(2, mix.pair_share(riding[1:]))
    reduced.update(mix.result(riding))
    dqkv = [dqkv0, dqkv1, dqkv2]

    d_qkv = [_tn_matmul_residue(dqkv[g], h[g], dil, f"grad_w_in_qkv{g}") for g, dil in enumerate(DILATIONS)]
    d_rest, _ = _tn_matmul(drest, h[0], "grad_w_in_rest", 1024, 1024)
    d_win = jnp.concatenate([d_qkv[g][s * GROUP_W:(s + 1) * GROUP_W] for s in range(3) for g in range(N_GROUPS)]
                            + [d_rest], axis=0)
    first = _GradReduction({"w_in": d_win.reshape(N_CHIPS, IN_W // N_CHIPS, D_MODEL)}, place, "w_in")
    tiles = seq // IN_PROJ_BWD_TM
    so_far = (lax.empty((seq, D_MODEL), F32), jnp.zeros((1, D_MODEL), F32))
    in_bwd = lambda so_far, span, rider: _in_proj_bwd(dqkv, drest, w_in, x, dx1, g0, so_far, span, rider=rider)
    so_far, riding = in_bwd(so_far, (0, 3 * tiles // 8), first.pair_exchange())
    (grad_x, dg0), riding = in_bwd(so_far, (3 * tiles // 8, 5 * tiles // 8), first.chip_exchange(riding))
    reduced.update(first.result(_run_exchange(first.pair_share(riding), "w_in_pair_share")))
    little = {"norm_pre_mix": dg0, "w_spatial": d_wsp, "b_spatial": d_bb[:, :, 0], "ln_v_gain": d_lg, "ln_v_bias": d_lb,
              "norm_post_mix": dg1, "norm_pre_mlp": dg2, "norm_post_mlp": dg3}
    return loss8, grad_x, reduced, little


def kernel(x, norm_pre_mix, w_in, w_spatial, b_spatial, ln_v_gain, ln_v_bias, w_branch_attn, w_branch_gmlp, w_out, norm_post_mix, norm_pre_mlp, w_mlp_in, w_mlp_out, norm_post_mlp, loss_target, m_norm_pre_mix, m_w_in, m_w_spatial, m_b_spatial, m_ln_v_gain, m_ln_v_bias, m_w_branch_attn, m_w_branch_gmlp, m_w_out, m_norm_post_mix, m_norm_pre_mlp, m_w_mlp_in, m_w_mlp_out, m_norm_post_mlp, v_norm_pre_mix, v_w_in, v_w_spatial, v_b_spatial, v_ln_v_gain, v_ln_v_bias, v_w_branch_attn, v_w_branch_gmlp, v_w_out, v_norm_post_mix, v_norm_pre_mlp, v_w_mlp_in, v_w_mlp_out, v_norm_post_mlp):
    given = dict(norm_pre_mix=norm_pre_mix, w_in=w_in, w_spatial=w_spatial, b_spatial=b_spatial, ln_v_gain=ln_v_gain,
                 ln_v_bias=ln_v_bias, w_branch_attn=w_branch_attn, w_branch_gmlp=w_branch_gmlp, w_out=w_out,
                 norm_post_mix=norm_post_mix, norm_pre_mlp=norm_pre_mlp, w_mlp_in=w_mlp_in, w_mlp_out=w_mlp_out,
                 norm_post_mlp=norm_post_mlp)
    moments_m = dict(norm_pre_mix=m_norm_pre_mix, w_in=m_w_in, w_spatial=m_w_spatial, b_spatial=m_b_spatial,
                     ln_v_gain=m_ln_v_gain, ln_v_bias=m_ln_v_bias, w_branch_attn=m_w_branch_attn,
                     w_branch_gmlp=m_w_branch_gmlp, w_out=m_w_out, norm_post_mix=m_norm_post_mix,
                     norm_pre_mlp=m_norm_pre_mlp, w_mlp_in=m_w_mlp_in, w_mlp_out=m_w_mlp_out, norm_post_mlp=m_norm_post_mlp)
    moments_v = dict(norm_pre_mix=v_norm_pre_mix, w_in=v_w_in, w_spatial=v_w_spatial, b_spatial=v_b_spatial,
                     ln_v_gain=v_ln_v_gain, ln_v_bias=v_ln_v_bias, w_branch_attn=v_w_branch_attn,
                     w_branch_gmlp=v_w_branch_gmlp, w_out=v_w_out, norm_post_mix=v_norm_post_mix,
                     norm_pre_mlp=v_norm_pre_mlp, w_mlp_in=v_w_mlp_in, w_mlp_out=v_w_mlp_out, norm_post_mlp=v_norm_post_mlp)
    cx, cy, cc = lax.axis_index("x"), lax.axis_index("y"), lax.axis_index("c")

    shards = [given[n][0].T if n == "w_in" else given[n][0] for n in BIG]
    small = {n: given[n][0] if given[n].ndim > 2 else given[n] for n in SMALL}
    place = jnp.stack([cc, 2 * cx + cy]).astype(jnp.int32)
    loss8, grad_x, grad_shard, grads = _device_step(x[0], loss_target[0], small, shards, place)

    packed = jnp.concatenate([_rows8(grads[n]) for n in SMALL] + [loss8], axis=0)
    summed = _all_reduce_small(packed)
    loss = summed[packed.shape[0] - loss8.shape[0], 0]
    row = 0
    for n in SMALL:
        shape = given[n][0].shape
        cnt = -(-(given[n][0].size // 128) // 8) * 8
        grad_shard[n] = summed[row:row + given[n][0].size // 128].reshape(shape)
        row += cnt

    grad_out, deltas, new_m, new_v = {}, {}, {}, {}
    for n in ORDER:
        shape = given[n].shape
        if n == "w_in":
            outs = _adamw(given[n][0].T, grad_shard[n], moments_m[n][0].T, moments_v[n][0].T, "adamw_" + n)
            outs = [o.T for o in outs]
        else:
            two_d = (-1, shape[-1])
            outs = _adamw(given[n].reshape(two_d), grad_shard[n].reshape(two_d), moments_m[n].reshape(two_d),
                          moments_v[n].reshape(two_d), "adamw_" + n)
        grad_out[n], deltas[n], new_m[n], new_v[n] = [o.reshape(shape) for o in outs]
    return (loss, grad_x[None], *[grad_out[n] for n in ORDER], *[deltas[n] for n in ORDER], *[new_m[n] for n in ORDER],
            *[new_v[n] for n in ORDER])
```

```python
import math

import jax
import jax.numpy as jnp
from jax import lax
from jax.experimental import pallas as pl
from jax.experimental.pallas import tpu as pltpu

F32 = jnp.float32
BF16 = jnp.bfloat16
MESH = pl.DeviceIdType.MESH

D_MODEL = 1024
HEAD_DIM = 64
HEADS_PER_GROUP = 4
GROUP_W = HEADS_PER_GROUP * HEAD_DIM
DILATIONS = (1, 4, 16)
N_GROUPS = len(DILATIONS)
ATTN_W = N_GROUPS * GROUP_W
QKV_W = 3 * ATTN_W
GMLP_W = 512
GMLP_GROUPS = 4
CHUNK = 128
REST_W = 2 * GMLP_W + 2 * D_MODEL
IN_W = QKV_W + REST_W
D_FF = 4096
QBLK = 128
ROPE_THETA = 10000.0
EPS = 1e-6
NEG = -1e30
SCALE = HEAD_DIM ** -0.5
N_CHIPS = 4

ADAM_LR = 0.001
ADAM_B1 = 0.9
ADAM_B2 = 0.999
ADAM_EPS = 1e-08
ADAM_WD = 0.01
ADAM_STEP = 10

MIB = 1024 * 1024
HBM_SPEC = pl.BlockSpec(memory_space=pltpu.HBM)
VMEM_SPEC = pl.BlockSpec(memory_space=pltpu.VMEM)


MLP_FWD_TM = 512
MLP_TM = 512


CALL_VMEM_MIB = 56
SMALL_VMEM_MIB = 32


def _params(semantics, vmem_mib, small=False):
    assert vmem_mib <= CALL_VMEM_MIB
    return pltpu.CompilerParams(dimension_semantics=semantics,
                                vmem_limit_bytes=(SMALL_VMEM_MIB if small else CALL_VMEM_MIB) * MIB)


def _in_hbm(a):
    return pltpu.with_memory_space_constraint(a, pltpu.HBM) if a.size * a.dtype.itemsize >= MIB else a


def _pallas(body, **kwargs):
    return pl.pallas_call(body, **kwargs)


def _resident(shape):
    return pl.BlockSpec(shape, lambda *_: (0,) * len(shape), pipeline_mode=pl.Buffered(1))


def _dot(a, b):
    return jnp.dot(a, b, preferred_element_type=F32)


def _dot_nt(a, b):
    return lax.dot_general(a, b, (((1,), (1,)), ((), ())), preferred_element_type=F32)


def _dot_tn(a, b):
    return lax.dot_general(a, b, (((0,), (0,)), ((), ())), preferred_element_type=F32)


_GELU_C = math.sqrt(2.0 / math.pi)


def _gelu(x):
    return x * (0.5 * (1.0 + jnp.tanh(_GELU_C * (x + 0.044715 * (x * x * x)))))


def _gelu_grad(x):
    t = jnp.tanh(_GELU_C * (x + 0.044715 * (x * x * x)))
    return 0.5 * (1.0 + t) + 0.5 * x * (1.0 - t * t) * (_GELU_C * (1.0 + 3.0 * 0.044715 * (x * x)))


def _rsqrt_ms(v):
    return lax.rsqrt(jnp.mean(v * v, axis=-1, keepdims=True) + EPS)


def _rmsnorm_bwd(dn, src, gain):
    r = _rsqrt_ms(src)
    t = gain * dn
    dgain = jnp.sum(dn * (src * r), axis=0, keepdims=True)
    dsrc = r * t - src * ((r * r * r) * jnp.mean(t * src, axis=-1, keepdims=True))
    return dsrc, dgain


def _rot_half(v):
    w = v.shape[-1]
    lane = lax.broadcasted_iota(jnp.int32, v.shape, v.ndim - 1)
    return jnp.where((lane % HEAD_DIM) < HEAD_DIM // 2, pltpu.roll(v, w - HEAD_DIM // 2, v.ndim - 1),
                     pltpu.roll(v, HEAD_DIM // 2, v.ndim - 1))


def _head_masks(shape):
    lane = lax.broadcasted_iota(jnp.int32, shape, 1)
    return [(lane >= h * HEAD_DIM) & (lane < (h + 1) * HEAD_DIM) for h in range(HEADS_PER_GROUP)]


def _head_stack(block, hmask):
    zero = jnp.zeros((), block.dtype)
    return jnp.concatenate([jnp.where(hm, block, zero) for hm in hmask], axis=0)


LANES = 128


def _put_residue(slab, val, out_ref, dil, width, col0):
    tm, w = val.shape
    if dil == 1:
        out_ref[:, col0:col0 + w] = val.astype(out_ref.dtype)
        return
    for k in range(w // LANES):
        slab[k] = val[:, k * LANES:(k + 1) * LANES]
    for r in range(dil):
        for k in range(w // LANES):
            c = r * width + col0 + k * LANES
            out_ref[:, c:c + LANES] = slab[k, pl.ds(r, tm // dil, stride=dil), :].astype(out_ref.dtype)


def _get_tokens(slab, in_ref, dil, width, col0, w):
    if dil == 1:
        return in_ref[:, col0:col0 + w].astype(F32)
    rows = in_ref.shape[0]
    for r in range(dil):
        for k in range(w // LANES):
            c = r * width + col0 + k * LANES
            slab[k, pl.ds(r, rows, stride=dil), :] = in_ref[:, c:c + LANES].astype(F32)
    return jnp.concatenate([slab[k] for k in range(w // LANES)], axis=1)


def _prepare(x, g0, rider=None):
    seq = x.shape[0]
    half = HEAD_DIM // 2
    inv_freq = ROPE_THETA ** (-jnp.arange(half, dtype=F32) / half)
    freq = jnp.tile(inv_freq, LANES // half).reshape(1, LANES)
    tm = 256

    def body(x_ref, g_ref, f_ref, *refs):
        h_refs, tabs, slab = refs[:N_GROUPS], refs[N_GROUPS:3 * N_GROUPS], refs[-1]
        xv = x_ref[...]
        hf = (xv * _rsqrt_ms(xv)) * g_ref[...]
        for g, dil in enumerate(DILATIONS):
            _put_residue(slab, hf, h_refs[g], dil, D_MODEL, 0)
        row = lax.broadcasted_iota(jnp.int32, (tm, LANES), 0) + pl.program_id(0) * tm
        lane = lax.broadcasted_iota(jnp.int32, (tm, LANES), 1)
        ang = row.astype(F32) * f_ref[...]
        cos = jnp.cos(ang)
        sin = jnp.where((lane % HEAD_DIM) < half, -jnp.sin(ang), jnp.sin(ang))
        for i, dil in enumerate(DILATIONS):
            for tab, val in ((tabs[2 * i], cos), (tabs[2 * i + 1], sin)):
                slab[0] = val
                for r in range(dil):
                    piece = slab[0, pl.ds(r, tm // dil, stride=dil), :] if dil > 1 else val
                    for k in range(GROUP_W // LANES):
                        tab[:, r * GROUP_W + k * LANES:r * GROUP_W + (k + 1) * LANES] = piece

    outs, riding = _call(
        body, name="prepare", grid=(seq // tm,),
        in_specs=[pl.BlockSpec((tm, D_MODEL), lambda i: (i, 0)), pl.BlockSpec((1, D_MODEL), lambda i: (0, 0)),
                  pl.BlockSpec((1, LANES), lambda i: (0, 0))],
        out_specs=[pl.BlockSpec((tm // d, d * D_MODEL), lambda i: (i, 0)) for d in DILATIONS]
        + [pl.BlockSpec((tm // d, d * GROUP_W), lambda i: (i, 0)) for d in DILATIONS for _ in range(2)],
        out_shape=[jax.ShapeDtypeStruct((seq // d, d * D_MODEL), BF16) for d in DILATIONS]
        + [jax.ShapeDtypeStruct((seq // d, d * GROUP_W), F32) for d in DILATIONS for _ in range(2)],
        scratch_shapes=[pltpu.VMEM((D_MODEL // LANES, tm, LANES), F32)],
        params=_params(("arbitrary",), 32), args=(x, g0, freq), rider=rider)
    tabs = outs[N_GROUPS:]
    return outs[:N_GROUPS], {d: (tabs[2 * i], tabs[2 * i + 1]) for i, d in enumerate(DILATIONS)}, riding


def _in_proj(h, w_in, cos_t, sin_t, rider=None):
    seq = h.shape[0]
    tm, tn = 512, GROUP_W
    n_qk = 2 * ATTN_W // tn
    n_qkv = QKV_W // tn

    def body(h_ref, w_ref, cos_ref, sin_ref, *refs):
        qkv_refs, rest_ref, slab = refs[:N_GROUPS], refs[N_GROUPS], refs[-1]
        hb = h_ref[...]
        cos, sin = cos_ref[...], sin_ref[...]
        for j in range(IN_W // tn):
            p = _dot_nt(hb, w_ref[j * tn:(j + 1) * tn, :])
            if j < n_qkv:
                if j < n_qk:
                    p = p * cos + _rot_half(p) * sin
                section, g = divmod(j, N_GROUPS)
                _put_residue(slab, p, qkv_refs[g], DILATIONS[g], 3 * GROUP_W, section * GROUP_W)
            else:
                rest_ref[:, (j - n_qkv) * tn:(j - n_qkv + 1) * tn] = p.astype(BF16)

    return _call(
        body, name="in_proj", grid=(seq // tm,),
        in_specs=[pl.BlockSpec((tm, D_MODEL), lambda i: (i, 0)),
                  _resident((IN_W, D_MODEL)),
                  pl.BlockSpec((tm, GROUP_W), lambda i: (i, 0)),
                  pl.BlockSpec((tm, GROUP_W), lambda i: (i, 0))],
        out_specs=[pl.BlockSpec((tm // d, d * 3 * GROUP_W), lambda i: (i, 0)) for d in DILATIONS]
        + [pl.BlockSpec((tm, REST_W), lambda i: (i, 0))],
        out_shape=[jax.ShapeDtypeStruct((seq // d, d * 3 * GROUP_W), BF16) for d in DILATIONS]
        + [jax.ShapeDtypeStruct((seq, REST_W), BF16)],
        scratch_shapes=[pltpu.VMEM((GROUP_W // LANES, tm, LANES), F32)],
        params=_params(("arbitrary",), 48), args=(h, w_in, cos_t, sin_t), rider=rider)


def _band_masks():
    qi = lax.broadcasted_iota(jnp.int32, (QBLK, QBLK), 0)
    kj = lax.broadcasted_iota(jnp.int32, (QBLK, QBLK), 1)
    return kj <= qi, kj >= qi


def _attn_tile(length):
    return min(1024, length)


def _attn_fwd(qkv, dil, rider=None):
    length = qkv.shape[0]
    tq = _attn_tile(length)
    nsub = tq // QBLK
    nblk = length // tq

    def body(q_ref, k_ref, v_ref, kp_ref, vp_ref, o_ref, l_ref):
        n = pl.program_id(1)
        mask_c, mask_p0 = _band_masks()
        hmask = _head_masks((QBLK, GROUP_W))
        zero = jnp.zeros((), BF16)
        for b in range(nsub):
            rows = slice(b * QBLK, (b + 1) * QBLK)
            q = q_ref[rows, :]
            kc, vc = k_ref[rows, :], v_ref[rows, :]
            if b == 0:
                kp, vp = kp_ref[...], vp_ref[...]
                mask_p = mask_p0 & (n > 0)
            else:
                prow = slice((b - 1) * QBLK, b * QBLK)
                kp, vp = k_ref[prow, :], v_ref[prow, :]
                mask_p = mask_p0
            o_acc = jnp.zeros((QBLK, GROUP_W), F32)
            l_acc = jnp.zeros((QBLK, GROUP_W), F32)
            for h in range(HEADS_PER_GROUP):
                hm = hmask[h]
                sc = jnp.where(mask_c, _dot_nt(q, jnp.where(hm, kc, zero)) * SCALE, NEG)
                sp = jnp.where(mask_p, _dot_nt(q, jnp.where(hm, kp, zero)) * SCALE, NEG)
                m = jnp.maximum(jnp.max(sc, axis=-1, keepdims=True), jnp.max(sp, axis=-1, keepdims=True))
                pc, pp = jnp.exp(sc - m), jnp.exp(sp - m)
                den = jnp.sum(pc, axis=-1, keepdims=True) + jnp.sum(pp, axis=-1, keepdims=True)
                pv = _dot(pc.astype(BF16), jnp.where(hm, vc, zero)) + _dot(pp.astype(BF16), jnp.where(hm, vp, zero))
                o_acc = o_acc + pv / den
                l_acc = l_acc + jnp.where(hm, m + jnp.log(den), 0.0)
            o_ref[rows, :] = o_acc.astype(BF16)
            l_ref[rows, :] = l_acc

    cur = lambda sec: pl.BlockSpec((tq, GROUP_W), lambda r, n: (n, r * 3 + sec))
    prev = lambda sec: pl.BlockSpec((QBLK, GROUP_W), lambda r, n: (jnp.maximum(n * nsub - 1, 0), r * 3 + sec))
    return _call(
        body, name=f"attn_fwd_d{dil}", grid=(dil, nblk),
        in_specs=[cur(0), cur(1), cur(2), prev(1), prev(2)],
        out_specs=[pl.BlockSpec((tq, GROUP_W), lambda r, n: (n, r))] * 2,
        out_shape=[jax.ShapeDtypeStruct((length, dil * GROUP_W), BF16),
                   jax.ShapeDtypeStruct((length, dil * GROUP_W), F32)], scratch_shapes=[],
        params=_params(("arbitrary", "arbitrary"), 32), args=(qkv, qkv, qkv, qkv, qkv), rider=rider)


def _attn_bwd(qkv, dy, y, lse, cos_t, sin_t, dil, rider=None):
    length = qkv.shape[0]
    tq = _attn_tile(length)
    nsub = tq // QBLK
    nblk = length // tq

    def body(q_ref, k_ref, v_ref, kp_ref, vp_ref, qn_ref, dy_ref, y_ref, l_ref, dyn_ref, yn_ref, ln_ref,
             cos_ref, sin_ref, out_ref, dq_s, dk_s, dv_s):
        n = pl.program_id(1)
        mask_c, mask_p0 = _band_masks()
        hmask = _head_masks((QBLK, GROUP_W))
        sub = lambda ref, b: ref[b * QBLK:(b + 1) * QBLK, :]
        kbd = [_head_stack(kp_ref[...], hmask)] + [_head_stack(sub(k_ref, b), hmask) for b in range(nsub)]
        vbd = [_head_stack(vp_ref[...], hmask)] + [_head_stack(sub(v_ref, b), hmask) for b in range(nsub)]
        dq_s[...] = jnp.zeros(dq_s.shape, F32)

        def query_terms(q, dyv, yv, lv):
            prod = dyv * yv
            return dict(
                q=q, dy=dyv.astype(BF16), q_heads=[jnp.where(hm, q, jnp.zeros((), BF16)) for hm in hmask],
                dy_heads=[jnp.where(hm, dyv, 0.0).astype(BF16) for hm in hmask],
                delta=[jnp.sum(jnp.where(hm, prod, 0.0), axis=-1, keepdims=True) for hm in hmask],
                lse=[jnp.max(jnp.where(hm, lv, NEG), axis=-1, keepdims=True) for hm in hmask])

        queries = [query_terms(sub(q_ref, b), sub(dy_ref, b), sub(y_ref, b), sub(l_ref, b)) for b in range(nsub)]
        queries.append(query_terms(qn_ref[...], dyn_ref[...], yn_ref[...], ln_ref[...]))
        rows_of = lambda items: items[0] if len(items) == 1 else jnp.concatenate(items, axis=0)
        for kb in range(nsub + 1):
            seen = [(kb - 1, mask_c)] if kb >= 1 else []
            if kb == 0:
                seen.append((0, mask_p0 & (n > 0)))
            elif kb < nsub:
                seen.append((kb, mask_p0))
            else:
                seen.append((nsub, mask_p0 & (n < nblk - 1)))
            qs = [queries[b] for b, _ in seen]
            mask = rows_of([m for _, m in seen])
            s = _dot_nt(rows_of([t["q"] for t in qs]), kbd[kb]) * SCALE
            dp = _dot_nt(rows_of([t["dy"] for t in qs]), vbd[kb])
            ps, dss = [], []
            for h in range(HEADS_PER_GROUP):
                cols = slice(h * QBLK, (h + 1) * QBLK)
                p = jnp.exp(jnp.where(mask, s[:, cols] - rows_of([t["lse"][h] for t in qs]), NEG))
                ps.append(p.astype(BF16))
                dss.append((p * (dp[:, cols] - rows_of([t["delta"][h] for t in qs]))).astype(BF16))
            dq = _dot(jnp.concatenate(dss, axis=1), kbd[kb]) * SCALE
            for i, (b, _) in enumerate(seen):
                if b < nsub:
                    dq_s[b * QBLK:(b + 1) * QBLK, :] += dq[i * QBLK:(i + 1) * QBLK, :]
            if kb >= 1:
                krows = slice((kb - 1) * QBLK, kb * QBLK)
                head_rows = lambda key: jnp.concatenate([t[key][h] for h in range(HEADS_PER_GROUP) for t in qs], axis=0)
                dv_s[krows, :] = _dot_tn(jnp.concatenate(ps, axis=0), head_rows("dy_heads"))
                dk_s[krows, :] = _dot_tn(jnp.concatenate(dss, axis=0), head_rows("q_heads")) * SCALE
        cos, sin = cos_ref[...], sin_ref[...]
        dq, dk = dq_s[...], dk_s[...]
        out_ref[:, 0:GROUP_W] = (dq * cos - _rot_half(dq) * sin).astype(BF16)
        out_ref[:, GROUP_W:2 * GROUP_W] = (dk * cos - _rot_half(dk) * sin).astype(BF16)
        out_ref[:, 2 * GROUP_W:3 * GROUP_W] = dv_s[...].astype(BF16)

    cur = lambda sec: pl.BlockSpec((tq, GROUP_W), lambda r, n: (n, r * 3 + sec))
    prev = lambda sec: pl.BlockSpec((QBLK, GROUP_W), lambda r, n: (jnp.maximum(n * nsub - 1, 0), r * 3 + sec))
    nxt_q = pl.BlockSpec((QBLK, GROUP_W), lambda r, n: (jnp.minimum((n + 1) * nsub, nblk * nsub - 1), r * 3))
    tok = pl.BlockSpec((tq, GROUP_W), lambda r, n: (n, r))
    tok_next = pl.BlockSpec((QBLK, GROUP_W), lambda r, n: (jnp.minimum((n + 1) * nsub, nblk * nsub - 1), r))
    (out,), riding = _call(
        body, name=f"attn_bwd_d{dil}", grid=(dil, nblk),
        in_specs=[cur(0), cur(1), cur(2), prev(1), prev(2), nxt_q,
                  tok, tok, tok, tok_next, tok_next, tok_next, tok, tok],
        out_specs=[pl.BlockSpec((tq, 3 * GROUP_W), lambda r, n: (n, r))],
        out_shape=[jax.ShapeDtypeStruct((length, dil * 3 * GROUP_W), BF16)],
        scratch_shapes=[pltpu.VMEM((tq, GROUP_W), F32)] * 3,
        params=_params(("arbitrary", "arbitrary"), 32),
        args=(qkv, qkv, qkv, qkv, qkv, qkv, dy, y, lse, dy, y, lse, cos_t, sin_t), rider=rider)
    return out, riding


def _layernorm_stats(z):
    mu = jnp.mean(z, axis=-1, keepdims=True)
    zc = z - mu
    rstd = lax.rsqrt(jnp.mean(zc * zc, axis=-1, keepdims=True) + EPS)
    return zc * rstd, rstd


def _tril_mask():
    row = lax.broadcasted_iota(jnp.int32, (CHUNK, CHUNK), 0)
    col = lax.broadcasted_iota(jnp.int32, (CHUNK, CHUNK), 1)
    return col <= row


def _mix_fwd(o_l, rest, x, w_sp, b_col, ln_g, ln_b, w_ba, w_bg, w_out, g1, rider=None):
    seq = x.shape[0]
    tm = 256

    def body(o0, l0, o1, l1, o2, l2, up_ref, zp_ref, gap_ref, gbp_ref, x_ref, wsp_ref, bcol_ref, lg_ref, lb_ref,
             wba_ref, wbg_ref, wout_ref, g1_ref, ya0, lj0, ya1, lj1, ya2, lj2, yg_ref, mg_ref, y_ref, x1_ref, slab):
        outs = [_get_tokens(slab, o, d, GROUP_W, 0, GROUP_W) for o, d in zip((o0, o1, o2), DILATIONS)]
        lses = [_get_tokens(slab, l, d, GROUP_W, 0, GROUP_W) for l, d in zip((l0, l1, l2), DILATIONS)]
        m = jnp.maximum(jnp.maximum(lses[0], lses[1]), lses[2])
        es = [jnp.exp(l - m) for l in lses]
        tot = es[0] + es[1] + es[2]
        ya = (es[0] * outs[0] + es[1] * outs[1] + es[2] * outs[2]) / tot
        lj = m + jnp.log(tot)
        for ya_ref, lj_ref, d in zip((ya0, ya1, ya2), (lj0, lj1, lj2), DILATIONS):
            _put_residue(slab, ya, ya_ref, d, GROUP_W, 0)
            _put_residue(slab, lj, lj_ref, d, GROUP_W, 0)
        zhat, _ = _layernorm_stats(_gelu(zp_ref[...].astype(F32)))
        zln = (zhat * lg_ref[...] + lb_ref[...]).astype(BF16)
        u = _gelu(up_ref[...].astype(F32))
        tril = _tril_mask()
        for g in range(GMLP_GROUPS):
            wm = jnp.where(tril, wsp_ref[g], 0.0).astype(BF16)
            cols = slice(g * CHUNK, (g + 1) * CHUNK)
            for c in range(tm // CHUNK):
                rows = slice(c * CHUNK, (c + 1) * CHUNK)
                sz = _dot(wm, zln[rows, cols]) + bcol_ref[g]
                yg_ref[rows, cols] = (u[rows, cols] * sz).astype(BF16)
        a = _dot(ya.astype(BF16), wba_ref[...])
        bm = _dot(yg_ref[...], wbg_ref[...])
        merged = (jax.nn.sigmoid(gap_ref[...].astype(F32)) * a + jax.nn.sigmoid(gbp_ref[...].astype(F32)) * bm).astype(BF16)
        mg_ref[...] = merged
        yv = _dot(merged, wout_ref[...])
        y_ref[...] = yv.astype(BF16)
        x1_ref[...] = x_ref[...] + (yv * _rsqrt_ms(yv)) * g1_ref[...]

    tok = lambda w: pl.BlockSpec((tm, w), lambda i: (i, 0))
    res = lambda d: pl.BlockSpec((tm // d, d * GROUP_W), lambda i: (i, 0))
    full = lambda *s: pl.BlockSpec(s, lambda i: (0,) * len(s))
    res_specs = [res(d) for d in DILATIONS for _ in range(2)]
    return _call(
        body, name="mix_fwd", grid=(seq // tm,),
        in_specs=res_specs + [
            pl.BlockSpec((tm, GMLP_W), lambda i: (i, 0)), pl.BlockSpec((tm, GMLP_W), lambda i: (i, 1)),
            pl.BlockSpec((tm, D_MODEL), lambda i: (i, 1)), pl.BlockSpec((tm, D_MODEL), lambda i: (i, 2)),
            tok(D_MODEL), full(GMLP_GROUPS, CHUNK, CHUNK), full(GMLP_GROUPS, CHUNK, 1), full(1, GMLP_W), full(1, GMLP_W),
            full(GROUP_W, D_MODEL), full(GMLP_W, D_MODEL), full(D_MODEL, D_MODEL), full(1, D_MODEL)],
        out_specs=res_specs + [tok(GMLP_W), tok(D_MODEL), tok(D_MODEL), tok(D_MODEL)],
        out_shape=[jax.ShapeDtypeStruct((seq // d, d * GROUP_W), F32) for d in DILATIONS for _ in range(2)]
        + [jax.ShapeDtypeStruct((seq, GMLP_W), BF16), jax.ShapeDtypeStruct((seq, D_MODEL), BF16),
           jax.ShapeDtypeStruct((seq, D_MODEL), BF16), jax.ShapeDtypeStruct((seq, D_MODEL), F32)],
        scratch_shapes=[pltpu.VMEM((GROUP_W // LANES, tm, LANES), F32)],
        params=_params(("arbitrary",), 48),
        args=(*o_l, rest, rest, rest, rest, x, w_sp, b_col, ln_g, ln_b, w_ba, w_bg, w_out, g1), rider=rider)


def _mlp_fwd(x1, g2, g3, w_mi, w_mo, target):
    seq = x1.shape[0]
    tm, tf = MLP_FWD_TM, 512

    def body(x1_ref, g2_ref, g3_ref, wmi_ref, wmo_ref, t_ref, h2_ref, a_ref, dy2_ref, dout_ref, loss_ref, dg3_ref, sq_s):
        @pl.when(pl.program_id(0) == 0)
        def _():
            loss_ref[...] = jnp.zeros(loss_ref.shape, F32)
            dg3_ref[...] = jnp.zeros(dg3_ref.shape, F32)

        xv = x1_ref[...]
        hb = ((xv * _rsqrt_ms(xv)) * g2_ref[...]).astype(BF16)
        h2_ref[...] = hb
        for j in range(D_FF // tf):
            cols = slice(j * tf, (j + 1) * tf)
            a = jnp.maximum(_dot(hb, wmi_ref[:, cols]), 0.0)
            a_ref[:, cols] = a.astype(BF16)
            sq_s[:, cols] = (a * a).astype(BF16)
        y2 = _dot(sq_s[...], wmo_ref[...])
        r3 = _rsqrt_ms(y2)
        out = xv + (y2 * r3) * g3_ref[...]
        diff = out - t_ref[...]
        tile_loss = 0.5 * jnp.sum(jnp.mean(diff * diff, axis=-1, keepdims=True), axis=0, keepdims=True)
        loss_ref[...] += jnp.broadcast_to(tile_loss, loss_ref.shape)
        dout = diff * (1.0 / D_MODEL)
        dout_ref[...] = dout
        dy2, dg3 = _rmsnorm_bwd(dout, y2, g3_ref[...])
        dy2_ref[...] = dy2.astype(BF16)
        dg3_ref[...] += dg3

    tok = lambda w: pl.BlockSpec((tm, w), lambda i: (i, 0))
    vec = pl.BlockSpec((1, D_MODEL), lambda i: (0, 0))
    return _pallas(
        body, name="mlp_fwd", grid=(seq // tm,),
        in_specs=[tok(D_MODEL), vec, vec, _resident((D_MODEL, D_FF)), _resident((D_FF, D_MODEL)), tok(D_MODEL)],
        out_specs=[tok(D_MODEL), tok(D_FF), tok(D_MODEL), tok(D_MODEL), pl.BlockSpec((8, 128), lambda i: (0, 0)), vec],
        out_shape=[jax.ShapeDtypeStruct((seq, D_MODEL), BF16), jax.ShapeDtypeStruct((seq, D_FF), BF16),
                   jax.ShapeDtypeStruct((seq, D_MODEL), BF16), jax.ShapeDtypeStruct((seq, D_MODEL), F32),
                   jax.ShapeDtypeStruct((8, 128), F32), jax.ShapeDtypeStruct((1, D_MODEL), F32)],
        scratch_shapes=[pltpu.VMEM((tm, D_FF), BF16)],
        compiler_params=_params(("arbitrary",), 56),
    )(*map(_in_hbm, (x1, g2, g3, w_mi, w_mo, target)))


def _mlp_bwd(dy2, a, w_mo, w_mi, dout, x1, y, g2, g1, rider=None):
    seq = x1.shape[0]
    tm, tf = MLP_TM, 512

    def body(dy2_ref, a_ref, wmo_ref, wmi_ref, dout_ref, x1_ref, y_ref, g2_ref, g1_ref,
             dap_ref, dx1_ref, dy_ref, dg2_ref, dg1_ref):
        @pl.when(pl.program_id(0) == 0)
        def _():
            dg2_ref[...] = jnp.zeros(dg2_ref.shape, F32)
            dg1_ref[...] = jnp.zeros(dg1_ref.shape, F32)

        dy2v = dy2_ref[...]
        for j in range(D_FF // tf):
            cols = slice(j * tf, (j + 1) * tf)
            da2 = _dot_nt(dy2v, wmo_ref[cols, :])
            dap_ref[:, cols] = (da2 * (2.0 * a_ref[:, cols].astype(F32))).astype(BF16)
        dh2 = _dot_nt(dap_ref[...], wmi_ref[...])
        dres, dg2 = _rmsnorm_bwd(dh2, x1_ref[...], g2_ref[...])
        dx1 = dout_ref[...] + dres
        dx1_ref[...] = dx1
        dg2_ref[...] += dg2
        dyv, dg1 = _rmsnorm_bwd(dx1, y_ref[...].astype(F32), g1_ref[...])
        dy_ref[...] = dyv.astype(BF16)
        dg1_ref[...] += dg1

    tok = lambda w: pl.BlockSpec((tm, w), lambda i: (i, 0))
    vec = pl.BlockSpec((1, D_MODEL), lambda i: (0, 0))
    return _call(
        body, name="mlp_bwd", grid=(seq // tm,),
        in_specs=[tok(D_MODEL), tok(D_FF), _resident((D_FF, D_MODEL)), _resident((D_MODEL, D_FF)),
                  tok(D_MODEL), tok(D_MODEL), tok(D_MODEL), vec, vec],
        out_specs=[tok(D_FF), tok(D_MODEL), tok(D_MODEL), vec, vec],
        out_shape=[jax.ShapeDtypeStruct((seq, D_FF), BF16), jax.ShapeDtypeStruct((seq, D_MODEL), F32),
                   jax.ShapeDtypeStruct((seq, D_MODEL), BF16), jax.ShapeDtypeStruct((1, D_MODEL), F32),
                   jax.ShapeDtypeStruct((1, D_MODEL), F32)], scratch_shapes=[],
        params=_params(("arbitrary",), 56), args=(dy2, a, w_mo, w_mi, dout, x1, y, g2, g1), rider=rider)


def _tn_matmul(a, b, name, bm, bn, square_a=False, column_shards=False, rider=None):
    seq, m = a.shape
    n = b.shape[1]
    ts = 2048

    def body(a_ref, b_ref, o_ref):
        @pl.when(pl.program_id(2) == 0)
        def _():
            o_ref[...] = jnp.zeros(o_ref.shape, F32)

        av = a_ref[...]
        if square_a:
            af = av.astype(F32)
            av = (af * af).astype(BF16)
        o_ref[...] += _dot_tn(av, b_ref[...])

    if column_shards:
        out_spec = pl.BlockSpec((None, bm, bn), lambda mi, ni, s: (ni, mi, 0))
        out_shape = jax.ShapeDtypeStruct((n // bn, m, bn), F32)
    else:
        out_spec = pl.BlockSpec((bm, bn), lambda mi, ni, s: (mi, ni))
        out_shape = jax.ShapeDtypeStruct((m, n), F32)
    (out,), riding = _call(
        body, name=name, grid=(m // bm, n // bn, seq // ts),
        in_specs=[pl.BlockSpec((ts, bm), lambda mi, ni, s: (s, mi)), pl.BlockSpec((ts, bn), lambda mi, ni, s: (s, ni))],
        out_specs=[out_spec], out_shape=[out_shape], scratch_shapes=[],
        params=_params(("arbitrary", "arbitrary", "arbitrary"), 40), args=(a, b), rider=rider)
    return out, riding


def _tn_matmul_residue(a, b, dil, name):
    length = a.shape[0]
    m, n = a.shape[1] // dil, b.shape[1] // dil
    ts = min(1024, length)

    def body(a_ref, b_ref, o_ref):
        @pl.when((pl.program_id(0) == 0) & (pl.program_id(1) == 0))
        def _():
            o_ref[...] = jnp.zeros(o_ref.shape, F32)

        o_ref[...] += _dot_tn(a_ref[...], b_ref[...])

    return _pallas(
        body, name=name, grid=(dil, length // ts),
        in_specs=[pl.BlockSpec((ts, m), lambda r, s: (s, r)), pl.BlockSpec((ts, n), lambda r, s: (s, r))],
        out_specs=pl.BlockSpec((m, n), lambda r, s: (0, 0)),
        out_shape=jax.ShapeDtypeStruct((m, n), F32),
        compiler_params=_params(("arbitrary", "arbitrary"), 40),
    )(_in_hbm(a), _in_hbm(b))


def _mix_bwd(dy, ya, yg, mg, rest, w_out, w_ba, w_bg, w_sp, b_col, ln_g, ln_b, rider=None):
    seq = dy.shape[0]
    tm = 256

    def body(dy_ref, ya_ref, yg_ref, mg_ref, up_ref, zp_ref, gap_ref, gbp_ref, wout_ref, wba_ref, wbg_ref,
             wsp_ref, bcol_ref, lg_ref, lb_ref,
             dya0, dya1, dya2, dpr_ref, dwout_ref, dwba_ref, dwbg_ref, dwsp_ref, dbb_ref, dlg_ref, dlb_ref,
             dzln_s, du_s, slab):
        @pl.when(pl.program_id(0) == 0)
        def _():
            for ref in (dwout_ref, dwba_ref, dwbg_ref, dwsp_ref, dbb_ref, dlg_ref, dlb_ref):
                ref[...] = jnp.zeros(ref.shape, F32)

        dyv = dy_ref[...]
        dm = _dot_nt(dyv, wout_ref[...])
        dwout_ref[...] += _dot_tn(mg_ref[...], dyv)
        yab = ya_ref[...].astype(BF16)
        ygb = yg_ref[...]
        a = _dot(yab, wba_ref[...])
        bm = _dot(ygb, wbg_ref[...])
        ga = jax.nn.sigmoid(gap_ref[...].astype(F32))
        gb = jax.nn.sigmoid(gbp_ref[...].astype(F32))
        dpr_ref[:, 2 * GMLP_W:2 * GMLP_W + D_MODEL] = (dm * a * (ga * (1.0 - ga))).astype(BF16)
        dpr_ref[:, 2 * GMLP_W + D_MODEL:REST_W] = (dm * bm * (gb * (1.0 - gb))).astype(BF16)
        da = (dm * ga).astype(BF16)
        db = (dm * gb).astype(BF16)
        dwba = _dot_tn(yab, da)
        dwbg = _dot_tn(ygb, db)
        shard_w = D_MODEL // N_CHIPS
        for j in range(N_CHIPS):
            dwba_ref[j] += dwba[:, j * shard_w:(j + 1) * shard_w]
            dwbg_ref[j] += dwbg[:, j * shard_w:(j + 1) * shard_w]
        dya = _dot_nt(da, wba_ref[...])
        for dya_ref, d in zip((dya0, dya1, dya2), DILATIONS):
            _put_residue(slab, dya, dya_ref, d, GROUP_W, 0)
        dyg = _dot_nt(db, wbg_ref[...])

        zp = zp_ref[...].astype(F32)
        zhat, rstd = _layernorm_stats(_gelu(zp))
        lg = lg_ref[...]
        zln = (zhat * lg + lb_ref[...]).astype(BF16)
        up = up_ref[...].astype(F32)
        u = _gelu(up)
        tril = _tril_mask()
        for g in range(GMLP_GROUPS):
            wm = jnp.where(tril, wsp_ref[g], 0.0).astype(BF16)
            cols = slice(g * CHUNK, (g + 1) * CHUNK)
            for c in range(tm // CHUNK):
                rows = slice(c * CHUNK, (c + 1) * CHUNK)
                zb = zln[rows, cols]
                sz = _dot(wm, zb) + bcol_ref[g]
                dyg_cg = dyg[rows, cols]
                du_s[rows, cols] = dyg_cg * sz
                dsz = dyg_cg * u[rows, cols]
                dszb = dsz.astype(BF16)
                dbb_ref[g] += jnp.broadcast_to(jnp.sum(dsz, axis=-1, keepdims=True), (CHUNK, CHUNK))
                dwsp_ref[g] += jnp.where(tril, _dot_nt(dszb, zb), 0.0)
                dzln_s[rows, cols] = _dot_tn(wm, dszb)
        dzln = dzln_s[...]
        dlg_ref[...] += jnp.sum(dzln * zhat, axis=0, keepdims=True)
        dlb_ref[...] += jnp.sum(dzln, axis=0, keepdims=True)
        dzh = dzln * lg
        dz = rstd * (dzh - jnp.mean(dzh, axis=-1, keepdims=True) - zhat * jnp.mean(dzh * zhat, axis=-1, keepdims=True))
        dpr_ref[:, GMLP_W:2 * GMLP_W] = (dz * _gelu_grad(zp)).astype(BF16)
        dpr_ref[:, 0:GMLP_W] = (du_s[...] * _gelu_grad(up)).astype(BF16)

    tok = lambda w: pl.BlockSpec((tm, w), lambda i: (i, 0))
    full = lambda *s: pl.BlockSpec(s, lambda i: (0,) * len(s))
    return _call(
        body, name="mix_bwd", grid=(seq // tm,),
        in_specs=[tok(D_MODEL), tok(GROUP_W), tok(GMLP_W), tok(D_MODEL),
                  pl.BlockSpec((tm, GMLP_W), lambda i: (i, 0)), pl.BlockSpec((tm, GMLP_W), lambda i: (i, 1)),
                  pl.BlockSpec((tm, D_MODEL), lambda i: (i, 1)), pl.BlockSpec((tm, D_MODEL), lambda i: (i, 2)),
                  full(D_MODEL, D_MODEL), full(GROUP_W, D_MODEL), full(GMLP_W, D_MODEL),
                  full(GMLP_GROUPS, CHUNK, CHUNK), full(GMLP_GROUPS, CHUNK, 1), full(1, GMLP_W), full(1, GMLP_W)],
        out_specs=[pl.BlockSpec((tm // d, d * GROUP_W), lambda i: (i, 0)) for d in DILATIONS]
        + [tok(REST_W), full(D_MODEL, D_MODEL), full(N_CHIPS, GROUP_W, D_MODEL // N_CHIPS),
           full(N_CHIPS, GMLP_W, D_MODEL // N_CHIPS),
           full(GMLP_GROUPS, CHUNK, CHUNK), full(GMLP_GROUPS, CHUNK, CHUNK), full(1, GMLP_W), full(1, GMLP_W)],
        out_shape=[jax.ShapeDtypeStruct((seq // d, d * GROUP_W), F32) for d in DILATIONS]
        + [jax.ShapeDtypeStruct((seq, REST_W), BF16),
           jax.ShapeDtypeStruct((D_MODEL, D_MODEL), F32), jax.ShapeDtypeStruct((N_CHIPS, GROUP_W, D_MODEL // N_CHIPS), F32),
           jax.ShapeDtypeStruct((N_CHIPS, GMLP_W, D_MODEL // N_CHIPS), F32),
           jax.ShapeDtypeStruct((GMLP_GROUPS, CHUNK, CHUNK), F32),
           jax.ShapeDtypeStruct((GMLP_GROUPS, CHUNK, CHUNK), F32), jax.ShapeDtypeStruct((1, GMLP_W), F32),
           jax.ShapeDtypeStruct((1, GMLP_W), F32)],
        scratch_shapes=[pltpu.VMEM((tm, GMLP_W), F32), pltpu.VMEM((tm, GMLP_W), F32),
                        pltpu.VMEM((GROUP_W // LANES, tm, LANES), F32)],
        params=_params(("arbitrary",), 56),
        args=(dy, ya, yg, mg, rest, rest, rest, rest, w_out, w_ba, w_bg, w_sp, b_col, ln_g, ln_b), rider=rider)


IN_PROJ_BWD_TM = 256


def _in_proj_bwd(dqkv, drest, w_in, x, dx1, g0, so_far, span, rider=None):
    seq = x.shape[0]
    tm = IN_PROJ_BWD_TM
    off, steps = span
    gx_so_far, dg_so_far = so_far

    def body(d0, d1, d2, dr_ref, w_ref, x_ref, dx1_ref, g_ref, dg_in_ref, gx_in_ref, gx_ref, dg_ref, slab):
        @pl.when(pl.program_id(0) == 0)
        def _():
            dg_ref[...] = dg_in_ref[...]

        dh = _dot(dr_ref[...], w_ref[QKV_W:, :])
        for g, (d_ref, dil) in enumerate(zip((d0, d1, d2), DILATIONS)):
            piece = d_ref[...] if dil == 1 else _get_tokens(slab, d_ref, dil, 3 * GROUP_W, 0, 3 * GROUP_W).astype(BF16)
            for section, (lo, hi) in enumerate(_qkv_columns(g)):
                dh = dh + _dot(piece[:, section * GROUP_W:(section + 1) * GROUP_W], w_ref[lo:hi, :])
        dres, dg = _rmsnorm_bwd(dh, x_ref[...], g_ref[...])
        gx_ref[...] = dx1_ref[...] + dres
        dg_ref[...] += dg

    tok = lambda w: pl.BlockSpec((tm, w), lambda i: (i + off, 0))
    full = lambda *s: pl.BlockSpec(s, lambda i: (0,) * len(s))
    in_specs = ([pl.BlockSpec((tm // d, d * 3 * GROUP_W), lambda i: (i + off, 0)) for d in DILATIONS] + [tok(REST_W)]
                + [_resident((IN_W, D_MODEL))]
                + [tok(D_MODEL), tok(D_MODEL), full(1, D_MODEL), full(1, D_MODEL), HBM_SPEC])
    return _call(
        body, name=f"in_proj_bwd_{off}", grid=(steps,), in_specs=in_specs,
        out_specs=[tok(D_MODEL), full(1, D_MODEL)],
        out_shape=[jax.ShapeDtypeStruct((seq, D_MODEL), F32), jax.ShapeDtypeStruct((1, D_MODEL), F32)],
        scratch_shapes=[pltpu.VMEM((3 * GROUP_W // LANES, tm, LANES), F32)],
        params=_params(("arbitrary",), 48), args=(*dqkv, drest, w_in, x, dx1, g0, dg_so_far, gx_so_far),
        rider=rider, aliases={len(in_specs) - 1: 0})


def _adamw(w, g, m, v, name):
    rows, cols = w.shape
    tr = _row_tile(rows) if rows % 16 == 0 else rows
    c1 = 1.0 - ADAM_B1 ** ADAM_STEP
    c2 = 1.0 - ADAM_B2 ** ADAM_STEP

    def body(w_ref, g_ref, m_ref, v_ref, go_ref, d_ref, nm_ref, nv_ref):
        gv = g_ref[...]
        go_ref[...] = gv
        nm = ADAM_B1 * m_ref[...] + (1.0 - ADAM_B1) * gv
        nv = ADAM_B2 * v_ref[...] + (1.0 - ADAM_B2) * (gv * gv)
        d_ref[...] = -ADAM_LR * ((nm / c1) / (jnp.sqrt(nv / c2) + ADAM_EPS) + ADAM_WD * w_ref[...])
        nm_ref[...] = nm
        nv_ref[...] = nv

    spec = pl.BlockSpec((tr, cols), lambda i: (i, 0))
    return _pallas(
        body, name=name, grid=(rows // tr,),
        in_specs=[spec] * 4, out_specs=[spec] * 4,
        out_shape=[jax.ShapeDtypeStruct((rows, cols), F32)] * 4,
        compiler_params=_params(("arbitrary",), 32, small=True),
    )(w, g, m, v)


def _place():
    x, y, c = lax.axis_index("x"), lax.axis_index("y"), lax.axis_index("c")
    chips = [(1 - x, y), (x, 1 - y), (1 - x, 1 - y)]
    return x, y, c, chips


class _Exchange:
    def __init__(self, inputs, out_shapes, n_sems, start, finish, aliases=None):
        self.inputs, self.out_shapes, self.n_sems = list(inputs), list(out_shapes), n_sems
        self.start, self.finish, self.aliases = start, finish, dict(aliases or {})

    def scratch(self):
        return [pltpu.SemaphoreType.DMA((self.n_sems,)), pltpu.SemaphoreType.DMA((self.n_sems,))]


def _together(*parts):
    ins = [len(p.inputs) for p in parts]
    outs = [len(p.out_shapes) for p in parts]

    def split(refs, counts):
        pos, pieces = 0, []
        for cnt in counts:
            pieces.append(refs[pos:pos + cnt])
            pos += cnt
        return pieces

    def run(which):
        def go(in_refs, out_refs, *sems):
            for k, (p, i, o) in enumerate(zip(parts, split(in_refs, ins), split(out_refs, outs))):
                getattr(p, which)(i, o, sems[2 * k], sems[2 * k + 1])
        return go

    both = _Exchange([a for p in parts for a in p.inputs], [s for p in parts for s in p.out_shapes], 0, run("start"),
                     run("finish"))
    both.aliases = {sum(ins[:k]) + i: sum(outs[:k]) + o for k, p in enumerate(parts) for i, o in p.aliases.items()}
    both.scratch = lambda: [s for p in parts for s in p.scratch()]
    return both


def _run_exchange(ex, name):
    n_in, n_out = len(ex.inputs), len(ex.out_shapes)

    def body(*refs):
        ins, outs, sems = refs[:n_in], refs[n_in:n_in + n_out], refs[n_in + n_out:]
        ex.start(ins, outs, *sems)
        ex.finish(ins, outs, *sems)

    return _pallas(
        body, name=name, in_specs=[HBM_SPEC] * n_in, out_specs=[HBM_SPEC] * n_out, out_shape=ex.out_shapes,
        scratch_shapes=ex.scratch(), input_output_aliases=ex.aliases,
    )(*ex.inputs)


def _call(body, *, name, grid, in_specs, out_specs, out_shape, scratch_shapes, params, args, rider=None, aliases=None):
    in_specs, out_specs, out_shape, scratch_shapes = list(in_specs), list(out_specs), list(out_shape), list(scratch_shapes)
    aliases = dict(aliases or {})
    args = [_in_hbm(a) for a in args]
    if rider is None:
        outs = _pallas(body, name=name, grid=grid, in_specs=in_specs, out_specs=out_specs, out_shape=out_shape,
                              scratch_shapes=scratch_shapes, input_output_aliases=aliases, compiler_params=params)(*args)
        return list(outs), []
    n_in, n_out, n_scr = len(in_specs), len(out_specs), len(scratch_shapes)
    r_in, r_out = len(rider.inputs), len(rider.out_shapes)

    def wrapped(*refs):
        ins, r_ins = refs[:n_in], refs[n_in:n_in + r_in]
        pos = n_in + r_in
        outs, r_outs = refs[pos:pos + n_out], refs[pos + n_out:pos + n_out + r_out]
        pos += n_out + r_out
        scr, sems = refs[pos:pos + n_scr], refs[pos + n_scr:]
        ids = [pl.program_id(k) for k in range(len(grid))]
        first, last = ids[0] == 0, ids[0] == grid[0] - 1
        for k in range(1, len(grid)):
            first, last = first & (ids[k] == 0), last & (ids[k] == grid[k] - 1)

        @pl.when(first)
        def _():
            rider.start(r_ins, r_outs, *sems)

        body(*ins, *outs, *scr)

        @pl.when(last)
        def _():
            rider.finish(r_ins, r_outs, *sems)

    outs = _pallas(
        wrapped, name=name, grid=grid, in_specs=in_specs + [HBM_SPEC] * r_in, out_specs=out_specs + [HBM_SPEC] * r_out,
        out_shape=out_shape + rider.out_shapes, scratch_shapes=scratch_shapes + rider.scratch(),
        input_output_aliases={**aliases, **{n_in + i: n_out + o for i, o in rider.aliases.items()}}, compiler_params=params,
    )(*args, *rider.inputs)
    return list(outs[:n_out]), list(outs[n_out:])


def _stage_weights(shards):
    n = len(shards)

    def body(*refs):
        ins, outs, stages, sems = refs[:n], refs[n:2 * n], refs[2 * n:3 * n], refs[3 * n]
        x, y, _, _ = _place()
        copies = []
        for t in range(n):
            stages[t][...] = ins[t][...].astype(BF16)
            copies.append(pltpu.make_async_copy(stages[t], outs[t].at[2 * x + y], sems.at[t]))
            copies[-1].start()
        for cp in copies:
            cp.wait()

    assert sum(s.size * 6 for s in shards) <= (CALL_VMEM_MIB - 8) * MIB
    return _pallas(
        body, name="stage_weights", in_specs=[VMEM_SPEC] * n, out_specs=[HBM_SPEC] * n,
        out_shape=[jax.ShapeDtypeStruct((N_CHIPS,) + s.shape, BF16) for s in shards],
        scratch_shapes=[pltpu.VMEM(s.shape, BF16) for s in shards] + [pltpu.SemaphoreType.DMA((n,))],
        compiler_params=pltpu.CompilerParams(vmem_limit_bytes=SMALL_VMEM_MIB * MIB),
    )(*shards)


def _gather(buffers, stage="both", part=(0, 1)):
    n = len(buffers)
    halves = [b.shape[1] // part[1] // 2 for b in buffers]

    def half_of(outs, t, chip, which):
        return outs[t].at[chip, pl.ds((2 * part[0] + which) * halves[t], halves[t]), :]

    def copy(outs, sems, t, k, chip, which, to):
        rows = half_of(outs, t, chip, which)
        return pltpu.make_async_remote_copy(src_ref=rows, dst_ref=rows, send_sem=sems[0].at[6 * t + k],
                                            recv_sem=sems[1].at[6 * t + k], device_id=to, device_id_type=MESH)

    def to_chips(outs, sems, what):
        x, y, c, chips = _place()
        for t in range(n):
            for j, (px, py) in enumerate(chips):
                if what == "start":
                    copy(outs, sems, t, j, 2 * x + y, c, (px, py, c)).start()
                else:
                    copy(outs, sems, t, j, 2 * px + py, c, (px, py, c)).wait_recv()
                    copy(outs, sems, t, j, 2 * x + y, c, (px, py, c)).wait_send()

    def to_sibling(outs, sems, what):
        x, y, c, chips = _place()
        for t in range(n):
            for j, (px, py) in enumerate(chips):
                if what == "start":
                    copy(outs, sems, t, 3 + j, 2 * px + py, c, (x, y, 1 - c)).start()
                else:
                    copy(outs, sems, t, 3 + j, 2 * px + py, 1 - c, (x, y, 1 - c)).wait_recv()
                    copy(outs, sems, t, 3 + j, 2 * px + py, c, (x, y, 1 - c)).wait_send()

    def start(ins, outs, *sems):
        (to_sibling if stage == "pair" else to_chips)(outs, sems, "start")

    def finish(ins, outs, *sems):
        if stage == "both":
            x, y, c, chips = _place()
            for j, (px, py) in enumerate(chips):
                for t in range(n):
                    copy(outs, sems, t, j, 2 * px + py, c, (px, py, c)).wait_recv()
                    copy(outs, sems, t, 3 + j, 2 * px + py, c, (x, y, 1 - c)).start()
            for j, (px, py) in enumerate(chips):
                for t in range(n):
                    copy(outs, sems, t, j, 2 * x + y, c, (px, py, c)).wait_send()
            to_sibling(outs, sems, "finish")
        elif stage == "chips":
            to_chips(outs, sems, "finish")
        else:
            to_sibling(outs, sems, "finish")

    return _Exchange(buffers, [jax.ShapeDtypeStruct(b.shape, b.dtype) for b in buffers], 6 * n, start, finish,
                     aliases={t: t for t in range(n)})


def _pair_exchange(grads):
    n = len(grads)
    halves = [g.shape[1] // 2 for g in grads]

    def copies(ins, outs, send_sems, recv_sems):
        x, y, c, _ = _place()
        return [pltpu.make_async_remote_copy(
            src_ref=ins[t].at[:, pl.ds((1 - c) * halves[t], halves[t]), :], dst_ref=outs[t],
            send_sem=send_sems.at[t], recv_sem=recv_sems.at[t], device_id=(x, y, 1 - c), device_id_type=MESH)
            for t in range(n)]

    def start(*refs):
        for cp in copies(*refs):
            cp.start()

    def finish(*refs):
        for cp in copies(*refs):
            cp.wait()

    return _Exchange(grads, [jax.ShapeDtypeStruct((N_CHIPS, h, g.shape[2]), F32) for g, h in zip(grads, halves)], n,
                     start, finish)


def _row_tile(rows):
    return max(t for t in range(16, 257, 16) if rows % t == 0)


def _pair_add(grad, other, place, name):
    _, rows, cols = grad.shape
    rh = rows // 2
    tr = _row_tile(rh)
    nb = rh // tr

    def body(p_ref, g_ref, a_ref, wire_ref, own_ref):
        s = g_ref[...] + a_ref[...]
        wire_ref[...] = s.astype(BF16)

        @pl.when(pl.program_id(1) == p_ref[1])
        def _():
            own_ref[...] = s

    blk = (None, tr, cols)
    return _pallas(
        body, name=name,
        grid_spec=pltpu.PrefetchScalarGridSpec(
            num_scalar_prefetch=1, grid=(nb, N_CHIPS),
            in_specs=[pl.BlockSpec(blk, lambda i, j, p: (j, p[0] * nb + i, 0)), pl.BlockSpec(blk, lambda i, j, p: (j, i, 0))],
            out_specs=[pl.BlockSpec(blk, lambda i, j, p: (j, i, 0)), pl.BlockSpec((tr, cols), lambda i, j, p: (i, 0))]),
        out_shape=[jax.ShapeDtypeStruct((N_CHIPS, rh, cols), BF16), jax.ShapeDtypeStruct((rh, cols), F32)],
        compiler_params=_params(("arbitrary", "arbitrary"), 32, small=True),
    )(place, grad, other)


def _chip_exchange(wires):
    n = len(wires)

    def copies(ins, outs, send_sems, recv_sems):
        x, y, c, chips = _place()
        return [pltpu.make_async_remote_copy(
            src_ref=ins[t].at[2 * px + py], dst_ref=outs[t].at[j], send_sem=send_sems.at[3 * t + j],
            recv_sem=recv_sems.at[3 * t + j], device_id=(px, py, c), device_id_type=MESH)
            for t in range(n) for j, (px, py) in enumerate(chips)]

    def start(*refs):
        for cp in copies(*refs):
            cp.start()

    def finish(*refs):
        for cp in copies(*refs):
            cp.wait()

    return _Exchange(wires, [jax.ShapeDtypeStruct((3,) + w.shape[1:], BF16) for w in wires], 3 * n, start, finish)


def _chip_add(own, arrived, place, name):
    rh, cols = own.shape
    tr = _row_tile(rh)
    nb = rh // tr

    def body(p_ref, s_ref, b0, b1, b2, o_ref):
        o_ref[...] = ((s_ref[...] + b0[...].astype(F32)) + b1[...].astype(F32)) + b2[...].astype(F32)

    blk = (None, tr, cols)
    return _pallas(
        body, name=name,
        grid_spec=pltpu.PrefetchScalarGridSpec(
            num_scalar_prefetch=1, grid=(nb,),
            in_specs=[pl.BlockSpec((tr, cols), lambda i, p: (i, 0)), pl.BlockSpec(blk, lambda i, p: (0, i, 0)),
                      pl.BlockSpec(blk, lambda i, p: (1, i, 0)), pl.BlockSpec(blk, lambda i, p: (2, i, 0))],
            out_specs=pl.BlockSpec((tr, cols), lambda i, p: (p[0] * nb + i, 0))),
        out_shape=jax.ShapeDtypeStruct((2 * rh, cols), F32),
        compiler_params=_params(("arbitrary",), 32, small=True),
    )(place, own, arrived, arrived, arrived)


def _pair_share(halves):
    n = len(halves)
    rhs = [h.shape[0] // 2 for h in halves]

    def copy(outs, send_sems, recv_sems, t, which):
        x, y, c, _ = _place()
        rows = outs[t].at[pl.ds(which * rhs[t], rhs[t]), :]
        return pltpu.make_async_remote_copy(src_ref=rows, dst_ref=rows, send_sem=send_sems.at[t], recv_sem=recv_sems.at[t],
                                            device_id=(x, y, 1 - c), device_id_type=MESH)

    def start(ins, outs, send_sems, recv_sems):
        c = lax.axis_index("c")
        for t in range(n):
            copy(outs, send_sems, recv_sems, t, c).start()

    def finish(ins, outs, send_sems, recv_sems):
        c = lax.axis_index("c")
        for t in range(n):
            copy(outs, send_sems, recv_sems, t, c).wait_send()
            copy(outs, send_sems, recv_sems, t, 1 - c).wait_recv()

    return _Exchange(halves, [jax.ShapeDtypeStruct(h.shape, F32) for h in halves], n, start, finish,
                     aliases={t: t for t in range(n)})


class _GradReduction:
    def __init__(self, grads, place, tag):
        self.names, self.grads, self.place, self.tag = list(grads), grads, place, tag

    def pair_exchange(self):
        return _pair_exchange([self.grads[n] for n in self.names])

    def chip_exchange(self, others):
        sums = [_pair_add(self.grads[n], o, self.place, f"{self.tag}_pair_add_{n}") for n, o in zip(self.names, others)]
        self.owns = [own for _, own in sums]
        return _chip_exchange([wire for wire, _ in sums])

    def pair_share(self, arrived):
        return _pair_share([_chip_add(own, arr, self.place, f"{self.tag}_chip_add_{n}")
                            for n, own, arr in zip(self.names, self.owns, arrived)])

    def result(self, shared):
        return dict(zip(self.names, shared))


def _all_reduce_small(p):
    rows, lanes = p.shape
    half = rows // 2

    def body(p_ref, o_ref, sib, sums, send_sems, recv_sems):
        x, y, c, chips = _place()
        mine, sibling = 2 * x + y, (x, y, 1 - c)
        swap = pltpu.make_async_remote_copy(src_ref=p_ref, dst_ref=sib, send_sem=send_sems.at[0], recv_sem=recv_sems.at[0],
                                            device_id=sibling, device_id_type=MESH)
        swap.start()
        swap.wait()
        sums[mine] = p_ref[...] + sib[...]

        def copy(k, chip, which, to):
            part = sums.at[chip, pl.ds(which * half, half), :]
            return pltpu.make_async_remote_copy(src_ref=part, dst_ref=part, send_sem=send_sems.at[k], recv_sem=recv_sems.at[k],
                                                device_id=to, device_id_type=MESH)

        for j, (px, py) in enumerate(chips):
            copy(1 + j, mine, c, (px, py, c)).start()
        for j, (px, py) in enumerate(chips):
            copy(1 + j, 2 * px + py, c, (px, py, c)).wait_recv()
            copy(4 + j, 2 * px + py, c, sibling).start()
        for j, (px, py) in enumerate(chips):
            copy(4 + j, 2 * px + py, 1 - c, sibling).wait_recv()
        for j, (px, py) in enumerate(chips):
            copy(1 + j, mine, c, (px, py, c)).wait_send()
            copy(4 + j, 2 * px + py, c, sibling).wait_send()
        o_ref[...] = ((sums[0] + sums[1]) + sums[2]) + sums[3]

    return _pallas(
        body, name="small_all_reduce", in_specs=[VMEM_SPEC], out_specs=VMEM_SPEC,
        out_shape=jax.ShapeDtypeStruct((rows, lanes), F32),
        scratch_shapes=[pltpu.VMEM((rows, lanes), F32), pltpu.VMEM((N_CHIPS, rows, lanes), F32),
                        pltpu.SemaphoreType.DMA((7,)), pltpu.SemaphoreType.DMA((7,))],
        compiler_params=pltpu.CompilerParams(vmem_limit_bytes=SMALL_VMEM_MIB * MIB),
    )(p)


BIG = ("w_in", "w_branch_attn", "w_branch_gmlp", "w_out", "w_mlp_in", "w_mlp_out")
COLUMN_SHARDED = ("w_branch_attn", "w_branch_gmlp", "w_mlp_in")
SMALL = ("norm_pre_mix", "w_spatial", "b_spatial", "ln_v_gain", "ln_v_bias", "norm_post_mix", "norm_pre_mlp", "norm_post_mlp")
ORDER = ("norm_pre_mix", "w_in", "w_spatial", "b_spatial", "ln_v_gain", "ln_v_bias", "w_branch_attn", "w_branch_gmlp",
         "w_out", "norm_post_mix", "norm_pre_mlp", "w_mlp_in", "w_mlp_out", "norm_post_mlp")


def _full_weight(name, gathered):
    if name in COLUMN_SHARDED:
        return jnp.transpose(gathered, (1, 0, 2)).reshape(gathered.shape[1], -1)
    return gathered.reshape(-1, gathered.shape[2])


def _rows8(a):
    a = a.reshape(-1, 128)
    pad = (-a.shape[0]) % 8
    return jnp.pad(a, ((0, pad), (0, 0))) if pad else a


def _qkv_columns(group):
    return [(sec * ATTN_W + group * GROUP_W, sec * ATTN_W + (group + 1) * GROUP_W) for sec in range(3)]


def _device_step(x, target, small, shards, place):
    seq = x.shape[0]
    g0, g1, g2, g3 = small["norm_pre_mix"], small["norm_post_mix"], small["norm_pre_mlp"], small["norm_post_mlp"]
    w_sp = small["w_spatial"]
    b_col = small["b_spatial"].reshape(GMLP_GROUPS, CHUNK, 1)
    ln_g, ln_b = small["ln_v_gain"], small["ln_v_bias"]

    staged = _stage_weights(shards)
    h, tables, (w_in,) = _prepare(x, g0, rider=_gather(staged[:1]))
    w_in = _full_weight("w_in", w_in)
    (*qkv, rest), landed = _in_proj(h[0], w_in, *tables[1], rider=_gather(staged[1:], "chips"))

    o_l, gathered = _attn_fwd(qkv[0], DILATIONS[0], rider=_gather(landed, "pair"))
    full = {n: _full_weight(n, gw) for n, gw in zip(BIG[1:], gathered)}
    for g in range(1, N_GROUPS):
        o_l.extend(_attn_fwd(qkv[g], DILATIONS[g])[0])
    (*ya_l, yg, mg, y, x1), _ = _mix_fwd(o_l, rest, x, w_sp, b_col, ln_g, ln_b, full["w_branch_attn"],
                                        full["w_branch_gmlp"], full["w_out"], g1)
    ya, lse = ya_l[0::2], ya_l[1::2]
    h2, a, dy2, dout, loss8, dg3 = _mlp_fwd(x1, g2, g3, full["w_mlp_in"], full["w_mlp_out"], target)
    d_wmo, _ = _tn_matmul(a, dy2, "grad_w_mlp_out", 1024, 1024, square_a=True)
    mlp_out = _GradReduction({"w_mlp_out": d_wmo.reshape(N_CHIPS, D_FF // N_CHIPS, D_MODEL)}, place, "mlp_out")
    (dap, dx1, dy, dg2, dg1), riding = _mlp_bwd(dy2, a, full["w_mlp_out"], full["w_mlp_in"], dout, x1, y, g2, g1,
                                                 rider=mlp_out.pair_exchange())
    d_wmi, riding = _tn_matmul(h2, dap, "grad_w_mlp_in", 1024, 1024, column_shards=True,
                               rider=mlp_out.chip_exchange(riding))
    mlp_in = _GradReduction({"w_mlp_in": d_wmi}, place, "mlp_in")
    (*dya, drest, d_wout, d_wba, d_wbg, d_wsp, d_bb, d_lg, d_lb), riding = _mix_bwd(
        dy, ya[0], yg, mg, rest, full["w_out"], full["w_branch_attn"], full["w_branch_gmlp"], w_sp, b_col, ln_g, ln_b,
        rider=_together(mlp_out.pair_share(riding), mlp_in.pair_exchange()))
    reduced = mlp_out.result(riding[:1])
    mix = _GradReduction({"w_branch_attn": d_wba, "w_branch_gmlp": d_wbg,
                          "w_out": d_wout.reshape(N_CHIPS, D_MODEL // N_CHIPS, D_MODEL)}, place, "mix")
    attn = lambda g, rider: _attn_bwd(qkv[g], dya[g], ya[g], lse[g], *tables[DILATIONS[g]], DILATIONS[g], rider=rider)
    dqkv0, riding = attn(0, _together(mlp_in.chip_exchange(riding[1:]), mix.pair_exchange()))
    dqkv1, riding = attn(1, _together(mlp_in.pair_share(riding[:1]), mix.chip_exchange(riding[1:])))
    reduced.update(mlp_in.result(riding[:1]))
    dqkv2, riding = attn(2, mix.pair_share(riding[1:]))
    reduced.update(mix.result(riding))
    dqkv = [dqkv0, dqkv1, dqkv2]

    d_qkv = [_tn_matmul_residue(dqkv[g], h[g], dil, f"grad_w_in_qkv{g}") for g, dil in enumerate(DILATIONS)]
    d_rest, _ = _tn_matmul(drest, h[0], "grad_w_in_rest", 1024, 1024)
    d_win = jnp.concatenate([d_qkv[g][s * GROUP_W:(s + 1) * GROUP_W] for s in range(3) for g in range(N_GROUPS)]
                            + [d_rest], axis=0)
    first = _GradReduction({"w_in": d_win.reshape(N_CHIPS, IN_W // N_CHIPS, D_MODEL)}, place, "w_in")
    tiles = seq // IN_PROJ_BWD_TM
    so_far = (lax.empty((seq, D_MODEL), F32), jnp.zeros((1, D_MODEL), F32))
    in_bwd = lambda so_far, span, rider: _in_proj_bwd(dqkv, drest, w_in, x, dx1, g0, so_far, span, rider=rider)
    so_far, riding = in_bwd(so_far, (0, 3 * tiles // 8), first.pair_exchange())
    (grad_x, dg0), riding = in_bwd(so_far, (3 * tiles // 8, 5 * tiles // 8), first.chip_exchange(riding))
    reduced.update(first.result(_run_exchange(first.pair_share(riding), "w_in_pair_share")))
    little = {"norm_pre_mix": dg0, "w_spatial": d_wsp, "b_spatial": d_bb[:, :, 0], "ln_v_gain": d_lg, "ln_v_bias": d_lb,
              "norm_post_mix": dg1, "norm_pre_mlp": dg2, "norm_post_mlp": dg3}
    return loss8, grad_x, reduced, little


def kernel(x, norm_pre_mix, w_in, w_spatial, b_spatial, ln_v_gain, ln_v_bias, w_branch_attn, w_branch_gmlp, w_out, norm_post_mix, norm_pre_mlp, w_mlp_in, w_mlp_out, norm_post_mlp, loss_target, m_norm_pre_mix, m_w_in, m_w_spatial, m_b_spatial, m_ln_v_gain, m_ln_v_bias, m_w_branch_attn, m_w_branch_gmlp, m_w_out, m_norm_post_mix, m_norm_pre_mlp, m_w_mlp_in, m_w_mlp_out, m_norm_post_mlp, v_norm_pre_mix, v_w_in, v_w_spatial, v_b_spatial, v_ln_v_gain, v_ln_v_bias, v_w_branch_attn, v_w_branch_gmlp, v_w_out, v_norm_post_mix, v_norm_pre_mlp, v_w_mlp_in, v_w_mlp_out, v_norm_post_mlp):
    given = dict(norm_pre_mix=norm_pre_mix, w_in=w_in, w_spatial=w_spatial, b_spatial=b_spatial, ln_v_gain=ln_v_gain,
                 ln_v_bias=ln_v_bias, w_branch_attn=w_branch_attn, w_branch_gmlp=w_branch_gmlp, w_out=w_out,
                 norm_post_mix=norm_post_mix, norm_pre_mlp=norm_pre_mlp, w_mlp_in=w_mlp_in, w_mlp_out=w_mlp_out,
                 norm_post_mlp=norm_post_mlp)
    moments_m = dict(norm_pre_mix=m_norm_pre_mix, w_in=m_w_in, w_spatial=m_w_spatial, b_spatial=m_b_spatial,
                     ln_v_gain=m_ln_v_gain, ln_v_bias=m_ln_v_bias, w_branch_attn=m_w_branch_attn,
                     w_branch_gmlp=m_w_branch_gmlp, w_out=m_w_out, norm_post_mix=m_norm_post_mix,
                     norm_pre_mlp=m_norm_pre_mlp, w_mlp_in=m_w_mlp_in, w_mlp_out=m_w_mlp_out, norm_post_mlp=m_norm_post_mlp)
    moments_v = dict(norm_pre_mix=v_norm_pre_mix, w_in=v_w_in, w_spatial=v_w_spatial, b_spatial=v_b_spatial,
                     ln_v_gain=v_ln_v_gain, ln_v_bias=v_ln_v_bias, w_branch_attn=v_w_branch_attn,
                     w_branch_gmlp=v_w_branch_gmlp, w_out=v_w_out, norm_post_mix=v_norm_post_mix,
                     norm_pre_mlp=v_norm_pre_mlp, w_mlp_in=v_w_mlp_in, w_mlp_out=v_w_mlp_out, norm_post_mlp=v_norm_post_mlp)
    cx, cy, cc = lax.axis_index("x"), lax.axis_index("y"), lax.axis_index("c")

    shards = [given[n][0].T if n == "w_in" else given[n][0] for n in BIG]
    small = {n: given[n][0] if given[n].ndim > 2 else given[n] for n in SMALL}
    place = jnp.stack([cc, 2 * cx + cy]).astype(jnp.int32)
    loss8, grad_x, grad_shard, grads = _device_step(x[0], loss_target[0], small, shards, place)

    packed = jnp.concatenate([_rows8(grads[n]) for n in SMALL] + [loss8], axis=0)
    summed = _all_reduce_small(packed)
    loss = summed[packed.shape[0] - loss8.shape[0], 0]
    row = 0
    for n in SMALL:
        shape = given[n][0].shape
        cnt = -(-(given[n][0].size // 128) // 8) * 8
        grad_shard[n] = summed[row:row + given[n][0].size // 128].reshape(shape)
        row += cnt

    grad_out, deltas, new_m, new_v = {}, {}, {}, {}
    for n in ORDER:
        shape = given[n].shape
        if n == "w_in":
            outs = _adamw(given[n][0].T, grad_shard[n], moments_m[n][0].T, moments_v[n][0].T, "adamw_" + n)
            outs = [o.T for o in outs]
        else:
            two_d = (-1, shape[-1])
            outs = _adamw(given[n].reshape(two_d), grad_shard[n].reshape(two_d), moments_m[n].reshape(two_d),
                          moments_v[n].reshape(two_d), "adamw_" + n)
        grad_out[n], deltas[n], new_m[n], new_v[n] = [o.reshape(shape) for o in outs]
    return (loss, grad_x[None], *[grad_out[n] for n in ORDER], *[deltas[n] for n in ORDER], *[new_m[n] for n in ORDER],
            *[new_v[n] for n in ORDER])
```

```python
import math

import jax
import jax.numpy as jnp
from jax import lax
from jax.experimental import pallas as pl
from jax.experimental.pallas import tpu as pltpu

F32 = jnp.float32
BF16 = jnp.bfloat16
MESH = pl.DeviceIdType.MESH

D_MODEL = 1024
HEAD_DIM = 64
HEADS_PER_GROUP = 4
GROUP_W = HEADS_PER_GROUP * HEAD_DIM
DILATIONS = (1, 4, 16)
N_GROUPS = len(DILATIONS)
ATTN_W = N_GROUPS * GROUP_W
QKV_W = 3 * ATTN_W
GMLP_W = 512
GMLP_GROUPS = 4
CHUNK = 128
REST_W = 2 * GMLP_W + 2 * D_MODEL
IN_W = QKV_W + REST_W
D_FF = 4096
QBLK = 128
ROPE_THETA = 10000.0
EPS = 1e-6
NEG = -1e30
SCALE = HEAD_DIM ** -0.5
N_CHIPS = 4

ADAM_LR = 0.001
ADAM_B1 = 0.9
ADAM_B2 = 0.999
ADAM_EPS = 1e-08
ADAM_WD = 0.01
ADAM_STEP = 10

MIB = 1024 * 1024
HBM_SPEC = pl.BlockSpec(memory_space=pltpu.HBM)
VMEM_SPEC = pl.BlockSpec(memory_space=pltpu.VMEM)


MLP_FWD_TM = 512
MLP_TM = 512


CALL_VMEM_MIB = 56
SMALL_VMEM_MIB = 32


def _params(semantics, vmem_mib, small=False):
    assert vmem_mib <= CALL_VMEM_MIB
    return pltpu.CompilerParams(dimension_semantics=semantics,
                                vmem_limit_bytes=(SMALL_VMEM_MIB if small else CALL_VMEM_MIB) * MIB)


def _in_hbm(a):
    return pltpu.with_memory_space_constraint(a, pltpu.HBM) if a.size * a.dtype.itemsize >= MIB else a


def _pallas(body, **kwargs):
    return pl.pallas_call(body, **kwargs)


def _resident(shape):
    return pl.BlockSpec(shape, lambda *_: (0,) * len(shape), pipeline_mode=pl.Buffered(1))


def _dot(a, b):
    return jnp.dot(a, b, preferred_element_type=F32)


def _dot_nt(a, b):
    return lax.dot_general(a, b, (((1,), (1,)), ((), ())), preferred_element_type=F32)


def _dot_tn(a, b):
    return lax.dot_general(a, b, (((0,), (0,)), ((), ())), preferred_element_type=F32)


_GELU_C = math.sqrt(2.0 / math.pi)


def _gelu(x):
    return x * (0.5 * (1.0 + jnp.tanh(_GELU_C * (x + 0.044715 * (x * x * x)))))


def _gelu_grad(x):
    t = jnp.tanh(_GELU_C * (x + 0.044715 * (x * x * x)))
    return 0.5 * (1.0 + t) + 0.5 * x * (1.0 - t * t) * (_GELU_C * (1.0 + 3.0 * 0.044715 * (x * x)))


def _rsqrt_ms(v):
    return lax.rsqrt(jnp.mean(v * v, axis=-1, keepdims=True) + EPS)


def _rmsnorm_bwd(dn, src, gain):
    r = _rsqrt_ms(src)
    t = gain * dn
    dgain = jnp.sum(dn * (src * r), axis=0, keepdims=True)
    dsrc = r * t - src * ((r * r * r) * jnp.mean(t * src, axis=-1, keepdims=True))
    return dsrc, dgain


def _rot_half(v):
    w = v.shape[-1]
    lane = lax.broadcasted_iota(jnp.int32, v.shape, v.ndim - 1)
    return jnp.where((lane % HEAD_DIM) < HEAD_DIM // 2, pltpu.roll(v, w - HEAD_DIM // 2, v.ndim - 1),
                     pltpu.roll(v, HEAD_DIM // 2, v.ndim - 1))


def _head_masks(shape):
    lane = lax.broadcasted_iota(jnp.int32, shape, 1)
    return [(lane >= h * HEAD_DIM) & (lane < (h + 1) * HEAD_DIM) for h in range(HEADS_PER_GROUP)]


def _head_stack(block, hmask):
    zero = jnp.zeros((), block.dtype)
    return jnp.concatenate([jnp.where(hm, block, zero) for hm in hmask], axis=0)


LANES = 128


def _put_residue(slab, val, out_ref, dil, width, col0):
    tm, w = val.shape
    if dil == 1:
        out_ref[:, col0:col0 + w] = val.astype(out_ref.dtype)
        return
    for k in range(w // LANES):
        slab[k] = val[:, k * LANES:(k + 1) * LANES]
    for r in range(dil):
        for k in range(w // LANES):
            c = r * width + col0 + k * LANES
            out_ref[:, c:c + LANES] = slab[k, pl.ds(r, tm // dil, stride=dil), :].astype(out_ref.dtype)


def _get_tokens(slab, in_ref, dil, width, col0, w):
    if dil == 1:
        return in_ref[:, col0:col0 + w].astype(F32)
    rows = in_ref.shape[0]
    for r in range(dil):
        for k in range(w // LANES):
            c = r * width + col0 + k * LANES
            slab[k, pl.ds(r, rows, stride=dil), :] = in_ref[:, c:c + LANES].astype(F32)
    return jnp.concatenate([slab[k] for k in range(w // LANES)], axis=1)


def _prepare(x, g0, rider=None):
    seq = x.shape[0]
    half = HEAD_DIM // 2
    inv_freq = ROPE_THETA ** (-jnp.arange(half, dtype=F32) / half)
    freq = jnp.tile(inv_freq, LANES // half).reshape(1, LANES)
    tm = 256

    def body(x_ref, g_ref, f_ref, *refs):
        h_refs, tabs, slab = refs[:N_GROUPS], refs[N_GROUPS:3 * N_GROUPS], refs[-1]
        xv = x_ref[...]
        hf = (xv * _rsqrt_ms(xv)) * g_ref[...]
        for g, dil in enumerate(DILATIONS):
            _put_residue(slab, hf, h_refs[g], dil, D_MODEL, 0)
        row = lax.broadcasted_iota(jnp.int32, (tm, LANES), 0) + pl.program_id(0) * tm
        lane = lax.broadcasted_iota(jnp.int32, (tm, LANES), 1)
        ang = row.astype(F32) * f_ref[...]
        cos = jnp.cos(ang)
        sin = jnp.where((lane % HEAD_DIM) < half, -jnp.sin(ang), jnp.sin(ang))
        for i, dil in enumerate(DILATIONS):
            for tab, val in ((tabs[2 * i], cos), (tabs[2 * i + 1], sin)):
                slab[0] = val
                for r in range(dil):
                    piece = slab[0, pl.ds(r, tm // dil, stride=dil), :] if dil > 1 else val
                    for k in range(GROUP_W // LANES):
                        tab[:, r * GROUP_W + k * LANES:r * GROUP_W + (k + 1) * LANES] = piece

    outs, riding = _call(
        body, name="prepare", grid=(seq // tm,),
        in_specs=[pl.BlockSpec((tm, D_MODEL), lambda i: (i, 0)), pl.BlockSpec((1, D_MODEL), lambda i: (0, 0)),
                  pl.BlockSpec((1, LANES), lambda i: (0, 0))],
        out_specs=[pl.BlockSpec((tm // d, d * D_MODEL), lambda i: (i, 0)) for d in DILATIONS]
        + [pl.BlockSpec((tm // d, d * GROUP_W), lambda i: (i, 0)) for d in DILATIONS for _ in range(2)],
        out_shape=[jax.ShapeDtypeStruct((seq // d, d * D_MODEL), BF16) for d in DILATIONS]
        + [jax.ShapeDtypeStruct((seq // d, d * GROUP_W), F32) for d in DILATIONS for _ in range(2)],
        scratch_shapes=[pltpu.VMEM((D_MODEL // LANES, tm, LANES), F32)],
        params=_params(("arbitrary",), 32), args=(x, g0, freq), rider=rider)
    tabs = outs[N_GROUPS:]
    return outs[:N_GROUPS], {d: (tabs[2 * i], tabs[2 * i + 1]) for i, d in enumerate(DILATIONS)}, riding


def _in_proj(h, w_in, cos_t, sin_t, rider=None):
    seq = h.shape[0]
    tm, tn = 512, GROUP_W
    n_qk = 2 * ATTN_W // tn
    n_qkv = QKV_W // tn

    def body(h_ref, w_ref, cos_ref, sin_ref, *refs):
        qkv_refs, rest_ref, slab = refs[:N_GROUPS], refs[N_GROUPS], refs[-1]
        hb = h_ref[...]
        cos, sin = cos_ref[...], sin_ref[...]
        for j in range(IN_W // tn):
            p = _dot_nt(hb, w_ref[j * tn:(j + 1) * tn, :])
            if j < n_qkv:
                if j < n_qk:
                    p = p * cos + _rot_half(p) * sin
                section, g = divmod(j, N_GROUPS)
                _put_residue(slab, p, qkv_refs[g], DILATIONS[g], 3 * GROUP_W, section * GROUP_W)
            else:
                rest_ref[:, (j - n_qkv) * tn:(j - n_qkv + 1) * tn] = p.astype(BF16)

    return _call(
        body, name="in_proj", grid=(seq // tm,),
        in_specs=[pl.BlockSpec((tm, D_MODEL), lambda i: (i, 0)),
                  _resident((IN_W, D_MODEL)),
                  pl.BlockSpec((tm, GROUP_W), lambda i: (i, 0)),
                  pl.BlockSpec((tm, GROUP_W), lambda i: (i, 0))],
        out_specs=[pl.BlockSpec((tm // d, d * 3 * GROUP_W), lambda i: (i, 0)) for d in DILATIONS]
        + [pl.BlockSpec((tm, REST_W), lambda i: (i, 0))],
        out_shape=[jax.ShapeDtypeStruct((seq // d, d * 3 * GROUP_W), BF16) for d in DILATIONS]
        + [jax.ShapeDtypeStruct((seq, REST_W), BF16)],
        scratch_shapes=[pltpu.VMEM((GROUP_W // LANES, tm, LANES), F32)],
        params=_params(("arbitrary",), 48), args=(h, w_in, cos_t, sin_t), rider=rider)


def _band_masks():
    qi = lax.broadcasted_iota(jnp.int32, (QBLK, QBLK), 0)
    kj = lax.broadcasted_iota(jnp.int32, (QBLK, QBLK), 1)
    return kj <= qi, kj >= qi


def _attn_tile(length):
    return min(1024, length)


def _attn_fwd(qkv, dil, rider=None):
    length = qkv.shape[0]
    tq = _attn_tile(length)
    nsub = tq // QBLK
    nblk = length // tq

    def body(q_ref, k_ref, v_ref, kp_ref, vp_ref, o_ref, l_ref):
        n = pl.program_id(1)
        mask_c, mask_p0 = _band_masks()
        hmask = _head_masks((QBLK, GROUP_W))
        zero = jnp.zeros((), BF16)
        for b in range(nsub):
            rows = slice(b * QBLK, (b + 1) * QBLK)
            q = q_ref[rows, :]
            kc, vc = k_ref[rows, :], v_ref[rows, :]
            if b == 0:
                kp, vp = kp_ref[...], vp_ref[...]
                mask_p = mask_p0 & (n > 0)
            else:
                prow = slice((b - 1) * QBLK, b * QBLK)
                kp, vp = k_ref[prow, :], v_ref[prow, :]
                mask_p = mask_p0
            o_acc = jnp.zeros((QBLK, GROUP_W), F32)
            l_acc = jnp.zeros((QBLK, GROUP_W), F32)
            for h in range(HEADS_PER_GROUP):
                hm = hmask[h]
                sc = jnp.where(mask_c, _dot_nt(q, jnp.where(hm, kc, zero)) * SCALE, NEG)
                sp = jnp.where(mask_p, _dot_nt(q, jnp.where(hm, kp, zero)) * SCALE, NEG)
                m = jnp.maximum(jnp.max(sc, axis=-1, keepdims=True), jnp.max(sp, axis=-1, keepdims=True))
                pc, pp = jnp.exp(sc - m), jnp.exp(sp - m)
                den = jnp.sum(pc, axis=-1, keepdims=True) + jnp.sum(pp, axis=-1, keepdims=True)
                pv = _dot(pc.astype(BF16), jnp.where(hm, vc, zero)) + _dot(pp.astype(BF16), jnp.where(hm, vp, zero))
                o_acc = o_acc + pv / den
                l_acc = l_acc + jnp.where(hm, m + jnp.log(den), 0.0)
            o_ref[rows, :] = o_acc.astype(BF16)
            l_ref[rows, :] = l_acc

    cur = lambda sec: pl.BlockSpec((tq, GROUP_W), lambda r, n: (n, r * 3 + sec))
    prev = lambda sec: pl.BlockSpec((QBLK, GROUP_W), lambda r, n: (jnp.maximum(n * nsub - 1, 0), r * 3 + sec))
    return _call(
        body, name=f"attn_fwd_d{dil}", grid=(dil, nblk),
        in_specs=[cur(0), cur(1), cur(2), prev(1), prev(2)],
        out_specs=[pl.BlockSpec((tq, GROUP_W), lambda r, n: (n, r))] * 2,
        out_shape=[jax.ShapeDtypeStruct((length, dil * GROUP_W), BF16),
                   jax.ShapeDtypeStruct((length, dil * GROUP_W), F32)], scratch_shapes=[],
        params=_params(("arbitrary", "arbitrary"), 32), args=(qkv, qkv, qkv, qkv, qkv), rider=rider)


def _attn_bwd(qkv, dy, y, lse, cos_t, sin_t, dil, rider=None):
    length = qkv.shape[0]
    tq = _attn_tile(length)
    nsub = tq // QBLK
    nblk = length // tq

    def body(q_ref, k_ref, v_ref, kp_ref, vp_ref, qn_ref, dy_ref, y_ref, l_ref, dyn_ref, yn_ref, ln_ref,
             cos_ref, sin_ref, out_ref, dq_s, dk_s, dv_s):
        n = pl.program_id(1)
        mask_c, mask_p0 = _band_masks()
        hmask = _head_masks((QBLK, GROUP_W))
        sub = lambda ref, b: ref[b * QBLK:(b + 1) * QBLK, :]
        kbd = [_head_stack(kp_ref[...], hmask)] + [_head_stack(sub(k_ref, b), hmask) for b in range(nsub)]
        vbd = [_head_stack(vp_ref[...], hmask)] + [_head_stack(sub(v_ref, b), hmask) for b in range(nsub)]
        dq_s[...] = jnp.zeros(dq_s.shape, F32)

        def query_terms(q, dyv, yv, lv):
            prod = dyv * yv
            return dict(
                q=q, dy=dyv.astype(BF16), q_heads=[jnp.where(hm, q, jnp.zeros((), BF16)) for hm in hmask],
                dy_heads=[jnp.where(hm, dyv, 0.0).astype(BF16) for hm in hmask],
                delta=[jnp.sum(jnp.where(hm, prod, 0.0), axis=-1, keepdims=True) for hm in hmask],
                lse=[jnp.max(jnp.where(hm, lv, NEG), axis=-1, keepdims=True) for hm in hmask])

        queries = [query_terms(sub(q_ref, b), sub(dy_ref, b), sub(y_ref, b), sub(l_ref, b)) for b in range(nsub)]
        queries.append(query_terms(qn_ref[...], dyn_ref[...], yn_ref[...], ln_ref[...]))
        rows_of = lambda items: items[0] if len(items) == 1 else jnp.concatenate(items, axis=0)
        for kb in range(nsub + 1):
            seen = [(kb - 1, mask_c)] if kb >= 1 else []
            if kb == 0:
                seen.append((0, mask_p0 & (n > 0)))
            elif kb < nsub:
                seen.append((kb, mask_p0))
            else:
                seen.append((nsub, mask_p0 & (n < nblk - 1)))
            qs = [queries[b] for b, _ in seen]
            mask = rows_of([m for _, m in seen])
            s = _dot_nt(rows_of([t["q"] for t in qs]), kbd[kb]) * SCALE
            dp = _dot_nt(rows_of([t["dy"] for t in qs]), vbd[kb])
            ps, dss = [], []
            for h in range(HEADS_PER_GROUP):
                cols = slice(h * QBLK, (h + 1) * QBLK)
                p = jnp.exp(jnp.where(mask, s[:, cols] - rows_of([t["lse"][h] for t in qs]), NEG))
                ps.append(p.astype(BF16))
                dss.append((p * (dp[:, cols] - rows_of([t["delta"][h] for t in qs]))).astype(BF16))
            dq = _dot(jnp.concatenate(dss, axis=1), kbd[kb]) * SCALE
            for i, (b, _) in enumerate(seen):
                if b < nsub:
                    dq_s[b * QBLK:(b + 1) * QBLK, :] += dq[i * QBLK:(i + 1) * QBLK, :]
            if kb >= 1:
                krows = slice((kb - 1) * QBLK, kb * QBLK)
                head_rows = lambda key: jnp.concatenate([t[key][h] for h in range(HEADS_PER_GROUP) for t in qs], axis=0)
                dv_s[krows, :] = _dot_tn(jnp.concatenate(ps, axis=0), head_rows("dy_heads"))
                dk_s[krows, :] = _dot_tn(jnp.concatenate(dss, axis=0), head_rows("q_heads")) * SCALE
        cos, sin = cos_ref[...], sin_ref[...]
        dq, dk = dq_s[...], dk_s[...]
        out_ref[:, 0:GROUP_W] = (dq * cos - _rot_half(dq) * sin).astype(BF16)
        out_ref[:, GROUP_W:2 * GROUP_W] = (dk * cos - _rot_half(dk) * sin).astype(BF16)
        out_ref[:, 2 * GROUP_W:3 * GROUP_W] = dv_s[...].astype(BF16)

    cur = lambda sec: pl.BlockSpec((tq, GROUP_W), lambda r, n: (n, r * 3 + sec))
    prev = lambda sec: pl.BlockSpec((QBLK, GROUP_W), lambda r, n: (jnp.maximum(n * nsub - 1, 0), r * 3 + sec))
    nxt_q = pl.BlockSpec((QBLK, GROUP_W), lambda r, n: (jnp.minimum((n + 1) * nsub, nblk * nsub - 1), r * 3))
    tok = pl.BlockSpec((tq, GROUP_W), lambda r, n: (n, r))
    tok_next = pl.BlockSpec((QBLK, GROUP_W), lambda r, n: (jnp.minimum((n + 1) * nsub, nblk * nsub - 1), r))
    (out,), riding = _call(
        body, name=f"attn_bwd_d{dil}", grid=(dil, nblk),
        in_specs=[cur(0), cur(1), cur(2), prev(1), prev(2), nxt_q,
                  tok, tok, tok, tok_next, tok_next, tok_next, tok, tok],
        out_specs=[pl.BlockSpec((tq, 3 * GROUP_W), lambda r, n: (n, r))],
        out_shape=[jax.ShapeDtypeStruct((length, dil * 3 * GROUP_W), BF16)],
        scratch_shapes=[pltpu.VMEM((tq, GROUP_W), F32)] * 3,
        params=_params(("arbitrary", "arbitrary"), 32),
        args=(qkv, qkv, qkv, qkv, qkv, qkv, dy, y, lse, dy, y, lse, cos_t, sin_t), rider=rider)
    return out, riding


def _layernorm_stats(z):
    mu = jnp.mean(z, axis=-1, keepdims=True)
    zc = z - mu
    rstd = lax.rsqrt(jnp.mean(zc * zc, axis=-1, keepdims=True) + EPS)
    return zc * rstd, rstd


def _tril_mask():
    row = lax.broadcasted_iota(jnp.int32, (CHUNK, CHUNK), 0)
    col = lax.broadcasted_iota(jnp.int32, (CHUNK, CHUNK), 1)
    return col <= row


def _mix_fwd(o_l, rest, x, w_sp, b_col, ln_g, ln_b, w_ba, w_bg, w_out, g1, rider=None):
    seq = x.shape[0]
    tm = 256

    def body(o0, l0, o1, l1, o2, l2, up_ref, zp_ref, gap_ref, gbp_ref, x_ref, wsp_ref, bcol_ref, lg_ref, lb_ref,
             wba_ref, wbg_ref, wout_ref, g1_ref, ya0, lj0, ya1, lj1, ya2, lj2, yg_ref, mg_ref, y_ref, x1_ref, slab):
        outs = [_get_tokens(slab, o, d, GROUP_W, 0, GROUP_W) for o, d in zip((o0, o1, o2), DILATIONS)]
        lses = [_get_tokens(slab, l, d, GROUP_W, 0, GROUP_W) for l, d in zip((l0, l1, l2), DILATIONS)]
        m = jnp.maximum(jnp.maximum(lses[0], lses[1]), lses[2])
        es = [jnp.exp(l - m) for l in lses]
        tot = es[0] + es[1] + es[2]
        ya = (es[0] * outs[0] + es[1] * outs[1] + es[2] * outs[2]) / tot
        lj = m + jnp.log(tot)
        for ya_ref, lj_ref, d in zip((ya0, ya1, ya2), (lj0, lj1, lj2), DILATIONS):
            _put_residue(slab, ya, ya_ref, d, GROUP_W, 0)
            _put_residue(slab, lj, lj_ref, d, GROUP_W, 0)
        zhat, _ = _layernorm_stats(_gelu(zp_ref[...].astype(F32)))
        zln = (zhat * lg_ref[...] + lb_ref[...]).astype(BF16)
        u = _gelu(up_ref[...].astype(F32))
        tril = _tril_mask()
        for g in range(GMLP_GROUPS):
            wm = jnp.where(tril, wsp_ref[g], 0.0).astype(BF16)
            cols = slice(g * CHUNK, (g + 1) * CHUNK)
            for c in range(tm // CHUNK):
                rows = slice(c * CHUNK, (c + 1) * CHUNK)
                sz = _dot(wm, zln[rows, cols]) + bcol_ref[g]
                yg_ref[rows, cols] = (u[rows, cols] * sz).astype(BF16)
        a = _dot(ya.astype(BF16), wba_ref[...])
        bm = _dot(yg_ref[...], wbg_ref[...])
        merged = (jax.nn.sigmoid(gap_ref[...].astype(F32)) * a + jax.nn.sigmoid(gbp_ref[...].astype(F32)) * bm).astype(BF16)
        mg_ref[...] = merged
        yv = _dot(merged, wout_ref[...])
        y_ref[...] = yv.astype(BF16)
        x1_ref[...] = x_ref[...] + (yv * _rsqrt_ms(yv)) * g1_ref[...]

    tok = lambda w: pl.BlockSpec((tm, w), lambda i: (i, 0))
    res = lambda d: pl.BlockSpec((tm // d, d * GROUP_W), lambda i: (i, 0))
    full = lambda *s: pl.BlockSpec(s, lambda i: (0,) * len(s))
    res_specs = [res(d) for d in DILATIONS for _ in range(2)]
    return _call(
        body, name="mix_fwd", grid=(seq // tm,),
        in_specs=res_specs + [
            pl.BlockSpec((tm, GMLP_W), lambda i: (i, 0)), pl.BlockSpec((tm, GMLP_W), lambda i: (i, 1)),
            pl.BlockSpec((tm, D_MODEL), lambda i: (i, 1)), pl.BlockSpec((tm, D_MODEL), lambda i: (i, 2)),
            tok(D_MODEL), full(GMLP_GROUPS, CHUNK, CHUNK), full(GMLP_GROUPS, CHUNK, 1), full(1, GMLP_W), full(1, GMLP_W),
            full(GROUP_W, D_MODEL), full(GMLP_W, D_MODEL), full(D_MODEL, D_MODEL), full(1, D_MODEL)],
        out_specs=res_specs + [tok(GMLP_W), tok(D_MODEL), tok(D_MODEL), tok(D_MODEL)],
        out_shape=[jax.ShapeDtypeStruct((seq // d, d * GROUP_W), F32) for d in DILATIONS for _ in range(2)]
        + [jax.ShapeDtypeStruct((seq, GMLP_W), BF16), jax.ShapeDtypeStruct((seq, D_MODEL), BF16),
           jax.ShapeDtypeStruct((seq, D_MODEL), BF16), jax.ShapeDtypeStruct((seq, D_MODEL), F32)],
        scratch_shapes=[pltpu.VMEM((GROUP_W // LANES, tm, LANES), F32)],
        params=_params(("arbitrary",), 48),
        args=(*o_l, rest, rest, rest, rest, x, w_sp, b_col, ln_g, ln_b, w_ba, w_bg, w_out, g1), rider=rider)


def _mlp_fwd(x1, g2, g3, w_mi, w_mo, target):
    seq = x1.shape[0]
    tm, tf = MLP_FWD_TM, 512

    def body(x1_ref, g2_ref, g3_ref, wmi_ref, wmo_ref, t_ref, h2_ref, a_ref, dy2_ref, dout_ref, loss_ref, dg3_ref, sq_s):
        @pl.when(pl.program_id(0) == 0)
        def _():
            loss_ref[...] = jnp.zeros(loss_ref.shape, F32)
            dg3_ref[...] = jnp.zeros(dg3_ref.shape, F32)

        xv = x1_ref[...]
        hb = ((xv * _rsqrt_ms(xv)) * g2_ref[...]).astype(BF16)
        h2_ref[...] = hb
        for j in range(D_FF // tf):
            cols = slice(j * tf, (j + 1) * tf)
            a = jnp.maximum(_dot(hb, wmi_ref[:, cols]), 0.0)
            a_ref[:, cols] = a.astype(BF16)
            sq_s[:, cols] = (a * a).astype(BF16)
        y2 = _dot(sq_s[...], wmo_ref[...])
        r3 = _rsqrt_ms(y2)
        out = xv + (y2 * r3) * g3_ref[...]
        diff = out - t_ref[...]
        tile_loss = 0.5 * jnp.sum(jnp.mean(diff * diff, axis=-1, keepdims=True), axis=0, keepdims=True)
        loss_ref[...] += jnp.broadcast_to(tile_loss, loss_ref.shape)
        dout = diff * (1.0 / D_MODEL)
        dout_ref[...] = dout
        dy2, dg3 = _rmsnorm_bwd(dout, y2, g3_ref[...])
        dy2_ref[...] = dy2.astype(BF16)
        dg3_ref[...] += dg3

    tok = lambda w: pl.BlockSpec((tm, w), lambda i: (i, 0))
    vec = pl.BlockSpec((1, D_MODEL), lambda i: (0, 0))
    return _pallas(
        body, name="mlp_fwd", grid=(seq // tm,),
        in_specs=[tok(D_MODEL), vec, vec, _resident((D_MODEL, D_FF)), _resident((D_FF, D_MODEL)), tok(D_MODEL)],
        out_specs=[tok(D_MODEL), tok(D_FF), tok(D_MODEL), tok(D_MODEL), pl.BlockSpec((8, 128), lambda i: (0, 0)), vec],
        out_shape=[jax.ShapeDtypeStruct((seq, D_MODEL), BF16), jax.ShapeDtypeStruct((seq, D_FF), BF16),
                   jax.ShapeDtypeStruct((seq, D_MODEL), BF16), jax.ShapeDtypeStruct((seq, D_MODEL), F32),
                   jax.ShapeDtypeStruct((8, 128), F32), jax.ShapeDtypeStruct((1, D_MODEL), F32)],
        scratch_shapes=[pltpu.VMEM((tm, D_FF), BF16)],
        compiler_params=_params(("arbitrary",), 56),
    )(*map(_in_hbm, (x1, g2, g3, w_mi, w_mo, target)))


def _mlp_bwd(dy2, a, w_mo, w_mi, dout, x1, y, g2, g1, rider=None):
    seq = x1.shape[0]
    tm, tf = MLP_TM, 512

    def body(dy2_ref, a_ref, wmo_ref, wmi_ref, dout_ref, x1_ref, y_ref, g2_ref, g1_ref,
             dap_ref, dx1_ref, dy_ref, dg2_ref, dg1_ref):
        @pl.when(pl.program_id(0) == 0)
        def _():
            dg2_ref[...] = jnp.zeros(dg2_ref.shape, F32)
            dg1_ref[...] = jnp.zeros(dg1_ref.shape, F32)

        dy2v = dy2_ref[...]
        for j in range(D_FF // tf):
            cols = slice(j * tf, (j + 1) * tf)
            da2 = _dot_nt(dy2v, wmo_ref[cols, :])
            dap_ref[:, cols] = (da2 * (2.0 * a_ref[:, cols].astype(F32))).astype(BF16)
        dh2 = _dot_nt(dap_ref[...], wmi_ref[...])
        dres, dg2 = _rmsnorm_bwd(dh2, x1_ref[...], g2_ref[...])
        dx1 = dout_ref[...] + dres
        dx1_ref[...] = dx1
        dg2_ref[...] += dg2
        dyv, dg1 = _rmsnorm_bwd(dx1, y_ref[...].astype(F32), g1_ref[...])
        dy_ref[...] = dyv.astype(BF16)
        dg1_ref[...] += dg1

    tok = lambda w: pl.BlockSpec((tm, w), lambda i: (i, 0))
    vec = pl.BlockSpec((1, D_MODEL), lambda i: (0, 0))
    return _call(
        body, name="mlp_bwd", grid=(seq // tm,),
        in_specs=[tok(D_MODEL), tok(D_FF), _resident((D_FF, D_MODEL)), _resident((D_MODEL, D_FF)),
                  tok(D_MODEL), tok(D_MODEL), tok(D_MODEL), vec, vec],
        out_specs=[tok(D_FF), tok(D_MODEL), tok(D_MODEL), vec, vec],
        out_shape=[jax.ShapeDtypeStruct((seq, D_FF), BF16), jax.ShapeDtypeStruct((seq, D_MODEL), F32),
                   jax.ShapeDtypeStruct((seq, D_MODEL), BF16), jax.ShapeDtypeStruct((1, D_MODEL), F32),
                   jax.ShapeDtypeStruct((1, D_MODEL), F32)], scratch_shapes=[],
        params=_params(("arbitrary",), 56), args=(dy2, a, w_mo, w_mi, dout, x1, y, g2, g1), rider=rider)


def _tn_matmul(a, b, name, bm, bn, square_a=False, column_shards=False, rider=None):
    seq, m = a.shape
    n = b.shape[1]
    ts = 2048

    def body(a_ref, b_ref, o_ref):
        @pl.when(pl.program_id(2) == 0)
        def _():
            o_ref[...] = jnp.zeros(o_ref.shape, F32)

        av = a_ref[...]
        if square_a:
            af = av.astype(F32)
            av = (af * af).astype(BF16)
        o_ref[...] += _dot_tn(av, b_ref[...])

    if column_shards:
        out_spec = pl.BlockSpec((None, bm, bn), lambda mi, ni, s: (ni, mi, 0))
        out_shape = jax.ShapeDtypeStruct((n // bn, m, bn), F32)
    else:
        out_spec = pl.BlockSpec((bm, bn), lambda mi, ni, s: (mi, ni))
        out_shape = jax.ShapeDtypeStruct((m, n), F32)
    (out,), riding = _call(
        body, name=name, grid=(m // bm, n // bn, seq // ts),
        in_specs=[pl.BlockSpec((ts, bm), lambda mi, ni, s: (s, mi)), pl.BlockSpec((ts, bn), lambda mi, ni, s: (s, ni))],
        out_specs=[out_spec], out_shape=[out_shape], scratch_shapes=[],
        params=_params(("arbitrary", "arbitrary", "arbitrary"), 40), args=(a, b), rider=rider)
    return out, riding


def _tn_matmul_residue(a, b, dil, name):
    length = a.shape[0]
    m, n = a.shape[1] // dil, b.shape[1] // dil
    ts = min(1024, length)

    def body(a_ref, b_ref, o_ref):
        @pl.when((pl.program_id(0) == 0) & (pl.program_id(1) == 0))
        def _():
            o_ref[...] = jnp.zeros(o_ref.shape, F32)

        o_ref[...] += _dot_tn(a_ref[...], b_ref[...])

    return _pallas(
        body, name=name, grid=(dil, length // ts),
        in_specs=[pl.BlockSpec((ts, m), lambda r, s: (s, r)), pl.BlockSpec((ts, n), lambda r, s: (s, r))],
        out_specs=pl.BlockSpec((m, n), lambda r, s: (0, 0)),
        out_shape=jax.ShapeDtypeStruct((m, n), F32),
        compiler_params=_params(("arbitrary", "arbitrary"), 40),
    )(_in_hbm(a), _in_hbm(b))


def _mix_bwd(dy, ya, yg, mg, rest, w_out, w_ba, w_bg, w_sp, b_col, ln_g, ln_b, rider=None):
    seq = dy.shape[0]
    tm = 256

    def body(dy_ref, ya_ref, yg_ref, mg_ref, up_ref, zp_ref, gap_ref, gbp_ref, wout_ref, wba_ref, wbg_ref,
             wsp_ref, bcol_ref, lg_ref, lb_ref,
             dya0, dya1, dya2, dpr_ref, dwout_ref, dwba_ref, dwbg_ref, dwsp_ref, dbb_ref, dlg_ref, dlb_ref,
             dzln_s, du_s, slab):
        @pl.when(pl.program_id(0) == 0)
        def _():
            for ref in (dwout_ref, dwba_ref, dwbg_ref, dwsp_ref, dbb_ref, dlg_ref, dlb_ref):
                ref[...] = jnp.zeros(ref.shape, F32)

        dyv = dy_ref[...]
        dm = _dot_nt(dyv, wout_ref[...])
        dwout_ref[...] += _dot_tn(mg_ref[...], dyv)
        yab = ya_ref[...].astype(BF16)
        ygb = yg_ref[...]
        a = _dot(yab, wba_ref[...])
        bm = _dot(ygb, wbg_ref[...])
        ga = jax.nn.sigmoid(gap_ref[...].astype(F32))
        gb = jax.nn.sigmoid(gbp_ref[...].astype(F32))
        dpr_ref[:, 2 * GMLP_W:2 * GMLP_W + D_MODEL] = (dm * a * (ga * (1.0 - ga))).astype(BF16)
        dpr_ref[:, 2 * GMLP_W + D_MODEL:REST_W] = (dm * bm * (gb * (1.0 - gb))).astype(BF16)
        da = (dm * ga).astype(BF16)
        db = (dm * gb).astype(BF16)
        dwba = _dot_tn(yab, da)
        dwbg = _dot_tn(ygb, db)
        shard_w = D_MODEL // N_CHIPS
        for j in range(N_CHIPS):
            dwba_ref[j] += dwba[:, j * shard_w:(j + 1) * shard_w]
            dwbg_ref[j] += dwbg[:, j * shard_w:(j + 1) * shard_w]
        dya = _dot_nt(da, wba_ref[...])
        for dya_ref, d in zip((dya0, dya1, dya2), DILATIONS):
            _put_residue(slab, dya, dya_ref, d, GROUP_W, 0)
        dyg = _dot_nt(db, wbg_ref[...])

        zp = zp_ref[...].astype(F32)
        zhat, rstd = _layernorm_stats(_gelu(zp))
        lg = lg_ref[...]
        zln = (zhat * lg + lb_ref[...]).astype(BF16)
        up = up_ref[...].astype(F32)
        u = _gelu(up)
        tril = _tril_mask()
        for g in range(GMLP_GROUPS):
            wm = jnp.where(tril, wsp_ref[g], 0.0).astype(BF16)
            cols = slice(g * CHUNK, (g + 1) * CHUNK)
            for c in range(tm // CHUNK):
                rows = slice(c * CHUNK, (c + 1) * CHUNK)
                zb = zln[rows, cols]
                sz = _dot(wm, zb) + bcol_ref[g]
                dyg_cg = dyg[rows, cols]
                du_s[rows, cols] = dyg_cg * sz
                dsz = dyg_cg * u[rows, cols]
                dszb = dsz.astype(BF16)
                dbb_ref[g] += jnp.broadcast_to(jnp.sum(dsz, axis=-1, keepdims=True), (CHUNK, CHUNK))
                dwsp_ref[g] += jnp.where(tril, _dot_nt(dszb, zb), 0.0)
                dzln_s[rows, cols] = _dot_tn(wm, dszb)
        dzln = dzln_s[...]
        dlg_ref[...] += jnp.sum(dzln * zhat, axis=0, keepdims=True)
        dlb_ref[...] += jnp.sum(dzln, axis=0, keepdims=True)
        dzh = dzln * lg
        dz = rstd * (dzh - jnp.mean(dzh, axis=-1, keepdims=True) - zhat * jnp.mean(dzh * zhat, axis=-1, keepdims=True))
        dpr_ref[:, GMLP_W:2 * GMLP_W] = (dz * _gelu_grad(zp)).astype(BF16)
        dpr_ref[:, 0:GMLP_W] = (du_s[...] * _gelu_grad(up)).astype(BF16)

    tok = lambda w: pl.BlockSpec((tm, w), lambda i: (i, 0))
    full = lambda *s: pl.BlockSpec(s, lambda i: (0,) * len(s))
    return _call(
        body, name="mix_bwd", grid=(seq // tm,),
        in_specs=[tok(D_MODEL), tok(GROUP_W), tok(GMLP_W), tok(D_MODEL),
                  pl.BlockSpec((tm, GMLP_W), lambda i: (i, 0)), pl.BlockSpec((tm, GMLP_W), lambda i: (i, 1)),
                  pl.BlockSpec((tm, D_MODEL), lambda i: (i, 1)), pl.BlockSpec((tm, D_MODEL), lambda i: (i, 2)),
                  full(D_MODEL, D_MODEL), full(GROUP_W, D_MODEL), full(GMLP_W, D_MODEL),
                  full(GMLP_GROUPS, CHUNK, CHUNK), full(GMLP_GROUPS, CHUNK, 1), full(1, GMLP_W), full(1, GMLP_W)],
        out_specs=[pl.BlockSpec((tm // d, d * GROUP_W), lambda i: (i, 0)) for d in DILATIONS]
        + [tok(REST_W), full(D_MODEL, D_MODEL), full(N_CHIPS, GROUP_W, D_MODEL // N_CHIPS),
           full(N_CHIPS, GMLP_W, D_MODEL // N_CHIPS),
           full(GMLP_GROUPS, CHUNK, CHUNK), full(GMLP_GROUPS, CHUNK, CHUNK), full(1, GMLP_W), full(1, GMLP_W)],
        out_shape=[jax.ShapeDtypeStruct((seq // d, d * GROUP_W), F32) for d in DILATIONS]
        + [jax.ShapeDtypeStruct((seq, REST_W), BF16),
           jax.ShapeDtypeStruct((D_MODEL, D_MODEL), F32), jax.ShapeDtypeStruct((N_CHIPS, GROUP_W, D_MODEL // N_CHIPS), F32),
           jax.ShapeDtypeStruct((N_CHIPS, GMLP_W, D_MODEL // N_CHIPS), F32),
           jax.ShapeDtypeStruct((GMLP_GROUPS, CHUNK, CHUNK), F32),
           jax.ShapeDtypeStruct((GMLP_GROUPS, CHUNK, CHUNK), F32), jax.ShapeDtypeStruct((1, GMLP_W), F32),
           jax.ShapeDtypeStruct((1, GMLP_W), F32)],
        scratch_shapes=[pltpu.VMEM((tm, GMLP_W), F32), pltpu.VMEM((tm, GMLP_W), F32),
                        pltpu.VMEM((GROUP_W // LANES, tm, LANES), F32)],
        params=_params(("arbitrary",), 56),
        args=(dy, ya, yg, mg, rest, rest, rest, rest, w_out, w_ba, w_bg, w_sp, b_col, ln_g, ln_b), rider=rider)


IN_PROJ_BWD_TM = 512


def _in_proj_bwd(dqkv, drest, w_in, x, dx1, g0, so_far, span, rider=None):
    seq = x.shape[0]
    tm = IN_PROJ_BWD_TM
    off, steps = span
    gx_so_far, dg_so_far = so_far

    def body(d0, d1, d2, dr_ref, w_ref, x_ref, dx1_ref, g_ref, dg_in_ref, gx_in_ref, gx_ref, dg_ref, slab):
        @pl.when(pl.program_id(0) == 0)
        def _():
            dg_ref[...] = dg_in_ref[...]

        dh = _dot(dr_ref[...], w_ref[QKV_W:, :])
        for g, (d_ref, dil) in enumerate(zip((d0, d1, d2), DILATIONS)):
            piece = d_ref[...] if dil == 1 else _get_tokens(slab, d_ref, dil, 3 * GROUP_W, 0, 3 * GROUP_W).astype(BF16)
            for section, (lo, hi) in enumerate(_qkv_columns(g)):
                dh = dh + _dot(piece[:, section * GROUP_W:(section + 1) * GROUP_W], w_ref[lo:hi, :])
        dres, dg = _rmsnorm_bwd(dh, x_ref[...], g_ref[...])
        gx_ref[...] = dx1_ref[...] + dres
        dg_ref[...] += dg

    tok = lambda w: pl.BlockSpec((tm, w), lambda i: (i + off, 0))
    full = lambda *s: pl.BlockSpec(s, lambda i: (0,) * len(s))
    in_specs = ([pl.BlockSpec((tm // d, d * 3 * GROUP_W), lambda i: (i + off, 0)) for d in DILATIONS] + [tok(REST_W)]
                + [_resident((IN_W, D_MODEL))]
                + [tok(D_MODEL), tok(D_MODEL), full(1, D_MODEL), full(1, D_MODEL), HBM_SPEC])
    return _call(
        body, name=f"in_proj_bwd_{off}", grid=(steps,), in_specs=in_specs,
        out_specs=[tok(D_MODEL), full(1, D_MODEL)],
        out_shape=[jax.ShapeDtypeStruct((seq, D_MODEL), F32), jax.ShapeDtypeStruct((1, D_MODEL), F32)],
        scratch_shapes=[pltpu.VMEM((3 * GROUP_W // LANES, tm, LANES), F32)],
        params=_params(("arbitrary",), 48), args=(*dqkv, drest, w_in, x, dx1, g0, dg_so_far, gx_so_far),
        rider=rider, aliases={len(in_specs) - 1: 0})


def _adamw(w, g, m, v, name):
    rows, cols = w.shape
    tr = _row_tile(rows) if rows % 16 == 0 else rows
    c1 = 1.0 - ADAM_B1 ** ADAM_STEP
    c2 = 1.0 - ADAM_B2 ** ADAM_STEP

    def body(w_ref, g_ref, m_ref, v_ref, go_ref, d_ref, nm_ref, nv_ref):
        gv = g_ref[...]
        go_ref[...] = gv
        nm = ADAM_B1 * m_ref[...] + (1.0 - ADAM_B1) * gv
        nv = ADAM_B2 * v_ref[...] + (1.0 - ADAM_B2) * (gv * gv)
        d_ref[...] = -ADAM_LR * ((nm / c1) / (jnp.sqrt(nv / c2) + ADAM_EPS) + ADAM_WD * w_ref[...])
        nm_ref[...] = nm
        nv_ref[...] = nv

    spec = pl.BlockSpec((tr, cols), lambda i: (i, 0))
    return _pallas(
        body, name=name, grid=(rows // tr,),
        in_specs=[spec] * 4, out_specs=[spec] * 4,
        out_shape=[jax.ShapeDtypeStruct((rows, cols), F32)] * 4,
        compiler_params=_params(("arbitrary",), 32, small=True),
    )(w, g, m, v)


def _place():
    x, y, c = lax.axis_index("x"), lax.axis_index("y"), lax.axis_index("c")
    chips = [(1 - x, y), (x, 1 - y), (1 - x, 1 - y)]
    return x, y, c, chips


class _Exchange:
    def __init__(self, inputs, out_shapes, n_sems, start, finish, aliases=None):
        self.inputs, self.out_shapes, self.n_sems = list(inputs), list(out_shapes), n_sems
        self.start, self.finish, self.aliases = start, finish, dict(aliases or {})

    def scratch(self):
        return [pltpu.SemaphoreType.DMA((self.n_sems,)), pltpu.SemaphoreType.DMA((self.n_sems,))]


def _together(*parts):
    ins = [len(p.inputs) for p in parts]
    outs = [len(p.out_shapes) for p in parts]

    def split(refs, counts):
        pos, pieces = 0, []
        for cnt in counts:
            pieces.append(refs[pos:pos + cnt])
            pos += cnt
        return pieces

    def run(which):
        def go(in_refs, out_refs, *sems):
            for k, (p, i, o) in enumerate(zip(parts, split(in_refs, ins), split(out_refs, outs))):
                getattr(p, which)(i, o, sems[2 * k], sems[2 * k + 1])
        return go

    both = _Exchange([a for p in parts for a in p.inputs], [s for p in parts for s in p.out_shapes], 0, run("start"),
                     run("finish"))
    both.aliases = {sum(ins[:k]) + i: sum(outs[:k]) + o for k, p in enumerate(parts) for i, o in p.aliases.items()}
    both.scratch = lambda: [s for p in parts for s in p.scratch()]
    return both


def _run_exchange(ex, name):
    n_in, n_out = len(ex.inputs), len(ex.out_shapes)

    def body(*refs):
        ins, outs, sems = refs[:n_in], refs[n_in:n_in + n_out], refs[n_in + n_out:]
        ex.start(ins, outs, *sems)
        ex.finish(ins, outs, *sems)

    return _pallas(
        body, name=name, in_specs=[HBM_SPEC] * n_in, out_specs=[HBM_SPEC] * n_out, out_shape=ex.out_shapes,
        scratch_shapes=ex.scratch(), input_output_aliases=ex.aliases,
    )(*ex.inputs)


def _call(body, *, name, grid, in_specs, out_specs, out_shape, scratch_shapes, params, args, rider=None, aliases=None):
    in_specs, out_specs, out_shape, scratch_shapes = list(in_specs), list(out_specs), list(out_shape), list(scratch_shapes)
    aliases = dict(aliases or {})
    args = [_in_hbm(a) for a in args]
    if rider is None:
        outs = _pallas(body, name=name, grid=grid, in_specs=in_specs, out_specs=out_specs, out_shape=out_shape,
                              scratch_shapes=scratch_shapes, input_output_aliases=aliases, compiler_params=params)(*args)
        return list(outs), []
    n_in, n_out, n_scr = len(in_specs), len(out_specs), len(scratch_shapes)
    r_in, r_out = len(rider.inputs), len(rider.out_shapes)

    def wrapped(*refs):
        ins, r_ins = refs[:n_in], refs[n_in:n_in + r_in]
        pos = n_in + r_in
        outs, r_outs = refs[pos:pos + n_out], refs[pos + n_out:pos + n_out + r_out]
        pos += n_out + r_out
        scr, sems = refs[pos:pos + n_scr], refs[pos + n_scr:]
        ids = [pl.program_id(k) for k in range(len(grid))]
        first, last = ids[0] == 0, ids[0] == grid[0] - 1
        for k in range(1, len(grid)):
            first, last = first & (ids[k] == 0), last & (ids[k] == grid[k] - 1)

        @pl.when(first)
        def _():
            rider.start(r_ins, r_outs, *sems)

        body(*ins, *outs, *scr)

        @pl.when(last)
        def _():
            rider.finish(r_ins, r_outs, *sems)

    outs = _pallas(
        wrapped, name=name, grid=grid, in_specs=in_specs + [HBM_SPEC] * r_in, out_specs=out_specs + [HBM_SPEC] * r_out,
        out_shape=out_shape + rider.out_shapes, scratch_shapes=scratch_shapes + rider.scratch(),
        input_output_aliases={**aliases, **{n_in + i: n_out + o for i, o in rider.aliases.items()}}, compiler_params=params,
    )(*args, *rider.inputs)
    return list(outs[:n_out]), list(outs[n_out:])


def _stage_weights(shards):
    n = len(shards)

    def body(*refs):
        ins, outs, stages, sems = refs[:n], refs[n:2 * n], refs[2 * n:3 * n], refs[3 * n]
        x, y, _, _ = _place()
        copies = []
        for t in range(n):
            stages[t][...] = ins[t][...].astype(BF16)
            copies.append(pltpu.make_async_copy(stages[t], outs[t].at[2 * x + y], sems.at[t]))
            copies[-1].start()
        for cp in copies:
            cp.wait()

    assert sum(s.size * 6 for s in shards) <= (CALL_VMEM_MIB - 8) * MIB
    return _pallas(
        body, name="stage_weights", in_specs=[VMEM_SPEC] * n, out_specs=[HBM_SPEC] * n,
        out_shape=[jax.ShapeDtypeStruct((N_CHIPS,) + s.shape, BF16) for s in shards],
        scratch_shapes=[pltpu.VMEM(s.shape, BF16) for s in shards] + [pltpu.SemaphoreType.DMA((n,))],
        compiler_params=pltpu.CompilerParams(vmem_limit_bytes=SMALL_VMEM_MIB * MIB),
    )(*shards)


def _gather(buffers, stage="both", part=(0, 1)):
    n = len(buffers)
    halves = [b.shape[1] // part[1] // 2 for b in buffers]

    def half_of(outs, t, chip, which):
        return outs[t].at[chip, pl.ds((2 * part[0] + which) * halves[t], halves[t]), :]

    def copy(outs, sems, t, k, chip, which, to):
        rows = half_of(outs, t, chip, which)
        return pltpu.make_async_remote_copy(src_ref=rows, dst_ref=rows, send_sem=sems[0].at[6 * t + k],
                                            recv_sem=sems[1].at[6 * t + k], device_id=to, device_id_type=MESH)

    def to_chips(outs, sems, what):
        x, y, c, chips = _place()
        for t in range(n):
            for j, (px, py) in enumerate(chips):
                if what == "start":
                    copy(outs, sems, t, j, 2 * x + y, c, (px, py, c)).start()
                else:
                    copy(outs, sems, t, j, 2 * px + py, c, (px, py, c)).wait_recv()
                    copy(outs, sems, t, j, 2 * x + y, c, (px, py, c)).wait_send()

    def to_sibling(outs, sems, what):
        x, y, c, chips = _place()
        for t in range(n):
            for j, (px, py) in enumerate(chips):
                if what == "start":
                    copy(outs, sems, t, 3 + j, 2 * px + py, c, (x, y, 1 - c)).start()
                else:
                    copy(outs, sems, t, 3 + j, 2 * px + py, 1 - c, (x, y, 1 - c)).wait_recv()
                    copy(outs, sems, t, 3 + j, 2 * px + py, c, (x, y, 1 - c)).wait_send()

    def start(ins, outs, *sems):
        (to_sibling if stage == "pair" else to_chips)(outs, sems, "start")

    def finish(ins, outs, *sems):
        if stage == "both":
            x, y, c, chips = _place()
            for j, (px, py) in enumerate(chips):
                for t in range(n):
                    copy(outs, sems, t, j, 2 * px + py, c, (px, py, c)).wait_recv()
                    copy(outs, sems, t, 3 + j, 2 * px + py, c, (x, y, 1 - c)).start()
            for j, (px, py) in enumerate(chips):
                for t in range(n):
                    copy(outs, sems, t, j, 2 * x + y, c, (px, py, c)).wait_send()
            to_sibling(outs, sems, "finish")
        elif stage == "chips":
            to_chips(outs, sems, "finish")
        else:
            to_sibling(outs, sems, "finish")

    return _Exchange(buffers, [jax.ShapeDtypeStruct(b.shape, b.dtype) for b in buffers], 6 * n, start, finish,
                     aliases={t: t for t in range(n)})


def _pair_exchange(grads):
    n = len(grads)
    halves = [g.shape[1] // 2 for g in grads]

    def copies(ins, outs, send_sems, recv_sems):
        x, y, c, _ = _place()
        return [pltpu.make_async_remote_copy(
            src_ref=ins[t].at[:, pl.ds((1 - c) * halves[t], halves[t]), :], dst_ref=outs[t],
            send_sem=send_sems.at[t], recv_sem=recv_sems.at[t], device_id=(x, y, 1 - c), device_id_type=MESH)
            for t in range(n)]

    def start(*refs):
        for cp in copies(*refs):
            cp.start()

    def finish(*refs):
        for cp in copies(*refs):
            cp.wait()

    return _Exchange(grads, [jax.ShapeDtypeStruct((N_CHIPS, h, g.shape[2]), F32) for g, h in zip(grads, halves)], n,
                     start, finish)


def _row_tile(rows):
    return max(t for t in range(16, 257, 16) if rows % t == 0)


def _pair_add(grad, other, place, name):
    _, rows, cols = grad.shape
    rh = rows // 2
    tr = _row_tile(rh)
    nb = rh // tr

    def body(p_ref, g_ref, a_ref, wire_ref, own_ref):
        s = g_ref[...] + a_ref[...]
        wire_ref[...] = s.astype(BF16)

        @pl.when(pl.program_id(1) == p_ref[1])
        def _():
            own_ref[...] = s

    blk = (None, tr, cols)
    return _pallas(
        body, name=name,
        grid_spec=pltpu.PrefetchScalarGridSpec(
            num_scalar_prefetch=1, grid=(nb, N_CHIPS),
            in_specs=[pl.BlockSpec(blk, lambda i, j, p: (j, p[0] * nb + i, 0)), pl.BlockSpec(blk, lambda i, j, p: (j, i, 0))],
            out_specs=[pl.BlockSpec(blk, lambda i, j, p: (j, i, 0)), pl.BlockSpec((tr, cols), lambda i, j, p: (i, 0))]),
        out_shape=[jax.ShapeDtypeStruct((N_CHIPS, rh, cols), BF16), jax.ShapeDtypeStruct((rh, cols), F32)],
        compiler_params=_params(("arbitrary", "arbitrary"), 32, small=True),
    )(place, grad, other)


def _chip_exchange(wires):
    n = len(wires)

    def copies(ins, outs, send_sems, recv_sems):
        x, y, c, chips = _place()
        return [pltpu.make_async_remote_copy(
            src_ref=ins[t].at[2 * px + py], dst_ref=outs[t].at[j], send_sem=send_sems.at[3 * t + j],
            recv_sem=recv_sems.at[3 * t + j], device_id=(px, py, c), device_id_type=MESH)
            for t in range(n) for j, (px, py) in enumerate(chips)]

    def start(*refs):
        for cp in copies(*refs):
            cp.start()

    def finish(*refs):
        for cp in copies(*refs):
            cp.wait()

    return _Exchange(wires, [jax.ShapeDtypeStruct((3,) + w.shape[1:], BF16) for w in wires], 3 * n, start, finish)


def _chip_add(own, arrived, place, name):
    rh, cols = own.shape
    tr = _row_tile(rh)
    nb = rh // tr

    def body(p_ref, s_ref, b0, b1, b2, o_ref):
        o_ref[...] = ((s_ref[...] + b0[...].astype(F32)) + b1[...].astype(F32)) + b2[...].astype(F32)

    blk = (None, tr, cols)
    return _pallas(
        body, name=name,
        grid_spec=pltpu.PrefetchScalarGridSpec(
            num_scalar_prefetch=1, grid=(nb,),
            in_specs=[pl.BlockSpec((tr, cols), lambda i, p: (i, 0)), pl.BlockSpec(blk, lambda i, p: (0, i, 0)),
                      pl.BlockSpec(blk, lambda i, p: (1, i, 0)), pl.BlockSpec(blk, lambda i, p: (2, i, 0))],
            out_specs=pl.BlockSpec((tr, cols), lambda i, p: (p[0] * nb + i, 0))),
        out_shape=jax.ShapeDtypeStruct((2 * rh, cols), F32),
        compiler_params=_params(("arbitrary",), 32, small=True),
    )(place, own, arrived, arrived, arrived)


def _pair_share(halves):
    n = len(halves)
    rhs = [h.shape[0] // 2 for h in halves]

    def copy(outs, send_sems, recv_sems, t, which):
        x, y, c, _ = _place()
        rows = outs[t].at[pl.ds(which * rhs[t], rhs[t]), :]
        return pltpu.make_async_remote_copy(src_ref=rows, dst_ref=rows, send_sem=send_sems.at[t], recv_sem=recv_sems.at[t],
                                            device_id=(x, y, 1 - c), device_id_type=MESH)

    def start(ins, outs, send_sems, recv_sems):
        c = lax.axis_index("c")
        for t in range(n):
            copy(outs, send_sems, recv_sems, t, c).start()

    def finish(ins, outs, send_sems, recv_sems):
        c = lax.axis_index("c")
        for t in range(n):
            copy(outs, send_sems, recv_sems, t, c).wait_send()
            copy(outs, send_sems, recv_sems, t, 1 - c).wait_recv()

    return _Exchange(halves, [jax.ShapeDtypeStruct(h.shape, F32) for h in halves], n, start, finish,
                     aliases={t: t for t in range(n)})


class _GradReduction:
    def __init__(self, grads, place, tag):
        self.names, self.grads, self.place, self.tag = list(grads), grads, place, tag

    def pair_exchange(self):
        return _pair_exchange([self.grads[n] for n in self.names])

    def chip_exchange(self, others):
        sums = [_pair_add(self.grads[n], o, self.place, f"{self.tag}_pair_add_{n}") for n, o in zip(self.names, others)]
        self.owns = [own for _, own in sums]
        return _chip_exchange([wire for wire, _ in sums])

    def pair_share(self, arrived):
        return _pair_share([_chip_add(own, arr, self.place, f"{self.tag}_chip_add_{n}")
                            for n, own, arr in zip(self.names, self.owns, arrived)])

    def result(self, shared):
        return dict(zip(self.names, shared))


def _all_reduce_small(p):
    rows, lanes = p.shape
    half = rows // 2

    def body(p_ref, o_ref, sib, sums, send_sems, recv_sems):
        x, y, c, chips = _place()
        mine, sibling = 2 * x + y, (x, y, 1 - c)
        swap = pltpu.make_async_remote_copy(src_ref=p_ref, dst_ref=sib, send_sem=send_sems.at[0], recv_sem=recv_sems.at[0],
                                            device_id=sibling, device_id_type=MESH)
        swap.start()
        swap.wait()
        sums[mine] = p_ref[...] + sib[...]

        def copy(k, chip, which, to):
            part = sums.at[chip, pl.ds(which * half, half), :]
            return pltpu.make_async_remote_copy(src_ref=part, dst_ref=part, send_sem=send_sems.at[k], recv_sem=recv_sems.at[k],
                                                device_id=to, device_id_type=MESH)

        for j, (px, py) in enumerate(chips):
            copy(1 + j, mine, c, (px, py, c)).start()
        for j, (px, py) in enumerate(chips):
            copy(1 + j, 2 * px + py, c, (px, py, c)).wait_recv()
            copy(4 + j, 2 * px + py, c, sibling).start()
        for j, (px, py) in enumerate(chips):
            copy(4 + j, 2 * px + py, 1 - c, sibling).wait_recv()
        for j, (px, py) in enumerate(chips):
            copy(1 + j, mine, c, (px, py, c)).wait_send()
            copy(4 + j, 2 * px + py, c, sibling).wait_send()
        o_ref[...] = ((sums[0] + sums[1]) + sums[2]) + sums[3]

    return _pallas(
        body, name="small_all_reduce", in_specs=[VMEM_SPEC], out_specs=VMEM_SPEC,
        out_shape=jax.ShapeDtypeStruct((rows, lanes), F32),
        scratch_shapes=[pltpu.VMEM((rows, lanes), F32), pltpu.VMEM((N_CHIPS, rows, lanes), F32),
                        pltpu.SemaphoreType.DMA((7,)), pltpu.SemaphoreType.DMA((7,))],
        compiler_params=pltpu.CompilerParams(vmem_limit_bytes=SMALL_VMEM_MIB * MIB),
    )(p)


BIG = ("w_in", "w_branch_attn", "w_branch_gmlp", "w_out", "w_mlp_in", "w_mlp_out")
COLUMN_SHARDED = ("w_branch_attn", "w_branch_gmlp", "w_mlp_in")
SMALL = ("norm_pre_mix", "w_spatial", "b_spatial", "ln_v_gain", "ln_v_bias", "norm_post_mix", "norm_pre_mlp", "norm_post_mlp")
ORDER = ("norm_pre_mix", "w_in", "w_spatial", "b_spatial", "ln_v_gain", "ln_v_bias", "w_branch_attn", "w_branch_gmlp",
         "w_out", "norm_post_mix", "norm_pre_mlp", "w_mlp_in", "w_mlp_out", "norm_post_mlp")


def _full_weight(name, gathered):
    if name in COLUMN_SHARDED:
        return jnp.transpose(gathered, (1, 0, 2)).reshape(gathered.shape[1], -1)
    return gathered.reshape(-1, gathered.shape[2])


def _rows8(a):
    a = a.reshape(-1, 128)
    pad = (-a.shape[0]) % 8
    return jnp.pad(a, ((0, pad), (0, 0))) if pad else a


def _qkv_columns(group):
    return [(sec * ATTN_W + group * GROUP_W, sec * ATTN_W + (group + 1) * GROUP_W) for sec in range(3)]


def _device_step(x, target, small, shards, place):
    seq = x.shape[0]
    g0, g1, g2, g3 = small["norm_pre_mix"], small["norm_post_mix"], small["norm_pre_mlp"], small["norm_post_mlp"]
    w_sp = small["w_spatial"]
    b_col = small["b_spatial"].reshape(GMLP_GROUPS, CHUNK, 1)
    ln_g, ln_b = small["ln_v_gain"], small["ln_v_bias"]

    staged = _stage_weights(shards)
    h, tables, (w_in,) = _prepare(x, g0, rider=_gather(staged[:1]))
    w_in = _full_weight("w_in", w_in)
    (*qkv, rest), landed = _in_proj(h[0], w_in, *tables[1], rider=_gather(staged[1:], "chips"))

    o_l, gathered = _attn_fwd(qkv[0], DILATIONS[0], rider=_gather(landed, "pair"))
    full = {n: _full_weight(n, gw) for n, gw in zip(BIG[1:], gathered)}
    for g in range(1, N_GROUPS):
        o_l.extend(_attn_fwd(qkv[g], DILATIONS[g])[0])
    (*ya_l, yg, mg, y, x1), _ = _mix_fwd(o_l, rest, x, w_sp, b_col, ln_g, ln_b, full["w_branch_attn"],
                                        full["w_branch_gmlp"], full["w_out"], g1)
    ya, lse = ya_l[0::2], ya_l[1::2]
    h2, a, dy2, dout, loss8, dg3 = _mlp_fwd(x1, g2, g3, full["w_mlp_in"], full["w_mlp_out"], target)
    d_wmo, _ = _tn_matmul(a, dy2, "grad_w_mlp_out", 1024, 1024, square_a=True)
    mlp_out = _GradReduction({"w_mlp_out": d_wmo.reshape(N_CHIPS, D_FF // N_CHIPS, D_MODEL)}, place, "mlp_out")
    (dap, dx1, dy, dg2, dg1), riding = _mlp_bwd(dy2, a, full["w_mlp_out"], full["w_mlp_in"], dout, x1, y, g2, g1,
                                                 rider=mlp_out.pair_exchange())
    d_wmi, riding = _tn_matmul(h2, dap, "grad_w_mlp_in", 1024, 1024, column_shards=True,
                               rider=mlp_out.chip_exchange(riding))
    mlp_in = _GradReduction({"w_mlp_in": d_wmi}, place, "mlp_in")
    (*dya, drest, d_wout, d_wba, d_wbg, d_wsp, d_bb, d_lg, d_lb), riding = _mix_bwd(
        dy, ya[0], yg, mg, rest, full["w_out"], full["w_branch_attn"], full["w_branch_gmlp"], w_sp, b_col, ln_g, ln_b,
        rider=_together(mlp_out.pair_share(riding), mlp_in.pair_exchange()))
    reduced = mlp_out.result(riding[:1])
    mix = _GradReduction({"w_branch_attn": d_wba, "w_branch_gmlp": d_wbg,
                          "w_out": d_wout.reshape(N_CHIPS, D_MODEL // N_CHIPS, D_MODEL)}, place, "mix")
    attn = lambda g, rider: _attn_bwd(qkv[g], dya[g], ya[g], lse[g], *tables[DILATIONS[g]], DILATIONS[g], rider=rider)
    dqkv0, riding = attn(0, _together(mlp_in.chip_exchange(riding[1:]), mix.pair_exchange()))
    dqkv1, riding = attn(1, _together(mlp_in.pair_share(riding[:1]), mix.chip_exchange(riding[1:])))
    reduced.update(mlp_in.result(riding[:1]))
    dqkv2, riding = attn(2, mix.pair_share(riding[1:]))
    reduced.update(mix.result(riding))
    dqkv = [dqkv0, dqkv1, dqkv2]

    d_qkv = [_tn_matmul_residue(dqkv[g], h[g], dil, f"grad_w_in_qkv{g}") for g, dil in enumerate(DILATIONS)]
    d_rest, _ = _tn_matmul(drest, h[0], "grad_w_in_rest", 1024, 1024)
    d_win = jnp.concatenate([d_qkv[g][s * GROUP_W:(s + 1) * GROUP_W] for s in range(3) for g in range(N_GROUPS)]
                            + [d_rest], axis=0)
    first = _GradReduction({"w_in": d_win.reshape(N_CHIPS, IN_W // N_CHIPS, D_MODEL)}, place, "w_in")
    tiles = seq // IN_PROJ_BWD_TM
    so_far = (lax.empty((seq, D_MODEL), F32), jnp.zeros((1, D_MODEL), F32))
    in_bwd = lambda so_far, span, rider: _in_proj_bwd(dqkv, drest, w_in, x, dx1, g0, so_far, span, rider=rider)
    head = 3 * tiles // 8
    so_far, riding = in_bwd(so_far, (0, head), first.pair_exchange())
    (grad_x, dg0), riding = in_bwd(so_far, (head, tiles - head), first.chip_exchange(riding))
    reduced.update(first.result(_run_exchange(first.pair_share(riding), "w_in_pair_share")))
    little = {"norm_pre_mix": dg0, "w_spatial": d_wsp, "b_spatial": d_bb[:, :, 0], "ln_v_gain": d_lg, "ln_v_bias": d_lb,
              "norm_post_mix": dg1, "norm_pre_mlp": dg2, "norm_post_mlp": dg3}
    return loss8, grad_x, reduced, little


def kernel(x, norm_pre_mix, w_in, w_spatial, b_spatial, ln_v_gain, ln_v_bias, w_branch_attn, w_branch_gmlp, w_out, norm_post_mix, norm_pre_mlp, w_mlp_in, w_mlp_out, norm_post_mlp, loss_target, m_norm_pre_mix, m_w_in, m_w_spatial, m_b_spatial, m_ln_v_gain, m_ln_v_bias, m_w_branch_attn, m_w_branch_gmlp, m_w_out, m_norm_post_mix, m_norm_pre_mlp, m_w_mlp_in, m_w_mlp_out, m_norm_post_mlp, v_norm_pre_mix, v_w_in, v_w_spatial, v_b_spatial, v_ln_v_gain, v_ln_v_bias, v_w_branch_attn, v_w_branch_gmlp, v_w_out, v_norm_post_mix, v_norm_pre_mlp, v_w_mlp_in, v_w_mlp_out, v_norm_post_mlp):
    given = dict(norm_pre_mix=norm_pre_mix, w_in=w_in, w_spatial=w_spatial, b_spatial=b_spatial, ln_v_gain=ln_v_gain,
                 ln_v_bias=ln_v_bias, w_branch_attn=w_branch_attn, w_branch_gmlp=w_branch_gmlp, w_out=w_out,
                 norm_post_mix=norm_post_mix, norm_pre_mlp=norm_pre_mlp, w_mlp_in=w_mlp_in, w_mlp_out=w_mlp_out,
                 norm_post_mlp=norm_post_mlp)
    moments_m = dict(norm_pre_mix=m_norm_pre_mix, w_in=m_w_in, w_spatial=m_w_spatial, b_spatial=m_b_spatial,
                     ln_v_gain=m_ln_v_gain, ln_v_bias=m_ln_v_bias, w_branch_attn=m_w_branch_attn,
                     w_branch_gmlp=m_w_branch_gmlp, w_out=m_w_out, norm_post_mix=m_norm_post_mix,
                     norm_pre_mlp=m_norm_pre_mlp, w_mlp_in=m_w_mlp_in, w_mlp_out=m_w_mlp_out, norm_post_mlp=m_norm_post_mlp)
    moments_v = dict(norm_pre_mix=v_norm_pre_mix, w_in=v_w_in, w_spatial=v_w_spatial, b_spatial=v_b_spatial,
                     ln_v_gain=v_ln_v_gain, ln_v_bias=v_ln_v_bias, w_branch_attn=v_w_branch_attn,
                     w_branch_gmlp=v_w_branch_gmlp, w_out=v_w_out, norm_post_mix=v_norm_post_mix,
                     norm_pre_mlp=v_norm_pre_mlp, w_mlp_in=v_w_mlp_in, w_mlp_out=v_w_mlp_out, norm_post_mlp=v_norm_post_mlp)
    cx, cy, cc = lax.axis_index("x"), lax.axis_index("y"), lax.axis_index("c")

    shards = [given[n][0].T if n == "w_in" else given[n][0] for n in BIG]
    small = {n: given[n][0] if given[n].ndim > 2 else given[n] for n in SMALL}
    place = jnp.stack([cc, 2 * cx + cy]).astype(jnp.int32)
    loss8, grad_x, grad_shard, grads = _device_step(x[0], loss_target[0], small, shards, place)

    packed = jnp.concatenate([_rows8(grads[n]) for n in SMALL] + [loss8], axis=0)
    summed = _all_reduce_small(packed)
    loss = summed[packed.shape[0] - loss8.shape[0], 0]
    row = 0
    for n in SMALL:
        shape = given[n][0].shape
        cnt = -(-(given[n][0].size // 128) // 8) * 8
        grad_shard[n] = summed[row:row + given[n][0].size // 128].reshape(shape)
        row += cnt

    grad_out, deltas, new_m, new_v = {}, {}, {}, {}
    for n in ORDER:
        shape = given[n].shape
        if n == "w_in":
            outs = _adamw(given[n][0].T, grad_shard[n], moments_m[n][0].T, moments_v[n][0].T, "adamw_" + n)
            outs = [o.T for o in outs]
        else:
            two_d = (-1, shape[-1])
            outs = _adamw(given[n].reshape(two_d), grad_shard[n].reshape(two_d), moments_m[n].reshape(two_d),
                          moments_v[n].reshape(two_d), "adamw_" + n)
        grad_out[n], deltas[n], new_m[n], new_v[n] = [o.reshape(shape) for o in outs]
    return (loss, grad_x[None], *[grad_out[n] for n in ORDER], *[deltas[n] for n in ORDER], *[new_m[n] for n in ORDER],
            *[new_v[n] for n in ORDER])
```

```python
import math

import jax
import jax.numpy as jnp
from jax import lax
from jax.experimental import pallas as pl
from jax.experimental.pallas import tpu as pltpu

F32 = jnp.float32
BF16 = jnp.bfloat16
MESH = pl.DeviceIdType.MESH

D_MODEL = 1024
HEAD_DIM = 64
HEADS_PER_GROUP = 4
GROUP_W = HEADS_PER_GROUP * HEAD_DIM
DILATIONS = (1, 4, 16)
N_GROUPS = len(DILATIONS)
ATTN_W = N_GROUPS * GROUP_W
QKV_W = 3 * ATTN_W
GMLP_W = 512
GMLP_GROUPS = 4
CHUNK = 128
REST_W = 2 * GMLP_W + 2 * D_MODEL
IN_W = QKV_W + REST_W
D_FF = 4096
QBLK = 128
ROPE_THETA = 10000.0
EPS = 1e-6
NEG = -1e30
SCALE = HEAD_DIM ** -0.5
N_CHIPS = 4

ADAM_LR = 0.001
ADAM_B1 = 0.9
ADAM_B2 = 0.999
ADAM_EPS = 1e-08
ADAM_WD = 0.01
ADAM_STEP = 10

MIB = 1024 * 1024
HBM_SPEC = pl.BlockSpec(memory_space=pltpu.HBM)
VMEM_SPEC = pl.BlockSpec(memory_space=pltpu.VMEM)


MLP_FWD_TM = 512
MLP_TM = 512


CALL_VMEM_MIB = 56
SMALL_VMEM_MIB = 32


def _params(semantics, vmem_mib, small=False):
    assert vmem_mib <= CALL_VMEM_MIB
    return pltpu.CompilerParams(dimension_semantics=semantics,
                                vmem_limit_bytes=(SMALL_VMEM_MIB if small else CALL_VMEM_MIB) * MIB)


def _in_hbm(a):
    return pltpu.with_memory_space_constraint(a, pltpu.HBM) if a.size * a.dtype.itemsize >= MIB else a


def _pallas(body, **kwargs):
    return pl.pallas_call(body, **kwargs)


def _resident(shape):
    return pl.BlockSpec(shape, lambda *_: (0,) * len(shape), pipeline_mode=pl.Buffered(1))


def _dot(a, b):
    return jnp.dot(a, b, preferred_element_type=F32)


def _dot_nt(a, b):
    return lax.dot_general(a, b, (((1,), (1,)), ((), ())), preferred_element_type=F32)


def _dot_tn(a, b):
    return lax.dot_general(a, b, (((0,), (0,)), ((), ())), preferred_element_type=F32)


_GELU_C = math.sqrt(2.0 / math.pi)


def _gelu(x):
    return x * (0.5 * (1.0 + jnp.tanh(_GELU_C * (x + 0.044715 * (x * x * x)))))


def _gelu_grad(x):
    t = jnp.tanh(_GELU_C * (x + 0.044715 * (x * x * x)))
    return 0.5 * (1.0 + t) + 0.5 * x * (1.0 - t * t) * (_GELU_C * (1.0 + 3.0 * 0.044715 * (x * x)))


def _rsqrt_ms(v):
    return lax.rsqrt(jnp.mean(v * v, axis=-1, keepdims=True) + EPS)


def _rmsnorm_bwd(dn, src, gain):
    r = _rsqrt_ms(src)
    t = gain * dn
    dgain = jnp.sum(dn * (src * r), axis=0, keepdims=True)
    dsrc = r * t - src * ((r * r * r) * jnp.mean(t * src, axis=-1, keepdims=True))
    return dsrc, dgain


def _rot_half(v):
    w = v.shape[-1]
    lane = lax.broadcasted_iota(jnp.int32, v.shape, v.ndim - 1)
    return jnp.where((lane % HEAD_DIM) < HEAD_DIM // 2, pltpu.roll(v, w - HEAD_DIM // 2, v.ndim - 1),
                     pltpu.roll(v, HEAD_DIM // 2, v.ndim - 1))


def _head_masks(shape):
    lane = lax.broadcasted_iota(jnp.int32, shape, 1)
    return [(lane >= h * HEAD_DIM) & (lane < (h + 1) * HEAD_DIM) for h in range(HEADS_PER_GROUP)]


def _head_stack(block, hmask):
    zero = jnp.zeros((), block.dtype)
    return jnp.concatenate([jnp.where(hm, block, zero) for hm in hmask], axis=0)


LANES = 128


def _put_residue(slab, val, out_ref, dil, width, col0):
    tm, w = val.shape
    if dil == 1:
        out_ref[:, col0:col0 + w] = val.astype(out_ref.dtype)
        return
    for k in range(w // LANES):
        slab[k] = val[:, k * LANES:(k + 1) * LANES]
    for r in range(dil):
        for k in range(w // LANES):
            c = r * width + col0 + k * LANES
            out_ref[:, c:c + LANES] = slab[k, pl.ds(r, tm // dil, stride=dil), :].astype(out_ref.dtype)


def _get_tokens(slab, in_ref, dil, width, col0, w):
    if dil == 1:
        return in_ref[:, col0:col0 + w].astype(F32)
    rows = in_ref.shape[0]
    for r in range(dil):
        for k in range(w // LANES):
            c = r * width + col0 + k * LANES
            slab[k, pl.ds(r, rows, stride=dil), :] = in_ref[:, c:c + LANES].astype(F32)
    return jnp.concatenate([slab[k] for k in range(w // LANES)], axis=1)


def _prepare(x, g0, rider=None):
    seq = x.shape[0]
    half = HEAD_DIM // 2
    inv_freq = ROPE_THETA ** (-jnp.arange(half, dtype=F32) / half)
    freq = jnp.tile(inv_freq, LANES // half).reshape(1, LANES)
    tm = 256

    def body(x_ref, g_ref, f_ref, *refs):
        h_refs, tabs, slab = refs[:N_GROUPS], refs[N_GROUPS:3 * N_GROUPS], refs[-1]
        xv = x_ref[...]
        hf = (xv * _rsqrt_ms(xv)) * g_ref[...]
        for g, dil in enumerate(DILATIONS):
            _put_residue(slab, hf, h_refs[g], dil, D_MODEL, 0)
        row = lax.broadcasted_iota(jnp.int32, (tm, LANES), 0) + pl.program_id(0) * tm
        lane = lax.broadcasted_iota(jnp.int32, (tm, LANES), 1)
        ang = row.astype(F32) * f_ref[...]
        cos = jnp.cos(ang)
        sin = jnp.where((lane % HEAD_DIM) < half, -jnp.sin(ang), jnp.sin(ang))
        for i, dil in enumerate(DILATIONS):
            for tab, val in ((tabs[2 * i], cos), (tabs[2 * i + 1], sin)):
                slab[0] = val
                for r in range(dil):
                    piece = slab[0, pl.ds(r, tm // dil, stride=dil), :] if dil > 1 else val
                    for k in range(GROUP_W // LANES):
                        tab[:, r * GROUP_W + k * LANES:r * GROUP_W + (k + 1) * LANES] = piece

    outs, riding = _call(
        body, name="prepare", grid=(seq // tm,),
        in_specs=[pl.BlockSpec((tm, D_MODEL), lambda i: (i, 0)), pl.BlockSpec((1, D_MODEL), lambda i: (0, 0)),
                  pl.BlockSpec((1, LANES), lambda i: (0, 0))],
        out_specs=[pl.BlockSpec((tm // d, d * D_MODEL), lambda i: (i, 0)) for d in DILATIONS]
        + [pl.BlockSpec((tm // d, d * GROUP_W), lambda i: (i, 0)) for d in DILATIONS for _ in range(2)],
        out_shape=[jax.ShapeDtypeStruct((seq // d, d * D_MODEL), BF16) for d in DILATIONS]
        + [jax.ShapeDtypeStruct((seq // d, d * GROUP_W), F32) for d in DILATIONS for _ in range(2)],
        scratch_shapes=[pltpu.VMEM((D_MODEL // LANES, tm, LANES), F32)],
        params=_params(("arbitrary",), 32), args=(x, g0, freq), rider=rider)
    tabs = outs[N_GROUPS:]
    return outs[:N_GROUPS], {d: (tabs[2 * i], tabs[2 * i + 1]) for i, d in enumerate(DILATIONS)}, riding


def _in_proj(h, w_in, cos_t, sin_t, rider=None):
    seq = h.shape[0]
    tm, tn = 512, GROUP_W
    n_qk = 2 * ATTN_W // tn
    n_qkv = QKV_W // tn

    def body(h_ref, w_ref, cos_ref, sin_ref, *refs):
        qkv_refs, rest_ref, slab = refs[:N_GROUPS], refs[N_GROUPS], refs[-1]
        hb = h_ref[...]
        cos, sin = cos_ref[...], sin_ref[...]
        for j in range(IN_W // tn):
            p = _dot_nt(hb, w_ref[j * tn:(j + 1) * tn, :])
            if j < n_qkv:
                if j < n_qk:
                    p = p * cos + _rot_half(p) * sin
                section, g = divmod(j, N_GROUPS)
                _put_residue(slab, p, qkv_refs[g], DILATIONS[g], 3 * GROUP_W, section * GROUP_W)
            else:
                rest_ref[:, (j - n_qkv) * tn:(j - n_qkv + 1) * tn] = p.astype(BF16)

    return _call(
        body, name="in_proj", grid=(seq // tm,),
        in_specs=[pl.BlockSpec((tm, D_MODEL), lambda i: (i, 0)),
                  _resident((IN_W, D_MODEL)),
                  pl.BlockSpec((tm, GROUP_W), lambda i: (i, 0)),
                  pl.BlockSpec((tm, GROUP_W), lambda i: (i, 0))],
        out_specs=[pl.BlockSpec((tm // d, d * 3 * GROUP_W), lambda i: (i, 0)) for d in DILATIONS]
        + [pl.BlockSpec((tm, REST_W), lambda i: (i, 0))],
        out_shape=[jax.ShapeDtypeStruct((seq // d, d * 3 * GROUP_W), BF16) for d in DILATIONS]
        + [jax.ShapeDtypeStruct((seq, REST_W), BF16)],
        scratch_shapes=[pltpu.VMEM((GROUP_W // LANES, tm, LANES), F32)],
        params=_params(("arbitrary",), 48), args=(h, w_in, cos_t, sin_t), rider=rider)


def _band_masks():
    qi = lax.broadcasted_iota(jnp.int32, (QBLK, QBLK), 0)
    kj = lax.broadcasted_iota(jnp.int32, (QBLK, QBLK), 1)
    return kj <= qi, kj >= qi


def _attn_tile(length):
    return min(1024, length)


def _attn_fwd(qkv, dil, rider=None):
    length = qkv.shape[0]
    tq = _attn_tile(length)
    nsub = tq // QBLK
    nblk = length // tq

    def body(q_ref, k_ref, v_ref, kp_ref, vp_ref, o_ref, l_ref):
        n = pl.program_id(1)
        mask_c, mask_p0 = _band_masks()
        hmask = _head_masks((QBLK, GROUP_W))
        zero = jnp.zeros((), BF16)
        for b in range(nsub):
            rows = slice(b * QBLK, (b + 1) * QBLK)
            q = q_ref[rows, :]
            kc, vc = k_ref[rows, :], v_ref[rows, :]
            if b == 0:
                kp, vp = kp_ref[...], vp_ref[...]
                mask_p = mask_p0 & (n > 0)
            else:
                prow = slice((b - 1) * QBLK, b * QBLK)
                kp, vp = k_ref[prow, :], v_ref[prow, :]
                mask_p = mask_p0
            o_acc = jnp.zeros((QBLK, GROUP_W), F32)
            l_acc = jnp.zeros((QBLK, GROUP_W), F32)
            for h in range(HEADS_PER_GROUP):
                hm = hmask[h]
                sc = jnp.where(mask_c, _dot_nt(q, jnp.where(hm, kc, zero)) * SCALE, NEG)
                sp = jnp.where(mask_p, _dot_nt(q, jnp.where(hm, kp, zero)) * SCALE, NEG)
                m = jnp.maximum(jnp.max(sc, axis=-1, keepdims=True), jnp.max(sp, axis=-1, keepdims=True))
                pc, pp = jnp.exp(sc - m), jnp.exp(sp - m)
                den = jnp.sum(pc, axis=-1, keepdims=True) + jnp.sum(pp, axis=-1, keepdims=True)
                pv = _dot(pc.astype(BF16), jnp.where(hm, vc, zero)) + _dot(pp.astype(BF16), jnp.where(hm, vp, zero))
                o_acc = o_acc + pv / den
                l_acc = l_acc + jnp.where(hm, m + jnp.log(den), 0.0)
            o_ref[rows, :] = o_acc.astype(BF16)
            l_ref[rows, :] = l_acc

    cur = lambda sec: pl.BlockSpec((tq, GROUP_W), lambda r, n: (n, r * 3 + sec))
    prev = lambda sec: pl.BlockSpec((QBLK, GROUP_W), lambda r, n: (jnp.maximum(n * nsub - 1, 0), r * 3 + sec))
    return _call(
        body, name=f"attn_fwd_d{dil}", grid=(dil, nblk),
        in_specs=[cur(0), cur(1), cur(2), prev(1), prev(2)],
        out_specs=[pl.BlockSpec((tq, GROUP_W), lambda r, n: (n, r))] * 2,
        out_shape=[jax.ShapeDtypeStruct((length, dil * GROUP_W), BF16),
                   jax.ShapeDtypeStruct((length, dil * GROUP_W), F32)], scratch_shapes=[],
        params=_params(("arbitrary", "arbitrary"), 32), args=(qkv, qkv, qkv, qkv, qkv), rider=rider)


def _attn_bwd(qkv, dy, y, lse, cos_t, sin_t, dil, rider=None):
    length = qkv.shape[0]
    tq = _attn_tile(length)
    nsub = tq // QBLK
    nblk = length // tq

    def body(q_ref, k_ref, v_ref, kp_ref, vp_ref, qn_ref, dy_ref, y_ref, l_ref, dyn_ref, yn_ref, ln_ref,
             cos_ref, sin_ref, out_ref, dq_s, dk_s, dv_s):
        n = pl.program_id(1)
        mask_c, mask_p0 = _band_masks()
        hmask = _head_masks((QBLK, GROUP_W))
        sub = lambda ref, b: ref[b * QBLK:(b + 1) * QBLK, :]
        kbd = [_head_stack(kp_ref[...], hmask)] + [_head_stack(sub(k_ref, b), hmask) for b in range(nsub)]
        vbd = [_head_stack(vp_ref[...], hmask)] + [_head_stack(sub(v_ref, b), hmask) for b in range(nsub)]
        dq_s[...] = jnp.zeros(dq_s.shape, F32)

        def query_terms(q, dyv, yv, lv):
            prod = dyv * yv
            return dict(
                q=q, dy=dyv.astype(BF16), q_heads=[jnp.where(hm, q, jnp.zeros((), BF16)) for hm in hmask],
                dy_heads=[jnp.where(hm, dyv, 0.0).astype(BF16) for hm in hmask],
                delta=[jnp.sum(jnp.where(hm, prod, 0.0), axis=-1, keepdims=True) for hm in hmask],
                lse=[jnp.max(jnp.where(hm, lv, NEG), axis=-1, keepdims=True) for hm in hmask])

        queries = [query_terms(sub(q_ref, b), sub(dy_ref, b), sub(y_ref, b), sub(l_ref, b)) for b in range(nsub)]
        queries.append(query_terms(qn_ref[...], dyn_ref[...], yn_ref[...], ln_ref[...]))
        rows_of = lambda items: items[0] if len(items) == 1 else jnp.concatenate(items, axis=0)
        for kb in range(nsub + 1):
            seen = [(kb - 1, mask_c)] if kb >= 1 else []
            if kb == 0:
                seen.append((0, mask_p0 & (n > 0)))
            elif kb < nsub:
                seen.append((kb, mask_p0))
            else:
                seen.append((nsub, mask_p0 & (n < nblk - 1)))
            qs = [queries[b] for b, _ in seen]
            mask = rows_of([m for _, m in seen])
            s = _dot_nt(rows_of([t["q"] for t in qs]), kbd[kb]) * SCALE
            dp = _dot_nt(rows_of([t["dy"] for t in qs]), vbd[kb])
            ps, dss = [], []
            for h in range(HEADS_PER_GROUP):
                cols = slice(h * QBLK, (h + 1) * QBLK)
                p = jnp.exp(jnp.where(mask, s[:, cols] - rows_of([t["lse"][h] for t in qs]), NEG))
                ps.append(p.astype(BF16))
                dss.append((p * (dp[:, cols] - rows_of([t["delta"][h] for t in qs]))).astype(BF16))
            dq = _dot(jnp.concatenate(dss, axis=1), kbd[kb]) * SCALE
            for i, (b, _) in enumerate(seen):
                if b < nsub:
                    dq_s[b * QBLK:(b + 1) * QBLK, :] += dq[i * QBLK:(i + 1) * QBLK, :]
            if kb >= 1:
                krows = slice((kb - 1) * QBLK, kb * QBLK)
                head_rows = lambda key: jnp.concatenate([t[key][h] for h in range(HEADS_PER_GROUP) for t in qs], axis=0)
                dv_s[krows, :] = _dot_tn(jnp.concatenate(ps, axis=0), head_rows("dy_heads"))
                dk_s[krows, :] = _dot_tn(jnp.concatenate(dss, axis=0), head_rows("q_heads")) * SCALE
        cos, sin = cos_ref[...], sin_ref[...]
        dq, dk = dq_s[...], dk_s[...]
        out_ref[:, 0:GROUP_W] = (dq * cos - _rot_half(dq) * sin).astype(BF16)
        out_ref[:, GROUP_W:2 * GROUP_W] = (dk * cos - _rot_half(dk) * sin).astype(BF16)
        out_ref[:, 2 * GROUP_W:3 * GROUP_W] = dv_s[...].astype(BF16)

    cur = lambda sec: pl.BlockSpec((tq, GROUP_W), lambda r, n: (n, r * 3 + sec))
    prev = lambda sec: pl.BlockSpec((QBLK, GROUP_W), lambda r, n: (jnp.maximum(n * nsub - 1, 0), r * 3 + sec))
    nxt_q = pl.BlockSpec((QBLK, GROUP_W), lambda r, n: (jnp.minimum((n + 1) * nsub, nblk * nsub - 1), r * 3))
    tok = pl.BlockSpec((tq, GROUP_W), lambda r, n: (n, r))
    tok_next = pl.BlockSpec((QBLK, GROUP_W), lambda r, n: (jnp.minimum((n + 1) * nsub, nblk * nsub - 1), r))
    (out,), riding = _call(
        body, name=f"attn_bwd_d{dil}", grid=(dil, nblk),
        in_specs=[cur(0), cur(1), cur(2), prev(1), prev(2), nxt_q,
                  tok, tok, tok, tok_next, tok_next, tok_next, tok, tok],
        out_specs=[pl.BlockSpec((tq, 3 * GROUP_W), lambda r, n: (n, r))],
        out_shape=[jax.ShapeDtypeStruct((length, dil * 3 * GROUP_W), BF16)],
        scratch_shapes=[pltpu.VMEM((tq, GROUP_W), F32)] * 3,
        params=_params(("arbitrary", "arbitrary"), 32),
        args=(qkv, qkv, qkv, qkv, qkv, qkv, dy, y, lse, dy, y, lse, cos_t, sin_t), rider=rider)
    return out, riding


def _layernorm_stats(z):
    mu = jnp.mean(z, axis=-1, keepdims=True)
    zc = z - mu
    rstd = lax.rsqrt(jnp.mean(zc * zc, axis=-1, keepdims=True) + EPS)
    return zc * rstd, rstd


def _tril_mask():
    row = lax.broadcasted_iota(jnp.int32, (CHUNK, CHUNK), 0)
    col = lax.broadcasted_iota(jnp.int32, (CHUNK, CHUNK), 1)
    return col <= row


def _mix_fwd(o_l, rest, x, w_sp, b_col, ln_g, ln_b, w_ba, w_bg, w_out, g1, rider=None):
    seq = x.shape[0]
    tm = 512

    def body(o0, l0, o1, l1, o2, l2, up_ref, zp_ref, gap_ref, gbp_ref, x_ref, wsp_ref, bcol_ref, lg_ref, lb_ref,
             wba_ref, wbg_ref, wout_ref, g1_ref, ya0, lj0, ya1, lj1, ya2, lj2, yg_ref, mg_ref, y_ref, x1_ref, slab):
        outs = [_get_tokens(slab, o, d, GROUP_W, 0, GROUP_W) for o, d in zip((o0, o1, o2), DILATIONS)]
        lses = [_get_tokens(slab, l, d, GROUP_W, 0, GROUP_W) for l, d in zip((l0, l1, l2), DILATIONS)]
        m = jnp.maximum(jnp.maximum(lses[0], lses[1]), lses[2])
        es = [jnp.exp(l - m) for l in lses]
        tot = es[0] + es[1] + es[2]
        ya = (es[0] * outs[0] + es[1] * outs[1] + es[2] * outs[2]) / tot
        lj = m + jnp.log(tot)
        for ya_ref, lj_ref, d in zip((ya0, ya1, ya2), (lj0, lj1, lj2), DILATIONS):
            _put_residue(slab, ya, ya_ref, d, GROUP_W, 0)
            _put_residue(slab, lj, lj_ref, d, GROUP_W, 0)
        zhat, _ = _layernorm_stats(_gelu(zp_ref[...].astype(F32)))
        zln = (zhat * lg_ref[...] + lb_ref[...]).astype(BF16)
        u = _gelu(up_ref[...].astype(F32))
        tril = _tril_mask()
        for g in range(GMLP_GROUPS):
            wm = jnp.where(tril, wsp_ref[g], 0.0).astype(BF16)
            cols = slice(g * CHUNK, (g + 1) * CHUNK)
            for c in range(tm // CHUNK):
                rows = slice(c * CHUNK, (c + 1) * CHUNK)
                sz = _dot(wm, zln[rows, cols]) + bcol_ref[g]
                yg_ref[rows, cols] = (u[rows, cols] * sz).astype(BF16)
        a = _dot(ya.astype(BF16), wba_ref[...])
        bm = _dot(yg_ref[...], wbg_ref[...])
        merged = (jax.nn.sigmoid(gap_ref[...].astype(F32)) * a + jax.nn.sigmoid(gbp_ref[...].astype(F32)) * bm).astype(BF16)
        mg_ref[...] = merged
        yv = _dot(merged, wout_ref[...])
        y_ref[...] = yv.astype(BF16)
        x1_ref[...] = x_ref[...] + (yv * _rsqrt_ms(yv)) * g1_ref[...]

    tok = lambda w: pl.BlockSpec((tm, w), lambda i: (i, 0))
    res = lambda d: pl.BlockSpec((tm // d, d * GROUP_W), lambda i: (i, 0))
    full = lambda *s: pl.BlockSpec(s, lambda i: (0,) * len(s))
    res_specs = [res(d) for d in DILATIONS for _ in range(2)]
    return _call(
        body, name="mix_fwd", grid=(seq // tm,),
        in_specs=res_specs + [
            pl.BlockSpec((tm, GMLP_W), lambda i: (i, 0)), pl.BlockSpec((tm, GMLP_W), lambda i: (i, 1)),
            pl.BlockSpec((tm, D_MODEL), lambda i: (i, 1)), pl.BlockSpec((tm, D_MODEL), lambda i: (i, 2)),
            tok(D_MODEL), full(GMLP_GROUPS, CHUNK, CHUNK), full(GMLP_GROUPS, CHUNK, 1), full(1, GMLP_W), full(1, GMLP_W),
            full(GROUP_W, D_MODEL), full(GMLP_W, D_MODEL), full(D_MODEL, D_MODEL), full(1, D_MODEL)],
        out_specs=res_specs + [tok(GMLP_W), tok(D_MODEL), tok(D_MODEL), tok(D_MODEL)],
        out_shape=[jax.ShapeDtypeStruct((seq // d, d * GROUP_W), F32) for d in DILATIONS for _ in range(2)]
        + [jax.ShapeDtypeStruct((seq, GMLP_W), BF16), jax.ShapeDtypeStruct((seq, D_MODEL), BF16),
           jax.ShapeDtypeStruct((seq, D_MODEL), BF16), jax.ShapeDtypeStruct((seq, D_MODEL), F32)],
        scratch_shapes=[pltpu.VMEM((GROUP_W // LANES, tm, LANES), F32)],
        params=_params(("arbitrary",), 48),
        args=(*o_l, rest, rest, rest, rest, x, w_sp, b_col, ln_g, ln_b, w_ba, w_bg, w_out, g1), rider=rider)


def _mlp_fwd(x1, g2, g3, w_mi, w_mo, target):
    seq = x1.shape[0]
    tm, tf = MLP_FWD_TM, 512

    def body(x1_ref, g2_ref, g3_ref, wmi_ref, wmo_ref, t_ref, h2_ref, a_ref, dy2_ref, dout_ref, loss_ref, dg3_ref, sq_s):
        @pl.when(pl.program_id(0) == 0)
        def _():
            loss_ref[...] = jnp.zeros(loss_ref.shape, F32)
            dg3_ref[...] = jnp.zeros(dg3_ref.shape, F32)

        xv = x1_ref[...]
        hb = ((xv * _rsqrt_ms(xv)) * g2_ref[...]).astype(BF16)
        h2_ref[...] = hb
        for j in range(D_FF // tf):
            cols = slice(j * tf, (j + 1) * tf)
            a = jnp.maximum(_dot(hb, wmi_ref[:, cols]), 0.0)
            a_ref[:, cols] = a.astype(BF16)
            sq_s[:, cols] = (a * a).astype(BF16)
        y2 = _dot(sq_s[...], wmo_ref[...])
        r3 = _rsqrt_ms(y2)
        out = xv + (y2 * r3) * g3_ref[...]
        diff = out - t_ref[...]
        tile_loss = 0.5 * jnp.sum(jnp.mean(diff * diff, axis=-1, keepdims=True), axis=0, keepdims=True)
        loss_ref[...] += jnp.broadcast_to(tile_loss, loss_ref.shape)
        dout = diff * (1.0 / D_MODEL)
        dout_ref[...] = dout
        dy2, dg3 = _rmsnorm_bwd(dout, y2, g3_ref[...])
        dy2_ref[...] = dy2.astype(BF16)
        dg3_ref[...] += dg3

    tok = lambda w: pl.BlockSpec((tm, w), lambda i: (i, 0))
    vec = pl.BlockSpec((1, D_MODEL), lambda i: (0, 0))
    return _pallas(
        body, name="mlp_fwd", grid=(seq // tm,),
        in_specs=[tok(D_MODEL), vec, vec, _resident((D_MODEL, D_FF)), _resident((D_FF, D_MODEL)), tok(D_MODEL)],
        out_specs=[tok(D_MODEL), tok(D_FF), tok(D_MODEL), tok(D_MODEL), pl.BlockSpec((8, 128), lambda i: (0, 0)), vec],
        out_shape=[jax.ShapeDtypeStruct((seq, D_MODEL), BF16), jax.ShapeDtypeStruct((seq, D_FF), BF16),
                   jax.ShapeDtypeStruct((seq, D_MODEL), BF16), jax.ShapeDtypeStruct((seq, D_MODEL), F32),
                   jax.ShapeDtypeStruct((8, 128), F32), jax.ShapeDtypeStruct((1, D_MODEL), F32)],
        scratch_shapes=[pltpu.VMEM((tm, D_FF), BF16)],
        compiler_params=_params(("arbitrary",), 56),
    )(*map(_in_hbm, (x1, g2, g3, w_mi, w_mo, target)))


def _mlp_bwd(dy2, a, w_mo, w_mi, dout, x1, y, g2, g1, rider=None):
    seq = x1.shape[0]
    tm, tf = MLP_TM, 512

    def body(dy2_ref, a_ref, wmo_ref, wmi_ref, dout_ref, x1_ref, y_ref, g2_ref, g1_ref,
             dap_ref, dx1_ref, dy_ref, dg2_ref, dg1_ref):
        @pl.when(pl.program_id(0) == 0)
        def _():
            dg2_ref[...] = jnp.zeros(dg2_ref.shape, F32)
            dg1_ref[...] = jnp.zeros(dg1_ref.shape, F32)

        dy2v = dy2_ref[...]
        for j in range(D_FF // tf):
            cols = slice(j * tf, (j + 1) * tf)
            da2 = _dot_nt(dy2v, wmo_ref[cols, :])
            dap_ref[:, cols] = (da2 * (2.0 * a_ref[:, cols].astype(F32))).astype(BF16)
        dh2 = _dot_nt(dap_ref[...], wmi_ref[...])
        dres, dg2 = _rmsnorm_bwd(dh2, x1_ref[...], g2_ref[...])
        dx1 = dout_ref[...] + dres
        dx1_ref[...] = dx1
        dg2_ref[...] += dg2
        dyv, dg1 = _rmsnorm_bwd(dx1, y_ref[...].astype(F32), g1_ref[...])
        dy_ref[...] = dyv.astype(BF16)
        dg1_ref[...] += dg1

    tok = lambda w: pl.BlockSpec((tm, w), lambda i: (i, 0))
    vec = pl.BlockSpec((1, D_MODEL), lambda i: (0, 0))
    return _call(
        body, name="mlp_bwd", grid=(seq // tm,),
        in_specs=[tok(D_MODEL), tok(D_FF), _resident((D_FF, D_MODEL)), _resident((D_MODEL, D_FF)),
                  tok(D_MODEL), tok(D_MODEL), tok(D_MODEL), vec, vec],
        out_specs=[tok(D_FF), tok(D_MODEL), tok(D_MODEL), vec, vec],
        out_shape=[jax.ShapeDtypeStruct((seq, D_FF), BF16), jax.ShapeDtypeStruct((seq, D_MODEL), F32),
                   jax.ShapeDtypeStruct((seq, D_MODEL), BF16), jax.ShapeDtypeStruct((1, D_MODEL), F32),
                   jax.ShapeDtypeStruct((1, D_MODEL), F32)], scratch_shapes=[],
        params=_params(("arbitrary",), 56), args=(dy2, a, w_mo, w_mi, dout, x1, y, g2, g1), rider=rider)


def _tn_matmul(a, b, name, bm, bn, square_a=False, column_shards=False, rider=None):
    seq, m = a.shape
    n = b.shape[1]
    ts = 2048

    def body(a_ref, b_ref, o_ref):
        @pl.when(pl.program_id(2) == 0)
        def _():
            o_ref[...] = jnp.zeros(o_ref.shape, F32)

        av = a_ref[...]
        if square_a:
            af = av.astype(F32)
            av = (af * af).astype(BF16)
        o_ref[...] += _dot_tn(av, b_ref[...])

    if column_shards:
        out_spec = pl.BlockSpec((None, bm, bn), lambda mi, ni, s: (ni, mi, 0))
        out_shape = jax.ShapeDtypeStruct((n // bn, m, bn), F32)
    else:
        out_spec = pl.BlockSpec((bm, bn), lambda mi, ni, s: (mi, ni))
        out_shape = jax.ShapeDtypeStruct((m, n), F32)
    (out,), riding = _call(
        body, name=name, grid=(m // bm, n // bn, seq // ts),
        in_specs=[pl.BlockSpec((ts, bm), lambda mi, ni, s: (s, mi)), pl.BlockSpec((ts, bn), lambda mi, ni, s: (s, ni))],
        out_specs=[out_spec], out_shape=[out_shape], scratch_shapes=[],
        params=_params(("arbitrary", "arbitrary", "arbitrary"), 40), args=(a, b), rider=rider)
    return out, riding


def _tn_matmul_residue(a, b, dil, name):
    length = a.shape[0]
    m, n = a.shape[1] // dil, b.shape[1] // dil
    ts = min(1024, length)

    def body(a_ref, b_ref, o_ref):
        @pl.when((pl.program_id(0) == 0) & (pl.program_id(1) == 0))
        def _():
            o_ref[...] = jnp.zeros(o_ref.shape, F32)

        o_ref[...] += _dot_tn(a_ref[...], b_ref[...])

    return _pallas(
        body, name=name, grid=(dil, length // ts),
        in_specs=[pl.BlockSpec((ts, m), lambda r, s: (s, r)), pl.BlockSpec((ts, n), lambda r, s: (s, r))],
        out_specs=pl.BlockSpec((m, n), lambda r, s: (0, 0)),
        out_shape=jax.ShapeDtypeStruct((m, n), F32),
        compiler_params=_params(("arbitrary", "arbitrary"), 40),
    )(_in_hbm(a), _in_hbm(b))


def _mix_bwd(dy, ya, yg, mg, rest, w_out, w_ba, w_bg, w_sp, b_col, ln_g, ln_b, rider=None):
    seq = dy.shape[0]
    tm = 256

    def body(dy_ref, ya_ref, yg_ref, mg_ref, up_ref, zp_ref, gap_ref, gbp_ref, wout_ref, wba_ref, wbg_ref,
             wsp_ref, bcol_ref, lg_ref, lb_ref,
             dya0, dya1, dya2, dpr_ref, dwout_ref, dwba_ref, dwbg_ref, dwsp_ref, dbb_ref, dlg_ref, dlb_ref,
             dzln_s, du_s, slab):
        @pl.when(pl.program_id(0) == 0)
        def _():
            for ref in (dwout_ref, dwba_ref, dwbg_ref, dwsp_ref, dbb_ref, dlg_ref, dlb_ref):
                ref[...] = jnp.zeros(ref.shape, F32)

        dyv = dy_ref[...]
        dm = _dot_nt(dyv, wout_ref[...])
        dwout_ref[...] += _dot_tn(mg_ref[...], dyv)
        yab = ya_ref[...].astype(BF16)
        ygb = yg_ref[...]
        a = _dot(yab, wba_ref[...])
        bm = _dot(ygb, wbg_ref[...])
        ga = jax.nn.sigmoid(gap_ref[...].astype(F32))
        gb = jax.nn.sigmoid(gbp_ref[...].astype(F32))
        dpr_ref[:, 2 * GMLP_W:2 * GMLP_W + D_MODEL] = (dm * a * (ga * (1.0 - ga))).astype(BF16)
        dpr_ref[:, 2 * GMLP_W + D_MODEL:REST_W] = (dm * bm * (gb * (1.0 - gb))).astype(BF16)
        da = (dm * ga).astype(BF16)
        db = (dm * gb).astype(BF16)
        dwba = _dot_tn(yab, da)
        dwbg = _dot_tn(ygb, db)
        shard_w = D_MODEL // N_CHIPS
        for j in range(N_CHIPS):
            dwba_ref[j] += dwba[:, j * shard_w:(j + 1) * shard_w]
            dwbg_ref[j] += dwbg[:, j * shard_w:(j + 1) * shard_w]
        dya = _dot_nt(da, wba_ref[...])
        for dya_ref, d in zip((dya0, dya1, dya2), DILATIONS):
            _put_residue(slab, dya, dya_ref, d, GROUP_W, 0)
        dyg = _dot_nt(db, wbg_ref[...])

        zp = zp_ref[...].astype(F32)
        zhat, rstd = _layernorm_stats(_gelu(zp))
        lg = lg_ref[...]
        zln = (zhat * lg + lb_ref[...]).astype(BF16)
        up = up_ref[...].astype(F32)
        u = _gelu(up)
        tril = _tril_mask()
        for g in range(GMLP_GROUPS):
            wm = jnp.where(tril, wsp_ref[g], 0.0).astype(BF16)
            cols = slice(g * CHUNK, (g + 1) * CHUNK)
            for c in range(tm // CHUNK):
                rows = slice(c * CHUNK, (c + 1) * CHUNK)
                zb = zln[rows, cols]
                sz = _dot(wm, zb) + bcol_ref[g]
                dyg_cg = dyg[rows, cols]
                du_s[rows, cols] = dyg_cg * sz
                dsz = dyg_cg * u[rows, cols]
                dszb = dsz.astype(BF16)
                dbb_ref[g] += jnp.broadcast_to(jnp.sum(dsz, axis=-1, keepdims=True), (CHUNK, CHUNK))
                dwsp_ref[g] += jnp.where(tril, _dot_nt(dszb, zb), 0.0)
                dzln_s[rows, cols] = _dot_tn(wm, dszb)
        dzln = dzln_s[...]
        dlg_ref[...] += jnp.sum(dzln * zhat, axis=0, keepdims=True)
        dlb_ref[...] += jnp.sum(dzln, axis=0, keepdims=True)
        dzh = dzln * lg
        dz = rstd * (dzh - jnp.mean(dzh, axis=-1, keepdims=True) - zhat * jnp.mean(dzh * zhat, axis=-1, keepdims=True))
        dpr_ref[:, GMLP_W:2 * GMLP_W] = (dz * _gelu_grad(zp)).astype(BF16)
        dpr_ref[:, 0:GMLP_W] = (du_s[...] * _gelu_grad(up)).astype(BF16)

    tok = lambda w: pl.BlockSpec((tm, w), lambda i: (i, 0))
    full = lambda *s: pl.BlockSpec(s, lambda i: (0,) * len(s))
    return _call(
        body, name="mix_bwd", grid=(seq // tm,),
        in_specs=[tok(D_MODEL), tok(GROUP_W), tok(GMLP_W), tok(D_MODEL),
                  pl.BlockSpec((tm, GMLP_W), lambda i: (i, 0)), pl.BlockSpec((tm, GMLP_W), lambda i: (i, 1)),
                  pl.BlockSpec((tm, D_MODEL), lambda i: (i, 1)), pl.BlockSpec((tm, D_MODEL), lambda i: (i, 2)),
                  full(D_MODEL, D_MODEL), full(GROUP_W, D_MODEL), full(GMLP_W, D_MODEL),
                  full(GMLP_GROUPS, CHUNK, CHUNK), full(GMLP_GROUPS, CHUNK, 1), full(1, GMLP_W), full(1, GMLP_W)],
        out_specs=[pl.BlockSpec((tm // d, d * GROUP_W), lambda i: (i, 0)) for d in DILATIONS]
        + [tok(REST_W), full(D_MODEL, D_MODEL), full(N_CHIPS, GROUP_W, D_MODEL // N_CHIPS),
           full(N_CHIPS, GMLP_W, D_MODEL // N_CHIPS),
           full(GMLP_GROUPS, CHUNK, CHUNK), full(GMLP_GROUPS, CHUNK, CHUNK), full(1, GMLP_W), full(1, GMLP_W)],
        out_shape=[jax.ShapeDtypeStruct((seq // d, d * GROUP_W), F32) for d in DILATIONS]
        + [jax.ShapeDtypeStruct((seq, REST_W), BF16),
           jax.ShapeDtypeStruct((D_MODEL, D_MODEL), F32), jax.ShapeDtypeStruct((N_CHIPS, GROUP_W, D_MODEL // N_CHIPS), F32),
           jax.ShapeDtypeStruct((N_CHIPS, GMLP_W, D_MODEL // N_CHIPS), F32),
           jax.ShapeDtypeStruct((GMLP_GROUPS, CHUNK, CHUNK), F32),
           jax.ShapeDtypeStruct((GMLP_GROUPS, CHUNK, CHUNK), F32), jax.ShapeDtypeStruct((1, GMLP_W), F32),
           jax.ShapeDtypeStruct((1, GMLP_W), F32)],
        scratch_shapes=[pltpu.VMEM((tm, GMLP_W), F32), pltpu.VMEM((tm, GMLP_W), F32),
                        pltpu.VMEM((GROUP_W // LANES, tm, LANES), F32)],
        params=_params(("arbitrary",), 56),
        args=(dy, ya, yg, mg, rest, rest, rest, rest, w_out, w_ba, w_bg, w_sp, b_col, ln_g, ln_b), rider=rider)


IN_PROJ_BWD_TM = 512


def _in_proj_bwd(dqkv, drest, w_in, x, dx1, g0, so_far, span, rider=None):
    seq = x.shape[0]
    tm = IN_PROJ_BWD_TM
    off, steps = span
    gx_so_far, dg_so_far = so_far

    def body(d0, d1, d2, dr_ref, w_ref, x_ref, dx1_ref, g_ref, dg_in_ref, gx_in_ref, gx_ref, dg_ref, slab):
        @pl.when(pl.program_id(0) == 0)
        def _():
            dg_ref[...] = dg_in_ref[...]

        dh = _dot(dr_ref[...], w_ref[QKV_W:, :])
        for g, (d_ref, dil) in enumerate(zip((d0, d1, d2), DILATIONS)):
            piece = d_ref[...] if dil == 1 else _get_tokens(slab, d_ref, dil, 3 * GROUP_W, 0, 3 * GROUP_W).astype(BF16)
            for section, (lo, hi) in enumerate(_qkv_columns(g)):
                dh = dh + _dot(piece[:, section * GROUP_W:(section + 1) * GROUP_W], w_ref[lo:hi, :])
        dres, dg = _rmsnorm_bwd(dh, x_ref[...], g_ref[...])
        gx_ref[...] = dx1_ref[...] + dres
        dg_ref[...] += dg

    tok = lambda w: pl.BlockSpec((tm, w), lambda i: (i + off, 0))
    full = lambda *s: pl.BlockSpec(s, lambda i: (0,) * len(s))
    in_specs = ([pl.BlockSpec((tm // d, d * 3 * GROUP_W), lambda i: (i + off, 0)) for d in DILATIONS] + [tok(REST_W)]
                + [_resident((IN_W, D_MODEL))]
                + [tok(D_MODEL), tok(D_MODEL), full(1, D_MODEL), full(1, D_MODEL), HBM_SPEC])
    return _call(
        body, name=f"in_proj_bwd_{off}", grid=(steps,), in_specs=in_specs,
        out_specs=[tok(D_MODEL), full(1, D_MODEL)],
        out_shape=[jax.ShapeDtypeStruct((seq, D_MODEL), F32), jax.ShapeDtypeStruct((1, D_MODEL), F32)],
        scratch_shapes=[pltpu.VMEM((3 * GROUP_W // LANES, tm, LANES), F32)],
        params=_params(("arbitrary",), 48), args=(*dqkv, drest, w_in, x, dx1, g0, dg_so_far, gx_so_far),
        rider=rider, aliases={len(in_specs) - 1: 0})


def _adamw(w, g, m, v, name):
    rows, cols = w.shape
    tr = _row_tile(rows) if rows % 16 == 0 else rows
    c1 = 1.0 - ADAM_B1 ** ADAM_STEP
    c2 = 1.0 - ADAM_B2 ** ADAM_STEP

    def body(w_ref, g_ref, m_ref, v_ref, go_ref, d_ref, nm_ref, nv_ref):
        gv = g_ref[...]
        go_ref[...] = gv
        nm = ADAM_B1 * m_ref[...] + (1.0 - ADAM_B1) * gv
        nv = ADAM_B2 * v_ref[...] + (1.0 - ADAM_B2) * (gv * gv)
        d_ref[...] = -ADAM_LR * ((nm / c1) / (jnp.sqrt(nv / c2) + ADAM_EPS) + ADAM_WD * w_ref[...])
        nm_ref[...] = nm
        nv_ref[...] = nv

    spec = pl.BlockSpec((tr, cols), lambda i: (i, 0))
    return _pallas(
        body, name=name, grid=(rows // tr,),
        in_specs=[spec] * 4, out_specs=[spec] * 4,
        out_shape=[jax.ShapeDtypeStruct((rows, cols), F32)] * 4,
        compiler_params=_params(("arbitrary",), 32, small=True),
    )(w, g, m, v)


def _place():
    x, y, c = lax.axis_index("x"), lax.axis_index("y"), lax.axis_index("c")
    chips = [(1 - x, y), (x, 1 - y), (1 - x, 1 - y)]
    return x, y, c, chips


class _Exchange:
    def __init__(self, inputs, out_shapes, n_sems, start, finish, aliases=None):
        self.inputs, self.out_shapes, self.n_sems = list(inputs), list(out_shapes), n_sems
        self.start, self.finish, self.aliases = start, finish, dict(aliases or {})

    def scratch(self):
        return [pltpu.SemaphoreType.DMA((self.n_sems,)), pltpu.SemaphoreType.DMA((self.n_sems,))]


def _together(*parts):
    ins = [len(p.inputs) for p in parts]
    outs = [len(p.out_shapes) for p in parts]

    def split(refs, counts):
        pos, pieces = 0, []
        for cnt in counts:
            pieces.append(refs[pos:pos + cnt])
            pos += cnt
        return pieces

    def run(which):
        def go(in_refs, out_refs, *sems):
            for k, (p, i, o) in enumerate(zip(parts, split(in_refs, ins), split(out_refs, outs))):
                getattr(p, which)(i, o, sems[2 * k], sems[2 * k + 1])
        return go

    both = _Exchange([a for p in parts for a in p.inputs], [s for p in parts for s in p.out_shapes], 0, run("start"),
                     run("finish"))
    both.aliases = {sum(ins[:k]) + i: sum(outs[:k]) + o for k, p in enumerate(parts) for i, o in p.aliases.items()}
    both.scratch = lambda: [s for p in parts for s in p.scratch()]
    return both


def _run_exchange(ex, name):
    n_in, n_out = len(ex.inputs), len(ex.out_shapes)

    def body(*refs):
        ins, outs, sems = refs[:n_in], refs[n_in:n_in + n_out], refs[n_in + n_out:]
        ex.start(ins, outs, *sems)
        ex.finish(ins, outs, *sems)

    return _pallas(
        body, name=name, in_specs=[HBM_SPEC] * n_in, out_specs=[HBM_SPEC] * n_out, out_shape=ex.out_shapes,
        scratch_shapes=ex.scratch(), input_output_aliases=ex.aliases,
    )(*ex.inputs)


def _call(body, *, name, grid, in_specs, out_specs, out_shape, scratch_shapes, params, args, rider=None, aliases=None):
    in_specs, out_specs, out_shape, scratch_shapes = list(in_specs), list(out_specs), list(out_shape), list(scratch_shapes)
    aliases = dict(aliases or {})
    args = [_in_hbm(a) for a in args]
    if rider is None:
        outs = _pallas(body, name=name, grid=grid, in_specs=in_specs, out_specs=out_specs, out_shape=out_shape,
                              scratch_shapes=scratch_shapes, input_output_aliases=aliases, compiler_params=params)(*args)
        return list(outs), []
    n_in, n_out, n_scr = len(in_specs), len(out_specs), len(scratch_shapes)
    r_in, r_out = len(rider.inputs), len(rider.out_shapes)

    def wrapped(*refs):
        ins, r_ins = refs[:n_in], refs[n_in:n_in + r_in]
        pos = n_in + r_in
        outs, r_outs = refs[pos:pos + n_out], refs[pos + n_out:pos + n_out + r_out]
        pos += n_out + r_out
        scr, sems = refs[pos:pos + n_scr], refs[pos + n_scr:]
        ids = [pl.program_id(k) for k in range(len(grid))]
        first, last = ids[0] == 0, ids[0] == grid[0] - 1
        for k in range(1, len(grid)):
            first, last = first & (ids[k] == 0), last & (ids[k] == grid[k] - 1)

        @pl.when(first)
        def _():
            rider.start(r_ins, r_outs, *sems)

        body(*ins, *outs, *scr)

        @pl.when(last)
        def _():
            rider.finish(r_ins, r_outs, *sems)

    outs = _pallas(
        wrapped, name=name, grid=grid, in_specs=in_specs + [HBM_SPEC] * r_in, out_specs=out_specs + [HBM_SPEC] * r_out,
        out_shape=out_shape + rider.out_shapes, scratch_shapes=scratch_shapes + rider.scratch(),
        input_output_aliases={**aliases, **{n_in + i: n_out + o for i, o in rider.aliases.items()}}, compiler_params=params,
    )(*args, *rider.inputs)
    return list(outs[:n_out]), list(outs[n_out:])


def _stage_weights(shards):
    n = len(shards)

    def body(*refs):
        ins, outs, stages, sems = refs[:n], refs[n:2 * n], refs[2 * n:3 * n], refs[3 * n]
        x, y, _, _ = _place()
        copies = []
        for t in range(n):
            stages[t][...] = ins[t][...].astype(BF16)
            copies.append(pltpu.make_async_copy(stages[t], outs[t].at[2 * x + y], sems.at[t]))
            copies[-1].start()
        for cp in copies:
            cp.wait()

    assert sum(s.size * 6 for s in shards) <= (CALL_VMEM_MIB - 8) * MIB
    return _pallas(
        body, name="stage_weights", in_specs=[VMEM_SPEC] * n, out_specs=[HBM_SPEC] * n,
        out_shape=[jax.ShapeDtypeStruct((N_CHIPS,) + s.shape, BF16) for s in shards],
        scratch_shapes=[pltpu.VMEM(s.shape, BF16) for s in shards] + [pltpu.SemaphoreType.DMA((n,))],
        compiler_params=pltpu.CompilerParams(vmem_limit_bytes=SMALL_VMEM_MIB * MIB),
    )(*shards)


def _gather(buffers, stage="both", part=(0, 1)):
    n = len(buffers)
    halves = [b.shape[1] // part[1] // 2 for b in buffers]

    def half_of(outs, t, chip, which):
        return outs[t].at[chip, pl.ds((2 * part[0] + which) * halves[t], halves[t]), :]

    def copy(outs, sems, t, k, chip, which, to):
        rows = half_of(outs, t, chip, which)
        return pltpu.make_async_remote_copy(src_ref=rows, dst_ref=rows, send_sem=sems[0].at[6 * t + k],
                                            recv_sem=sems[1].at[6 * t + k], device_id=to, device_id_type=MESH)

    def to_chips(outs, sems, what):
        x, y, c, chips = _place()
        for t in range(n):
            for j, (px, py) in enumerate(chips):
                if what == "start":
                    copy(outs, sems, t, j, 2 * x + y, c, (px, py, c)).start()
                else:
                    copy(outs, sems, t, j, 2 * px + py, c, (px, py, c)).wait_recv()
                    copy(outs, sems, t, j, 2 * x + y, c, (px, py, c)).wait_send()

    def to_sibling(outs, sems, what):
        x, y, c, chips = _place()
        for t in range(n):
            for j, (px, py) in enumerate(chips):
                if what == "start":
                    copy(outs, sems, t, 3 + j, 2 * px + py, c, (x, y, 1 - c)).start()
                else:
                    copy(outs, sems, t, 3 + j, 2 * px + py, 1 - c, (x, y, 1 - c)).wait_recv()
                    copy(outs, sems, t, 3 + j, 2 * px + py, c, (x, y, 1 - c)).wait_send()

    def start(ins, outs, *sems):
        (to_sibling if stage == "pair" else to_chips)(outs, sems, "start")

    def finish(ins, outs, *sems):
        if stage == "both":
            x, y, c, chips = _place()
            for j, (px, py) in enumerate(chips):
                for t in range(n):
                    copy(outs, sems, t, j, 2 * px + py, c, (px, py, c)).wait_recv()
                    copy(outs, sems, t, 3 + j, 2 * px + py, c, (x, y, 1 - c)).start()
            for j, (px, py) in enumerate(chips):
                for t in range(n):
                    copy(outs, sems, t, j, 2 * x + y, c, (px, py, c)).wait_send()
            to_sibling(outs, sems, "finish")
        elif stage == "chips":
            to_chips(outs, sems, "finish")
        else:
            to_sibling(outs, sems, "finish")

    return _Exchange(buffers, [jax.ShapeDtypeStruct(b.shape, b.dtype) for b in buffers], 6 * n, start, finish,
                     aliases={t: t for t in range(n)})


def _pair_exchange(grads):
    n = len(grads)
    halves = [g.shape[1] // 2 for g in grads]

    def copies(ins, outs, send_sems, recv_sems):
        x, y, c, _ = _place()
        return [pltpu.make_async_remote_copy(
            src_ref=ins[t].at[:, pl.ds((1 - c) * halves[t], halves[t]), :], dst_ref=outs[t],
            send_sem=send_sems.at[t], recv_sem=recv_sems.at[t], device_id=(x, y, 1 - c), device_id_type=MESH)
            for t in range(n)]

    def start(*refs):
        for cp in copies(*refs):
            cp.start()

    def finish(*refs):
        for cp in copies(*refs):
            cp.wait()

    return _Exchange(grads, [jax.ShapeDtypeStruct((N_CHIPS, h, g.shape[2]), F32) for g, h in zip(grads, halves)], n,
                     start, finish)


def _row_tile(rows):
    return max(t for t in range(16, 257, 16) if rows % t == 0)


def _pair_add(grad, other, place, name):
    _, rows, cols = grad.shape
    rh = rows // 2
    tr = _row_tile(rh)
    nb = rh // tr

    def body(p_ref, g_ref, a_ref, wire_ref, own_ref):
        s = g_ref[...] + a_ref[...]
        wire_ref[...] = s.astype(BF16)

        @pl.when(pl.program_id(1) == p_ref[1])
        def _():
            own_ref[...] = s

    blk = (None, tr, cols)
    return _pallas(
        body, name=name,
        grid_spec=pltpu.PrefetchScalarGridSpec(
            num_scalar_prefetch=1, grid=(nb, N_CHIPS),
            in_specs=[pl.BlockSpec(blk, lambda i, j, p: (j, p[0] * nb + i, 0)), pl.BlockSpec(blk, lambda i, j, p: (j, i, 0))],
            out_specs=[pl.BlockSpec(blk, lambda i, j, p: (j, i, 0)), pl.BlockSpec((tr, cols), lambda i, j, p: (i, 0))]),
        out_shape=[jax.ShapeDtypeStruct((N_CHIPS, rh, cols), BF16), jax.ShapeDtypeStruct((rh, cols), F32)],
        compiler_params=_params(("arbitrary", "arbitrary"), 32, small=True),
    )(place, grad, other)


def _chip_exchange(wires):
    n = len(wires)

    def copies(ins, outs, send_sems, recv_sems):
        x, y, c, chips = _place()
        return [pltpu.make_async_remote_copy(
            src_ref=ins[t].at[2 * px + py], dst_ref=outs[t].at[j], send_sem=send_sems.at[3 * t + j],
            recv_sem=recv_sems.at[3 * t + j], device_id=(px, py, c), device_id_type=MESH)
            for t in range(n) for j, (px, py) in enumerate(chips)]

    def start(*refs):
        for cp in copies(*refs):
            cp.start()

    def finish(*refs):
        for cp in copies(*refs):
            cp.wait()

    return _Exchange(wires, [jax.ShapeDtypeStruct((3,) + w.shape[1:], BF16) for w in wires], 3 * n, start, finish)


def _chip_add(own, arrived, place, name):
    rh, cols = own.shape
    tr = _row_tile(rh)
    nb = rh // tr

    def body(p_ref, s_ref, b0, b1, b2, o_ref):
        o_ref[...] = ((s_ref[...] + b0[...].astype(F32)) + b1[...].astype(F32)) + b2[...].astype(F32)

    blk = (None, tr, cols)
    return _pallas(
        body, name=name,
        grid_spec=pltpu.PrefetchScalarGridSpec(
            num_scalar_prefetch=1, grid=(nb,),
            in_specs=[pl.BlockSpec((tr, cols), lambda i, p: (i, 0)), pl.BlockSpec(blk, lambda i, p: (0, i, 0)),
                      pl.BlockSpec(blk, lambda i, p: (1, i, 0)), pl.BlockSpec(blk, lambda i, p: (2, i, 0))],
            out_specs=pl.BlockSpec((tr, cols), lambda i, p: (p[0] * nb + i, 0))),
        out_shape=jax.ShapeDtypeStruct((2 * rh, cols), F32),
        compiler_params=_params(("arbitrary",), 32, small=True),
    )(place, own, arrived, arrived, arrived)


def _pair_share(halves):
    n = len(halves)
    rhs = [h.shape[0] // 2 for h in halves]

    def copy(outs, send_sems, recv_sems, t, which):
        x, y, c, _ = _place()
        rows = outs[t].at[pl.ds(which * rhs[t], rhs[t]), :]
        return pltpu.make_async_remote_copy(src_ref=rows, dst_ref=rows, send_sem=send_sems.at[t], recv_sem=recv_sems.at[t],
                                            device_id=(x, y, 1 - c), device_id_type=MESH)

    def start(ins, outs, send_sems, recv_sems):
        c = lax.axis_index("c")
        for t in range(n):
            copy(outs, send_sems, recv_sems, t, c).start()

    def finish(ins, outs, send_sems, recv_sems):
        c = lax.axis_index("c")
        for t in range(n):
            copy(outs, send_sems, recv_sems, t, c).wait_send()
            copy(outs, send_sems, recv_sems, t, 1 - c).wait_recv()

    return _Exchange(halves, [jax.ShapeDtypeStruct(h.shape, F32) for h in halves], n, start, finish,
                     aliases={t: t for t in range(n)})


class _GradReduction:
    def __init__(self, grads, place, tag):
        self.names, self.grads, self.place, self.tag = list(grads), grads, place, tag

    def pair_exchange(self):
        return _pair_exchange([self.grads[n] for n in self.names])

    def chip_exchange(self, others):
        sums = [_pair_add(self.grads[n], o, self.place, f"{self.tag}_pair_add_{n}") for n, o in zip(self.names, others)]
        self.owns = [own for _, own in sums]
        return _chip_exchange([wire for wire, _ in sums])

    def pair_share(self, arrived):
        return _pair_share([_chip_add(own, arr, self.place, f"{self.tag}_chip_add_{n}")
                            for n, own, arr in zip(self.names, self.owns, arrived)])

    def result(self, shared):
        return dict(zip(self.names, shared))


def _all_reduce_small(p):
    rows, lanes = p.shape
    half = rows // 2

    def body(p_ref, o_ref, sib, sums, send_sems, recv_sems):
        x, y, c, chips = _place()
        mine, sibling = 2 * x + y, (x, y, 1 - c)
        swap = pltpu.make_async_remote_copy(src_ref=p_ref, dst_ref=sib, send_sem=send_sems.at[0], recv_sem=recv_sems.at[0],
                                            device_id=sibling, device_id_type=MESH)
        swap.start()
        swap.wait()
        sums[mine] = p_ref[...] + sib[...]

        def copy(k, chip, which, to):
            part = sums.at[chip, pl.ds(which * half, half), :]
            return pltpu.make_async_remote_copy(src_ref=part, dst_ref=part, send_sem=send_sems.at[k], recv_sem=recv_sems.at[k],
                                                device_id=to, device_id_type=MESH)

        for j, (px, py) in enumerate(chips):
            copy(1 + j, mine, c, (px, py, c)).start()
        for j, (px, py) in enumerate(chips):
            copy(1 + j, 2 * px + py, c, (px, py, c)).wait_recv()
            copy(4 + j, 2 * px + py, c, sibling).start()
        for j, (px, py) in enumerate(chips):
            copy(4 + j, 2 * px + py, 1 - c, sibling).wait_recv()
        for j, (px, py) in enumerate(chips):
            copy(1 + j, mine, c, (px, py, c)).wait_send()
            copy(4 + j, 2 * px + py, c, sibling).wait_send()
        o_ref[...] = ((sums[0] + sums[1]) + sums[2]) + sums[3]

    return _pallas(
        body, name="small_all_reduce", in_specs=[VMEM_SPEC], out_specs=VMEM_SPEC,
        out_shape=jax.ShapeDtypeStruct((rows, lanes), F32),
        scratch_shapes=[pltpu.VMEM((rows, lanes), F32), pltpu.VMEM((N_CHIPS, rows, lanes), F32),
                        pltpu.SemaphoreType.DMA((7,)), pltpu.SemaphoreType.DMA((7,))],
        compiler_params=pltpu.CompilerParams(vmem_limit_bytes=SMALL_VMEM_MIB * MIB),
    )(p)


BIG = ("w_in", "w_branch_attn", "w_branch_gmlp", "w_out", "w_mlp_in", "w_mlp_out")
COLUMN_SHARDED = ("w_branch_attn", "w_branch_gmlp", "w_mlp_in")
SMALL = ("norm_pre_mix", "w_spatial", "b_spatial", "ln_v_gain", "ln_v_bias", "norm_post_mix", "norm_pre_mlp", "norm_post_mlp")
ORDER = ("norm_pre_mix", "w_in", "w_spatial", "b_spatial", "ln_v_gain", "ln_v_bias", "w_branch_attn", "w_branch_gmlp",
         "w_out", "norm_post_mix", "norm_pre_mlp", "w_mlp_in", "w_mlp_out", "norm_post_mlp")


def _full_weight(name, gathered):
    if name in COLUMN_SHARDED:
        return jnp.transpose(gathered, (1, 0, 2)).reshape(gathered.shape[1], -1)
    return gathered.reshape(-1, gathered.shape[2])


def _rows8(a):
    a = a.reshape(-1, 128)
    pad = (-a.shape[0]) % 8
    return jnp.pad(a, ((0, pad), (0, 0))) if pad else a


def _qkv_columns(group):
    return [(sec * ATTN_W + group * GROUP_W, sec * ATTN_W + (group + 1) * GROUP_W) for sec in range(3)]


def _device_step(x, target, small, shards, place):
    seq = x.shape[0]
    g0, g1, g2, g3 = small["norm_pre_mix"], small["norm_post_mix"], small["norm_pre_mlp"], small["norm_post_mlp"]
    w_sp = small["w_spatial"]
    b_col = small["b_spatial"].reshape(GMLP_GROUPS, CHUNK, 1)
    ln_g, ln_b = small["ln_v_gain"], small["ln_v_bias"]

    staged = _stage_weights(shards)
    h, tables, (w_in,) = _prepare(x, g0, rider=_gather(staged[:1]))
    w_in = _full_weight("w_in", w_in)
    (*qkv, rest), landed = _in_proj(h[0], w_in, *tables[1], rider=_gather(staged[1:], "chips"))

    o_l, gathered = _attn_fwd(qkv[0], DILATIONS[0], rider=_gather(landed, "pair"))
    full = {n: _full_weight(n, gw) for n, gw in zip(BIG[1:], gathered)}
    for g in range(1, N_GROUPS):
        o_l.extend(_attn_fwd(qkv[g], DILATIONS[g])[0])
    (*ya_l, yg, mg, y, x1), _ = _mix_fwd(o_l, rest, x, w_sp, b_col, ln_g, ln_b, full["w_branch_attn"],
                                        full["w_branch_gmlp"], full["w_out"], g1)
    ya, lse = ya_l[0::2], ya_l[1::2]
    h2, a, dy2, dout, loss8, dg3 = _mlp_fwd(x1, g2, g3, full["w_mlp_in"], full["w_mlp_out"], target)
    d_wmo, _ = _tn_matmul(a, dy2, "grad_w_mlp_out", 1024, 1024, square_a=True)
    mlp_out = _GradReduction({"w_mlp_out": d_wmo.reshape(N_CHIPS, D_FF // N_CHIPS, D_MODEL)}, place, "mlp_out")
    (dap, dx1, dy, dg2, dg1), riding = _mlp_bwd(dy2, a, full["w_mlp_out"], full["w_mlp_in"], dout, x1, y, g2, g1,
                                                 rider=mlp_out.pair_exchange())
    d_wmi, riding = _tn_matmul(h2, dap, "grad_w_mlp_in", 1024, 1024, column_shards=True,
                               rider=mlp_out.chip_exchange(riding))
    mlp_in = _GradReduction({"w_mlp_in": d_wmi}, place, "mlp_in")
    (*dya, drest, d_wout, d_wba, d_wbg, d_wsp, d_bb, d_lg, d_lb), riding = _mix_bwd(
        dy, ya[0], yg, mg, rest, full["w_out"], full["w_branch_attn"], full["w_branch_gmlp"], w_sp, b_col, ln_g, ln_b,
        rider=_together(mlp_out.pair_share(riding), mlp_in.pair_exchange()))
    reduced = mlp_out.result(riding[:1])
    mix = _GradReduction({"w_branch_attn": d_wba, "w_branch_gmlp": d_wbg,
                          "w_out": d_wout.reshape(N_CHIPS, D_MODEL // N_CHIPS, D_MODEL)}, place, "mix")
    attn = lambda g, rider: _attn_bwd(qkv[g], dya[g], ya[g], lse[g], *tables[DILATIONS[g]], DILATIONS[g], rider=rider)
    dqkv0, riding = attn(0, _together(mlp_in.chip_exchange(riding[1:]), mix.pair_exchange()))
    dqkv1, riding = attn(1, _together(mlp_in.pair_share(riding[:1]), mix.chip_exchange(riding[1:])))
    reduced.update(mlp_in.result(riding[:1]))
    dqkv2, riding = attn(2, mix.pair_share(riding[1:]))
    reduced.update(mix.result(riding))
    dqkv = [dqkv0, dqkv1, dqkv2]

    d_qkv = [_tn_matmul_residue(dqkv[g], h[g], dil, f"grad_w_in_qkv{g}") for g, dil in enumerate(DILATIONS)]
    d_rest, _ = _tn_matmul(drest, h[0], "grad_w_in_rest", 1024, 1024)
    d_win = jnp.concatenate([d_qkv[g][s * GROUP_W:(s + 1) * GROUP_W] for s in range(3) for g in range(N_GROUPS)]
                            + [d_rest], axis=0)
    first = _GradReduction({"w_in": d_win.reshape(N_CHIPS, IN_W // N_CHIPS, D_MODEL)}, place, "w_in")
    tiles = seq // IN_PROJ_BWD_TM
    so_far = (lax.empty((seq, D_MODEL), F32), jnp.zeros((1, D_MODEL), F32))
    in_bwd = lambda so_far, span, rider: _in_proj_bwd(dqkv, drest, w_in, x, dx1, g0, so_far, span, rider=rider)
    head = 3 * tiles // 8
    so_far, riding = in_bwd(so_far, (0, head), first.pair_exchange())
    (grad_x, dg0), riding = in_bwd(so_far, (head, tiles - head), first.chip_exchange(riding))
    reduced.update(first.result(_run_exchange(first.pair_share(riding), "w_in_pair_share")))
    little = {"norm_pre_mix": dg0, "w_spatial": d_wsp, "b_spatial": d_bb[:, :, 0], "ln_v_gain": d_lg, "ln_v_bias": d_lb,
              "norm_post_mix": dg1, "norm_pre_mlp": dg2, "norm_post_mlp": dg3}
    return loss8, grad_x, reduced, little


def kernel(x, norm_pre_mix, w_in, w_spatial, b_spatial, ln_v_gain, ln_v_bias, w_branch_attn, w_branch_gmlp, w_out, norm_post_mix, norm_pre_mlp, w_mlp_in, w_mlp_out, norm_post_mlp, loss_target, m_norm_pre_mix, m_w_in, m_w_spatial, m_b_spatial, m_ln_v_gain, m_ln_v_bias, m_w_branch_attn, m_w_branch_gmlp, m_w_out, m_norm_post_mix, m_norm_pre_mlp, m_w_mlp_in, m_w_mlp_out, m_norm_post_mlp, v_norm_pre_mix, v_w_in, v_w_spatial, v_b_spatial, v_ln_v_gain, v_ln_v_bias, v_w_branch_attn, v_w_branch_gmlp, v_w_out, v_norm_post_mix, v_norm_pre_mlp, v_w_mlp_in, v_w_mlp_out, v_norm_post_mlp):
    given = dict(norm_pre_mix=norm_pre_mix, w_in=w_in, w_spatial=w_spatial, b_spatial=b_spatial, ln_v_gain=ln_v_gain,
                 ln_v_bias=ln_v_bias, w_branch_attn=w_branch_attn, w_branch_gmlp=w_branch_gmlp, w_out=w_out,
                 norm_post_mix=norm_post_mix, norm_pre_mlp=norm_pre_mlp, w_mlp_in=w_mlp_in, w_mlp_out=w_mlp_out,
                 norm_post_mlp=norm_post_mlp)
    moments_m = dict(norm_pre_mix=m_norm_pre_mix, w_in=m_w_in, w_spatial=m_w_spatial, b_spatial=m_b_spatial,
                     ln_v_gain=m_ln_v_gain, ln_v_bias=m_ln_v_bias, w_branch_attn=m_w_branch_attn,
                     w_branch_gmlp=m_w_branch_gmlp, w_out=m_w_out, norm_post_mix=m_norm_post_mix,
                     norm_pre_mlp=m_norm_pre_mlp, w_mlp_in=m_w_mlp_in, w_mlp_out=m_w_mlp_out, norm_post_mlp=m_norm_post_mlp)
    moments_v = dict(norm_pre_mix=v_norm_pre_mix, w_in=v_w_in, w_spatial=v_w_spatial, b_spatial=v_b_spatial,
                     ln_v_gain=v_ln_v_gain, ln_v_bias=v_ln_v_bias, w_branch_attn=v_w_branch_attn,
                     w_branch_gmlp=v_w_branch_gmlp, w_out=v_w_out, norm_post_mix=v_norm_post_mix,
                     norm_pre_mlp=v_norm_pre_mlp, w_mlp_in=v_w_mlp_in, w_mlp_out=v_w_mlp_out, norm_post_mlp=v_norm_post_mlp)
    cx, cy, cc = lax.axis_index("x"), lax.axis_index("y"), lax.axis_index("c")

    shards = [given[n][0].T if n == "w_in" else given[n][0] for n in BIG]
    small = {n: given[n][0] if given[n].ndim > 2 else given[n] for n in SMALL}
    place = jnp.stack([cc, 2 * cx + cy]).astype(jnp.int32)
    loss8, grad_x, grad_shard, grads = _device_step(x[0], loss_target[0], small, shards, place)

    packed = jnp.concatenate([_rows8(grads[n]) for n in SMALL] + [loss8], axis=0)
    summed = _all_reduce_small(packed)
    loss = summed[packed.shape[0] - loss8.shape[0], 0]
    row = 0
    for n in SMALL:
        shape = given[n][0].shape
        cnt = -(-(given[n][0].size // 128) // 8) * 8
        grad_shard[n] = summed[row:row + given[n][0].size // 128].reshape(shape)
        row += cnt

    grad_out, deltas, new_m, new_v = {}, {}, {}, {}
    for n in ORDER:
        shape = given[n].shape
        if n == "w_in":
            outs = _adamw(given[n][0].T, grad_shard[n], moments_m[n][0].T, moments_v[n][0].T, "adamw_" + n)
            outs = [o.T for o in outs]
        else:
            two_d = (-1, shape[-1])
            outs = _adamw(given[n].reshape(two_d), grad_shard[n].reshape(two_d), moments_m[n].reshape(two_d),
                          moments_v[n].reshape(two_d), "adamw_" + n)
        grad_out[n], deltas[n], new_m[n], new_v[n] = [o.reshape(shape) for o in outs]
    return (loss, grad_x[None], *[grad_out[n] for n in ORDER], *[deltas[n] for n in ORDER], *[new_m[n] for n in ORDER],
            *[new_v[n] for n in ORDER])
```

```python
import math

import jax
import jax.numpy as jnp
from jax import lax
from jax.experimental import pallas as pl
from jax.experimental.pallas import tpu as pltpu

F32 = jnp.float32
BF16 = jnp.bfloat16
MESH = pl.DeviceIdType.MESH

D_MODEL = 1024
HEAD_DIM = 64
HEADS_PER_GROUP = 4
GROUP_W = HEADS_PER_GROUP * HEAD_DIM
DILATIONS = (1, 4, 16)
N_GROUPS = len(DILATIONS)
ATTN_W = N_GROUPS * GROUP_W
QKV_W = 3 * ATTN_W
GMLP_W = 512
GMLP_GROUPS = 4
CHUNK = 128
REST_W = 2 * GMLP_W + 2 * D_MODEL
IN_W = QKV_W + REST_W
D_FF = 4096
QBLK = 128
ROPE_THETA = 10000.0
EPS = 1e-6
NEG = -1e30
SCALE = HEAD_DIM ** -0.5
N_CHIPS = 4

ADAM_LR = 0.001
ADAM_B1 = 0.9
ADAM_B2 = 0.999
ADAM_EPS = 1e-08
ADAM_WD = 0.01
ADAM_STEP = 10

MIB = 1024 * 1024
HBM_SPEC = pl.BlockSpec(memory_space=pltpu.HBM)
VMEM_SPEC = pl.BlockSpec(memory_space=pltpu.VMEM)


MLP_FWD_TM = 512
MLP_TM = 512


CALL_VMEM_MIB = 56
SMALL_VMEM_MIB = 32


def _params(semantics, vmem_mib, small=False):
    assert vmem_mib <= CALL_VMEM_MIB
    return pltpu.CompilerParams(dimension_semantics=semantics,
                                vmem_limit_bytes=(SMALL_VMEM_MIB if small else CALL_VMEM_MIB) * MIB)


def _in_hbm(a):
    return pltpu.with_memory_space_constraint(a, pltpu.HBM) if a.size * a.dtype.itemsize >= MIB else a


def _pallas(body, **kwargs):
    return pl.pallas_call(body, **kwargs)


def _resident(shape):
    return pl.BlockSpec(shape, lambda *_: (0,) * len(shape), pipeline_mode=pl.Buffered(1))


def _dot(a, b):
    return jnp.dot(a, b, preferred_element_type=F32)


def _dot_nt(a, b):
    return lax.dot_general(a, b, (((1,), (1,)), ((), ())), preferred_element_type=F32)


def _dot_tn(a, b):
    return lax.dot_general(a, b, (((0,), (0,)), ((), ())), preferred_element_type=F32)


_GELU_C = math.sqrt(2.0 / math.pi)


def _gelu(x):
    return x * (0.5 * (1.0 + jnp.tanh(_GELU_C * (x + 0.044715 * (x * x * x)))))


def _gelu_grad(x):
    t = jnp.tanh(_GELU_C * (x + 0.044715 * (x * x * x)))
    return 0.5 * (1.0 + t) + 0.5 * x * (1.0 - t * t) * (_GELU_C * (1.0 + 3.0 * 0.044715 * (x * x)))


def _rsqrt_ms(v):
    return lax.rsqrt(jnp.mean(v * v, axis=-1, keepdims=True) + EPS)


def _rmsnorm_bwd(dn, src, gain):
    r = _rsqrt_ms(src)
    t = gain * dn
    dgain = jnp.sum(dn * (src * r), axis=0, keepdims=True)
    dsrc = r * t - src * ((r * r * r) * jnp.mean(t * src, axis=-1, keepdims=True))
    return dsrc, dgain


def _rot_half(v):
    w = v.shape[-1]
    lane = lax.broadcasted_iota(jnp.int32, v.shape, v.ndim - 1)
    return jnp.where((lane % HEAD_DIM) < HEAD_DIM // 2, pltpu.roll(v, w - HEAD_DIM // 2, v.ndim - 1),
                     pltpu.roll(v, HEAD_DIM // 2, v.ndim - 1))


def _head_masks(shape):
    lane = lax.broadcasted_iota(jnp.int32, shape, 1)
    return [(lane >= h * HEAD_DIM) & (lane < (h + 1) * HEAD_DIM) for h in range(HEADS_PER_GROUP)]


def _head_stack(block, hmask):
    zero = jnp.zeros((), block.dtype)
    return jnp.concatenate([jnp.where(hm, block, zero) for hm in hmask], axis=0)


LANES = 128


def _put_residue(slab, val, out_ref, dil, width, col0):
    tm, w = val.shape
    if dil == 1:
        out_ref[:, col0:col0 + w] = val.astype(out_ref.dtype)
        return
    for k in range(w // LANES):
        slab[k] = val[:, k * LANES:(k + 1) * LANES]
    for r in range(dil):
        for k in range(w // LANES):
            c = r * width + col0 + k * LANES
            out_ref[:, c:c + LANES] = slab[k, pl.ds(r, tm // dil, stride=dil), :].astype(out_ref.dtype)


def _get_tokens(slab, in_ref, dil, width, col0, w):
    if dil == 1:
        return in_ref[:, col0:col0 + w].astype(F32)
    rows = in_ref.shape[0]
    for r in range(dil):
        for k in range(w // LANES):
            c = r * width + col0 + k * LANES
            slab[k, pl.ds(r, rows, stride=dil), :] = in_ref[:, c:c + LANES].astype(F32)
    return jnp.concatenate([slab[k] for k in range(w // LANES)], axis=1)


def _prepare(x, g0, rider=None):
    seq = x.shape[0]
    half = HEAD_DIM // 2
    inv_freq = ROPE_THETA ** (-jnp.arange(half, dtype=F32) / half)
    freq = jnp.tile(inv_freq, LANES // half).reshape(1, LANES)
    tm = 256

    def body(x_ref, g_ref, f_ref, *refs):
        h_refs, tabs, slab = refs[:N_GROUPS], refs[N_GROUPS:3 * N_GROUPS], refs[-1]
        xv = x_ref[...]
        hf = (xv * _rsqrt_ms(xv)) * g_ref[...]
        for g, dil in enumerate(DILATIONS):
            _put_residue(slab, hf, h_refs[g], dil, D_MODEL, 0)
        row = lax.broadcasted_iota(jnp.int32, (tm, LANES), 0) + pl.program_id(0) * tm
        lane = lax.broadcasted_iota(jnp.int32, (tm, LANES), 1)
        ang = row.astype(F32) * f_ref[...]
        cos = jnp.cos(ang)
        sin = jnp.where((lane % HEAD_DIM) < half, -jnp.sin(ang), jnp.sin(ang))
        for i, dil in enumerate(DILATIONS):
            for tab, val in ((tabs[2 * i], cos), (tabs[2 * i + 1], sin)):
                slab[0] = val
                for r in range(dil):
                    piece = slab[0, pl.ds(r, tm // dil, stride=dil), :] if dil > 1 else val
                    for k in range(GROUP_W // LANES):
                        tab[:, r * GROUP_W + k * LANES:r * GROUP_W + (k + 1) * LANES] = piece

    outs, riding = _call(
        body, name="prepare", grid=(seq // tm,),
        in_specs=[pl.BlockSpec((tm, D_MODEL), lambda i: (i, 0)), pl.BlockSpec((1, D_MODEL), lambda i: (0, 0)),
                  pl.BlockSpec((1, LANES), lambda i: (0, 0))],
        out_specs=[pl.BlockSpec((tm // d, d * D_MODEL), lambda i: (i, 0)) for d in DILATIONS]
        + [pl.BlockSpec((tm // d, d * GROUP_W), lambda i: (i, 0)) for d in DILATIONS for _ in range(2)],
        out_shape=[jax.ShapeDtypeStruct((seq // d, d * D_MODEL), BF16) for d in DILATIONS]
        + [jax.ShapeDtypeStruct((seq // d, d * GROUP_W), F32) for d in DILATIONS for _ in range(2)],
        scratch_shapes=[pltpu.VMEM((D_MODEL // LANES, tm, LANES), F32)],
        params=_params(("arbitrary",), 32), args=(x, g0, freq), rider=rider)
    tabs = outs[N_GROUPS:]
    return outs[:N_GROUPS], {d: (tabs[2 * i], tabs[2 * i + 1]) for i, d in enumerate(DILATIONS)}, riding


def _in_proj(h, w_in, cos_t, sin_t, rider=None):
    seq = h.shape[0]
    tm, tn = 512, GROUP_W
    n_qk = 2 * ATTN_W // tn
    n_qkv = QKV_W // tn

    def body(h_ref, w_ref, cos_ref, sin_ref, *refs):
        qkv_refs, rest_ref, slab = refs[:N_GROUPS], refs[N_GROUPS], refs[-1]
        hb = h_ref[...]
        cos, sin = cos_ref[...], sin_ref[...]
        for j in range(IN_W // tn):
            p = _dot_nt(hb, w_ref[j * tn:(j + 1) * tn, :])
            if j < n_qkv:
                if j < n_qk:
                    p = p * cos + _rot_half(p) * sin
                section, g = divmod(j, N_GROUPS)
                _put_residue(slab, p, qkv_refs[g], DILATIONS[g], 3 * GROUP_W, section * GROUP_W)
            else:
                rest_ref[:, (j - n_qkv) * tn:(j - n_qkv + 1) * tn] = p.astype(BF16)

    return _call(
        body, name="in_proj", grid=(seq // tm,),
        in_specs=[pl.BlockSpec((tm, D_MODEL), lambda i: (i, 0)),
                  _resident((IN_W, D_MODEL)),
                  pl.BlockSpec((tm, GROUP_W), lambda i: (i, 0)),
                  pl.BlockSpec((tm, GROUP_W), lambda i: (i, 0))],
        out_specs=[pl.BlockSpec((tm // d, d * 3 * GROUP_W), lambda i: (i, 0)) for d in DILATIONS]
        + [pl.BlockSpec((tm, REST_W), lambda i: (i, 0))],
        out_shape=[jax.ShapeDtypeStruct((seq // d, d * 3 * GROUP_W), BF16) for d in DILATIONS]
        + [jax.ShapeDtypeStruct((seq, REST_W), BF16)],
        scratch_shapes=[pltpu.VMEM((GROUP_W // LANES, tm, LANES), F32)],
        params=_params(("arbitrary",), 48), args=(h, w_in, cos_t, sin_t), rider=rider)


def _band_masks():
    qi = lax.broadcasted_iota(jnp.int32, (QBLK, QBLK), 0)
    kj = lax.broadcasted_iota(jnp.int32, (QBLK, QBLK), 1)
    return kj <= qi, kj >= qi


def _attn_tile(length):
    return min(1024, length)


def _attn_fwd(qkv, dil, rider=None):
    length = qkv.shape[0]
    tq = _attn_tile(length)
    nsub = tq // QBLK
    nblk = length // tq

    def body(q_ref, k_ref, v_ref, kp_ref, vp_ref, o_ref, l_ref):
        n = pl.program_id(1)
        mask_c, mask_p0 = _band_masks()
        hmask = _head_masks((QBLK, GROUP_W))
        zero = jnp.zeros((), BF16)
        for b in range(nsub):
            rows = slice(b * QBLK, (b + 1) * QBLK)
            q = q_ref[rows, :]
            kc, vc = k_ref[rows, :], v_ref[rows, :]
            if b == 0:
                kp, vp = kp_ref[...], vp_ref[...]
                mask_p = mask_p0 & (n > 0)
            else:
                prow = slice((b - 1) * QBLK, b * QBLK)
                kp, vp = k_ref[prow, :], v_ref[prow, :]
                mask_p = mask_p0
            o_acc = jnp.zeros((QBLK, GROUP_W), F32)
            l_acc = jnp.zeros((QBLK, GROUP_W), F32)
            for h in range(HEADS_PER_GROUP):
                hm = hmask[h]
                sc = jnp.where(mask_c, _dot_nt(q, jnp.where(hm, kc, zero)) * SCALE, NEG)
                sp = jnp.where(mask_p, _dot_nt(q, jnp.where(hm, kp, zero)) * SCALE, NEG)
                m = jnp.maximum(jnp.max(sc, axis=-1, keepdims=True), jnp.max(sp, axis=-1, keepdims=True))
                pc, pp = jnp.exp(sc - m), jnp.exp(sp - m)
                den = jnp.sum(pc, axis=-1, keepdims=True) + jnp.sum(pp, axis=-1, keepdims=True)
                pv = _dot(pc.astype(BF16), jnp.where(hm, vc, zero)) + _dot(pp.astype(BF16), jnp.where(hm, vp, zero))
                o_acc = o_acc + pv / den
                l_acc = l_acc + jnp.where(hm, m + jnp.log(den), 0.0)
            o_ref[rows, :] = o_acc.astype(BF16)
            l_ref[rows, :] = l_acc

    cur = lambda sec: pl.BlockSpec((tq, GROUP_W), lambda r, n: (n, r * 3 + sec))
    prev = lambda sec: pl.BlockSpec((QBLK, GROUP_W), lambda r, n: (jnp.maximum(n * nsub - 1, 0), r * 3 + sec))
    return _call(
        body, name=f"attn_fwd_d{dil}", grid=(dil, nblk),
        in_specs=[cur(0), cur(1), cur(2), prev(1), prev(2)],
        out_specs=[pl.BlockSpec((tq, GROUP_W), lambda r, n: (n, r))] * 2,
        out_shape=[jax.ShapeDtypeStruct((length, dil * GROUP_W), BF16),
                   jax.ShapeDtypeStruct((length, dil * GROUP_W), F32)], scratch_shapes=[],
        params=_params(("arbitrary", "arbitrary"), 32), args=(qkv, qkv, qkv, qkv, qkv), rider=rider)


def _attn_bwd(qkv, dy, y, lse, cos_t, sin_t, dil, rider=None):
    length = qkv.shape[0]
    tq = _attn_tile(length)
    nsub = tq // QBLK
    nblk = length // tq

    def body(q_ref, k_ref, v_ref, kp_ref, vp_ref, qn_ref, dy_ref, y_ref, l_ref, dyn_ref, yn_ref, ln_ref,
             cos_ref, sin_ref, out_ref, dq_s, dk_s, dv_s):
        n = pl.program_id(1)
        mask_c, mask_p0 = _band_masks()
        hmask = _head_masks((QBLK, GROUP_W))
        sub = lambda ref, b: ref[b * QBLK:(b + 1) * QBLK, :]
        kbd = [_head_stack(kp_ref[...], hmask)] + [_head_stack(sub(k_ref, b), hmask) for b in range(nsub)]
        vbd = [_head_stack(vp_ref[...], hmask)] + [_head_stack(sub(v_ref, b), hmask) for b in range(nsub)]
        dq_s[...] = jnp.zeros(dq_s.shape, F32)

        def query_terms(q, dyv, yv, lv):
            prod = dyv * yv
            return dict(
                q=q, dy=dyv.astype(BF16), q_heads=[jnp.where(hm, q, jnp.zeros((), BF16)) for hm in hmask],
                dy_heads=[jnp.where(hm, dyv, 0.0).astype(BF16) for hm in hmask],
                delta=[jnp.sum(jnp.where(hm, prod, 0.0), axis=-1, keepdims=True) for hm in hmask],
                lse=[jnp.max(jnp.where(hm, lv, NEG), axis=-1, keepdims=True) for hm in hmask])

        queries = [query_terms(sub(q_ref, b), sub(dy_ref, b), sub(y_ref, b), sub(l_ref, b)) for b in range(nsub)]
        queries.append(query_terms(qn_ref[...], dyn_ref[...], yn_ref[...], ln_ref[...]))
        rows_of = lambda items: items[0] if len(items) == 1 else jnp.concatenate(items, axis=0)
        for kb in range(nsub + 1):
            seen = [(kb - 1, mask_c)] if kb >= 1 else []
            if kb == 0:
                seen.append((0, mask_p0 & (n > 0)))
            elif kb < nsub:
                seen.append((kb, mask_p0))
            else:
                seen.append((nsub, mask_p0 & (n < nblk - 1)))
            qs = [queries[b] for b, _ in seen]
            mask = rows_of([m for _, m in seen])
            s = _dot_nt(rows_of([t["q"] for t in qs]), kbd[kb]) * SCALE
            dp = _dot_nt(rows_of([t["dy"] for t in qs]), vbd[kb])
            ps, dss = [], []
            for h in range(HEADS_PER_GROUP):
                cols = slice(h * QBLK, (h + 1) * QBLK)
                p = jnp.exp(jnp.where(mask, s[:, cols] - rows_of([t["lse"][h] for t in qs]), NEG))
                ps.append(p.astype(BF16))
                dss.append((p * (dp[:, cols] - rows_of([t["delta"][h] for t in qs]))).astype(BF16))
            dq = _dot(jnp.concatenate(dss, axis=1), kbd[kb]) * SCALE
            for i, (b, _) in enumerate(seen):
                if b < nsub:
                    dq_s[b * QBLK:(b + 1) * QBLK, :] += dq[i * QBLK:(i + 1) * QBLK, :]
            if kb >= 1:
                krows = slice((kb - 1) * QBLK, kb * QBLK)
                head_rows = lambda key: jnp.concatenate([t[key][h] for h in range(HEADS_PER_GROUP) for t in qs], axis=0)
                dv_s[krows, :] = _dot_tn(jnp.concatenate(ps, axis=0), head_rows("dy_heads"))
                dk_s[krows, :] = _dot_tn(jnp.concatenate(dss, axis=0), head_rows("q_heads")) * SCALE
        cos, sin = cos_ref[...], sin_ref[...]
        dq, dk = dq_s[...], dk_s[...]
        out_ref[:, 0:GROUP_W] = (dq * cos - _rot_half(dq) * sin).astype(BF16)
        out_ref[:, GROUP_W:2 * GROUP_W] = (dk * cos - _rot_half(dk) * sin).astype(BF16)
        out_ref[:, 2 * GROUP_W:3 * GROUP_W] = dv_s[...].astype(BF16)

    cur = lambda sec: pl.BlockSpec((tq, GROUP_W), lambda r, n: (n, r * 3 + sec))
    prev = lambda sec: pl.BlockSpec((QBLK, GROUP_W), lambda r, n: (jnp.maximum(n * nsub - 1, 0), r * 3 + sec))
    nxt_q = pl.BlockSpec((QBLK, GROUP_W), lambda r, n: (jnp.minimum((n + 1) * nsub, nblk * nsub - 1), r * 3))
    tok = pl.BlockSpec((tq, GROUP_W), lambda r, n: (n, r))
    tok_next = pl.BlockSpec((QBLK, GROUP_W), lambda r, n: (jnp.minimum((n + 1) * nsub, nblk * nsub - 1), r))
    (out,), riding = _call(
        body, name=f"attn_bwd_d{dil}", grid=(dil, nblk),
        in_specs=[cur(0), cur(1), cur(2), prev(1), prev(2), nxt_q,
                  tok, tok, tok, tok_next, tok_next, tok_next, tok, tok],
        out_specs=[pl.BlockSpec((tq, 3 * GROUP_W), lambda r, n: (n, r))],
        out_shape=[jax.ShapeDtypeStruct((length, dil * 3 * GROUP_W), BF16)],
        scratch_shapes=[pltpu.VMEM((tq, GROUP_W), F32)] * 3,
        params=_params(("arbitrary", "arbitrary"), 32),
        args=(qkv, qkv, qkv, qkv, qkv, qkv, dy, y, lse, dy, y, lse, cos_t, sin_t), rider=rider)
    return out, riding


def _layernorm_stats(z):
    mu = jnp.mean(z, axis=-1, keepdims=True)
    zc = z - mu
    rstd = lax.rsqrt(jnp.mean(zc * zc, axis=-1, keepdims=True) + EPS)
    return zc * rstd, rstd


def _tril_mask():
    row = lax.broadcasted_iota(jnp.int32, (CHUNK, CHUNK), 0)
    col = lax.broadcasted_iota(jnp.int32, (CHUNK, CHUNK), 1)
    return col <= row


def _mix_fwd(o_l, rest, x, w_sp, b_col, ln_g, ln_b, w_ba, w_bg, w_out, g1, rider=None):
    seq = x.shape[0]
    tm = 512

    def body(o0, l0, o1, l1, o2, l2, up_ref, zp_ref, gap_ref, gbp_ref, x_ref, wsp_ref, bcol_ref, lg_ref, lb_ref,
             wba_ref, wbg_ref, wout_ref, g1_ref, ya0, lj0, ya1, lj1, ya2, lj2, yg_ref, mg_ref, y_ref, x1_ref, slab):
        outs = [_get_tokens(slab, o, d, GROUP_W, 0, GROUP_W) for o, d in zip((o0, o1, o2), DILATIONS)]
        lses = [_get_tokens(slab, l, d, GROUP_W, 0, GROUP_W) for l, d in zip((l0, l1, l2), DILATIONS)]
        m = jnp.maximum(jnp.maximum(lses[0], lses[1]), lses[2])
        es = [jnp.exp(l - m) for l in lses]
        tot = es[0] + es[1] + es[2]
        ya = (es[0] * outs[0] + es[1] * outs[1] + es[2] * outs[2]) / tot
        lj = m + jnp.log(tot)
        for ya_ref, lj_ref, d in zip((ya0, ya1, ya2), (lj0, lj1, lj2), DILATIONS):
            _put_residue(slab, ya, ya_ref, d, GROUP_W, 0)
            _put_residue(slab, lj, lj_ref, d, GROUP_W, 0)
        zhat, _ = _layernorm_stats(_gelu(zp_ref[...].astype(F32)))
        zln = (zhat * lg_ref[...] + lb_ref[...]).astype(BF16)
        u = _gelu(up_ref[...].astype(F32))
        tril = _tril_mask()
        for g in range(GMLP_GROUPS):
            wm = jnp.where(tril, wsp_ref[g], 0.0).astype(BF16)
            cols = slice(g * CHUNK, (g + 1) * CHUNK)
            for c in range(tm // CHUNK):
                rows = slice(c * CHUNK, (c + 1) * CHUNK)
                sz = _dot(wm, zln[rows, cols]) + bcol_ref[g]
                yg_ref[rows, cols] = (u[rows, cols] * sz).astype(BF16)
        a = _dot(ya.astype(BF16), wba_ref[...])
        bm = _dot(yg_ref[...], wbg_ref[...])
        merged = (jax.nn.sigmoid(gap_ref[...].astype(F32)) * a + jax.nn.sigmoid(gbp_ref[...].astype(F32)) * bm).astype(BF16)
        mg_ref[...] = merged
        yv = _dot(merged, wout_ref[...])
        y_ref[...] = yv.astype(BF16)
        x1_ref[...] = x_ref[...] + (yv * _rsqrt_ms(yv)) * g1_ref[...]

    tok = lambda w: pl.BlockSpec((tm, w), lambda i: (i, 0))
    res = lambda d: pl.BlockSpec((tm // d, d * GROUP_W), lambda i: (i, 0))
    full = lambda *s: pl.BlockSpec(s, lambda i: (0,) * len(s))
    res_specs = [res(d) for d in DILATIONS for _ in range(2)]
    return _call(
        body, name="mix_fwd", grid=(seq // tm,),
        in_specs=res_specs + [
            pl.BlockSpec((tm, GMLP_W), lambda i: (i, 0)), pl.BlockSpec((tm, GMLP_W), lambda i: (i, 1)),
            pl.BlockSpec((tm, D_MODEL), lambda i: (i, 1)), pl.BlockSpec((tm, D_MODEL), lambda i: (i, 2)),
            tok(D_MODEL), full(GMLP_GROUPS, CHUNK, CHUNK), full(GMLP_GROUPS, CHUNK, 1), full(1, GMLP_W), full(1, GMLP_W),
            full(GROUP_W, D_MODEL), full(GMLP_W, D_MODEL), full(D_MODEL, D_MODEL), full(1, D_MODEL)],
        out_specs=res_specs + [tok(GMLP_W), tok(D_MODEL), tok(D_MODEL), tok(D_MODEL)],
        out_shape=[jax.ShapeDtypeStruct((seq // d, d * GROUP_W), F32) for d in DILATIONS for _ in range(2)]
        + [jax.ShapeDtypeStruct((seq, GMLP_W), BF16), jax.ShapeDtypeStruct((seq, D_MODEL), BF16),
           jax.ShapeDtypeStruct((seq, D_MODEL), BF16), jax.ShapeDtypeStruct((seq, D_MODEL), F32)],
        scratch_shapes=[pltpu.VMEM((GROUP_W // LANES, tm, LANES), F32)],
        params=_params(("arbitrary",), 48),
        args=(*o_l, rest, rest, rest, rest, x, w_sp, b_col, ln_g, ln_b, w_ba, w_bg, w_out, g1), rider=rider)


def _mlp_fwd(x1, g2, g3, w_mi, w_mo, target):
    seq = x1.shape[0]
    tm, tf = MLP_FWD_TM, 512

    def body(x1_ref, g2_ref, g3_ref, wmi_ref, wmo_ref, t_ref, h2_ref, a_ref, dy2_ref, dout_ref, loss_ref, dg3_ref, sq_s):
        @pl.when(pl.program_id(0) == 0)
        def _():
            loss_ref[...] = jnp.zeros(loss_ref.shape, F32)
            dg3_ref[...] = jnp.zeros(dg3_ref.shape, F32)

        xv = x1_ref[...]
        hb = ((xv * _rsqrt_ms(xv)) * g2_ref[...]).astype(BF16)
        h2_ref[...] = hb
        for j in range(D_FF // tf):
            cols = slice(j * tf, (j + 1) * tf)
            a = jnp.maximum(_dot(hb, wmi_ref[:, cols]), 0.0)
            a_ref[:, cols] = a.astype(BF16)
            sq_s[:, cols] = (a * a).astype(BF16)
        y2 = _dot(sq_s[...], wmo_ref[...])
        r3 = _rsqrt_ms(y2)
        out = xv + (y2 * r3) * g3_ref[...]
        diff = out - t_ref[...]
        tile_loss = 0.5 * jnp.sum(jnp.mean(diff * diff, axis=-1, keepdims=True), axis=0, keepdims=True)
        loss_ref[...] += jnp.broadcast_to(tile_loss, loss_ref.shape)
        dout = diff * (1.0 / D_MODEL)
        dout_ref[...] = dout
        dy2, dg3 = _rmsnorm_bwd(dout, y2, g3_ref[...])
        dy2_ref[...] = dy2.astype(BF16)
        dg3_ref[...] += dg3

    tok = lambda w: pl.BlockSpec((tm, w), lambda i: (i, 0))
    vec = pl.BlockSpec((1, D_MODEL), lambda i: (0, 0))
    return _pallas(
        body, name="mlp_fwd", grid=(seq // tm,),
        in_specs=[tok(D_MODEL), vec, vec, _resident((D_MODEL, D_FF)), _resident((D_FF, D_MODEL)), tok(D_MODEL)],
        out_specs=[tok(D_MODEL), tok(D_FF), tok(D_MODEL), tok(D_MODEL), pl.BlockSpec((8, 128), lambda i: (0, 0)), vec],
        out_shape=[jax.ShapeDtypeStruct((seq, D_MODEL), BF16), jax.ShapeDtypeStruct((seq, D_FF), BF16),
                   jax.ShapeDtypeStruct((seq, D_MODEL), BF16), jax.ShapeDtypeStruct((seq, D_MODEL), F32),
                   jax.ShapeDtypeStruct((8, 128), F32), jax.ShapeDtypeStruct((1, D_MODEL), F32)],
        scratch_shapes=[pltpu.VMEM((tm, D_FF), BF16)],
        compiler_params=_params(("arbitrary",), 56),
    )(*map(_in_hbm, (x1, g2, g3, w_mi, w_mo, target)))


def _mlp_bwd(dy2, a, w_mo, w_mi, dout, x1, y, g2, g1, rider=None):
    seq = x1.shape[0]
    tm, tf = MLP_TM, 512

    def body(dy2_ref, a_ref, wmo_ref, wmi_ref, dout_ref, x1_ref, y_ref, g2_ref, g1_ref,
             dap_ref, dx1_ref, dy_ref, dg2_ref, dg1_ref):
        @pl.when(pl.program_id(0) == 0)
        def _():
            dg2_ref[...] = jnp.zeros(dg2_ref.shape, F32)
            dg1_ref[...] = jnp.zeros(dg1_ref.shape, F32)

        dy2v = dy2_ref[...]
        for j in range(D_FF // tf):
            cols = slice(j * tf, (j + 1) * tf)
            da2 = _dot_nt(dy2v, wmo_ref[cols, :])
            dap_ref[:, cols] = (da2 * (2.0 * a_ref[:, cols].astype(F32))).astype(BF16)
        dh2 = _dot_nt(dap_ref[...], wmi_ref[...])
        dres, dg2 = _rmsnorm_bwd(dh2, x1_ref[...], g2_ref[...])
        dx1 = dout_ref[...] + dres
        dx1_ref[...] = dx1
        dg2_ref[...] += dg2
        dyv, dg1 = _rmsnorm_bwd(dx1, y_ref[...].astype(F32), g1_ref[...])
        dy_ref[...] = dyv.astype(BF16)
        dg1_ref[...] += dg1

    tok = lambda w: pl.BlockSpec((tm, w), lambda i: (i, 0))
    vec = pl.BlockSpec((1, D_MODEL), lambda i: (0, 0))
    return _call(
        body, name="mlp_bwd", grid=(seq // tm,),
        in_specs=[tok(D_MODEL), tok(D_FF), _resident((D_FF, D_MODEL)), _resident((D_MODEL, D_FF)),
                  tok(D_MODEL), tok(D_MODEL), tok(D_MODEL), vec, vec],
        out_specs=[tok(D_FF), tok(D_MODEL), tok(D_MODEL), vec, vec],
        out_shape=[jax.ShapeDtypeStruct((seq, D_FF), BF16), jax.ShapeDtypeStruct((seq, D_MODEL), F32),
                   jax.ShapeDtypeStruct((seq, D_MODEL), BF16), jax.ShapeDtypeStruct((1, D_MODEL), F32),
                   jax.ShapeDtypeStruct((1, D_MODEL), F32)], scratch_shapes=[],
        params=_params(("arbitrary",), 56), args=(dy2, a, w_mo, w_mi, dout, x1, y, g2, g1), rider=rider)


def _tn_matmul(a, b, name, bm, bn, square_a=False, column_shards=False, rider=None, rows_in=None):
    seq, m = a.shape
    n = b.shape[1]
    ts = 2048

    def body(a_ref, b_ref, o_ref):
        @pl.when(pl.program_id(2) == 0)
        def _():
            o_ref[...] = jnp.zeros(o_ref.shape, F32)

        av = a_ref[...]
        if square_a:
            af = av.astype(F32)
            av = (af * af).astype(BF16)
        o_ref[...] += _dot_tn(av, b_ref[...])

    if rows_in is not None:
        first, total = rows_in
        out_spec = pl.BlockSpec((bm, bn), lambda mi, ni, s: (mi + first // bm, ni))
        out_shape = jax.ShapeDtypeStruct((total, n), F32)
    elif column_shards:
        out_spec = pl.BlockSpec((None, bm, bn), lambda mi, ni, s: (ni, mi, 0))
        out_shape = jax.ShapeDtypeStruct((n // bn, m, bn), F32)
    else:
        out_spec = pl.BlockSpec((bm, bn), lambda mi, ni, s: (mi, ni))
        out_shape = jax.ShapeDtypeStruct((m, n), F32)
    (out,), riding = _call(
        body, name=name, grid=(m // bm, n // bn, seq // ts),
        in_specs=[pl.BlockSpec((ts, bm), lambda mi, ni, s: (s, mi)), pl.BlockSpec((ts, bn), lambda mi, ni, s: (s, ni))],
        out_specs=[out_spec], out_shape=[out_shape], scratch_shapes=[],
        params=_params(("arbitrary", "arbitrary", "arbitrary"), 40), args=(a, b), rider=rider)
    return out, riding


def _tn_matmul_residue(a, b, dil, group, into, name):
    length = a.shape[0]
    n = b.shape[1] // dil
    ts = min(1024, length)

    def body(a_ref, b_ref, into_ref, o_ref):
        @pl.when((pl.program_id(1) == 0) & (pl.program_id(2) == 0))
        def _():
            o_ref[...] = jnp.zeros(o_ref.shape, F32)

        o_ref[...] += _dot_tn(a_ref[...], b_ref[...])

    return _pallas(
        body, name=name, grid=(3, dil, length // ts),
        in_specs=[pl.BlockSpec((ts, GROUP_W), lambda sec, r, s: (s, r * 3 + sec)),
                  pl.BlockSpec((ts, n), lambda sec, r, s: (s, r)), HBM_SPEC],
        out_specs=pl.BlockSpec((GROUP_W, n), lambda sec, r, s: (sec * N_GROUPS + group, 0)),
        out_shape=jax.ShapeDtypeStruct(into.shape, F32), input_output_aliases={2: 0},
        compiler_params=_params(("arbitrary", "arbitrary", "arbitrary"), 40),
    )(_in_hbm(a), _in_hbm(b), into)


def _mix_bwd(dy, ya, yg, mg, rest, w_out, w_ba, w_bg, w_sp, b_col, ln_g, ln_b, rider=None):
    seq = dy.shape[0]
    tm = 256

    def body(dy_ref, ya_ref, yg_ref, mg_ref, up_ref, zp_ref, gap_ref, gbp_ref, wout_ref, wba_ref, wbg_ref,
             wsp_ref, bcol_ref, lg_ref, lb_ref,
             dya0, dya1, dya2, dpr_ref, dwout_ref, dwba_ref, dwbg_ref, dwsp_ref, dbb_ref, dlg_ref, dlb_ref,
             dzln_s, du_s, slab):
        @pl.when(pl.program_id(0) == 0)
        def _():
            for ref in (dwout_ref, dwba_ref, dwbg_ref, dwsp_ref, dbb_ref, dlg_ref, dlb_ref):
                ref[...] = jnp.zeros(ref.shape, F32)

        dyv = dy_ref[...]
        dm = _dot_nt(dyv, wout_ref[...])
        dwout_ref[...] += _dot_tn(mg_ref[...], dyv)
        yab = ya_ref[...].astype(BF16)
        ygb = yg_ref[...]
        a = _dot(yab, wba_ref[...])
        bm = _dot(ygb, wbg_ref[...])
        ga = jax.nn.sigmoid(gap_ref[...].astype(F32))
        gb = jax.nn.sigmoid(gbp_ref[...].astype(F32))
        dpr_ref[:, 2 * GMLP_W:2 * GMLP_W + D_MODEL] = (dm * a * (ga * (1.0 - ga))).astype(BF16)
        dpr_ref[:, 2 * GMLP_W + D_MODEL:REST_W] = (dm * bm * (gb * (1.0 - gb))).astype(BF16)
        da = (dm * ga).astype(BF16)
        db = (dm * gb).astype(BF16)
        dwba = _dot_tn(yab, da)
        dwbg = _dot_tn(ygb, db)
        shard_w = D_MODEL // N_CHIPS
        for j in range(N_CHIPS):
            dwba_ref[j] += dwba[:, j * shard_w:(j + 1) * shard_w]
            dwbg_ref[j] += dwbg[:, j * shard_w:(j + 1) * shard_w]
        dya = _dot_nt(da, wba_ref[...])
        for dya_ref, d in zip((dya0, dya1, dya2), DILATIONS):
            _put_residue(slab, dya, dya_ref, d, GROUP_W, 0)
        dyg = _dot_nt(db, wbg_ref[...])

        zp = zp_ref[...].astype(F32)
        zhat, rstd = _layernorm_stats(_gelu(zp))
        lg = lg_ref[...]
        zln = (zhat * lg + lb_ref[...]).astype(BF16)
        up = up_ref[...].astype(F32)
        u = _gelu(up)
        tril = _tril_mask()
        for g in range(GMLP_GROUPS):
            wm = jnp.where(tril, wsp_ref[g], 0.0).astype(BF16)
            cols = slice(g * CHUNK, (g + 1) * CHUNK)
            for c in range(tm // CHUNK):
                rows = slice(c * CHUNK, (c + 1) * CHUNK)
                zb = zln[rows, cols]
                sz = _dot(wm, zb) + bcol_ref[g]
                dyg_cg = dyg[rows, cols]
                du_s[rows, cols] = dyg_cg * sz
                dsz = dyg_cg * u[rows, cols]
                dszb = dsz.astype(BF16)
                dbb_ref[g] += jnp.broadcast_to(jnp.sum(dsz, axis=-1, keepdims=True), (CHUNK, CHUNK))
                dwsp_ref[g] += jnp.where(tril, _dot_nt(dszb, zb), 0.0)
                dzln_s[rows, cols] = _dot_tn(wm, dszb)
        dzln = dzln_s[...]
        dlg_ref[...] += jnp.sum(dzln * zhat, axis=0, keepdims=True)
        dlb_ref[...] += jnp.sum(dzln, axis=0, keepdims=True)
        dzh = dzln * lg
        dz = rstd * (dzh - jnp.mean(dzh, axis=-1, keepdims=True) - zhat * jnp.mean(dzh * zhat, axis=-1, keepdims=True))
        dpr_ref[:, GMLP_W:2 * GMLP_W] = (dz * _gelu_grad(zp)).astype(BF16)
        dpr_ref[:, 0:GMLP_W] = (du_s[...] * _gelu_grad(up)).astype(BF16)

    tok = lambda w: pl.BlockSpec((tm, w), lambda i: (i, 0))
    full = lambda *s: pl.BlockSpec(s, lambda i: (0,) * len(s))
    return _call(
        body, name="mix_bwd", grid=(seq // tm,),
        in_specs=[tok(D_MODEL), tok(GROUP_W), tok(GMLP_W), tok(D_MODEL),
                  pl.BlockSpec((tm, GMLP_W), lambda i: (i, 0)), pl.BlockSpec((tm, GMLP_W), lambda i: (i, 1)),
                  pl.BlockSpec((tm, D_MODEL), lambda i: (i, 1)), pl.BlockSpec((tm, D_MODEL), lambda i: (i, 2)),
                  full(D_MODEL, D_MODEL), full(GROUP_W, D_MODEL), full(GMLP_W, D_MODEL),
                  full(GMLP_GROUPS, CHUNK, CHUNK), full(GMLP_GROUPS, CHUNK, 1), full(1, GMLP_W), full(1, GMLP_W)],
        out_specs=[pl.BlockSpec((tm // d, d * GROUP_W), lambda i: (i, 0)) for d in DILATIONS]
        + [tok(REST_W), full(D_MODEL, D_MODEL), full(N_CHIPS, GROUP_W, D_MODEL // N_CHIPS),
           full(N_CHIPS, GMLP_W, D_MODEL // N_CHIPS),
           full(GMLP_GROUPS, CHUNK, CHUNK), full(GMLP_GROUPS, CHUNK, CHUNK), full(1, GMLP_W), full(1, GMLP_W)],
        out_shape=[jax.ShapeDtypeStruct((seq // d, d * GROUP_W), F32) for d in DILATIONS]
        + [jax.ShapeDtypeStruct((seq, REST_W), BF16),
           jax.ShapeDtypeStruct((D_MODEL, D_MODEL), F32), jax.ShapeDtypeStruct((N_CHIPS, GROUP_W, D_MODEL // N_CHIPS), F32),
           jax.ShapeDtypeStruct((N_CHIPS, GMLP_W, D_MODEL // N_CHIPS), F32),
           jax.ShapeDtypeStruct((GMLP_GROUPS, CHUNK, CHUNK), F32),
           jax.ShapeDtypeStruct((GMLP_GROUPS, CHUNK, CHUNK), F32), jax.ShapeDtypeStruct((1, GMLP_W), F32),
           jax.ShapeDtypeStruct((1, GMLP_W), F32)],
        scratch_shapes=[pltpu.VMEM((tm, GMLP_W), F32), pltpu.VMEM((tm, GMLP_W), F32),
                        pltpu.VMEM((GROUP_W // LANES, tm, LANES), F32)],
        params=_params(("arbitrary",), 56),
        args=(dy, ya, yg, mg, rest, rest, rest, rest, w_out, w_ba, w_bg, w_sp, b_col, ln_g, ln_b), rider=rider)


IN_PROJ_BWD_TM = 512


def _in_proj_bwd(dqkv, drest, w_in, x, dx1, g0, so_far, span, rider=None):
    seq = x.shape[0]
    tm = IN_PROJ_BWD_TM
    off, steps = span
    gx_so_far, dg_so_far = so_far

    def body(d0, d1, d2, dr_ref, w_ref, x_ref, dx1_ref, g_ref, dg_in_ref, gx_in_ref, gx_ref, dg_ref, slab):
        @pl.when(pl.program_id(0) == 0)
        def _():
            dg_ref[...] = dg_in_ref[...]

        dh = _dot(dr_ref[...], w_ref[QKV_W:, :])
        for g, (d_ref, dil) in enumerate(zip((d0, d1, d2), DILATIONS)):
            piece = d_ref[...] if dil == 1 else _get_tokens(slab, d_ref, dil, 3 * GROUP_W, 0, 3 * GROUP_W).astype(BF16)
            for section, (lo, hi) in enumerate(_qkv_columns(g)):
                dh = dh + _dot(piece[:, section * GROUP_W:(section + 1) * GROUP_W], w_ref[lo:hi, :])
        dres, dg = _rmsnorm_bwd(dh, x_ref[...], g_ref[...])
        gx_ref[...] = dx1_ref[...] + dres
        dg_ref[...] += dg

    tok = lambda w: pl.BlockSpec((tm, w), lambda i: (i + off, 0))
    full = lambda *s: pl.BlockSpec(s, lambda i: (0,) * len(s))
    in_specs = ([pl.BlockSpec((tm // d, d * 3 * GROUP_W), lambda i: (i + off, 0)) for d in DILATIONS] + [tok(REST_W)]
                + [_resident((IN_W, D_MODEL))]
                + [tok(D_MODEL), tok(D_MODEL), full(1, D_MODEL), full(1, D_MODEL), HBM_SPEC])
    return _call(
        body, name=f"in_proj_bwd_{off}", grid=(steps,), in_specs=in_specs,
        out_specs=[tok(D_MODEL), full(1, D_MODEL)],
        out_shape=[jax.ShapeDtypeStruct((seq, D_MODEL), F32), jax.ShapeDtypeStruct((1, D_MODEL), F32)],
        scratch_shapes=[pltpu.VMEM((3 * GROUP_W // LANES, tm, LANES), F32)],
        params=_params(("arbitrary",), 48), args=(*dqkv, drest, w_in, x, dx1, g0, dg_so_far, gx_so_far),
        rider=rider, aliases={len(in_specs) - 1: 0})


def _adamw(w, g, m, v, name):
    rows, cols = w.shape
    tr = _row_tile(rows) if rows % 16 == 0 else rows
    c1 = 1.0 - ADAM_B1 ** ADAM_STEP
    c2 = 1.0 - ADAM_B2 ** ADAM_STEP

    def body(w_ref, g_ref, m_ref, v_ref, go_ref, d_ref, nm_ref, nv_ref):
        gv = g_ref[...]
        go_ref[...] = gv
        nm = ADAM_B1 * m_ref[...] + (1.0 - ADAM_B1) * gv
        nv = ADAM_B2 * v_ref[...] + (1.0 - ADAM_B2) * (gv * gv)
        d_ref[...] = -ADAM_LR * ((nm / c1) / (jnp.sqrt(nv / c2) + ADAM_EPS) + ADAM_WD * w_ref[...])
        nm_ref[...] = nm
        nv_ref[...] = nv

    spec = pl.BlockSpec((tr, cols), lambda i: (i, 0))
    return _pallas(
        body, name=name, grid=(rows // tr,),
        in_specs=[spec] * 4, out_specs=[spec] * 4,
        out_shape=[jax.ShapeDtypeStruct((rows, cols), F32)] * 4,
        compiler_params=_params(("arbitrary",), 32, small=True),
    )(w, g, m, v)


def _place():
    x, y, c = lax.axis_index("x"), lax.axis_index("y"), lax.axis_index("c")
    chips = [(1 - x, y), (x, 1 - y), (1 - x, 1 - y)]
    return x, y, c, chips


class _Exchange:
    def __init__(self, inputs, out_shapes, n_sems, start, finish, aliases=None):
        self.inputs, self.out_shapes, self.n_sems = list(inputs), list(out_shapes), n_sems
        self.start, self.finish, self.aliases = start, finish, dict(aliases or {})

    def scratch(self):
        return [pltpu.SemaphoreType.DMA((self.n_sems,)), pltpu.SemaphoreType.DMA((self.n_sems,))]


def _together(*parts):
    ins = [len(p.inputs) for p in parts]
    outs = [len(p.out_shapes) for p in parts]

    def split(refs, counts):
        pos, pieces = 0, []
        for cnt in counts:
            pieces.append(refs[pos:pos + cnt])
            pos += cnt
        return pieces

    def run(which):
        def go(in_refs, out_refs, *sems):
            for k, (p, i, o) in enumerate(zip(parts, split(in_refs, ins), split(out_refs, outs))):
                getattr(p, which)(i, o, sems[2 * k], sems[2 * k + 1])
        return go

    both = _Exchange([a for p in parts for a in p.inputs], [s for p in parts for s in p.out_shapes], 0, run("start"),
                     run("finish"))
    both.aliases = {sum(ins[:k]) + i: sum(outs[:k]) + o for k, p in enumerate(parts) for i, o in p.aliases.items()}
    both.scratch = lambda: [s for p in parts for s in p.scratch()]
    return both


def _run_exchange(ex, name):
    n_in, n_out = len(ex.inputs), len(ex.out_shapes)

    def body(*refs):
        ins, outs, sems = refs[:n_in], refs[n_in:n_in + n_out], refs[n_in + n_out:]
        ex.start(ins, outs, *sems)
        ex.finish(ins, outs, *sems)

    return _pallas(
        body, name=name, in_specs=[HBM_SPEC] * n_in, out_specs=[HBM_SPEC] * n_out, out_shape=ex.out_shapes,
        scratch_shapes=ex.scratch(), input_output_aliases=ex.aliases,
    )(*ex.inputs)


def _call(body, *, name, grid, in_specs, out_specs, out_shape, scratch_shapes, params, args, rider=None, aliases=None):
    in_specs, out_specs, out_shape, scratch_shapes = list(in_specs), list(out_specs), list(out_shape), list(scratch_shapes)
    aliases = dict(aliases or {})
    args = [_in_hbm(a) for a in args]
    if rider is None:
        outs = _pallas(body, name=name, grid=grid, in_specs=in_specs, out_specs=out_specs, out_shape=out_shape,
                              scratch_shapes=scratch_shapes, input_output_aliases=aliases, compiler_params=params)(*args)
        return list(outs), []
    n_in, n_out, n_scr = len(in_specs), len(out_specs), len(scratch_shapes)
    r_in, r_out = len(rider.inputs), len(rider.out_shapes)

    def wrapped(*refs):
        ins, r_ins = refs[:n_in], refs[n_in:n_in + r_in]
        pos = n_in + r_in
        outs, r_outs = refs[pos:pos + n_out], refs[pos + n_out:pos + n_out + r_out]
        pos += n_out + r_out
        scr, sems = refs[pos:pos + n_scr], refs[pos + n_scr:]
        ids = [pl.program_id(k) for k in range(len(grid))]
        first, last = ids[0] == 0, ids[0] == grid[0] - 1
        for k in range(1, len(grid)):
            first, last = first & (ids[k] == 0), last & (ids[k] == grid[k] - 1)

        @pl.when(first)
        def _():
            rider.start(r_ins, r_outs, *sems)

        body(*ins, *outs, *scr)

        @pl.when(last)
        def _():
            rider.finish(r_ins, r_outs, *sems)

    outs = _pallas(
        wrapped, name=name, grid=grid, in_specs=in_specs + [HBM_SPEC] * r_in, out_specs=out_specs + [HBM_SPEC] * r_out,
        out_shape=out_shape + rider.out_shapes, scratch_shapes=scratch_shapes + rider.scratch(),
        input_output_aliases={**aliases, **{n_in + i: n_out + o for i, o in rider.aliases.items()}}, compiler_params=params,
    )(*args, *rider.inputs)
    return list(outs[:n_out]), list(outs[n_out:])


def _stage_weights(shards):
    n = len(shards)

    def body(*refs):
        ins, outs, stages, sems = refs[:n], refs[n:2 * n], refs[2 * n:3 * n], refs[3 * n]
        x, y, _, _ = _place()
        copies = []
        for t in range(n):
            stages[t][...] = ins[t][...].astype(BF16)
            copies.append(pltpu.make_async_copy(stages[t], outs[t].at[2 * x + y], sems.at[t]))
            copies[-1].start()
        for cp in copies:
            cp.wait()

    assert sum(s.size * 6 for s in shards) <= (CALL_VMEM_MIB - 8) * MIB
    return _pallas(
        body, name="stage_weights", in_specs=[VMEM_SPEC] * n, out_specs=[HBM_SPEC] * n,
        out_shape=[jax.ShapeDtypeStruct((N_CHIPS,) + s.shape, BF16) for s in shards],
        scratch_shapes=[pltpu.VMEM(s.shape, BF16) for s in shards] + [pltpu.SemaphoreType.DMA((n,))],
        compiler_params=pltpu.CompilerParams(vmem_limit_bytes=SMALL_VMEM_MIB * MIB),
    )(*shards)


def _gather(buffers, stage="both", part=(0, 1)):
    n = len(buffers)
    halves = [b.shape[1] // part[1] // 2 for b in buffers]

    def half_of(outs, t, chip, which):
        return outs[t].at[chip, pl.ds((2 * part[0] + which) * halves[t], halves[t]), :]

    def copy(outs, sems, t, k, chip, which, to):
        rows = half_of(outs, t, chip, which)
        return pltpu.make_async_remote_copy(src_ref=rows, dst_ref=rows, send_sem=sems[0].at[6 * t + k],
                                            recv_sem=sems[1].at[6 * t + k], device_id=to, device_id_type=MESH)

    def to_chips(outs, sems, what):
        x, y, c, chips = _place()
        for t in range(n):
            for j, (px, py) in enumerate(chips):
                if what == "start":
                    copy(outs, sems, t, j, 2 * x + y, c, (px, py, c)).start()
                else:
                    copy(outs, sems, t, j, 2 * px + py, c, (px, py, c)).wait_recv()
                    copy(outs, sems, t, j, 2 * x + y, c, (px, py, c)).wait_send()

    def to_sibling(outs, sems, what):
        x, y, c, chips = _place()
        for t in range(n):
            for j, (px, py) in enumerate(chips):
                if what == "start":
                    copy(outs, sems, t, 3 + j, 2 * px + py, c, (x, y, 1 - c)).start()
                else:
                    copy(outs, sems, t, 3 + j, 2 * px + py, 1 - c, (x, y, 1 - c)).wait_recv()
                    copy(outs, sems, t, 3 + j, 2 * px + py, c, (x, y, 1 - c)).wait_send()

    def start(ins, outs, *sems):
        (to_sibling if stage == "pair" else to_chips)(outs, sems, "start")

    def finish(ins, outs, *sems):
        if stage == "both":
            x, y, c, chips = _place()
            for j, (px, py) in enumerate(chips):
                for t in range(n):
                    copy(outs, sems, t, j, 2 * px + py, c, (px, py, c)).wait_recv()
                    copy(outs, sems, t, 3 + j, 2 * px + py, c, (x, y, 1 - c)).start()
            for j, (px, py) in enumerate(chips):
                for t in range(n):
                    copy(outs, sems, t, j, 2 * x + y, c, (px, py, c)).wait_send()
            to_sibling(outs, sems, "finish")
        elif stage == "chips":
            to_chips(outs, sems, "finish")
        else:
            to_sibling(outs, sems, "finish")

    return _Exchange(buffers, [jax.ShapeDtypeStruct(b.shape, b.dtype) for b in buffers], 6 * n, start, finish,
                     aliases={t: t for t in range(n)})


def _pair_exchange(grads):
    n = len(grads)
    halves = [g.shape[1] // 2 for g in grads]

    def copies(ins, outs, send_sems, recv_sems):
        x, y, c, _ = _place()
        return [pltpu.make_async_remote_copy(
            src_ref=ins[t].at[:, pl.ds((1 - c) * halves[t], halves[t]), :], dst_ref=outs[t],
            send_sem=send_sems.at[t], recv_sem=recv_sems.at[t], device_id=(x, y, 1 - c), device_id_type=MESH)
            for t in range(n)]

    def start(*refs):
        for cp in copies(*refs):
            cp.start()

    def finish(*refs):
        for cp in copies(*refs):
            cp.wait()

    return _Exchange(grads, [jax.ShapeDtypeStruct((N_CHIPS, h, g.shape[2]), F32) for g, h in zip(grads, halves)], n,
                     start, finish)


def _row_tile(rows):
    return max(t for t in range(16, 257, 16) if rows % t == 0)


def _pair_add(grad, other, place, name):
    _, rows, cols = grad.shape
    rh = rows // 2
    tr = _row_tile(rh)
    nb = rh // tr

    def body(p_ref, g_ref, a_ref, wire_ref, own_ref):
        s = g_ref[...] + a_ref[...]
        wire_ref[...] = s.astype(BF16)

        @pl.when(pl.program_id(1) == p_ref[1])
        def _():
            own_ref[...] = s

    blk = (None, tr, cols)
    return _pallas(
        body, name=name,
        grid_spec=pltpu.PrefetchScalarGridSpec(
            num_scalar_prefetch=1, grid=(nb, N_CHIPS),
            in_specs=[pl.BlockSpec(blk, lambda i, j, p: (j, p[0] * nb + i, 0)), pl.BlockSpec(blk, lambda i, j, p: (j, i, 0))],
            out_specs=[pl.BlockSpec(blk, lambda i, j, p: (j, i, 0)), pl.BlockSpec((tr, cols), lambda i, j, p: (i, 0))]),
        out_shape=[jax.ShapeDtypeStruct((N_CHIPS, rh, cols), BF16), jax.ShapeDtypeStruct((rh, cols), F32)],
        compiler_params=_params(("arbitrary", "arbitrary"), 32, small=True),
    )(place, grad, other)


def _chip_exchange(wires):
    n = len(wires)

    def copies(ins, outs, send_sems, recv_sems):
        x, y, c, chips = _place()
        return [pltpu.make_async_remote_copy(
            src_ref=ins[t].at[2 * px + py], dst_ref=outs[t].at[j], send_sem=send_sems.at[3 * t + j],
            recv_sem=recv_sems.at[3 * t + j], device_id=(px, py, c), device_id_type=MESH)
            for t in range(n) for j, (px, py) in enumerate(chips)]

    def start(*refs):
        for cp in copies(*refs):
            cp.start()

    def finish(*refs):
        for cp in copies(*refs):
            cp.wait()

    return _Exchange(wires, [jax.ShapeDtypeStruct((3,) + w.shape[1:], BF16) for w in wires], 3 * n, start, finish)


def _chip_add(own, arrived, place, name):
    rh, cols = own.shape
    tr = _row_tile(rh)
    nb = rh // tr

    def body(p_ref, s_ref, b0, b1, b2, o_ref):
        o_ref[...] = ((s_ref[...] + b0[...].astype(F32)) + b1[...].astype(F32)) + b2[...].astype(F32)

    blk = (None, tr, cols)
    return _pallas(
        body, name=name,
        grid_spec=pltpu.PrefetchScalarGridSpec(
            num_scalar_prefetch=1, grid=(nb,),
            in_specs=[pl.BlockSpec((tr, cols), lambda i, p: (i, 0)), pl.BlockSpec(blk, lambda i, p: (0, i, 0)),
                      pl.BlockSpec(blk, lambda i, p: (1, i, 0)), pl.BlockSpec(blk, lambda i, p: (2, i, 0))],
            out_specs=pl.BlockSpec((tr, cols), lambda i, p: (p[0] * nb + i, 0))),
        out_shape=jax.ShapeDtypeStruct((2 * rh, cols), F32),
        compiler_params=_params(("arbitrary",), 32, small=True),
    )(place, own, arrived, arrived, arrived)


def _pair_share(halves):
    n = len(halves)
    rhs = [h.shape[0] // 2 for h in halves]

    def copy(outs, send_sems, recv_sems, t, which):
        x, y, c, _ = _place()
        rows = outs[t].at[pl.ds(which * rhs[t], rhs[t]), :]
        return pltpu.make_async_remote_copy(src_ref=rows, dst_ref=rows, send_sem=send_sems.at[t], recv_sem=recv_sems.at[t],
                                            device_id=(x, y, 1 - c), device_id_type=MESH)

    def start(ins, outs, send_sems, recv_sems):
        c = lax.axis_index("c")
        for t in range(n):
            copy(outs, send_sems, recv_sems, t, c).start()

    def finish(ins, outs, send_sems, recv_sems):
        c = lax.axis_index("c")
        for t in range(n):
            copy(outs, send_sems, recv_sems, t, c).wait_send()
            copy(outs, send_sems, recv_sems, t, 1 - c).wait_recv()

    return _Exchange(halves, [jax.ShapeDtypeStruct(h.shape, F32) for h in halves], n, start, finish,
                     aliases={t: t for t in range(n)})


class _GradReduction:
    def __init__(self, grads, place, tag):
        self.names, self.grads, self.place, self.tag = list(grads), grads, place, tag

    def pair_exchange(self):
        return _pair_exchange([self.grads[n] for n in self.names])

    def chip_exchange(self, others):
        sums = [_pair_add(self.grads[n], o, self.place, f"{self.tag}_pair_add_{n}") for n, o in zip(self.names, others)]
        self.owns = [own for _, own in sums]
        return _chip_exchange([wire for wire, _ in sums])

    def pair_share(self, arrived):
        return _pair_share([_chip_add(own, arr, self.place, f"{self.tag}_chip_add_{n}")
                            for n, own, arr in zip(self.names, self.owns, arrived)])

    def result(self, shared):
        return dict(zip(self.names, shared))


def _all_reduce_small(p):
    rows, lanes = p.shape
    half = rows // 2

    def body(p_ref, o_ref, sib, sums, send_sems, recv_sems):
        x, y, c, chips = _place()
        mine, sibling = 2 * x + y, (x, y, 1 - c)
        swap = pltpu.make_async_remote_copy(src_ref=p_ref, dst_ref=sib, send_sem=send_sems.at[0], recv_sem=recv_sems.at[0],
                                            device_id=sibling, device_id_type=MESH)
        swap.start()
        swap.wait()
        sums[mine] = p_ref[...] + sib[...]

        def copy(k, chip, which, to):
            part = sums.at[chip, pl.ds(which * half, half), :]
            return pltpu.make_async_remote_copy(src_ref=part, dst_ref=part, send_sem=send_sems.at[k], recv_sem=recv_sems.at[k],
                                                device_id=to, device_id_type=MESH)

        for j, (px, py) in enumerate(chips):
            copy(1 + j, mine, c, (px, py, c)).start()
        for j, (px, py) in enumerate(chips):
            copy(1 + j, 2 * px + py, c, (px, py, c)).wait_recv()
            copy(4 + j, 2 * px + py, c, sibling).start()
        for j, (px, py) in enumerate(chips):
            copy(4 + j, 2 * px + py, 1 - c, sibling).wait_recv()
        for j, (px, py) in enumerate(chips):
            copy(1 + j, mine, c, (px, py, c)).wait_send()
            copy(4 + j, 2 * px + py, c, sibling).wait_send()
        o_ref[...] = ((sums[0] + sums[1]) + sums[2]) + sums[3]

    return _pallas(
        body, name="small_all_reduce", in_specs=[VMEM_SPEC], out_specs=VMEM_SPEC,
        out_shape=jax.ShapeDtypeStruct((rows, lanes), F32),
        scratch_shapes=[pltpu.VMEM((rows, lanes), F32), pltpu.VMEM((N_CHIPS, rows, lanes), F32),
                        pltpu.SemaphoreType.DMA((7,)), pltpu.SemaphoreType.DMA((7,))],
        compiler_params=pltpu.CompilerParams(vmem_limit_bytes=SMALL_VMEM_MIB * MIB),
    )(p)


BIG = ("w_in", "w_branch_attn", "w_branch_gmlp", "w_out", "w_mlp_in", "w_mlp_out")
COLUMN_SHARDED = ("w_branch_attn", "w_branch_gmlp", "w_mlp_in")
SMALL = ("norm_pre_mix", "w_spatial", "b_spatial", "ln_v_gain", "ln_v_bias", "norm_post_mix", "norm_pre_mlp", "norm_post_mlp")
ORDER = ("norm_pre_mix", "w_in", "w_spatial", "b_spatial", "ln_v_gain", "ln_v_bias", "w_branch_attn", "w_branch_gmlp",
         "w_out", "norm_post_mix", "norm_pre_mlp", "w_mlp_in", "w_mlp_out", "norm_post_mlp")


def _full_weight(name, gathered):
    if name in COLUMN_SHARDED:
        return jnp.transpose(gathered, (1, 0, 2)).reshape(gathered.shape[1], -1)
    return gathered.reshape(-1, gathered.shape[2])


def _rows8(a):
    a = a.reshape(-1, 128)
    pad = (-a.shape[0]) % 8
    return jnp.pad(a, ((0, pad), (0, 0))) if pad else a


def _qkv_columns(group):
    return [(sec * ATTN_W + group * GROUP_W, sec * ATTN_W + (group + 1) * GROUP_W) for sec in range(3)]


def _device_step(x, target, small, shards, place):
    seq = x.shape[0]
    g0, g1, g2, g3 = small["norm_pre_mix"], small["norm_post_mix"], small["norm_pre_mlp"], small["norm_post_mlp"]
    w_sp = small["w_spatial"]
    b_col = small["b_spatial"].reshape(GMLP_GROUPS, CHUNK, 1)
    ln_g, ln_b = small["ln_v_gain"], small["ln_v_bias"]

    staged = _stage_weights(shards)
    h, tables, (w_in,) = _prepare(x, g0, rider=_gather(staged[:1]))
    w_in = _full_weight("w_in", w_in)
    (*qkv, rest), landed = _in_proj(h[0], w_in, *tables[1], rider=_gather(staged[1:], "chips"))

    o_l, gathered = _attn_fwd(qkv[0], DILATIONS[0], rider=_gather(landed, "pair"))
    full = {n: _full_weight(n, gw) for n, gw in zip(BIG[1:], gathered)}
    for g in range(1, N_GROUPS):
        o_l.extend(_attn_fwd(qkv[g], DILATIONS[g])[0])
    (*ya_l, yg, mg, y, x1), _ = _mix_fwd(o_l, rest, x, w_sp, b_col, ln_g, ln_b, full["w_branch_attn"],
                                        full["w_branch_gmlp"], full["w_out"], g1)
    ya, lse = ya_l[0::2], ya_l[1::2]
    h2, a, dy2, dout, loss8, dg3 = _mlp_fwd(x1, g2, g3, full["w_mlp_in"], full["w_mlp_out"], target)
    d_wmo, _ = _tn_matmul(a, dy2, "grad_w_mlp_out", 1024, 1024, square_a=True)
    mlp_out = _GradReduction({"w_mlp_out": d_wmo.reshape(N_CHIPS, D_FF // N_CHIPS, D_MODEL)}, place, "mlp_out")
    (dap, dx1, dy, dg2, dg1), riding = _mlp_bwd(dy2, a, full["w_mlp_out"], full["w_mlp_in"], dout, x1, y, g2, g1,
                                                 rider=mlp_out.pair_exchange())
    d_wmi, riding = _tn_matmul(h2, dap, "grad_w_mlp_in", 1024, 1024, column_shards=True,
                               rider=mlp_out.chip_exchange(riding))
    mlp_in = _GradReduction({"w_mlp_in": d_wmi}, place, "mlp_in")
    (*dya, drest, d_wout, d_wba, d_wbg, d_wsp, d_bb, d_lg, d_lb), riding = _mix_bwd(
        dy, ya[0], yg, mg, rest, full["w_out"], full["w_branch_attn"], full["w_branch_gmlp"], w_sp, b_col, ln_g, ln_b,
        rider=_together(mlp_out.pair_share(riding), mlp_in.pair_exchange()))
    reduced = mlp_out.result(riding[:1])
    mix = _GradReduction({"w_branch_attn": d_wba, "w_branch_gmlp": d_wbg,
                          "w_out": d_wout.reshape(N_CHIPS, D_MODEL // N_CHIPS, D_MODEL)}, place, "mix")
    attn = lambda g, rider: _attn_bwd(qkv[g], dya[g], ya[g], lse[g], *tables[DILATIONS[g]], DILATIONS[g], rider=rider)
    dqkv0, riding = attn(0, _together(mlp_in.chip_exchange(riding[1:]), mix.pair_exchange()))
    dqkv1, riding = attn(1, _together(mlp_in.pair_share(riding[:1]), mix.chip_exchange(riding[1:])))
    reduced.update(mlp_in.result(riding[:1]))
    dqkv2, riding = attn(2, mix.pair_share(riding[1:]))
    reduced.update(mix.result(riding))
    dqkv = [dqkv0, dqkv1, dqkv2]

    d_win, _ = _tn_matmul(drest, h[0], "grad_w_in_rest", 3 * GROUP_W, 1024, rows_in=(QKV_W, IN_W))
    for g, dil in enumerate(DILATIONS):
        d_win = _tn_matmul_residue(dqkv[g], h[g], dil, g, d_win, f"grad_w_in_qkv{g}")
    first = _GradReduction({"w_in": d_win.reshape(N_CHIPS, IN_W // N_CHIPS, D_MODEL)}, place, "w_in")
    tiles = seq // IN_PROJ_BWD_TM
    so_far = (lax.empty((seq, D_MODEL), F32), jnp.zeros((1, D_MODEL), F32))
    in_bwd = lambda so_far, span, rider: _in_proj_bwd(dqkv, drest, w_in, x, dx1, g0, so_far, span, rider=rider)
    head = 3 * tiles // 8
    so_far, riding = in_bwd(so_far, (0, head), first.pair_exchange())
    (grad_x, dg0), riding = in_bwd(so_far, (head, tiles - head), first.chip_exchange(riding))
    reduced.update(first.result(_run_exchange(first.pair_share(riding), "w_in_pair_share")))
    little = {"norm_pre_mix": dg0, "w_spatial": d_wsp, "b_spatial": d_bb[:, :, 0], "ln_v_gain": d_lg, "ln_v_bias": d_lb,
              "norm_post_mix": dg1, "norm_pre_mlp": dg2, "norm_post_mlp": dg3}
    return loss8, grad_x, reduced, little


def kernel(x, norm_pre_mix, w_in, w_spatial, b_spatial, ln_v_gain, ln_v_bias, w_branch_attn, w_branch_gmlp, w_out, norm_post_mix, norm_pre_mlp, w_mlp_in, w_mlp_out, norm_post_mlp, loss_target, m_norm_pre_mix, m_w_in, m_w_spatial, m_b_spatial, m_ln_v_gain, m_ln_v_bias, m_w_branch_attn, m_w_branch_gmlp, m_w_out, m_norm_post_mix, m_norm_pre_mlp, m_w_mlp_in, m_w_mlp_out, m_norm_post_mlp, v_norm_pre_mix, v_w_in, v_w_spatial, v_b_spatial, v_ln_v_gain, v_ln_v_bias, v_w_branch_attn, v_w_branch_gmlp, v_w_out, v_norm_post_mix, v_norm_pre_mlp, v_w_mlp_in, v_w_mlp_out, v_norm_post_mlp):
    given = dict(norm_pre_mix=norm_pre_mix, w_in=w_in, w_spatial=w_spatial, b_spatial=b_spatial, ln_v_gain=ln_v_gain,
                 ln_v_bias=ln_v_bias, w_branch_attn=w_branch_attn, w_branch_gmlp=w_branch_gmlp, w_out=w_out,
                 norm_post_mix=norm_post_mix, norm_pre_mlp=norm_pre_mlp, w_mlp_in=w_mlp_in, w_mlp_out=w_mlp_out,
                 norm_post_mlp=norm_post_mlp)
    moments_m = dict(norm_pre_mix=m_norm_pre_mix, w_in=m_w_in, w_spatial=m_w_spatial, b_spatial=m_b_spatial,
                     ln_v_gain=m_ln_v_gain, ln_v_bias=m_ln_v_bias, w_branch_attn=m_w_branch_attn,
                     w_branch_gmlp=m_w_branch_gmlp, w_out=m_w_out, norm_post_mix=m_norm_post_mix,
                     norm_pre_mlp=m_norm_pre_mlp, w_mlp_in=m_w_mlp_in, w_mlp_out=m_w_mlp_out, norm_post_mlp=m_norm_post_mlp)
    moments_v = dict(norm_pre_mix=v_norm_pre_mix, w_in=v_w_in, w_spatial=v_w_spatial, b_spatial=v_b_spatial,
                     ln_v_gain=v_ln_v_gain, ln_v_bias=v_ln_v_bias, w_branch_attn=v_w_branch_attn,
                     w_branch_gmlp=v_w_branch_gmlp, w_out=v_w_out, norm_post_mix=v_norm_post_mix,
                     norm_pre_mlp=v_norm_pre_mlp, w_mlp_in=v_w_mlp_in, w_mlp_out=v_w_mlp_out, norm_post_mlp=v_norm_post_mlp)
    cx, cy, cc = lax.axis_index("x"), lax.axis_index("y"), lax.axis_index("c")

    shards = [given[n][0].T if n == "w_in" else given[n][0] for n in BIG]
    small = {n: given[n][0] if given[n].ndim > 2 else given[n] for n in SMALL}
    place = jnp.stack([cc, 2 * cx + cy]).astype(jnp.int32)
    loss8, grad_x, grad_shard, grads = _device_step(x[0], loss_target[0], small, shards, place)

    packed = jnp.concatenate([_rows8(grads[n]) for n in SMALL] + [loss8], axis=0)
    summed = _all_reduce_small(packed)
    loss = summed[packed.shape[0] - loss8.shape[0], 0]
    row = 0
    for n in SMALL:
        shape = given[n][0].shape
        cnt = -(-(given[n][0].size // 128) // 8) * 8
        grad_shard[n] = summed[row:row + given[n][0].size // 128].reshape(shape)
        row += cnt

    grad_out, deltas, new_m, new_v = {}, {}, {}, {}
    for n in ORDER:
        shape = given[n].shape
        if n == "w_in":
            outs = _adamw(given[n][0].T, grad_shard[n], moments_m[n][0].T, moments_v[n][0].T, "adamw_" + n)
            outs = [o.T for o in outs]
        else:
            two_d = (-1, shape[-1])
            outs = _adamw(given[n].reshape(two_d), grad_shard[n].reshape(two_d), moments_m[n].reshape(two_d),
                          moments_v[n].reshape(two_d), "adamw_" + n)
        grad_out[n], deltas[n], new_m[n], new_v[n] = [o.reshape(shape) for o in outs]
    return (loss, grad_x[None], *[grad_out[n] for n in ORDER], *[deltas[n] for n in ORDER], *[new_m[n] for n in ORDER],
            *[new_v[n] for n in ORDER])
```
